```python
import math
import jax, jax.numpy as jnp
from jax import lax
import numpy as np

D_MODEL = 1024
BATCH = 8
SEQ = 2048
DEPTH = 4

CHUNK = 64
N_META = 16
Q_BLOCK = 128
N_MIXERS = 3
NEG_INF = -1e30

REL_BUCKETS = 32
REL_MAX_DIST = 128
REL_HEADS = 16

D_FF = 2816

A_HEADS = 8
A_HD = 64
A_VD = 2 * A_HD

B_HEADS = 16
B_Q_RANK = 256
B_KV_RANK = 256
B_VD = 64
IDX_HEADS = 8
IDX_DIM = 64
TOPK_MAX = 256

C_Q_HEADS = 16
C_KV_HEADS = 2
C_GROUP = C_Q_HEADS // C_KV_HEADS
C_HD = 64
WINDOW = 128
WINDOW_CHUNKS = -(-WINDOW // CHUNK)
C_BEHIND = WINDOW_CHUNKS * CHUNK

kernel_name = "chunked_hybrid_diff_dsa_swa_macaron"


def _rms_norm(x, g=None, eps=1e-6):
    xf = x.astype(jnp.float32)
    y = xf * lax.rsqrt(jnp.mean(xf * xf, axis=-1, keepdims=True) + eps)
    if g is not None:
        y = y * g.astype(jnp.float32)
    return y.astype(x.dtype)


def _swiglu(x, wi, wo):
    a, b = jnp.split(x @ wi, 2, axis=-1)
    return (jax.nn.silu(a) * b) @ wo


def _chunk_id(p):
    return jnp.where(p < N_META, 0, 1 + (p - N_META) // CHUNK)


def _rel_bucket(rel):
    half = REL_BUCKETS // 2
    max_exact = half // 2
    n = jnp.abs(rel)
    large = max_exact + (jnp.log(jnp.maximum(n, 1).astype(jnp.float32) / max_exact)
                         / math.log(REL_MAX_DIST / max_exact) * (half - max_exact)).astype(jnp.int32)
    large = jnp.minimum(large, half - 1)
    return jnp.where(rel > 0, half, 0) + jnp.where(n < max_exact, n, large)


def _rel_bias(rel_bias, rel):
    return rel_bias[_rel_bucket(rel)]


def _to_blocks(a):
    b, s = a.shape[:2]
    return jnp.moveaxis(a.reshape(b, s // Q_BLOCK, Q_BLOCK, *a.shape[2:]), 1, 0)


def _from_blocks(a):
    nb, b = a.shape[:2]
    return jnp.moveaxis(a, 0, 1).reshape(b, nb * Q_BLOCK, *a.shape[3:])


def _diff_attention(h, pos, cid, w_in, qk_norm, lam, subln, lambda_init, rel_bias):
    bsz, t, _ = h.shape
    q, k, v = jnp.split(h @ w_in, [2 * A_HEADS * A_HD, 4 * A_HEADS * A_HD], axis=-1)
    q = _rms_norm(q.reshape(bsz, t, A_HEADS, 2, A_HD), qk_norm[0])
    k = _rms_norm(k.reshape(bsz, t, A_HEADS, 2, A_HD), qk_norm[1])
    v = v.reshape(bsz, t, A_HEADS, A_VD)
    lf = lam.astype(jnp.float32)
    lam_full = jnp.exp(jnp.dot(lf[0], lf[1])) - jnp.exp(jnp.dot(lf[2], lf[3])) + lambda_init
    scale = A_HD ** -0.5

    def block(qb, qpos):
        nq = qpos.shape[0]
        bias = _rel_bias(rel_bias, pos[None, :] - qpos[:, None])
        bias = bias.reshape(nq, t, 2, A_HEADS).transpose(2, 3, 0, 1)
        logits = jnp.einsum("bqhmd,bkhmd->bmhqk", qb, k).astype(jnp.float32) * scale + bias
        visible = cid[None, :] <= _chunk_id(qpos)[:, None]
        p = jax.nn.softmax(jnp.where(visible, logits, NEG_INF), axis=-1)
        pd = p[:, 0] - lam_full * p[:, 1]
        return jnp.einsum("bhqk,bkhe->bqhe", pd.astype(v.dtype), v)

    o_meta = block(q[:, :N_META], pos[:N_META])
    o_real = _from_blocks(lax.map(lambda a: block(*a),
                                  (_to_blocks(q[:, N_META:]), pos[N_META:].reshape(-1, Q_BLOCK))))
    o = jnp.concatenate([o_meta, o_real], axis=1)
    o = _rms_norm(o, subln) * (1.0 - lambda_init)
    return o.reshape(bsz, t, A_HEADS * A_VD)


def _dsa_attention(h, pos, cid, w_in, latent_norm, w_uq, q_norm, w_uv, rel_bias, k_sel):
    bsz, t, _ = h.shape
    c_q, c_kv, k_idx, w_idx = jnp.split(
        h @ w_in, [B_Q_RANK, B_Q_RANK + B_KV_RANK, B_Q_RANK + B_KV_RANK + IDX_DIM], axis=-1)
    c_q = _rms_norm(c_q, latent_norm[0])
    c_kv = _rms_norm(c_kv, latent_norm[1])
    q_abs, q_idx = jnp.split(c_q @ w_uq, [B_HEADS * B_KV_RANK], axis=-1)
    q_abs = _rms_norm(q_abs.reshape(bsz, t, B_HEADS, B_KV_RANK), q_norm)
    q_idx = q_idx.reshape(bsz, t, IDX_HEADS, IDX_DIM)
    k_idx = _rms_norm(k_idx)
    w_idx = w_idx * IDX_HEADS ** -0.5
    gather = jax.vmap(lambda src, idx: src[idx])

    def block(qa, qi, wi, qpos):
        cid_q = _chunk_id(qpos)
        act = jax.nn.relu(jnp.einsum("bqhd,bsd->bqhs", qi, k_idx) * IDX_DIM ** -0.5)
        score = jnp.einsum("bqh,bqhs->bqs", wi, act).astype(jnp.float32)
        score = jnp.where(cid[None, :] <= cid_q[:, None], score, NEG_INF)
        _, sel = lax.top_k(score, k_sel)
        kv = gather(c_kv, sel)
        valid = cid[sel] <= cid_q[None, :, None]
        bias = _rel_bias(rel_bias, pos[sel] - qpos[None, :, None])
        logits = (jnp.einsum("bqhr,bqkr->bqhk", qa, kv).astype(jnp.float32) * B_KV_RANK ** -0.5
                  + jnp.swapaxes(bias, -1, -2))
        p = jax.nn.softmax(jnp.where(valid[:, :, None, :], logits, NEG_INF), axis=-1)
        o_lat = jnp.einsum("bqhk,bqkr->bqhr", p.astype(kv.dtype), kv)
        o = jnp.einsum("bqhr,hrd->bqhd", o_lat, w_uv)
        return o.reshape(o.shape[0], o.shape[1], B_HEADS * B_VD)

    o_meta = block(q_abs[:, :N_META], q_idx[:, :N_META], w_idx[:, :N_META], pos[:N_META])
    o_real = _from_blocks(lax.map(lambda a: block(*a),
                                  (_to_blocks(q_abs[:, N_META:]), _to_blocks(q_idx[:, N_META:]),
                                   _to_blocks(w_idx[:, N_META:]), pos[N_META:].reshape(-1, Q_BLOCK))))
    return jnp.concatenate([o_meta, o_real], axis=1)


def _swa_attention(h, w_in, qk_norm, sinks, rel_bias):
    bsz, t, _ = h.shape
    q, k, v = jnp.split(h @ w_in, [C_Q_HEADS * C_HD, (C_Q_HEADS + C_KV_HEADS) * C_HD], axis=-1)
    q = _rms_norm(q.reshape(bsz, t, C_Q_HEADS, C_HD), qk_norm[0])
    k = _rms_norm(k.reshape(bsz, t, C_KV_HEADS, C_HD), qk_norm[1])
    v = v.reshape(bsz, t, C_KV_HEADS, C_HD)
    sink = sinks.astype(jnp.float32).reshape(C_KV_HEADS, C_GROUP, 1, 1)

    def attend(qb, kb, vb, qpos, kpos, visible):
        nq, nk = qb.shape[1], kb.shape[1]
        bias = _rel_bias(rel_bias, kpos[None, :] - qpos[:, None])
        bias = bias.transpose(2, 0, 1).reshape(C_KV_HEADS, C_GROUP, nq, nk)
        qg = qb.reshape(bsz, nq, C_KV_HEADS, C_GROUP, C_HD)
        logits = jnp.einsum("bqgjd,bkgd->bgjqk", qg, kb).astype(jnp.float32) * C_HD ** -0.5 + bias
        logits = jnp.where(visible, logits, NEG_INF)
        logits = jnp.concatenate([logits, jnp.broadcast_to(sink, logits.shape[:-1] + (1,))], axis=-1)
        p = jax.nn.softmax(logits, axis=-1)[..., :-1]
        o = jnp.einsum("bgjqk,bkgd->bqgjd", p.astype(vb.dtype), vb)
        return o.reshape(bsz, nq, C_Q_HEADS * C_HD)

    meta_pos = jnp.arange(N_META)
    o_meta = attend(q[:, :N_META], k[:, :N_META], v[:, :N_META], meta_pos, meta_pos,
                    jnp.ones((N_META, N_META), dtype=bool))
    pad = ((0, 0), (C_BEHIND, 0), (0, 0), (0, 0))
    k_pad = jnp.pad(k[:, N_META:], pad)
    v_pad = jnp.pad(v[:, N_META:], pad)
    band = Q_BLOCK + C_BEHIND

    def real_block(qb, blk):
        start = blk * Q_BLOCK
        fq = start + jnp.arange(Q_BLOCK)
        fk = start - C_BEHIND + jnp.arange(band)
        kb = jnp.concatenate([k[:, :N_META], lax.dynamic_slice_in_dim(k_pad, start, band, axis=1)], axis=1)
        vb = jnp.concatenate([v[:, :N_META], lax.dynamic_slice_in_dim(v_pad, start, band, axis=1)], axis=1)
        cq, ck = fq // CHUNK, fk // CHUNK
        in_window = ((fk[None, :] >= 0) & (ck[None, :] <= cq[:, None])
                     & (ck[None, :] >= cq[:, None] - WINDOW_CHUNKS))
        visible = jnp.concatenate([jnp.ones((Q_BLOCK, N_META), dtype=bool), in_window], axis=1)
        kpos = jnp.concatenate([meta_pos, N_META + fk])
        return attend(qb, kb, vb, N_META + fq, kpos, visible)

    nblk = (t - N_META) // Q_BLOCK
    o_real = _from_blocks(lax.map(lambda a: real_block(*a), (_to_blocks(q[:, N_META:]), jnp.arange(nblk))))
    return jnp.concatenate([o_meta, o_real], axis=1)


def setup_inputs(seed: int = 0) -> dict:
    key = jax.random.key(seed)
    ks = iter(jax.random.split(key, 32))
    f32 = jnp.float32
    n_a = len(range(0, DEPTH, N_MIXERS))
    n_b = len(range(1, DEPTH, N_MIXERS))
    n_c = len(range(2, DEPTH, N_MIXERS))

    def w(shape, fan_in):
        return jax.random.normal(next(ks), shape, f32) * fan_in ** -0.5

    def gain(shape):
        return 1.0 + 0.05 * jax.random.normal(next(ks), shape, f32)

    def rnd(shape, s):
        return s * jax.random.normal(next(ks), shape, f32)

    return {
        "x": jax.random.normal(next(ks), (BATCH, SEQ, D_MODEL), f32),
        "meta_tokens": rnd((N_META, D_MODEL), 1.0),
        "rel_bias": rnd((REL_BUCKETS, REL_HEADS), 0.5),
        "ln_ffn1": gain((DEPTH, D_MODEL)),
        "ffn1_wi": w((DEPTH, D_MODEL, 2 * D_FF), D_MODEL),
        "ffn1_wo": w((DEPTH, D_FF, D_MODEL), D_FF),
        "ln_mix": gain((DEPTH, D_MODEL)),
        "w_out": w((DEPTH, D_MODEL, D_MODEL), D_MODEL),
        "ln_ffn2": gain((DEPTH, D_MODEL)),
        "ffn2_wi": w((DEPTH, D_MODEL, 2 * D_FF), D_MODEL),
        "ffn2_wo": w((DEPTH, D_FF, D_MODEL), D_FF),
        "a_w_in": w((n_a, D_MODEL, 4 * A_HEADS * A_HD + A_HEADS * A_VD), D_MODEL),
        "a_qk_norm": gain((n_a, 2, A_HD)),
        "a_lambda": rnd((n_a, 4, A_HD), 0.1),
        "a_subln": gain((n_a, A_VD)),
        "b_w_in": w((n_b, D_MODEL, B_Q_RANK + B_KV_RANK + IDX_DIM + IDX_HEADS), D_MODEL),
        "b_latent_norm": gain((n_b, 2, B_KV_RANK)),
        "b_w_uq": w((n_b, B_Q_RANK, B_HEADS * B_KV_RANK + IDX_HEADS * IDX_DIM), B_Q_RANK),
        "b_q_norm": gain((n_b, B_KV_RANK)),
        "b_w_uv": w((n_b, B_HEADS, B_KV_RANK, B_VD), B_KV_RANK),
        "c_w_in": w((n_c, D_MODEL, (C_Q_HEADS + 2 * C_KV_HEADS) * C_HD), D_MODEL),
        "c_qk_norm": gain((n_c, 2, C_HD)),
        "c_sinks": rnd((n_c, C_Q_HEADS), 0.5),
    }


def reference(x, meta_tokens, rel_bias, ln_ffn1, ffn1_wi, ffn1_wo, ln_mix, w_out, ln_ffn2, ffn2_wi,
              ffn2_wo, a_w_in, a_qk_norm, a_lambda, a_subln, b_w_in, b_latent_norm, b_w_uq, b_q_norm,
              b_w_uv, c_w_in, c_qk_norm, c_sinks):
    bsz, s, d = x.shape
    k_sel = min(TOPK_MAX, s // 4)
    t = N_META + s
    pos = jnp.arange(t)
    cid = _chunk_id(pos)
    h = jnp.concatenate([jnp.broadcast_to(meta_tokens.astype(x.dtype), (bsz, N_META, d)), x], axis=1)
    for layer in range(DEPTH):
        h = h + 0.5 * _swiglu(_rms_norm(h, ln_ffn1[layer]), ffn1_wi[layer], ffn1_wo[layer])
        hn = _rms_norm(h, ln_mix[layer])
        kind, j = layer % N_MIXERS, layer // N_MIXERS
        if kind == 0:
            lambda_init = 0.8 - 0.6 * math.exp(-0.3 * layer)
            mix = _diff_attention(hn, pos, cid, a_w_in[j], a_qk_norm[j], a_lambda[j], a_subln[j],
                                  lambda_init, rel_bias)
        elif kind == 1:
            mix = _dsa_attention(hn, pos, cid, b_w_in[j], b_latent_norm[j], b_w_uq[j], b_q_norm[j],
                                 b_w_uv[j], rel_bias, k_sel)
        else:
            mix = _swa_attention(hn, c_w_in[j], c_qk_norm[j], c_sinks[j], rel_bias)
        h = h + mix @ w_out[layer]
        h = h + 0.5 * _swiglu(_rms_norm(h, ln_ffn2[layer]), ffn2_wi[layer], ffn2_wo[layer])
    return h[:, N_META:]
```

```python
import functools
import math

import numpy as np
import jax
import jax.numpy as jnp
from jax import lax
from jax.experimental import pallas as pl
from jax.experimental.pallas import tpu as pltpu

D_MODEL = 1024
DEPTH = 4
CHUNK = 64
N_META = 16
N_MIXERS = 3
NEG_INF = -1e30
REL_BUCKETS = 32
REL_MAX_DIST = 128
REL_HEADS = 16
D_FF = 2816
A_HEADS = 8
A_HD = 64
A_VD = 2 * A_HD
B_HEADS = 16
B_Q_RANK = 256
B_KV_RANK = 256
B_VD = 64
IDX_HEADS = 8
IDX_DIM = 64
TOPK_MAX = 256
C_Q_HEADS = 16
C_KV_HEADS = 2
C_GROUP = C_Q_HEADS // C_KV_HEADS
C_HD = 64
EPS = 1e-6

LANES = 128
BF16_ROWS = 16
VMEM_LIMIT = 56 * 1024 * 1024
INT_MIN = -(2 ** 31)

KIND_DIAG, KIND_PREV, KIND_FAR, KIND_META0, KIND_METAMETA = 0, 1, 2, 3, 4
N_KINDS = 5

f32 = jnp.float32
bf16 = jnp.bfloat16


def _cparams(sem):
    return pltpu.CompilerParams(dimension_semantics=sem, vmem_limit_bytes=VMEM_LIMIT)


def _row_tile(n, cap):
    best = None
    for t in range(BF16_ROWS, cap + 1, BF16_ROWS):
        if n % t == 0:
            best = t
    assert best is not None
    return best


def _dot(a, b):
    return jnp.dot(a, b, preferred_element_type=f32)


def _dot_nt(a, b):
    return lax.dot_general(a, b, (((1,), (1,)), ((), ())), preferred_element_type=f32)


def _rms_rows(x):
    return x * lax.rsqrt(jnp.mean(x * x, axis=-1, keepdims=True) + EPS)


def _lo_half_mask(shape, period, half):
    return (lax.broadcasted_iota(jnp.int32, shape, 1) & (period - 1)) < half


def _group_rms(x, group):
    r, c = x.shape
    outs = []
    if group == 64:
        lo = _lo_half_mask((r, LANES), LANES, 64)
        for ci in range(c // LANES):
            xc = x[:, ci * LANES:(ci + 1) * LANES]
            x2 = xc * xc
            s_lo = jnp.sum(jnp.where(lo, x2, 0.0), axis=-1, keepdims=True)
            s_hi = jnp.sum(jnp.where(lo, 0.0, x2), axis=-1, keepdims=True)
            inv = jnp.where(lo, lax.rsqrt(s_lo * (1.0 / 64) + EPS), lax.rsqrt(s_hi * (1.0 / 64) + EPS))
            outs.append(xc * inv)
    else:
        for gi in range(c // group):
            outs.append(_rms_rows(x[:, gi * group:(gi + 1) * group]))
    return outs[0] if len(outs) == 1 else jnp.concatenate(outs, axis=-1)


def _tile_iotas():
    r = lax.broadcasted_iota(jnp.int32, (LANES, LANES), 0)
    c = lax.broadcasted_iota(jnp.int32, (LANES, LANES), 1)
    return r, c


def _vec(s):
    return jnp.full((LANES, LANES), s, jnp.int32)


def _ffn_kernel(h_ref, g_ref, wa_ref, wb_ref, wo_ref, o_ref, xn_ref, acc_ref):
    j = pl.program_id(1)

    @pl.when(j == 0)
    def _():
        xn_ref[...] = (_rms_rows(h_ref[...]) * g_ref[...]).astype(bf16)
        acc_ref[...] = jnp.zeros_like(acc_ref)

    xn = xn_ref[...]
    a = _dot(xn, wa_ref[...])
    b = _dot(xn, wb_ref[...])
    act = (a / (1.0 + jnp.exp(-a)) * b).astype(bf16)
    acc_ref[...] += _dot(act, wo_ref[...])

    @pl.when(j == pl.num_programs(1) - 1)
    def _():
        o_ref[...] = h_ref[...] + 0.5 * acc_ref[...]


def _ffn(h, g, wi, wo, *, tm, fc):
    n, d = h.shape
    dff = wo.shape[0]
    nj = dff // fc
    return pl.pallas_call(
        _ffn_kernel,
        grid=(n // tm, nj),
        in_specs=[
            pl.BlockSpec((tm, d), lambda i, j: (i, 0)),
            pl.BlockSpec((1, d), lambda i, j: (0, 0)),
            pl.BlockSpec((d, fc), lambda i, j: (0, j)),
            pl.BlockSpec((d, fc), lambda i, j: (0, j + nj)),
            pl.BlockSpec((fc, d), lambda i, j: (j, 0)),
        ],
        out_specs=pl.BlockSpec((tm, d), lambda i, j: (i, 0)),
        out_shape=jax.ShapeDtypeStruct((n, d), f32),
        scratch_shapes=[pltpu.VMEM((tm, d), bf16), pltpu.VMEM((tm, d), f32)],
        compiler_params=_cparams(("arbitrary", "arbitrary")),
        name="ffn",
    )(h, g.reshape(1, d), wi, wi, wo)


def _outproj_kernel(h_ref, m_ref, w_ref, o_ref):
    o_ref[...] = h_ref[...] + _dot(m_ref[...], w_ref[...])


def _outproj(h, mix, w, *, tm):
    n, d = h.shape
    return pl.pallas_call(
        _outproj_kernel,
        grid=(n // tm,),
        in_specs=[
            pl.BlockSpec((tm, d), lambda i: (i, 0)),
            pl.BlockSpec((tm, mix.shape[1]), lambda i: (i, 0)),
            pl.BlockSpec(w.shape, lambda i: (0, 0)),
        ],
        out_specs=pl.BlockSpec((tm, d), lambda i: (i, 0)),
        out_shape=jax.ShapeDtypeStruct((n, d), f32),
        compiler_params=_cparams(("arbitrary",)),
        name="outproj",
    )(h, mix, w)


def _rel_bucket(rel):
    half = REL_BUCKETS // 2
    max_exact = half // 2
    n = jnp.abs(rel)
    large = max_exact + (jnp.log(jnp.maximum(n, 1).astype(jnp.float32) / max_exact)
                         / math.log(REL_MAX_DIST / max_exact) * (half - max_exact)).astype(jnp.int32)
    large = jnp.minimum(large, half - 1)
    return jnp.where(rel > 0, half, 0) + jnp.where(n < max_exact, n, large)


def _rel_tiles():
    r = np.arange(LANES)[:, None]
    c = np.arange(LANES)[None, :]
    rel = np.stack([
        c - r,
        c - r - LANES,
        np.full((LANES, LANES), -4 * LANES),
        (c % N_META) - N_META - r,
        (c % N_META) - (r % N_META),
    ]).astype(np.int32)
    return rel


def _bias_kernel(rb_ref, bucket_ref, o_ref):
    h = pl.program_id(0)
    for kind in range(N_KINDS):
        bk = bucket_ref[kind]
        acc = jnp.zeros((LANES, LANES), f32)
        for b in range(REL_BUCKETS):
            acc = jnp.where(bk == b, rb_ref[b, h], acc)
        o_ref[kind, 0] = acc


def _bias_tiles(rel_bias):
    bucket = _rel_bucket(jnp.asarray(_rel_tiles()))
    return pl.pallas_call(
        _bias_kernel,
        grid=(REL_HEADS,),
        in_specs=[
            pl.BlockSpec(memory_space=pltpu.SMEM),
            pl.BlockSpec((N_KINDS, LANES, LANES), lambda h: (0, 0, 0)),
        ],
        out_specs=pl.BlockSpec((N_KINDS, 1, LANES, LANES), lambda h: (0, h, 0, 0)),
        out_shape=jax.ShapeDtypeStruct((N_KINDS, REL_HEADS, LANES, LANES), f32),
        compiler_params=_cparams(("arbitrary",)),
        name="bias_tiles",
    )(rel_bias, bucket)


A_QD = A_HEADS * 2 * A_HD


def _proj_a_kernel(h_ref, g_ref, w_ref, gq_ref, gk_ref, o_ref):
    xn = (_rms_rows(h_ref[...]) * g_ref[...]).astype(bf16)
    y = _dot(xn, w_ref[...])
    q = _group_rms(y[:, :A_QD], A_HD) * gq_ref[...] * (A_HD ** -0.5)
    lo = _lo_half_mask(q.shape, 2 * A_HD, A_HD)
    o_ref[:, :A_QD] = jnp.where(lo, q, 0.0).astype(bf16)
    o_ref[:, A_QD:2 * A_QD] = jnp.where(lo, 0.0, q).astype(bf16)
    o_ref[:, 2 * A_QD:3 * A_QD] = (_group_rms(y[:, A_QD:2 * A_QD], A_HD) * gk_ref[...]).astype(bf16)
    o_ref[:, 3 * A_QD:] = y[:, 2 * A_QD:].astype(bf16)


def _proj_a(h, g, w, qk_norm, *, tm):
    n, d = h.shape
    nw = w.shape[1]
    nout = nw + A_QD
    gq = jnp.tile(qk_norm[0], A_QD // A_HD).reshape(1, A_QD)
    gk = jnp.tile(qk_norm[1], A_QD // A_HD).reshape(1, A_QD)
    return pl.pallas_call(
        _proj_a_kernel,
        grid=(n // tm,),
        in_specs=[
            pl.BlockSpec((tm, d), lambda i: (i, 0)),
            pl.BlockSpec((1, d), lambda i: (0, 0)),
            pl.BlockSpec((d, nw), lambda i: (0, 0)),
            pl.BlockSpec((1, A_QD), lambda i: (0, 0)),
            pl.BlockSpec((1, A_QD), lambda i: (0, 0)),
        ],
        out_specs=pl.BlockSpec((tm, nout), lambda i: (i, 0)),
        out_shape=jax.ShapeDtypeStruct((n, nout), bf16),
        compiler_params=_cparams(("arbitrary",)),
        name="proj_a",
    )(h, g.reshape(1, d), w, gq, gk)


def _attn_a_kernel(qlo_ref, qhi_ref, k_ref, v_ref, km_ref, vm_ref, bias_ref, lam_ref, sub_ref, o_ref,
                   qs_ref, m_ref, l_ref, acc_ref, *, nblk, lambda_init):
    s_id = pl.program_id(0)
    nb_real = pl.num_programs(0) - 1
    hw = 2 * A_HD
    r_io, c_io = _tile_iotas()

    def init():
        for h in range(A_HEADS):
            qs_ref[h, :LANES, :] = qlo_ref[:, h * hw:(h + 1) * hw]
            qs_ref[h, LANES:, :] = qhi_ref[:, h * hw:(h + 1) * hw]
        m_ref[...] = jnp.full(m_ref.shape, NEG_INF, f32)
        l_ref[...] = jnp.zeros(l_ref.shape, f32)
        acc_ref[...] = jnp.zeros(acc_ref.shape, f32)

    def tile(kt, vt, kind, mask):
        for h in range(A_HEADS):
            s = _dot_nt(qs_ref[h], kt[:, h * hw:(h + 1) * hw])
            b0, b1 = bias_ref[kind, h], bias_ref[kind, A_HEADS + h]
            if mask is not None:
                b0, b1 = jnp.where(mask, b0, NEG_INF), jnp.where(mask, b1, NEG_INF)
            s = s + jnp.concatenate([b0, b1], axis=0)
            m_old = m_ref[h]
            m_new = jnp.maximum(m_old, jnp.max(s, axis=-1, keepdims=True))
            alpha = jnp.exp(m_old - m_new)
            p = jnp.exp(s - m_new)
            l_ref[h] = alpha * l_ref[h] + jnp.sum(p, axis=-1, keepdims=True)
            acc_ref[h] = alpha * acc_ref[h] + _dot(p.astype(bf16), vt[:, h * hw:(h + 1) * hw])
            m_ref[h] = m_new

    def finalize():
        lam = lam_ref[...]
        lam_full = (jnp.exp(jnp.sum(lam[0:1] * lam[1:2], axis=-1, keepdims=True))
                    - jnp.exp(jnp.sum(lam[2:3] * lam[3:4], axis=-1, keepdims=True)) + lambda_init)
        for h in range(A_HEADS):
            o = acc_ref[h] / l_ref[h]
            d = o[:LANES] - lam_full * o[LANES:]
            d = _rms_rows(d) * sub_ref[...] * (1.0 - lambda_init)
            o_ref[:, h * hw:(h + 1) * hw] = d.astype(bf16)

    @pl.when(s_id < nb_real)
    def _():
        b = s_id // nblk
        i = s_id % nblk
        init()
        tile(km_ref[...], vm_ref[...], jnp.where(i == 0, KIND_META0, KIND_FAR), (c_io >> 4) == b)

        def body(j, carry):
            off = pl.multiple_of(j * LANES, LANES)
            tile(k_ref[pl.ds(off, LANES), :], v_ref[pl.ds(off, LANES), :],
                 jnp.where(j == i - 1, KIND_PREV, KIND_FAR), None)
            return carry

        lax.fori_loop(0, i, body, 0)
        off = pl.multiple_of(i * LANES, LANES)
        tile(k_ref[pl.ds(off, LANES), :], v_ref[pl.ds(off, LANES), :], KIND_DIAG,
             (c_io >> 6) <= (r_io >> 6))
        finalize()

    @pl.when(s_id == nb_real)
    def _():
        init()
        tile(km_ref[...], vm_ref[...], KIND_METAMETA, (c_io >> 4) == (r_io >> 4))
        finalize()


def _attn_a(qkv, bias, lam, subln, *, bsz, nblk, lambda_init):
    n = qkv.shape[0]
    nb_real = bsz * nblk
    seq = nblk * LANES
    d = A_QD
    kern = functools.partial(_attn_a_kernel, nblk=nblk, lambda_init=lambda_init)
    bclamp = lambda s: jnp.minimum(s // nblk, bsz - 1)
    return pl.pallas_call(
        kern,
        grid=(nb_real + 1,),
        in_specs=[
            pl.BlockSpec((LANES, d), lambda s: (s, 0)),
            pl.BlockSpec((LANES, d), lambda s: (s, 1)),
            pl.BlockSpec((seq, d), lambda s: (bclamp(s), 2)),
            pl.BlockSpec((seq, d), lambda s: (bclamp(s), 3)),
            pl.BlockSpec((LANES, d), lambda s: (nb_real, 2)),
            pl.BlockSpec((LANES, d), lambda s: (nb_real, 3)),
            pl.BlockSpec(bias.shape, lambda s: (0, 0, 0, 0)),
            pl.BlockSpec((4, A_HD), lambda s: (0, 0)),
            pl.BlockSpec((1, A_VD), lambda s: (0, 0)),
        ],
        out_specs=pl.BlockSpec((LANES, d), lambda s: (s, 0)),
        out_shape=jax.ShapeDtypeStruct((n, d), bf16),
        scratch_shapes=[
            pltpu.VMEM((A_HEADS, 2 * LANES, A_VD), bf16),
            pltpu.VMEM((A_HEADS, 2 * LANES, 1), f32),
            pltpu.VMEM((A_HEADS, 2 * LANES, 1), f32),
            pltpu.VMEM((A_HEADS, 2 * LANES, A_VD), f32),
        ],
        compiler_params=_cparams(("arbitrary",)),
        name="attn_a",
    )(qkv, qkv, qkv, qkv, qkv, qkv, bias, lam, subln.reshape(1, A_VD))


C_QD = C_Q_HEADS * C_HD
C_KD = 2 * C_KV_HEADS * C_HD


def _proj_c_kernel(h_ref, g_ref, w_ref, gq_ref, gk_ref, o_ref):
    xn = (_rms_rows(h_ref[...]) * g_ref[...]).astype(bf16)
    y = _dot(xn, w_ref[...])
    q = _group_rms(y[:, :C_QD], C_HD) * gq_ref[...] * (C_HD ** -0.5)
    lo = _lo_half_mask(q.shape, 2 * C_HD, C_HD)
    o_ref[:, :C_QD] = jnp.where(lo, q, 0.0).astype(bf16)
    o_ref[:, C_QD:2 * C_QD] = jnp.where(lo, 0.0, q).astype(bf16)
    o_ref[:, 2 * C_QD:2 * C_QD + C_KD] = (_group_rms(y[:, C_QD:C_QD + C_KD], C_HD) * gk_ref[...]).astype(bf16)
    o_ref[:, 2 * C_QD + C_KD:] = y[:, C_QD + C_KD:].astype(bf16)


def _proj_c(h, g, w, qk_norm, *, tm):
    n, d = h.shape
    nw = w.shape[1]
    nout = nw + C_QD
    gq = jnp.tile(qk_norm[0], C_QD // C_HD).reshape(1, C_QD)
    gk = jnp.tile(qk_norm[1], C_KD // C_HD).reshape(1, C_KD)
    return pl.pallas_call(
        _proj_c_kernel,
        grid=(n // tm,),
        in_specs=[
            pl.BlockSpec((tm, d), lambda i: (i, 0)),
            pl.BlockSpec((1, d), lambda i: (0, 0)),
            pl.BlockSpec((d, nw), lambda i: (0, 0)),
            pl.BlockSpec((1, C_QD), lambda i: (0, 0)),
            pl.BlockSpec((1, C_KD), lambda i: (0, 0)),
        ],
        out_specs=pl.BlockSpec((tm, nout), lambda i: (i, 0)),
        out_shape=jax.ShapeDtypeStruct((n, nout), bf16),
        compiler_params=_cparams(("arbitrary",)),
        name="proj_c",
    )(h, g.reshape(1, d), w, gq, gk)


def _attn_c_kernel(sink_ref, qlo_ref, qhi_ref, k_ref, v_ref, km_ref, vm_ref, bias_ref, o_ref, qs_ref, *, nblk):
    s_id = pl.program_id(0)
    nb_real = pl.num_programs(0) - 1
    r_io, c_io = _tile_iotas()
    rows = C_GROUP * LANES

    def attend(tiles):
        lo_t = c_io < C_HD
        for g in range(C_KV_HEADS):
            for hh in range(C_GROUP):
                cc = (g * C_GROUP + hh) // 2
                src = qlo_ref if hh % 2 == 0 else qhi_ref
                qs_ref[hh * LANES:(hh + 1) * LANES, :] = src[:, cc * LANES:(cc + 1) * LANES]
            sink = jnp.concatenate(
                [jnp.full((LANES, 1), sink_ref[g * C_GROUP + hh], f32) for hh in range(C_GROUP)], axis=0)
            ss = []
            m = sink
            for (kt, vt, kind, mask) in tiles:
                s = _dot_nt(qs_ref[...], kt[:, g * LANES:(g + 1) * LANES])
                s = s + jnp.concatenate(
                    [jnp.where(mask, bias_ref[kind, g * C_GROUP + hh], NEG_INF) for hh in range(C_GROUP)], axis=0)
                m = jnp.maximum(m, jnp.max(s, axis=-1, keepdims=True))
                ss.append(s)
            l = jnp.exp(sink - m)
            o = jnp.zeros((rows, LANES), f32)
            for s, (kt, vt, kind, mask) in zip(ss, tiles):
                p = jnp.exp(s - m)
                l = l + jnp.sum(p, axis=-1, keepdims=True)
                o = o + _dot(p.astype(bf16), vt[:, g * LANES:(g + 1) * LANES])
            o = o / l
            for cc in range(C_GROUP // 2):
                even = o[(2 * cc) * LANES:(2 * cc + 1) * LANES]
                odd = o[(2 * cc + 1) * LANES:(2 * cc + 2) * LANES]
                col = (g * (C_GROUP // 2) + cc) * LANES
                o_ref[:, col:col + LANES] = jnp.where(lo_t, even, odd).astype(bf16)

    @pl.when(s_id < nb_real)
    def _():
        b = s_id // nblk
        i = s_id % nblk
        first_half = r_io < CHUNK
        poff = pl.multiple_of(jnp.maximum(i - 1, 0) * LANES, LANES)
        coff = pl.multiple_of(i * LANES, LANES)
        attend([
            (km_ref[...], vm_ref[...], jnp.where(i == 0, KIND_META0, KIND_FAR), (c_io >> 4) == b),
            (k_ref[pl.ds(poff, LANES), :], v_ref[pl.ds(poff, LANES), :], KIND_PREV,
             (_vec(i) > 0) & (first_half | (c_io >= CHUNK))),
            (k_ref[pl.ds(coff, LANES), :], v_ref[pl.ds(coff, LANES), :], KIND_DIAG,
             (c_io < CHUNK) | (r_io >= CHUNK)),
        ])

    @pl.when(s_id == nb_real)
    def _():
        attend([(km_ref[...], vm_ref[...], KIND_METAMETA, (c_io >> 4) == (r_io >> 4))])


def _attn_c(qkv, bias, sinks, *, bsz, nblk):
    n = qkv.shape[0]
    nb_real = bsz * nblk
    seq = nblk * LANES
    kern = functools.partial(_attn_c_kernel, nblk=nblk)
    bclamp = lambda s: jnp.minimum(s // nblk, bsz - 1)
    kcol = 2 * C_QD // C_KD
    return pl.pallas_call(
        kern,
        grid=(nb_real + 1,),
        in_specs=[
            pl.BlockSpec(memory_space=pltpu.SMEM),
            pl.BlockSpec((LANES, C_QD), lambda s: (s, 0)),
            pl.BlockSpec((LANES, C_QD), lambda s: (s, 1)),
            pl.BlockSpec((seq, C_KD), lambda s: (bclamp(s), kcol)),
            pl.BlockSpec((seq, C_KD), lambda s: (bclamp(s), kcol + 1)),
            pl.BlockSpec((LANES, C_KD), lambda s: (nb_real, kcol)),
            pl.BlockSpec((LANES, C_KD), lambda s: (nb_real, kcol + 1)),
            pl.BlockSpec(bias.shape, lambda s: (0, 0, 0, 0)),
        ],
        out_specs=pl.BlockSpec((LANES, C_QD), lambda s: (s, 0)),
        out_shape=jax.ShapeDtypeStruct((n, C_QD), bf16),
        scratch_shapes=[pltpu.VMEM((C_GROUP * LANES, LANES), bf16)],
        compiler_params=_cparams(("arbitrary",)),
        name="attn_c",
    )(sinks, qkv, qkv, qkv, qkv, qkv, qkv, bias)


B_QA = B_HEADS * B_KV_RANK
B_QI = IDX_HEADS * IDX_DIM
B_W1 = 2 * B_Q_RANK + 2 * LANES


def _proj_b_kernel(h_ref, g_ref, w1_ref, ln_ref, wuq_ref, qn_ref, qa_ref, qi_ref, ckv_ref, kk_ref, wi_ref):
    xn = (_rms_rows(h_ref[...]) * g_ref[...]).astype(bf16)
    y = _dot(xn, w1_ref[...])
    r = B_Q_RANK
    cq = (_rms_rows(y[:, :r]) * ln_ref[0:1, :]).astype(bf16)
    ckv_ref[...] = (_rms_rows(y[:, r:2 * r]) * ln_ref[1:2, :]).astype(bf16)
    kk_ref[...] = _rms_rows(y[:, 2 * r:2 * r + LANES]).astype(bf16)
    wi_ref[...] = y[:, 2 * r + LANES:] * (IDX_HEADS ** -0.5)
    z = _dot(cq, wuq_ref[...])
    qa_ref[...] = (_group_rms(z[:, :B_QA], B_KV_RANK) * qn_ref[...] * (B_KV_RANK ** -0.5)).astype(bf16)
    qi = z[:, B_QA:] * (IDX_DIM ** -0.5)
    lo = _lo_half_mask(qi.shape, 2 * IDX_DIM, IDX_DIM)
    qi_ref[:, :B_QI] = jnp.where(lo, qi, 0.0).astype(bf16)
    qi_ref[:, B_QI:] = jnp.where(lo, 0.0, qi).astype(bf16)


def _proj_b(h, g, w1, latent_norm, wuq, q_norm, *, tm):
    n, d = h.shape
    qn = jnp.tile(q_norm, B_HEADS).reshape(1, B_QA)
    row = lambda i: (i, 0)
    const = lambda i: (0, 0)
    return pl.pallas_call(
        _proj_b_kernel,
        grid=(n // tm,),
        in_specs=[
            pl.BlockSpec((tm, d), row),
            pl.BlockSpec((1, d), const),
            pl.BlockSpec(w1.shape, const),
            pl.BlockSpec(latent_norm.shape, const),
            pl.BlockSpec(wuq.shape, const),
            pl.BlockSpec((1, B_QA), const),
        ],
        out_specs=[
            pl.BlockSpec((tm, B_QA), row),
            pl.BlockSpec((tm, 2 * B_QI), row),
            pl.BlockSpec((tm, B_KV_RANK), row),
            pl.BlockSpec((tm, LANES), row),
            pl.BlockSpec((tm, LANES), row),
        ],
        out_shape=[
            jax.ShapeDtypeStruct((n, B_QA), bf16),
            jax.ShapeDtypeStruct((n, 2 * B_QI), bf16),
            jax.ShapeDtypeStruct((n, B_KV_RANK), bf16),
            jax.ShapeDtypeStruct((n, LANES), bf16),
            jax.ShapeDtypeStruct((n, LANES), f32),
        ],
        compiler_params=_cparams(("arbitrary",)),
        name="proj_b",
    )(h, g.reshape(1, d), w1, latent_norm, wuq, qn)


def _attn_b_kernel(qa_ref, qi_ref, wi_ref, ckv_ref, kk_ref, ckvm_ref, kkm_ref, bias_ref, wlo_ref, whi_ref,
                   o_ref, qs_ref, is_ref, key_ref, m_ref, l_ref, acc_ref, *, nblk, k_sel):
    s_id = pl.program_id(0)
    nb_real = pl.num_programs(0) - 1
    r_io, c_io = _tile_iotas()
    rk = B_KV_RANK

    def init():
        for h in range(B_HEADS):
            qs_ref[h * LANES:(h + 1) * LANES, :] = qa_ref[:, h * rk:(h + 1) * rk]
        m_ref[...] = jnp.full(m_ref.shape, NEG_INF, f32)
        l_ref[...] = jnp.zeros(l_ref.shape, f32)
        acc_ref[...] = jnp.zeros(acc_ref.shape, f32)

    def tile(ckv, kind, sel):
        s_all = _dot_nt(qs_ref[...], ckv)
        ps = []
        alphas = []
        for h in range(B_HEADS):
            s = jnp.where(sel, s_all[h * LANES:(h + 1) * LANES] + bias_ref[kind, h], NEG_INF)
            m_old = m_ref[h]
            m_new = jnp.maximum(m_old, jnp.max(s, axis=-1, keepdims=True))
            alpha = jnp.exp(m_old - m_new)
            p = jnp.exp(s - m_new)
            l_ref[h] = alpha * l_ref[h] + jnp.sum(p, axis=-1, keepdims=True)
            m_ref[h] = m_new
            ps.append(p.astype(bf16))
            alphas.append(alpha)
        pv = _dot(jnp.concatenate(ps, axis=0), ckv)
        for h in range(B_HEADS):
            acc_ref[h] = alphas[h] * acc_ref[h] + pv[h * LANES:(h + 1) * LANES]

    def finalize():
        for cc in range(B_HEADS // 2):
            o_even = (acc_ref[2 * cc] / l_ref[2 * cc]).astype(bf16)
            o_odd = (acc_ref[2 * cc + 1] / l_ref[2 * cc + 1]).astype(bf16)
            o_ref[:, cc * LANES:(cc + 1) * LANES] = (_dot(o_even, wlo_ref[cc]) + _dot(o_odd, whi_ref[cc])).astype(bf16)

    def index_scores(kk):
        s = jnp.maximum(_dot_nt(is_ref[...], kk), 0.0)
        w = wi_ref[...]
        sc = jnp.zeros((LANES, LANES), f32)
        for hh in range(IDX_HEADS):
            sc = sc + w[:, hh:hh + 1] * s[hh * LANES:(hh + 1) * LANES]
        return sc

    def sort_key(sc, visible):
        bits = lax.bitcast_convert_type(sc + 0.0, jnp.int32)
        key = jnp.where(bits < 0, bits ^ jnp.int32(0x7FFFFFFF), bits)
        return jnp.where(visible, key, jnp.int32(INT_MIN))

    @pl.when(s_id < nb_real)
    def _():
        b = s_id // nblk
        i = s_id % nblk
        ntile = i + 2
        init()
        for hh in range(IDX_HEADS):
            base = (hh % 2) * B_QI + (hh // 2) * LANES
            is_ref[hh * LANES:(hh + 1) * LANES, :] = qi_ref[:, base:base + LANES]

        key_ref[0] = sort_key(index_scores(kkm_ref[...]), (c_io >> 4) == b)

        def score_body(j, carry):
            off = pl.multiple_of(j * LANES, LANES)
            vis = (_vec(j) < i) | ((c_io >> 6) <= (r_io >> 6))
            key_ref[j + 1] = sort_key(index_scores(kk_ref[pl.ds(off, LANES), :]), vis)
            return carry

        lax.fori_loop(0, i + 1, score_body, 0)

        def count(pred):
            def cbody(t, accv):
                return accv + jnp.where(pred(key_ref[t], t), 1.0, 0.0)
            accv = lax.fori_loop(0, ntile, cbody, jnp.zeros((LANES, LANES), f32))
            return jnp.sum(accv, axis=-1, keepdims=True)

        kf = float(k_sel)
        zero = jnp.zeros((LANES, 1), jnp.int32)
        t0 = jnp.where(count(lambda k, t: k >= zero) >= kf, zero, jnp.int32(INT_MIN))

        def bit_body(it, tcur):
            cand = tcur | jnp.left_shift(jnp.int32(1), 30 - it)
            return jnp.where(count(lambda k, t: k >= cand) >= kf, cand, tcur)

        thr = lax.fori_loop(0, 31, bit_body, t0)

        need = kf - count(lambda k, t: k > thr)
        n_eq = count(lambda k, t: k == thr)
        has_thr = thr > jnp.int32(INT_MIN)
        tied = jnp.max(jnp.where(has_thr & (n_eq > need), 1.0, 0.0)) > 0.0

        def tie_search(_):
            def jbody(it, jcur):
                cand = jcur | jnp.left_shift(jnp.int32(1), 11 - it)
                cnt = count(lambda k, t: (k == thr) & ((t * LANES + c_io) < cand))
                return jnp.where(cnt < need, cand, jcur)
            return lax.fori_loop(0, 12, jbody, jnp.zeros((LANES, 1), jnp.int32))

        j_last = lax.cond(tied, tie_search, lambda _: jnp.full((LANES, 1), 4095, jnp.int32), 0)
        j_last = jnp.where(has_thr, j_last, -1)

        def selected(t):
            k = key_ref[t]
            return (k > thr) | ((k == thr) & ((t * LANES + c_io) <= j_last))

        tile(ckvm_ref[...], jnp.where(i == 0, KIND_META0, KIND_FAR), selected(0))

        def att_body(j, carry):
            off = pl.multiple_of(j * LANES, LANES)
            kind = jnp.where(j == i, KIND_DIAG, jnp.where(j == i - 1, KIND_PREV, KIND_FAR))
            tile(ckv_ref[pl.ds(off, LANES), :], kind, selected(j + 1))
            return carry

        lax.fori_loop(0, i + 1, att_body, 0)
        finalize()

    @pl.when(s_id == nb_real)
    def _():
        init()
        tile(ckvm_ref[...], KIND_METAMETA, (c_io >> 4) == (r_io >> 4))
        finalize()


def _attn_b(qa, qi, wi, ckv, kk, bias, wlo, whi, *, bsz, nblk, k_sel):
    n = qa.shape[0]
    nb_real = bsz * nblk
    seq = nblk * LANES
    assert k_sel >= N_META and (nblk + 1) * LANES <= 4096
    kern = functools.partial(_attn_b_kernel, nblk=nblk, k_sel=k_sel)
    bclamp = lambda s: (jnp.minimum(s // nblk, bsz - 1), 0)
    blk = lambda s: (s, 0)
    meta = lambda s: (nb_real, 0)
    return pl.pallas_call(
        kern,
        grid=(nb_real + 1,),
        in_specs=[
            pl.BlockSpec((LANES, B_QA), blk),
            pl.BlockSpec((LANES, 2 * B_QI), blk),
            pl.BlockSpec((LANES, LANES), blk),
            pl.BlockSpec((seq, B_KV_RANK), bclamp),
            pl.BlockSpec((seq, LANES), bclamp),
            pl.BlockSpec((LANES, B_KV_RANK), meta),
            pl.BlockSpec((LANES, LANES), meta),
            pl.BlockSpec(bias.shape, lambda s: (0, 0, 0, 0)),
            pl.BlockSpec(wlo.shape, lambda s: (0, 0, 0)),
            pl.BlockSpec(whi.shape, lambda s: (0, 0, 0)),
        ],
        out_specs=pl.BlockSpec((LANES, B_HEADS * B_VD), blk),
        out_shape=jax.ShapeDtypeStruct((n, B_HEADS * B_VD), bf16),
        scratch_shapes=[
            pltpu.VMEM((B_HEADS * LANES, B_KV_RANK), bf16),
            pltpu.VMEM((IDX_HEADS * LANES, LANES), bf16),
            pltpu.VMEM((nblk + 1, LANES, LANES), jnp.int32),
            pltpu.VMEM((B_HEADS, LANES, 1), f32),
            pltpu.VMEM((B_HEADS, LANES, 1), f32),
            pltpu.VMEM((B_HEADS, LANES, B_KV_RANK), f32),
        ],
        compiler_params=_cparams(("arbitrary",)),
        name="attn_b",
    )(qa, qi, wi, ckv, kk, ckv, kk, bias, wlo, whi)


def kernel(x, meta_tokens, rel_bias, ln_ffn1, ffn1_wi, ffn1_wo, ln_mix, w_out, ln_ffn2, ffn2_wi, ffn2_wo, a_w_in, a_qk_norm, a_lambda, a_subln, b_w_in, b_latent_norm, b_w_uq, b_q_norm, b_w_uv, c_w_in, c_qk_norm, c_sinks):
    bsz, seq, d = x.shape
    assert d == D_MODEL and seq % LANES == 0 and bsz * N_META == LANES
    nblk = seq // LANES
    n = bsz * seq + LANES
    k_sel = min(TOPK_MAX, seq // 4)
    tm_ffn = _row_tile(n, 1408)
    tm_proj = _row_tile(n, 384)
    fc = 256

    h = jnp.concatenate([x.reshape(bsz * seq, d),
                         jnp.broadcast_to(meta_tokens.astype(x.dtype), (bsz, N_META, d)).reshape(LANES, d)], axis=0)
    bias = _bias_tiles(rel_bias)

    for layer in range(DEPTH):
        h = _ffn(h, ln_ffn1[layer], ffn1_wi[layer].astype(bf16), ffn1_wo[layer].astype(bf16), tm=tm_ffn, fc=fc)
        kind, j = layer % N_MIXERS, layer // N_MIXERS
        g = ln_mix[layer]
        if kind == 0:
            lambda_init = 0.8 - 0.6 * math.exp(-0.3 * layer)
            qkv = _proj_a(h, g, a_w_in[j].astype(bf16), a_qk_norm[j], tm=tm_proj)
            mix = _attn_a(qkv, bias, a_lambda[j], a_subln[j], bsz=bsz, nblk=nblk, lambda_init=lambda_init)
        elif kind == 1:
            w = b_w_in[j]
            r2 = B_Q_RANK + B_KV_RANK
            kcol = w[:, r2:r2 + IDX_DIM]
            w1 = jnp.concatenate([w[:, :r2], kcol, kcol, w[:, r2 + IDX_DIM:],
                                  jnp.zeros((d, LANES - IDX_HEADS), w.dtype)], axis=1).astype(bf16)
            assert w1.shape[1] == B_W1
            qa, qi, ckv, kk, wi = _proj_b(h, g, w1, b_latent_norm[j], b_w_uq[j].astype(bf16), b_q_norm[j], tm=tm_proj)
            wuv = b_w_uv[j].astype(bf16)
            zeros = jnp.zeros_like(wuv[0::2])
            wlo = jnp.concatenate([wuv[0::2], zeros], axis=-1)
            whi = jnp.concatenate([zeros, wuv[1::2]], axis=-1)
            mix = _attn_b(qa, qi, wi, ckv, kk, bias, wlo, whi, bsz=bsz, nblk=nblk, k_sel=k_sel)
        else:
            w = c_w_in[j]
            kcols = [w[:, C_QD + gi * C_HD:C_QD + (gi + 1) * C_HD] for gi in range(C_KV_HEADS)]
            voff = C_QD + C_KV_HEADS * C_HD
            vcols = [w[:, voff + gi * C_HD:voff + (gi + 1) * C_HD] for gi in range(C_KV_HEADS)]
            wc = jnp.concatenate([w[:, :C_QD]] + [kc for kc in kcols for _ in range(2)]
                                 + [vc for vc in vcols for _ in range(2)], axis=1).astype(bf16)
            qkv = _proj_c(h, g, wc, c_qk_norm[j], tm=tm_proj)
            mix = _attn_c(qkv, bias, c_sinks[j], bsz=bsz, nblk=nblk)
        h = _outproj(h, mix, w_out[layer].astype(bf16), tm=tm_ffn)
        h = _ffn(h, ln_ffn2[layer], ffn2_wi[layer].astype(bf16), ffn2_wo[layer].astype(bf16), tm=tm_ffn, fc=fc)
    return h[:bsz * seq].reshape(bsz, seq, d)
```

```python
import functools
import math

import numpy as np
import jax
import jax.numpy as jnp
from jax import lax
from jax.experimental import pallas as pl
from jax.experimental.pallas import tpu as pltpu

D_MODEL = 1024
DEPTH = 4
CHUNK = 64
N_META = 16
N_MIXERS = 3
NEG_INF = -1e30
REL_BUCKETS = 32
REL_MAX_DIST = 128
REL_HEADS = 16
D_FF = 2816
A_HEADS = 8
A_HD = 64
A_VD = 2 * A_HD
B_HEADS = 16
B_Q_RANK = 256
B_KV_RANK = 256
B_VD = 64
IDX_HEADS = 8
IDX_DIM = 64
TOPK_MAX = 256
C_Q_HEADS = 16
C_KV_HEADS = 2
C_GROUP = C_Q_HEADS // C_KV_HEADS
C_HD = 64
EPS = 1e-6

LANES = 128
BF16_ROWS = 16
VMEM_LIMIT = 56 * 1024 * 1024
INT_MIN = -(2 ** 31)
NSUB = 2

KIND_DIAG, KIND_PREV, KIND_FAR, KIND_META0, KIND_METAMETA, KIND_MASKED = 0, 1, 2, 3, 4, 5
N_KINDS = 6

f32 = jnp.float32
bf16 = jnp.bfloat16


def _cparams(sem):
    return pltpu.CompilerParams(dimension_semantics=sem, vmem_limit_bytes=VMEM_LIMIT)


def _row_tile(n, cap, mult=BF16_ROWS):
    best = None
    for t in range(mult, cap + 1, mult):
        if n % t == 0:
            best = t
    assert best is not None
    return best


def _dot(a, b):
    return jnp.dot(a, b, preferred_element_type=f32)


def _dot_nt(a, b):
    return lax.dot_general(a, b, (((1,), (1,)), ((), ())), preferred_element_type=f32)


def _rms_rows(x):
    return x * lax.rsqrt(jnp.mean(x * x, axis=-1, keepdims=True) + EPS)


def _lo_half_mask(shape, period, half):
    return (lax.broadcasted_iota(jnp.int32, shape, 1) & (period - 1)) < half


def _group_rms(x, group):
    r, c = x.shape
    outs = []
    if group == 64:
        lo = _lo_half_mask((r, LANES), LANES, 64)
        for ci in range(c // LANES):
            xc = x[:, ci * LANES:(ci + 1) * LANES]
            x2 = xc * xc
            s_lo = jnp.sum(jnp.where(lo, x2, 0.0), axis=-1, keepdims=True)
            s_hi = jnp.sum(jnp.where(lo, 0.0, x2), axis=-1, keepdims=True)
            inv = jnp.where(lo, lax.rsqrt(s_lo * (1.0 / 64) + EPS), lax.rsqrt(s_hi * (1.0 / 64) + EPS))
            outs.append(xc * inv)
    else:
        for gi in range(c // group):
            outs.append(_rms_rows(x[:, gi * group:(gi + 1) * group]))
    return outs[0] if len(outs) == 1 else jnp.concatenate(outs, axis=-1)


def _tile_iotas():
    r = lax.broadcasted_iota(jnp.int32, (LANES, LANES), 0)
    c = lax.broadcasted_iota(jnp.int32, (LANES, LANES), 1)
    return r, c


def _vec(s):
    return jnp.full((LANES, LANES), s, jnp.int32)


def _block_kind(rel):
    return jnp.where(rel < -1, KIND_FAR,
                     jnp.where(rel == -1, KIND_PREV, jnp.where(rel == 0, KIND_DIAG, KIND_MASKED)))


def _ffn_kernel(h_ref, g_ref, wa_ref, wb_ref, wo_ref, o_ref, xn_ref, acc_ref):
    j = pl.program_id(1)

    @pl.when(j == 0)
    def _():
        xn_ref[...] = (_rms_rows(h_ref[...]) * g_ref[...]).astype(bf16)
        acc_ref[...] = jnp.zeros_like(acc_ref)

    xn = xn_ref[...]
    a = _dot(xn, wa_ref[...])
    b = _dot(xn, wb_ref[...])
    act = (a / (1.0 + jnp.exp(-a)) * b).astype(bf16)
    acc_ref[...] += _dot(act, wo_ref[...])

    @pl.when(j == pl.num_programs(1) - 1)
    def _():
        o_ref[...] = h_ref[...] + 0.5 * acc_ref[...]


def _ffn(h, g, wi, wo, *, tm, fc):
    n, d = h.shape
    dff = wo.shape[0]
    nj = dff // fc
    return pl.pallas_call(
        _ffn_kernel,
        grid=(n // tm, nj),
        in_specs=[
            pl.BlockSpec((tm, d), lambda i, j: (i, 0)),
            pl.BlockSpec((1, d), lambda i, j: (0, 0)),
            pl.BlockSpec((d, fc), lambda i, j: (0, j)),
            pl.BlockSpec((d, fc), lambda i, j: (0, j + nj)),
            pl.BlockSpec((fc, d), lambda i, j: (j, 0)),
        ],
        out_specs=pl.BlockSpec((tm, d), lambda i, j: (i, 0)),
        out_shape=jax.ShapeDtypeStruct((n, d), f32),
        scratch_shapes=[pltpu.VMEM((tm, d), bf16), pltpu.VMEM((tm, d), f32)],
        compiler_params=_cparams(("arbitrary", "arbitrary")),
        name="ffn",
    )(h, g.reshape(1, d), wi, wi, wo)


def _outproj_kernel(h_ref, m_ref, w_ref, o_ref):
    o_ref[...] = h_ref[...] + _dot(m_ref[...], w_ref[...])


def _outproj(h, mix, w, *, tm):
    n, d = h.shape
    return pl.pallas_call(
        _outproj_kernel,
        grid=(n // tm,),
        in_specs=[
            pl.BlockSpec((tm, d), lambda i: (i, 0)),
            pl.BlockSpec((tm, mix.shape[1]), lambda i: (i, 0)),
            pl.BlockSpec(w.shape, lambda i: (0, 0)),
        ],
        out_specs=pl.BlockSpec((tm, d), lambda i: (i, 0)),
        out_shape=jax.ShapeDtypeStruct((n, d), f32),
        compiler_params=_cparams(("arbitrary",)),
        name="outproj",
    )(h, mix, w)


def _rel_bucket(rel):
    half = REL_BUCKETS // 2
    max_exact = half // 2
    n = jnp.abs(rel)
    large = max_exact + (jnp.log(jnp.maximum(n, 1).astype(jnp.float32) / max_exact)
                         / math.log(REL_MAX_DIST / max_exact) * (half - max_exact)).astype(jnp.int32)
    large = jnp.minimum(large, half - 1)
    return jnp.where(rel > 0, half, 0) + jnp.where(n < max_exact, n, large)


def _rel_tiles():
    r = np.arange(LANES)[:, None]
    c = np.arange(LANES)[None, :]
    far = np.full((LANES, LANES), -4 * LANES)
    ones = np.ones((LANES, LANES), bool)
    rels, vis = [], []
    for q, k in ((r, c), (c, r)):
        rels += [k - q, k - q - LANES, far, (k % N_META) - N_META - q, (k % N_META) - (q % N_META), far]
        if q is r:
            vis += [ones] * N_KINDS
        else:
            vis += [(k // CHUNK) <= (q // CHUNK), ones, ones, ones, (k // N_META) == (q // N_META), ~ones]
    return (np.stack([np.broadcast_to(a, (LANES, LANES)) for a in rels]).astype(np.int32),
            np.stack([np.broadcast_to(a, (LANES, LANES)) for a in vis]).astype(np.int32))


def _bias_kernel(rb_ref, bucket_ref, vis_ref, o_ref):
    h = pl.program_id(0)
    for kind in range(2 * N_KINDS):
        bk = bucket_ref[kind]
        acc = jnp.zeros((LANES, LANES), f32)
        for b in range(REL_BUCKETS):
            acc = jnp.where(bk == b, rb_ref[b, h], acc)
        o_ref[kind, 0] = jnp.where(vis_ref[kind] != 0, acc, NEG_INF)


def _bias_tiles(rel_bias):
    rel, vis = _rel_tiles()
    bucket = _rel_bucket(jnp.asarray(rel))
    nk = 2 * N_KINDS
    return pl.pallas_call(
        _bias_kernel,
        grid=(REL_HEADS,),
        in_specs=[
            pl.BlockSpec(memory_space=pltpu.SMEM),
            pl.BlockSpec((nk, LANES, LANES), lambda h: (0, 0, 0)),
            pl.BlockSpec((nk, LANES, LANES), lambda h: (0, 0, 0)),
        ],
        out_specs=pl.BlockSpec((nk, 1, LANES, LANES), lambda h: (0, h, 0, 0)),
        out_shape=jax.ShapeDtypeStruct((nk, REL_HEADS, LANES, LANES), f32),
        compiler_params=_cparams(("arbitrary",)),
        name="bias_tiles",
    )(rel_bias, bucket, jnp.asarray(vis))


def _bias_spec(which):
    return pl.BlockSpec((N_KINDS, REL_HEADS, LANES, LANES), lambda s: (which, 0, 0, 0))


A_QD = A_HEADS * 2 * A_HD


def _proj_a_kernel(h_ref, g_ref, w_ref, gq_ref, gk_ref, o_ref, vt_ref):
    xn = (_rms_rows(h_ref[...]) * g_ref[...]).astype(bf16)
    y = _dot(xn, w_ref[...])
    q = _group_rms(y[:, :A_QD], A_HD) * gq_ref[...] * (A_HD ** -0.5)
    lo = _lo_half_mask(q.shape, 2 * A_HD, A_HD)
    o_ref[:, :A_QD] = jnp.where(lo, q, 0.0).astype(bf16)
    o_ref[:, A_QD:2 * A_QD] = jnp.where(lo, 0.0, q).astype(bf16)
    o_ref[:, 2 * A_QD:] = (_group_rms(y[:, A_QD:2 * A_QD], A_HD) * gk_ref[...]).astype(bf16)
    for t in range(vt_ref.shape[0]):
        vt_ref[t] = y[t * LANES:(t + 1) * LANES, 2 * A_QD:].T.astype(bf16)


def _proj_a(h, g, w, qk_norm, *, tm):
    n, d = h.shape
    nw = w.shape[1]
    gq = jnp.tile(qk_norm[0], A_QD // A_HD).reshape(1, A_QD)
    gk = jnp.tile(qk_norm[1], A_QD // A_HD).reshape(1, A_QD)
    return pl.pallas_call(
        _proj_a_kernel,
        grid=(n // tm,),
        in_specs=[
            pl.BlockSpec((tm, d), lambda i: (i, 0)),
            pl.BlockSpec((1, d), lambda i: (0, 0)),
            pl.BlockSpec((d, nw), lambda i: (0, 0)),
            pl.BlockSpec((1, A_QD), lambda i: (0, 0)),
            pl.BlockSpec((1, A_QD), lambda i: (0, 0)),
        ],
        out_specs=[
            pl.BlockSpec((tm, 3 * A_QD), lambda i: (i, 0)),
            pl.BlockSpec((tm // LANES, A_HEADS * A_VD, LANES), lambda i: (i, 0, 0)),
        ],
        out_shape=[
            jax.ShapeDtypeStruct((n, 3 * A_QD), bf16),
            jax.ShapeDtypeStruct((n // LANES, A_HEADS * A_VD, LANES), bf16),
        ],
        compiler_params=_cparams(("arbitrary",)),
        name="proj_a",
    )(h, g.reshape(1, d), w, gq, gk)


def _attn_a_kernel(qlo_ref, qhi_ref, k_ref, vt_ref, km_ref, vtm_ref, kmf_ref, vtmf_ref, bias_ref, lam_ref, sub_ref,
                   o_ref, qs_ref, m_ref, l_ref, acc_ref, *, nblk, lambda_init):
    s_id = pl.program_id(0)
    nb_real = pl.num_programs(0) - 1
    hw = 2 * A_HD

    def init():
        for h in range(A_HEADS):
            qs_ref[h, :LANES, :] = qlo_ref[:, h * hw:(h + 1) * hw]
            qs_ref[h, LANES:, :] = qhi_ref[:, h * hw:(h + 1) * hw]
        m_ref[...] = jnp.full(m_ref.shape, NEG_INF, f32)
        l_ref[...] = jnp.zeros(l_ref.shape, f32)
        acc_ref[...] = jnp.zeros(acc_ref.shape, f32)

    def update(kt_fn, vtt_fn, bias_fn):
        st = jnp.concatenate([_dot_nt(kt_fn(h), qs_ref[h]) + bias_fn(h) for h in range(A_HEADS)], axis=1)
        m_old = m_ref[...]
        m_new = jnp.maximum(m_old, jnp.max(st, axis=0, keepdims=True))
        alpha = jnp.exp(m_old - m_new)
        p = jnp.exp(st - m_new)
        l_ref[...] = alpha * l_ref[...] + jnp.sum(p, axis=0, keepdims=True)
        pb = p.astype(bf16)
        pv = jnp.concatenate([_dot(vtt_fn(h), pb[:, h * 2 * LANES:(h + 1) * 2 * LANES]) for h in range(A_HEADS)],
                             axis=1)
        acc_ref[...] = alpha * acc_ref[...] + pv
        m_ref[...] = m_new

    def bias_rows(kind, h, rows):
        return jnp.concatenate([bias_ref[kind, h, 0:rows, :], bias_ref[kind, A_HEADS + h, 0:rows, :]], axis=1)

    def finalize():
        lam = lam_ref[...]
        lam_full = (jnp.exp(jnp.sum(lam[0:1] * lam[1:2], axis=-1, keepdims=True))
                    - jnp.exp(jnp.sum(lam[2:3] * lam[3:4], axis=-1, keepdims=True)) + lambda_init)
        o = acc_ref[...] / l_ref[...]
        for h in range(A_HEADS):
            d = o[:, 2 * h * LANES:(2 * h + 1) * LANES] - lam_full * o[:, (2 * h + 1) * LANES:(2 * h + 2) * LANES]
            d = d * lax.rsqrt(jnp.mean(d * d, axis=0, keepdims=True) + EPS) * sub_ref[...] * (1.0 - lambda_init)
            o_ref[:, h * hw:(h + 1) * hw] = d.T.astype(bf16)

    @pl.when(s_id < nb_real)
    def _():
        i = s_id % nblk
        init()
        kind_m = jnp.where(i == 0, KIND_META0, KIND_FAR)
        update(lambda h: km_ref[0, :, h * hw:(h + 1) * hw], lambda h: vtm_ref[0, h * hw:(h + 1) * hw, :],
               lambda h: bias_rows(kind_m, h, N_META))

        def body(w, carry):
            off = pl.multiple_of(w * (NSUB * LANES), NSUB * LANES)
            kinds = [_block_kind(w * NSUB + t - i) for t in range(NSUB)]
            update(lambda h: k_ref[pl.ds(off, NSUB * LANES), h * hw:(h + 1) * hw],
                   lambda h: jnp.concatenate([vt_ref[w * NSUB + t, h * hw:(h + 1) * hw, :] for t in range(NSUB)],
                                             axis=1),
                   lambda h: jnp.concatenate([bias_rows(kinds[t], h, LANES) for t in range(NSUB)], axis=0))
            return carry

        lax.fori_loop(0, i // NSUB + 1, body, 0)
        finalize()

    @pl.when(s_id == nb_real)
    def _():
        init()
        update(lambda h: kmf_ref[:, h * hw:(h + 1) * hw], lambda h: vtmf_ref[0, h * hw:(h + 1) * hw, :],
               lambda h: bias_rows(KIND_METAMETA, h, LANES))
        finalize()


def _meta_views(rows, cols_t, bsz):
    f = rows.shape[1]
    return (rows.reshape(bsz, N_META, f),
            cols_t.reshape(cols_t.shape[0], bsz, N_META).transpose(1, 0, 2))


def _attn_a(qkv, vt, bias, lam, subln, *, bsz, nblk, lambda_init):
    n = qkv.shape[0]
    nb_real = bsz * nblk
    seq = nblk * LANES
    d = A_QD
    assert nblk % NSUB == 0
    km, vtm = _meta_views(qkv[nb_real * LANES:, 2 * d:], vt[nb_real], bsz)
    kern = functools.partial(_attn_a_kernel, nblk=nblk, lambda_init=lambda_init)
    bclamp = lambda s: jnp.minimum(s // nblk, bsz - 1)
    return pl.pallas_call(
        kern,
        grid=(nb_real + 1,),
        in_specs=[
            pl.BlockSpec((LANES, d), lambda s: (s, 0)),
            pl.BlockSpec((LANES, d), lambda s: (s, 1)),
            pl.BlockSpec((seq, d), lambda s: (bclamp(s), 2)),
            pl.BlockSpec((nblk, d, LANES), lambda s: (bclamp(s), 0, 0)),
            pl.BlockSpec((1, N_META, d), lambda s: (bclamp(s), 0, 0)),
            pl.BlockSpec((1, d, N_META), lambda s: (bclamp(s), 0, 0)),
            pl.BlockSpec((LANES, d), lambda s: (nb_real, 2)),
            pl.BlockSpec((1, d, LANES), lambda s: (nb_real, 0, 0)),
            _bias_spec(1),
            pl.BlockSpec((4, A_HD), lambda s: (0, 0)),
            pl.BlockSpec((A_VD, LANES), lambda s: (0, 0)),
        ],
        out_specs=pl.BlockSpec((LANES, d), lambda s: (s, 0)),
        out_shape=jax.ShapeDtypeStruct((n, d), bf16),
        scratch_shapes=[
            pltpu.VMEM((A_HEADS, 2 * LANES, A_VD), bf16),
            pltpu.VMEM((1, A_HEADS * 2 * LANES), f32),
            pltpu.VMEM((1, A_HEADS * 2 * LANES), f32),
            pltpu.VMEM((A_VD, A_HEADS * 2 * LANES), f32),
        ],
        compiler_params=_cparams(("arbitrary",)),
        name="attn_a",
    )(qkv, qkv, qkv, vt, km, vtm, qkv, vt, bias, lam, jnp.broadcast_to(subln[:, None], (A_VD, LANES)))


C_QD = C_Q_HEADS * C_HD
C_KD = 2 * C_KV_HEADS * C_HD


def _proj_c_kernel(h_ref, g_ref, w_ref, gq_ref, gk_ref, o_ref):
    xn = (_rms_rows(h_ref[...]) * g_ref[...]).astype(bf16)
    y = _dot(xn, w_ref[...])
    q = _group_rms(y[:, :C_QD], C_HD) * gq_ref[...] * (C_HD ** -0.5)
    lo = _lo_half_mask(q.shape, 2 * C_HD, C_HD)
    o_ref[:, :C_QD] = jnp.where(lo, q, 0.0).astype(bf16)
    o_ref[:, C_QD:2 * C_QD] = jnp.where(lo, 0.0, q).astype(bf16)
    o_ref[:, 2 * C_QD:2 * C_QD + C_KD] = (_group_rms(y[:, C_QD:C_QD + C_KD], C_HD) * gk_ref[...]).astype(bf16)
    o_ref[:, 2 * C_QD + C_KD:] = y[:, C_QD + C_KD:].astype(bf16)


def _proj_c(h, g, w, qk_norm, *, tm):
    n, d = h.shape
    nw = w.shape[1]
    nout = nw + C_QD
    gq = jnp.tile(qk_norm[0], C_QD // C_HD).reshape(1, C_QD)
    gk = jnp.tile(qk_norm[1], C_KD // C_HD).reshape(1, C_KD)
    return pl.pallas_call(
        _proj_c_kernel,
        grid=(n // tm,),
        in_specs=[
            pl.BlockSpec((tm, d), lambda i: (i, 0)),
            pl.BlockSpec((1, d), lambda i: (0, 0)),
            pl.BlockSpec((d, nw), lambda i: (0, 0)),
            pl.BlockSpec((1, C_QD), lambda i: (0, 0)),
            pl.BlockSpec((1, C_KD), lambda i: (0, 0)),
        ],
        out_specs=pl.BlockSpec((tm, nout), lambda i: (i, 0)),
        out_shape=jax.ShapeDtypeStruct((n, nout), bf16),
        compiler_params=_cparams(("arbitrary",)),
        name="proj_c",
    )(h, g.reshape(1, d), w, gq, gk)


def _attn_c_kernel(sink_ref, qlo_ref, qhi_ref, k_ref, v_ref, km_ref, vm_ref, bias_ref, o_ref, qs_ref, *, nblk):
    s_id = pl.program_id(0)
    nb_real = pl.num_programs(0) - 1
    r_io, c_io = _tile_iotas()
    rows = C_GROUP * LANES

    def attend(tiles):
        lo_t = c_io < C_HD
        for g in range(C_KV_HEADS):
            for hh in range(C_GROUP):
                cc = (g * C_GROUP + hh) // 2
                src = qlo_ref if hh % 2 == 0 else qhi_ref
                qs_ref[hh * LANES:(hh + 1) * LANES, :] = src[:, cc * LANES:(cc + 1) * LANES]
            sink = jnp.concatenate(
                [jnp.full((LANES, 1), sink_ref[g * C_GROUP + hh], f32) for hh in range(C_GROUP)], axis=0)
            ss = []
            m = sink
            for (kt, vt, kind, mask) in tiles:
                s = _dot_nt(qs_ref[...], kt[:, g * LANES:(g + 1) * LANES])
                s = s + jnp.concatenate(
                    [jnp.where(mask, bias_ref[kind, g * C_GROUP + hh], NEG_INF) for hh in range(C_GROUP)], axis=0)
                m = jnp.maximum(m, jnp.max(s, axis=-1, keepdims=True))
                ss.append(s)
            l = jnp.exp(sink - m)
            o = jnp.zeros((rows, LANES), f32)
            for s, (kt, vt, kind, mask) in zip(ss, tiles):
                p = jnp.exp(s - m)
                l = l + jnp.sum(p, axis=-1, keepdims=True)
                o = o + _dot(p.astype(bf16), vt[:, g * LANES:(g + 1) * LANES])
            o = o / l
            for cc in range(C_GROUP // 2):
                even = o[(2 * cc) * LANES:(2 * cc + 1) * LANES]
                odd = o[(2 * cc + 1) * LANES:(2 * cc + 2) * LANES]
                col = (g * (C_GROUP // 2) + cc) * LANES
                o_ref[:, col:col + LANES] = jnp.where(lo_t, even, odd).astype(bf16)

    @pl.when(s_id < nb_real)
    def _():
        b = s_id // nblk
        i = s_id % nblk
        first_half = r_io < CHUNK
        poff = pl.multiple_of(jnp.maximum(i - 1, 0) * LANES, LANES)
        coff = pl.multiple_of(i * LANES, LANES)
        attend([
            (km_ref[...], vm_ref[...], jnp.where(i == 0, KIND_META0, KIND_FAR), (c_io >> 4) == b),
            (k_ref[pl.ds(poff, LANES), :], v_ref[pl.ds(poff, LANES), :], KIND_PREV,
             (_vec(i) > 0) & (first_half | (c_io >= CHUNK))),
            (k_ref[pl.ds(coff, LANES), :], v_ref[pl.ds(coff, LANES), :], KIND_DIAG,
             (c_io < CHUNK) | (r_io >= CHUNK)),
        ])

    @pl.when(s_id == nb_real)
    def _():
        attend([(km_ref[...], vm_ref[...], KIND_METAMETA, (c_io >> 4) == (r_io >> 4))])


def _attn_c(qkv, bias, sinks, *, bsz, nblk):
    n = qkv.shape[0]
    nb_real = bsz * nblk
    seq = nblk * LANES
    kern = functools.partial(_attn_c_kernel, nblk=nblk)
    bclamp = lambda s: jnp.minimum(s // nblk, bsz - 1)
    kcol = 2 * C_QD // C_KD
    return pl.pallas_call(
        kern,
        grid=(nb_real + 1,),
        in_specs=[
            pl.BlockSpec(memory_space=pltpu.SMEM),
            pl.BlockSpec((LANES, C_QD), lambda s: (s, 0)),
            pl.BlockSpec((LANES, C_QD), lambda s: (s, 1)),
            pl.BlockSpec((seq, C_KD), lambda s: (bclamp(s), kcol)),
            pl.BlockSpec((seq, C_KD), lambda s: (bclamp(s), kcol + 1)),
            pl.BlockSpec((LANES, C_KD), lambda s: (nb_real, kcol)),
            pl.BlockSpec((LANES, C_KD), lambda s: (nb_real, kcol + 1)),
            _bias_spec(0),
        ],
        out_specs=pl.BlockSpec((LANES, C_QD), lambda s: (s, 0)),
        out_shape=jax.ShapeDtypeStruct((n, C_QD), bf16),
        scratch_shapes=[pltpu.VMEM((C_GROUP * LANES, LANES), bf16)],
        compiler_params=_cparams(("arbitrary",)),
        name="attn_c",
    )(sinks, qkv, qkv, qkv, qkv, qkv, qkv, bias)


B_QA = B_HEADS * B_KV_RANK
B_QI = IDX_HEADS * IDX_DIM
B_W1 = 2 * B_Q_RANK + 2 * LANES


def _proj_b_kernel(h_ref, g_ref, w1_ref, ln_ref, wuq_ref, qn_ref,
                   qa_ref, qi_ref, ckv_ref, ckvt_ref, kk_ref, wit_ref):
    xn = (_rms_rows(h_ref[...]) * g_ref[...]).astype(bf16)
    y = _dot(xn, w1_ref[...])
    r = B_Q_RANK
    cq = (_rms_rows(y[:, :r]) * ln_ref[0:1, :]).astype(bf16)
    ckv = _rms_rows(y[:, r:2 * r]) * ln_ref[1:2, :]
    ckv_ref[...] = ckv.astype(bf16)
    kk_ref[...] = _rms_rows(y[:, 2 * r:2 * r + LANES]).astype(bf16)
    wi = y[:, 2 * r + LANES:] * (IDX_HEADS ** -0.5)
    for t in range(ckvt_ref.shape[0]):
        ckvt_ref[t] = ckv[t * LANES:(t + 1) * LANES, :].T.astype(bf16)
        wit_ref[t] = wi[t * LANES:(t + 1) * LANES, :].T[0:IDX_HEADS, :]
    z = _dot(cq, wuq_ref[...])
    qa_ref[...] = (_group_rms(z[:, :B_QA], B_KV_RANK) * qn_ref[...] * (B_KV_RANK ** -0.5)).astype(bf16)
    qi = z[:, B_QA:] * (IDX_DIM ** -0.5)
    lo = _lo_half_mask(qi.shape, 2 * IDX_DIM, IDX_DIM)
    qi_ref[:, :B_QI] = jnp.where(lo, qi, 0.0).astype(bf16)
    qi_ref[:, B_QI:] = jnp.where(lo, 0.0, qi).astype(bf16)


def _proj_b(h, g, w1, latent_norm, wuq, q_norm, *, tm):
    n, d = h.shape
    qn = jnp.tile(q_norm, B_HEADS).reshape(1, B_QA)
    row = lambda i: (i, 0)
    row3 = lambda i: (i, 0, 0)
    const = lambda i: (0, 0)
    nt = tm // LANES
    return pl.pallas_call(
        _proj_b_kernel,
        grid=(n // tm,),
        in_specs=[
            pl.BlockSpec((tm, d), row),
            pl.BlockSpec((1, d), const),
            pl.BlockSpec(w1.shape, const),
            pl.BlockSpec(latent_norm.shape, const),
            pl.BlockSpec(wuq.shape, const),
            pl.BlockSpec((1, B_QA), const),
        ],
        out_specs=[
            pl.BlockSpec((tm, B_QA), row),
            pl.BlockSpec((tm, 2 * B_QI), row),
            pl.BlockSpec((tm, B_KV_RANK), row),
            pl.BlockSpec((nt, B_KV_RANK, LANES), row3),
            pl.BlockSpec((tm, LANES), row),
            pl.BlockSpec((nt, IDX_HEADS, LANES), row3),
        ],
        out_shape=[
            jax.ShapeDtypeStruct((n, B_QA), bf16),
            jax.ShapeDtypeStruct((n, 2 * B_QI), bf16),
            jax.ShapeDtypeStruct((n, B_KV_RANK), bf16),
            jax.ShapeDtypeStruct((n // LANES, B_KV_RANK, LANES), bf16),
            jax.ShapeDtypeStruct((n, LANES), bf16),
            jax.ShapeDtypeStruct((n // LANES, IDX_HEADS, LANES), f32),
        ],
        compiler_params=_cparams(("arbitrary",)),
        name="proj_b",
    )(h, g.reshape(1, d), w1, latent_norm, wuq, qn)


def _attn_b_kernel(qa_ref, qi_ref, wit_ref, ckv_ref, ckvt_ref, kk_ref, ckvm_ref, ckvtm_ref, kkm_ref,
                   ckvmf_ref, ckvtmf_ref, bias_ref, wuvt_ref,
                   o_ref, qs_ref, is_ref, key_ref, m_ref, l_ref, acc_ref, *, nblk, k_sel):
    s_id = pl.program_id(0)
    nb_real = pl.num_programs(0) - 1
    r_io, c_io = _tile_iotas()
    rk = B_KV_RANK

    def init():
        for h in range(B_HEADS):
            qs_ref[h * LANES:(h + 1) * LANES, :] = qa_ref[:, h * rk:(h + 1) * rk]
        m_ref[...] = jnp.full(m_ref.shape, NEG_INF, f32)
        l_ref[...] = jnp.zeros(l_ref.shape, f32)
        acc_ref[...] = jnp.zeros(acc_ref.shape, f32)

    def update(ckv, ckvt, bias_fn, sel):
        st = _dot_nt(ckv, qs_ref[...])
        st = jnp.concatenate(
            [jnp.where(sel, st[:, h * LANES:(h + 1) * LANES] + bias_fn(h), NEG_INF) for h in range(B_HEADS)], axis=1)
        m_old = m_ref[...]
        m_new = jnp.maximum(m_old, jnp.max(st, axis=0, keepdims=True))
        alpha = jnp.exp(m_old - m_new)
        p = jnp.exp(st - m_new)
        l_ref[...] = alpha * l_ref[...] + jnp.sum(p, axis=0, keepdims=True)
        acc_ref[...] = alpha * acc_ref[...] + _dot(ckvt, p.astype(bf16))
        m_ref[...] = m_new

    def finalize():
        olat = (acc_ref[...] / l_ref[...]).astype(bf16)
        ot = jnp.concatenate([_dot(wuvt_ref[h], olat[:, h * LANES:(h + 1) * LANES]) for h in range(B_HEADS)], axis=0)
        o_ref[...] = ot.T.astype(bf16)

    def index_scores(kk):
        s = jnp.maximum(_dot_nt(kk, is_ref[...]), 0.0)
        wt = wit_ref[0]
        sc = jnp.zeros((kk.shape[0], LANES), f32)
        for hh in range(IDX_HEADS):
            sc = sc + wt[hh:hh + 1, :] * s[:, hh * LANES:(hh + 1) * LANES]
        return sc

    def sort_key(sc):
        bits = lax.bitcast_convert_type(sc + 0.0, jnp.int32)
        return jnp.where(bits < 0, bits ^ jnp.int32(0x7FFFFFFF), bits)

    @pl.when(s_id < nb_real)
    def _():
        i = s_id % nblk
        ntile = i + 2
        init()
        for hh in range(IDX_HEADS):
            base = (hh % 2) * B_QI + (hh // 2) * LANES
            is_ref[hh * LANES:(hh + 1) * LANES, :] = qi_ref[:, base:base + LANES]

        int_min_tile = jnp.full((LANES, LANES), INT_MIN, jnp.int32)
        key_ref[0] = int_min_tile
        key_ref[0, 0:N_META, :] = sort_key(index_scores(kkm_ref[0]))
        for t in range(1, NSUB):
            key_ref[i + 1 + t] = int_min_tile

        def score_body(j, carry):
            off = pl.multiple_of(j * LANES, LANES)
            vis = (_vec(j) < i) | ((r_io >> 6) <= (c_io >> 6))
            key_ref[j + 1] = jnp.where(vis, sort_key(index_scores(kk_ref[pl.ds(off, LANES), :])), jnp.int32(INT_MIN))
            return carry

        lax.fori_loop(0, i + 1, score_body, 0)

        def count(pred):
            def cbody(t, accv):
                return accv + jnp.where(pred(key_ref[t], t), 1.0, 0.0)
            accv = lax.fori_loop(0, ntile, cbody, jnp.zeros((LANES, LANES), f32))
            return jnp.sum(accv, axis=0, keepdims=True)

        kf = float(k_sel)
        zero = jnp.zeros((1, LANES), jnp.int32)
        t0 = jnp.where(count(lambda k, t: k >= zero) >= kf, zero, jnp.int32(INT_MIN))

        def bit_body(it, tcur):
            cand = tcur | jnp.left_shift(jnp.int32(1), 30 - it)
            return jnp.where(count(lambda k, t: k >= cand) >= kf, cand, tcur)

        thr = lax.fori_loop(0, 31, bit_body, t0)

        need = kf - count(lambda k, t: k > thr)
        n_eq = count(lambda k, t: k == thr)
        has_thr = thr > jnp.int32(INT_MIN)
        tied = jnp.max(jnp.where(has_thr & (n_eq > need), 1.0, 0.0)) > 0.0

        def tie_search(_):
            def jbody(it, jcur):
                cand = jcur | jnp.left_shift(jnp.int32(1), 11 - it)
                cnt = count(lambda k, t: (k == thr) & ((t * LANES + r_io) < cand))
                return jnp.where(cnt < need, cand, jcur)
            return lax.fori_loop(0, 12, jbody, jnp.zeros((1, LANES), jnp.int32))

        j_last = lax.cond(tied, tie_search, lambda _: jnp.full((1, LANES), 4095, jnp.int32), 0)
        j_last = jnp.where(has_thr, j_last, -1)

        def selected(t, rows=LANES):
            k = key_ref[t, 0:rows, :]
            return (k > thr) | ((k == thr) & ((t * LANES + r_io[0:rows, :]) <= j_last))

        kind_m = jnp.where(i == 0, KIND_META0, KIND_FAR)
        update(ckvm_ref[0], ckvtm_ref[0], lambda h: bias_ref[kind_m, h, 0:N_META, :], selected(0, N_META))

        def att_body(w, carry):
            off = pl.multiple_of(w * (NSUB * LANES), NSUB * LANES)
            kinds = [_block_kind(w * NSUB + t - i) for t in range(NSUB)]
            sel = jnp.concatenate([selected(w * NSUB + t + 1) for t in range(NSUB)], axis=0)
            ckvt = jnp.concatenate([ckvt_ref[w * NSUB + t] for t in range(NSUB)], axis=1)
            update(ckv_ref[pl.ds(off, NSUB * LANES), :], ckvt,
                   lambda h: jnp.concatenate([bias_ref[kinds[t], h] for t in range(NSUB)], axis=0), sel)
            return carry

        lax.fori_loop(0, i // NSUB + 1, att_body, 0)
        finalize()

    @pl.when(s_id == nb_real)
    def _():
        init()
        update(ckvmf_ref[...], ckvtmf_ref[0], lambda h: bias_ref[KIND_METAMETA, h], (r_io >> 4) == (c_io >> 4))
        finalize()


def _attn_b(qa, qi, wit, ckv, ckvt, kk, bias, wuvt, *, bsz, nblk, k_sel):
    n = qa.shape[0]
    nb_real = bsz * nblk
    seq = nblk * LANES
    assert k_sel >= N_META and (nblk + 1) * LANES <= 4096 and nblk % NSUB == 0
    ckvm, ckvtm = _meta_views(ckv[nb_real * LANES:], ckvt[nb_real], bsz)
    kkm = kk[nb_real * LANES:].reshape(bsz, N_META, LANES)
    kern = functools.partial(_attn_b_kernel, nblk=nblk, k_sel=k_sel)
    bidx = lambda s: jnp.minimum(s // nblk, bsz - 1)
    blk = lambda s: (s, 0)
    return pl.pallas_call(
        kern,
        grid=(nb_real + 1,),
        in_specs=[
            pl.BlockSpec((LANES, B_QA), blk),
            pl.BlockSpec((LANES, 2 * B_QI), blk),
            pl.BlockSpec((1, IDX_HEADS, LANES), lambda s: (s, 0, 0)),
            pl.BlockSpec((seq, B_KV_RANK), lambda s: (bidx(s), 0)),
            pl.BlockSpec((nblk, B_KV_RANK, LANES), lambda s: (bidx(s), 0, 0)),
            pl.BlockSpec((seq, LANES), lambda s: (bidx(s), 0)),
            pl.BlockSpec((1, N_META, B_KV_RANK), lambda s: (bidx(s), 0, 0)),
            pl.BlockSpec((1, B_KV_RANK, N_META), lambda s: (bidx(s), 0, 0)),
            pl.BlockSpec((1, N_META, LANES), lambda s: (bidx(s), 0, 0)),
            pl.BlockSpec((LANES, B_KV_RANK), lambda s: (nb_real, 0)),
            pl.BlockSpec((1, B_KV_RANK, LANES), lambda s: (nb_real, 0, 0)),
            _bias_spec(1),
            pl.BlockSpec(wuvt.shape, lambda s: (0, 0, 0)),
        ],
        out_specs=pl.BlockSpec((LANES, B_HEADS * B_VD), blk),
        out_shape=jax.ShapeDtypeStruct((n, B_HEADS * B_VD), bf16),
        scratch_shapes=[
            pltpu.VMEM((B_HEADS * LANES, B_KV_RANK), bf16),
            pltpu.VMEM((IDX_HEADS * LANES, LANES), bf16),
            pltpu.VMEM((nblk + NSUB, LANES, LANES), jnp.int32),
            pltpu.VMEM((1, B_HEADS * LANES), f32),
            pltpu.VMEM((1, B_HEADS * LANES), f32),
            pltpu.VMEM((B_KV_RANK, B_HEADS * LANES), f32),
        ],
        compiler_params=_cparams(("arbitrary",)),
        name="attn_b",
    )(qa, qi, wit, ckv, ckvt, kk, ckvm, ckvtm, kkm, ckv, ckvt, bias, wuvt)


def kernel(x, meta_tokens, rel_bias, ln_ffn1, ffn1_wi, ffn1_wo, ln_mix, w_out, ln_ffn2, ffn2_wi, ffn2_wo, a_w_in, a_qk_norm, a_lambda, a_subln, b_w_in, b_latent_norm, b_w_uq, b_q_norm, b_w_uv, c_w_in, c_qk_norm, c_sinks):
    bsz, seq, d = x.shape
    assert d == D_MODEL and seq % LANES == 0 and bsz * N_META == LANES
    nblk = seq // LANES
    n = bsz * seq + LANES
    k_sel = min(TOPK_MAX, seq // 4)
    tm_ffn = _row_tile(n, 1408)
    tm_proj = _row_tile(n, 384, LANES)
    fc = 256

    h = jnp.concatenate([x.reshape(bsz * seq, d),
                         jnp.broadcast_to(meta_tokens.astype(x.dtype), (bsz, N_META, d)).reshape(LANES, d)], axis=0)
    bias = _bias_tiles(rel_bias)

    for layer in range(DEPTH):
        h = _ffn(h, ln_ffn1[layer], ffn1_wi[layer].astype(bf16), ffn1_wo[layer].astype(bf16), tm=tm_ffn, fc=fc)
        kind, j = layer % N_MIXERS, layer // N_MIXERS
        g = ln_mix[layer]
        if kind == 0:
            lambda_init = 0.8 - 0.6 * math.exp(-0.3 * layer)
            qkv, vt = _proj_a(h, g, a_w_in[j].astype(bf16), a_qk_norm[j], tm=tm_proj)
            mix = _attn_a(qkv, vt, bias, a_lambda[j], a_subln[j], bsz=bsz, nblk=nblk, lambda_init=lambda_init)
        elif kind == 1:
            w = b_w_in[j]
            r2 = B_Q_RANK + B_KV_RANK
            kcol = w[:, r2:r2 + IDX_DIM]
            w1 = jnp.concatenate([w[:, :r2], kcol, kcol, w[:, r2 + IDX_DIM:],
                                  jnp.zeros((d, LANES - IDX_HEADS), w.dtype)], axis=1).astype(bf16)
            assert w1.shape[1] == B_W1
            qa, qi, ckv, ckvt, kk, wit = _proj_b(h, g, w1, b_latent_norm[j], b_w_uq[j].astype(bf16), b_q_norm[j],
                                                 tm=tm_proj)
            wuvt = jnp.swapaxes(b_w_uv[j], 1, 2).astype(bf16)
            mix = _attn_b(qa, qi, wit, ckv, ckvt, kk, bias, wuvt, bsz=bsz, nblk=nblk, k_sel=k_sel)
        else:
            w = c_w_in[j]
            kcols = [w[:, C_QD + gi * C_HD:C_QD + (gi + 1) * C_HD] for gi in range(C_KV_HEADS)]
            voff = C_QD + C_KV_HEADS * C_HD
            vcols = [w[:, voff + gi * C_HD:voff + (gi + 1) * C_HD] for gi in range(C_KV_HEADS)]
            wc = jnp.concatenate([w[:, :C_QD]] + [kc for kc in kcols for _ in range(2)]
                                 + [vc for vc in vcols for _ in range(2)], axis=1).astype(bf16)
            qkv = _proj_c(h, g, wc, c_qk_norm[j], tm=tm_proj)
            mix = _attn_c(qkv, bias, c_sinks[j], bsz=bsz, nblk=nblk)
        h = _outproj(h, mix, w_out[layer].astype(bf16), tm=tm_ffn)
        h = _ffn(h, ln_ffn2[layer], ffn2_wi[layer].astype(bf16), ffn2_wo[layer].astype(bf16), tm=tm_ffn, fc=fc)
    return h[:bsz * seq].reshape(bsz, seq, d)
```

```python
import functools
import math

import numpy as np
import jax
import jax.numpy as jnp
from jax import lax
from jax.experimental import pallas as pl
from jax.experimental.pallas import tpu as pltpu

D_MODEL = 1024
DEPTH = 4
CHUNK = 64
N_META = 16
N_MIXERS = 3
NEG_INF = -1e30
REL_BUCKETS = 32
REL_MAX_DIST = 128
REL_HEADS = 16
D_FF = 2816
A_HEADS = 8
A_HD = 64
A_VD = 2 * A_HD
B_HEADS = 16
B_Q_RANK = 256
B_KV_RANK = 256
B_VD = 64
IDX_HEADS = 8
IDX_DIM = 64
TOPK_MAX = 256
C_Q_HEADS = 16
C_KV_HEADS = 2
C_GROUP = C_Q_HEADS // C_KV_HEADS
C_HD = 64
EPS = 1e-6

LANES = 128
BF16_ROWS = 16
VMEM_LIMIT = 56 * 1024 * 1024
INT_MIN = -(2 ** 31)
NSUB = 2
LOG2E = math.log2(math.e)
LAZY_LIMIT = 2.0 ** 60
ONES_ROWS = BF16_ROWS

KIND_DIAG, KIND_PREV, KIND_FAR, KIND_META0, KIND_METAMETA, KIND_MASKED = 0, 1, 2, 3, 4, 5
N_KINDS = 6

f32 = jnp.float32
bf16 = jnp.bfloat16


def _cparams(sem):
    return pltpu.CompilerParams(dimension_semantics=sem, vmem_limit_bytes=VMEM_LIMIT)


def _row_tile(n, cap, mult=BF16_ROWS):
    best = None
    for t in range(mult, cap + 1, mult):
        if n % t == 0:
            best = t
    assert best is not None
    return best


def _dot(a, b):
    return jnp.dot(a, b, preferred_element_type=f32)


def _dot_nt(a, b):
    return lax.dot_general(a, b, (((1,), (1,)), ((), ())), preferred_element_type=f32)


def _rms_rows(x):
    return x * lax.rsqrt(jnp.mean(x * x, axis=-1, keepdims=True) + EPS)


def _lo_half_mask(shape, period, half):
    return (lax.broadcasted_iota(jnp.int32, shape, 1) & (period - 1)) < half


def _group_rms(x, group):
    r, c = x.shape
    outs = []
    if group == 64:
        lo = _lo_half_mask((r, LANES), LANES, 64)
        for ci in range(c // LANES):
            xc = x[:, ci * LANES:(ci + 1) * LANES]
            x2 = xc * xc
            s_lo = jnp.sum(jnp.where(lo, x2, 0.0), axis=-1, keepdims=True)
            s_hi = jnp.sum(jnp.where(lo, 0.0, x2), axis=-1, keepdims=True)
            inv = jnp.where(lo, lax.rsqrt(s_lo * (1.0 / 64) + EPS), lax.rsqrt(s_hi * (1.0 / 64) + EPS))
            outs.append(xc * inv)
    else:
        for gi in range(c // group):
            outs.append(_rms_rows(x[:, gi * group:(gi + 1) * group]))
    return outs[0] if len(outs) == 1 else jnp.concatenate(outs, axis=-1)


def _tile_iotas():
    r = lax.broadcasted_iota(jnp.int32, (LANES, LANES), 0)
    c = lax.broadcasted_iota(jnp.int32, (LANES, LANES), 1)
    return r, c


def _vec(s):
    return jnp.full((LANES, LANES), s, jnp.int32)


def _softmax_step(logits_fn, pv_fn, m_ref, acc_ref, nd, lazy, offset=None):
    def rescale():
        st = logits_fn()
        if offset is not None:
            st = st + offset
        m_old = m_ref[...]
        m_new = jnp.maximum(m_old, jnp.max(st, axis=0, keepdims=True))
        acc_ref[...] = jnp.exp2(m_old - m_new) * acc_ref[...] + pv_fn(jnp.exp2(st - m_new).astype(bf16))
        m_ref[...] = m_new

    if not lazy:
        rescale()
        return
    shift = m_ref[...] if offset is None else m_ref[...] - offset
    pv = pv_fn(jnp.exp2(logits_fn() - shift).astype(bf16))
    within = jnp.max(pv[nd:nd + 1, :]) <= LAZY_LIMIT

    def commit():
        acc_ref[...] += pv

    lax.cond(within, commit, rescale)


def _block_kind(rel):
    return jnp.where(rel < -1, KIND_FAR,
                     jnp.where(rel == -1, KIND_PREV, jnp.where(rel == 0, KIND_DIAG, KIND_MASKED)))


def _far_steps(i):
    return jnp.maximum(i - 1, 0) // NSUB


def _ffn_kernel(*refs, fuse_out):
    if fuse_out:
        h_ref, mix_ref, wout_ref, g_ref, wa_ref, wb_ref, wo_ref, o_ref, xn_ref = refs
    else:
        h_ref, g_ref, wa_ref, wb_ref, wo_ref, o_ref, xn_ref = refs
    j = pl.program_id(1)

    @pl.when(j == 0)
    def _():
        r = h_ref[...]
        if fuse_out:
            r = r + _dot(mix_ref[...], wout_ref[...])
        o_ref[...] = r
        xn_ref[...] = (_rms_rows(r) * g_ref[...]).astype(bf16)

    xn = xn_ref[...]
    a = _dot(xn, wa_ref[...].astype(bf16))
    b = _dot(xn, wb_ref[...].astype(bf16))
    act = (a / (1.0 + jnp.exp(-a)) * b).astype(bf16)
    o_ref[...] += 0.5 * _dot(act, wo_ref[...].astype(bf16))


def _ffn(h, g, wi, wo, *, tm, fc, mix=None, wout=None):
    n, d = h.shape
    dff = wo.shape[0]
    nj = dff // fc
    fuse = mix is not None
    row = lambda i, j: (i, 0)
    in_specs = [pl.BlockSpec((tm, d), row)]
    args = [h]
    if fuse:
        in_specs += [pl.BlockSpec((tm, mix.shape[1]), row), pl.BlockSpec(wout.shape, lambda i, j: (0, 0))]
        args += [mix, wout]
    in_specs += [
        pl.BlockSpec((1, d), lambda i, j: (0, 0)),
        pl.BlockSpec((d, fc), lambda i, j: (0, j)),
        pl.BlockSpec((d, fc), lambda i, j: (0, j + nj)),
        pl.BlockSpec((fc, d), lambda i, j: (j, 0)),
    ]
    args += [g.reshape(1, d), wi, wi, wo]
    return pl.pallas_call(
        functools.partial(_ffn_kernel, fuse_out=fuse),
        grid=(n // tm, nj),
        in_specs=in_specs,
        out_specs=pl.BlockSpec((tm, d), row),
        out_shape=jax.ShapeDtypeStruct((n, d), f32),
        scratch_shapes=[pltpu.VMEM((tm, d), bf16)],
        compiler_params=_cparams(("arbitrary", "arbitrary")),
        name="ffn_out" if fuse else "ffn",
    )(*args)


def _rel_bucket(rel):
    half = REL_BUCKETS // 2
    max_exact = half // 2
    n = jnp.abs(rel)
    large = max_exact + (jnp.log(jnp.maximum(n, 1).astype(jnp.float32) / max_exact)
                         / math.log(REL_MAX_DIST / max_exact) * (half - max_exact)).astype(jnp.int32)
    large = jnp.minimum(large, half - 1)
    return jnp.where(rel > 0, half, 0) + jnp.where(n < max_exact, n, large)


def _rel_tiles():
    r = np.arange(LANES)[:, None]
    c = np.arange(LANES)[None, :]
    far = np.full((LANES, LANES), -4 * LANES)
    ones = np.ones((LANES, LANES), bool)
    rels, vis = [], []
    for q, k in ((r, c), (c, r)):
        rels += [k - q, k - q - LANES, far, (k % N_META) - N_META - q, (k % N_META) - (q % N_META), far]
        if q is r:
            vis += [ones] * N_KINDS
        else:
            vis += [(k // CHUNK) <= (q // CHUNK), ones, ones, ones, (k // N_META) == (q // N_META), ~ones]
    return (np.stack([np.broadcast_to(a, (LANES, LANES)) for a in rels]).astype(np.int32),
            np.stack([np.broadcast_to(a, (LANES, LANES)) for a in vis]).astype(np.int32))


def _bias_kernel(rb_ref, bucket_ref, vis_ref, o_ref):
    h = pl.program_id(0)
    for kind in range(2 * N_KINDS):
        bk = bucket_ref[kind]
        acc = jnp.zeros((LANES, LANES), f32)
        for b in range(REL_BUCKETS):
            acc = jnp.where(bk == b, rb_ref[b, h], acc)
        if kind >= N_KINDS:
            acc = acc * LOG2E
        o_ref[kind, 0] = jnp.where(vis_ref[kind] != 0, acc, NEG_INF)


def _bias_tiles(rel_bias):
    rel, vis = _rel_tiles()
    bucket = _rel_bucket(jnp.asarray(rel))
    nk = 2 * N_KINDS
    return pl.pallas_call(
        _bias_kernel,
        grid=(REL_HEADS,),
        in_specs=[
            pl.BlockSpec(memory_space=pltpu.SMEM),
            pl.BlockSpec((nk, LANES, LANES), lambda h: (0, 0, 0)),
            pl.BlockSpec((nk, LANES, LANES), lambda h: (0, 0, 0)),
        ],
        out_specs=pl.BlockSpec((nk, 1, LANES, LANES), lambda h: (0, h, 0, 0)),
        out_shape=jax.ShapeDtypeStruct((nk, REL_HEADS, LANES, LANES), f32),
        compiler_params=_cparams(("arbitrary",)),
        name="bias_tiles",
    )(rel_bias, bucket, jnp.asarray(vis))


def _bias_spec(which):
    return pl.BlockSpec((N_KINDS, REL_HEADS, LANES, LANES), lambda s: (which, 0, 0, 0))


def _meta_views(rows, cols_t, bsz):
    f = rows.shape[1]
    return (rows.reshape(bsz, N_META, f),
            cols_t.reshape(cols_t.shape[0], bsz, N_META).transpose(1, 0, 2))


A_QD = A_HEADS * 2 * A_HD
A_VR = A_VD + ONES_ROWS


def _proj_a_kernel(h_ref, g_ref, w_ref, gq_ref, gk_ref, o_ref, vt_ref):
    xn = (_rms_rows(h_ref[...]) * g_ref[...]).astype(bf16)
    y = _dot(xn, w_ref[...])
    q = _group_rms(y[:, :A_QD], A_HD) * gq_ref[...] * (A_HD ** -0.5 * LOG2E)
    lo = _lo_half_mask(q.shape, 2 * A_HD, A_HD)
    o_ref[:, :A_QD] = jnp.where(lo, q, 0.0).astype(bf16)
    o_ref[:, A_QD:2 * A_QD] = jnp.where(lo, 0.0, q).astype(bf16)
    o_ref[:, 2 * A_QD:] = (_group_rms(y[:, A_QD:2 * A_QD], A_HD) * gk_ref[...]).astype(bf16)
    ones = jnp.ones((ONES_ROWS, LANES), bf16)
    for t in range(vt_ref.shape[0]):
        vt = y[t * LANES:(t + 1) * LANES, 2 * A_QD:].T.astype(bf16)
        for h in range(A_HEADS):
            vt_ref[t, h * A_VR:h * A_VR + A_VD, :] = vt[h * A_VD:(h + 1) * A_VD]
            vt_ref[t, h * A_VR + A_VD:(h + 1) * A_VR, :] = ones


def _proj_a(h, g, w, qk_norm, *, tm):
    n, d = h.shape
    nw = w.shape[1]
    gq = jnp.tile(qk_norm[0], A_QD // A_HD).reshape(1, A_QD)
    gk = jnp.tile(qk_norm[1], A_QD // A_HD).reshape(1, A_QD)
    return pl.pallas_call(
        _proj_a_kernel,
        grid=(n // tm,),
        in_specs=[
            pl.BlockSpec((tm, d), lambda i: (i, 0)),
            pl.BlockSpec((1, d), lambda i: (0, 0)),
            pl.BlockSpec((d, nw), lambda i: (0, 0)),
            pl.BlockSpec((1, A_QD), lambda i: (0, 0)),
            pl.BlockSpec((1, A_QD), lambda i: (0, 0)),
        ],
        out_specs=[
            pl.BlockSpec((tm, 3 * A_QD), lambda i: (i, 0)),
            pl.BlockSpec((tm // LANES, A_HEADS * A_VR, LANES), lambda i: (i, 0, 0)),
        ],
        out_shape=[
            jax.ShapeDtypeStruct((n, 3 * A_QD), bf16),
            jax.ShapeDtypeStruct((n // LANES, A_HEADS * A_VR, LANES), bf16),
        ],
        compiler_params=_cparams(("arbitrary",)),
        name="proj_a",
    )(h, g.reshape(1, d), w, gq, gk)


def _attn_a_kernel(qlo_ref, qhi_ref, k_ref, vt_ref, km_ref, vtm_ref, kmf_ref, vtmf_ref, bias_ref, lam_ref, sub_ref,
                   o_ref, qs_ref, m_ref, acc_ref, *, nblk, lambda_init):
    s_id = pl.program_id(0)
    nb_real = pl.num_programs(0) - 1
    hw = 2 * A_HD

    def init():
        for h in range(A_HEADS):
            qs_ref[h, :LANES, :] = qlo_ref[:, h * hw:(h + 1) * hw]
            qs_ref[h, LANES:, :] = qhi_ref[:, h * hw:(h + 1) * hw]
        m_ref[...] = jnp.full(m_ref.shape, NEG_INF, f32)
        acc_ref[...] = jnp.zeros(acc_ref.shape, f32)

    def update(kt_fn, vtt_fn, bias_fn, lazy, offset=None):
        def logits():
            sts = [_dot_nt(kt_fn(h), qs_ref[h]) for h in range(A_HEADS)]
            if bias_fn is not None:
                sts = [st + bias_fn(h) for h, st in enumerate(sts)]
            return jnp.concatenate(sts, axis=1)

        def pv(pb):
            return jnp.concatenate(
                [_dot(vtt_fn(h), pb[:, h * 2 * LANES:(h + 1) * 2 * LANES]) for h in range(A_HEADS)], axis=1)

        _softmax_step(logits, pv, m_ref, acc_ref, A_VD, lazy, offset)

    def bias_rows(kind, h, rows):
        return jnp.concatenate([bias_ref[kind, h, 0:rows, :], bias_ref[kind, A_HEADS + h, 0:rows, :]], axis=1)

    def finalize():
        lam = lam_ref[...]
        lam_full = (jnp.exp(jnp.sum(lam[0:1] * lam[1:2], axis=-1, keepdims=True))
                    - jnp.exp(jnp.sum(lam[2:3] * lam[3:4], axis=-1, keepdims=True)) + lambda_init)
        o = acc_ref[0:A_VD, :] / acc_ref[A_VD:A_VD + 1, :]
        for h in range(A_HEADS):
            d = o[:, 2 * h * LANES:(2 * h + 1) * LANES] - lam_full * o[:, (2 * h + 1) * LANES:(2 * h + 2) * LANES]
            d = d * lax.rsqrt(jnp.mean(d * d, axis=0, keepdims=True) + EPS) * sub_ref[...] * (1.0 - lambda_init)
            o_ref[:, h * hw:(h + 1) * hw] = d.T.astype(bf16)

    @pl.when(s_id < nb_real)
    def _():
        i = s_id % nblk
        init()
        kind_m = jnp.where(i == 0, KIND_META0, KIND_FAR)
        update(lambda h: km_ref[0, :, h * hw:(h + 1) * hw], lambda h: vtm_ref[0, h * A_VR:(h + 1) * A_VR, :],
               lambda h: bias_rows(kind_m, h, N_META), lazy=False)

        def kv_fns(w):
            off = pl.multiple_of(w * (NSUB * LANES), NSUB * LANES)
            return (lambda h: k_ref[pl.ds(off, NSUB * LANES), h * hw:(h + 1) * hw],
                    lambda h: jnp.concatenate([vt_ref[w * NSUB + t, h * A_VR:(h + 1) * A_VR, :] for t in range(NSUB)],
                                              axis=1))

        far_bias = jnp.concatenate([bias_rows(KIND_FAR, h, 1) for h in range(A_HEADS)], axis=1)

        def far_body(w, carry):
            kt_fn, vtt_fn = kv_fns(w)
            update(kt_fn, vtt_fn, None, lazy=True, offset=far_bias)
            return carry

        def near_body(w, carry):
            kt_fn, vtt_fn = kv_fns(w)
            kinds = [_block_kind(w * NSUB + t - i) for t in range(NSUB)]
            update(kt_fn, vtt_fn,
                   lambda h: jnp.concatenate([bias_rows(kinds[t], h, LANES) for t in range(NSUB)], axis=0), lazy=True)
            return carry

        nfar = _far_steps(i)
        lax.fori_loop(0, nfar, far_body, 0)
        lax.fori_loop(nfar, i // NSUB + 1, near_body, 0)
        finalize()

    @pl.when(s_id == nb_real)
    def _():
        init()
        update(lambda h: kmf_ref[:, h * hw:(h + 1) * hw], lambda h: vtmf_ref[0, h * A_VR:(h + 1) * A_VR, :],
               lambda h: bias_rows(KIND_METAMETA, h, LANES), lazy=False)
        finalize()


def _attn_a(qkv, vt, bias, lam, subln, *, bsz, nblk, lambda_init):
    n = qkv.shape[0]
    nb_real = bsz * nblk
    seq = nblk * LANES
    d = A_QD
    vr = A_HEADS * A_VR
    assert nblk % NSUB == 0
    km, vtm = _meta_views(qkv[nb_real * LANES:, 2 * d:], vt[nb_real], bsz)
    kern = functools.partial(_attn_a_kernel, nblk=nblk, lambda_init=lambda_init)
    bclamp = lambda s: jnp.minimum(s // nblk, bsz - 1)
    return pl.pallas_call(
        kern,
        grid=(nb_real + 1,),
        in_specs=[
            pl.BlockSpec((LANES, d), lambda s: (s, 0)),
            pl.BlockSpec((LANES, d), lambda s: (s, 1)),
            pl.BlockSpec((seq, d), lambda s: (bclamp(s), 2)),
            pl.BlockSpec((nblk, vr, LANES), lambda s: (bclamp(s), 0, 0)),
            pl.BlockSpec((1, N_META, d), lambda s: (bclamp(s), 0, 0)),
            pl.BlockSpec((1, vr, N_META), lambda s: (bclamp(s), 0, 0)),
            pl.BlockSpec((LANES, d), lambda s: (nb_real, 2)),
            pl.BlockSpec((1, vr, LANES), lambda s: (nb_real, 0, 0)),
            _bias_spec(1),
            pl.BlockSpec((4, A_HD), lambda s: (0, 0)),
            pl.BlockSpec((A_VD, LANES), lambda s: (0, 0)),
        ],
        out_specs=pl.BlockSpec((LANES, d), lambda s: (s, 0)),
        out_shape=jax.ShapeDtypeStruct((n, d), bf16),
        scratch_shapes=[
            pltpu.VMEM((A_HEADS, 2 * LANES, A_VD), bf16),
            pltpu.VMEM((1, A_HEADS * 2 * LANES), f32),
            pltpu.VMEM((A_VR, A_HEADS * 2 * LANES), f32),
        ],
        compiler_params=_cparams(("arbitrary",)),
        name="attn_a",
    )(qkv, qkv, qkv, vt, km, vtm, qkv, vt, bias, lam, jnp.broadcast_to(subln[:, None], (A_VD, LANES)))


C_QD = C_Q_HEADS * C_HD
C_KD = 2 * C_KV_HEADS * C_HD


def _proj_c_kernel(h_ref, g_ref, w_ref, gq_ref, gk_ref, o_ref):
    xn = (_rms_rows(h_ref[...]) * g_ref[...]).astype(bf16)
    y = _dot(xn, w_ref[...])
    q = _group_rms(y[:, :C_QD], C_HD) * gq_ref[...] * (C_HD ** -0.5)
    lo = _lo_half_mask(q.shape, 2 * C_HD, C_HD)
    o_ref[:, :C_QD] = jnp.where(lo, q, 0.0).astype(bf16)
    o_ref[:, C_QD:2 * C_QD] = jnp.where(lo, 0.0, q).astype(bf16)
    o_ref[:, 2 * C_QD:2 * C_QD + C_KD] = (_group_rms(y[:, C_QD:C_QD + C_KD], C_HD) * gk_ref[...]).astype(bf16)
    o_ref[:, 2 * C_QD + C_KD:] = y[:, C_QD + C_KD:].astype(bf16)


def _proj_c(h, g, w, qk_norm, *, tm):
    n, d = h.shape
    nw = w.shape[1]
    nout = nw + C_QD
    gq = jnp.tile(qk_norm[0], C_QD // C_HD).reshape(1, C_QD)
    gk = jnp.tile(qk_norm[1], C_KD // C_HD).reshape(1, C_KD)
    return pl.pallas_call(
        _proj_c_kernel,
        grid=(n // tm,),
        in_specs=[
            pl.BlockSpec((tm, d), lambda i: (i, 0)),
            pl.BlockSpec((1, d), lambda i: (0, 0)),
            pl.BlockSpec((d, nw), lambda i: (0, 0)),
            pl.BlockSpec((1, C_QD), lambda i: (0, 0)),
            pl.BlockSpec((1, C_KD), lambda i: (0, 0)),
        ],
        out_specs=pl.BlockSpec((tm, nout), lambda i: (i, 0)),
        out_shape=jax.ShapeDtypeStruct((n, nout), bf16),
        compiler_params=_cparams(("arbitrary",)),
        name="proj_c",
    )(h, g.reshape(1, d), w, gq, gk)


def _attn_c_kernel(sink_ref, qlo_ref, qhi_ref, k_ref, v_ref, km_ref, vm_ref, bias_ref, o_ref, qs_ref, *, nblk):
    s_id = pl.program_id(0)
    nb_real = pl.num_programs(0) - 1
    r_io, c_io = _tile_iotas()
    rows = C_GROUP * LANES

    def attend(tiles):
        lo_t = c_io < C_HD
        for g in range(C_KV_HEADS):
            for hh in range(C_GROUP):
                cc = (g * C_GROUP + hh) // 2
                src = qlo_ref if hh % 2 == 0 else qhi_ref
                qs_ref[hh * LANES:(hh + 1) * LANES, :] = src[:, cc * LANES:(cc + 1) * LANES]
            sink = jnp.concatenate(
                [jnp.full((LANES, 1), sink_ref[g * C_GROUP + hh], f32) for hh in range(C_GROUP)], axis=0)
            ss = []
            m = sink
            for (kt, vt, kind, mask) in tiles:
                s = _dot_nt(qs_ref[...], kt[:, g * LANES:(g + 1) * LANES])
                s = s + jnp.concatenate(
                    [jnp.where(mask, bias_ref[kind, g * C_GROUP + hh], NEG_INF) for hh in range(C_GROUP)], axis=0)
                m = jnp.maximum(m, jnp.max(s, axis=-1, keepdims=True))
                ss.append(s)
            l = jnp.exp(sink - m)
            o = jnp.zeros((rows, LANES), f32)
            for s, (kt, vt, kind, mask) in zip(ss, tiles):
                p = jnp.exp(s - m)
                l = l + jnp.sum(p, axis=-1, keepdims=True)
                o = o + _dot(p.astype(bf16), vt[:, g * LANES:(g + 1) * LANES])
            o = o / l
            for cc in range(C_GROUP // 2):
                even = o[(2 * cc) * LANES:(2 * cc + 1) * LANES]
                odd = o[(2 * cc + 1) * LANES:(2 * cc + 2) * LANES]
                col = (g * (C_GROUP // 2) + cc) * LANES
                o_ref[:, col:col + LANES] = jnp.where(lo_t, even, odd).astype(bf16)

    @pl.when(s_id < nb_real)
    def _():
        b = s_id // nblk
        i = s_id % nblk
        first_half = r_io < CHUNK
        poff = pl.multiple_of(jnp.maximum(i - 1, 0) * LANES, LANES)
        coff = pl.multiple_of(i * LANES, LANES)
        attend([
            (km_ref[...], vm_ref[...], jnp.where(i == 0, KIND_META0, KIND_FAR), (c_io >> 4) == b),
            (k_ref[pl.ds(poff, LANES), :], v_ref[pl.ds(poff, LANES), :], KIND_PREV,
             (_vec(i) > 0) & (first_half | (c_io >= CHUNK))),
            (k_ref[pl.ds(coff, LANES), :], v_ref[pl.ds(coff, LANES), :], KIND_DIAG,
             (c_io < CHUNK) | (r_io >= CHUNK)),
        ])

    @pl.when(s_id == nb_real)
    def _():
        attend([(km_ref[...], vm_ref[...], KIND_METAMETA, (c_io >> 4) == (r_io >> 4))])


def _attn_c(qkv, bias, sinks, *, bsz, nblk):
    n = qkv.shape[0]
    nb_real = bsz * nblk
    seq = nblk * LANES
    kern = functools.partial(_attn_c_kernel, nblk=nblk)
    bclamp = lambda s: jnp.minimum(s // nblk, bsz - 1)
    kcol = 2 * C_QD // C_KD
    return pl.pallas_call(
        kern,
        grid=(nb_real + 1,),
        in_specs=[
            pl.BlockSpec(memory_space=pltpu.SMEM),
            pl.BlockSpec((LANES, C_QD), lambda s: (s, 0)),
            pl.BlockSpec((LANES, C_QD), lambda s: (s, 1)),
            pl.BlockSpec((seq, C_KD), lambda s: (bclamp(s), kcol)),
            pl.BlockSpec((seq, C_KD), lambda s: (bclamp(s), kcol + 1)),
            pl.BlockSpec((LANES, C_KD), lambda s: (nb_real, kcol)),
            pl.BlockSpec((LANES, C_KD), lambda s: (nb_real, kcol + 1)),
            _bias_spec(0),
        ],
        out_specs=pl.BlockSpec((LANES, C_QD), lambda s: (s, 0)),
        out_shape=jax.ShapeDtypeStruct((n, C_QD), bf16),
        scratch_shapes=[pltpu.VMEM((C_GROUP * LANES, LANES), bf16)],
        compiler_params=_cparams(("arbitrary",)),
        name="attn_c",
    )(sinks, qkv, qkv, qkv, qkv, qkv, qkv, bias)


B_QA = B_HEADS * B_KV_RANK
B_QI = IDX_HEADS * IDX_DIM
B_W1 = 2 * B_Q_RANK + 2 * LANES
B_TR = B_KV_RANK + ONES_ROWS


def _proj_b_kernel(h_ref, g_ref, w1_ref, ln_ref, wuq_ref, qn_ref,
                   qa_ref, qi_ref, ckv_ref, ckvt_ref, kk_ref, wit_ref):
    xn = (_rms_rows(h_ref[...]) * g_ref[...]).astype(bf16)
    y = _dot(xn, w1_ref[...])
    r = B_Q_RANK
    cq = (_rms_rows(y[:, :r]) * ln_ref[0:1, :]).astype(bf16)
    ckv = _rms_rows(y[:, r:2 * r]) * ln_ref[1:2, :]
    ckv_ref[...] = ckv.astype(bf16)
    kk_ref[...] = _rms_rows(y[:, 2 * r:2 * r + LANES]).astype(bf16)
    wi = y[:, 2 * r + LANES:] * (IDX_HEADS ** -0.5)
    ones = jnp.ones((ONES_ROWS, LANES), bf16)
    for t in range(ckvt_ref.shape[0]):
        ckvt_ref[t, 0:r, :] = ckv[t * LANES:(t + 1) * LANES, :].T.astype(bf16)
        ckvt_ref[t, r:, :] = ones
        wit_ref[t] = wi[t * LANES:(t + 1) * LANES, :].T[0:IDX_HEADS, :]
    z = _dot(cq, wuq_ref[...])
    qa_ref[...] = (_group_rms(z[:, :B_QA], B_KV_RANK) * qn_ref[...] * (B_KV_RANK ** -0.5 * LOG2E)).astype(bf16)
    qi = z[:, B_QA:] * (IDX_DIM ** -0.5)
    lo = _lo_half_mask(qi.shape, 2 * IDX_DIM, IDX_DIM)
    qi_ref[:, :B_QI] = jnp.where(lo, qi, 0.0).astype(bf16)
    qi_ref[:, B_QI:] = jnp.where(lo, 0.0, qi).astype(bf16)


def _proj_b(h, g, w1, latent_norm, wuq, q_norm, *, tm):
    n, d = h.shape
    qn = jnp.tile(q_norm, B_HEADS).reshape(1, B_QA)
    row = lambda i: (i, 0)
    row3 = lambda i: (i, 0, 0)
    const = lambda i: (0, 0)
    nt = tm // LANES
    return pl.pallas_call(
        _proj_b_kernel,
        grid=(n // tm,),
        in_specs=[
            pl.BlockSpec((tm, d), row),
            pl.BlockSpec((1, d), const),
            pl.BlockSpec(w1.shape, const),
            pl.BlockSpec(latent_norm.shape, const),
            pl.BlockSpec(wuq.shape, const),
            pl.BlockSpec((1, B_QA), const),
        ],
        out_specs=[
            pl.BlockSpec((tm, B_QA), row),
            pl.BlockSpec((tm, 2 * B_QI), row),
            pl.BlockSpec((tm, B_KV_RANK), row),
            pl.BlockSpec((nt, B_TR, LANES), row3),
            pl.BlockSpec((tm, LANES), row),
            pl.BlockSpec((nt, IDX_HEADS, LANES), row3),
        ],
        out_shape=[
            jax.ShapeDtypeStruct((n, B_QA), bf16),
            jax.ShapeDtypeStruct((n, 2 * B_QI), bf16),
            jax.ShapeDtypeStruct((n, B_KV_RANK), bf16),
            jax.ShapeDtypeStruct((n // LANES, B_TR, LANES), bf16),
            jax.ShapeDtypeStruct((n, LANES), bf16),
            jax.ShapeDtypeStruct((n // LANES, IDX_HEADS, LANES), f32),
        ],
        compiler_params=_cparams(("arbitrary",)),
        name="proj_b",
    )(h, g.reshape(1, d), w1, latent_norm, wuq, qn)


def _attn_b_kernel(qa_ref, qi_ref, wit_ref, ckv_ref, ckvt_ref, kk_ref, ckvm_ref, ckvtm_ref, kkm_ref,
                   ckvmf_ref, ckvtmf_ref, bias_ref, wuvt_ref,
                   o_ref, qs_ref, is_ref, key_ref, pen_ref, m_ref, acc_ref, *, nblk, k_sel):
    s_id = pl.program_id(0)
    nb_real = pl.num_programs(0) - 1
    r_io, c_io = _tile_iotas()
    rk = B_KV_RANK

    def init():
        for h in range(B_HEADS):
            qs_ref[h * LANES:(h + 1) * LANES, :] = qa_ref[:, h * rk:(h + 1) * rk]
        m_ref[...] = jnp.full(m_ref.shape, NEG_INF, f32)
        acc_ref[...] = jnp.zeros(acc_ref.shape, f32)

    def update(ckv_fn, ckvt_fn, bias_fn, pen_fn, lazy, offset=None):
        def logits():
            st = _dot_nt(ckv_fn(), qs_ref[...])
            pen = None if pen_fn is None else pen_fn()
            cols = []
            for h in range(B_HEADS):
                add = pen if bias_fn is None else (bias_fn(h) if pen is None else bias_fn(h) + pen)
                cols.append(st[:, h * LANES:(h + 1) * LANES] + add)
            return jnp.concatenate(cols, axis=1)

        _softmax_step(logits, lambda pb: _dot(ckvt_fn(), pb), m_ref, acc_ref, rk, lazy, offset)

    def finalize():
        olat = (acc_ref[0:rk, :] / acc_ref[rk:rk + 1, :]).astype(bf16)
        ot = jnp.concatenate([_dot(wuvt_ref[h], olat[:, h * LANES:(h + 1) * LANES]) for h in range(B_HEADS)], axis=0)
        o_ref[...] = ot.T.astype(bf16)

    def index_scores(kk):
        s = jnp.maximum(_dot_nt(kk, is_ref[...]), 0.0)
        wt = wit_ref[0]
        sc = jnp.zeros((kk.shape[0], LANES), f32)
        for hh in range(IDX_HEADS):
            sc = sc + wt[hh:hh + 1, :] * s[:, hh * LANES:(hh + 1) * LANES]
        return sc

    def sort_key(sc):
        bits = lax.bitcast_convert_type(sc + 0.0, jnp.int32)
        return jnp.where(bits < 0, bits ^ jnp.int32(0x7FFFFFFF), bits)

    @pl.when(s_id < nb_real)
    def _():
        i = s_id % nblk
        ntile = i + 2
        init()
        for hh in range(IDX_HEADS):
            base = (hh % 2) * B_QI + (hh // 2) * LANES
            is_ref[hh * LANES:(hh + 1) * LANES, :] = qi_ref[:, base:base + LANES]

        key_ref[0] = jnp.full((LANES, LANES), INT_MIN, jnp.int32)
        key_ref[0, 0:N_META, :] = sort_key(index_scores(kkm_ref[0]))

        def score_body(j, carry):
            off = pl.multiple_of(j * LANES, LANES)
            vis = (_vec(j) < i) | ((r_io >> 6) <= (c_io >> 6))
            key_ref[j + 1] = jnp.where(vis, sort_key(index_scores(kk_ref[pl.ds(off, LANES), :])), jnp.int32(INT_MIN))
            return carry

        lax.fori_loop(0, i + 1, score_body, 0)

        def count(pred):
            def cbody(t, accv):
                return accv + jnp.where(pred(key_ref[t], t), 1.0, 0.0)
            accv = lax.fori_loop(0, ntile, cbody, jnp.zeros((LANES, LANES), f32))
            return jnp.sum(accv, axis=0, keepdims=True)

        kf = float(k_sel)
        zero = jnp.zeros((1, LANES), jnp.int32)
        t0 = jnp.where(count(lambda k, t: k >= zero) >= kf, zero, jnp.int32(INT_MIN))

        def bit_body(it, tcur):
            cand = tcur | jnp.left_shift(jnp.int32(1), 30 - it)
            return jnp.where(count(lambda k, t: k >= cand) >= kf, cand, tcur)

        thr = lax.fori_loop(0, 31, bit_body, t0)

        need = kf - count(lambda k, t: k > thr)
        n_eq = count(lambda k, t: k == thr)
        has_thr = thr > jnp.int32(INT_MIN)
        tied = jnp.max(jnp.where(has_thr & (n_eq > need), 1.0, 0.0)) > 0.0

        def tie_search(_):
            def jbody(it, jcur):
                cand = jcur | jnp.left_shift(jnp.int32(1), 11 - it)
                cnt = count(lambda k, t: (k == thr) & ((t * LANES + r_io) < cand))
                return jnp.where(cnt < need, cand, jcur)
            return lax.fori_loop(0, 12, jbody, jnp.zeros((1, LANES), jnp.int32))

        j_last = lax.cond(tied, tie_search, lambda _: jnp.full((1, LANES), 4095, jnp.int32), 0)
        j_last = jnp.where(has_thr, j_last, -1)

        def pen_body(t, carry):
            k = key_ref[t]
            sel = (k > thr) | ((k == thr) & ((t * LANES + r_io) <= j_last))
            pen_ref[t] = jnp.where(sel, 0.0, NEG_INF)
            return carry

        lax.fori_loop(0, ntile, pen_body, 0)
        for t in range(1, NSUB):
            pen_ref[i + 1 + t] = jnp.full((LANES, LANES), NEG_INF, f32)

        kind_m = jnp.where(i == 0, KIND_META0, KIND_FAR)
        update(lambda: ckvm_ref[0], lambda: ckvtm_ref[0], lambda h: bias_ref[kind_m, h, 0:N_META, :],
               lambda: pen_ref[0, 0:N_META, :], lazy=False)

        def kv_fns(w):
            off = pl.multiple_of(w * (NSUB * LANES), NSUB * LANES)
            return (lambda: ckv_ref[pl.ds(off, NSUB * LANES), :],
                    lambda: jnp.concatenate([ckvt_ref[w * NSUB + t] for t in range(NSUB)], axis=1),
                    lambda: jnp.concatenate([pen_ref[w * NSUB + t + 1] for t in range(NSUB)], axis=0))

        far_bias = jnp.concatenate([bias_ref[KIND_FAR, h, 0:1, :] for h in range(B_HEADS)], axis=1)

        def far_body(w, carry):
            ckv_fn, ckvt_fn, pen_fn = kv_fns(w)
            update(ckv_fn, ckvt_fn, None, pen_fn, lazy=True, offset=far_bias)
            return carry

        def near_body(w, carry):
            ckv_fn, ckvt_fn, pen_fn = kv_fns(w)
            kinds = [_block_kind(w * NSUB + t - i) for t in range(NSUB)]
            update(ckv_fn, ckvt_fn,
                   lambda h: jnp.concatenate([bias_ref[kinds[t], h] for t in range(NSUB)], axis=0), pen_fn,
                   lazy=True)
            return carry

        nfar = _far_steps(i)
        lax.fori_loop(0, nfar, far_body, 0)
        lax.fori_loop(nfar, i // NSUB + 1, near_body, 0)
        finalize()

    @pl.when(s_id == nb_real)
    def _():
        init()
        update(lambda: ckvmf_ref[...], lambda: ckvtmf_ref[0], lambda h: bias_ref[KIND_METAMETA, h], None, lazy=False)
        finalize()


def _attn_b(qa, qi, wit, ckv, ckvt, kk, bias, wuvt, *, bsz, nblk, k_sel):
    n = qa.shape[0]
    nb_real = bsz * nblk
    seq = nblk * LANES
    assert k_sel >= N_META and (nblk + 1) * LANES <= 4096 and nblk % NSUB == 0
    ckvm, ckvtm = _meta_views(ckv[nb_real * LANES:], ckvt[nb_real], bsz)
    kkm = kk[nb_real * LANES:].reshape(bsz, N_META, LANES)
    kern = functools.partial(_attn_b_kernel, nblk=nblk, k_sel=k_sel)
    bidx = lambda s: jnp.minimum(s // nblk, bsz - 1)
    blk = lambda s: (s, 0)
    return pl.pallas_call(
        kern,
        grid=(nb_real + 1,),
        in_specs=[
            pl.BlockSpec((LANES, B_QA), blk),
            pl.BlockSpec((LANES, 2 * B_QI), blk),
            pl.BlockSpec((1, IDX_HEADS, LANES), lambda s: (s, 0, 0)),
            pl.BlockSpec((seq, B_KV_RANK), lambda s: (bidx(s), 0)),
            pl.BlockSpec((nblk, B_TR, LANES), lambda s: (bidx(s), 0, 0)),
            pl.BlockSpec((seq, LANES), lambda s: (bidx(s), 0)),
            pl.BlockSpec((1, N_META, B_KV_RANK), lambda s: (bidx(s), 0, 0)),
            pl.BlockSpec((1, B_TR, N_META), lambda s: (bidx(s), 0, 0)),
            pl.BlockSpec((1, N_META, LANES), lambda s: (bidx(s), 0, 0)),
            pl.BlockSpec((LANES, B_KV_RANK), lambda s: (nb_real, 0)),
            pl.BlockSpec((1, B_TR, LANES), lambda s: (nb_real, 0, 0)),
            _bias_spec(1),
            pl.BlockSpec(wuvt.shape, lambda s: (0, 0, 0)),
        ],
        out_specs=pl.BlockSpec((LANES, B_HEADS * B_VD), blk),
        out_shape=jax.ShapeDtypeStruct((n, B_HEADS * B_VD), bf16),
        scratch_shapes=[
            pltpu.VMEM((B_HEADS * LANES, B_KV_RANK), bf16),
            pltpu.VMEM((IDX_HEADS * LANES, LANES), bf16),
            pltpu.VMEM((nblk + 1, LANES, LANES), jnp.int32),
            pltpu.VMEM((nblk + NSUB, LANES, LANES), f32),
            pltpu.VMEM((1, B_HEADS * LANES), f32),
            pltpu.VMEM((B_TR, B_HEADS * LANES), f32),
        ],
        compiler_params=_cparams(("arbitrary",)),
        name="attn_b",
    )(qa, qi, wit, ckv, ckvt, kk, ckvm, ckvtm, kkm, ckv, ckvt, bias, wuvt)


def kernel(x, meta_tokens, rel_bias, ln_ffn1, ffn1_wi, ffn1_wo, ln_mix, w_out, ln_ffn2, ffn2_wi, ffn2_wo, a_w_in, a_qk_norm, a_lambda, a_subln, b_w_in, b_latent_norm, b_w_uq, b_q_norm, b_w_uv, c_w_in, c_qk_norm, c_sinks):
    bsz, seq, d = x.shape
    assert d == D_MODEL and seq % LANES == 0 and bsz * N_META == LANES
    nblk = seq // LANES
    n = bsz * seq + LANES
    k_sel = min(TOPK_MAX, seq // 4)
    tm_ffn = _row_tile(n, 1408)
    tm_proj = _row_tile(n, 384, LANES)
    fc = 256

    h = jnp.concatenate([x.reshape(bsz * seq, d),
                         jnp.broadcast_to(meta_tokens.astype(x.dtype), (bsz, N_META, d)).reshape(LANES, d)], axis=0)
    bias = _bias_tiles(rel_bias)

    for layer in range(DEPTH):
        h = _ffn(h, ln_ffn1[layer], ffn1_wi[layer], ffn1_wo[layer], tm=tm_ffn, fc=fc)
        kind, j = layer % N_MIXERS, layer // N_MIXERS
        g = ln_mix[layer]
        if kind == 0:
            lambda_init = 0.8 - 0.6 * math.exp(-0.3 * layer)
            qkv, vt = _proj_a(h, g, a_w_in[j].astype(bf16), a_qk_norm[j], tm=tm_proj)
            mix = _attn_a(qkv, vt, bias, a_lambda[j], a_subln[j], bsz=bsz, nblk=nblk, lambda_init=lambda_init)
        elif kind == 1:
            w = b_w_in[j]
            r2 = B_Q_RANK + B_KV_RANK
            kcol = w[:, r2:r2 + IDX_DIM]
            w1 = jnp.concatenate([w[:, :r2], kcol, kcol, w[:, r2 + IDX_DIM:],
                                  jnp.zeros((d, LANES - IDX_HEADS), w.dtype)], axis=1).astype(bf16)
            assert w1.shape[1] == B_W1
            qa, qi, ckv, ckvt, kk, wit = _proj_b(h, g, w1, b_latent_norm[j], b_w_uq[j].astype(bf16), b_q_norm[j],
                                                 tm=tm_proj)
            wuvt = jnp.swapaxes(b_w_uv[j], 1, 2).astype(bf16)
            mix = _attn_b(qa, qi, wit, ckv, ckvt, kk, bias, wuvt, bsz=bsz, nblk=nblk, k_sel=k_sel)
        else:
            w = c_w_in[j]
            kcols = [w[:, C_QD + gi * C_HD:C_QD + (gi + 1) * C_HD] for gi in range(C_KV_HEADS)]
            voff = C_QD + C_KV_HEADS * C_HD
            vcols = [w[:, voff + gi * C_HD:voff + (gi + 1) * C_HD] for gi in range(C_KV_HEADS)]
            wc = jnp.concatenate([w[:, :C_QD]] + [kc for kc in kcols for _ in range(2)]
                                 + [vc for vc in vcols for _ in range(2)], axis=1).astype(bf16)
            qkv = _proj_c(h, g, wc, c_qk_norm[j], tm=tm_proj)
            mix = _attn_c(qkv, bias, c_sinks[j], bsz=bsz, nblk=nblk)
        h = _ffn(h, ln_ffn2[layer], ffn2_wi[layer], ffn2_wo[layer], tm=tm_ffn, fc=fc,
                 mix=mix, wout=w_out[layer].astype(bf16))
    return h[:bsz * seq].reshape(bsz, seq, d)
```

```python
import functools
import math

import numpy as np
import jax
import jax.numpy as jnp
from jax import lax
from jax.experimental import pallas as pl
from jax.experimental.pallas import tpu as pltpu

D_MODEL = 1024
DEPTH = 4
CHUNK = 64
N_META = 16
N_MIXERS = 3
NEG_INF = -1e30
REL_BUCKETS = 32
REL_MAX_DIST = 128
REL_HEADS = 16
D_FF = 2816
A_HEADS = 8
A_HD = 64
A_VD = 2 * A_HD
B_HEADS = 16
B_Q_RANK = 256
B_KV_RANK = 256
B_VD = 64
IDX_HEADS = 8
IDX_DIM = 64
TOPK_MAX = 256
C_Q_HEADS = 16
C_KV_HEADS = 2
C_GROUP = C_Q_HEADS // C_KV_HEADS
C_HD = 64
EPS = 1e-6

LANES = 128
BF16_ROWS = 16
VMEM_LIMIT = 56 * 1024 * 1024
INT_MIN = -(2 ** 31)
NSUB = 2
LOG2E = math.log2(math.e)
LAZY_GAP = 57.0
ONES_ROWS = BF16_ROWS

KIND_DIAG, KIND_PREV, KIND_FAR, KIND_META0, KIND_METAMETA, KIND_MASKED = 0, 1, 2, 3, 4, 5
N_KINDS = 6

f32 = jnp.float32
bf16 = jnp.bfloat16


def _cparams(sem):
    return pltpu.CompilerParams(dimension_semantics=sem, vmem_limit_bytes=VMEM_LIMIT)


def _row_tile(n, cap, mult=BF16_ROWS):
    best = None
    for t in range(mult, cap + 1, mult):
        if n % t == 0:
            best = t
    assert best is not None
    return best


def _dot(a, b):
    return jnp.dot(a, b, preferred_element_type=f32)


def _dot_nt(a, b):
    return lax.dot_general(a, b, (((1,), (1,)), ((), ())), preferred_element_type=f32)


def _rms_rows(x):
    return x * lax.rsqrt(jnp.mean(x * x, axis=-1, keepdims=True) + EPS)


def _lo_half_mask(shape, period, half):
    return (lax.broadcasted_iota(jnp.int32, shape, 1) & (period - 1)) < half


def _group_rms(x, group):
    r, c = x.shape
    outs = []
    if group == 64:
        lo = _lo_half_mask((r, LANES), LANES, 64)
        for ci in range(c // LANES):
            xc = x[:, ci * LANES:(ci + 1) * LANES]
            x2 = xc * xc
            s_lo = jnp.sum(jnp.where(lo, x2, 0.0), axis=-1, keepdims=True)
            s_hi = jnp.sum(jnp.where(lo, 0.0, x2), axis=-1, keepdims=True)
            inv = jnp.where(lo, lax.rsqrt(s_lo * (1.0 / 64) + EPS), lax.rsqrt(s_hi * (1.0 / 64) + EPS))
            outs.append(xc * inv)
    else:
        for gi in range(c // group):
            outs.append(_rms_rows(x[:, gi * group:(gi + 1) * group]))
    return outs[0] if len(outs) == 1 else jnp.concatenate(outs, axis=-1)


def _tile_iotas():
    r = lax.broadcasted_iota(jnp.int32, (LANES, LANES), 0)
    c = lax.broadcasted_iota(jnp.int32, (LANES, LANES), 1)
    return r, c


def _vec(s):
    return jnp.full((LANES, LANES), s, jnp.int32)


def _softmax_step(logits_fn, pv_fn, m_ref, acc_ref, gap_ref=None, offset=None):
    st = logits_fn()
    if gap_ref is None:
        if offset is not None:
            st = st + offset
        m_old = m_ref[...]
        m_new = jnp.maximum(m_old, jnp.max(st, axis=0, keepdims=True))
        acc_ref[...] = jnp.exp2(m_old - m_new) * acc_ref[...] + pv_fn(jnp.exp2(st - m_new).astype(bf16))
        m_ref[...] = m_new
    else:
        shift = m_ref[...] if offset is None else m_ref[...] - offset
        gap_ref[...] = jnp.maximum(gap_ref[...], jnp.max(st, axis=0, keepdims=True) - shift)
        acc_ref[...] += pv_fn(jnp.exp2(st - shift).astype(bf16))


def _softmax_loop(lo, hi, step_fn, m_ref, acc_ref, gap_ref):
    def body(w, carry):
        logits_fn, pv_fn, offset = step_fn(w)
        _softmax_step(logits_fn, pv_fn, m_ref, acc_ref, gap_ref, offset)
        return carry

    lax.fori_loop(lo, hi, body, 0)


def _overshot(gap_ref):
    return jnp.logical_not(jnp.max(gap_ref[...]) <= LAZY_GAP)


def _block_kind(rel):
    return jnp.where(rel < -1, KIND_FAR,
                     jnp.where(rel == -1, KIND_PREV, jnp.where(rel == 0, KIND_DIAG, KIND_MASKED)))


def _far_steps(i):
    return jnp.maximum(i - 1, 0) // NSUB


def _ffn_kernel(*refs, fuse_out):
    if fuse_out:
        h_ref, mix_ref, wout_ref, g_ref, wa_ref, wb_ref, wo_ref, o_ref, xn_ref = refs
    else:
        h_ref, g_ref, wa_ref, wb_ref, wo_ref, o_ref, xn_ref = refs
    j = pl.program_id(1)

    @pl.when(j == 0)
    def _():
        r = h_ref[...]
        if fuse_out:
            r = r + _dot(mix_ref[...], wout_ref[...])
        o_ref[...] = r
        xn_ref[...] = (_rms_rows(r) * g_ref[...]).astype(bf16)

    xn = xn_ref[...]
    a = _dot(xn, wa_ref[...].astype(bf16))
    b = _dot(xn, wb_ref[...].astype(bf16))
    act = (a / (1.0 + jnp.exp(-a)) * b).astype(bf16)
    o_ref[...] += 0.5 * _dot(act, wo_ref[...].astype(bf16))


def _ffn(h, g, wi, wo, layer, *, tm, fc, mix=None, wout=None):
    n, d = h.shape
    dff = wo.shape[1]
    nj = dff // fc
    fuse = mix is not None
    row = lambda i, j: (i, 0)
    in_specs = [pl.BlockSpec((tm, d), row)]
    args = [h]
    if fuse:
        in_specs += [pl.BlockSpec((tm, mix.shape[1]), row), pl.BlockSpec(wout.shape, lambda i, j: (0, 0))]
        args += [mix, wout]
    in_specs += [
        pl.BlockSpec((1, d), lambda i, j: (0, 0)),
        pl.BlockSpec((None, d, fc), lambda i, j: (layer, 0, j)),
        pl.BlockSpec((None, d, fc), lambda i, j: (layer, 0, j + nj)),
        pl.BlockSpec((None, fc, d), lambda i, j: (layer, j, 0)),
    ]
    args += [g.reshape(1, d), wi, wi, wo]
    return pl.pallas_call(
        functools.partial(_ffn_kernel, fuse_out=fuse),
        grid=(n // tm, nj),
        in_specs=in_specs,
        out_specs=pl.BlockSpec((tm, d), row),
        out_shape=jax.ShapeDtypeStruct((n, d), f32),
        scratch_shapes=[pltpu.VMEM((tm, d), bf16)],
        compiler_params=_cparams(("arbitrary", "arbitrary")),
        name="ffn_out" if fuse else "ffn",
    )(*args)


def _rel_bucket(rel):
    half = REL_BUCKETS // 2
    max_exact = half // 2
    n = jnp.abs(rel)
    large = max_exact + (jnp.log(jnp.maximum(n, 1).astype(jnp.float32) / max_exact)
                         / math.log(REL_MAX_DIST / max_exact) * (half - max_exact)).astype(jnp.int32)
    large = jnp.minimum(large, half - 1)
    return jnp.where(rel > 0, half, 0) + jnp.where(n < max_exact, n, large)


def _rel_tiles():
    r = np.arange(LANES)[:, None]
    c = np.arange(LANES)[None, :]
    far = np.full((LANES, LANES), -4 * LANES)
    ones = np.ones((LANES, LANES), bool)
    rels, vis = [], []
    for q, k in ((r, c), (c, r)):
        rels += [k - q, k - q - LANES, far, (k % N_META) - N_META - q, (k % N_META) - (q % N_META), far]
        if q is r:
            vis += [ones] * N_KINDS
        else:
            vis += [(k // CHUNK) <= (q // CHUNK), ones, ones, ones, (k // N_META) == (q // N_META), ~ones]
    return (np.stack([np.broadcast_to(a, (LANES, LANES)) for a in rels]).astype(np.int32),
            np.stack([np.broadcast_to(a, (LANES, LANES)) for a in vis]).astype(np.int32))


def _bias_kernel(rb_ref, bucket_ref, vis_ref, o_ref):
    h = pl.program_id(0)
    for kind in range(2 * N_KINDS):
        bk = bucket_ref[kind]
        acc = jnp.zeros((LANES, LANES), f32)
        for b in range(REL_BUCKETS):
            acc = jnp.where(bk == b, rb_ref[b, h], acc)
        if kind >= N_KINDS:
            acc = acc * LOG2E
        o_ref[kind, 0] = jnp.where(vis_ref[kind] != 0, acc, NEG_INF)


def _bias_tiles(rel_bias):
    rel, vis = _rel_tiles()
    bucket = _rel_bucket(jnp.asarray(rel))
    nk = 2 * N_KINDS
    return pl.pallas_call(
        _bias_kernel,
        grid=(REL_HEADS,),
        in_specs=[
            pl.BlockSpec(memory_space=pltpu.SMEM),
            pl.BlockSpec((nk, LANES, LANES), lambda h: (0, 0, 0)),
            pl.BlockSpec((nk, LANES, LANES), lambda h: (0, 0, 0)),
        ],
        out_specs=pl.BlockSpec((nk, 1, LANES, LANES), lambda h: (0, h, 0, 0)),
        out_shape=jax.ShapeDtypeStruct((nk, REL_HEADS, LANES, LANES), f32),
        compiler_params=_cparams(("arbitrary",)),
        name="bias_tiles",
    )(rel_bias, bucket, jnp.asarray(vis))


def _bias_spec(which):
    return pl.BlockSpec((N_KINDS, REL_HEADS, LANES, LANES), lambda s: (which, 0, 0, 0))


def _meta_views(rows, cols_t, bsz):
    f = rows.shape[1]
    return (rows.reshape(bsz, N_META, f),
            cols_t.reshape(cols_t.shape[0], bsz, N_META).transpose(1, 0, 2))


A_QD = A_HEADS * 2 * A_HD
A_VR = A_VD + ONES_ROWS


def _proj_a_kernel(h_ref, g_ref, w_ref, gq_ref, gk_ref, o_ref, vt_ref):
    xn = (_rms_rows(h_ref[...]) * g_ref[...]).astype(bf16)
    y = _dot(xn, w_ref[...])
    q = _group_rms(y[:, :A_QD], A_HD) * gq_ref[...] * (A_HD ** -0.5 * LOG2E)
    lo = _lo_half_mask(q.shape, 2 * A_HD, A_HD)
    o_ref[:, :A_QD] = jnp.where(lo, q, 0.0).astype(bf16)
    o_ref[:, A_QD:2 * A_QD] = jnp.where(lo, 0.0, q).astype(bf16)
    o_ref[:, 2 * A_QD:] = (_group_rms(y[:, A_QD:2 * A_QD], A_HD) * gk_ref[...]).astype(bf16)
    ones = jnp.ones((ONES_ROWS, LANES), bf16)
    for t in range(vt_ref.shape[0]):
        vt = y[t * LANES:(t + 1) * LANES, 2 * A_QD:].T.astype(bf16)
        for h in range(A_HEADS):
            vt_ref[t, h * A_VR:h * A_VR + A_VD, :] = vt[h * A_VD:(h + 1) * A_VD]
            vt_ref[t, h * A_VR + A_VD:(h + 1) * A_VR, :] = ones


def _proj_a(h, g, w, qk_norm, *, tm):
    n, d = h.shape
    nw = w.shape[1]
    gq = jnp.tile(qk_norm[0], A_QD // A_HD).reshape(1, A_QD)
    gk = jnp.tile(qk_norm[1], A_QD // A_HD).reshape(1, A_QD)
    return pl.pallas_call(
        _proj_a_kernel,
        grid=(n // tm,),
        in_specs=[
            pl.BlockSpec((tm, d), lambda i: (i, 0)),
            pl.BlockSpec((1, d), lambda i: (0, 0)),
            pl.BlockSpec((d, nw), lambda i: (0, 0)),
            pl.BlockSpec((1, A_QD), lambda i: (0, 0)),
            pl.BlockSpec((1, A_QD), lambda i: (0, 0)),
        ],
        out_specs=[
            pl.BlockSpec((tm, 3 * A_QD), lambda i: (i, 0)),
            pl.BlockSpec((tm // LANES, A_HEADS * A_VR, LANES), lambda i: (i, 0, 0)),
        ],
        out_shape=[
            jax.ShapeDtypeStruct((n, 3 * A_QD), bf16),
            jax.ShapeDtypeStruct((n // LANES, A_HEADS * A_VR, LANES), bf16),
        ],
        compiler_params=_cparams(("arbitrary",)),
        name="proj_a",
    )(h, g.reshape(1, d), w, gq, gk)


def _attn_a_kernel(qlo_ref, qhi_ref, k_ref, vt_ref, km_ref, vtm_ref, kmf_ref, vtmf_ref, bias_ref, lam_ref, sub_ref,
                   o_ref, qs_ref, m_ref, acc_ref, gap_ref, *, nblk, lambda_init):
    s_id = pl.program_id(0)
    nb_real = pl.num_programs(0) - 1
    hw = 2 * A_HD

    def init():
        for h in range(A_HEADS):
            qs_ref[h, :LANES, :] = qlo_ref[:, h * hw:(h + 1) * hw]
            qs_ref[h, LANES:, :] = qhi_ref[:, h * hw:(h + 1) * hw]
        m_ref[...] = jnp.full(m_ref.shape, NEG_INF, f32)
        acc_ref[...] = jnp.zeros(acc_ref.shape, f32)

    def step_fns(kt_fn, vtt_fn, bias_fn):
        def logits():
            sts = [_dot_nt(kt_fn(h), qs_ref[h]) for h in range(A_HEADS)]
            if bias_fn is not None:
                sts = [st + bias_fn(h) for h, st in enumerate(sts)]
            return jnp.concatenate(sts, axis=1)

        def pv(pb):
            return jnp.concatenate(
                [_dot(vtt_fn(h), pb[:, h * 2 * LANES:(h + 1) * 2 * LANES]) for h in range(A_HEADS)], axis=1)

        return logits, pv

    def bias_rows(kind, h, rows):
        return jnp.concatenate([bias_ref[kind, h, 0:rows, :], bias_ref[kind, A_HEADS + h, 0:rows, :]], axis=1)

    def finalize():
        lam = lam_ref[...]
        lam_full = (jnp.exp(jnp.sum(lam[0:1] * lam[1:2], axis=-1, keepdims=True))
                    - jnp.exp(jnp.sum(lam[2:3] * lam[3:4], axis=-1, keepdims=True)) + lambda_init)
        o = acc_ref[0:A_VD, :] / acc_ref[A_VD:A_VD + 1, :]
        for h in range(A_HEADS):
            d = o[:, 2 * h * LANES:(2 * h + 1) * LANES] - lam_full * o[:, (2 * h + 1) * LANES:(2 * h + 2) * LANES]
            d = d * lax.rsqrt(jnp.mean(d * d, axis=0, keepdims=True) + EPS) * sub_ref[...] * (1.0 - lambda_init)
            o_ref[:, h * hw:(h + 1) * hw] = d.T.astype(bf16)

    @pl.when(s_id < nb_real)
    def _():
        i = s_id % nblk
        kind_m = jnp.where(i == 0, KIND_META0, KIND_FAR)
        nfar = _far_steps(i)

        def kv_fns(w):
            off = pl.multiple_of(w * (NSUB * LANES), NSUB * LANES)
            return (lambda h: k_ref[pl.ds(off, NSUB * LANES), h * hw:(h + 1) * hw],
                    lambda h: jnp.concatenate([vt_ref[w * NSUB + t, h * A_VR:(h + 1) * A_VR, :] for t in range(NSUB)],
                                              axis=1))

        def far_step(w):
            far_bias = jnp.concatenate([bias_rows(KIND_FAR, h, 1) for h in range(A_HEADS)], axis=1)
            return step_fns(*kv_fns(w), None) + (far_bias,)

        def near_step(w):
            kinds = [_block_kind(w * NSUB + t - i) for t in range(NSUB)]
            return step_fns(*kv_fns(w), lambda h: jnp.concatenate(
                [bias_rows(kinds[t], h, LANES) for t in range(NSUB)], axis=0)) + (None,)

        def sweep(gap_ref):
            init()
            _softmax_step(*step_fns(lambda h: km_ref[0, :, h * hw:(h + 1) * hw],
                                    lambda h: vtm_ref[0, h * A_VR:(h + 1) * A_VR, :],
                                    lambda h: bias_rows(kind_m, h, N_META)), m_ref, acc_ref)
            _softmax_loop(0, nfar, far_step, m_ref, acc_ref, gap_ref)
            _softmax_loop(nfar, i // NSUB + 1, near_step, m_ref, acc_ref, gap_ref)

        gap_ref[...] = jnp.full(gap_ref.shape, NEG_INF, f32)
        sweep(gap_ref)
        pl.when(_overshot(gap_ref))(lambda: sweep(None))
        finalize()

    @pl.when(s_id == nb_real)
    def _():
        init()
        _softmax_step(*step_fns(lambda h: kmf_ref[:, h * hw:(h + 1) * hw],
                                lambda h: vtmf_ref[0, h * A_VR:(h + 1) * A_VR, :],
                                lambda h: bias_rows(KIND_METAMETA, h, LANES)), m_ref, acc_ref)
        finalize()


def _attn_a(qkv, vt, bias, lam, subln, *, bsz, nblk, lambda_init):
    n = qkv.shape[0]
    nb_real = bsz * nblk
    seq = nblk * LANES
    d = A_QD
    vr = A_HEADS * A_VR
    assert nblk % NSUB == 0
    km, vtm = _meta_views(qkv[nb_real * LANES:, 2 * d:], vt[nb_real], bsz)
    kern = functools.partial(_attn_a_kernel, nblk=nblk, lambda_init=lambda_init)
    bclamp = lambda s: jnp.minimum(s // nblk, bsz - 1)
    return pl.pallas_call(
        kern,
        grid=(nb_real + 1,),
        in_specs=[
            pl.BlockSpec((LANES, d), lambda s: (s, 0)),
            pl.BlockSpec((LANES, d), lambda s: (s, 1)),
            pl.BlockSpec((seq, d), lambda s: (bclamp(s), 2)),
            pl.BlockSpec((nblk, vr, LANES), lambda s: (bclamp(s), 0, 0)),
            pl.BlockSpec((1, N_META, d), lambda s: (bclamp(s), 0, 0)),
            pl.BlockSpec((1, vr, N_META), lambda s: (bclamp(s), 0, 0)),
            pl.BlockSpec((LANES, d), lambda s: (nb_real, 2)),
            pl.BlockSpec((1, vr, LANES), lambda s: (nb_real, 0, 0)),
            _bias_spec(1),
            pl.BlockSpec((4, A_HD), lambda s: (0, 0)),
            pl.BlockSpec((A_VD, LANES), lambda s: (0, 0)),
        ],
        out_specs=pl.BlockSpec((LANES, d), lambda s: (s, 0)),
        out_shape=jax.ShapeDtypeStruct((n, d), bf16),
        scratch_shapes=[
            pltpu.VMEM((A_HEADS, 2 * LANES, A_VD), bf16),
            pltpu.VMEM((1, A_HEADS * 2 * LANES), f32),
            pltpu.VMEM((A_VR, A_HEADS * 2 * LANES), f32),
            pltpu.VMEM((1, A_HEADS * 2 * LANES), f32),
        ],
        compiler_params=_cparams(("arbitrary",)),
        name="attn_a",
    )(qkv, qkv, qkv, vt, km, vtm, qkv, vt, bias, lam, jnp.broadcast_to(subln[:, None], (A_VD, LANES)))


C_QD = C_Q_HEADS * C_HD
C_KD = 2 * C_KV_HEADS * C_HD


def _proj_c_kernel(h_ref, g_ref, w_ref, gq_ref, gk_ref, o_ref):
    xn = (_rms_rows(h_ref[...]) * g_ref[...]).astype(bf16)
    y = _dot(xn, w_ref[...])
    q = _group_rms(y[:, :C_QD], C_HD) * gq_ref[...] * (C_HD ** -0.5)
    lo = _lo_half_mask(q.shape, 2 * C_HD, C_HD)
    o_ref[:, :C_QD] = jnp.where(lo, q, 0.0).astype(bf16)
    o_ref[:, C_QD:2 * C_QD] = jnp.where(lo, 0.0, q).astype(bf16)
    o_ref[:, 2 * C_QD:2 * C_QD + C_KD] = (_group_rms(y[:, C_QD:C_QD + C_KD], C_HD) * gk_ref[...]).astype(bf16)
    o_ref[:, 2 * C_QD + C_KD:] = y[:, C_QD + C_KD:].astype(bf16)


def _proj_c(h, g, w, qk_norm, *, tm):
    n, d = h.shape
    nw = w.shape[1]
    nout = nw + C_QD
    gq = jnp.tile(qk_norm[0], C_QD // C_HD).reshape(1, C_QD)
    gk = jnp.tile(qk_norm[1], C_KD // C_HD).reshape(1, C_KD)
    return pl.pallas_call(
        _proj_c_kernel,
        grid=(n // tm,),
        in_specs=[
            pl.BlockSpec((tm, d), lambda i: (i, 0)),
            pl.BlockSpec((1, d), lambda i: (0, 0)),
            pl.BlockSpec((d, nw), lambda i: (0, 0)),
            pl.BlockSpec((1, C_QD), lambda i: (0, 0)),
            pl.BlockSpec((1, C_KD), lambda i: (0, 0)),
        ],
        out_specs=pl.BlockSpec((tm, nout), lambda i: (i, 0)),
        out_shape=jax.ShapeDtypeStruct((n, nout), bf16),
        compiler_params=_cparams(("arbitrary",)),
        name="proj_c",
    )(h, g.reshape(1, d), w, gq, gk)


def _attn_c_kernel(sink_ref, qlo_ref, qhi_ref, k_ref, v_ref, km_ref, vm_ref, bias_ref, o_ref, qs_ref, *, nblk):
    s_id = pl.program_id(0)
    nb_real = pl.num_programs(0) - 1
    r_io, c_io = _tile_iotas()
    rows = C_GROUP * LANES

    def attend(tiles):
        lo_t = c_io < C_HD
        for g in range(C_KV_HEADS):
            for hh in range(C_GROUP):
                cc = (g * C_GROUP + hh) // 2
                src = qlo_ref if hh % 2 == 0 else qhi_ref
                qs_ref[hh * LANES:(hh + 1) * LANES, :] = src[:, cc * LANES:(cc + 1) * LANES]
            sink = jnp.concatenate(
                [jnp.full((LANES, 1), sink_ref[g * C_GROUP + hh], f32) for hh in range(C_GROUP)], axis=0)
            ss = []
            m = sink
            for (kt, vt, kind, mask) in tiles:
                s = _dot_nt(qs_ref[...], kt[:, g * LANES:(g + 1) * LANES])
                s = s + jnp.concatenate(
                    [jnp.where(mask, bias_ref[kind, g * C_GROUP + hh], NEG_INF) for hh in range(C_GROUP)], axis=0)
                m = jnp.maximum(m, jnp.max(s, axis=-1, keepdims=True))
                ss.append(s)
            l = jnp.exp(sink - m)
            o = jnp.zeros((rows, LANES), f32)
            for s, (kt, vt, kind, mask) in zip(ss, tiles):
                p = jnp.exp(s - m)
                l = l + jnp.sum(p, axis=-1, keepdims=True)
                o = o + _dot(p.astype(bf16), vt[:, g * LANES:(g + 1) * LANES])
            o = o / l
            for cc in range(C_GROUP // 2):
                even = o[(2 * cc) * LANES:(2 * cc + 1) * LANES]
                odd = o[(2 * cc + 1) * LANES:(2 * cc + 2) * LANES]
                col = (g * (C_GROUP // 2) + cc) * LANES
                o_ref[:, col:col + LANES] = jnp.where(lo_t, even, odd).astype(bf16)

    @pl.when(s_id < nb_real)
    def _():
        b = s_id // nblk
        i = s_id % nblk
        first_half = r_io < CHUNK
        poff = pl.multiple_of(jnp.maximum(i - 1, 0) * LANES, LANES)
        coff = pl.multiple_of(i * LANES, LANES)
        attend([
            (km_ref[...], vm_ref[...], jnp.where(i == 0, KIND_META0, KIND_FAR), (c_io >> 4) == b),
            (k_ref[pl.ds(poff, LANES), :], v_ref[pl.ds(poff, LANES), :], KIND_PREV,
             (_vec(i) > 0) & (first_half | (c_io >= CHUNK))),
            (k_ref[pl.ds(coff, LANES), :], v_ref[pl.ds(coff, LANES), :], KIND_DIAG,
             (c_io < CHUNK) | (r_io >= CHUNK)),
        ])

    @pl.when(s_id == nb_real)
    def _():
        attend([(km_ref[...], vm_ref[...], KIND_METAMETA, (c_io >> 4) == (r_io >> 4))])


def _attn_c(qkv, bias, sinks, *, bsz, nblk):
    n = qkv.shape[0]
    nb_real = bsz * nblk
    seq = nblk * LANES
    kern = functools.partial(_attn_c_kernel, nblk=nblk)
    bclamp = lambda s: jnp.minimum(s // nblk, bsz - 1)
    kcol = 2 * C_QD // C_KD
    return pl.pallas_call(
        kern,
        grid=(nb_real + 1,),
        in_specs=[
            pl.BlockSpec(memory_space=pltpu.SMEM),
            pl.BlockSpec((LANES, C_QD), lambda s: (s, 0)),
            pl.BlockSpec((LANES, C_QD), lambda s: (s, 1)),
            pl.BlockSpec((seq, C_KD), lambda s: (bclamp(s), kcol)),
            pl.BlockSpec((seq, C_KD), lambda s: (bclamp(s), kcol + 1)),
            pl.BlockSpec((LANES, C_KD), lambda s: (nb_real, kcol)),
            pl.BlockSpec((LANES, C_KD), lambda s: (nb_real, kcol + 1)),
            _bias_spec(0),
        ],
        out_specs=pl.BlockSpec((LANES, C_QD), lambda s: (s, 0)),
        out_shape=jax.ShapeDtypeStruct((n, C_QD), bf16),
        scratch_shapes=[pltpu.VMEM((C_GROUP * LANES, LANES), bf16)],
        compiler_params=_cparams(("arbitrary",)),
        name="attn_c",
    )(sinks, qkv, qkv, qkv, qkv, qkv, qkv, bias)


B_QA = B_HEADS * B_KV_RANK
B_QI = IDX_HEADS * IDX_DIM
B_W1 = 2 * B_Q_RANK + 2 * LANES
B_TR = B_KV_RANK + ONES_ROWS


def _proj_b_kernel(h_ref, g_ref, w1_ref, ln_ref, wuq_ref, qn_ref,
                   qa_ref, qi_ref, ckv_ref, ckvt_ref, kk_ref, wit_ref):
    xn = (_rms_rows(h_ref[...]) * g_ref[...]).astype(bf16)
    y = _dot(xn, w1_ref[...])
    r = B_Q_RANK
    cq = (_rms_rows(y[:, :r]) * ln_ref[0:1, :]).astype(bf16)
    ckv = _rms_rows(y[:, r:2 * r]) * ln_ref[1:2, :]
    ckv_ref[...] = ckv.astype(bf16)
    kk_ref[...] = _rms_rows(y[:, 2 * r:2 * r + LANES]).astype(bf16)
    wi = y[:, 2 * r + LANES:] * (IDX_HEADS ** -0.5)
    ones = jnp.ones((ONES_ROWS, LANES), bf16)
    for t in range(ckvt_ref.shape[0]):
        ckvt_ref[t, 0:r, :] = ckv[t * LANES:(t + 1) * LANES, :].T.astype(bf16)
        ckvt_ref[t, r:, :] = ones
        wit_ref[t] = wi[t * LANES:(t + 1) * LANES, :].T[0:IDX_HEADS, :]
    z = _dot(cq, wuq_ref[...])
    qa_ref[...] = (_group_rms(z[:, :B_QA], B_KV_RANK) * qn_ref[...] * (B_KV_RANK ** -0.5 * LOG2E)).astype(bf16)
    qi = z[:, B_QA:] * (IDX_DIM ** -0.5)
    lo = _lo_half_mask(qi.shape, 2 * IDX_DIM, IDX_DIM)
    qi_ref[:, :B_QI] = jnp.where(lo, qi, 0.0).astype(bf16)
    qi_ref[:, B_QI:] = jnp.where(lo, 0.0, qi).astype(bf16)


def _proj_b(h, g, w1, latent_norm, wuq, q_norm, *, tm):
    n, d = h.shape
    qn = jnp.tile(q_norm, B_HEADS).reshape(1, B_QA)
    row = lambda i: (i, 0)
    row3 = lambda i: (i, 0, 0)
    const = lambda i: (0, 0)
    nt = tm // LANES
    return pl.pallas_call(
        _proj_b_kernel,
        grid=(n // tm,),
        in_specs=[
            pl.BlockSpec((tm, d), row),
            pl.BlockSpec((1, d), const),
            pl.BlockSpec(w1.shape, const),
            pl.BlockSpec(latent_norm.shape, const),
            pl.BlockSpec(wuq.shape, const),
            pl.BlockSpec((1, B_QA), const),
        ],
        out_specs=[
            pl.BlockSpec((tm, B_QA), row),
            pl.BlockSpec((tm, 2 * B_QI), row),
            pl.BlockSpec((tm, B_KV_RANK), row),
            pl.BlockSpec((nt, B_TR, LANES), row3),
            pl.BlockSpec((tm, LANES), row),
            pl.BlockSpec((nt, IDX_HEADS, LANES), row3),
        ],
        out_shape=[
            jax.ShapeDtypeStruct((n, B_QA), bf16),
            jax.ShapeDtypeStruct((n, 2 * B_QI), bf16),
            jax.ShapeDtypeStruct((n, B_KV_RANK), bf16),
            jax.ShapeDtypeStruct((n // LANES, B_TR, LANES), bf16),
            jax.ShapeDtypeStruct((n, LANES), bf16),
            jax.ShapeDtypeStruct((n // LANES, IDX_HEADS, LANES), f32),
        ],
        compiler_params=_cparams(("arbitrary",)),
        name="proj_b",
    )(h, g.reshape(1, d), w1, latent_norm, wuq, qn)


def _attn_b_kernel(qa_ref, qi_ref, wit_ref, ckv_ref, ckvt_ref, kk_ref, ckvm_ref, ckvtm_ref, kkm_ref,
                   ckvmf_ref, ckvtmf_ref, bias_ref, wuvt_ref,
                   o_ref, qs_ref, is_ref, key_ref, pen_ref, m_ref, acc_ref, gap_ref, *, nblk, k_sel):
    s_id = pl.program_id(0)
    nb_real = pl.num_programs(0) - 1
    r_io, c_io = _tile_iotas()
    rk = B_KV_RANK

    def init():
        for h in range(B_HEADS):
            qs_ref[h * LANES:(h + 1) * LANES, :] = qa_ref[:, h * rk:(h + 1) * rk]
        m_ref[...] = jnp.full(m_ref.shape, NEG_INF, f32)
        acc_ref[...] = jnp.zeros(acc_ref.shape, f32)

    def step_fns(ckv_fn, ckvt_fn, bias_fn, pen_fn):
        def logits():
            st = _dot_nt(ckv_fn(), qs_ref[...])
            pen = None if pen_fn is None else pen_fn()
            cols = []
            for h in range(B_HEADS):
                add = pen if bias_fn is None else (bias_fn(h) if pen is None else bias_fn(h) + pen)
                cols.append(st[:, h * LANES:(h + 1) * LANES] + add)
            return jnp.concatenate(cols, axis=1)

        return logits, lambda pb: _dot(ckvt_fn(), pb)

    def finalize():
        olat = (acc_ref[0:rk, :] / acc_ref[rk:rk + 1, :]).astype(bf16)
        ot = jnp.concatenate([_dot(wuvt_ref[h], olat[:, h * LANES:(h + 1) * LANES]) for h in range(B_HEADS)], axis=0)
        o_ref[...] = ot.T.astype(bf16)

    def index_scores(kk):
        s = jnp.maximum(_dot_nt(kk, is_ref[...]), 0.0)
        wt = wit_ref[0]
        sc = jnp.zeros((kk.shape[0], LANES), f32)
        for hh in range(IDX_HEADS):
            sc = sc + wt[hh:hh + 1, :] * s[:, hh * LANES:(hh + 1) * LANES]
        return sc

    def sort_key(sc):
        bits = lax.bitcast_convert_type(sc + 0.0, jnp.int32)
        return jnp.where(bits < 0, bits ^ jnp.int32(0x7FFFFFFF), bits)

    @pl.when(s_id < nb_real)
    def _():
        i = s_id % nblk
        ntile = i + 2
        init()
        for hh in range(IDX_HEADS):
            base = (hh % 2) * B_QI + (hh // 2) * LANES
            is_ref[hh * LANES:(hh + 1) * LANES, :] = qi_ref[:, base:base + LANES]

        key_ref[0] = jnp.full((LANES, LANES), INT_MIN, jnp.int32)
        key_ref[0, 0:N_META, :] = sort_key(index_scores(kkm_ref[0]))

        def score_body(j, carry):
            off = pl.multiple_of(j * LANES, LANES)
            vis = (_vec(j) < i) | ((r_io >> 6) <= (c_io >> 6))
            key_ref[j + 1] = jnp.where(vis, sort_key(index_scores(kk_ref[pl.ds(off, LANES), :])), jnp.int32(INT_MIN))
            return carry

        lax.fori_loop(0, i + 1, score_body, 0)

        def count(pred):
            def cbody(t, accv):
                return accv + jnp.where(pred(key_ref[t], t), 1.0, 0.0)
            accv = lax.fori_loop(0, ntile, cbody, jnp.zeros((LANES, LANES), f32))
            return jnp.sum(accv, axis=0, keepdims=True)

        kf = float(k_sel)
        zero = jnp.zeros((1, LANES), jnp.int32)
        t0 = jnp.where(count(lambda k, t: k >= zero) >= kf, zero, jnp.int32(INT_MIN))

        def bit_body(it, tcur):
            cand = tcur | jnp.left_shift(jnp.int32(1), 30 - it)
            return jnp.where(count(lambda k, t: k >= cand) >= kf, cand, tcur)

        thr = lax.fori_loop(0, 31, bit_body, t0)

        need = kf - count(lambda k, t: k > thr)
        n_eq = count(lambda k, t: k == thr)
        has_thr = thr > jnp.int32(INT_MIN)
        tied = jnp.max(jnp.where(has_thr & (n_eq > need), 1.0, 0.0)) > 0.0

        def tie_search(_):
            def jbody(it, jcur):
                cand = jcur | jnp.left_shift(jnp.int32(1), 11 - it)
                cnt = count(lambda k, t: (k == thr) & ((t * LANES + r_io) < cand))
                return jnp.where(cnt < need, cand, jcur)
            return lax.fori_loop(0, 12, jbody, jnp.zeros((1, LANES), jnp.int32))

        j_last = lax.cond(tied, tie_search, lambda _: jnp.full((1, LANES), 4095, jnp.int32), 0)
        j_last = jnp.where(has_thr, j_last, -1)

        def pen_body(t, carry):
            k = key_ref[t]
            sel = (k > thr) | ((k == thr) & ((t * LANES + r_io) <= j_last))
            pen_ref[t] = jnp.where(sel, 0.0, NEG_INF)
            return carry

        lax.fori_loop(0, ntile, pen_body, 0)
        for t in range(1, NSUB):
            pen_ref[i + 1 + t] = jnp.full((LANES, LANES), NEG_INF, f32)

        kind_m = jnp.where(i == 0, KIND_META0, KIND_FAR)
        nfar = _far_steps(i)

        def kv_fns(w):
            off = pl.multiple_of(w * (NSUB * LANES), NSUB * LANES)
            return (lambda: ckv_ref[pl.ds(off, NSUB * LANES), :],
                    lambda: jnp.concatenate([ckvt_ref[w * NSUB + t] for t in range(NSUB)], axis=1),
                    lambda: jnp.concatenate([pen_ref[w * NSUB + t + 1] for t in range(NSUB)], axis=0))

        def far_step(w):
            ckv_fn, ckvt_fn, pen_fn = kv_fns(w)
            far_bias = jnp.concatenate([bias_ref[KIND_FAR, h, 0:1, :] for h in range(B_HEADS)], axis=1)
            return step_fns(ckv_fn, ckvt_fn, None, pen_fn) + (far_bias,)

        def near_step(w):
            ckv_fn, ckvt_fn, pen_fn = kv_fns(w)
            kinds = [_block_kind(w * NSUB + t - i) for t in range(NSUB)]
            return step_fns(ckv_fn, ckvt_fn,
                            lambda h: jnp.concatenate([bias_ref[kinds[t], h] for t in range(NSUB)], axis=0),
                            pen_fn) + (None,)

        def sweep(gap_ref):
            m_ref[...] = jnp.full(m_ref.shape, NEG_INF, f32)
            acc_ref[...] = jnp.zeros(acc_ref.shape, f32)
            _softmax_step(*step_fns(lambda: ckvm_ref[0], lambda: ckvtm_ref[0],
                                    lambda h: bias_ref[kind_m, h, 0:N_META, :], lambda: pen_ref[0, 0:N_META, :]),
                          m_ref, acc_ref)
            _softmax_loop(0, nfar, far_step, m_ref, acc_ref, gap_ref)
            _softmax_loop(nfar, i // NSUB + 1, near_step, m_ref, acc_ref, gap_ref)

        gap_ref[...] = jnp.full(gap_ref.shape, NEG_INF, f32)
        sweep(gap_ref)
        pl.when(_overshot(gap_ref))(lambda: sweep(None))
        finalize()

    @pl.when(s_id == nb_real)
    def _():
        init()
        _softmax_step(*step_fns(lambda: ckvmf_ref[...], lambda: ckvtmf_ref[0],
                                lambda h: bias_ref[KIND_METAMETA, h], None), m_ref, acc_ref)
        finalize()


def _attn_b(qa, qi, wit, ckv, ckvt, kk, bias, wuvt, *, bsz, nblk, k_sel):
    n = qa.shape[0]
    nb_real = bsz * nblk
    seq = nblk * LANES
    assert k_sel >= N_META and (nblk + 1) * LANES <= 4096 and nblk % NSUB == 0
    ckvm, ckvtm = _meta_views(ckv[nb_real * LANES:], ckvt[nb_real], bsz)
    kkm = kk[nb_real * LANES:].reshape(bsz, N_META, LANES)
    kern = functools.partial(_attn_b_kernel, nblk=nblk, k_sel=k_sel)
    bidx = lambda s: jnp.minimum(s // nblk, bsz - 1)
    blk = lambda s: (s, 0)
    return pl.pallas_call(
        kern,
        grid=(nb_real + 1,),
        in_specs=[
            pl.BlockSpec((LANES, B_QA), blk),
            pl.BlockSpec((LANES, 2 * B_QI), blk),
            pl.BlockSpec((1, IDX_HEADS, LANES), lambda s: (s, 0, 0)),
            pl.BlockSpec((seq, B_KV_RANK), lambda s: (bidx(s), 0)),
            pl.BlockSpec((nblk, B_TR, LANES), lambda s: (bidx(s), 0, 0)),
            pl.BlockSpec((seq, LANES), lambda s: (bidx(s), 0)),
            pl.BlockSpec((1, N_META, B_KV_RANK), lambda s: (bidx(s), 0, 0)),
            pl.BlockSpec((1, B_TR, N_META), lambda s: (bidx(s), 0, 0)),
            pl.BlockSpec((1, N_META, LANES), lambda s: (bidx(s), 0, 0)),
            pl.BlockSpec((LANES, B_KV_RANK), lambda s: (nb_real, 0)),
            pl.BlockSpec((1, B_TR, LANES), lambda s: (nb_real, 0, 0)),
            _bias_spec(1),
            pl.BlockSpec(wuvt.shape, lambda s: (0, 0, 0)),
        ],
        out_specs=pl.BlockSpec((LANES, B_HEADS * B_VD), blk),
        out_shape=jax.ShapeDtypeStruct((n, B_HEADS * B_VD), bf16),
        scratch_shapes=[
            pltpu.VMEM((B_HEADS * LANES, B_KV_RANK), bf16),
            pltpu.VMEM((IDX_HEADS * LANES, LANES), bf16),
            pltpu.VMEM((nblk + 1, LANES, LANES), jnp.int32),
            pltpu.VMEM((nblk + NSUB, LANES, LANES), f32),
            pltpu.VMEM((1, B_HEADS * LANES), f32),
            pltpu.VMEM((B_TR, B_HEADS * LANES), f32),
            pltpu.VMEM((1, B_HEADS * LANES), f32),
        ],
        compiler_params=_cparams(("arbitrary",)),
        name="attn_b",
    )(qa, qi, wit, ckv, ckvt, kk, ckvm, ckvtm, kkm, ckv, ckvt, bias, wuvt)


def kernel(x, meta_tokens, rel_bias, ln_ffn1, ffn1_wi, ffn1_wo, ln_mix, w_out, ln_ffn2, ffn2_wi, ffn2_wo, a_w_in, a_qk_norm, a_lambda, a_subln, b_w_in, b_latent_norm, b_w_uq, b_q_norm, b_w_uv, c_w_in, c_qk_norm, c_sinks):
    bsz, seq, d = x.shape
    assert d == D_MODEL and seq % LANES == 0 and bsz * N_META == LANES
    nblk = seq // LANES
    n = bsz * seq + LANES
    k_sel = min(TOPK_MAX, seq // 4)
    tm_ffn = _row_tile(n, 1408)
    tm_proj = _row_tile(n, 384, LANES)
    fc = 256

    h = jnp.concatenate([x.reshape(bsz * seq, d),
                         jnp.broadcast_to(meta_tokens.astype(x.dtype), (bsz, N_META, d)).reshape(LANES, d)], axis=0)
    bias = _bias_tiles(rel_bias)

    for layer in range(DEPTH):
        h = _ffn(h, ln_ffn1[layer], ffn1_wi, ffn1_wo, layer, tm=tm_ffn, fc=fc)
        kind, j = layer % N_MIXERS, layer // N_MIXERS
        g = ln_mix[layer]
        if kind == 0:
            lambda_init = 0.8 - 0.6 * math.exp(-0.3 * layer)
            qkv, vt = _proj_a(h, g, a_w_in[j].astype(bf16), a_qk_norm[j], tm=tm_proj)
            mix = _attn_a(qkv, vt, bias, a_lambda[j], a_subln[j], bsz=bsz, nblk=nblk, lambda_init=lambda_init)
        elif kind == 1:
            w = b_w_in[j]
            r2 = B_Q_RANK + B_KV_RANK
            kcol = w[:, r2:r2 + IDX_DIM]
            w1 = jnp.concatenate([w[:, :r2], kcol, kcol, w[:, r2 + IDX_DIM:],
                                  jnp.zeros((d, LANES - IDX_HEADS), w.dtype)], axis=1).astype(bf16)
            assert w1.shape[1] == B_W1
            qa, qi, ckv, ckvt, kk, wit = _proj_b(h, g, w1, b_latent_norm[j], b_w_uq[j].astype(bf16), b_q_norm[j],
                                                 tm=tm_proj)
            wuvt = jnp.swapaxes(b_w_uv[j], 1, 2).astype(bf16)
            mix = _attn_b(qa, qi, wit, ckv, ckvt, kk, bias, wuvt, bsz=bsz, nblk=nblk, k_sel=k_sel)
        else:
            w = c_w_in[j]
            kcols = [w[:, C_QD + gi * C_HD:C_QD + (gi + 1) * C_HD] for gi in range(C_KV_HEADS)]
            voff = C_QD + C_KV_HEADS * C_HD
            vcols = [w[:, voff + gi * C_HD:voff + (gi + 1) * C_HD] for gi in range(C_KV_HEADS)]
            wc = jnp.concatenate([w[:, :C_QD]] + [kc for kc in kcols for _ in range(2)]
                                 + [vc for vc in vcols for _ in range(2)], axis=1).astype(bf16)
            qkv = _proj_c(h, g, wc, c_qk_norm[j], tm=tm_proj)
            mix = _attn_c(qkv, bias, c_sinks[j], bsz=bsz, nblk=nblk)
        h = _ffn(h, ln_ffn2[layer], ffn2_wi, ffn2_wo, layer, tm=tm_ffn, fc=fc,
                 mix=mix, wout=w_out[layer].astype(bf16))
    return h[:bsz * seq].reshape(bsz, seq, d)
```

```python
import functools
import math

import numpy as np
import jax
import jax.numpy as jnp
from jax import lax
from jax.experimental import pallas as pl
from jax.experimental.pallas import tpu as pltpu

D_MODEL = 1024
DEPTH = 4
CHUNK = 64
N_META = 16
N_MIXERS = 3
NEG_INF = -1e30
REL_BUCKETS = 32
REL_MAX_DIST = 128
REL_HEADS = 16
D_FF = 2816
A_HEADS = 8
A_HD = 64
A_VD = 2 * A_HD
B_HEADS = 16
B_Q_RANK = 256
B_KV_RANK = 256
B_VD = 64
IDX_HEADS = 8
IDX_DIM = 64
TOPK_MAX = 256
C_Q_HEADS = 16
C_KV_HEADS = 2
C_GROUP = C_Q_HEADS // C_KV_HEADS
C_HD = 64
EPS = 1e-6

LANES = 128
BF16_ROWS = 16
VMEM_LIMIT = 56 * 1024 * 1024
INT_MIN = -(2 ** 31)
NSUB = 2
LOG2E = math.log2(math.e)
LAZY_GAP = 57.0
LAZY_FLOOR = 2.0 ** -100
ONES_ROWS = BF16_ROWS

KIND_DIAG, KIND_PREV, KIND_FAR, KIND_META0, KIND_METAMETA, KIND_MASKED, KIND_PREVWIN = 0, 1, 2, 3, 4, 5, 6
N_KINDS = 7

f32 = jnp.float32
bf16 = jnp.bfloat16


def _cparams(sem):
    return pltpu.CompilerParams(dimension_semantics=sem, vmem_limit_bytes=VMEM_LIMIT)


def _row_tile(n, cap, mult=BF16_ROWS):
    best = None
    for t in range(mult, cap + 1, mult):
        if n % t == 0:
            best = t
    assert best is not None
    return best


def _dot(a, b):
    return jnp.dot(a, b, preferred_element_type=f32)


def _dot_nt(a, b):
    return lax.dot_general(a, b, (((1,), (1,)), ((), ())), preferred_element_type=f32)


def _rms_rows(x):
    return x * lax.rsqrt(jnp.mean(x * x, axis=-1, keepdims=True) + EPS)


def _lo_half_mask(shape, period, half):
    return (lax.broadcasted_iota(jnp.int32, shape, 1) & (period - 1)) < half


def _group_rms(x, group):
    r, c = x.shape
    outs = []
    if group == 64:
        lo = _lo_half_mask((r, LANES), LANES, 64)
        for ci in range(c // LANES):
            xc = x[:, ci * LANES:(ci + 1) * LANES]
            x2 = xc * xc
            s_lo = jnp.sum(jnp.where(lo, x2, 0.0), axis=-1, keepdims=True)
            s_hi = jnp.sum(jnp.where(lo, 0.0, x2), axis=-1, keepdims=True)
            inv = jnp.where(lo, lax.rsqrt(s_lo * (1.0 / 64) + EPS), lax.rsqrt(s_hi * (1.0 / 64) + EPS))
            outs.append(xc * inv)
    else:
        for gi in range(c // group):
            outs.append(_rms_rows(x[:, gi * group:(gi + 1) * group]))
    return outs[0] if len(outs) == 1 else jnp.concatenate(outs, axis=-1)


def _tile_iotas():
    r = lax.broadcasted_iota(jnp.int32, (LANES, LANES), 0)
    c = lax.broadcasted_iota(jnp.int32, (LANES, LANES), 1)
    return r, c


def _vec(s):
    return jnp.full((LANES, LANES), s, jnp.int32)


def _softmax_step(logits_fn, pv_fn, m_ref, acc_ref, gap_ref=None, offset=None):
    st = logits_fn()
    if gap_ref is None:
        if offset is not None:
            st = st + offset
        m_old = m_ref[...]
        m_new = jnp.maximum(m_old, jnp.max(st, axis=0, keepdims=True))
        acc_ref[...] = jnp.exp2(m_old - m_new) * acc_ref[...] + pv_fn(jnp.exp2(st - m_new).astype(bf16))
        m_ref[...] = m_new
    else:
        shift = m_ref[...] if offset is None else m_ref[...] - offset
        gap_ref[...] = jnp.maximum(gap_ref[...], jnp.max(st, axis=0, keepdims=True) - shift)
        acc_ref[...] += pv_fn(jnp.exp2(st - shift).astype(bf16))


def _softmax_loop(lo, hi, step_fn, m_ref, acc_ref, gap_ref):
    def body(w, carry):
        logits_fn, pv_fn, offset = step_fn(w)
        _softmax_step(logits_fn, pv_fn, m_ref, acc_ref, gap_ref, offset)
        return carry

    lax.fori_loop(lo, hi, body, 0)


def _overshot(gap_ref):
    return jnp.logical_not(jnp.max(gap_ref[...]) <= LAZY_GAP)


def _block_kind(rel):
    return jnp.where(rel < -1, KIND_FAR,
                     jnp.where(rel == -1, KIND_PREV, jnp.where(rel == 0, KIND_DIAG, KIND_MASKED)))


def _far_steps(i):
    return jnp.maximum(i - 1, 0) // NSUB


def _ffn_kernel(*refs, fuse_out):
    if fuse_out:
        h_ref, mix_ref, wout_ref, g_ref, wa_ref, wb_ref, wo_ref, o_ref, xn_ref = refs
    else:
        h_ref, g_ref, wa_ref, wb_ref, wo_ref, o_ref, xn_ref = refs
    j = pl.program_id(1)

    @pl.when(j == 0)
    def _():
        r = h_ref[...]
        if fuse_out:
            r = r + _dot(mix_ref[...], wout_ref[...])
        o_ref[...] = r
        xn_ref[...] = (_rms_rows(r) * g_ref[...]).astype(bf16)

    xn = xn_ref[...]
    a = _dot(xn, wa_ref[...].astype(bf16))
    b = _dot(xn, wb_ref[...].astype(bf16))
    act = (a / (1.0 + jnp.exp(-a)) * b).astype(bf16)
    o_ref[...] += 0.5 * _dot(act, wo_ref[...].astype(bf16))


def _ffn(h, g, wi, wo, layer, *, tm, fc, mix=None, wout=None):
    n, d = h.shape
    dff = wo.shape[1]
    nj = dff // fc
    fuse = mix is not None
    row = lambda i, j: (i, 0)
    in_specs = [pl.BlockSpec((tm, d), row)]
    args = [h]
    if fuse:
        in_specs += [pl.BlockSpec((tm, mix.shape[1]), row), pl.BlockSpec(wout.shape, lambda i, j: (0, 0))]
        args += [mix, wout]
    in_specs += [
        pl.BlockSpec((1, d), lambda i, j: (0, 0)),
        pl.BlockSpec((None, d, fc), lambda i, j: (layer, 0, j)),
        pl.BlockSpec((None, d, fc), lambda i, j: (layer, 0, j + nj)),
        pl.BlockSpec((None, fc, d), lambda i, j: (layer, j, 0)),
    ]
    args += [g.reshape(1, d), wi, wi, wo]
    return pl.pallas_call(
        functools.partial(_ffn_kernel, fuse_out=fuse),
        grid=(n // tm, nj),
        in_specs=in_specs,
        out_specs=pl.BlockSpec((tm, d), row),
        out_shape=jax.ShapeDtypeStruct((n, d), f32),
        scratch_shapes=[pltpu.VMEM((tm, d), bf16)],
        compiler_params=_cparams(("arbitrary", "arbitrary")),
        name="ffn_out" if fuse else "ffn",
    )(*args)


def _rel_bucket(rel):
    half = REL_BUCKETS // 2
    max_exact = half // 2
    n = jnp.abs(rel)
    large = max_exact + (jnp.log(jnp.maximum(n, 1).astype(jnp.float32) / max_exact)
                         / math.log(REL_MAX_DIST / max_exact) * (half - max_exact)).astype(jnp.int32)
    large = jnp.minimum(large, half - 1)
    return jnp.where(rel > 0, half, 0) + jnp.where(n < max_exact, n, large)


def _rel_tiles():
    k = np.arange(LANES)[:, None]
    q = np.arange(LANES)[None, :]
    far = np.full((LANES, LANES), -4 * LANES)
    ones = np.ones((LANES, LANES), bool)
    rels = [k - q, k - q - LANES, far, (k % N_META) - N_META - q, (k % N_META) - (q % N_META), far, k - q - LANES]
    vis = [(k // CHUNK) <= (q // CHUNK), ones, ones, ones, (k // N_META) == (q // N_META), ~ones,
           (q < CHUNK) | (k >= CHUNK)]
    return (np.stack([np.broadcast_to(a, (LANES, LANES)) for a in rels]).astype(np.int32),
            np.stack([np.broadcast_to(a, (LANES, LANES)) for a in vis]).astype(np.int32))


def _bias_kernel(rb_ref, bucket_ref, vis_ref, o_ref):
    h = pl.program_id(0)
    for kind in range(N_KINDS):
        bk = bucket_ref[kind]
        acc = jnp.zeros((LANES, LANES), f32)
        for b in range(REL_BUCKETS):
            acc = jnp.where(bk == b, rb_ref[b, h], acc)
        o_ref[kind, 0] = jnp.where(vis_ref[kind] != 0, acc * LOG2E, NEG_INF)


def _bias_tiles(rel_bias):
    rel, vis = _rel_tiles()
    bucket = _rel_bucket(jnp.asarray(rel))
    nk = N_KINDS
    return pl.pallas_call(
        _bias_kernel,
        grid=(REL_HEADS,),
        in_specs=[
            pl.BlockSpec(memory_space=pltpu.SMEM),
            pl.BlockSpec((nk, LANES, LANES), lambda h: (0, 0, 0)),
            pl.BlockSpec((nk, LANES, LANES), lambda h: (0, 0, 0)),
        ],
        out_specs=pl.BlockSpec((nk, 1, LANES, LANES), lambda h: (0, h, 0, 0)),
        out_shape=jax.ShapeDtypeStruct((nk, REL_HEADS, LANES, LANES), f32),
        compiler_params=_cparams(("arbitrary",)),
        name="bias_tiles",
    )(rel_bias, bucket, jnp.asarray(vis))


def _bias_spec():
    return pl.BlockSpec((N_KINDS, REL_HEADS, LANES, LANES), lambda s: (0, 0, 0, 0))


def _meta_views(rows, cols_t, bsz):
    f = rows.shape[1]
    return (rows.reshape(bsz, N_META, f),
            cols_t.reshape(cols_t.shape[0], bsz, N_META).transpose(1, 0, 2))


A_QD = A_HEADS * 2 * A_HD
A_VR = A_VD + ONES_ROWS


def _proj_a_kernel(h_ref, g_ref, w_ref, gq_ref, gk_ref, o_ref, vt_ref):
    xn = (_rms_rows(h_ref[...]) * g_ref[...]).astype(bf16)
    y = _dot(xn, w_ref[...])
    q = _group_rms(y[:, :A_QD], A_HD) * gq_ref[...] * (A_HD ** -0.5 * LOG2E)
    lo = _lo_half_mask(q.shape, 2 * A_HD, A_HD)
    o_ref[:, :A_QD] = jnp.where(lo, q, 0.0).astype(bf16)
    o_ref[:, A_QD:2 * A_QD] = jnp.where(lo, 0.0, q).astype(bf16)
    o_ref[:, 2 * A_QD:] = (_group_rms(y[:, A_QD:2 * A_QD], A_HD) * gk_ref[...]).astype(bf16)
    ones = jnp.ones((ONES_ROWS, LANES), bf16)
    for t in range(vt_ref.shape[0]):
        vt = y[t * LANES:(t + 1) * LANES, 2 * A_QD:].T.astype(bf16)
        for h in range(A_HEADS):
            vt_ref[t, h * A_VR:h * A_VR + A_VD, :] = vt[h * A_VD:(h + 1) * A_VD]
            vt_ref[t, h * A_VR + A_VD:(h + 1) * A_VR, :] = ones


def _proj_a(h, g, w, qk_norm, *, tm):
    n, d = h.shape
    nw = w.shape[1]
    gq = jnp.tile(qk_norm[0], A_QD // A_HD).reshape(1, A_QD)
    gk = jnp.tile(qk_norm[1], A_QD // A_HD).reshape(1, A_QD)
    return pl.pallas_call(
        _proj_a_kernel,
        grid=(n // tm,),
        in_specs=[
            pl.BlockSpec((tm, d), lambda i: (i, 0)),
            pl.BlockSpec((1, d), lambda i: (0, 0)),
            pl.BlockSpec((d, nw), lambda i: (0, 0)),
            pl.BlockSpec((1, A_QD), lambda i: (0, 0)),
            pl.BlockSpec((1, A_QD), lambda i: (0, 0)),
        ],
        out_specs=[
            pl.BlockSpec((tm, 3 * A_QD), lambda i: (i, 0)),
            pl.BlockSpec((tm // LANES, A_HEADS * A_VR, LANES), lambda i: (i, 0, 0)),
        ],
        out_shape=[
            jax.ShapeDtypeStruct((n, 3 * A_QD), bf16),
            jax.ShapeDtypeStruct((n // LANES, A_HEADS * A_VR, LANES), bf16),
        ],
        compiler_params=_cparams(("arbitrary",)),
        name="proj_a",
    )(h, g.reshape(1, d), w, gq, gk)


def _attn_a_kernel(qlo_ref, qhi_ref, k_ref, vt_ref, km_ref, vtm_ref, kmf_ref, vtmf_ref, bias_ref, lam_ref, sub_ref,
                   o_ref, qs_ref, m_ref, acc_ref, gap_ref, *, nblk, lambda_init):
    s_id = pl.program_id(0)
    nb_real = pl.num_programs(0) - 1
    hw = 2 * A_HD

    def init():
        for h in range(A_HEADS):
            qs_ref[h, :LANES, :] = qlo_ref[:, h * hw:(h + 1) * hw]
            qs_ref[h, LANES:, :] = qhi_ref[:, h * hw:(h + 1) * hw]
        m_ref[...] = jnp.full(m_ref.shape, NEG_INF, f32)
        acc_ref[...] = jnp.zeros(acc_ref.shape, f32)

    def step_fns(kt_fn, vtt_fn, bias_fn):
        def logits():
            sts = [_dot_nt(kt_fn(h), qs_ref[h]) for h in range(A_HEADS)]
            if bias_fn is not None:
                sts = [st + bias_fn(h) for h, st in enumerate(sts)]
            return jnp.concatenate(sts, axis=1)

        def pv(pb):
            return jnp.concatenate(
                [_dot(vtt_fn(h), pb[:, h * 2 * LANES:(h + 1) * 2 * LANES]) for h in range(A_HEADS)], axis=1)

        return logits, pv

    def bias_rows(kind, h, rows):
        return jnp.concatenate([bias_ref[kind, h, 0:rows, :], bias_ref[kind, A_HEADS + h, 0:rows, :]], axis=1)

    def finalize():
        lam = lam_ref[...]
        lam_full = (jnp.exp(jnp.sum(lam[0:1] * lam[1:2], axis=-1, keepdims=True))
                    - jnp.exp(jnp.sum(lam[2:3] * lam[3:4], axis=-1, keepdims=True)) + lambda_init)
        o = acc_ref[0:A_VD, :] / acc_ref[A_VD:A_VD + 1, :]
        for h in range(A_HEADS):
            d = o[:, 2 * h * LANES:(2 * h + 1) * LANES] - lam_full * o[:, (2 * h + 1) * LANES:(2 * h + 2) * LANES]
            d = d * lax.rsqrt(jnp.mean(d * d, axis=0, keepdims=True) + EPS) * sub_ref[...] * (1.0 - lambda_init)
            o_ref[:, h * hw:(h + 1) * hw] = d.T.astype(bf16)

    @pl.when(s_id < nb_real)
    def _():
        i = s_id % nblk
        kind_m = jnp.where(i == 0, KIND_META0, KIND_FAR)
        nfar = _far_steps(i)

        def kv_fns(w):
            off = pl.multiple_of(w * (NSUB * LANES), NSUB * LANES)
            return (lambda h: k_ref[pl.ds(off, NSUB * LANES), h * hw:(h + 1) * hw],
                    lambda h: jnp.concatenate([vt_ref[w * NSUB + t, h * A_VR:(h + 1) * A_VR, :] for t in range(NSUB)],
                                              axis=1))

        def far_step(w):
            far_bias = jnp.concatenate([bias_rows(KIND_FAR, h, 1) for h in range(A_HEADS)], axis=1)
            return step_fns(*kv_fns(w), None) + (far_bias,)

        def near_step(w):
            kinds = [_block_kind(w * NSUB + t - i) for t in range(NSUB)]
            return step_fns(*kv_fns(w), lambda h: jnp.concatenate(
                [bias_rows(kinds[t], h, LANES) for t in range(NSUB)], axis=0)) + (None,)

        def sweep(gap_ref):
            init()
            _softmax_step(*step_fns(lambda h: km_ref[0, :, h * hw:(h + 1) * hw],
                                    lambda h: vtm_ref[0, h * A_VR:(h + 1) * A_VR, :],
                                    lambda h: bias_rows(kind_m, h, N_META)), m_ref, acc_ref)
            _softmax_loop(0, nfar, far_step, m_ref, acc_ref, gap_ref)
            _softmax_loop(nfar, i // NSUB + 1, near_step, m_ref, acc_ref, gap_ref)

        gap_ref[...] = jnp.full(gap_ref.shape, NEG_INF, f32)
        sweep(gap_ref)
        pl.when(_overshot(gap_ref))(lambda: sweep(None))
        finalize()

    @pl.when(s_id == nb_real)
    def _():
        init()
        _softmax_step(*step_fns(lambda h: kmf_ref[:, h * hw:(h + 1) * hw],
                                lambda h: vtmf_ref[0, h * A_VR:(h + 1) * A_VR, :],
                                lambda h: bias_rows(KIND_METAMETA, h, LANES)), m_ref, acc_ref)
        finalize()


def _attn_a(qkv, vt, bias, lam, subln, *, bsz, nblk, lambda_init):
    n = qkv.shape[0]
    nb_real = bsz * nblk
    seq = nblk * LANES
    d = A_QD
    vr = A_HEADS * A_VR
    assert nblk % NSUB == 0
    km, vtm = _meta_views(qkv[nb_real * LANES:, 2 * d:], vt[nb_real], bsz)
    kern = functools.partial(_attn_a_kernel, nblk=nblk, lambda_init=lambda_init)
    bclamp = lambda s: jnp.minimum(s // nblk, bsz - 1)
    return pl.pallas_call(
        kern,
        grid=(nb_real + 1,),
        in_specs=[
            pl.BlockSpec((LANES, d), lambda s: (s, 0)),
            pl.BlockSpec((LANES, d), lambda s: (s, 1)),
            pl.BlockSpec((seq, d), lambda s: (bclamp(s), 2)),
            pl.BlockSpec((nblk, vr, LANES), lambda s: (bclamp(s), 0, 0)),
            pl.BlockSpec((1, N_META, d), lambda s: (bclamp(s), 0, 0)),
            pl.BlockSpec((1, vr, N_META), lambda s: (bclamp(s), 0, 0)),
            pl.BlockSpec((LANES, d), lambda s: (nb_real, 2)),
            pl.BlockSpec((1, vr, LANES), lambda s: (nb_real, 0, 0)),
            _bias_spec(),
            pl.BlockSpec((4, A_HD), lambda s: (0, 0)),
            pl.BlockSpec((A_VD, LANES), lambda s: (0, 0)),
        ],
        out_specs=pl.BlockSpec((LANES, d), lambda s: (s, 0)),
        out_shape=jax.ShapeDtypeStruct((n, d), bf16),
        scratch_shapes=[
            pltpu.VMEM((A_HEADS, 2 * LANES, A_VD), bf16),
            pltpu.VMEM((1, A_HEADS * 2 * LANES), f32),
            pltpu.VMEM((A_VR, A_HEADS * 2 * LANES), f32),
            pltpu.VMEM((1, A_HEADS * 2 * LANES), f32),
        ],
        compiler_params=_cparams(("arbitrary",)),
        name="attn_a",
    )(qkv, qkv, qkv, vt, km, vtm, qkv, vt, bias, lam, jnp.broadcast_to(subln[:, None], (A_VD, LANES)))


C_QD = C_Q_HEADS * C_HD
C_KD = 2 * C_KV_HEADS * C_HD
C_VR = 2 * C_HD + ONES_ROWS


def _proj_c_kernel(h_ref, g_ref, w_ref, gq_ref, gk_ref, o_ref, vt_ref):
    xn = (_rms_rows(h_ref[...]) * g_ref[...]).astype(bf16)
    y = _dot(xn, w_ref[...])
    q = _group_rms(y[:, :C_QD], C_HD) * gq_ref[...] * (C_HD ** -0.5 * LOG2E)
    lo = _lo_half_mask(q.shape, 2 * C_HD, C_HD)
    o_ref[:, :C_QD] = jnp.where(lo, q, 0.0).astype(bf16)
    o_ref[:, C_QD:2 * C_QD] = jnp.where(lo, 0.0, q).astype(bf16)
    o_ref[:, 2 * C_QD:] = (_group_rms(y[:, C_QD:C_QD + C_KD], C_HD) * gk_ref[...]).astype(bf16)
    ones = jnp.ones((ONES_ROWS, LANES), bf16)
    for t in range(vt_ref.shape[0]):
        vt = y[t * LANES:(t + 1) * LANES, C_QD + C_KD:].T.astype(bf16)
        for g in range(C_KV_HEADS):
            vt_ref[t, g * C_VR:g * C_VR + 2 * C_HD, :] = vt[g * 2 * C_HD:(g + 1) * 2 * C_HD]
            vt_ref[t, g * C_VR + 2 * C_HD:(g + 1) * C_VR, :] = ones


def _proj_c(h, g, w, qk_norm, *, tm):
    n, d = h.shape
    nw = w.shape[1]
    nout = 2 * C_QD + C_KD
    gq = jnp.tile(qk_norm[0], C_QD // C_HD).reshape(1, C_QD)
    gk = jnp.tile(qk_norm[1], C_KD // C_HD).reshape(1, C_KD)
    return pl.pallas_call(
        _proj_c_kernel,
        grid=(n // tm,),
        in_specs=[
            pl.BlockSpec((tm, d), lambda i: (i, 0)),
            pl.BlockSpec((1, d), lambda i: (0, 0)),
            pl.BlockSpec((d, nw), lambda i: (0, 0)),
            pl.BlockSpec((1, C_QD), lambda i: (0, 0)),
            pl.BlockSpec((1, C_KD), lambda i: (0, 0)),
        ],
        out_specs=[
            pl.BlockSpec((tm, nout), lambda i: (i, 0)),
            pl.BlockSpec((tm // LANES, C_KV_HEADS * C_VR, LANES), lambda i: (i, 0, 0)),
        ],
        out_shape=[
            jax.ShapeDtypeStruct((n, nout), bf16),
            jax.ShapeDtypeStruct((n // LANES, C_KV_HEADS * C_VR, LANES), bf16),
        ],
        compiler_params=_cparams(("arbitrary",)),
        name="proj_c",
    )(h, g.reshape(1, d), w, gq, gk)


def _attn_c_kernel(sink_ref, qlo_ref, qhi_ref, k_ref, vt_ref, km_ref, vtm_ref, kmf_ref, vtmf_ref, bias_ref,
                   o_ref, qs_ref, *, nblk):
    s_id = pl.program_id(0)
    nb_real = pl.num_programs(0) - 1
    r_io, _ = _tile_iotas()
    vd = 2 * C_HD

    def attend(tiles):
        top = r_io < C_HD
        for g in range(C_KV_HEADS):
            for hh in range(C_GROUP):
                cc = (g * C_GROUP + hh) // 2
                src = qlo_ref if hh % 2 == 0 else qhi_ref
                qs_ref[hh * LANES:(hh + 1) * LANES, :] = src[:, cc * LANES:(cc + 1) * LANES]
            sink = jnp.concatenate(
                [jnp.full((1, LANES), sink_ref[g * C_GROUP + hh] * LOG2E, f32) for hh in range(C_GROUP)], axis=1)
            sts = []
            m = sink
            for (k_fn, vt_fn, bias_fn) in tiles:
                st = _dot_nt(k_fn(g), qs_ref[...])
                st = st + jnp.concatenate([bias_fn(g * C_GROUP + hh) for hh in range(C_GROUP)], axis=1)
                m = jnp.maximum(m, jnp.max(st, axis=0, keepdims=True))
                sts.append(st)
            acc = None
            for st, (k_fn, vt_fn, bias_fn) in zip(sts, tiles):
                pv = _dot(vt_fn(g), jnp.exp2(st - m).astype(bf16))
                acc = pv if acc is None else acc + pv
            o = acc[0:vd, :] / (acc[vd:vd + 1, :] + jnp.exp2(sink - m))
            for cc in range(C_GROUP // 2):
                even = o[:, (2 * cc) * LANES:(2 * cc + 1) * LANES]
                odd = o[:, (2 * cc + 1) * LANES:(2 * cc + 2) * LANES]
                col = (g * (C_GROUP // 2) + cc) * LANES
                o_ref[:, col:col + LANES] = jnp.where(top, even, odd).T.astype(bf16)

    @pl.when(s_id < nb_real)
    def _():
        i = s_id % nblk
        prev = jnp.maximum(i - 1, 0)
        poff = pl.multiple_of(prev * LANES, LANES)
        coff = pl.multiple_of(i * LANES, LANES)
        kind_m = jnp.where(i == 0, KIND_META0, KIND_FAR)
        kind_p = jnp.where(i == 0, KIND_MASKED, KIND_PREVWIN)
        attend([
            (lambda g: km_ref[0, :, g * LANES:(g + 1) * LANES], lambda g: vtm_ref[0, g * C_VR:(g + 1) * C_VR, :],
             lambda h: bias_ref[kind_m, h, 0:N_META, :]),
            (lambda g: k_ref[pl.ds(poff, LANES), g * LANES:(g + 1) * LANES],
             lambda g: vt_ref[prev, g * C_VR:(g + 1) * C_VR, :], lambda h: bias_ref[kind_p, h]),
            (lambda g: k_ref[pl.ds(coff, LANES), g * LANES:(g + 1) * LANES],
             lambda g: vt_ref[i, g * C_VR:(g + 1) * C_VR, :], lambda h: bias_ref[KIND_DIAG, h]),
        ])

    @pl.when(s_id == nb_real)
    def _():
        attend([(lambda g: kmf_ref[:, g * LANES:(g + 1) * LANES], lambda g: vtmf_ref[0, g * C_VR:(g + 1) * C_VR, :],
                 lambda h: bias_ref[KIND_METAMETA, h])])


def _attn_c(qkv, vt, bias, sinks, *, bsz, nblk):
    n = qkv.shape[0]
    nb_real = bsz * nblk
    seq = nblk * LANES
    vr = C_KV_HEADS * C_VR
    kern = functools.partial(_attn_c_kernel, nblk=nblk)
    bclamp = lambda s: jnp.minimum(s // nblk, bsz - 1)
    kcol = 2 * C_QD // C_KD
    km, vtm = _meta_views(qkv[nb_real * LANES:, 2 * C_QD:], vt[nb_real], bsz)
    return pl.pallas_call(
        kern,
        grid=(nb_real + 1,),
        in_specs=[
            pl.BlockSpec(memory_space=pltpu.SMEM),
            pl.BlockSpec((LANES, C_QD), lambda s: (s, 0)),
            pl.BlockSpec((LANES, C_QD), lambda s: (s, 1)),
            pl.BlockSpec((seq, C_KD), lambda s: (bclamp(s), kcol)),
            pl.BlockSpec((nblk, vr, LANES), lambda s: (bclamp(s), 0, 0)),
            pl.BlockSpec((1, N_META, C_KD), lambda s: (bclamp(s), 0, 0)),
            pl.BlockSpec((1, vr, N_META), lambda s: (bclamp(s), 0, 0)),
            pl.BlockSpec((LANES, C_KD), lambda s: (nb_real, kcol)),
            pl.BlockSpec((1, vr, LANES), lambda s: (nb_real, 0, 0)),
            _bias_spec(),
        ],
        out_specs=pl.BlockSpec((LANES, C_QD), lambda s: (s, 0)),
        out_shape=jax.ShapeDtypeStruct((n, C_QD), bf16),
        scratch_shapes=[pltpu.VMEM((C_GROUP * LANES, LANES), bf16)],
        compiler_params=_cparams(("arbitrary",)),
        name="attn_c",
    )(sinks, qkv, qkv, qkv, vt, km, vtm, qkv, vt, bias)


B_QA = B_HEADS * B_KV_RANK
B_QI = IDX_HEADS * IDX_DIM
B_W1 = 2 * B_Q_RANK + 2 * LANES
B_TR = B_KV_RANK + ONES_ROWS


def _proj_b_kernel(h_ref, g_ref, w1_ref, ln_ref, wuq_ref, qn_ref,
                   qa_ref, qi_ref, ckv_ref, ckvt_ref, kk_ref, wit_ref):
    xn = (_rms_rows(h_ref[...]) * g_ref[...]).astype(bf16)
    y = _dot(xn, w1_ref[...])
    r = B_Q_RANK
    cq = (_rms_rows(y[:, :r]) * ln_ref[0:1, :]).astype(bf16)
    ckv = _rms_rows(y[:, r:2 * r]) * ln_ref[1:2, :]
    ckv_ref[...] = ckv.astype(bf16)
    kk_ref[...] = _rms_rows(y[:, 2 * r:2 * r + LANES]).astype(bf16)
    wi = y[:, 2 * r + LANES:] * (IDX_HEADS ** -0.5)
    ones = jnp.ones((ONES_ROWS, LANES), bf16)
    for t in range(ckvt_ref.shape[0]):
        ckvt_ref[t, 0:r, :] = ckv[t * LANES:(t + 1) * LANES, :].T.astype(bf16)
        ckvt_ref[t, r:, :] = ones
        wit_ref[t] = wi[t * LANES:(t + 1) * LANES, :].T[0:IDX_HEADS, :]
    z = _dot(cq, wuq_ref[...])
    qa_ref[...] = (_group_rms(z[:, :B_QA], B_KV_RANK) * qn_ref[...] * (B_KV_RANK ** -0.5 * LOG2E)).astype(bf16)
    qi = z[:, B_QA:] * (IDX_DIM ** -0.5)
    lo = _lo_half_mask(qi.shape, 2 * IDX_DIM, IDX_DIM)
    qi_ref[:, :B_QI] = jnp.where(lo, qi, 0.0).astype(bf16)
    qi_ref[:, B_QI:] = jnp.where(lo, 0.0, qi).astype(bf16)


def _proj_b(h, g, w1, latent_norm, wuq, q_norm, *, tm):
    n, d = h.shape
    qn = jnp.tile(q_norm, B_HEADS).reshape(1, B_QA)
    row = lambda i: (i, 0)
    row3 = lambda i: (i, 0, 0)
    const = lambda i: (0, 0)
    nt = tm // LANES
    return pl.pallas_call(
        _proj_b_kernel,
        grid=(n // tm,),
        in_specs=[
            pl.BlockSpec((tm, d), row),
            pl.BlockSpec((1, d), const),
            pl.BlockSpec(w1.shape, const),
            pl.BlockSpec(latent_norm.shape, const),
            pl.BlockSpec(wuq.shape, const),
            pl.BlockSpec((1, B_QA), const),
        ],
        out_specs=[
            pl.BlockSpec((tm, B_QA), row),
            pl.BlockSpec((tm, 2 * B_QI), row),
            pl.BlockSpec((tm, B_KV_RANK), row),
            pl.BlockSpec((nt, B_TR, LANES), row3),
            pl.BlockSpec((tm, LANES), row),
            pl.BlockSpec((nt, IDX_HEADS, LANES), row3),
        ],
        out_shape=[
            jax.ShapeDtypeStruct((n, B_QA), bf16),
            jax.ShapeDtypeStruct((n, 2 * B_QI), bf16),
            jax.ShapeDtypeStruct((n, B_KV_RANK), bf16),
            jax.ShapeDtypeStruct((n // LANES, B_TR, LANES), bf16),
            jax.ShapeDtypeStruct((n, LANES), bf16),
            jax.ShapeDtypeStruct((n // LANES, IDX_HEADS, LANES), f32),
        ],
        compiler_params=_cparams(("arbitrary",)),
        name="proj_b",
    )(h, g.reshape(1, d), w1, latent_norm, wuq, qn)


def _attn_b_kernel(qa_ref, qi_ref, wit_ref, ckv_ref, ckvt_ref, kk_ref, ckvm_ref, ckvtm_ref, kkm_ref,
                   ckvmf_ref, ckvtmf_ref, bias_ref, wuvt_ref,
                   o_ref, qs_ref, is_ref, key_ref, pen_ref, m_ref, acc_ref, gap_ref, *, nblk, k_sel):
    s_id = pl.program_id(0)
    nb_real = pl.num_programs(0) - 1
    r_io, c_io = _tile_iotas()
    rk = B_KV_RANK

    def init():
        for h in range(B_HEADS):
            qs_ref[h * LANES:(h + 1) * LANES, :] = qa_ref[:, h * rk:(h + 1) * rk]
        m_ref[...] = jnp.full(m_ref.shape, NEG_INF, f32)
        acc_ref[...] = jnp.zeros(acc_ref.shape, f32)

    def step_fns(ckv_fn, ckvt_fn, bias_fn, pen_fn):
        def logits():
            st = _dot_nt(ckv_fn(), qs_ref[...])
            pen = None if pen_fn is None else pen_fn()
            cols = []
            for h in range(B_HEADS):
                add = pen if bias_fn is None else (bias_fn(h) if pen is None else bias_fn(h) + pen)
                cols.append(st[:, h * LANES:(h + 1) * LANES] + add)
            return jnp.concatenate(cols, axis=1)

        return logits, lambda pb: _dot(ckvt_fn(), pb)

    def finalize():
        olat = (acc_ref[0:rk, :] / acc_ref[rk:rk + 1, :]).astype(bf16)
        ot = jnp.concatenate([_dot(wuvt_ref[h], olat[:, h * LANES:(h + 1) * LANES]) for h in range(B_HEADS)], axis=0)
        o_ref[...] = ot.T.astype(bf16)

    def index_scores(kk):
        s = jnp.maximum(_dot_nt(kk, is_ref[...]), 0.0)
        wt = wit_ref[0]
        sc = jnp.zeros((kk.shape[0], LANES), f32)
        for hh in range(IDX_HEADS):
            sc = sc + wt[hh:hh + 1, :] * s[:, hh * LANES:(hh + 1) * LANES]
        return sc

    def sort_key(sc):
        bits = lax.bitcast_convert_type(sc + 0.0, jnp.int32)
        return jnp.where(bits < 0, bits ^ jnp.int32(0x7FFFFFFF), bits)

    @pl.when(s_id < nb_real)
    def _():
        i = s_id % nblk
        ntile = i + 2
        init()
        for hh in range(IDX_HEADS):
            base = (hh % 2) * B_QI + (hh // 2) * LANES
            is_ref[hh * LANES:(hh + 1) * LANES, :] = qi_ref[:, base:base + LANES]

        key_ref[0] = jnp.full((LANES, LANES), INT_MIN, jnp.int32)
        key_ref[0, 0:N_META, :] = sort_key(index_scores(kkm_ref[0]))

        def score_body(j, carry):
            off = pl.multiple_of(j * LANES, LANES)
            vis = (_vec(j) < i) | ((r_io >> 6) <= (c_io >> 6))
            key_ref[j + 1] = jnp.where(vis, sort_key(index_scores(kk_ref[pl.ds(off, LANES), :])), jnp.int32(INT_MIN))
            return carry

        lax.fori_loop(0, i + 1, score_body, 0)

        def count(pred):
            def cbody(t, accv):
                return accv + jnp.where(pred(key_ref[t], t), 1.0, 0.0)
            accv = lax.fori_loop(0, ntile, cbody, jnp.zeros((LANES, LANES), f32))
            return jnp.sum(accv, axis=0, keepdims=True)

        kf = float(k_sel)
        zero = jnp.zeros((1, LANES), jnp.int32)
        t0 = jnp.where(count(lambda k, t: k >= zero) >= kf, zero, jnp.int32(INT_MIN))

        def bit_body(it, tcur):
            cand = tcur | jnp.left_shift(jnp.int32(1), 30 - it)
            return jnp.where(count(lambda k, t: k >= cand) >= kf, cand, tcur)

        thr = lax.fori_loop(0, 31, bit_body, t0)

        need = kf - count(lambda k, t: k > thr)
        n_eq = count(lambda k, t: k == thr)
        has_thr = thr > jnp.int32(INT_MIN)
        tied = jnp.max(jnp.where(has_thr & (n_eq > need), 1.0, 0.0)) > 0.0

        def tie_search(_):
            def jbody(it, jcur):
                cand = jcur | jnp.left_shift(jnp.int32(1), 11 - it)
                cnt = count(lambda k, t: (k == thr) & ((t * LANES + r_io) < cand))
                return jnp.where(cnt < need, cand, jcur)
            return lax.fori_loop(0, 12, jbody, jnp.zeros((1, LANES), jnp.int32))

        j_last = lax.cond(tied, tie_search, lambda _: jnp.full((1, LANES), 4095, jnp.int32), 0)
        j_last = jnp.where(has_thr, j_last, -1)

        def pen_body(t, carry):
            k = key_ref[t]
            sel = (k > thr) | ((k == thr) & ((t * LANES + r_io) <= j_last))
            pen_ref[t] = jnp.where(sel, 0.0, NEG_INF)
            return carry

        lax.fori_loop(0, ntile, pen_body, 0)
        for t in range(1, NSUB):
            pen_ref[i + 1 + t] = jnp.full((LANES, LANES), NEG_INF, f32)

        kind_m = jnp.where(i == 0, KIND_META0, KIND_FAR)
        nfar = _far_steps(i)

        def kv_fns(w):
            off = pl.multiple_of(w * (NSUB * LANES), NSUB * LANES)
            return (lambda: ckv_ref[pl.ds(off, NSUB * LANES), :],
                    lambda: jnp.concatenate([ckvt_ref[w * NSUB + t] for t in range(NSUB)], axis=1),
                    lambda: jnp.concatenate([pen_ref[w * NSUB + t + 1] for t in range(NSUB)], axis=0))

        def far_step(w):
            ckv_fn, ckvt_fn, pen_fn = kv_fns(w)
            far_bias = jnp.concatenate([bias_ref[KIND_FAR, h, 0:1, :] for h in range(B_HEADS)], axis=1)
            return step_fns(ckv_fn, ckvt_fn, None, pen_fn) + (far_bias,)

        def near_step(w):
            ckv_fn, ckvt_fn, pen_fn = kv_fns(w)
            kinds = [_block_kind(w * NSUB + t - i) for t in range(NSUB)]
            return step_fns(ckv_fn, ckvt_fn,
                            lambda h: jnp.concatenate([bias_ref[kinds[t], h] for t in range(NSUB)], axis=0),
                            pen_fn) + (None,)

        def meta_fns(with_pen):
            return step_fns(lambda: ckvm_ref[0], lambda: ckvtm_ref[0], lambda h: bias_ref[kind_m, h, 0:N_META, :],
                            (lambda: pen_ref[0, 0:N_META, :]) if with_pen else None)

        def sweep(gap_ref):
            acc_ref[...] = jnp.zeros(acc_ref.shape, f32)
            if gap_ref is None:
                m_ref[...] = jnp.full(m_ref.shape, NEG_INF, f32)
            else:
                m_ref[...] = jnp.max(meta_fns(False)[0](), axis=0, keepdims=True)
            _softmax_step(*meta_fns(True), m_ref, acc_ref)
            _softmax_loop(0, nfar, far_step, m_ref, acc_ref, gap_ref)
            _softmax_loop(nfar, i // NSUB + 1, near_step, m_ref, acc_ref, gap_ref)

        gap_ref[...] = jnp.full(gap_ref.shape, NEG_INF, f32)
        sweep(gap_ref)
        faded = jnp.logical_not(jnp.min(acc_ref[rk:rk + 1, :]) >= LAZY_FLOOR)
        pl.when(_overshot(gap_ref) | faded)(lambda: sweep(None))
        finalize()

    @pl.when(s_id == nb_real)
    def _():
        init()
        _softmax_step(*step_fns(lambda: ckvmf_ref[...], lambda: ckvtmf_ref[0],
                                lambda h: bias_ref[KIND_METAMETA, h], None), m_ref, acc_ref)
        finalize()


def _attn_b(qa, qi, wit, ckv, ckvt, kk, bias, wuvt, *, bsz, nblk, k_sel):
    n = qa.shape[0]
    nb_real = bsz * nblk
    seq = nblk * LANES
    assert k_sel >= N_META and (nblk + 1) * LANES <= 4096 and nblk % NSUB == 0
    ckvm, ckvtm = _meta_views(ckv[nb_real * LANES:], ckvt[nb_real], bsz)
    kkm = kk[nb_real * LANES:].reshape(bsz, N_META, LANES)
    kern = functools.partial(_attn_b_kernel, nblk=nblk, k_sel=k_sel)
    bidx = lambda s: jnp.minimum(s // nblk, bsz - 1)
    blk = lambda s: (s, 0)
    return pl.pallas_call(
        kern,
        grid=(nb_real + 1,),
        in_specs=[
            pl.BlockSpec((LANES, B_QA), blk),
            pl.BlockSpec((LANES, 2 * B_QI), blk),
            pl.BlockSpec((1, IDX_HEADS, LANES), lambda s: (s, 0, 0)),
            pl.BlockSpec((seq, B_KV_RANK), lambda s: (bidx(s), 0)),
            pl.BlockSpec((nblk, B_TR, LANES), lambda s: (bidx(s), 0, 0)),
            pl.BlockSpec((seq, LANES), lambda s: (bidx(s), 0)),
            pl.BlockSpec((1, N_META, B_KV_RANK), lambda s: (bidx(s), 0, 0)),
            pl.BlockSpec((1, B_TR, N_META), lambda s: (bidx(s), 0, 0)),
            pl.BlockSpec((1, N_META, LANES), lambda s: (bidx(s), 0, 0)),
            pl.BlockSpec((LANES, B_KV_RANK), lambda s: (nb_real, 0)),
            pl.BlockSpec((1, B_TR, LANES), lambda s: (nb_real, 0, 0)),
            _bias_spec(),
            pl.BlockSpec(wuvt.shape, lambda s: (0, 0, 0)),
        ],
        out_specs=pl.BlockSpec((LANES, B_HEADS * B_VD), blk),
        out_shape=jax.ShapeDtypeStruct((n, B_HEADS * B_VD), bf16),
        scratch_shapes=[
            pltpu.VMEM((B_HEADS * LANES, B_KV_RANK), bf16),
            pltpu.VMEM((IDX_HEADS * LANES, LANES), bf16),
            pltpu.VMEM((nblk + 1, LANES, LANES), jnp.int32),
            pltpu.VMEM((nblk + NSUB, LANES, LANES), f32),
            pltpu.VMEM((1, B_HEADS * LANES), f32),
            pltpu.VMEM((B_TR, B_HEADS * LANES), f32),
            pltpu.VMEM((1, B_HEADS * LANES), f32),
        ],
        compiler_params=_cparams(("arbitrary",)),
        name="attn_b",
    )(qa, qi, wit, ckv, ckvt, kk, ckvm, ckvtm, kkm, ckv, ckvt, bias, wuvt)


def kernel(x, meta_tokens, rel_bias, ln_ffn1, ffn1_wi, ffn1_wo, ln_mix, w_out, ln_ffn2, ffn2_wi, ffn2_wo, a_w_in, a_qk_norm, a_lambda, a_subln, b_w_in, b_latent_norm, b_w_uq, b_q_norm, b_w_uv, c_w_in, c_qk_norm, c_sinks):
    bsz, seq, d = x.shape
    assert d == D_MODEL and seq % LANES == 0 and bsz * N_META == LANES
    nblk = seq // LANES
    n = bsz * seq + LANES
    k_sel = min(TOPK_MAX, seq // 4)
    tm_ffn = _row_tile(n, 1408)
    tm_proj = _row_tile(n, 384, LANES)
    fc = 256

    h = jnp.concatenate([x.reshape(bsz * seq, d),
                         jnp.broadcast_to(meta_tokens.astype(x.dtype), (bsz, N_META, d)).reshape(LANES, d)], axis=0)
    bias = _bias_tiles(rel_bias)

    for layer in range(DEPTH):
        h = _ffn(h, ln_ffn1[layer], ffn1_wi, ffn1_wo, layer, tm=tm_ffn, fc=fc)
        kind, j = layer % N_MIXERS, layer // N_MIXERS
        g = ln_mix[layer]
        if kind == 0:
            lambda_init = 0.8 - 0.6 * math.exp(-0.3 * layer)
            qkv, vt = _proj_a(h, g, a_w_in[j].astype(bf16), a_qk_norm[j], tm=tm_proj)
            mix = _attn_a(qkv, vt, bias, a_lambda[j], a_subln[j], bsz=bsz, nblk=nblk, lambda_init=lambda_init)
        elif kind == 1:
            w = b_w_in[j]
            r2 = B_Q_RANK + B_KV_RANK
            kcol = w[:, r2:r2 + IDX_DIM]
            w1 = jnp.concatenate([w[:, :r2], kcol, kcol, w[:, r2 + IDX_DIM:],
                                  jnp.zeros((d, LANES - IDX_HEADS), w.dtype)], axis=1).astype(bf16)
            assert w1.shape[1] == B_W1
            qa, qi, ckv, ckvt, kk, wit = _proj_b(h, g, w1, b_latent_norm[j], b_w_uq[j].astype(bf16), b_q_norm[j],
                                                 tm=tm_proj)
            wuvt = jnp.swapaxes(b_w_uv[j], 1, 2).astype(bf16)
            mix = _attn_b(qa, qi, wit, ckv, ckvt, kk, bias, wuvt, bsz=bsz, nblk=nblk, k_sel=k_sel)
        else:
            w = c_w_in[j]
            kcols = [w[:, C_QD + gi * C_HD:C_QD + (gi + 1) * C_HD] for gi in range(C_KV_HEADS)]
            voff = C_QD + C_KV_HEADS * C_HD
            vcols = [w[:, voff + gi * C_HD:voff + (gi + 1) * C_HD] for gi in range(C_KV_HEADS)]
            wc = jnp.concatenate([w[:, :C_QD]] + [kc for kc in kcols for _ in range(2)]
                                 + [vc for vc in vcols for _ in range(2)], axis=1).astype(bf16)
            qkv, vt = _proj_c(h, g, wc, c_qk_norm[j], tm=tm_proj)
            mix = _attn_c(qkv, vt, bias, c_sinks[j], bsz=bsz, nblk=nblk)
        h = _ffn(h, ln_ffn2[layer], ffn2_wi, ffn2_wo, layer, tm=tm_ffn, fc=fc,
                 mix=mix, wout=w_out[layer].astype(bf16))
    return h[:bsz * seq].reshape(bsz, seq, d)
```

```python
import functools
import math

import numpy as np
import jax
import jax.numpy as jnp
from jax import lax
from jax.experimental import pallas as pl
from jax.experimental.pallas import tpu as pltpu

D_MODEL = 1024
DEPTH = 4
CHUNK = 64
N_META = 16
N_MIXERS = 3
NEG_INF = -1e30
REL_BUCKETS = 32
REL_MAX_DIST = 128
REL_HEADS = 16
D_FF = 2816
A_HEADS = 8
A_HD = 64
A_VD = 2 * A_HD
B_HEADS = 16
B_Q_RANK = 256
B_KV_RANK = 256
B_VD = 64
IDX_HEADS = 8
IDX_DIM = 64
TOPK_MAX = 256
C_Q_HEADS = 16
C_KV_HEADS = 2
C_GROUP = C_Q_HEADS // C_KV_HEADS
C_HD = 64
EPS = 1e-6

LANES = 128
BF16_ROWS = 16
VMEM_LIMIT = 56 * 1024 * 1024
INT_MIN = -(2 ** 31)
NSUB = 2
NSUB_FAR = 4
LOG2E = math.log2(math.e)
LAZY_GAP = 57.0
LAZY_FLOOR = 2.0 ** -100
ONES_ROWS = BF16_ROWS

KIND_DIAG, KIND_PREV, KIND_FAR, KIND_META0, KIND_METAMETA, KIND_MASKED, KIND_PREVWIN = 0, 1, 2, 3, 4, 5, 6
N_KINDS = 7

f32 = jnp.float32
bf16 = jnp.bfloat16


def _cparams(sem):
    return pltpu.CompilerParams(dimension_semantics=sem, vmem_limit_bytes=VMEM_LIMIT)


def _row_tile(n, cap, mult=BF16_ROWS):
    best = None
    for t in range(mult, cap + 1, mult):
        if n % t == 0:
            best = t
    assert best is not None
    return best


def _dot(a, b):
    return jnp.dot(a, b, preferred_element_type=f32)


def _dot_nt(a, b):
    return lax.dot_general(a, b, (((1,), (1,)), ((), ())), preferred_element_type=f32)


def _rms_rows(x):
    return x * lax.rsqrt(jnp.mean(x * x, axis=-1, keepdims=True) + EPS)


def _lo_half_mask(shape, period, half):
    return (lax.broadcasted_iota(jnp.int32, shape, 1) & (period - 1)) < half


def _group_rms(x, group):
    r, c = x.shape
    outs = []
    if group == 64:
        lo = _lo_half_mask((r, LANES), LANES, 64)
        for ci in range(c // LANES):
            xc = x[:, ci * LANES:(ci + 1) * LANES]
            x2 = xc * xc
            s_lo = jnp.sum(jnp.where(lo, x2, 0.0), axis=-1, keepdims=True)
            s_hi = jnp.sum(jnp.where(lo, 0.0, x2), axis=-1, keepdims=True)
            inv = jnp.where(lo, lax.rsqrt(s_lo * (1.0 / 64) + EPS), lax.rsqrt(s_hi * (1.0 / 64) + EPS))
            outs.append(xc * inv)
    else:
        for gi in range(c // group):
            outs.append(_rms_rows(x[:, gi * group:(gi + 1) * group]))
    return outs[0] if len(outs) == 1 else jnp.concatenate(outs, axis=-1)


def _tile_iotas():
    r = lax.broadcasted_iota(jnp.int32, (LANES, LANES), 0)
    c = lax.broadcasted_iota(jnp.int32, (LANES, LANES), 1)
    return r, c


def _vec(s):
    return jnp.full((LANES, LANES), s, jnp.int32)


def _softmax_step(logits_fn, pv_fn, m_ref, acc_ref, gap_ref=None, offset=None):
    st = logits_fn()
    if gap_ref is None:
        if offset is not None:
            st = st + offset
        m_old = m_ref[...]
        m_new = jnp.maximum(m_old, jnp.max(st, axis=0, keepdims=True))
        acc_ref[...] = jnp.exp2(m_old - m_new) * acc_ref[...] + pv_fn(jnp.exp2(st - m_new).astype(bf16))
        m_ref[...] = m_new
    else:
        shift = m_ref[...] if offset is None else m_ref[...] - offset
        gap_ref[...] = jnp.maximum(gap_ref[...], jnp.max(st, axis=0, keepdims=True) - shift)
        acc_ref[...] += pv_fn(jnp.exp2(st - shift).astype(bf16))


def _softmax_loop(lo, hi, step_fn, m_ref, acc_ref, gap_ref):
    def body(w, carry):
        logits_fn, pv_fn, offset = step_fn(w)
        _softmax_step(logits_fn, pv_fn, m_ref, acc_ref, gap_ref, offset)
        return carry

    lax.fori_loop(lo, hi, body, 0)


def _overshot(gap_ref):
    return jnp.logical_not(jnp.max(gap_ref[...]) <= LAZY_GAP)


def _block_kind(rel):
    return jnp.where(rel < -1, KIND_FAR,
                     jnp.where(rel == -1, KIND_PREV, jnp.where(rel == 0, KIND_DIAG, KIND_MASKED)))


def _sweep_steps(i):
    nfar = jnp.maximum(i - 1, 0) // NSUB_FAR
    return nfar, (i - nfar * NSUB_FAR) // NSUB + 1


def _ffn_kernel(*refs, fuse_out):
    if fuse_out:
        h_ref, mix_ref, wout_ref, g_ref, wa_ref, wb_ref, wo_ref, o_ref, xn_ref = refs
    else:
        h_ref, g_ref, wa_ref, wb_ref, wo_ref, o_ref, xn_ref = refs
    j = pl.program_id(1)

    @pl.when(j == 0)
    def _():
        r = h_ref[...]
        if fuse_out:
            r = r + _dot(mix_ref[...], wout_ref[...])
        o_ref[...] = r
        xn_ref[...] = (_rms_rows(r) * g_ref[...]).astype(bf16)

    xn = xn_ref[...]
    a = _dot(xn, wa_ref[...].astype(bf16))
    b = _dot(xn, wb_ref[...].astype(bf16))
    act = (a / (1.0 + jnp.exp(-a)) * b).astype(bf16)
    o_ref[...] += 0.5 * _dot(act, wo_ref[...].astype(bf16))


def _ffn(h, g, wi, wo, layer, *, tm, fc, mix=None, wout=None):
    n, d = h.shape
    dff = wo.shape[1]
    nj = dff // fc
    fuse = mix is not None
    row = lambda i, j: (i, 0)
    in_specs = [pl.BlockSpec((tm, d), row)]
    args = [h]
    if fuse:
        in_specs += [pl.BlockSpec((tm, mix.shape[1]), row), pl.BlockSpec(wout.shape, lambda i, j: (0, 0))]
        args += [mix, wout]
    in_specs += [
        pl.BlockSpec((1, d), lambda i, j: (0, 0)),
        pl.BlockSpec((None, d, fc), lambda i, j: (layer, 0, j)),
        pl.BlockSpec((None, d, fc), lambda i, j: (layer, 0, j + nj)),
        pl.BlockSpec((None, fc, d), lambda i, j: (layer, j, 0)),
    ]
    args += [g.reshape(1, d), wi, wi, wo]
    return pl.pallas_call(
        functools.partial(_ffn_kernel, fuse_out=fuse),
        grid=(n // tm, nj),
        in_specs=in_specs,
        out_specs=pl.BlockSpec((tm, d), row),
        out_shape=jax.ShapeDtypeStruct((n, d), f32),
        scratch_shapes=[pltpu.VMEM((tm, d), bf16)],
        compiler_params=_cparams(("arbitrary", "arbitrary")),
        name="ffn_out" if fuse else "ffn",
    )(*args)


def _rel_bucket(rel):
    half = REL_BUCKETS // 2
    max_exact = half // 2
    n = jnp.abs(rel)
    large = max_exact + (jnp.log(jnp.maximum(n, 1).astype(jnp.float32) / max_exact)
                         / math.log(REL_MAX_DIST / max_exact) * (half - max_exact)).astype(jnp.int32)
    large = jnp.minimum(large, half - 1)
    return jnp.where(rel > 0, half, 0) + jnp.where(n < max_exact, n, large)


def _rel_tiles():
    k = np.arange(LANES)[:, None]
    q = np.arange(LANES)[None, :]
    far = np.full((LANES, LANES), -4 * LANES)
    ones = np.ones((LANES, LANES), bool)
    rels = [k - q, k - q - LANES, far, (k % N_META) - N_META - q, (k % N_META) - (q % N_META), far, k - q - LANES]
    vis = [(k // CHUNK) <= (q // CHUNK), ones, ones, ones, (k // N_META) == (q // N_META), ~ones,
           (q < CHUNK) | (k >= CHUNK)]
    return (np.stack([np.broadcast_to(a, (LANES, LANES)) for a in rels]).astype(np.int32),
            np.stack([np.broadcast_to(a, (LANES, LANES)) for a in vis]).astype(np.int32))


def _bias_kernel(rb_ref, bucket_ref, vis_ref, o_ref):
    h = pl.program_id(0)
    for kind in range(N_KINDS):
        bk = bucket_ref[kind]
        acc = jnp.zeros((LANES, LANES), f32)
        for b in range(REL_BUCKETS):
            acc = jnp.where(bk == b, rb_ref[b, h], acc)
        o_ref[kind, 0] = jnp.where(vis_ref[kind] != 0, acc * LOG2E, NEG_INF)


def _bias_tiles(rel_bias):
    rel, vis = _rel_tiles()
    bucket = _rel_bucket(jnp.asarray(rel))
    nk = N_KINDS
    return pl.pallas_call(
        _bias_kernel,
        grid=(REL_HEADS,),
        in_specs=[
            pl.BlockSpec(memory_space=pltpu.SMEM),
            pl.BlockSpec((nk, LANES, LANES), lambda h: (0, 0, 0)),
            pl.BlockSpec((nk, LANES, LANES), lambda h: (0, 0, 0)),
        ],
        out_specs=pl.BlockSpec((nk, 1, LANES, LANES), lambda h: (0, h, 0, 0)),
        out_shape=jax.ShapeDtypeStruct((nk, REL_HEADS, LANES, LANES), f32),
        compiler_params=_cparams(("arbitrary",)),
        name="bias_tiles",
    )(rel_bias, bucket, jnp.asarray(vis))


def _bias_spec():
    return pl.BlockSpec((N_KINDS, REL_HEADS, LANES, LANES), lambda s: (0, 0, 0, 0))


def _meta_views(rows, cols_t, bsz):
    f = rows.shape[1]
    return (rows.reshape(bsz, N_META, f),
            cols_t.reshape(cols_t.shape[0], bsz, N_META).transpose(1, 0, 2))


A_QD = A_HEADS * 2 * A_HD
A_VR = A_VD + ONES_ROWS


def _proj_a_kernel(h_ref, g_ref, w_ref, gq_ref, gk_ref, o_ref, vt_ref):
    xn = (_rms_rows(h_ref[...]) * g_ref[...]).astype(bf16)
    y = _dot(xn, w_ref[...])
    q = _group_rms(y[:, :A_QD], A_HD) * gq_ref[...] * (A_HD ** -0.5 * LOG2E)
    lo = _lo_half_mask(q.shape, 2 * A_HD, A_HD)
    o_ref[:, :A_QD] = jnp.where(lo, q, 0.0).astype(bf16)
    o_ref[:, A_QD:2 * A_QD] = jnp.where(lo, 0.0, q).astype(bf16)
    o_ref[:, 2 * A_QD:] = (_group_rms(y[:, A_QD:2 * A_QD], A_HD) * gk_ref[...]).astype(bf16)
    ones = jnp.ones((ONES_ROWS, LANES), bf16)
    for t in range(vt_ref.shape[0]):
        vt = y[t * LANES:(t + 1) * LANES, 2 * A_QD:].T.astype(bf16)
        for h in range(A_HEADS):
            vt_ref[t, h * A_VR:h * A_VR + A_VD, :] = vt[h * A_VD:(h + 1) * A_VD]
            vt_ref[t, h * A_VR + A_VD:(h + 1) * A_VR, :] = ones


def _proj_a(h, g, w, qk_norm, *, tm):
    n, d = h.shape
    nw = w.shape[1]
    gq = jnp.tile(qk_norm[0], A_QD // A_HD).reshape(1, A_QD)
    gk = jnp.tile(qk_norm[1], A_QD // A_HD).reshape(1, A_QD)
    return pl.pallas_call(
        _proj_a_kernel,
        grid=(n // tm,),
        in_specs=[
            pl.BlockSpec((tm, d), lambda i: (i, 0)),
            pl.BlockSpec((1, d), lambda i: (0, 0)),
            pl.BlockSpec((d, nw), lambda i: (0, 0)),
            pl.BlockSpec((1, A_QD), lambda i: (0, 0)),
            pl.BlockSpec((1, A_QD), lambda i: (0, 0)),
        ],
        out_specs=[
            pl.BlockSpec((tm, 3 * A_QD), lambda i: (i, 0)),
            pl.BlockSpec((tm // LANES, A_HEADS * A_VR, LANES), lambda i: (i, 0, 0)),
        ],
        out_shape=[
            jax.ShapeDtypeStruct((n, 3 * A_QD), bf16),
            jax.ShapeDtypeStruct((n // LANES, A_HEADS * A_VR, LANES), bf16),
        ],
        compiler_params=_cparams(("arbitrary",)),
        name="proj_a",
    )(h, g.reshape(1, d), w, gq, gk)


def _attn_a_kernel(qlo_ref, qhi_ref, k_ref, vt_ref, km_ref, vtm_ref, kmf_ref, vtmf_ref, bias_ref, lam_ref, sub_ref,
                   o_ref, qs_ref, m_ref, acc_ref, gap_ref, *, nblk, lambda_init):
    s_id = pl.program_id(0)
    nb_real = pl.num_programs(0) - 1
    hw = 2 * A_HD

    def init():
        for h in range(A_HEADS):
            qs_ref[h, :LANES, :] = qlo_ref[:, h * hw:(h + 1) * hw]
            qs_ref[h, LANES:, :] = qhi_ref[:, h * hw:(h + 1) * hw]
        m_ref[...] = jnp.full(m_ref.shape, NEG_INF, f32)
        acc_ref[...] = jnp.zeros(acc_ref.shape, f32)

    def step_fns(kt_fn, vtt_fn, bias_fn):
        def logits():
            sts = [_dot_nt(kt_fn(h), qs_ref[h]) for h in range(A_HEADS)]
            if bias_fn is not None:
                sts = [st + bias_fn(h) for h, st in enumerate(sts)]
            return jnp.concatenate(sts, axis=1)

        def pv(pb):
            return jnp.concatenate(
                [_dot(vtt_fn(h), pb[:, h * 2 * LANES:(h + 1) * 2 * LANES]) for h in range(A_HEADS)], axis=1)

        return logits, pv

    def bias_rows(kind, h, rows):
        return jnp.concatenate([bias_ref[kind, h, 0:rows, :], bias_ref[kind, A_HEADS + h, 0:rows, :]], axis=1)

    def finalize():
        lam = lam_ref[...]
        lam_full = (jnp.exp(jnp.sum(lam[0:1] * lam[1:2], axis=-1, keepdims=True))
                    - jnp.exp(jnp.sum(lam[2:3] * lam[3:4], axis=-1, keepdims=True)) + lambda_init)
        o = acc_ref[0:A_VD, :] / acc_ref[A_VD:A_VD + 1, :]
        for h in range(A_HEADS):
            d = o[:, 2 * h * LANES:(2 * h + 1) * LANES] - lam_full * o[:, (2 * h + 1) * LANES:(2 * h + 2) * LANES]
            d = d * lax.rsqrt(jnp.mean(d * d, axis=0, keepdims=True) + EPS) * sub_ref[...] * (1.0 - lambda_init)
            o_ref[:, h * hw:(h + 1) * hw] = d.T.astype(bf16)

    @pl.when(s_id < nb_real)
    def _():
        i = s_id % nblk
        kind_m = jnp.where(i == 0, KIND_META0, KIND_FAR)
        nfar, nnear = _sweep_steps(i)

        def kv_fns(b0, nsub):
            off = pl.multiple_of(b0 * LANES, NSUB * LANES)
            return (lambda h: k_ref[pl.ds(off, nsub * LANES), h * hw:(h + 1) * hw],
                    lambda h: jnp.concatenate([vt_ref[b0 + t, h * A_VR:(h + 1) * A_VR, :] for t in range(nsub)],
                                              axis=1))

        def far_step(w):
            far_bias = jnp.concatenate([bias_rows(KIND_FAR, h, 1) for h in range(A_HEADS)], axis=1)
            return step_fns(*kv_fns(w * NSUB_FAR, NSUB_FAR), None) + (far_bias,)

        def near_step(u):
            b0 = nfar * NSUB_FAR + u * NSUB
            kinds = [_block_kind(b0 + t - i) for t in range(NSUB)]
            return step_fns(*kv_fns(b0, NSUB), lambda h: jnp.concatenate(
                [bias_rows(kinds[t], h, LANES) for t in range(NSUB)], axis=0)) + (None,)

        def sweep(gap_ref):
            init()
            _softmax_step(*step_fns(lambda h: km_ref[0, :, h * hw:(h + 1) * hw],
                                    lambda h: vtm_ref[0, h * A_VR:(h + 1) * A_VR, :],
                                    lambda h: bias_rows(kind_m, h, N_META)), m_ref, acc_ref)
            _softmax_loop(0, nfar, far_step, m_ref, acc_ref, gap_ref)
            _softmax_loop(0, nnear, near_step, m_ref, acc_ref, gap_ref)

        gap_ref[...] = jnp.full(gap_ref.shape, NEG_INF, f32)
        sweep(gap_ref)
        pl.when(_overshot(gap_ref))(lambda: sweep(None))
        finalize()

    @pl.when(s_id == nb_real)
    def _():
        init()
        _softmax_step(*step_fns(lambda h: kmf_ref[:, h * hw:(h + 1) * hw],
                                lambda h: vtmf_ref[0, h * A_VR:(h + 1) * A_VR, :],
                                lambda h: bias_rows(KIND_METAMETA, h, LANES)), m_ref, acc_ref)
        finalize()


def _attn_a(qkv, vt, bias, lam, subln, *, bsz, nblk, lambda_init):
    n = qkv.shape[0]
    nb_real = bsz * nblk
    seq = nblk * LANES
    d = A_QD
    vr = A_HEADS * A_VR
    assert nblk % NSUB == 0
    km, vtm = _meta_views(qkv[nb_real * LANES:, 2 * d:], vt[nb_real], bsz)
    kern = functools.partial(_attn_a_kernel, nblk=nblk, lambda_init=lambda_init)
    bclamp = lambda s: jnp.minimum(s // nblk, bsz - 1)
    return pl.pallas_call(
        kern,
        grid=(nb_real + 1,),
        in_specs=[
            pl.BlockSpec((LANES, d), lambda s: (s, 0)),
            pl.BlockSpec((LANES, d), lambda s: (s, 1)),
            pl.BlockSpec((seq, d), lambda s: (bclamp(s), 2)),
            pl.BlockSpec((nblk, vr, LANES), lambda s: (bclamp(s), 0, 0)),
            pl.BlockSpec((1, N_META, d), lambda s: (bclamp(s), 0, 0)),
            pl.BlockSpec((1, vr, N_META), lambda s: (bclamp(s), 0, 0)),
            pl.BlockSpec((LANES, d), lambda s: (nb_real, 2)),
            pl.BlockSpec((1, vr, LANES), lambda s: (nb_real, 0, 0)),
            _bias_spec(),
            pl.BlockSpec((4, A_HD), lambda s: (0, 0)),
            pl.BlockSpec((A_VD, LANES), lambda s: (0, 0)),
        ],
        out_specs=pl.BlockSpec((LANES, d), lambda s: (s, 0)),
        out_shape=jax.ShapeDtypeStruct((n, d), bf16),
        scratch_shapes=[
            pltpu.VMEM((A_HEADS, 2 * LANES, A_VD), bf16),
            pltpu.VMEM((1, A_HEADS * 2 * LANES), f32),
            pltpu.VMEM((A_VR, A_HEADS * 2 * LANES), f32),
            pltpu.VMEM((1, A_HEADS * 2 * LANES), f32),
        ],
        compiler_params=_cparams(("arbitrary",)),
        name="attn_a",
    )(qkv, qkv, qkv, vt, km, vtm, qkv, vt, bias, lam, jnp.broadcast_to(subln[:, None], (A_VD, LANES)))


C_QD = C_Q_HEADS * C_HD
C_KD = 2 * C_KV_HEADS * C_HD
C_VR = 2 * C_HD + ONES_ROWS


def _proj_c_kernel(h_ref, g_ref, w_ref, gq_ref, gk_ref, o_ref, vt_ref):
    xn = (_rms_rows(h_ref[...]) * g_ref[...]).astype(bf16)
    y = _dot(xn, w_ref[...])
    q = _group_rms(y[:, :C_QD], C_HD) * gq_ref[...] * (C_HD ** -0.5 * LOG2E)
    lo = _lo_half_mask(q.shape, 2 * C_HD, C_HD)
    o_ref[:, :C_QD] = jnp.where(lo, q, 0.0).astype(bf16)
    o_ref[:, C_QD:2 * C_QD] = jnp.where(lo, 0.0, q).astype(bf16)
    o_ref[:, 2 * C_QD:] = (_group_rms(y[:, C_QD:C_QD + C_KD], C_HD) * gk_ref[...]).astype(bf16)
    ones = jnp.ones((ONES_ROWS, LANES), bf16)
    for t in range(vt_ref.shape[0]):
        vt = y[t * LANES:(t + 1) * LANES, C_QD + C_KD:].T.astype(bf16)
        for g in range(C_KV_HEADS):
            vt_ref[t, g * C_VR:g * C_VR + 2 * C_HD, :] = vt[g * 2 * C_HD:(g + 1) * 2 * C_HD]
            vt_ref[t, g * C_VR + 2 * C_HD:(g + 1) * C_VR, :] = ones


def _proj_c(h, g, w, qk_norm, *, tm):
    n, d = h.shape
    nw = w.shape[1]
    nout = 2 * C_QD + C_KD
    gq = jnp.tile(qk_norm[0], C_QD // C_HD).reshape(1, C_QD)
    gk = jnp.tile(qk_norm[1], C_KD // C_HD).reshape(1, C_KD)
    return pl.pallas_call(
        _proj_c_kernel,
        grid=(n // tm,),
        in_specs=[
            pl.BlockSpec((tm, d), lambda i: (i, 0)),
            pl.BlockSpec((1, d), lambda i: (0, 0)),
            pl.BlockSpec((d, nw), lambda i: (0, 0)),
            pl.BlockSpec((1, C_QD), lambda i: (0, 0)),
            pl.BlockSpec((1, C_KD), lambda i: (0, 0)),
        ],
        out_specs=[
            pl.BlockSpec((tm, nout), lambda i: (i, 0)),
            pl.BlockSpec((tm // LANES, C_KV_HEADS * C_VR, LANES), lambda i: (i, 0, 0)),
        ],
        out_shape=[
            jax.ShapeDtypeStruct((n, nout), bf16),
            jax.ShapeDtypeStruct((n // LANES, C_KV_HEADS * C_VR, LANES), bf16),
        ],
        compiler_params=_cparams(("arbitrary",)),
        name="proj_c",
    )(h, g.reshape(1, d), w, gq, gk)


def _attn_c_kernel(sink_ref, qlo_ref, qhi_ref, k_ref, vt_ref, km_ref, vtm_ref, kmf_ref, vtmf_ref, bias_ref,
                   o_ref, qs_ref, *, nblk):
    s_id = pl.program_id(0)
    nb_real = pl.num_programs(0) - 1
    r_io, _ = _tile_iotas()
    vd = 2 * C_HD

    def attend(tiles):
        top = r_io < C_HD
        for g in range(C_KV_HEADS):
            for hh in range(C_GROUP):
                cc = (g * C_GROUP + hh) // 2
                src = qlo_ref if hh % 2 == 0 else qhi_ref
                qs_ref[hh * LANES:(hh + 1) * LANES, :] = src[:, cc * LANES:(cc + 1) * LANES]
            sink = jnp.concatenate(
                [jnp.full((1, LANES), sink_ref[g * C_GROUP + hh] * LOG2E, f32) for hh in range(C_GROUP)], axis=1)
            sts = []
            m = sink
            for (k_fn, vt_fn, bias_fn) in tiles:
                st = _dot_nt(k_fn(g), qs_ref[...])
                st = st + jnp.concatenate([bias_fn(g * C_GROUP + hh) for hh in range(C_GROUP)], axis=1)
                m = jnp.maximum(m, jnp.max(st, axis=0, keepdims=True))
                sts.append(st)
            acc = None
            for st, (k_fn, vt_fn, bias_fn) in zip(sts, tiles):
                pv = _dot(vt_fn(g), jnp.exp2(st - m).astype(bf16))
                acc = pv if acc is None else acc + pv
            o = acc[0:vd, :] / (acc[vd:vd + 1, :] + jnp.exp2(sink - m))
            for cc in range(C_GROUP // 2):
                even = o[:, (2 * cc) * LANES:(2 * cc + 1) * LANES]
                odd = o[:, (2 * cc + 1) * LANES:(2 * cc + 2) * LANES]
                col = (g * (C_GROUP // 2) + cc) * LANES
                o_ref[:, col:col + LANES] = jnp.where(top, even, odd).T.astype(bf16)

    @pl.when(s_id < nb_real)
    def _():
        i = s_id % nblk
        prev = jnp.maximum(i - 1, 0)
        poff = pl.multiple_of(prev * LANES, LANES)
        coff = pl.multiple_of(i * LANES, LANES)
        kind_m = jnp.where(i == 0, KIND_META0, KIND_FAR)
        kind_p = jnp.where(i == 0, KIND_MASKED, KIND_PREVWIN)
        attend([
            (lambda g: km_ref[0, :, g * LANES:(g + 1) * LANES], lambda g: vtm_ref[0, g * C_VR:(g + 1) * C_VR, :],
             lambda h: bias_ref[kind_m, h, 0:N_META, :]),
            (lambda g: k_ref[pl.ds(poff, LANES), g * LANES:(g + 1) * LANES],
             lambda g: vt_ref[prev, g * C_VR:(g + 1) * C_VR, :], lambda h: bias_ref[kind_p, h]),
            (lambda g: k_ref[pl.ds(coff, LANES), g * LANES:(g + 1) * LANES],
             lambda g: vt_ref[i, g * C_VR:(g + 1) * C_VR, :], lambda h: bias_ref[KIND_DIAG, h]),
        ])

    @pl.when(s_id == nb_real)
    def _():
        attend([(lambda g: kmf_ref[:, g * LANES:(g + 1) * LANES], lambda g: vtmf_ref[0, g * C_VR:(g + 1) * C_VR, :],
                 lambda h: bias_ref[KIND_METAMETA, h])])


def _attn_c(qkv, vt, bias, sinks, *, bsz, nblk):
    n = qkv.shape[0]
    nb_real = bsz * nblk
    seq = nblk * LANES
    vr = C_KV_HEADS * C_VR
    kern = functools.partial(_attn_c_kernel, nblk=nblk)
    bclamp = lambda s: jnp.minimum(s // nblk, bsz - 1)
    kcol = 2 * C_QD // C_KD
    km, vtm = _meta_views(qkv[nb_real * LANES:, 2 * C_QD:], vt[nb_real], bsz)
    return pl.pallas_call(
        kern,
        grid=(nb_real + 1,),
        in_specs=[
            pl.BlockSpec(memory_space=pltpu.SMEM),
            pl.BlockSpec((LANES, C_QD), lambda s: (s, 0)),
            pl.BlockSpec((LANES, C_QD), lambda s: (s, 1)),
            pl.BlockSpec((seq, C_KD), lambda s: (bclamp(s), kcol)),
            pl.BlockSpec((nblk, vr, LANES), lambda s: (bclamp(s), 0, 0)),
            pl.BlockSpec((1, N_META, C_KD), lambda s: (bclamp(s), 0, 0)),
            pl.BlockSpec((1, vr, N_META), lambda s: (bclamp(s), 0, 0)),
            pl.BlockSpec((LANES, C_KD), lambda s: (nb_real, kcol)),
            pl.BlockSpec((1, vr, LANES), lambda s: (nb_real, 0, 0)),
            _bias_spec(),
        ],
        out_specs=pl.BlockSpec((LANES, C_QD), lambda s: (s, 0)),
        out_shape=jax.ShapeDtypeStruct((n, C_QD), bf16),
        scratch_shapes=[pltpu.VMEM((C_GROUP * LANES, LANES), bf16)],
        compiler_params=_cparams(("arbitrary",)),
        name="attn_c",
    )(sinks, qkv, qkv, qkv, vt, km, vtm, qkv, vt, bias)


B_QA = B_HEADS * B_KV_RANK
B_QI = IDX_HEADS * IDX_DIM
B_W1 = 2 * B_Q_RANK + 2 * LANES
B_TR = B_KV_RANK + ONES_ROWS


def _proj_b_kernel(h_ref, g_ref, w1_ref, ln_ref, wuq_ref, qn_ref,
                   qa_ref, qi_ref, ckv_ref, ckvt_ref, kk_ref, wit_ref):
    xn = (_rms_rows(h_ref[...]) * g_ref[...]).astype(bf16)
    y = _dot(xn, w1_ref[...])
    r = B_Q_RANK
    cq = (_rms_rows(y[:, :r]) * ln_ref[0:1, :]).astype(bf16)
    ckv = _rms_rows(y[:, r:2 * r]) * ln_ref[1:2, :]
    ckv_ref[...] = ckv.astype(bf16)
    kk_ref[...] = _rms_rows(y[:, 2 * r:2 * r + LANES]).astype(bf16)
    wi = y[:, 2 * r + LANES:] * (IDX_HEADS ** -0.5)
    ones = jnp.ones((ONES_ROWS, LANES), bf16)
    for t in range(ckvt_ref.shape[0]):
        ckvt_ref[t, 0:r, :] = ckv[t * LANES:(t + 1) * LANES, :].T.astype(bf16)
        ckvt_ref[t, r:, :] = ones
        wit_ref[t] = wi[t * LANES:(t + 1) * LANES, :].T[0:IDX_HEADS, :]
    z = _dot(cq, wuq_ref[...])
    qa_ref[...] = (_group_rms(z[:, :B_QA], B_KV_RANK) * qn_ref[...] * (B_KV_RANK ** -0.5 * LOG2E)).astype(bf16)
    qi = z[:, B_QA:] * (IDX_DIM ** -0.5)
    lo = _lo_half_mask(qi.shape, 2 * IDX_DIM, IDX_DIM)
    qi_ref[:, :B_QI] = jnp.where(lo, qi, 0.0).astype(bf16)
    qi_ref[:, B_QI:] = jnp.where(lo, 0.0, qi).astype(bf16)


def _proj_b(h, g, w1, latent_norm, wuq, q_norm, *, tm):
    n, d = h.shape
    qn = jnp.tile(q_norm, B_HEADS).reshape(1, B_QA)
    row = lambda i: (i, 0)
    row3 = lambda i: (i, 0, 0)
    const = lambda i: (0, 0)
    nt = tm // LANES
    return pl.pallas_call(
        _proj_b_kernel,
        grid=(n // tm,),
        in_specs=[
            pl.BlockSpec((tm, d), row),
            pl.BlockSpec((1, d), const),
            pl.BlockSpec(w1.shape, const),
            pl.BlockSpec(latent_norm.shape, const),
            pl.BlockSpec(wuq.shape, const),
            pl.BlockSpec((1, B_QA), const),
        ],
        out_specs=[
            pl.BlockSpec((tm, B_QA), row),
            pl.BlockSpec((tm, 2 * B_QI), row),
            pl.BlockSpec((tm, B_KV_RANK), row),
            pl.BlockSpec((nt, B_TR, LANES), row3),
            pl.BlockSpec((tm, LANES), row),
            pl.BlockSpec((nt, IDX_HEADS, LANES), row3),
        ],
        out_shape=[
            jax.ShapeDtypeStruct((n, B_QA), bf16),
            jax.ShapeDtypeStruct((n, 2 * B_QI), bf16),
            jax.ShapeDtypeStruct((n, B_KV_RANK), bf16),
            jax.ShapeDtypeStruct((n // LANES, B_TR, LANES), bf16),
            jax.ShapeDtypeStruct((n, LANES), bf16),
            jax.ShapeDtypeStruct((n // LANES, IDX_HEADS, LANES), f32),
        ],
        compiler_params=_cparams(("arbitrary",)),
        name="proj_b",
    )(h, g.reshape(1, d), w1, latent_norm, wuq, qn)


def _attn_b_kernel(qa_ref, qi_ref, wit_ref, ckv_ref, ckvt_ref, kk_ref, ckvm_ref, ckvtm_ref, kkm_ref,
                   ckvmf_ref, ckvtmf_ref, bias_ref, wuvt_ref,
                   o_ref, qs_ref, is_ref, key_ref, pen_ref, m_ref, acc_ref, gap_ref, *, nblk, k_sel):
    s_id = pl.program_id(0)
    nb_real = pl.num_programs(0) - 1
    r_io, c_io = _tile_iotas()
    rk = B_KV_RANK

    def init():
        for h in range(B_HEADS):
            qs_ref[h * LANES:(h + 1) * LANES, :] = qa_ref[:, h * rk:(h + 1) * rk]
        m_ref[...] = jnp.full(m_ref.shape, NEG_INF, f32)
        acc_ref[...] = jnp.zeros(acc_ref.shape, f32)

    def step_fns(ckv_fn, ckvt_fn, bias_fn, pen_fn):
        def logits():
            st = _dot_nt(ckv_fn(), qs_ref[...])
            pen = None if pen_fn is None else pen_fn()
            cols = []
            for h in range(B_HEADS):
                add = pen if bias_fn is None else (bias_fn(h) if pen is None else bias_fn(h) + pen)
                cols.append(st[:, h * LANES:(h + 1) * LANES] + add)
            return jnp.concatenate(cols, axis=1)

        return logits, lambda pb: _dot(ckvt_fn(), pb)

    def finalize():
        olat = (acc_ref[0:rk, :] / acc_ref[rk:rk + 1, :]).astype(bf16)
        ot = jnp.concatenate([_dot(wuvt_ref[h], olat[:, h * LANES:(h + 1) * LANES]) for h in range(B_HEADS)], axis=0)
        o_ref[...] = ot.T.astype(bf16)

    def index_scores(kk):
        s = jnp.maximum(_dot_nt(kk, is_ref[...]), 0.0)
        wt = wit_ref[0]
        sc = jnp.zeros((kk.shape[0], LANES), f32)
        for hh in range(IDX_HEADS):
            sc = sc + wt[hh:hh + 1, :] * s[:, hh * LANES:(hh + 1) * LANES]
        return sc

    def sort_key(sc):
        bits = lax.bitcast_convert_type(sc + 0.0, jnp.int32)
        return jnp.where(bits < 0, bits ^ jnp.int32(0x7FFFFFFF), bits)

    @pl.when(s_id < nb_real)
    def _():
        i = s_id % nblk
        ntile = i + 2
        init()
        for hh in range(IDX_HEADS):
            base = (hh % 2) * B_QI + (hh // 2) * LANES
            is_ref[hh * LANES:(hh + 1) * LANES, :] = qi_ref[:, base:base + LANES]

        int_min_tile = jnp.full((LANES, LANES), INT_MIN, jnp.int32)
        key_ref[0] = int_min_tile
        key_ref[0, 0:N_META, :] = sort_key(index_scores(kkm_ref[0]))
        key_ref[i + 2] = int_min_tile

        def score_body(jp, carry):
            off = pl.multiple_of(jp * (2 * LANES), 2 * LANES)
            keys = sort_key(index_scores(kk_ref[pl.ds(off, 2 * LANES), :]))
            for t in range(2):
                j = _vec(2 * jp + t)
                vis = (j < i) | ((j == i) & ((r_io >> 6) <= (c_io >> 6)))
                key_ref[2 * jp + t + 1] = jnp.where(vis, keys[t * LANES:(t + 1) * LANES], jnp.int32(INT_MIN))
            return carry

        lax.fori_loop(0, i // 2 + 1, score_body, 0)

        def count(pred):
            def cbody(tp, accv):
                for t in (2 * tp, 2 * tp + 1):
                    accv = accv + jnp.where(pred(key_ref[t], t), 1.0, 0.0)
                return accv
            accv = lax.fori_loop(0, (ntile + 1) // 2, cbody, jnp.zeros((LANES, LANES), f32))
            return jnp.sum(accv, axis=0, keepdims=True)

        kf = float(k_sel)
        zero = jnp.zeros((1, LANES), jnp.int32)
        t0 = jnp.where(count(lambda k, t: k >= zero) >= kf, zero, jnp.int32(INT_MIN))

        def bit_body(it, tcur):
            cand = tcur | jnp.left_shift(jnp.int32(1), 30 - it)
            return jnp.where(count(lambda k, t: k >= cand) >= kf, cand, tcur)

        thr = lax.fori_loop(0, 31, bit_body, t0)

        need = kf - count(lambda k, t: k > thr)
        n_eq = count(lambda k, t: k == thr)
        has_thr = thr > jnp.int32(INT_MIN)
        tied = jnp.max(jnp.where(has_thr & (n_eq > need), 1.0, 0.0)) > 0.0

        def tie_search(_):
            def jbody(it, jcur):
                cand = jcur | jnp.left_shift(jnp.int32(1), 11 - it)
                cnt = count(lambda k, t: (k == thr) & ((t * LANES + r_io) < cand))
                return jnp.where(cnt < need, cand, jcur)
            return lax.fori_loop(0, 12, jbody, jnp.zeros((1, LANES), jnp.int32))

        j_last = lax.cond(tied, tie_search, lambda _: jnp.full((1, LANES), 4095, jnp.int32), 0)
        j_last = jnp.where(has_thr, j_last, -1)

        def pen_body(t, carry):
            k = key_ref[t]
            sel = (k > thr) | ((k == thr) & ((t * LANES + r_io) <= j_last))
            pen_ref[t] = jnp.where(sel, 0.0, NEG_INF)
            return carry

        lax.fori_loop(0, ntile, pen_body, 0)
        for t in range(1, NSUB):
            pen_ref[i + 1 + t] = jnp.full((LANES, LANES), NEG_INF, f32)

        kind_m = jnp.where(i == 0, KIND_META0, KIND_FAR)
        nfar, nnear = _sweep_steps(i)

        def kv_fns(b0, nsub):
            off = pl.multiple_of(b0 * LANES, NSUB * LANES)
            return (lambda: ckv_ref[pl.ds(off, nsub * LANES), :],
                    lambda: jnp.concatenate([ckvt_ref[b0 + t] for t in range(nsub)], axis=1),
                    lambda: jnp.concatenate([pen_ref[b0 + t + 1] for t in range(nsub)], axis=0))

        def far_step(w):
            ckv_fn, ckvt_fn, pen_fn = kv_fns(w * NSUB_FAR, NSUB_FAR)
            far_bias = jnp.concatenate([bias_ref[KIND_FAR, h, 0:1, :] for h in range(B_HEADS)], axis=1)
            return step_fns(ckv_fn, ckvt_fn, None, pen_fn) + (far_bias,)

        def near_step(u):
            b0 = nfar * NSUB_FAR + u * NSUB
            ckv_fn, ckvt_fn, pen_fn = kv_fns(b0, NSUB)
            kinds = [_block_kind(b0 + t - i) for t in range(NSUB)]
            return step_fns(ckv_fn, ckvt_fn,
                            lambda h: jnp.concatenate([bias_ref[kinds[t], h] for t in range(NSUB)], axis=0),
                            pen_fn) + (None,)

        def meta_fns(with_pen):
            return step_fns(lambda: ckvm_ref[0], lambda: ckvtm_ref[0], lambda h: bias_ref[kind_m, h, 0:N_META, :],
                            (lambda: pen_ref[0, 0:N_META, :]) if with_pen else None)

        def sweep(gap_ref):
            acc_ref[...] = jnp.zeros(acc_ref.shape, f32)
            if gap_ref is None:
                m_ref[...] = jnp.full(m_ref.shape, NEG_INF, f32)
            else:
                m_ref[...] = jnp.max(meta_fns(False)[0](), axis=0, keepdims=True)
            _softmax_step(*meta_fns(True), m_ref, acc_ref)
            _softmax_loop(0, nfar, far_step, m_ref, acc_ref, gap_ref)
            _softmax_loop(0, nnear, near_step, m_ref, acc_ref, gap_ref)

        gap_ref[...] = jnp.full(gap_ref.shape, NEG_INF, f32)
        sweep(gap_ref)
        faded = jnp.logical_not(jnp.min(acc_ref[rk:rk + 1, :]) >= LAZY_FLOOR)
        pl.when(_overshot(gap_ref) | faded)(lambda: sweep(None))
        finalize()

    @pl.when(s_id == nb_real)
    def _():
        init()
        _softmax_step(*step_fns(lambda: ckvmf_ref[...], lambda: ckvtmf_ref[0],
                                lambda h: bias_ref[KIND_METAMETA, h], None), m_ref, acc_ref)
        finalize()


def _attn_b(qa, qi, wit, ckv, ckvt, kk, bias, wuvt, *, bsz, nblk, k_sel):
    n = qa.shape[0]
    nb_real = bsz * nblk
    seq = nblk * LANES
    assert k_sel >= N_META and (nblk + 1) * LANES <= 4096 and nblk % NSUB == 0
    ckvm, ckvtm = _meta_views(ckv[nb_real * LANES:], ckvt[nb_real], bsz)
    kkm = kk[nb_real * LANES:].reshape(bsz, N_META, LANES)
    kern = functools.partial(_attn_b_kernel, nblk=nblk, k_sel=k_sel)
    bidx = lambda s: jnp.minimum(s // nblk, bsz - 1)
    blk = lambda s: (s, 0)
    return pl.pallas_call(
        kern,
        grid=(nb_real + 1,),
        in_specs=[
            pl.BlockSpec((LANES, B_QA), blk),
            pl.BlockSpec((LANES, 2 * B_QI), blk),
            pl.BlockSpec((1, IDX_HEADS, LANES), lambda s: (s, 0, 0)),
            pl.BlockSpec((seq, B_KV_RANK), lambda s: (bidx(s), 0)),
            pl.BlockSpec((nblk, B_TR, LANES), lambda s: (bidx(s), 0, 0)),
            pl.BlockSpec((seq, LANES), lambda s: (bidx(s), 0)),
            pl.BlockSpec((1, N_META, B_KV_RANK), lambda s: (bidx(s), 0, 0)),
            pl.BlockSpec((1, B_TR, N_META), lambda s: (bidx(s), 0, 0)),
            pl.BlockSpec((1, N_META, LANES), lambda s: (bidx(s), 0, 0)),
            pl.BlockSpec((LANES, B_KV_RANK), lambda s: (nb_real, 0)),
            pl.BlockSpec((1, B_TR, LANES), lambda s: (nb_real, 0, 0)),
            _bias_spec(),
            pl.BlockSpec(wuvt.shape, lambda s: (0, 0, 0)),
        ],
        out_specs=pl.BlockSpec((LANES, B_HEADS * B_VD), blk),
        out_shape=jax.ShapeDtypeStruct((n, B_HEADS * B_VD), bf16),
        scratch_shapes=[
            pltpu.VMEM((B_HEADS * LANES, B_KV_RANK), bf16),
            pltpu.VMEM((IDX_HEADS * LANES, LANES), bf16),
            pltpu.VMEM((nblk + 2, LANES, LANES), jnp.int32),
            pltpu.VMEM((nblk + NSUB, LANES, LANES), f32),
            pltpu.VMEM((1, B_HEADS * LANES), f32),
            pltpu.VMEM((B_TR, B_HEADS * LANES), f32),
            pltpu.VMEM((1, B_HEADS * LANES), f32),
        ],
        compiler_params=_cparams(("arbitrary",)),
        name="attn_b",
    )(qa, qi, wit, ckv, ckvt, kk, ckvm, ckvtm, kkm, ckv, ckvt, bias, wuvt)


def kernel(x, meta_tokens, rel_bias, ln_ffn1, ffn1_wi, ffn1_wo, ln_mix, w_out, ln_ffn2, ffn2_wi, ffn2_wo, a_w_in, a_qk_norm, a_lambda, a_subln, b_w_in, b_latent_norm, b_w_uq, b_q_norm, b_w_uv, c_w_in, c_qk_norm, c_sinks):
    bsz, seq, d = x.shape
    assert d == D_MODEL and seq % LANES == 0 and bsz * N_META == LANES
    nblk = seq // LANES
    n = bsz * seq + LANES
    k_sel = min(TOPK_MAX, seq // 4)
    tm_ffn = _row_tile(n, 1408)
    tm_proj = _row_tile(n, 384, LANES)
    fc = 256

    h = jnp.concatenate([x.reshape(bsz * seq, d),
                         jnp.broadcast_to(meta_tokens.astype(x.dtype), (bsz, N_META, d)).reshape(LANES, d)], axis=0)
    bias = _bias_tiles(rel_bias)

    for layer in range(DEPTH):
        h = _ffn(h, ln_ffn1[layer], ffn1_wi, ffn1_wo, layer, tm=tm_ffn, fc=fc)
        kind, j = layer % N_MIXERS, layer // N_MIXERS
        g = ln_mix[layer]
        if kind == 0:
            lambda_init = 0.8 - 0.6 * math.exp(-0.3 * layer)
            qkv, vt = _proj_a(h, g, a_w_in[j].astype(bf16), a_qk_norm[j], tm=tm_proj)
            mix = _attn_a(qkv, vt, bias, a_lambda[j], a_subln[j], bsz=bsz, nblk=nblk, lambda_init=lambda_init)
        elif kind == 1:
            w = b_w_in[j]
            r2 = B_Q_RANK + B_KV_RANK
            kcol = w[:, r2:r2 + IDX_DIM]
            w1 = jnp.concatenate([w[:, :r2], kcol, kcol, w[:, r2 + IDX_DIM:],
                                  jnp.zeros((d, LANES - IDX_HEADS), w.dtype)], axis=1).astype(bf16)
            assert w1.shape[1] == B_W1
            qa, qi, ckv, ckvt, kk, wit = _proj_b(h, g, w1, b_latent_norm[j], b_w_uq[j].astype(bf16), b_q_norm[j],
                                                 tm=tm_proj)
            wuvt = jnp.swapaxes(b_w_uv[j], 1, 2).astype(bf16)
            mix = _attn_b(qa, qi, wit, ckv, ckvt, kk, bias, wuvt, bsz=bsz, nblk=nblk, k_sel=k_sel)
        else:
            w = c_w_in[j]
            kcols = [w[:, C_QD + gi * C_HD:C_QD + (gi + 1) * C_HD] for gi in range(C_KV_HEADS)]
            voff = C_QD + C_KV_HEADS * C_HD
            vcols = [w[:, voff + gi * C_HD:voff + (gi + 1) * C_HD] for gi in range(C_KV_HEADS)]
            wc = jnp.concatenate([w[:, :C_QD]] + [kc for kc in kcols for _ in range(2)]
                                 + [vc for vc in vcols for _ in range(2)], axis=1).astype(bf16)
            qkv, vt = _proj_c(h, g, wc, c_qk_norm[j], tm=tm_proj)
            mix = _attn_c(qkv, vt, bias, c_sinks[j], bsz=bsz, nblk=nblk)
        h = _ffn(h, ln_ffn2[layer], ffn2_wi, ffn2_wo, layer, tm=tm_ffn, fc=fc,
                 mix=mix, wout=w_out[layer].astype(bf16))
    return h[:bsz * seq].reshape(bsz, seq, d)
```

```python
import functools
import math

import numpy as np
import jax
import jax.numpy as jnp
from jax import lax
from jax.experimental import pallas as pl
from jax.experimental.pallas import tpu as pltpu

D_MODEL = 1024
DEPTH = 4
CHUNK = 64
N_META = 16
N_MIXERS = 3
NEG_INF = -1e30
REL_BUCKETS = 32
REL_MAX_DIST = 128
REL_HEADS = 16
D_FF = 2816
A_HEADS = 8
A_HD = 64
A_VD = 2 * A_HD
B_HEADS = 16
B_Q_RANK = 256
B_KV_RANK = 256
B_VD = 64
IDX_HEADS = 8
IDX_DIM = 64
TOPK_MAX = 256
C_Q_HEADS = 16
C_KV_HEADS = 2
C_GROUP = C_Q_HEADS // C_KV_HEADS
C_HD = 64
EPS = 1e-6

LANES = 128
BF16_ROWS = 16
VMEM_LIMIT = 56 * 1024 * 1024
INT_MIN = -(2 ** 31)
NSUB = 2
NSUB_FAR = 4
LOG2E = math.log2(math.e)
LAZY_GAP = 57.0
LAZY_FLOOR = 2.0 ** -100
ONES_ROWS = BF16_ROWS

KIND_DIAG, KIND_PREV, KIND_FAR, KIND_META0, KIND_METAMETA, KIND_MASKED, KIND_PREVWIN = 0, 1, 2, 3, 4, 5, 6
N_KINDS = 7

f32 = jnp.float32
bf16 = jnp.bfloat16


def _cparams(sem):
    return pltpu.CompilerParams(dimension_semantics=sem, vmem_limit_bytes=VMEM_LIMIT)


def _row_tile(n, cap, mult=BF16_ROWS):
    best = None
    for t in range(mult, cap + 1, mult):
        if n % t == 0:
            best = t
    assert best is not None
    return best


def _dot(a, b):
    return jnp.dot(a, b, preferred_element_type=f32)


def _dot_nt(a, b):
    return lax.dot_general(a, b, (((1,), (1,)), ((), ())), preferred_element_type=f32)


def _rms_rows(x):
    return x * lax.rsqrt(jnp.mean(x * x, axis=-1, keepdims=True) + EPS)


def _lo_half_mask(shape, period, half):
    return (lax.broadcasted_iota(jnp.int32, shape, 1) & (period - 1)) < half


def _group_rms(x, group):
    r, c = x.shape
    outs = []
    if group == 64:
        lo = _lo_half_mask((r, LANES), LANES, 64)
        for ci in range(c // LANES):
            xc = x[:, ci * LANES:(ci + 1) * LANES]
            x2 = xc * xc
            s_lo = jnp.sum(jnp.where(lo, x2, 0.0), axis=-1, keepdims=True)
            s_hi = jnp.sum(jnp.where(lo, 0.0, x2), axis=-1, keepdims=True)
            inv = jnp.where(lo, lax.rsqrt(s_lo * (1.0 / 64) + EPS), lax.rsqrt(s_hi * (1.0 / 64) + EPS))
            outs.append(xc * inv)
    else:
        for gi in range(c // group):
            outs.append(_rms_rows(x[:, gi * group:(gi + 1) * group]))
    return outs[0] if len(outs) == 1 else jnp.concatenate(outs, axis=-1)


def _tile_iotas():
    r = lax.broadcasted_iota(jnp.int32, (LANES, LANES), 0)
    c = lax.broadcasted_iota(jnp.int32, (LANES, LANES), 1)
    return r, c


def _vec(s):
    return jnp.full((LANES, LANES), s, jnp.int32)


def _softmax_step(logits_fn, pv_fn, m_ref, acc_ref, gap_ref=None, offset=None, post_fn=None):
    st = logits_fn()
    if gap_ref is None:
        if offset is not None:
            st = st + offset
        m_old = m_ref[...]
        m_new = jnp.maximum(m_old, jnp.max(st, axis=0, keepdims=True))
        if post_fn is not None:
            st = post_fn(st)
        acc_ref[...] = jnp.exp2(m_old - m_new) * acc_ref[...] + pv_fn(jnp.exp2(st - m_new).astype(bf16))
        m_ref[...] = m_new
    else:
        shift = m_ref[...] if offset is None else m_ref[...] - offset
        gap_ref[...] = jnp.maximum(gap_ref[...], jnp.max(st, axis=0, keepdims=True) - shift)
        acc_ref[...] += pv_fn(jnp.exp2(st - shift).astype(bf16))


def _softmax_loop(lo, hi, step_fn, m_ref, acc_ref, gap_ref):
    def body(w, carry):
        logits_fn, pv_fn, offset = step_fn(w)
        _softmax_step(logits_fn, pv_fn, m_ref, acc_ref, gap_ref, offset)
        return carry

    lax.fori_loop(lo, hi, body, 0)


def _overshot(gap_ref):
    return jnp.logical_not(jnp.max(gap_ref[...]) <= LAZY_GAP)


def _block_kind(rel):
    return jnp.where(rel < -1, KIND_FAR,
                     jnp.where(rel == -1, KIND_PREV, jnp.where(rel == 0, KIND_DIAG, KIND_MASKED)))


def _sweep_steps(i):
    nfar = jnp.maximum(i - 1 - NSUB, 0) // NSUB_FAR
    near0 = NSUB + nfar * NSUB_FAR
    return nfar, near0, (i - near0 + NSUB) // NSUB


def _ffn_kernel(*refs, fuse_out):
    if fuse_out:
        h_ref, mix_ref, wout_ref, g_ref, wa_ref, wb_ref, wo_ref, o_ref, xn_ref = refs
    else:
        h_ref, g_ref, wa_ref, wb_ref, wo_ref, o_ref, xn_ref = refs
    j = pl.program_id(1)

    @pl.when(j == 0)
    def _():
        r = h_ref[...]
        if fuse_out:
            r = r + _dot(mix_ref[...], wout_ref[...])
        o_ref[...] = r
        xn_ref[...] = (_rms_rows(r) * g_ref[...]).astype(bf16)

    xn = xn_ref[...]
    a = _dot(xn, wa_ref[...].astype(bf16))
    b = _dot(xn, wb_ref[...].astype(bf16))
    act = (a / (1.0 + jnp.exp(-a)) * b).astype(bf16)
    o_ref[...] += 0.5 * _dot(act, wo_ref[...].astype(bf16))


def _ffn(h, g, wi, wo, layer, *, tm, fc, mix=None, wout=None):
    n, d = h.shape
    dff = wo.shape[1]
    nj = dff // fc
    fuse = mix is not None
    row = lambda i, j: (i, 0)
    in_specs = [pl.BlockSpec((tm, d), row)]
    args = [h]
    if fuse:
        in_specs += [pl.BlockSpec((tm, mix.shape[1]), row), pl.BlockSpec(wout.shape, lambda i, j: (0, 0))]
        args += [mix, wout]
    in_specs += [
        pl.BlockSpec((1, d), lambda i, j: (0, 0)),
        pl.BlockSpec((None, d, fc), lambda i, j: (layer, 0, j)),
        pl.BlockSpec((None, d, fc), lambda i, j: (layer, 0, j + nj)),
        pl.BlockSpec((None, fc, d), lambda i, j: (layer, j, 0)),
    ]
    args += [g.reshape(1, d), wi, wi, wo]
    return pl.pallas_call(
        functools.partial(_ffn_kernel, fuse_out=fuse),
        grid=(n // tm, nj),
        in_specs=in_specs,
        out_specs=pl.BlockSpec((tm, d), row),
        out_shape=jax.ShapeDtypeStruct((n, d), f32),
        scratch_shapes=[pltpu.VMEM((tm, d), bf16)],
        compiler_params=_cparams(("arbitrary", "arbitrary")),
        name="ffn_out" if fuse else "ffn",
    )(*args)


def _rel_bucket(rel):
    half = REL_BUCKETS // 2
    max_exact = half // 2
    n = jnp.abs(rel)
    large = max_exact + (jnp.log(jnp.maximum(n, 1).astype(jnp.float32) / max_exact)
                         / math.log(REL_MAX_DIST / max_exact) * (half - max_exact)).astype(jnp.int32)
    large = jnp.minimum(large, half - 1)
    return jnp.where(rel > 0, half, 0) + jnp.where(n < max_exact, n, large)


def _rel_tiles():
    k = np.arange(LANES)[:, None]
    q = np.arange(LANES)[None, :]
    far = np.full((LANES, LANES), -4 * LANES)
    ones = np.ones((LANES, LANES), bool)
    rels = [k - q, k - q - LANES, far, (k % N_META) - N_META - q, (k % N_META) - (q % N_META), far, k - q - LANES]
    vis = [(k // CHUNK) <= (q // CHUNK), ones, ones, ones, (k // N_META) == (q // N_META), ~ones,
           (q < CHUNK) | (k >= CHUNK)]
    return (np.stack([np.broadcast_to(a, (LANES, LANES)) for a in rels]).astype(np.int32),
            np.stack([np.broadcast_to(a, (LANES, LANES)) for a in vis]).astype(np.int32))


def _bias_kernel(rb_ref, bucket_ref, vis_ref, o_ref):
    h = pl.program_id(0)
    for kind in range(N_KINDS):
        bk = bucket_ref[kind]
        acc = jnp.zeros((LANES, LANES), f32)
        for b in range(REL_BUCKETS):
            acc = jnp.where(bk == b, rb_ref[b, h], acc)
        o_ref[kind, 0] = jnp.where(vis_ref[kind] != 0, acc * LOG2E, NEG_INF)


def _bias_tiles(rel_bias):
    rel, vis = _rel_tiles()
    bucket = _rel_bucket(jnp.asarray(rel))
    nk = N_KINDS
    return pl.pallas_call(
        _bias_kernel,
        grid=(REL_HEADS,),
        in_specs=[
            pl.BlockSpec(memory_space=pltpu.SMEM),
            pl.BlockSpec((nk, LANES, LANES), lambda h: (0, 0, 0)),
            pl.BlockSpec((nk, LANES, LANES), lambda h: (0, 0, 0)),
        ],
        out_specs=pl.BlockSpec((nk, 1, LANES, LANES), lambda h: (0, h, 0, 0)),
        out_shape=jax.ShapeDtypeStruct((nk, REL_HEADS, LANES, LANES), f32),
        compiler_params=_cparams(("arbitrary",)),
        name="bias_tiles",
    )(rel_bias, bucket, jnp.asarray(vis))


def _bias_spec():
    return pl.BlockSpec((N_KINDS, REL_HEADS, LANES, LANES), lambda s: (0, 0, 0, 0))


def _meta_views(rows, cols_t, bsz):
    f = rows.shape[1]
    return (rows.reshape(bsz, N_META, f),
            cols_t.reshape(cols_t.shape[0], bsz, N_META).transpose(1, 0, 2))


A_QD = A_HEADS * 2 * A_HD
A_VR = A_VD + ONES_ROWS


def _proj_a_kernel(h_ref, g_ref, w_ref, gq_ref, gk_ref, qs_ref, k_ref, vt_ref):
    xn = (_rms_rows(h_ref[...]) * g_ref[...]).astype(bf16)
    y = _dot(xn, w_ref[...])
    q = _group_rms(y[:, :A_QD], A_HD) * gq_ref[...] * (A_HD ** -0.5 * LOG2E)
    lo = _lo_half_mask(q.shape, 2 * A_HD, A_HD)
    q_lo = jnp.where(lo, q, 0.0).astype(bf16)
    q_hi = jnp.where(lo, 0.0, q).astype(bf16)
    k_ref[...] = (_group_rms(y[:, A_QD:2 * A_QD], A_HD) * gk_ref[...]).astype(bf16)
    ones = jnp.ones((ONES_ROWS, LANES), bf16)
    for t in range(vt_ref.shape[0]):
        rows = slice(t * LANES, (t + 1) * LANES)
        vt = y[rows, 2 * A_QD:].T.astype(bf16)
        for h in range(A_HEADS):
            qs_ref[t, h, :LANES, :] = q_lo[rows, h * A_VD:(h + 1) * A_VD]
            qs_ref[t, h, LANES:, :] = q_hi[rows, h * A_VD:(h + 1) * A_VD]
            vt_ref[t, h * A_VR:h * A_VR + A_VD, :] = vt[h * A_VD:(h + 1) * A_VD]
            vt_ref[t, h * A_VR + A_VD:(h + 1) * A_VR, :] = ones


def _proj_a(h, g, w, qk_norm, *, tm):
    n, d = h.shape
    nw = w.shape[1]
    gq = jnp.tile(qk_norm[0], A_QD // A_HD).reshape(1, A_QD)
    gk = jnp.tile(qk_norm[1], A_QD // A_HD).reshape(1, A_QD)
    return pl.pallas_call(
        _proj_a_kernel,
        grid=(n // tm,),
        in_specs=[
            pl.BlockSpec((tm, d), lambda i: (i, 0)),
            pl.BlockSpec((1, d), lambda i: (0, 0)),
            pl.BlockSpec((d, nw), lambda i: (0, 0)),
            pl.BlockSpec((1, A_QD), lambda i: (0, 0)),
            pl.BlockSpec((1, A_QD), lambda i: (0, 0)),
        ],
        out_specs=[
            pl.BlockSpec((tm // LANES, A_HEADS, 2 * LANES, A_VD), lambda i: (i, 0, 0, 0)),
            pl.BlockSpec((tm, A_QD), lambda i: (i, 0)),
            pl.BlockSpec((tm // LANES, A_HEADS * A_VR, LANES), lambda i: (i, 0, 0)),
        ],
        out_shape=[
            jax.ShapeDtypeStruct((n // LANES, A_HEADS, 2 * LANES, A_VD), bf16),
            jax.ShapeDtypeStruct((n, A_QD), bf16),
            jax.ShapeDtypeStruct((n // LANES, A_HEADS * A_VR, LANES), bf16),
        ],
        compiler_params=_cparams(("arbitrary",)),
        name="proj_a",
    )(h, g.reshape(1, d), w, gq, gk)


def _attn_a_kernel(qs_ref, k_ref, vt_ref, km_ref, vtm_ref, kmf_ref, vtmf_ref, bias_ref, lam_ref, sub_ref,
                   o_ref, m_ref, acc_ref, gap_ref, *, nblk, lambda_init):
    s_id = pl.program_id(0)
    nb_real = pl.num_programs(0) - 1
    hw = 2 * A_HD

    def init():
        m_ref[...] = jnp.full(m_ref.shape, NEG_INF, f32)
        acc_ref[...] = jnp.zeros(acc_ref.shape, f32)

    def step_fns(kt_fn, vtt_fns, bias_fn):
        def logits():
            sts = [_dot_nt(kt_fn(h), qs_ref[0, h]) for h in range(A_HEADS)]
            if bias_fn is not None:
                sts = [st + bias_fn(h) for h, st in enumerate(sts)]
            return jnp.concatenate(sts, axis=1)

        def pv(pb):
            outs = []
            for h in range(A_HEADS):
                acc = None
                for vtt_fn, r0 in vtt_fns:
                    vtt = vtt_fn(h)
                    part = _dot(vtt, pb[r0:r0 + vtt.shape[1], h * 2 * LANES:(h + 1) * 2 * LANES])
                    acc = part if acc is None else acc + part
                outs.append(acc)
            return jnp.concatenate(outs, axis=1)

        return logits, pv

    def bias_rows(kind, h, rows):
        return jnp.concatenate([bias_ref[kind, h, 0:rows, :], bias_ref[kind, A_HEADS + h, 0:rows, :]], axis=1)

    def finalize():
        lam = lam_ref[...]
        lam_full = (jnp.exp(jnp.sum(lam[0:1] * lam[1:2], axis=-1, keepdims=True))
                    - jnp.exp(jnp.sum(lam[2:3] * lam[3:4], axis=-1, keepdims=True)) + lambda_init)
        o = acc_ref[0:A_VD, :] / acc_ref[A_VD:A_VD + 1, :]
        for h in range(A_HEADS):
            d = o[:, 2 * h * LANES:(2 * h + 1) * LANES] - lam_full * o[:, (2 * h + 1) * LANES:(2 * h + 2) * LANES]
            d = d * lax.rsqrt(jnp.mean(d * d, axis=0, keepdims=True) + EPS) * sub_ref[...] * (1.0 - lambda_init)
            o_ref[:, h * hw:(h + 1) * hw] = d.T.astype(bf16)

    @pl.when(s_id < nb_real)
    def _():
        i = s_id % nblk
        kind_m = jnp.where(i == 0, KIND_META0, KIND_FAR)
        nfar, near0, nnear = _sweep_steps(i)

        def k_fn(b0, nsub):
            off = pl.multiple_of(b0 * LANES, NSUB * LANES)
            return lambda h: k_ref[pl.ds(off, nsub * LANES), h * hw:(h + 1) * hw]

        def vt_fn(b0, nsub):
            return lambda h: jnp.concatenate([vt_ref[b0 + t, h * A_VR:(h + 1) * A_VR, :] for t in range(nsub)], axis=1)

        def near_bias(b0):
            kinds = [_block_kind(b0 + t - i) for t in range(NSUB)]
            return [lambda h, kind=kind: bias_rows(kind, h, LANES) for kind in kinds]

        def first_fns():
            biases = [lambda h: bias_rows(kind_m, h, N_META)] + near_bias(0)
            return step_fns(
                lambda h: jnp.concatenate([km_ref[0, :, h * hw:(h + 1) * hw], k_fn(0, NSUB)(h)], axis=0),
                [(lambda h: vtm_ref[0, h * A_VR:(h + 1) * A_VR, :], 0), (vt_fn(0, NSUB), N_META)],
                lambda h: jnp.concatenate([b(h) for b in biases], axis=0))

        def far_step(w):
            b0 = NSUB + w * NSUB_FAR
            far_bias = jnp.concatenate([bias_rows(KIND_FAR, h, 1) for h in range(A_HEADS)], axis=1)
            return step_fns(k_fn(b0, NSUB_FAR), [(vt_fn(b0, NSUB_FAR), 0)], None) + (far_bias,)

        def near_step(u):
            b0 = near0 + u * NSUB
            biases = near_bias(b0)
            return step_fns(k_fn(b0, NSUB), [(vt_fn(b0, NSUB), 0)],
                            lambda h: jnp.concatenate([b(h) for b in biases], axis=0)) + (None,)

        def sweep(gap_ref):
            init()
            _softmax_step(*first_fns(), m_ref, acc_ref)
            _softmax_loop(0, nfar, far_step, m_ref, acc_ref, gap_ref)
            _softmax_loop(0, nnear, near_step, m_ref, acc_ref, gap_ref)

        gap_ref[...] = jnp.full(gap_ref.shape, NEG_INF, f32)
        sweep(gap_ref)
        pl.when(_overshot(gap_ref))(lambda: sweep(None))
        finalize()

    @pl.when(s_id == nb_real)
    def _():
        init()
        _softmax_step(*step_fns(lambda h: kmf_ref[:, h * hw:(h + 1) * hw],
                                [(lambda h: vtmf_ref[0, h * A_VR:(h + 1) * A_VR, :], 0)],
                                lambda h: bias_rows(KIND_METAMETA, h, LANES)), m_ref, acc_ref)
        finalize()


def _attn_a(qs, k, vt, bias, lam, subln, *, bsz, nblk, lambda_init):
    n = k.shape[0]
    nb_real = bsz * nblk
    seq = nblk * LANES
    d = A_QD
    vr = A_HEADS * A_VR
    assert nblk % NSUB == 0
    km, vtm = _meta_views(k[nb_real * LANES:], vt[nb_real], bsz)
    kern = functools.partial(_attn_a_kernel, nblk=nblk, lambda_init=lambda_init)
    bclamp = lambda s: jnp.minimum(s // nblk, bsz - 1)
    return pl.pallas_call(
        kern,
        grid=(nb_real + 1,),
        in_specs=[
            pl.BlockSpec((1,) + qs.shape[1:], lambda s: (s, 0, 0, 0)),
            pl.BlockSpec((seq, d), lambda s: (bclamp(s), 0)),
            pl.BlockSpec((nblk, vr, LANES), lambda s: (bclamp(s), 0, 0)),
            pl.BlockSpec((1, N_META, d), lambda s: (bclamp(s), 0, 0)),
            pl.BlockSpec((1, vr, N_META), lambda s: (bclamp(s), 0, 0)),
            pl.BlockSpec((LANES, d), lambda s: (nb_real, 0)),
            pl.BlockSpec((1, vr, LANES), lambda s: (nb_real, 0, 0)),
            _bias_spec(),
            pl.BlockSpec((4, A_HD), lambda s: (0, 0)),
            pl.BlockSpec((A_VD, LANES), lambda s: (0, 0)),
        ],
        out_specs=pl.BlockSpec((LANES, d), lambda s: (s, 0)),
        out_shape=jax.ShapeDtypeStruct((n, d), bf16),
        scratch_shapes=[
            pltpu.VMEM((1, A_HEADS * 2 * LANES), f32),
            pltpu.VMEM((A_VR, A_HEADS * 2 * LANES), f32),
            pltpu.VMEM((1, A_HEADS * 2 * LANES), f32),
        ],
        compiler_params=_cparams(("arbitrary",)),
        name="attn_a",
    )(qs, k, vt, km, vtm, k, vt, bias, lam, jnp.broadcast_to(subln[:, None], (A_VD, LANES)))


C_QD = C_Q_HEADS * C_HD
C_KD = 2 * C_KV_HEADS * C_HD
C_VR = 2 * C_HD + ONES_ROWS


def _proj_c_kernel(h_ref, g_ref, w_ref, gq_ref, gk_ref, o_ref, vt_ref):
    xn = (_rms_rows(h_ref[...]) * g_ref[...]).astype(bf16)
    y = _dot(xn, w_ref[...])
    q = _group_rms(y[:, :C_QD], C_HD) * gq_ref[...] * (C_HD ** -0.5 * LOG2E)
    lo = _lo_half_mask(q.shape, 2 * C_HD, C_HD)
    o_ref[:, :C_QD] = jnp.where(lo, q, 0.0).astype(bf16)
    o_ref[:, C_QD:2 * C_QD] = jnp.where(lo, 0.0, q).astype(bf16)
    o_ref[:, 2 * C_QD:] = (_group_rms(y[:, C_QD:C_QD + C_KD], C_HD) * gk_ref[...]).astype(bf16)
    ones = jnp.ones((ONES_ROWS, LANES), bf16)
    for t in range(vt_ref.shape[0]):
        vt = y[t * LANES:(t + 1) * LANES, C_QD + C_KD:].T.astype(bf16)
        for g in range(C_KV_HEADS):
            vt_ref[t, g * C_VR:g * C_VR + 2 * C_HD, :] = vt[g * 2 * C_HD:(g + 1) * 2 * C_HD]
            vt_ref[t, g * C_VR + 2 * C_HD:(g + 1) * C_VR, :] = ones


def _proj_c(h, g, w, qk_norm, *, tm):
    n, d = h.shape
    nw = w.shape[1]
    nout = 2 * C_QD + C_KD
    gq = jnp.tile(qk_norm[0], C_QD // C_HD).reshape(1, C_QD)
    gk = jnp.tile(qk_norm[1], C_KD // C_HD).reshape(1, C_KD)
    return pl.pallas_call(
        _proj_c_kernel,
        grid=(n // tm,),
        in_specs=[
            pl.BlockSpec((tm, d), lambda i: (i, 0)),
            pl.BlockSpec((1, d), lambda i: (0, 0)),
            pl.BlockSpec((d, nw), lambda i: (0, 0)),
            pl.BlockSpec((1, C_QD), lambda i: (0, 0)),
            pl.BlockSpec((1, C_KD), lambda i: (0, 0)),
        ],
        out_specs=[
            pl.BlockSpec((tm, nout), lambda i: (i, 0)),
            pl.BlockSpec((tm // LANES, C_KV_HEADS * C_VR, LANES), lambda i: (i, 0, 0)),
        ],
        out_shape=[
            jax.ShapeDtypeStruct((n, nout), bf16),
            jax.ShapeDtypeStruct((n // LANES, C_KV_HEADS * C_VR, LANES), bf16),
        ],
        compiler_params=_cparams(("arbitrary",)),
        name="proj_c",
    )(h, g.reshape(1, d), w, gq, gk)


def _attn_c_kernel(sink_ref, qlo_ref, qhi_ref, k_ref, vt_ref, km_ref, vtm_ref, kmf_ref, vtmf_ref, bias_ref,
                   o_ref, qs_ref, *, nblk):
    s_id = pl.program_id(0)
    nb_real = pl.num_programs(0) - 1
    r_io, _ = _tile_iotas()
    vd = 2 * C_HD

    def attend(tiles):
        top = r_io < C_HD
        for g in range(C_KV_HEADS):
            for hh in range(C_GROUP):
                cc = (g * C_GROUP + hh) // 2
                src = qlo_ref if hh % 2 == 0 else qhi_ref
                qs_ref[hh * LANES:(hh + 1) * LANES, :] = src[:, cc * LANES:(cc + 1) * LANES]
            sink = jnp.concatenate(
                [jnp.full((1, LANES), sink_ref[g * C_GROUP + hh] * LOG2E, f32) for hh in range(C_GROUP)], axis=1)
            sts = []
            m = sink
            for (k_fn, vt_fn, bias_fn) in tiles:
                st = _dot_nt(k_fn(g), qs_ref[...])
                st = st + jnp.concatenate([bias_fn(g * C_GROUP + hh) for hh in range(C_GROUP)], axis=1)
                m = jnp.maximum(m, jnp.max(st, axis=0, keepdims=True))
                sts.append(st)
            acc = None
            for st, (k_fn, vt_fn, bias_fn) in zip(sts, tiles):
                pv = _dot(vt_fn(g), jnp.exp2(st - m).astype(bf16))
                acc = pv if acc is None else acc + pv
            o = acc[0:vd, :] / (acc[vd:vd + 1, :] + jnp.exp2(sink - m))
            for cc in range(C_GROUP // 2):
                even = o[:, (2 * cc) * LANES:(2 * cc + 1) * LANES]
                odd = o[:, (2 * cc + 1) * LANES:(2 * cc + 2) * LANES]
                col = (g * (C_GROUP // 2) + cc) * LANES
                o_ref[:, col:col + LANES] = jnp.where(top, even, odd).T.astype(bf16)

    @pl.when(s_id < nb_real)
    def _():
        i = s_id % nblk
        prev = jnp.maximum(i - 1, 0)
        poff = pl.multiple_of(prev * LANES, LANES)
        coff = pl.multiple_of(i * LANES, LANES)
        kind_m = jnp.where(i == 0, KIND_META0, KIND_FAR)
        kind_p = jnp.where(i == 0, KIND_MASKED, KIND_PREVWIN)
        attend([
            (lambda g: km_ref[0, :, g * LANES:(g + 1) * LANES], lambda g: vtm_ref[0, g * C_VR:(g + 1) * C_VR, :],
             lambda h: bias_ref[kind_m, h, 0:N_META, :]),
            (lambda g: k_ref[pl.ds(poff, LANES), g * LANES:(g + 1) * LANES],
             lambda g: vt_ref[prev, g * C_VR:(g + 1) * C_VR, :], lambda h: bias_ref[kind_p, h]),
            (lambda g: k_ref[pl.ds(coff, LANES), g * LANES:(g + 1) * LANES],
             lambda g: vt_ref[i, g * C_VR:(g + 1) * C_VR, :], lambda h: bias_ref[KIND_DIAG, h]),
        ])

    @pl.when(s_id == nb_real)
    def _():
        attend([(lambda g: kmf_ref[:, g * LANES:(g + 1) * LANES], lambda g: vtmf_ref[0, g * C_VR:(g + 1) * C_VR, :],
                 lambda h: bias_ref[KIND_METAMETA, h])])


def _attn_c(qkv, vt, bias, sinks, *, bsz, nblk):
    n = qkv.shape[0]
    nb_real = bsz * nblk
    seq = nblk * LANES
    vr = C_KV_HEADS * C_VR
    kern = functools.partial(_attn_c_kernel, nblk=nblk)
    bclamp = lambda s: jnp.minimum(s // nblk, bsz - 1)
    kcol = 2 * C_QD // C_KD
    km, vtm = _meta_views(qkv[nb_real * LANES:, 2 * C_QD:], vt[nb_real], bsz)
    return pl.pallas_call(
        kern,
        grid=(nb_real + 1,),
        in_specs=[
            pl.BlockSpec(memory_space=pltpu.SMEM),
            pl.BlockSpec((LANES, C_QD), lambda s: (s, 0)),
            pl.BlockSpec((LANES, C_QD), lambda s: (s, 1)),
            pl.BlockSpec((seq, C_KD), lambda s: (bclamp(s), kcol)),
            pl.BlockSpec((nblk, vr, LANES), lambda s: (bclamp(s), 0, 0)),
            pl.BlockSpec((1, N_META, C_KD), lambda s: (bclamp(s), 0, 0)),
            pl.BlockSpec((1, vr, N_META), lambda s: (bclamp(s), 0, 0)),
            pl.BlockSpec((LANES, C_KD), lambda s: (nb_real, kcol)),
            pl.BlockSpec((1, vr, LANES), lambda s: (nb_real, 0, 0)),
            _bias_spec(),
        ],
        out_specs=pl.BlockSpec((LANES, C_QD), lambda s: (s, 0)),
        out_shape=jax.ShapeDtypeStruct((n, C_QD), bf16),
        scratch_shapes=[pltpu.VMEM((C_GROUP * LANES, LANES), bf16)],
        compiler_params=_cparams(("arbitrary",)),
        name="attn_c",
    )(sinks, qkv, qkv, qkv, vt, km, vtm, qkv, vt, bias)


B_QA = B_HEADS * B_KV_RANK
B_QI = IDX_HEADS * IDX_DIM
B_W1 = 2 * B_Q_RANK + 2 * LANES
B_TR = B_KV_RANK + ONES_ROWS


def _proj_b_kernel(h_ref, g_ref, w1_ref, ln_ref, wuq_ref, qn_ref,
                   qa_ref, qi_ref, ckv_ref, ckvt_ref, kk_ref, wit_ref):
    xn = (_rms_rows(h_ref[...]) * g_ref[...]).astype(bf16)
    y = _dot(xn, w1_ref[...])
    r = B_Q_RANK
    cq = (_rms_rows(y[:, :r]) * ln_ref[0:1, :]).astype(bf16)
    ckv = _rms_rows(y[:, r:2 * r]) * ln_ref[1:2, :]
    ckv_ref[...] = ckv.astype(bf16)
    kk_ref[...] = _rms_rows(y[:, 2 * r:2 * r + LANES]).astype(bf16)
    wi = y[:, 2 * r + LANES:] * (IDX_HEADS ** -0.5)
    ones = jnp.ones((ONES_ROWS, LANES), bf16)
    for t in range(ckvt_ref.shape[0]):
        ckvt_ref[t, 0:r, :] = ckv[t * LANES:(t + 1) * LANES, :].T.astype(bf16)
        ckvt_ref[t, r:, :] = ones
        wit_ref[t] = wi[t * LANES:(t + 1) * LANES, :].T[0:IDX_HEADS, :]
    z = _dot(cq, wuq_ref[...])
    qa = (_group_rms(z[:, :B_QA], B_KV_RANK) * qn_ref[...] * (B_KV_RANK ** -0.5 * LOG2E)).astype(bf16)
    qi = z[:, B_QA:] * (IDX_DIM ** -0.5)
    lo = _lo_half_mask(qi.shape, 2 * IDX_DIM, IDX_DIM)
    qi_lo = jnp.where(lo, qi, 0.0).astype(bf16)
    qi_hi = jnp.where(lo, 0.0, qi).astype(bf16)
    for t in range(qa_ref.shape[0]):
        rows = slice(t * LANES, (t + 1) * LANES)
        for hd in range(B_HEADS):
            qa_ref[t, hd * LANES:(hd + 1) * LANES, :] = qa[rows, hd * r:(hd + 1) * r]
        for hh in range(IDX_HEADS):
            src = qi_lo if hh % 2 == 0 else qi_hi
            qi_ref[t, hh * LANES:(hh + 1) * LANES, :] = src[rows, (hh // 2) * LANES:(hh // 2 + 1) * LANES]


def _proj_b(h, g, w1, latent_norm, wuq, q_norm, *, tm):
    n, d = h.shape
    qn = jnp.tile(q_norm, B_HEADS).reshape(1, B_QA)
    row = lambda i: (i, 0)
    row3 = lambda i: (i, 0, 0)
    const = lambda i: (0, 0)
    nt = tm // LANES
    return pl.pallas_call(
        _proj_b_kernel,
        grid=(n // tm,),
        in_specs=[
            pl.BlockSpec((tm, d), row),
            pl.BlockSpec((1, d), const),
            pl.BlockSpec(w1.shape, const),
            pl.BlockSpec(latent_norm.shape, const),
            pl.BlockSpec(wuq.shape, const),
            pl.BlockSpec((1, B_QA), const),
        ],
        out_specs=[
            pl.BlockSpec((nt, B_HEADS * LANES, B_KV_RANK), row3),
            pl.BlockSpec((nt, IDX_HEADS * LANES, LANES), row3),
            pl.BlockSpec((tm, B_KV_RANK), row),
            pl.BlockSpec((nt, B_TR, LANES), row3),
            pl.BlockSpec((tm, LANES), row),
            pl.BlockSpec((nt, IDX_HEADS, LANES), row3),
        ],
        out_shape=[
            jax.ShapeDtypeStruct((n // LANES, B_HEADS * LANES, B_KV_RANK), bf16),
            jax.ShapeDtypeStruct((n // LANES, IDX_HEADS * LANES, LANES), bf16),
            jax.ShapeDtypeStruct((n, B_KV_RANK), bf16),
            jax.ShapeDtypeStruct((n // LANES, B_TR, LANES), bf16),
            jax.ShapeDtypeStruct((n, LANES), bf16),
            jax.ShapeDtypeStruct((n // LANES, IDX_HEADS, LANES), f32),
        ],
        compiler_params=_cparams(("arbitrary",)),
        name="proj_b",
    )(h, g.reshape(1, d), w1, latent_norm, wuq, qn)


def _attn_b_kernel(qs_ref, is_ref, wit_ref, ckv_ref, ckvt_ref, kk_ref, ckvm_ref, ckvtm_ref, kkm_ref,
                   ckvmf_ref, ckvtmf_ref, bias_ref, wuvt_ref,
                   o_ref, key_ref, pen_ref, m_ref, acc_ref, gap_ref, *, nblk, k_sel):
    s_id = pl.program_id(0)
    nb_real = pl.num_programs(0) - 1
    r_io, c_io = _tile_iotas()
    rk = B_KV_RANK

    def init():
        m_ref[...] = jnp.full(m_ref.shape, NEG_INF, f32)
        acc_ref[...] = jnp.zeros(acc_ref.shape, f32)

    def add_per_head(st, bias_fn, pen):
        cols = []
        for h in range(B_HEADS):
            add = pen if bias_fn is None else (bias_fn(h) if pen is None else bias_fn(h) + pen)
            cols.append(st[:, h * LANES:(h + 1) * LANES] + add)
        return jnp.concatenate(cols, axis=1)

    def step_fns(ckv_fn, ckvt_fns, bias_fn, pen_fn):
        def logits():
            st = _dot_nt(ckv_fn(), qs_ref[0])
            return add_per_head(st, bias_fn, None if pen_fn is None else pen_fn())

        def pv(pb):
            acc = None
            for ckvt_fn, r0 in ckvt_fns:
                ckvt = ckvt_fn()
                part = _dot(ckvt, pb[r0:r0 + ckvt.shape[1], :])
                acc = part if acc is None else acc + part
            return acc

        return logits, pv

    def finalize():
        olat = (acc_ref[0:rk, :] / acc_ref[rk:rk + 1, :]).astype(bf16)
        ot = jnp.concatenate([_dot(wuvt_ref[h], olat[:, h * LANES:(h + 1) * LANES]) for h in range(B_HEADS)], axis=0)
        o_ref[...] = ot.T.astype(bf16)

    def index_scores(kk):
        s = jnp.maximum(_dot_nt(kk, is_ref[0]), 0.0)
        wt = wit_ref[0]
        sc = jnp.zeros((kk.shape[0], LANES), f32)
        for hh in range(IDX_HEADS):
            sc = sc + wt[hh:hh + 1, :] * s[:, hh * LANES:(hh + 1) * LANES]
        return sc

    def sort_key(sc):
        bits = lax.bitcast_convert_type(sc + 0.0, jnp.int32)
        return jnp.where(bits < 0, bits ^ jnp.int32(0x7FFFFFFF), bits)

    @pl.when(s_id < nb_real)
    def _():
        i = s_id % nblk
        ntile = i + 2

        int_min_tile = jnp.full((LANES, LANES), INT_MIN, jnp.int32)
        key_ref[0] = int_min_tile
        key_ref[0, 0:N_META, :] = sort_key(index_scores(kkm_ref[0]))
        key_ref[i + 2] = int_min_tile

        def score_body(jp, carry):
            off = pl.multiple_of(jp * (2 * LANES), 2 * LANES)
            keys = sort_key(index_scores(kk_ref[pl.ds(off, 2 * LANES), :]))
            for t in range(2):
                j = _vec(2 * jp + t)
                vis = (j < i) | ((j == i) & ((r_io >> 6) <= (c_io >> 6)))
                key_ref[2 * jp + t + 1] = jnp.where(vis, keys[t * LANES:(t + 1) * LANES], jnp.int32(INT_MIN))
            return carry

        lax.fori_loop(0, i // 2 + 1, score_body, 0)

        def count(pred):
            def cbody(tp, accv):
                for t in (2 * tp, 2 * tp + 1):
                    accv = accv + jnp.where(pred(key_ref[t], t), 1.0, 0.0)
                return accv
            accv = lax.fori_loop(0, (ntile + 1) // 2, cbody, jnp.zeros((LANES, LANES), f32))
            return jnp.sum(accv, axis=0, keepdims=True)

        kf = float(k_sel)
        zero = jnp.zeros((1, LANES), jnp.int32)
        t0 = jnp.where(count(lambda k, t: k >= zero) >= kf, zero, jnp.int32(INT_MIN))

        def bit_body(it, tcur):
            cand = tcur | jnp.left_shift(jnp.int32(1), 30 - it)
            return jnp.where(count(lambda k, t: k >= cand) >= kf, cand, tcur)

        thr = lax.fori_loop(0, 31, bit_body, t0)

        need = kf - count(lambda k, t: k > thr)
        n_eq = count(lambda k, t: k == thr)
        has_thr = thr > jnp.int32(INT_MIN)
        tied = jnp.max(jnp.where(has_thr & (n_eq > need), 1.0, 0.0)) > 0.0

        def tie_search(_):
            def jbody(it, jcur):
                cand = jcur | jnp.left_shift(jnp.int32(1), 11 - it)
                cnt = count(lambda k, t: (k == thr) & ((t * LANES + r_io) < cand))
                return jnp.where(cnt < need, cand, jcur)
            return lax.fori_loop(0, 12, jbody, jnp.zeros((1, LANES), jnp.int32))

        j_last = lax.cond(tied, tie_search, lambda _: jnp.full((1, LANES), 4095, jnp.int32), 0)
        j_last = jnp.where(has_thr, j_last, -1)

        def pen_body(t, carry):
            k = key_ref[t]
            sel = (k > thr) | ((k == thr) & ((t * LANES + r_io) <= j_last))
            pen_ref[t] = jnp.where(sel, 0.0, NEG_INF)
            return carry

        lax.fori_loop(0, ntile, pen_body, 0)
        for t in range(1, NSUB):
            pen_ref[i + 1 + t] = jnp.full((LANES, LANES), NEG_INF, f32)

        kind_m = jnp.where(i == 0, KIND_META0, KIND_FAR)
        nfar, near0, nnear = _sweep_steps(i)

        def ckv_fn(b0, nsub):
            off = pl.multiple_of(b0 * LANES, NSUB * LANES)
            return lambda: ckv_ref[pl.ds(off, nsub * LANES), :]

        def ckvt_fn(b0, nsub):
            return lambda: jnp.concatenate([ckvt_ref[b0 + t] for t in range(nsub)], axis=1)

        def pen_fn(b0, nsub):
            return lambda: jnp.concatenate([pen_ref[b0 + t + 1] for t in range(nsub)], axis=0)

        def near_bias(b0):
            kinds = [_block_kind(b0 + t - i) for t in range(NSUB)]
            return lambda h: jnp.concatenate([bias_ref[kinds[t], h] for t in range(NSUB)], axis=0)

        def first_pen():
            return jnp.concatenate([pen_ref[0, 0:N_META, :], pen_fn(0, NSUB)()], axis=0)

        def first_fns(with_pen):
            bias01 = near_bias(0)
            return step_fns(lambda: jnp.concatenate([ckvm_ref[0], ckv_fn(0, NSUB)()], axis=0),
                            [(lambda: ckvtm_ref[0], 0), (ckvt_fn(0, NSUB), N_META)],
                            lambda h: jnp.concatenate([bias_ref[kind_m, h, 0:N_META, :], bias01(h)], axis=0),
                            first_pen if with_pen else None)

        def far_step(w):
            b0 = NSUB + w * NSUB_FAR
            far_bias = jnp.concatenate([bias_ref[KIND_FAR, h, 0:1, :] for h in range(B_HEADS)], axis=1)
            return step_fns(ckv_fn(b0, NSUB_FAR), [(ckvt_fn(b0, NSUB_FAR), 0)], None, pen_fn(b0, NSUB_FAR)) + (far_bias,)

        def near_step(u):
            b0 = near0 + u * NSUB
            return step_fns(ckv_fn(b0, NSUB), [(ckvt_fn(b0, NSUB), 0)], near_bias(b0), pen_fn(b0, NSUB)) + (None,)

        def sweep(gap_ref):
            init()
            if gap_ref is None:
                _softmax_step(*first_fns(True), m_ref, acc_ref)
            else:
                _softmax_step(*first_fns(False), m_ref, acc_ref,
                              post_fn=lambda st: add_per_head(st, None, first_pen()))
            _softmax_loop(0, nfar, far_step, m_ref, acc_ref, gap_ref)
            _softmax_loop(0, nnear, near_step, m_ref, acc_ref, gap_ref)

        gap_ref[...] = jnp.full(gap_ref.shape, NEG_INF, f32)
        sweep(gap_ref)
        faded = jnp.logical_not(jnp.min(acc_ref[rk:rk + 1, :]) >= LAZY_FLOOR)
        pl.when(_overshot(gap_ref) | faded)(lambda: sweep(None))
        finalize()

    @pl.when(s_id == nb_real)
    def _():
        init()
        _softmax_step(*step_fns(lambda: ckvmf_ref[...], [(lambda: ckvtmf_ref[0], 0)],
                                lambda h: bias_ref[KIND_METAMETA, h], None), m_ref, acc_ref)
        finalize()


def _attn_b(qa, qi, wit, ckv, ckvt, kk, bias, wuvt, *, bsz, nblk, k_sel):
    n = ckv.shape[0]
    nb_real = bsz * nblk
    seq = nblk * LANES
    assert k_sel >= N_META and (nblk + 1) * LANES <= 4096 and nblk % NSUB == 0
    ckvm, ckvtm = _meta_views(ckv[nb_real * LANES:], ckvt[nb_real], bsz)
    kkm = kk[nb_real * LANES:].reshape(bsz, N_META, LANES)
    kern = functools.partial(_attn_b_kernel, nblk=nblk, k_sel=k_sel)
    bidx = lambda s: jnp.minimum(s // nblk, bsz - 1)
    blk = lambda s: (s, 0)
    return pl.pallas_call(
        kern,
        grid=(nb_real + 1,),
        in_specs=[
            pl.BlockSpec((1,) + qa.shape[1:], lambda s: (s, 0, 0)),
            pl.BlockSpec((1,) + qi.shape[1:], lambda s: (s, 0, 0)),
            pl.BlockSpec((1, IDX_HEADS, LANES), lambda s: (s, 0, 0)),
            pl.BlockSpec((seq, B_KV_RANK), lambda s: (bidx(s), 0)),
            pl.BlockSpec((nblk, B_TR, LANES), lambda s: (bidx(s), 0, 0)),
            pl.BlockSpec((seq, LANES), lambda s: (bidx(s), 0)),
            pl.BlockSpec((1, N_META, B_KV_RANK), lambda s: (bidx(s), 0, 0)),
            pl.BlockSpec((1, B_TR, N_META), lambda s: (bidx(s), 0, 0)),
            pl.BlockSpec((1, N_META, LANES), lambda s: (bidx(s), 0, 0)),
            pl.BlockSpec((LANES, B_KV_RANK), lambda s: (nb_real, 0)),
            pl.BlockSpec((1, B_TR, LANES), lambda s: (nb_real, 0, 0)),
            _bias_spec(),
            pl.BlockSpec(wuvt.shape, lambda s: (0, 0, 0)),
        ],
        out_specs=pl.BlockSpec((LANES, B_HEADS * B_VD), blk),
        out_shape=jax.ShapeDtypeStruct((n, B_HEADS * B_VD), bf16),
        scratch_shapes=[
            pltpu.VMEM((nblk + 2, LANES, LANES), jnp.int32),
            pltpu.VMEM((nblk + NSUB, LANES, LANES), f32),
            pltpu.VMEM((1, B_HEADS * LANES), f32),
            pltpu.VMEM((B_TR, B_HEADS * LANES), f32),
            pltpu.VMEM((1, B_HEADS * LANES), f32),
        ],
        compiler_params=_cparams(("arbitrary",)),
        name="attn_b",
    )(qa, qi, wit, ckv, ckvt, kk, ckvm, ckvtm, kkm, ckv, ckvt, bias, wuvt)


def kernel(x, meta_tokens, rel_bias, ln_ffn1, ffn1_wi, ffn1_wo, ln_mix, w_out, ln_ffn2, ffn2_wi, ffn2_wo, a_w_in, a_qk_norm, a_lambda, a_subln, b_w_in, b_latent_norm, b_w_uq, b_q_norm, b_w_uv, c_w_in, c_qk_norm, c_sinks):
    bsz, seq, d = x.shape
    assert d == D_MODEL and seq % LANES == 0 and bsz * N_META == LANES
    nblk = seq // LANES
    n = bsz * seq + LANES
    k_sel = min(TOPK_MAX, seq // 4)
    tm_ffn = _row_tile(n, 1408)
    tm_proj = _row_tile(n, 384, LANES)
    fc = 256

    h = jnp.concatenate([x.reshape(bsz * seq, d),
                         jnp.broadcast_to(meta_tokens.astype(x.dtype), (bsz, N_META, d)).reshape(LANES, d)], axis=0)
    bias = _bias_tiles(rel_bias)

    for layer in range(DEPTH):
        h = _ffn(h, ln_ffn1[layer], ffn1_wi, ffn1_wo, layer, tm=tm_ffn, fc=fc)
        kind, j = layer % N_MIXERS, layer // N_MIXERS
        g = ln_mix[layer]
        if kind == 0:
            lambda_init = 0.8 - 0.6 * math.exp(-0.3 * layer)
            qs, k, vt = _proj_a(h, g, a_w_in[j].astype(bf16), a_qk_norm[j], tm=tm_proj)
            mix = _attn_a(qs, k, vt, bias, a_lambda[j], a_subln[j], bsz=bsz, nblk=nblk, lambda_init=lambda_init)
        elif kind == 1:
            w = b_w_in[j]
            r2 = B_Q_RANK + B_KV_RANK
            kcol = w[:, r2:r2 + IDX_DIM]
            w1 = jnp.concatenate([w[:, :r2], kcol, kcol, w[:, r2 + IDX_DIM:],
                                  jnp.zeros((d, LANES - IDX_HEADS), w.dtype)], axis=1).astype(bf16)
            assert w1.shape[1] == B_W1
            qa, qi, ckv, ckvt, kk, wit = _proj_b(h, g, w1, b_latent_norm[j], b_w_uq[j].astype(bf16), b_q_norm[j],
                                                 tm=tm_proj)
            wuvt = jnp.swapaxes(b_w_uv[j], 1, 2).astype(bf16)
            mix = _attn_b(qa, qi, wit, ckv, ckvt, kk, bias, wuvt, bsz=bsz, nblk=nblk, k_sel=k_sel)
        else:
            w = c_w_in[j]
            kcols = [w[:, C_QD + gi * C_HD:C_QD + (gi + 1) * C_HD] for gi in range(C_KV_HEADS)]
            voff = C_QD + C_KV_HEADS * C_HD
            vcols = [w[:, voff + gi * C_HD:voff + (gi + 1) * C_HD] for gi in range(C_KV_HEADS)]
            wc = jnp.concatenate([w[:, :C_QD]] + [kc for kc in kcols for _ in range(2)]
                                 + [vc for vc in vcols for _ in range(2)], axis=1).astype(bf16)
            qkv, vt = _proj_c(h, g, wc, c_qk_norm[j], tm=tm_proj)
            mix = _attn_c(qkv, vt, bias, c_sinks[j], bsz=bsz, nblk=nblk)
        h = _ffn(h, ln_ffn2[layer], ffn2_wi, ffn2_wo, layer, tm=tm_ffn, fc=fc,
                 mix=mix, wout=w_out[layer].astype(bf16))
    return h[:bsz * seq].reshape(bsz, seq, d)
```

```python
import functools
import math

import numpy as np
import jax
import jax.numpy as jnp
from jax import lax
from jax.experimental import pallas as pl
from jax.experimental.pallas import tpu as pltpu

D_MODEL = 1024
DEPTH = 4
CHUNK = 64
N_META = 16
N_MIXERS = 3
NEG_INF = -1e30
REL_BUCKETS = 32
REL_MAX_DIST = 128
REL_HEADS = 16
D_FF = 2816
A_HEADS = 8
A_HD = 64
A_VD = 2 * A_HD
B_HEADS = 16
B_Q_RANK = 256
B_KV_RANK = 256
B_VD = 64
IDX_HEADS = 8
IDX_DIM = 64
TOPK_MAX = 256
C_Q_HEADS = 16
C_KV_HEADS = 2
C_GROUP = C_Q_HEADS // C_KV_HEADS
C_HD = 64
EPS = 1e-6

LANES = 128
BF16_ROWS = 16
VMEM_LIMIT = 56 * 1024 * 1024
INT_MIN = -(2 ** 31)
NSUB = 2
NSUB_FAR = 4
LOG2E = math.log2(math.e)
LAZY_GAP = 57.0
LAZY_FLOOR = 2.0 ** -100
ONES_ROWS = BF16_ROWS

KIND_DIAG, KIND_PREV, KIND_FAR, KIND_META0, KIND_METAMETA, KIND_MASKED, KIND_PREVWIN = 0, 1, 2, 3, 4, 5, 6
N_KINDS = 7

f32 = jnp.float32
bf16 = jnp.bfloat16


def _cparams(sem):
    return pltpu.CompilerParams(dimension_semantics=sem, vmem_limit_bytes=VMEM_LIMIT)


def _row_tile(n, cap, mult=BF16_ROWS):
    best = None
    for t in range(mult, cap + 1, mult):
        if n % t == 0:
            best = t
    assert best is not None
    return best


def _dot(a, b):
    return jnp.dot(a, b, preferred_element_type=f32)


def _dot_nt(a, b):
    return lax.dot_general(a, b, (((1,), (1,)), ((), ())), preferred_element_type=f32)


def _rms_rows(x):
    return x * lax.rsqrt(jnp.mean(x * x, axis=-1, keepdims=True) + EPS)


def _lo_half_mask(shape, period, half):
    return (lax.broadcasted_iota(jnp.int32, shape, 1) & (period - 1)) < half


def _group_rms(x, group):
    r, c = x.shape
    outs = []
    if group == 64:
        lo = _lo_half_mask((r, LANES), LANES, 64)
        for ci in range(c // LANES):
            xc = x[:, ci * LANES:(ci + 1) * LANES]
            x2 = xc * xc
            s_lo = jnp.sum(jnp.where(lo, x2, 0.0), axis=-1, keepdims=True)
            s_hi = jnp.sum(jnp.where(lo, 0.0, x2), axis=-1, keepdims=True)
            inv = jnp.where(lo, lax.rsqrt(s_lo * (1.0 / 64) + EPS), lax.rsqrt(s_hi * (1.0 / 64) + EPS))
            outs.append(xc * inv)
    else:
        for gi in range(c // group):
            outs.append(_rms_rows(x[:, gi * group:(gi + 1) * group]))
    return outs[0] if len(outs) == 1 else jnp.concatenate(outs, axis=-1)


def _tile_iotas():
    r = lax.broadcasted_iota(jnp.int32, (LANES, LANES), 0)
    c = lax.broadcasted_iota(jnp.int32, (LANES, LANES), 1)
    return r, c


def _vec(s):
    return jnp.full((LANES, LANES), s, jnp.int32)


def _softmax_step(logits_fn, pv_fn, m_ref, acc_ref, gap_ref=None, offset=None, post_fn=None):
    st = logits_fn()
    if gap_ref is None:
        if offset is not None:
            st = st + offset
        m_old = m_ref[...]
        m_new = jnp.maximum(m_old, jnp.max(st, axis=0, keepdims=True))
        if post_fn is not None:
            st = post_fn(st)
        acc_ref[...] = jnp.exp2(m_old - m_new) * acc_ref[...] + pv_fn(jnp.exp2(st - m_new).astype(bf16))
        m_ref[...] = m_new
    else:
        shift = m_ref[...] if offset is None else m_ref[...] - offset
        gap_ref[...] = jnp.maximum(gap_ref[...], jnp.max(st, axis=0, keepdims=True) - shift)
        acc_ref[...] += pv_fn(jnp.exp2(st - shift).astype(bf16))


def _softmax_loop(lo, hi, step_fn, m_ref, acc_ref, gap_ref):
    def body(w, carry):
        logits_fn, pv_fn, offset = step_fn(w)
        _softmax_step(logits_fn, pv_fn, m_ref, acc_ref, gap_ref, offset)
        return carry

    lax.fori_loop(lo, hi, body, 0)


def _overshot(gap_ref):
    return jnp.logical_not(jnp.max(gap_ref[...]) <= LAZY_GAP)


def _block_kind(rel):
    return jnp.where(rel < -1, KIND_FAR,
                     jnp.where(rel == -1, KIND_PREV, jnp.where(rel == 0, KIND_DIAG, KIND_MASKED)))


def _sweep_steps(i):
    nfar = jnp.maximum(i - 1 - NSUB, 0) // NSUB_FAR
    near0 = NSUB + nfar * NSUB_FAR
    return nfar, near0, (i - near0 + NSUB) // NSUB


def _ffn_kernel(*refs, fuse_out, layer, fc, nj):
    if fuse_out:
        h_ref, mix_ref, wout_ref, g_ref, wi_hbm, wo_hbm, o_ref, xn_ref, wa_buf, wb_buf, wo_buf, sem = refs
    else:
        h_ref, g_ref, wi_hbm, wo_hbm, o_ref, xn_ref, wa_buf, wb_buf, wo_buf, sem = refs

    def chunk_copies(j, slot):
        lo = pl.multiple_of(j * fc, fc)
        hi = pl.multiple_of((nj + j) * fc, fc)
        return (pltpu.make_async_copy(wi_hbm.at[layer, :, pl.ds(lo, fc)], wa_buf.at[slot], sem.at[0, slot]),
                pltpu.make_async_copy(wi_hbm.at[layer, :, pl.ds(hi, fc)], wb_buf.at[slot], sem.at[1, slot]),
                pltpu.make_async_copy(wo_hbm.at[layer, pl.ds(lo, fc), :], wo_buf.at[slot], sem.at[2, slot]))

    for c in chunk_copies(0, 0):
        c.start()
    r = h_ref[...]
    if fuse_out:
        r = r + _dot(mix_ref[...], wout_ref[...])
    o_ref[...] = r
    xn_ref[...] = (_rms_rows(r) * g_ref[...]).astype(bf16)

    def body(j, carry):
        slot = j & 1

        @pl.when(j + 1 < nj)
        def _():
            for c in chunk_copies(j + 1, 1 - slot):
                c.start()

        for c in chunk_copies(j, slot):
            c.wait()
        xn = xn_ref[...]
        a = _dot(xn, wa_buf[slot].astype(bf16))
        b = _dot(xn, wb_buf[slot].astype(bf16))
        act = (a / (1.0 + jnp.exp(-a)) * b).astype(bf16)
        o_ref[...] += 0.5 * _dot(act, wo_buf[slot].astype(bf16))
        return carry

    lax.fori_loop(0, nj, body, 0)


def _ffn(h, g, wi, wo, layer, *, tm, fc, mix=None, wout=None):
    n, d = h.shape
    dff = wo.shape[1]
    nj = dff // fc
    fuse = mix is not None
    row = lambda i: (i, 0)
    in_specs = [pl.BlockSpec((tm, d), row)]
    args = [h]
    if fuse:
        in_specs += [pl.BlockSpec((tm, mix.shape[1]), row), pl.BlockSpec(wout.shape, lambda i: (0, 0))]
        args += [mix, wout]
    in_specs += [
        pl.BlockSpec((1, d), lambda i: (0, 0)),
        pl.BlockSpec(memory_space=pl.ANY),
        pl.BlockSpec(memory_space=pl.ANY),
    ]
    args += [g.reshape(1, d), wi, wo]
    return pl.pallas_call(
        functools.partial(_ffn_kernel, fuse_out=fuse, layer=layer, fc=fc, nj=nj),
        grid=(n // tm,),
        in_specs=in_specs,
        out_specs=pl.BlockSpec((tm, d), row),
        out_shape=jax.ShapeDtypeStruct((n, d), f32),
        scratch_shapes=[
            pltpu.VMEM((tm, d), bf16),
            pltpu.VMEM((2, d, fc), f32),
            pltpu.VMEM((2, d, fc), f32),
            pltpu.VMEM((2, fc, d), f32),
            pltpu.SemaphoreType.DMA((3, 2)),
        ],
        compiler_params=_cparams(("arbitrary",)),
        name="ffn_out" if fuse else "ffn",
    )(*args)


def _rel_bucket(rel):
    half = REL_BUCKETS // 2
    max_exact = half // 2
    n = jnp.abs(rel)
    large = max_exact + (jnp.log(jnp.maximum(n, 1).astype(jnp.float32) / max_exact)
                         / math.log(REL_MAX_DIST / max_exact) * (half - max_exact)).astype(jnp.int32)
    large = jnp.minimum(large, half - 1)
    return jnp.where(rel > 0, half, 0) + jnp.where(n < max_exact, n, large)


def _rel_tiles():
    k = np.arange(LANES)[:, None]
    q = np.arange(LANES)[None, :]
    far = np.full((LANES, LANES), -4 * LANES)
    ones = np.ones((LANES, LANES), bool)
    rels = [k - q, k - q - LANES, far, (k % N_META) - N_META - q, (k % N_META) - (q % N_META), far, k - q - LANES]
    vis = [(k // CHUNK) <= (q // CHUNK), ones, ones, ones, (k // N_META) == (q // N_META), ~ones,
           (q < CHUNK) | (k >= CHUNK)]
    return (np.stack([np.broadcast_to(a, (LANES, LANES)) for a in rels]).astype(np.int32),
            np.stack([np.broadcast_to(a, (LANES, LANES)) for a in vis]).astype(np.int32))


def _bias_kernel(rb_ref, bucket_ref, vis_ref, o_ref):
    h = pl.program_id(0)
    for kind in range(N_KINDS):
        bk = bucket_ref[kind]
        acc = jnp.zeros((LANES, LANES), f32)
        for b in range(REL_BUCKETS):
            acc = jnp.where(bk == b, rb_ref[b, h], acc)
        o_ref[kind, 0] = jnp.where(vis_ref[kind] != 0, acc * LOG2E, NEG_INF)


def _bias_tiles(rel_bias):
    rel, vis = _rel_tiles()
    bucket = _rel_bucket(jnp.asarray(rel))
    nk = N_KINDS
    return pl.pallas_call(
        _bias_kernel,
        grid=(REL_HEADS,),
        in_specs=[
            pl.BlockSpec(memory_space=pltpu.SMEM),
            pl.BlockSpec((nk, LANES, LANES), lambda h: (0, 0, 0)),
            pl.BlockSpec((nk, LANES, LANES), lambda h: (0, 0, 0)),
        ],
        out_specs=pl.BlockSpec((nk, 1, LANES, LANES), lambda h: (0, h, 0, 0)),
        out_shape=jax.ShapeDtypeStruct((nk, REL_HEADS, LANES, LANES), f32),
        compiler_params=_cparams(("arbitrary",)),
        name="bias_tiles",
    )(rel_bias, bucket, jnp.asarray(vis))


def _bias_spec():
    return pl.BlockSpec((N_KINDS, REL_HEADS, LANES, LANES), lambda s: (0, 0, 0, 0))


def _meta_views(rows, cols_t, bsz):
    f = rows.shape[1]
    return (rows.reshape(bsz, N_META, f),
            cols_t.reshape(cols_t.shape[0], bsz, N_META).transpose(1, 0, 2))


A_QD = A_HEADS * 2 * A_HD
A_VR = A_VD + ONES_ROWS


def _proj_a_kernel(h_ref, g_ref, w_ref, gq_ref, gk_ref, qs_ref, k_ref, vt_ref):
    xn = (_rms_rows(h_ref[...]) * g_ref[...]).astype(bf16)
    y = _dot(xn, w_ref[...])
    q = _group_rms(y[:, :A_QD], A_HD) * gq_ref[...] * (A_HD ** -0.5 * LOG2E)
    lo = _lo_half_mask(q.shape, 2 * A_HD, A_HD)
    q_lo = jnp.where(lo, q, 0.0).astype(bf16)
    q_hi = jnp.where(lo, 0.0, q).astype(bf16)
    k_ref[...] = (_group_rms(y[:, A_QD:2 * A_QD], A_HD) * gk_ref[...]).astype(bf16)
    ones = jnp.ones((ONES_ROWS, LANES), bf16)
    for t in range(vt_ref.shape[0]):
        rows = slice(t * LANES, (t + 1) * LANES)
        vt = y[rows, 2 * A_QD:].T.astype(bf16)
        for h in range(A_HEADS):
            qs_ref[t, h, :LANES, :] = q_lo[rows, h * A_VD:(h + 1) * A_VD]
            qs_ref[t, h, LANES:, :] = q_hi[rows, h * A_VD:(h + 1) * A_VD]
            vt_ref[t, h * A_VR:h * A_VR + A_VD, :] = vt[h * A_VD:(h + 1) * A_VD]
            vt_ref[t, h * A_VR + A_VD:(h + 1) * A_VR, :] = ones


def _proj_a(h, g, w, qk_norm, *, tm):
    n, d = h.shape
    nw = w.shape[1]
    gq = jnp.tile(qk_norm[0], A_QD // A_HD).reshape(1, A_QD)
    gk = jnp.tile(qk_norm[1], A_QD // A_HD).reshape(1, A_QD)
    return pl.pallas_call(
        _proj_a_kernel,
        grid=(n // tm,),
        in_specs=[
            pl.BlockSpec((tm, d), lambda i: (i, 0)),
            pl.BlockSpec((1, d), lambda i: (0, 0)),
            pl.BlockSpec((d, nw), lambda i: (0, 0)),
            pl.BlockSpec((1, A_QD), lambda i: (0, 0)),
            pl.BlockSpec((1, A_QD), lambda i: (0, 0)),
        ],
        out_specs=[
            pl.BlockSpec((tm // LANES, A_HEADS, 2 * LANES, A_VD), lambda i: (i, 0, 0, 0)),
            pl.BlockSpec((tm, A_QD), lambda i: (i, 0)),
            pl.BlockSpec((tm // LANES, A_HEADS * A_VR, LANES), lambda i: (i, 0, 0)),
        ],
        out_shape=[
            jax.ShapeDtypeStruct((n // LANES, A_HEADS, 2 * LANES, A_VD), bf16),
            jax.ShapeDtypeStruct((n, A_QD), bf16),
            jax.ShapeDtypeStruct((n // LANES, A_HEADS * A_VR, LANES), bf16),
        ],
        compiler_params=_cparams(("arbitrary",)),
        name="proj_a",
    )(h, g.reshape(1, d), w, gq, gk)


def _attn_a_kernel(qs_ref, k_ref, vt_ref, km_ref, vtm_ref, kmf_ref, vtmf_ref, bias_ref, lam_ref, sub_ref,
                   o_ref, m_ref, acc_ref, gap_ref, *, nblk, lambda_init):
    s_id = pl.program_id(0)
    nb_real = pl.num_programs(0) - 1
    hw = 2 * A_HD

    def init():
        m_ref[...] = jnp.full(m_ref.shape, NEG_INF, f32)
        acc_ref[...] = jnp.zeros(acc_ref.shape, f32)

    def step_fns(kt_fn, vtt_fns, bias_fn):
        def logits():
            sts = [_dot_nt(kt_fn(h), qs_ref[0, h]) for h in range(A_HEADS)]
            if bias_fn is not None:
                sts = [st + bias_fn(h) for h, st in enumerate(sts)]
            return jnp.concatenate(sts, axis=1)

        def pv(pb):
            outs = []
            for h in range(A_HEADS):
                acc = None
                for vtt_fn, r0 in vtt_fns:
                    vtt = vtt_fn(h)
                    part = _dot(vtt, pb[r0:r0 + vtt.shape[1], h * 2 * LANES:(h + 1) * 2 * LANES])
                    acc = part if acc is None else acc + part
                outs.append(acc)
            return jnp.concatenate(outs, axis=1)

        return logits, pv

    def bias_rows(kind, h, rows):
        return jnp.concatenate([bias_ref[kind, h, 0:rows, :], bias_ref[kind, A_HEADS + h, 0:rows, :]], axis=1)

    def finalize():
        lam = lam_ref[...]
        lam_full = (jnp.exp(jnp.sum(lam[0:1] * lam[1:2], axis=-1, keepdims=True))
                    - jnp.exp(jnp.sum(lam[2:3] * lam[3:4], axis=-1, keepdims=True)) + lambda_init)
        o = acc_ref[0:A_VD, :] / acc_ref[A_VD:A_VD + 1, :]
        for h in range(A_HEADS):
            d = o[:, 2 * h * LANES:(2 * h + 1) * LANES] - lam_full * o[:, (2 * h + 1) * LANES:(2 * h + 2) * LANES]
            d = d * lax.rsqrt(jnp.mean(d * d, axis=0, keepdims=True) + EPS) * sub_ref[...] * (1.0 - lambda_init)
            o_ref[:, h * hw:(h + 1) * hw] = d.T.astype(bf16)

    @pl.when(s_id < nb_real)
    def _():
        i = s_id % nblk
        kind_m = jnp.where(i == 0, KIND_META0, KIND_FAR)
        nfar, near0, nnear = _sweep_steps(i)

        def k_fn(b0, nsub):
            off = pl.multiple_of(b0 * LANES, NSUB * LANES)
            return lambda h: k_ref[pl.ds(off, nsub * LANES), h * hw:(h + 1) * hw]

        def vt_fn(b0, nsub):
            return lambda h: jnp.concatenate([vt_ref[b0 + t, h * A_VR:(h + 1) * A_VR, :] for t in range(nsub)], axis=1)

        def near_bias(b0):
            kinds = [_block_kind(b0 + t - i) for t in range(NSUB)]
            return [lambda h, kind=kind: bias_rows(kind, h, LANES) for kind in kinds]

        def first_fns():
            biases = [lambda h: bias_rows(kind_m, h, N_META)] + near_bias(0)
            return step_fns(
                lambda h: jnp.concatenate([km_ref[0, :, h * hw:(h + 1) * hw], k_fn(0, NSUB)(h)], axis=0),
                [(lambda h: vtm_ref[0, h * A_VR:(h + 1) * A_VR, :], 0), (vt_fn(0, NSUB), N_META)],
                lambda h: jnp.concatenate([b(h) for b in biases], axis=0))

        def far_step(w):
            b0 = NSUB + w * NSUB_FAR
            far_bias = jnp.concatenate([bias_rows(KIND_FAR, h, 1) for h in range(A_HEADS)], axis=1)
            return step_fns(k_fn(b0, NSUB_FAR), [(vt_fn(b0, NSUB_FAR), 0)], None) + (far_bias,)

        def near_step(u):
            b0 = near0 + u * NSUB
            biases = near_bias(b0)
            return step_fns(k_fn(b0, NSUB), [(vt_fn(b0, NSUB), 0)],
                            lambda h: jnp.concatenate([b(h) for b in biases], axis=0)) + (None,)

        def sweep(gap_ref):
            init()
            _softmax_step(*first_fns(), m_ref, acc_ref)
            _softmax_loop(0, nfar, far_step, m_ref, acc_ref, gap_ref)
            _softmax_loop(0, nnear, near_step, m_ref, acc_ref, gap_ref)

        gap_ref[...] = jnp.full(gap_ref.shape, NEG_INF, f32)
        sweep(gap_ref)
        pl.when(_overshot(gap_ref))(lambda: sweep(None))
        finalize()

    @pl.when(s_id == nb_real)
    def _():
        init()
        _softmax_step(*step_fns(lambda h: kmf_ref[:, h * hw:(h + 1) * hw],
                                [(lambda h: vtmf_ref[0, h * A_VR:(h + 1) * A_VR, :], 0)],
                                lambda h: bias_rows(KIND_METAMETA, h, LANES)), m_ref, acc_ref)
        finalize()


def _attn_a(qs, k, vt, bias, lam, subln, *, bsz, nblk, lambda_init):
    n = k.shape[0]
    nb_real = bsz * nblk
    seq = nblk * LANES
    d = A_QD
    vr = A_HEADS * A_VR
    assert nblk % NSUB == 0
    km, vtm = _meta_views(k[nb_real * LANES:], vt[nb_real], bsz)
    kern = functools.partial(_attn_a_kernel, nblk=nblk, lambda_init=lambda_init)
    bclamp = lambda s: jnp.minimum(s // nblk, bsz - 1)
    return pl.pallas_call(
        kern,
        grid=(nb_real + 1,),
        in_specs=[
            pl.BlockSpec((1,) + qs.shape[1:], lambda s: (s, 0, 0, 0)),
            pl.BlockSpec((seq, d), lambda s: (bclamp(s), 0)),
            pl.BlockSpec((nblk, vr, LANES), lambda s: (bclamp(s), 0, 0)),
            pl.BlockSpec((1, N_META, d), lambda s: (bclamp(s), 0, 0)),
            pl.BlockSpec((1, vr, N_META), lambda s: (bclamp(s), 0, 0)),
            pl.BlockSpec((LANES, d), lambda s: (nb_real, 0)),
            pl.BlockSpec((1, vr, LANES), lambda s: (nb_real, 0, 0)),
            _bias_spec(),
            pl.BlockSpec((4, A_HD), lambda s: (0, 0)),
            pl.BlockSpec((A_VD, LANES), lambda s: (0, 0)),
        ],
        out_specs=pl.BlockSpec((LANES, d), lambda s: (s, 0)),
        out_shape=jax.ShapeDtypeStruct((n, d), bf16),
        scratch_shapes=[
            pltpu.VMEM((1, A_HEADS * 2 * LANES), f32),
            pltpu.VMEM((A_VR, A_HEADS * 2 * LANES), f32),
            pltpu.VMEM((1, A_HEADS * 2 * LANES), f32),
        ],
        compiler_params=_cparams(("arbitrary",)),
        name="attn_a",
    )(qs, k, vt, km, vtm, k, vt, bias, lam, jnp.broadcast_to(subln[:, None], (A_VD, LANES)))


C_QD = C_Q_HEADS * C_HD
C_KD = 2 * C_KV_HEADS * C_HD
C_VR = 2 * C_HD + ONES_ROWS


def _proj_c_kernel(h_ref, g_ref, w_ref, gq_ref, gk_ref, o_ref, vt_ref):
    xn = (_rms_rows(h_ref[...]) * g_ref[...]).astype(bf16)
    y = _dot(xn, w_ref[...])
    q = _group_rms(y[:, :C_QD], C_HD) * gq_ref[...] * (C_HD ** -0.5 * LOG2E)
    lo = _lo_half_mask(q.shape, 2 * C_HD, C_HD)
    o_ref[:, :C_QD] = jnp.where(lo, q, 0.0).astype(bf16)
    o_ref[:, C_QD:2 * C_QD] = jnp.where(lo, 0.0, q).astype(bf16)
    o_ref[:, 2 * C_QD:] = (_group_rms(y[:, C_QD:C_QD + C_KD], C_HD) * gk_ref[...]).astype(bf16)
    ones = jnp.ones((ONES_ROWS, LANES), bf16)
    for t in range(vt_ref.shape[0]):
        vt = y[t * LANES:(t + 1) * LANES, C_QD + C_KD:].T.astype(bf16)
        for g in range(C_KV_HEADS):
            vt_ref[t, g * C_VR:g * C_VR + 2 * C_HD, :] = vt[g * 2 * C_HD:(g + 1) * 2 * C_HD]
            vt_ref[t, g * C_VR + 2 * C_HD:(g + 1) * C_VR, :] = ones


def _proj_c(h, g, w, qk_norm, *, tm):
    n, d = h.shape
    nw = w.shape[1]
    nout = 2 * C_QD + C_KD
    gq = jnp.tile(qk_norm[0], C_QD // C_HD).reshape(1, C_QD)
    gk = jnp.tile(qk_norm[1], C_KD // C_HD).reshape(1, C_KD)
    return pl.pallas_call(
        _proj_c_kernel,
        grid=(n // tm,),
        in_specs=[
            pl.BlockSpec((tm, d), lambda i: (i, 0)),
            pl.BlockSpec((1, d), lambda i: (0, 0)),
            pl.BlockSpec((d, nw), lambda i: (0, 0)),
            pl.BlockSpec((1, C_QD), lambda i: (0, 0)),
            pl.BlockSpec((1, C_KD), lambda i: (0, 0)),
        ],
        out_specs=[
            pl.BlockSpec((tm, nout), lambda i: (i, 0)),
            pl.BlockSpec((tm // LANES, C_KV_HEADS * C_VR, LANES), lambda i: (i, 0, 0)),
        ],
        out_shape=[
            jax.ShapeDtypeStruct((n, nout), bf16),
            jax.ShapeDtypeStruct((n // LANES, C_KV_HEADS * C_VR, LANES), bf16),
        ],
        compiler_params=_cparams(("arbitrary",)),
        name="proj_c",
    )(h, g.reshape(1, d), w, gq, gk)


def _attn_c_kernel(sink_ref, qlo_ref, qhi_ref, k_ref, vt_ref, km_ref, vtm_ref, kmf_ref, vtmf_ref, bias_ref,
                   o_ref, qs_ref, *, nblk):
    s_id = pl.program_id(0)
    nb_real = pl.num_programs(0) - 1
    r_io, _ = _tile_iotas()
    vd = 2 * C_HD

    def attend(tiles):
        top = r_io < C_HD
        for g in range(C_KV_HEADS):
            for hh in range(C_GROUP):
                cc = (g * C_GROUP + hh) // 2
                src = qlo_ref if hh % 2 == 0 else qhi_ref
                qs_ref[hh * LANES:(hh + 1) * LANES, :] = src[:, cc * LANES:(cc + 1) * LANES]
            sink = jnp.concatenate(
                [jnp.full((1, LANES), sink_ref[g * C_GROUP + hh] * LOG2E, f32) for hh in range(C_GROUP)], axis=1)
            sts = []
            m = sink
            for (k_fn, vt_fn, bias_fn) in tiles:
                st = _dot_nt(k_fn(g), qs_ref[...])
                st = st + jnp.concatenate([bias_fn(g * C_GROUP + hh) for hh in range(C_GROUP)], axis=1)
                m = jnp.maximum(m, jnp.max(st, axis=0, keepdims=True))
                sts.append(st)
            acc = None
            for st, (k_fn, vt_fn, bias_fn) in zip(sts, tiles):
                pv = _dot(vt_fn(g), jnp.exp2(st - m).astype(bf16))
                acc = pv if acc is None else acc + pv
            o = acc[0:vd, :] / (acc[vd:vd + 1, :] + jnp.exp2(sink - m))
            for cc in range(C_GROUP // 2):
                even = o[:, (2 * cc) * LANES:(2 * cc + 1) * LANES]
                odd = o[:, (2 * cc + 1) * LANES:(2 * cc + 2) * LANES]
                col = (g * (C_GROUP // 2) + cc) * LANES
                o_ref[:, col:col + LANES] = jnp.where(top, even, odd).T.astype(bf16)

    @pl.when(s_id < nb_real)
    def _():
        i = s_id % nblk
        prev = jnp.maximum(i - 1, 0)
        poff = pl.multiple_of(prev * LANES, LANES)
        coff = pl.multiple_of(i * LANES, LANES)
        kind_m = jnp.where(i == 0, KIND_META0, KIND_FAR)
        kind_p = jnp.where(i == 0, KIND_MASKED, KIND_PREVWIN)
        attend([
            (lambda g: km_ref[0, :, g * LANES:(g + 1) * LANES], lambda g: vtm_ref[0, g * C_VR:(g + 1) * C_VR, :],
             lambda h: bias_ref[kind_m, h, 0:N_META, :]),
            (lambda g: k_ref[pl.ds(poff, LANES), g * LANES:(g + 1) * LANES],
             lambda g: vt_ref[prev, g * C_VR:(g + 1) * C_VR, :], lambda h: bias_ref[kind_p, h]),
            (lambda g: k_ref[pl.ds(coff, LANES), g * LANES:(g + 1) * LANES],
             lambda g: vt_ref[i, g * C_VR:(g + 1) * C_VR, :], lambda h: bias_ref[KIND_DIAG, h]),
        ])

    @pl.when(s_id == nb_real)
    def _():
        attend([(lambda g: kmf_ref[:, g * LANES:(g + 1) * LANES], lambda g: vtmf_ref[0, g * C_VR:(g + 1) * C_VR, :],
                 lambda h: bias_ref[KIND_METAMETA, h])])


def _attn_c(qkv, vt, bias, sinks, *, bsz, nblk):
    n = qkv.shape[0]
    nb_real = bsz * nblk
    seq = nblk * LANES
    vr = C_KV_HEADS * C_VR
    kern = functools.partial(_attn_c_kernel, nblk=nblk)
    bclamp = lambda s: jnp.minimum(s // nblk, bsz - 1)
    kcol = 2 * C_QD // C_KD
    km, vtm = _meta_views(qkv[nb_real * LANES:, 2 * C_QD:], vt[nb_real], bsz)
    return pl.pallas_call(
        kern,
        grid=(nb_real + 1,),
        in_specs=[
            pl.BlockSpec(memory_space=pltpu.SMEM),
            pl.BlockSpec((LANES, C_QD), lambda s: (s, 0)),
            pl.BlockSpec((LANES, C_QD), lambda s: (s, 1)),
            pl.BlockSpec((seq, C_KD), lambda s: (bclamp(s), kcol)),
            pl.BlockSpec((nblk, vr, LANES), lambda s: (bclamp(s), 0, 0)),
            pl.BlockSpec((1, N_META, C_KD), lambda s: (bclamp(s), 0, 0)),
            pl.BlockSpec((1, vr, N_META), lambda s: (bclamp(s), 0, 0)),
            pl.BlockSpec((LANES, C_KD), lambda s: (nb_real, kcol)),
            pl.BlockSpec((1, vr, LANES), lambda s: (nb_real, 0, 0)),
            _bias_spec(),
        ],
        out_specs=pl.BlockSpec((LANES, C_QD), lambda s: (s, 0)),
        out_shape=jax.ShapeDtypeStruct((n, C_QD), bf16),
        scratch_shapes=[pltpu.VMEM((C_GROUP * LANES, LANES), bf16)],
        compiler_params=_cparams(("arbitrary",)),
        name="attn_c",
    )(sinks, qkv, qkv, qkv, vt, km, vtm, qkv, vt, bias)


B_QA = B_HEADS * B_KV_RANK
B_QI = IDX_HEADS * IDX_DIM
B_W1 = 2 * B_Q_RANK + 2 * LANES
B_TR = B_KV_RANK + ONES_ROWS


def _proj_b_kernel(h_ref, g_ref, w1_ref, ln_ref, wuq_ref, qn_ref,
                   qa_ref, qi_ref, ckv_ref, ckvt_ref, kk_ref, wit_ref):
    xn = (_rms_rows(h_ref[...]) * g_ref[...]).astype(bf16)
    y = _dot(xn, w1_ref[...])
    r = B_Q_RANK
    cq = (_rms_rows(y[:, :r]) * ln_ref[0:1, :]).astype(bf16)
    ckv = _rms_rows(y[:, r:2 * r]) * ln_ref[1:2, :]
    ckv_ref[...] = ckv.astype(bf16)
    kk_ref[...] = _rms_rows(y[:, 2 * r:2 * r + LANES]).astype(bf16)
    wi = y[:, 2 * r + LANES:] * (IDX_HEADS ** -0.5)
    ones = jnp.ones((ONES_ROWS, LANES), bf16)
    for t in range(ckvt_ref.shape[0]):
        ckvt_ref[t, 0:r, :] = ckv[t * LANES:(t + 1) * LANES, :].T.astype(bf16)
        ckvt_ref[t, r:, :] = ones
        wit_ref[t] = wi[t * LANES:(t + 1) * LANES, :].T[0:IDX_HEADS, :]
    z = _dot(cq, wuq_ref[...])
    qa = (_group_rms(z[:, :B_QA], B_KV_RANK) * qn_ref[...] * (B_KV_RANK ** -0.5 * LOG2E)).astype(bf16)
    qi = z[:, B_QA:] * (IDX_DIM ** -0.5)
    lo = _lo_half_mask(qi.shape, 2 * IDX_DIM, IDX_DIM)
    qi_lo = jnp.where(lo, qi, 0.0).astype(bf16)
    qi_hi = jnp.where(lo, 0.0, qi).astype(bf16)
    for t in range(qa_ref.shape[0]):
        rows = slice(t * LANES, (t + 1) * LANES)
        for hd in range(B_HEADS):
            qa_ref[t, hd * LANES:(hd + 1) * LANES, :] = qa[rows, hd * r:(hd + 1) * r]
        for hh in range(IDX_HEADS):
            src = qi_lo if hh % 2 == 0 else qi_hi
            qi_ref[t, hh * LANES:(hh + 1) * LANES, :] = src[rows, (hh // 2) * LANES:(hh // 2 + 1) * LANES]


def _proj_b(h, g, w1, latent_norm, wuq, q_norm, *, tm):
    n, d = h.shape
    qn = jnp.tile(q_norm, B_HEADS).reshape(1, B_QA)
    row = lambda i: (i, 0)
    row3 = lambda i: (i, 0, 0)
    const = lambda i: (0, 0)
    nt = tm // LANES
    return pl.pallas_call(
        _proj_b_kernel,
        grid=(n // tm,),
        in_specs=[
            pl.BlockSpec((tm, d), row),
            pl.BlockSpec((1, d), const),
            pl.BlockSpec(w1.shape, const),
            pl.BlockSpec(latent_norm.shape, const),
            pl.BlockSpec(wuq.shape, const),
            pl.BlockSpec((1, B_QA), const),
        ],
        out_specs=[
            pl.BlockSpec((nt, B_HEADS * LANES, B_KV_RANK), row3),
            pl.BlockSpec((nt, IDX_HEADS * LANES, LANES), row3),
            pl.BlockSpec((tm, B_KV_RANK), row),
            pl.BlockSpec((nt, B_TR, LANES), row3),
            pl.BlockSpec((tm, LANES), row),
            pl.BlockSpec((nt, IDX_HEADS, LANES), row3),
        ],
        out_shape=[
            jax.ShapeDtypeStruct((n // LANES, B_HEADS * LANES, B_KV_RANK), bf16),
            jax.ShapeDtypeStruct((n // LANES, IDX_HEADS * LANES, LANES), bf16),
            jax.ShapeDtypeStruct((n, B_KV_RANK), bf16),
            jax.ShapeDtypeStruct((n // LANES, B_TR, LANES), bf16),
            jax.ShapeDtypeStruct((n, LANES), bf16),
            jax.ShapeDtypeStruct((n // LANES, IDX_HEADS, LANES), f32),
        ],
        compiler_params=_cparams(("arbitrary",)),
        name="proj_b",
    )(h, g.reshape(1, d), w1, latent_norm, wuq, qn)


def _attn_b_kernel(qs_ref, is_ref, wit_ref, ckv_ref, ckvt_ref, kk_ref, ckvm_ref, ckvtm_ref, kkm_ref,
                   ckvmf_ref, ckvtmf_ref, bias_ref, wuvt_ref,
                   o_ref, key_ref, pen_ref, m_ref, acc_ref, gap_ref, *, nblk, k_sel):
    s_id = pl.program_id(0)
    nb_real = pl.num_programs(0) - 1
    r_io, c_io = _tile_iotas()
    rk = B_KV_RANK

    def init():
        m_ref[...] = jnp.full(m_ref.shape, NEG_INF, f32)
        acc_ref[...] = jnp.zeros(acc_ref.shape, f32)

    def add_per_head(st, bias_fn, pen):
        cols = []
        for h in range(B_HEADS):
            add = pen if bias_fn is None else (bias_fn(h) if pen is None else bias_fn(h) + pen)
            cols.append(st[:, h * LANES:(h + 1) * LANES] + add)
        return jnp.concatenate(cols, axis=1)

    def step_fns(ckv_fn, ckvt_fns, bias_fn, pen_fn):
        def logits():
            st = _dot_nt(ckv_fn(), qs_ref[0])
            return add_per_head(st, bias_fn, None if pen_fn is None else pen_fn())

        def pv(pb):
            acc = None
            for ckvt_fn, r0 in ckvt_fns:
                ckvt = ckvt_fn()
                part = _dot(ckvt, pb[r0:r0 + ckvt.shape[1], :])
                acc = part if acc is None else acc + part
            return acc

        return logits, pv

    def finalize():
        olat = (acc_ref[0:rk, :] / acc_ref[rk:rk + 1, :]).astype(bf16)
        ot = jnp.concatenate([_dot(wuvt_ref[h], olat[:, h * LANES:(h + 1) * LANES]) for h in range(B_HEADS)], axis=0)
        o_ref[...] = ot.T.astype(bf16)

    def index_scores(kk):
        s = jnp.maximum(_dot_nt(kk, is_ref[0]), 0.0)
        wt = wit_ref[0]
        sc = jnp.zeros((kk.shape[0], LANES), f32)
        for hh in range(IDX_HEADS):
            sc = sc + wt[hh:hh + 1, :] * s[:, hh * LANES:(hh + 1) * LANES]
        return sc

    def sort_key(sc):
        bits = lax.bitcast_convert_type(sc + 0.0, jnp.int32)
        return jnp.where(bits < 0, bits ^ jnp.int32(0x7FFFFFFF), bits)

    @pl.when(s_id < nb_real)
    def _():
        i = s_id % nblk
        ntile = i + 2

        int_min_tile = jnp.full((LANES, LANES), INT_MIN, jnp.int32)
        key_ref[0] = int_min_tile
        key_ref[0, 0:N_META, :] = sort_key(index_scores(kkm_ref[0]))
        key_ref[i + 2] = int_min_tile

        def score_body(jp, carry):
            off = pl.multiple_of(jp * (2 * LANES), 2 * LANES)
            keys = sort_key(index_scores(kk_ref[pl.ds(off, 2 * LANES), :]))
            for t in range(2):
                j = _vec(2 * jp + t)
                vis = (j < i) | ((j == i) & ((r_io >> 6) <= (c_io >> 6)))
                key_ref[2 * jp + t + 1] = jnp.where(vis, keys[t * LANES:(t + 1) * LANES], jnp.int32(INT_MIN))
            return carry

        lax.fori_loop(0, i // 2 + 1, score_body, 0)

        def count(pred):
            def cbody(tp, accv):
                for t in (2 * tp, 2 * tp + 1):
                    accv = accv + jnp.where(pred(key_ref[t], t), 1.0, 0.0)
                return accv
            accv = lax.fori_loop(0, (ntile + 1) // 2, cbody, jnp.zeros((LANES, LANES), f32))
            return jnp.sum(accv, axis=0, keepdims=True)

        kf = float(k_sel)
        zero = jnp.zeros((1, LANES), jnp.int32)
        t0 = jnp.where(count(lambda k, t: k >= zero) >= kf, zero, jnp.int32(INT_MIN))

        def bit_body(it, tcur):
            cand = tcur | jnp.left_shift(jnp.int32(1), 30 - it)
            return jnp.where(count(lambda k, t: k >= cand) >= kf, cand, tcur)

        thr = lax.fori_loop(0, 31, bit_body, t0)

        need = kf - count(lambda k, t: k > thr)
        n_eq = count(lambda k, t: k == thr)
        has_thr = thr > jnp.int32(INT_MIN)
        tied = jnp.max(jnp.where(has_thr & (n_eq > need), 1.0, 0.0)) > 0.0

        def tie_search(_):
            def jbody(it, jcur):
                cand = jcur | jnp.left_shift(jnp.int32(1), 11 - it)
                cnt = count(lambda k, t: (k == thr) & ((t * LANES + r_io) < cand))
                return jnp.where(cnt < need, cand, jcur)
            return lax.fori_loop(0, 12, jbody, jnp.zeros((1, LANES), jnp.int32))

        j_last = lax.cond(tied, tie_search, lambda _: jnp.full((1, LANES), 4095, jnp.int32), 0)
        j_last = jnp.where(has_thr, j_last, -1)

        def pen_body(t, carry):
            k = key_ref[t]
            sel = (k > thr) | ((k == thr) & ((t * LANES + r_io) <= j_last))
            pen_ref[t] = jnp.where(sel, 0.0, NEG_INF)
            return carry

        lax.fori_loop(0, ntile, pen_body, 0)
        for t in range(1, NSUB):
            pen_ref[i + 1 + t] = jnp.full((LANES, LANES), NEG_INF, f32)

        kind_m = jnp.where(i == 0, KIND_META0, KIND_FAR)
        nfar, near0, nnear = _sweep_steps(i)

        def ckv_fn(b0, nsub):
            off = pl.multiple_of(b0 * LANES, NSUB * LANES)
            return lambda: ckv_ref[pl.ds(off, nsub * LANES), :]

        def ckvt_fn(b0, nsub):
            return lambda: jnp.concatenate([ckvt_ref[b0 + t] for t in range(nsub)], axis=1)

        def pen_fn(b0, nsub):
            return lambda: jnp.concatenate([pen_ref[b0 + t + 1] for t in range(nsub)], axis=0)

        def near_bias(b0):
            kinds = [_block_kind(b0 + t - i) for t in range(NSUB)]
            return lambda h: jnp.concatenate([bias_ref[kinds[t], h] for t in range(NSUB)], axis=0)

        def first_pen():
            return jnp.concatenate([pen_ref[0, 0:N_META, :], pen_fn(0, NSUB)()], axis=0)

        def first_fns(with_pen):
            bias01 = near_bias(0)
            return step_fns(lambda: jnp.concatenate([ckvm_ref[0], ckv_fn(0, NSUB)()], axis=0),
                            [(lambda: ckvtm_ref[0], 0), (ckvt_fn(0, NSUB), N_META)],
                            lambda h: jnp.concatenate([bias_ref[kind_m, h, 0:N_META, :], bias01(h)], axis=0),
                            first_pen if with_pen else None)

        def far_step(w):
            b0 = NSUB + w * NSUB_FAR
            far_bias = jnp.concatenate([bias_ref[KIND_FAR, h, 0:1, :] for h in range(B_HEADS)], axis=1)
            return step_fns(ckv_fn(b0, NSUB_FAR), [(ckvt_fn(b0, NSUB_FAR), 0)], None, pen_fn(b0, NSUB_FAR)) + (far_bias,)

        def near_step(u):
            b0 = near0 + u * NSUB
            return step_fns(ckv_fn(b0, NSUB), [(ckvt_fn(b0, NSUB), 0)], near_bias(b0), pen_fn(b0, NSUB)) + (None,)

        def sweep(gap_ref):
            init()
            if gap_ref is None:
                _softmax_step(*first_fns(True), m_ref, acc_ref)
            else:
                _softmax_step(*first_fns(False), m_ref, acc_ref,
                              post_fn=lambda st: add_per_head(st, None, first_pen()))
            _softmax_loop(0, nfar, far_step, m_ref, acc_ref, gap_ref)
            _softmax_loop(0, nnear, near_step, m_ref, acc_ref, gap_ref)

        gap_ref[...] = jnp.full(gap_ref.shape, NEG_INF, f32)
        sweep(gap_ref)
        faded = jnp.logical_not(jnp.min(acc_ref[rk:rk + 1, :]) >= LAZY_FLOOR)
        pl.when(_overshot(gap_ref) | faded)(lambda: sweep(None))
        finalize()

    @pl.when(s_id == nb_real)
    def _():
        init()
        _softmax_step(*step_fns(lambda: ckvmf_ref[...], [(lambda: ckvtmf_ref[0], 0)],
                                lambda h: bias_ref[KIND_METAMETA, h], None), m_ref, acc_ref)
        finalize()


def _attn_b(qa, qi, wit, ckv, ckvt, kk, bias, wuvt, *, bsz, nblk, k_sel):
    n = ckv.shape[0]
    nb_real = bsz * nblk
    seq = nblk * LANES
    assert k_sel >= N_META and (nblk + 1) * LANES <= 4096 and nblk % NSUB == 0
    ckvm, ckvtm = _meta_views(ckv[nb_real * LANES:], ckvt[nb_real], bsz)
    kkm = kk[nb_real * LANES:].reshape(bsz, N_META, LANES)
    kern = functools.partial(_attn_b_kernel, nblk=nblk, k_sel=k_sel)
    bidx = lambda s: jnp.minimum(s // nblk, bsz - 1)
    blk = lambda s: (s, 0)
    return pl.pallas_call(
        kern,
        grid=(nb_real + 1,),
        in_specs=[
            pl.BlockSpec((1,) + qa.shape[1:], lambda s: (s, 0, 0)),
            pl.BlockSpec((1,) + qi.shape[1:], lambda s: (s, 0, 0)),
            pl.BlockSpec((1, IDX_HEADS, LANES), lambda s: (s, 0, 0)),
            pl.BlockSpec((seq, B_KV_RANK), lambda s: (bidx(s), 0)),
            pl.BlockSpec((nblk, B_TR, LANES), lambda s: (bidx(s), 0, 0)),
            pl.BlockSpec((seq, LANES), lambda s: (bidx(s), 0)),
            pl.BlockSpec((1, N_META, B_KV_RANK), lambda s: (bidx(s), 0, 0)),
            pl.BlockSpec((1, B_TR, N_META), lambda s: (bidx(s), 0, 0)),
            pl.BlockSpec((1, N_META, LANES), lambda s: (bidx(s), 0, 0)),
            pl.BlockSpec((LANES, B_KV_RANK), lambda s: (nb_real, 0)),
            pl.BlockSpec((1, B_TR, LANES), lambda s: (nb_real, 0, 0)),
            _bias_spec(),
            pl.BlockSpec(wuvt.shape, lambda s: (0, 0, 0)),
        ],
        out_specs=pl.BlockSpec((LANES, B_HEADS * B_VD), blk),
        out_shape=jax.ShapeDtypeStruct((n, B_HEADS * B_VD), bf16),
        scratch_shapes=[
            pltpu.VMEM((nblk + 2, LANES, LANES), jnp.int32),
            pltpu.VMEM((nblk + NSUB, LANES, LANES), f32),
            pltpu.VMEM((1, B_HEADS * LANES), f32),
            pltpu.VMEM((B_TR, B_HEADS * LANES), f32),
            pltpu.VMEM((1, B_HEADS * LANES), f32),
        ],
        compiler_params=_cparams(("arbitrary",)),
        name="attn_b",
    )(qa, qi, wit, ckv, ckvt, kk, ckvm, ckvtm, kkm, ckv, ckvt, bias, wuvt)


def kernel(x, meta_tokens, rel_bias, ln_ffn1, ffn1_wi, ffn1_wo, ln_mix, w_out, ln_ffn2, ffn2_wi, ffn2_wo, a_w_in, a_qk_norm, a_lambda, a_subln, b_w_in, b_latent_norm, b_w_uq, b_q_norm, b_w_uv, c_w_in, c_qk_norm, c_sinks):
    bsz, seq, d = x.shape
    assert d == D_MODEL and seq % LANES == 0 and bsz * N_META == LANES
    nblk = seq // LANES
    n = bsz * seq + LANES
    k_sel = min(TOPK_MAX, seq // 4)
    tm_ffn = _row_tile(n, 1408)
    tm_proj = _row_tile(n, 384, LANES)
    fc = 256

    h = jnp.concatenate([x.reshape(bsz * seq, d),
                         jnp.broadcast_to(meta_tokens.astype(x.dtype), (bsz, N_META, d)).reshape(LANES, d)], axis=0)
    bias = _bias_tiles(rel_bias)

    for layer in range(DEPTH):
        h = _ffn(h, ln_ffn1[layer], ffn1_wi, ffn1_wo, layer, tm=tm_ffn, fc=fc)
        kind, j = layer % N_MIXERS, layer // N_MIXERS
        g = ln_mix[layer]
        if kind == 0:
            lambda_init = 0.8 - 0.6 * math.exp(-0.3 * layer)
            qs, k, vt = _proj_a(h, g, a_w_in[j].astype(bf16), a_qk_norm[j], tm=tm_proj)
            mix = _attn_a(qs, k, vt, bias, a_lambda[j], a_subln[j], bsz=bsz, nblk=nblk, lambda_init=lambda_init)
        elif kind == 1:
            w = b_w_in[j]
            r2 = B_Q_RANK + B_KV_RANK
            kcol = w[:, r2:r2 + IDX_DIM]
            w1 = jnp.concatenate([w[:, :r2], kcol, kcol, w[:, r2 + IDX_DIM:],
                                  jnp.zeros((d, LANES - IDX_HEADS), w.dtype)], axis=1).astype(bf16)
            assert w1.shape[1] == B_W1
            qa, qi, ckv, ckvt, kk, wit = _proj_b(h, g, w1, b_latent_norm[j], b_w_uq[j].astype(bf16), b_q_norm[j],
                                                 tm=tm_proj)
            wuvt = jnp.swapaxes(b_w_uv[j], 1, 2).astype(bf16)
            mix = _attn_b(qa, qi, wit, ckv, ckvt, kk, bias, wuvt, bsz=bsz, nblk=nblk, k_sel=k_sel)
        else:
            w = c_w_in[j]
            kcols = [w[:, C_QD + gi * C_HD:C_QD + (gi + 1) * C_HD] for gi in range(C_KV_HEADS)]
            voff = C_QD + C_KV_HEADS * C_HD
            vcols = [w[:, voff + gi * C_HD:voff + (gi + 1) * C_HD] for gi in range(C_KV_HEADS)]
            wc = jnp.concatenate([w[:, :C_QD]] + [kc for kc in kcols for _ in range(2)]
                                 + [vc for vc in vcols for _ in range(2)], axis=1).astype(bf16)
            qkv, vt = _proj_c(h, g, wc, c_qk_norm[j], tm=tm_proj)
            mix = _attn_c(qkv, vt, bias, c_sinks[j], bsz=bsz, nblk=nblk)
        h = _ffn(h, ln_ffn2[layer], ffn2_wi, ffn2_wo, layer, tm=tm_ffn, fc=fc,
                 mix=mix, wout=w_out[layer].astype(bf16))
    return h[:bsz * seq].reshape(bsz, seq, d)
```

```python
import functools
import math

import numpy as np
import jax
import jax.numpy as jnp
from jax import lax
from jax.experimental import pallas as pl
from jax.experimental.pallas import tpu as pltpu

D_MODEL = 1024
DEPTH = 4
CHUNK = 64
N_META = 16
N_MIXERS = 3
NEG_INF = -1e30
REL_BUCKETS = 32
REL_MAX_DIST = 128
REL_HEADS = 16
D_FF = 2816
A_HEADS = 8
A_HD = 64
A_VD = 2 * A_HD
B_HEADS = 16
B_Q_RANK = 256
B_KV_RANK = 256
B_VD = 64
IDX_HEADS = 8
IDX_DIM = 64
TOPK_MAX = 256
C_Q_HEADS = 16
C_KV_HEADS = 2
C_GROUP = C_Q_HEADS // C_KV_HEADS
C_HD = 64
EPS = 1e-6

LANES = 128
BF16_ROWS = 16
VMEM_LIMIT = 56 * 1024 * 1024
INT_MIN = -(2 ** 31)
NSUB = 2
NSUB_FAR = 4
LOG2E = math.log2(math.e)
LAZY_GAP = 57.0
LAZY_FLOOR = 2.0 ** -100
ONES_ROWS = BF16_ROWS

KIND_DIAG, KIND_PREV, KIND_FAR, KIND_META0, KIND_METAMETA, KIND_MASKED, KIND_PREVWIN = 0, 1, 2, 3, 4, 5, 6
N_KINDS = 7

f32 = jnp.float32
bf16 = jnp.bfloat16


def _cparams(sem):
    return pltpu.CompilerParams(dimension_semantics=sem, vmem_limit_bytes=VMEM_LIMIT)


def _row_tile(n, cap, mult=BF16_ROWS):
    best = None
    for t in range(mult, cap + 1, mult):
        if n % t == 0:
            best = t
    assert best is not None
    return best


def _dot(a, b):
    return jnp.dot(a, b, preferred_element_type=f32)


def _dot_nt(a, b):
    return lax.dot_general(a, b, (((1,), (1,)), ((), ())), preferred_element_type=f32)


def _rms_rows(x):
    return x * lax.rsqrt(jnp.mean(x * x, axis=-1, keepdims=True) + EPS)


def _lo_half_mask(shape, period, half):
    return (lax.broadcasted_iota(jnp.int32, shape, 1) & (period - 1)) < half


def _group_rms(x, group):
    r, c = x.shape
    outs = []
    if group == 64:
        lo = _lo_half_mask((r, LANES), LANES, 64)
        for ci in range(c // LANES):
            xc = x[:, ci * LANES:(ci + 1) * LANES]
            x2 = xc * xc
            s_lo = jnp.sum(jnp.where(lo, x2, 0.0), axis=-1, keepdims=True)
            s_hi = jnp.sum(jnp.where(lo, 0.0, x2), axis=-1, keepdims=True)
            inv = jnp.where(lo, lax.rsqrt(s_lo * (1.0 / 64) + EPS), lax.rsqrt(s_hi * (1.0 / 64) + EPS))
            outs.append(xc * inv)
    else:
        for gi in range(c // group):
            outs.append(_rms_rows(x[:, gi * group:(gi + 1) * group]))
    return outs[0] if len(outs) == 1 else jnp.concatenate(outs, axis=-1)


def _tile_iotas():
    r = lax.broadcasted_iota(jnp.int32, (LANES, LANES), 0)
    c = lax.broadcasted_iota(jnp.int32, (LANES, LANES), 1)
    return r, c


def _vec(s):
    return jnp.full((LANES, LANES), s, jnp.int32)


def _softmax_step(logits_fn, pv_fn, m_ref, acc_ref, gap_ref=None, offset=None, post_fn=None):
    st = logits_fn()
    if gap_ref is None:
        if offset is not None:
            st = st + offset
        m_old = m_ref[...]
        m_new = jnp.maximum(m_old, jnp.max(st, axis=0, keepdims=True))
        if post_fn is not None:
            st = post_fn(st)
        acc_ref[...] = jnp.exp2(m_old - m_new) * acc_ref[...] + pv_fn(jnp.exp2(st - m_new).astype(bf16))
        m_ref[...] = m_new
    else:
        shift = m_ref[...] if offset is None else m_ref[...] - offset
        gap_ref[...] = jnp.maximum(gap_ref[...], jnp.max(st, axis=0, keepdims=True) - shift)
        acc_ref[...] += pv_fn(jnp.exp2(st - shift).astype(bf16))


def _softmax_loop(lo, hi, step_fn, m_ref, acc_ref, gap_ref):
    def body(w, carry):
        logits_fn, pv_fn, offset = step_fn(w)
        _softmax_step(logits_fn, pv_fn, m_ref, acc_ref, gap_ref, offset)
        return carry

    lax.fori_loop(lo, hi, body, 0)


def _overshot(gap_ref):
    return jnp.logical_not(jnp.max(gap_ref[...]) <= LAZY_GAP)


def _block_kind(rel):
    return jnp.where(rel < -1, KIND_FAR,
                     jnp.where(rel == -1, KIND_PREV, jnp.where(rel == 0, KIND_DIAG, KIND_MASKED)))


def _sweep_steps(i):
    nfar = jnp.maximum(i - 1 - NSUB, 0) // NSUB_FAR
    near0 = NSUB + nfar * NSUB_FAR
    return nfar, near0, (i - near0 + NSUB) // NSUB


def _ffn_kernel(*refs, fuse_out, layer, fc, nj):
    if fuse_out:
        h_ref, mix_ref, wout_ref, g_ref, wi_hbm, wo_hbm, o_ref, xn_ref, wa_buf, wb_buf, wo_buf, sem = refs
    else:
        h_ref, g_ref, wi_hbm, wo_hbm, o_ref, xn_ref, wa_buf, wb_buf, wo_buf, sem = refs

    def chunk_copies(j, slot):
        lo = pl.multiple_of(j * fc, fc)
        hi = pl.multiple_of((nj + j) * fc, fc)
        return (pltpu.make_async_copy(wi_hbm.at[layer, :, pl.ds(lo, fc)], wa_buf.at[slot], sem.at[0, slot]),
                pltpu.make_async_copy(wi_hbm.at[layer, :, pl.ds(hi, fc)], wb_buf.at[slot], sem.at[1, slot]),
                pltpu.make_async_copy(wo_hbm.at[layer, pl.ds(lo, fc), :], wo_buf.at[slot], sem.at[2, slot]))

    i = pl.program_id(0)
    first = i * nj

    @pl.when(i == 0)
    def _():
        for c in chunk_copies(0, 0):
            c.start()

    r = h_ref[...]
    if fuse_out:
        r = r + _dot(mix_ref[...], wout_ref[...])
    o_ref[...] = r
    xn_ref[...] = (_rms_rows(r) * g_ref[...]).astype(bf16)

    def body(j, carry):
        slot = (first + j) & 1

        @pl.when((j + 1 < nj) | (i + 1 < pl.num_programs(0)))
        def _():
            for c in chunk_copies(jnp.where(j + 1 < nj, j + 1, 0), 1 - slot):
                c.start()

        for c in chunk_copies(j, slot):
            c.wait()
        xn = xn_ref[...]
        a = _dot(xn, wa_buf[slot].astype(bf16))
        b = _dot(xn, wb_buf[slot].astype(bf16))
        act = (a / (1.0 + jnp.exp(-a)) * b).astype(bf16)
        o_ref[...] += 0.5 * _dot(act, wo_buf[slot].astype(bf16))
        return carry

    lax.fori_loop(0, nj, body, 0)


def _ffn(h, g, wi, wo, layer, *, tm, fc, mix=None, wout=None):
    n, d = h.shape
    dff = wo.shape[1]
    nj = dff // fc
    fuse = mix is not None
    row = lambda i: (i, 0)
    in_specs = [pl.BlockSpec((tm, d), row)]
    args = [h]
    if fuse:
        in_specs += [pl.BlockSpec((tm, mix.shape[1]), row), pl.BlockSpec(wout.shape, lambda i: (0, 0))]
        args += [mix, wout]
    in_specs += [
        pl.BlockSpec((1, d), lambda i: (0, 0)),
        pl.BlockSpec(memory_space=pl.ANY),
        pl.BlockSpec(memory_space=pl.ANY),
    ]
    args += [g.reshape(1, d), wi, wo]
    return pl.pallas_call(
        functools.partial(_ffn_kernel, fuse_out=fuse, layer=layer, fc=fc, nj=nj),
        grid=(n // tm,),
        in_specs=in_specs,
        out_specs=pl.BlockSpec((tm, d), row),
        out_shape=jax.ShapeDtypeStruct((n, d), f32),
        scratch_shapes=[
            pltpu.VMEM((tm, d), bf16),
            pltpu.VMEM((2, d, fc), f32),
            pltpu.VMEM((2, d, fc), f32),
            pltpu.VMEM((2, fc, d), f32),
            pltpu.SemaphoreType.DMA((3, 2)),
        ],
        compiler_params=_cparams(("arbitrary",)),
        name="ffn_out" if fuse else "ffn",
    )(*args)


def _rel_bucket(rel):
    half = REL_BUCKETS // 2
    max_exact = half // 2
    n = jnp.abs(rel)
    large = max_exact + (jnp.log(jnp.maximum(n, 1).astype(jnp.float32) / max_exact)
                         / math.log(REL_MAX_DIST / max_exact) * (half - max_exact)).astype(jnp.int32)
    large = jnp.minimum(large, half - 1)
    return jnp.where(rel > 0, half, 0) + jnp.where(n < max_exact, n, large)


def _rel_tiles():
    k = np.arange(LANES)[:, None]
    q = np.arange(LANES)[None, :]
    far = np.full((LANES, LANES), -4 * LANES)
    ones = np.ones((LANES, LANES), bool)
    rels = [k - q, k - q - LANES, far, (k % N_META) - N_META - q, (k % N_META) - (q % N_META), far, k - q - LANES]
    vis = [(k // CHUNK) <= (q // CHUNK), ones, ones, ones, (k // N_META) == (q // N_META), ~ones,
           (q < CHUNK) | (k >= CHUNK)]
    return (np.stack([np.broadcast_to(a, (LANES, LANES)) for a in rels]).astype(np.int32),
            np.stack([np.broadcast_to(a, (LANES, LANES)) for a in vis]).astype(np.int32))


def _bias_kernel(rb_ref, bucket_ref, vis_ref, o_ref):
    h = pl.program_id(0)
    for kind in range(N_KINDS):
        bk = bucket_ref[kind]
        acc = jnp.zeros((LANES, LANES), f32)
        for b in range(REL_BUCKETS):
            acc = jnp.where(bk == b, rb_ref[b, h], acc)
        o_ref[kind, 0] = jnp.where(vis_ref[kind] != 0, acc * LOG2E, NEG_INF)


def _bias_tiles(rel_bias):
    rel, vis = _rel_tiles()
    bucket = _rel_bucket(jnp.asarray(rel))
    nk = N_KINDS
    return pl.pallas_call(
        _bias_kernel,
        grid=(REL_HEADS,),
        in_specs=[
            pl.BlockSpec(memory_space=pltpu.SMEM),
            pl.BlockSpec((nk, LANES, LANES), lambda h: (0, 0, 0)),
            pl.BlockSpec((nk, LANES, LANES), lambda h: (0, 0, 0)),
        ],
        out_specs=pl.BlockSpec((nk, 1, LANES, LANES), lambda h: (0, h, 0, 0)),
        out_shape=jax.ShapeDtypeStruct((nk, REL_HEADS, LANES, LANES), f32),
        compiler_params=_cparams(("arbitrary",)),
        name="bias_tiles",
    )(rel_bias, bucket, jnp.asarray(vis))


def _bias_spec():
    return pl.BlockSpec((N_KINDS, REL_HEADS, LANES, LANES), lambda s: (0, 0, 0, 0))


def _meta_views(rows, cols_t, bsz):
    f = rows.shape[1]
    return (rows.reshape(bsz, N_META, f),
            cols_t.reshape(cols_t.shape[0], bsz, N_META).transpose(1, 0, 2))


A_QD = A_HEADS * 2 * A_HD
A_VR = A_VD + ONES_ROWS


def _proj_a_kernel(h_ref, g_ref, w_ref, gq_ref, gk_ref, qs_ref, k_ref, vt_ref):
    xn = (_rms_rows(h_ref[...]) * g_ref[...]).astype(bf16)
    y = _dot(xn, w_ref[...])
    q = _group_rms(y[:, :A_QD], A_HD) * gq_ref[...] * (A_HD ** -0.5 * LOG2E)
    lo = _lo_half_mask(q.shape, 2 * A_HD, A_HD)
    q_lo = jnp.where(lo, q, 0.0).astype(bf16)
    q_hi = jnp.where(lo, 0.0, q).astype(bf16)
    k_ref[...] = (_group_rms(y[:, A_QD:2 * A_QD], A_HD) * gk_ref[...]).astype(bf16)
    ones = jnp.ones((ONES_ROWS, LANES), bf16)
    for t in range(vt_ref.shape[0]):
        rows = slice(t * LANES, (t + 1) * LANES)
        vt = y[rows, 2 * A_QD:].T.astype(bf16)
        for h in range(A_HEADS):
            qs_ref[t, h, :LANES, :] = q_lo[rows, h * A_VD:(h + 1) * A_VD]
            qs_ref[t, h, LANES:, :] = q_hi[rows, h * A_VD:(h + 1) * A_VD]
            vt_ref[t, h * A_VR:h * A_VR + A_VD, :] = vt[h * A_VD:(h + 1) * A_VD]
            vt_ref[t, h * A_VR + A_VD:(h + 1) * A_VR, :] = ones


def _proj_a(h, g, w, qk_norm, *, tm):
    n, d = h.shape
    nw = w.shape[1]
    gq = jnp.tile(qk_norm[0], A_QD // A_HD).reshape(1, A_QD)
    gk = jnp.tile(qk_norm[1], A_QD // A_HD).reshape(1, A_QD)
    return pl.pallas_call(
        _proj_a_kernel,
        grid=(n // tm,),
        in_specs=[
            pl.BlockSpec((tm, d), lambda i: (i, 0)),
            pl.BlockSpec((1, d), lambda i: (0, 0)),
            pl.BlockSpec((d, nw), lambda i: (0, 0)),
            pl.BlockSpec((1, A_QD), lambda i: (0, 0)),
            pl.BlockSpec((1, A_QD), lambda i: (0, 0)),
        ],
        out_specs=[
            pl.BlockSpec((tm // LANES, A_HEADS, 2 * LANES, A_VD), lambda i: (i, 0, 0, 0)),
            pl.BlockSpec((tm, A_QD), lambda i: (i, 0)),
            pl.BlockSpec((tm // LANES, A_HEADS * A_VR, LANES), lambda i: (i, 0, 0)),
        ],
        out_shape=[
            jax.ShapeDtypeStruct((n // LANES, A_HEADS, 2 * LANES, A_VD), bf16),
            jax.ShapeDtypeStruct((n, A_QD), bf16),
            jax.ShapeDtypeStruct((n // LANES, A_HEADS * A_VR, LANES), bf16),
        ],
        compiler_params=_cparams(("arbitrary",)),
        name="proj_a",
    )(h, g.reshape(1, d), w, gq, gk)


def _attn_a_kernel(qs_ref, k_ref, vt_ref, km_ref, vtm_ref, kmf_ref, vtmf_ref, bias_ref, lam_ref, sub_ref,
                   o_ref, m_ref, acc_ref, gap_ref, *, nblk, lambda_init):
    s_id = pl.program_id(0)
    nb_real = pl.num_programs(0) - 1
    hw = 2 * A_HD

    def init():
        m_ref[...] = jnp.full(m_ref.shape, NEG_INF, f32)
        acc_ref[...] = jnp.zeros(acc_ref.shape, f32)

    def step_fns(kt_fn, vtt_fns, bias_fn):
        def logits():
            sts = [_dot_nt(kt_fn(h), qs_ref[0, h]) for h in range(A_HEADS)]
            if bias_fn is not None:
                sts = [st + bias_fn(h) for h, st in enumerate(sts)]
            return jnp.concatenate(sts, axis=1)

        def pv(pb):
            outs = []
            for h in range(A_HEADS):
                acc = None
                for vtt_fn, r0 in vtt_fns:
                    vtt = vtt_fn(h)
                    part = _dot(vtt, pb[r0:r0 + vtt.shape[1], h * 2 * LANES:(h + 1) * 2 * LANES])
                    acc = part if acc is None else acc + part
                outs.append(acc)
            return jnp.concatenate(outs, axis=1)

        return logits, pv

    def bias_rows(kind, h, rows):
        return jnp.concatenate([bias_ref[kind, h, 0:rows, :], bias_ref[kind, A_HEADS + h, 0:rows, :]], axis=1)

    def finalize():
        lam = lam_ref[...]
        lam_full = (jnp.exp(jnp.sum(lam[0:1] * lam[1:2], axis=-1, keepdims=True))
                    - jnp.exp(jnp.sum(lam[2:3] * lam[3:4], axis=-1, keepdims=True)) + lambda_init)
        o = acc_ref[0:A_VD, :] / acc_ref[A_VD:A_VD + 1, :]
        for h in range(A_HEADS):
            d = o[:, 2 * h * LANES:(2 * h + 1) * LANES] - lam_full * o[:, (2 * h + 1) * LANES:(2 * h + 2) * LANES]
            d = d * lax.rsqrt(jnp.mean(d * d, axis=0, keepdims=True) + EPS) * sub_ref[...] * (1.0 - lambda_init)
            o_ref[:, h * hw:(h + 1) * hw] = d.T.astype(bf16)

    @pl.when(s_id < nb_real)
    def _():
        i = s_id % nblk
        kind_m = jnp.where(i == 0, KIND_META0, KIND_FAR)
        nfar, near0, nnear = _sweep_steps(i)

        def k_fn(b0, nsub):
            off = pl.multiple_of(b0 * LANES, NSUB * LANES)
            return lambda h: k_ref[pl.ds(off, nsub * LANES), h * hw:(h + 1) * hw]

        def vt_fn(b0, nsub):
            return lambda h: jnp.concatenate([vt_ref[b0 + t, h * A_VR:(h + 1) * A_VR, :] for t in range(nsub)], axis=1)

        def near_bias(b0):
            kinds = [_block_kind(b0 + t - i) for t in range(NSUB)]
            return [lambda h, kind=kind: bias_rows(kind, h, LANES) for kind in kinds]

        def first_fns():
            biases = [lambda h: bias_rows(kind_m, h, N_META)] + near_bias(0)
            return step_fns(
                lambda h: jnp.concatenate([km_ref[0, :, h * hw:(h + 1) * hw], k_fn(0, NSUB)(h)], axis=0),
                [(lambda h: vtm_ref[0, h * A_VR:(h + 1) * A_VR, :], 0), (vt_fn(0, NSUB), N_META)],
                lambda h: jnp.concatenate([b(h) for b in biases], axis=0))

        def far_step(w):
            b0 = NSUB + w * NSUB_FAR
            far_bias = jnp.concatenate([bias_rows(KIND_FAR, h, 1) for h in range(A_HEADS)], axis=1)
            return step_fns(k_fn(b0, NSUB_FAR), [(vt_fn(b0, NSUB_FAR), 0)], None) + (far_bias,)

        def near_step(u):
            b0 = near0 + u * NSUB
            biases = near_bias(b0)
            return step_fns(k_fn(b0, NSUB), [(vt_fn(b0, NSUB), 0)],
                            lambda h: jnp.concatenate([b(h) for b in biases], axis=0)) + (None,)

        def sweep(gap_ref):
            init()
            _softmax_step(*first_fns(), m_ref, acc_ref)
            _softmax_loop(0, nfar, far_step, m_ref, acc_ref, gap_ref)
            _softmax_loop(0, nnear, near_step, m_ref, acc_ref, gap_ref)

        gap_ref[...] = jnp.full(gap_ref.shape, NEG_INF, f32)
        sweep(gap_ref)
        pl.when(_overshot(gap_ref))(lambda: sweep(None))
        finalize()

    @pl.when(s_id == nb_real)
    def _():
        init()
        _softmax_step(*step_fns(lambda h: kmf_ref[:, h * hw:(h + 1) * hw],
                                [(lambda h: vtmf_ref[0, h * A_VR:(h + 1) * A_VR, :], 0)],
                                lambda h: bias_rows(KIND_METAMETA, h, LANES)), m_ref, acc_ref)
        finalize()


def _attn_a(qs, k, vt, bias, lam, subln, *, bsz, nblk, lambda_init):
    n = k.shape[0]
    nb_real = bsz * nblk
    seq = nblk * LANES
    d = A_QD
    vr = A_HEADS * A_VR
    assert nblk % NSUB == 0
    km, vtm = _meta_views(k[nb_real * LANES:], vt[nb_real], bsz)
    kern = functools.partial(_attn_a_kernel, nblk=nblk, lambda_init=lambda_init)
    bclamp = lambda s: jnp.minimum(s // nblk, bsz - 1)
    return pl.pallas_call(
        kern,
        grid=(nb_real + 1,),
        in_specs=[
            pl.BlockSpec((1,) + qs.shape[1:], lambda s: (s, 0, 0, 0)),
            pl.BlockSpec((seq, d), lambda s: (bclamp(s), 0)),
            pl.BlockSpec((nblk, vr, LANES), lambda s: (bclamp(s), 0, 0)),
            pl.BlockSpec((1, N_META, d), lambda s: (bclamp(s), 0, 0)),
            pl.BlockSpec((1, vr, N_META), lambda s: (bclamp(s), 0, 0)),
            pl.BlockSpec((LANES, d), lambda s: (nb_real, 0)),
            pl.BlockSpec((1, vr, LANES), lambda s: (nb_real, 0, 0)),
            _bias_spec(),
            pl.BlockSpec((4, A_HD), lambda s: (0, 0)),
            pl.BlockSpec((A_VD, LANES), lambda s: (0, 0)),
        ],
        out_specs=pl.BlockSpec((LANES, d), lambda s: (s, 0)),
        out_shape=jax.ShapeDtypeStruct((n, d), bf16),
        scratch_shapes=[
            pltpu.VMEM((1, A_HEADS * 2 * LANES), f32),
            pltpu.VMEM((A_VR, A_HEADS * 2 * LANES), f32),
            pltpu.VMEM((1, A_HEADS * 2 * LANES), f32),
        ],
        compiler_params=_cparams(("arbitrary",)),
        name="attn_a",
    )(qs, k, vt, km, vtm, k, vt, bias, lam, jnp.broadcast_to(subln[:, None], (A_VD, LANES)))


C_QD = C_Q_HEADS * C_HD
C_KD = 2 * C_KV_HEADS * C_HD
C_VR = 2 * C_HD + ONES_ROWS


def _proj_c_kernel(h_ref, g_ref, w_ref, gq_ref, gk_ref, o_ref, vt_ref):
    xn = (_rms_rows(h_ref[...]) * g_ref[...]).astype(bf16)
    y = _dot(xn, w_ref[...])
    q = _group_rms(y[:, :C_QD], C_HD) * gq_ref[...] * (C_HD ** -0.5 * LOG2E)
    lo = _lo_half_mask(q.shape, 2 * C_HD, C_HD)
    o_ref[:, :C_QD] = jnp.where(lo, q, 0.0).astype(bf16)
    o_ref[:, C_QD:2 * C_QD] = jnp.where(lo, 0.0, q).astype(bf16)
    o_ref[:, 2 * C_QD:] = (_group_rms(y[:, C_QD:C_QD + C_KD], C_HD) * gk_ref[...]).astype(bf16)
    ones = jnp.ones((ONES_ROWS, LANES), bf16)
    for t in range(vt_ref.shape[0]):
        vt = y[t * LANES:(t + 1) * LANES, C_QD + C_KD:].T.astype(bf16)
        for g in range(C_KV_HEADS):
            vt_ref[t, g * C_VR:g * C_VR + 2 * C_HD, :] = vt[g * 2 * C_HD:(g + 1) * 2 * C_HD]
            vt_ref[t, g * C_VR + 2 * C_HD:(g + 1) * C_VR, :] = ones


def _proj_c(h, g, w, qk_norm, *, tm):
    n, d = h.shape
    nw = w.shape[1]
    nout = 2 * C_QD + C_KD
    gq = jnp.tile(qk_norm[0], C_QD // C_HD).reshape(1, C_QD)
    gk = jnp.tile(qk_norm[1], C_KD // C_HD).reshape(1, C_KD)
    return pl.pallas_call(
        _proj_c_kernel,
        grid=(n // tm,),
        in_specs=[
            pl.BlockSpec((tm, d), lambda i: (i, 0)),
            pl.BlockSpec((1, d), lambda i: (0, 0)),
            pl.BlockSpec((d, nw), lambda i: (0, 0)),
            pl.BlockSpec((1, C_QD), lambda i: (0, 0)),
            pl.BlockSpec((1, C_KD), lambda i: (0, 0)),
        ],
        out_specs=[
            pl.BlockSpec((tm, nout), lambda i: (i, 0)),
            pl.BlockSpec((tm // LANES, C_KV_HEADS * C_VR, LANES), lambda i: (i, 0, 0)),
        ],
        out_shape=[
            jax.ShapeDtypeStruct((n, nout), bf16),
            jax.ShapeDtypeStruct((n // LANES, C_KV_HEADS * C_VR, LANES), bf16),
        ],
        compiler_params=_cparams(("arbitrary",)),
        name="proj_c",
    )(h, g.reshape(1, d), w, gq, gk)


def _attn_c_kernel(sink_ref, qlo_ref, qhi_ref, k_ref, vt_ref, km_ref, vtm_ref, kmf_ref, vtmf_ref, bias_ref,
                   o_ref, qs_ref, *, nblk):
    s_id = pl.program_id(0)
    nb_real = pl.num_programs(0) - 1
    r_io, _ = _tile_iotas()
    vd = 2 * C_HD

    def attend(tiles):
        top = r_io < C_HD
        for g in range(C_KV_HEADS):
            for hh in range(C_GROUP):
                cc = (g * C_GROUP + hh) // 2
                src = qlo_ref if hh % 2 == 0 else qhi_ref
                qs_ref[hh * LANES:(hh + 1) * LANES, :] = src[:, cc * LANES:(cc + 1) * LANES]
            sink = jnp.concatenate(
                [jnp.full((1, LANES), sink_ref[g * C_GROUP + hh] * LOG2E, f32) for hh in range(C_GROUP)], axis=1)
            sts = []
            m = sink
            for (k_fn, vt_fn, bias_fn) in tiles:
                st = _dot_nt(k_fn(g), qs_ref[...])
                st = st + jnp.concatenate([bias_fn(g * C_GROUP + hh) for hh in range(C_GROUP)], axis=1)
                m = jnp.maximum(m, jnp.max(st, axis=0, keepdims=True))
                sts.append(st)
            acc = None
            for st, (k_fn, vt_fn, bias_fn) in zip(sts, tiles):
                pv = _dot(vt_fn(g), jnp.exp2(st - m).astype(bf16))
                acc = pv if acc is None else acc + pv
            o = acc[0:vd, :] / (acc[vd:vd + 1, :] + jnp.exp2(sink - m))
            for cc in range(C_GROUP // 2):
                even = o[:, (2 * cc) * LANES:(2 * cc + 1) * LANES]
                odd = o[:, (2 * cc + 1) * LANES:(2 * cc + 2) * LANES]
                col = (g * (C_GROUP // 2) + cc) * LANES
                o_ref[:, col:col + LANES] = jnp.where(top, even, odd).T.astype(bf16)

    @pl.when(s_id < nb_real)
    def _():
        i = s_id % nblk
        prev = jnp.maximum(i - 1, 0)
        poff = pl.multiple_of(prev * LANES, LANES)
        coff = pl.multiple_of(i * LANES, LANES)
        kind_m = jnp.where(i == 0, KIND_META0, KIND_FAR)
        kind_p = jnp.where(i == 0, KIND_MASKED, KIND_PREVWIN)
        attend([
            (lambda g: km_ref[0, :, g * LANES:(g + 1) * LANES], lambda g: vtm_ref[0, g * C_VR:(g + 1) * C_VR, :],
             lambda h: bias_ref[kind_m, h, 0:N_META, :]),
            (lambda g: k_ref[pl.ds(poff, LANES), g * LANES:(g + 1) * LANES],
             lambda g: vt_ref[prev, g * C_VR:(g + 1) * C_VR, :], lambda h: bias_ref[kind_p, h]),
            (lambda g: k_ref[pl.ds(coff, LANES), g * LANES:(g + 1) * LANES],
             lambda g: vt_ref[i, g * C_VR:(g + 1) * C_VR, :], lambda h: bias_ref[KIND_DIAG, h]),
        ])

    @pl.when(s_id == nb_real)
    def _():
        attend([(lambda g: kmf_ref[:, g * LANES:(g + 1) * LANES], lambda g: vtmf_ref[0, g * C_VR:(g + 1) * C_VR, :],
                 lambda h: bias_ref[KIND_METAMETA, h])])


def _attn_c(qkv, vt, bias, sinks, *, bsz, nblk):
    n = qkv.shape[0]
    nb_real = bsz * nblk
    seq = nblk * LANES
    vr = C_KV_HEADS * C_VR
    kern = functools.partial(_attn_c_kernel, nblk=nblk)
    bclamp = lambda s: jnp.minimum(s // nblk, bsz - 1)
    kcol = 2 * C_QD // C_KD
    km, vtm = _meta_views(qkv[nb_real * LANES:, 2 * C_QD:], vt[nb_real], bsz)
    return pl.pallas_call(
        kern,
        grid=(nb_real + 1,),
        in_specs=[
            pl.BlockSpec(memory_space=pltpu.SMEM),
            pl.BlockSpec((LANES, C_QD), lambda s: (s, 0)),
            pl.BlockSpec((LANES, C_QD), lambda s: (s, 1)),
            pl.BlockSpec((seq, C_KD), lambda s: (bclamp(s), kcol)),
            pl.BlockSpec((nblk, vr, LANES), lambda s: (bclamp(s), 0, 0)),
            pl.BlockSpec((1, N_META, C_KD), lambda s: (bclamp(s), 0, 0)),
            pl.BlockSpec((1, vr, N_META), lambda s: (bclamp(s), 0, 0)),
            pl.BlockSpec((LANES, C_KD), lambda s: (nb_real, kcol)),
            pl.BlockSpec((1, vr, LANES), lambda s: (nb_real, 0, 0)),
            _bias_spec(),
        ],
        out_specs=pl.BlockSpec((LANES, C_QD), lambda s: (s, 0)),
        out_shape=jax.ShapeDtypeStruct((n, C_QD), bf16),
        scratch_shapes=[pltpu.VMEM((C_GROUP * LANES, LANES), bf16)],
        compiler_params=_cparams(("arbitrary",)),
        name="attn_c",
    )(sinks, qkv, qkv, qkv, vt, km, vtm, qkv, vt, bias)


B_QA = B_HEADS * B_KV_RANK
B_QI = IDX_HEADS * IDX_DIM
B_W1 = 2 * B_Q_RANK + 2 * LANES
B_TR = B_KV_RANK + ONES_ROWS


def _proj_b_kernel(h_ref, g_ref, w1_ref, ln_ref, wuq_ref, qn_ref,
                   qa_ref, qi_ref, ckv_ref, ckvt_ref, kk_ref, wit_ref):
    xn = (_rms_rows(h_ref[...]) * g_ref[...]).astype(bf16)
    y = _dot(xn, w1_ref[...])
    r = B_Q_RANK
    cq = (_rms_rows(y[:, :r]) * ln_ref[0:1, :]).astype(bf16)
    ckv = _rms_rows(y[:, r:2 * r]) * ln_ref[1:2, :]
    ckv_ref[...] = ckv.astype(bf16)
    kk_ref[...] = _rms_rows(y[:, 2 * r:2 * r + LANES]).astype(bf16)
    wi = y[:, 2 * r + LANES:] * (IDX_HEADS ** -0.5)
    ones = jnp.ones((ONES_ROWS, LANES), bf16)
    for t in range(ckvt_ref.shape[0]):
        ckvt_ref[t, 0:r, :] = ckv[t * LANES:(t + 1) * LANES, :].T.astype(bf16)
        ckvt_ref[t, r:, :] = ones
        wit_ref[t] = wi[t * LANES:(t + 1) * LANES, :].T[0:IDX_HEADS, :]
    z = _dot(cq, wuq_ref[...])
    qa = (_group_rms(z[:, :B_QA], B_KV_RANK) * qn_ref[...] * (B_KV_RANK ** -0.5 * LOG2E)).astype(bf16)
    qi = z[:, B_QA:] * (IDX_DIM ** -0.5)
    lo = _lo_half_mask(qi.shape, 2 * IDX_DIM, IDX_DIM)
    qi_lo = jnp.where(lo, qi, 0.0).astype(bf16)
    qi_hi = jnp.where(lo, 0.0, qi).astype(bf16)
    for t in range(qa_ref.shape[0]):
        rows = slice(t * LANES, (t + 1) * LANES)
        for hd in range(B_HEADS):
            qa_ref[t, hd * LANES:(hd + 1) * LANES, :] = qa[rows, hd * r:(hd + 1) * r]
        for hh in range(IDX_HEADS):
            src = qi_lo if hh % 2 == 0 else qi_hi
            qi_ref[t, hh * LANES:(hh + 1) * LANES, :] = src[rows, (hh // 2) * LANES:(hh // 2 + 1) * LANES]


def _proj_b(h, g, w1, latent_norm, wuq, q_norm, *, tm):
    n, d = h.shape
    qn = jnp.tile(q_norm, B_HEADS).reshape(1, B_QA)
    row = lambda i: (i, 0)
    row3 = lambda i: (i, 0, 0)
    const = lambda i: (0, 0)
    nt = tm // LANES
    return pl.pallas_call(
        _proj_b_kernel,
        grid=(n // tm,),
        in_specs=[
            pl.BlockSpec((tm, d), row),
            pl.BlockSpec((1, d), const),
            pl.BlockSpec(w1.shape, const),
            pl.BlockSpec(latent_norm.shape, const),
            pl.BlockSpec(wuq.shape, const),
            pl.BlockSpec((1, B_QA), const),
        ],
        out_specs=[
            pl.BlockSpec((nt, B_HEADS * LANES, B_KV_RANK), row3),
            pl.BlockSpec((nt, IDX_HEADS * LANES, LANES), row3),
            pl.BlockSpec((tm, B_KV_RANK), row),
            pl.BlockSpec((nt, B_TR, LANES), row3),
            pl.BlockSpec((tm, LANES), row),
            pl.BlockSpec((nt, IDX_HEADS, LANES), row3),
        ],
        out_shape=[
            jax.ShapeDtypeStruct((n // LANES, B_HEADS * LANES, B_KV_RANK), bf16),
            jax.ShapeDtypeStruct((n // LANES, IDX_HEADS * LANES, LANES), bf16),
            jax.ShapeDtypeStruct((n, B_KV_RANK), bf16),
            jax.ShapeDtypeStruct((n // LANES, B_TR, LANES), bf16),
            jax.ShapeDtypeStruct((n, LANES), bf16),
            jax.ShapeDtypeStruct((n // LANES, IDX_HEADS, LANES), f32),
        ],
        compiler_params=_cparams(("arbitrary",)),
        name="proj_b",
    )(h, g.reshape(1, d), w1, latent_norm, wuq, qn)


def _attn_b_kernel(qs_ref, is_ref, wit_ref, ckv_ref, ckvt_ref, kk_ref, ckvm_ref, ckvtm_ref, kkm_ref,
                   ckvmf_ref, ckvtmf_ref, bias_ref, wuvt_ref,
                   o_ref, key_ref, pen_ref, m_ref, acc_ref, gap_ref, *, nblk, k_sel):
    s_id = pl.program_id(0)
    nb_real = pl.num_programs(0) - 1
    r_io, c_io = _tile_iotas()
    rk = B_KV_RANK

    def init():
        m_ref[...] = jnp.full(m_ref.shape, NEG_INF, f32)
        acc_ref[...] = jnp.zeros(acc_ref.shape, f32)

    def add_per_head(st, bias_fn, pen):
        cols = []
        for h in range(B_HEADS):
            add = pen if bias_fn is None else (bias_fn(h) if pen is None else bias_fn(h) + pen)
            cols.append(st[:, h * LANES:(h + 1) * LANES] + add)
        return jnp.concatenate(cols, axis=1)

    def step_fns(ckv_fn, ckvt_fns, bias_fn, pen_fn):
        def logits():
            st = _dot_nt(ckv_fn(), qs_ref[0])
            return add_per_head(st, bias_fn, None if pen_fn is None else pen_fn())

        def pv(pb):
            acc = None
            for ckvt_fn, r0 in ckvt_fns:
                ckvt = ckvt_fn()
                part = _dot(ckvt, pb[r0:r0 + ckvt.shape[1], :])
                acc = part if acc is None else acc + part
            return acc

        return logits, pv

    def finalize():
        olat = (acc_ref[0:rk, :] / acc_ref[rk:rk + 1, :]).astype(bf16)
        ot = jnp.concatenate([_dot(wuvt_ref[h], olat[:, h * LANES:(h + 1) * LANES]) for h in range(B_HEADS)], axis=0)
        o_ref[...] = ot.T.astype(bf16)

    def index_scores(kk):
        s = jnp.maximum(_dot_nt(kk, is_ref[0]), 0.0)
        wt = wit_ref[0]
        sc = jnp.zeros((kk.shape[0], LANES), f32)
        for hh in range(IDX_HEADS):
            sc = sc + wt[hh:hh + 1, :] * s[:, hh * LANES:(hh + 1) * LANES]
        return sc

    def sort_key(sc):
        bits = lax.bitcast_convert_type(sc + 0.0, jnp.int32)
        return jnp.where(bits < 0, bits ^ jnp.int32(0x7FFFFFFF), bits)

    @pl.when(s_id < nb_real)
    def _():
        i = s_id % nblk
        ntile = i + 2

        int_min_tile = jnp.full((LANES, LANES), INT_MIN, jnp.int32)
        key_ref[0] = int_min_tile
        key_ref[0, 0:N_META, :] = sort_key(index_scores(kkm_ref[0]))
        key_ref[i + 2] = int_min_tile

        def score_body(jp, carry):
            off = pl.multiple_of(jp * (2 * LANES), 2 * LANES)
            keys = sort_key(index_scores(kk_ref[pl.ds(off, 2 * LANES), :]))
            for t in range(2):
                j = _vec(2 * jp + t)
                vis = (j < i) | ((j == i) & ((r_io >> 6) <= (c_io >> 6)))
                key_ref[2 * jp + t + 1] = jnp.where(vis, keys[t * LANES:(t + 1) * LANES], jnp.int32(INT_MIN))
            return carry

        lax.fori_loop(0, i // 2 + 1, score_body, 0)

        def count(pred):
            def cbody(tp, accv):
                for t in (2 * tp, 2 * tp + 1):
                    accv = accv + jnp.where(pred(key_ref[t], t), 1.0, 0.0)
                return accv
            accv = lax.fori_loop(0, (ntile + 1) // 2, cbody, jnp.zeros((LANES, LANES), f32))
            return jnp.sum(accv, axis=0, keepdims=True)

        kf = float(k_sel)
        zero = jnp.zeros((1, LANES), jnp.int32)
        t0 = jnp.where(count(lambda k, t: k >= zero) >= kf, zero, jnp.int32(INT_MIN))

        def bit_body(it, tcur):
            cand = tcur | jnp.left_shift(jnp.int32(1), 30 - it)
            return jnp.where(count(lambda k, t: k >= cand) >= kf, cand, tcur)

        thr = lax.fori_loop(0, 31, bit_body, t0)

        need = kf - count(lambda k, t: k > thr)
        n_eq = count(lambda k, t: k == thr)
        has_thr = thr > jnp.int32(INT_MIN)
        tied = jnp.max(jnp.where(has_thr & (n_eq > need), 1.0, 0.0)) > 0.0

        def tie_search(_):
            def jbody(it, jcur):
                cand = jcur | jnp.left_shift(jnp.int32(1), 11 - it)
                cnt = count(lambda k, t: (k == thr) & ((t * LANES + r_io) < cand))
                return jnp.where(cnt < need, cand, jcur)
            return lax.fori_loop(0, 12, jbody, jnp.zeros((1, LANES), jnp.int32))

        j_last = lax.cond(tied, tie_search, lambda _: jnp.full((1, LANES), 4095, jnp.int32), 0)
        j_last = jnp.where(has_thr, j_last, -1)

        def pen_body(t, carry):
            k = key_ref[t]
            sel = (k > thr) | ((k == thr) & ((t * LANES + r_io) <= j_last))
            pen_ref[t] = jnp.where(sel, 0.0, NEG_INF)
            return carry

        lax.fori_loop(0, ntile, pen_body, 0)
        for t in range(1, NSUB):
            pen_ref[i + 1 + t] = jnp.full((LANES, LANES), NEG_INF, f32)

        kind_m = jnp.where(i == 0, KIND_META0, KIND_FAR)
        nfar, near0, nnear = _sweep_steps(i)

        def ckv_fn(b0, nsub):
            off = pl.multiple_of(b0 * LANES, NSUB * LANES)
            return lambda: ckv_ref[pl.ds(off, nsub * LANES), :]

        def ckvt_fn(b0, nsub):
            return lambda: jnp.concatenate([ckvt_ref[b0 + t] for t in range(nsub)], axis=1)

        def pen_fn(b0, nsub):
            return lambda: jnp.concatenate([pen_ref[b0 + t + 1] for t in range(nsub)], axis=0)

        def near_bias(b0):
            kinds = [_block_kind(b0 + t - i) for t in range(NSUB)]
            return lambda h: jnp.concatenate([bias_ref[kinds[t], h] for t in range(NSUB)], axis=0)

        def first_pen():
            return jnp.concatenate([pen_ref[0, 0:N_META, :], pen_fn(0, NSUB)()], axis=0)

        def first_fns(with_pen):
            bias01 = near_bias(0)
            return step_fns(lambda: jnp.concatenate([ckvm_ref[0], ckv_fn(0, NSUB)()], axis=0),
                            [(lambda: ckvtm_ref[0], 0), (ckvt_fn(0, NSUB), N_META)],
                            lambda h: jnp.concatenate([bias_ref[kind_m, h, 0:N_META, :], bias01(h)], axis=0),
                            first_pen if with_pen else None)

        def far_step(w):
            b0 = NSUB + w * NSUB_FAR
            far_bias = jnp.concatenate([bias_ref[KIND_FAR, h, 0:1, :] for h in range(B_HEADS)], axis=1)
            return step_fns(ckv_fn(b0, NSUB_FAR), [(ckvt_fn(b0, NSUB_FAR), 0)], None, pen_fn(b0, NSUB_FAR)) + (far_bias,)

        def near_step(u):
            b0 = near0 + u * NSUB
            return step_fns(ckv_fn(b0, NSUB), [(ckvt_fn(b0, NSUB), 0)], near_bias(b0), pen_fn(b0, NSUB)) + (None,)

        def sweep(gap_ref):
            init()
            if gap_ref is None:
                _softmax_step(*first_fns(True), m_ref, acc_ref)
            else:
                _softmax_step(*first_fns(False), m_ref, acc_ref,
                              post_fn=lambda st: add_per_head(st, None, first_pen()))
            _softmax_loop(0, nfar, far_step, m_ref, acc_ref, gap_ref)
            _softmax_loop(0, nnear, near_step, m_ref, acc_ref, gap_ref)

        gap_ref[...] = jnp.full(gap_ref.shape, NEG_INF, f32)
        sweep(gap_ref)
        faded = jnp.logical_not(jnp.min(acc_ref[rk:rk + 1, :]) >= LAZY_FLOOR)
        pl.when(_overshot(gap_ref) | faded)(lambda: sweep(None))
        finalize()

    @pl.when(s_id == nb_real)
    def _():
        init()
        _softmax_step(*step_fns(lambda: ckvmf_ref[...], [(lambda: ckvtmf_ref[0], 0)],
                                lambda h: bias_ref[KIND_METAMETA, h], None), m_ref, acc_ref)
        finalize()


def _attn_b(qa, qi, wit, ckv, ckvt, kk, bias, wuvt, *, bsz, nblk, k_sel):
    n = ckv.shape[0]
    nb_real = bsz * nblk
    seq = nblk * LANES
    assert k_sel >= N_META and (nblk + 1) * LANES <= 4096 and nblk % NSUB == 0
    ckvm, ckvtm = _meta_views(ckv[nb_real * LANES:], ckvt[nb_real], bsz)
    kkm = kk[nb_real * LANES:].reshape(bsz, N_META, LANES)
    kern = functools.partial(_attn_b_kernel, nblk=nblk, k_sel=k_sel)
    bidx = lambda s: jnp.minimum(s // nblk, bsz - 1)
    blk = lambda s: (s, 0)
    return pl.pallas_call(
        kern,
        grid=(nb_real + 1,),
        in_specs=[
            pl.BlockSpec((1,) + qa.shape[1:], lambda s: (s, 0, 0)),
            pl.BlockSpec((1,) + qi.shape[1:], lambda s: (s, 0, 0)),
            pl.BlockSpec((1, IDX_HEADS, LANES), lambda s: (s, 0, 0)),
            pl.BlockSpec((seq, B_KV_RANK), lambda s: (bidx(s), 0)),
            pl.BlockSpec((nblk, B_TR, LANES), lambda s: (bidx(s), 0, 0)),
            pl.BlockSpec((seq, LANES), lambda s: (bidx(s), 0)),
            pl.BlockSpec((1, N_META, B_KV_RANK), lambda s: (bidx(s), 0, 0)),
            pl.BlockSpec((1, B_TR, N_META), lambda s: (bidx(s), 0, 0)),
            pl.BlockSpec((1, N_META, LANES), lambda s: (bidx(s), 0, 0)),
            pl.BlockSpec((LANES, B_KV_RANK), lambda s: (nb_real, 0)),
            pl.BlockSpec((1, B_TR, LANES), lambda s: (nb_real, 0, 0)),
            _bias_spec(),
            pl.BlockSpec(wuvt.shape, lambda s: (0, 0, 0)),
        ],
        out_specs=pl.BlockSpec((LANES, B_HEADS * B_VD), blk),
        out_shape=jax.ShapeDtypeStruct((n, B_HEADS * B_VD), bf16),
        scratch_shapes=[
            pltpu.VMEM((nblk + 2, LANES, LANES), jnp.int32),
            pltpu.VMEM((nblk + NSUB, LANES, LANES), f32),
            pltpu.VMEM((1, B_HEADS * LANES), f32),
            pltpu.VMEM((B_TR, B_HEADS * LANES), f32),
            pltpu.VMEM((1, B_HEADS * LANES), f32),
        ],
        compiler_params=_cparams(("arbitrary",)),
        name="attn_b",
    )(qa, qi, wit, ckv, ckvt, kk, ckvm, ckvtm, kkm, ckv, ckvt, bias, wuvt)


def kernel(x, meta_tokens, rel_bias, ln_ffn1, ffn1_wi, ffn1_wo, ln_mix, w_out, ln_ffn2, ffn2_wi, ffn2_wo, a_w_in, a_qk_norm, a_lambda, a_subln, b_w_in, b_latent_norm, b_w_uq, b_q_norm, b_w_uv, c_w_in, c_qk_norm, c_sinks):
    bsz, seq, d = x.shape
    assert d == D_MODEL and seq % LANES == 0 and bsz * N_META == LANES
    nblk = seq // LANES
    n = bsz * seq + LANES
    k_sel = min(TOPK_MAX, seq // 4)
    tm_ffn = _row_tile(n, 1408)
    tm_proj = _row_tile(n, 384, LANES)
    fc = 256

    h = jnp.concatenate([x.reshape(bsz * seq, d),
                         jnp.broadcast_to(meta_tokens.astype(x.dtype), (bsz, N_META, d)).reshape(LANES, d)], axis=0)
    bias = _bias_tiles(rel_bias)

    for layer in range(DEPTH):
        h = _ffn(h, ln_ffn1[layer], ffn1_wi, ffn1_wo, layer, tm=tm_ffn, fc=fc)
        kind, j = layer % N_MIXERS, layer // N_MIXERS
        g = ln_mix[layer]
        if kind == 0:
            lambda_init = 0.8 - 0.6 * math.exp(-0.3 * layer)
            qs, k, vt = _proj_a(h, g, a_w_in[j].astype(bf16), a_qk_norm[j], tm=tm_proj)
            mix = _attn_a(qs, k, vt, bias, a_lambda[j], a_subln[j], bsz=bsz, nblk=nblk, lambda_init=lambda_init)
        elif kind == 1:
            w = b_w_in[j]
            r2 = B_Q_RANK + B_KV_RANK
            kcol = w[:, r2:r2 + IDX_DIM]
            w1 = jnp.concatenate([w[:, :r2], kcol, kcol, w[:, r2 + IDX_DIM:],
                                  jnp.zeros((d, LANES - IDX_HEADS), w.dtype)], axis=1).astype(bf16)
            assert w1.shape[1] == B_W1
            qa, qi, ckv, ckvt, kk, wit = _proj_b(h, g, w1, b_latent_norm[j], b_w_uq[j].astype(bf16), b_q_norm[j],
                                                 tm=tm_proj)
            wuvt = jnp.swapaxes(b_w_uv[j], 1, 2).astype(bf16)
            mix = _attn_b(qa, qi, wit, ckv, ckvt, kk, bias, wuvt, bsz=bsz, nblk=nblk, k_sel=k_sel)
        else:
            w = c_w_in[j]
            kcols = [w[:, C_QD + gi * C_HD:C_QD + (gi + 1) * C_HD] for gi in range(C_KV_HEADS)]
            voff = C_QD + C_KV_HEADS * C_HD
            vcols = [w[:, voff + gi * C_HD:voff + (gi + 1) * C_HD] for gi in range(C_KV_HEADS)]
            wc = jnp.concatenate([w[:, :C_QD]] + [kc for kc in kcols for _ in range(2)]
                                 + [vc for vc in vcols for _ in range(2)], axis=1).astype(bf16)
            qkv, vt = _proj_c(h, g, wc, c_qk_norm[j], tm=tm_proj)
            mix = _attn_c(qkv, vt, bias, c_sinks[j], bsz=bsz, nblk=nblk)
        h = _ffn(h, ln_ffn2[layer], ffn2_wi, ffn2_wo, layer, tm=tm_ffn, fc=fc,
                 mix=mix, wout=w_out[layer].astype(bf16))
    return h[:bsz * seq].reshape(bsz, seq, d)
```

```python
import functools
import math

import numpy as np
import jax
import jax.numpy as jnp
from jax import lax
from jax.experimental import pallas as pl
from jax.experimental.pallas import tpu as pltpu

D_MODEL = 1024
DEPTH = 4
CHUNK = 64
N_META = 16
N_MIXERS = 3
NEG_INF = -1e30
REL_BUCKETS = 32
REL_MAX_DIST = 128
REL_HEADS = 16
D_FF = 2816
A_HEADS = 8
A_HD = 64
A_VD = 2 * A_HD
B_HEADS = 16
B_Q_RANK = 256
B_KV_RANK = 256
B_VD = 64
IDX_HEADS = 8
IDX_DIM = 64
TOPK_MAX = 256
C_Q_HEADS = 16
C_KV_HEADS = 2
C_GROUP = C_Q_HEADS // C_KV_HEADS
C_HD = 64
EPS = 1e-6

LANES = 128
BF16_ROWS = 16
VMEM_LIMIT = 56 * 1024 * 1024
INT_MIN = -(2 ** 31)
NSUB = 2
NSUB_FAR = 4
LOG2E = math.log2(math.e)
LAZY_GAP = 57.0
LAZY_FLOOR = 2.0 ** -100
ONES_ROWS = BF16_ROWS

KIND_DIAG, KIND_PREV, KIND_FAR, KIND_META0, KIND_METAMETA, KIND_MASKED, KIND_PREVWIN = 0, 1, 2, 3, 4, 5, 6
N_KINDS = 7

f32 = jnp.float32
bf16 = jnp.bfloat16


def _cparams(sem):
    return pltpu.CompilerParams(dimension_semantics=sem, vmem_limit_bytes=VMEM_LIMIT)


def _row_tile(n, cap, mult=BF16_ROWS):
    best = None
    for t in range(mult, cap + 1, mult):
        if n % t == 0:
            best = t
    assert best is not None
    return best


def _dot(a, b):
    return jnp.dot(a, b, preferred_element_type=f32)


def _dot_nt(a, b):
    return lax.dot_general(a, b, (((1,), (1,)), ((), ())), preferred_element_type=f32)


def _rms_rows(x):
    return x * lax.rsqrt(jnp.mean(x * x, axis=-1, keepdims=True) + EPS)


def _lo_half_mask(shape, period, half):
    return (lax.broadcasted_iota(jnp.int32, shape, 1) & (period - 1)) < half


def _group_rms(x, group):
    r, c = x.shape
    outs = []
    if group == 64:
        lo = _lo_half_mask((r, LANES), LANES, 64)
        for ci in range(c // LANES):
            xc = x[:, ci * LANES:(ci + 1) * LANES]
            x2 = xc * xc
            s_lo = jnp.sum(jnp.where(lo, x2, 0.0), axis=-1, keepdims=True)
            s_hi = jnp.sum(jnp.where(lo, 0.0, x2), axis=-1, keepdims=True)
            inv = jnp.where(lo, lax.rsqrt(s_lo * (1.0 / 64) + EPS), lax.rsqrt(s_hi * (1.0 / 64) + EPS))
            outs.append(xc * inv)
    else:
        for gi in range(c // group):
            outs.append(_rms_rows(x[:, gi * group:(gi + 1) * group]))
    return outs[0] if len(outs) == 1 else jnp.concatenate(outs, axis=-1)


def _tile_iotas():
    r = lax.broadcasted_iota(jnp.int32, (LANES, LANES), 0)
    c = lax.broadcasted_iota(jnp.int32, (LANES, LANES), 1)
    return r, c


def _vec(s):
    return jnp.full((LANES, LANES), s, jnp.int32)


def _softmax_step(logits_fn, pv_fn, m_ref, acc_ref, gap_ref=None, offset=None, post_fn=None):
    st = logits_fn()
    if gap_ref is None:
        if offset is not None:
            st = st + offset
        m_old = m_ref[...]
        m_new = jnp.maximum(m_old, jnp.max(st, axis=0, keepdims=True))
        if post_fn is not None:
            st = post_fn(st)
        acc_ref[...] = jnp.exp2(m_old - m_new) * acc_ref[...] + pv_fn(jnp.exp2(st - m_new).astype(bf16))
        m_ref[...] = m_new
    else:
        shift = m_ref[...] if offset is None else m_ref[...] - offset
        gap_ref[...] = jnp.maximum(gap_ref[...], jnp.max(st, axis=0, keepdims=True) - shift)
        acc_ref[...] += pv_fn(jnp.exp2(st - shift).astype(bf16))


def _softmax_loop(lo, hi, step_fn, m_ref, acc_ref, gap_ref):
    def body(w, carry):
        logits_fn, pv_fn, offset = step_fn(w)
        _softmax_step(logits_fn, pv_fn, m_ref, acc_ref, gap_ref, offset)
        return carry

    lax.fori_loop(lo, hi, body, 0)


def _overshot(gap_ref):
    return jnp.logical_not(jnp.max(gap_ref[...]) <= LAZY_GAP)


def _block_kind(rel):
    return jnp.where(rel < -1, KIND_FAR,
                     jnp.where(rel == -1, KIND_PREV, jnp.where(rel == 0, KIND_DIAG, KIND_MASKED)))


def _sweep_steps(i):
    nfar = jnp.maximum(i - 1 - NSUB, 0) // NSUB_FAR
    near0 = NSUB + nfar * NSUB_FAR
    return nfar, near0, (i - near0 + NSUB) // NSUB


def _ffn_kernel(*refs, fuse_out, layer, fc, nj):
    if fuse_out:
        h_ref, mix_ref, wout_ref, g_ref, wi_hbm, wo_hbm, o_ref, xn_ref, wa_buf, wb_buf, wo_buf, sem = refs
    else:
        h_ref, g_ref, wi_hbm, wo_hbm, o_ref, xn_ref, wa_buf, wb_buf, wo_buf, sem = refs

    def chunk_copies(j, slot):
        lo = pl.multiple_of(j * fc, fc)
        hi = pl.multiple_of((nj + j) * fc, fc)
        return (pltpu.make_async_copy(wi_hbm.at[layer, :, pl.ds(lo, fc)], wa_buf.at[slot], sem.at[0, slot]),
                pltpu.make_async_copy(wi_hbm.at[layer, :, pl.ds(hi, fc)], wb_buf.at[slot], sem.at[1, slot]),
                pltpu.make_async_copy(wo_hbm.at[layer, pl.ds(lo, fc), :], wo_buf.at[slot], sem.at[2, slot]))

    i = pl.program_id(0)
    first = i * nj

    @pl.when(i == 0)
    def _():
        for c in chunk_copies(0, 0):
            c.start()

    r = h_ref[...]
    if fuse_out:
        r = r + _dot(mix_ref[...], wout_ref[...])
    o_ref[...] = r
    xn_ref[...] = (_rms_rows(r) * g_ref[...]).astype(bf16)

    def body(j, carry):
        slot = (first + j) & 1

        @pl.when((j + 1 < nj) | (i + 1 < pl.num_programs(0)))
        def _():
            for c in chunk_copies(jnp.where(j + 1 < nj, j + 1, 0), 1 - slot):
                c.start()

        for c in chunk_copies(j, slot):
            c.wait()
        xn = xn_ref[...]
        a = _dot(xn, wa_buf[slot].astype(bf16))
        b = _dot(xn, wb_buf[slot].astype(bf16))
        act = (a / (1.0 + jnp.exp(-a)) * b).astype(bf16)
        o_ref[...] += 0.5 * _dot(act, wo_buf[slot].astype(bf16))
        return carry

    lax.fori_loop(0, nj, body, 0)


def _ffn(h, g, wi, wo, layer, *, tm, fc, mix=None, wout=None, n_rows=None):
    d = h.shape[1]
    n = h.shape[0] if n_rows is None else n_rows
    assert n % tm == 0
    dff = wo.shape[1]
    nj = dff // fc
    fuse = mix is not None
    row = lambda i: (i, 0)
    in_specs = [pl.BlockSpec((tm, d), row)]
    args = [h]
    if fuse:
        in_specs += [pl.BlockSpec((tm, mix.shape[1]), row), pl.BlockSpec(wout.shape, lambda i: (0, 0))]
        args += [mix, wout]
    in_specs += [
        pl.BlockSpec((1, d), lambda i: (0, 0)),
        pl.BlockSpec(memory_space=pl.ANY),
        pl.BlockSpec(memory_space=pl.ANY),
    ]
    args += [g.reshape(1, d), wi, wo]
    return pl.pallas_call(
        functools.partial(_ffn_kernel, fuse_out=fuse, layer=layer, fc=fc, nj=nj),
        grid=(n // tm,),
        in_specs=in_specs,
        out_specs=pl.BlockSpec((tm, d), row),
        out_shape=jax.ShapeDtypeStruct((n, d), f32),
        scratch_shapes=[
            pltpu.VMEM((tm, d), bf16),
            pltpu.VMEM((2, d, fc), f32),
            pltpu.VMEM((2, d, fc), f32),
            pltpu.VMEM((2, fc, d), f32),
            pltpu.SemaphoreType.DMA((3, 2)),
        ],
        compiler_params=_cparams(("arbitrary",)),
        name="ffn_out" if fuse else "ffn",
    )(*args)


def _rel_bucket(rel):
    half = REL_BUCKETS // 2
    max_exact = half // 2
    n = jnp.abs(rel)
    large = max_exact + (jnp.log(jnp.maximum(n, 1).astype(jnp.float32) / max_exact)
                         / math.log(REL_MAX_DIST / max_exact) * (half - max_exact)).astype(jnp.int32)
    large = jnp.minimum(large, half - 1)
    return jnp.where(rel > 0, half, 0) + jnp.where(n < max_exact, n, large)


def _rel_tiles():
    k = np.arange(LANES)[:, None]
    q = np.arange(LANES)[None, :]
    far = np.full((LANES, LANES), -4 * LANES)
    ones = np.ones((LANES, LANES), bool)
    rels = [k - q, k - q - LANES, far, (k % N_META) - N_META - q, (k % N_META) - (q % N_META), far, k - q - LANES]
    vis = [(k // CHUNK) <= (q // CHUNK), ones, ones, ones, (k // N_META) == (q // N_META), ~ones,
           (q < CHUNK) | (k >= CHUNK)]
    return (np.stack([np.broadcast_to(a, (LANES, LANES)) for a in rels]).astype(np.int32),
            np.stack([np.broadcast_to(a, (LANES, LANES)) for a in vis]).astype(np.int32))


def _bias_kernel(rb_ref, bucket_ref, vis_ref, o_ref):
    h = pl.program_id(0)
    for kind in range(N_KINDS):
        bk = bucket_ref[kind]
        acc = jnp.zeros((LANES, LANES), f32)
        for b in range(REL_BUCKETS):
            acc = jnp.where(bk == b, rb_ref[b, h], acc)
        o_ref[kind, 0] = jnp.where(vis_ref[kind] != 0, acc * LOG2E, NEG_INF)


def _bias_tiles(rel_bias):
    rel, vis = _rel_tiles()
    bucket = _rel_bucket(jnp.asarray(rel))
    nk = N_KINDS
    return pl.pallas_call(
        _bias_kernel,
        grid=(REL_HEADS,),
        in_specs=[
            pl.BlockSpec(memory_space=pltpu.SMEM),
            pl.BlockSpec((nk, LANES, LANES), lambda h: (0, 0, 0)),
            pl.BlockSpec((nk, LANES, LANES), lambda h: (0, 0, 0)),
        ],
        out_specs=pl.BlockSpec((nk, 1, LANES, LANES), lambda h: (0, h, 0, 0)),
        out_shape=jax.ShapeDtypeStruct((nk, REL_HEADS, LANES, LANES), f32),
        compiler_params=_cparams(("arbitrary",)),
        name="bias_tiles",
    )(rel_bias, bucket, jnp.asarray(vis))


def _bias_spec():
    return pl.BlockSpec((N_KINDS, REL_HEADS, LANES, LANES), lambda s: (0, 0, 0, 0))


def _meta_views(rows, cols_t, bsz):
    f = rows.shape[1]
    return (rows.reshape(bsz, N_META, f),
            cols_t.reshape(cols_t.shape[0], bsz, N_META).transpose(1, 0, 2))


A_QD = A_HEADS * 2 * A_HD
A_VR = A_VD + ONES_ROWS


def _proj_a_kernel(h_ref, g_ref, w_ref, gq_ref, gk_ref, qs_ref, k_ref, vt_ref):
    xn = (_rms_rows(h_ref[...]) * g_ref[...]).astype(bf16)
    y = _dot(xn, w_ref[...])
    q = _group_rms(y[:, :A_QD], A_HD) * gq_ref[...]
    lo = _lo_half_mask(q.shape, 2 * A_HD, A_HD)
    q_lo = jnp.where(lo, q, 0.0).astype(bf16)
    q_hi = jnp.where(lo, 0.0, q).astype(bf16)
    k_ref[...] = (_group_rms(y[:, A_QD:2 * A_QD], A_HD) * gk_ref[...]).astype(bf16)
    ones = jnp.ones((ONES_ROWS, LANES), bf16)
    for t in range(vt_ref.shape[0]):
        rows = slice(t * LANES, (t + 1) * LANES)
        vt = y[rows, 2 * A_QD:].T.astype(bf16)
        for h in range(A_HEADS):
            qs_ref[t, h, :LANES, :] = q_lo[rows, h * A_VD:(h + 1) * A_VD]
            qs_ref[t, h, LANES:, :] = q_hi[rows, h * A_VD:(h + 1) * A_VD]
            vt_ref[t, h * A_VR:h * A_VR + A_VD, :] = vt[h * A_VD:(h + 1) * A_VD]
            vt_ref[t, h * A_VR + A_VD:(h + 1) * A_VR, :] = ones


def _proj_a(h, g, w, qk_norm, *, tm):
    n, d = h.shape
    nw = w.shape[1]
    gq = jnp.tile(qk_norm[0] * (A_HD ** -0.5 * LOG2E), A_QD // A_HD).reshape(1, A_QD)
    gk = jnp.tile(qk_norm[1], A_QD // A_HD).reshape(1, A_QD)
    return pl.pallas_call(
        _proj_a_kernel,
        grid=(n // tm,),
        in_specs=[
            pl.BlockSpec((tm, d), lambda i: (i, 0)),
            pl.BlockSpec((1, d), lambda i: (0, 0)),
            pl.BlockSpec((d, nw), lambda i: (0, 0)),
            pl.BlockSpec((1, A_QD), lambda i: (0, 0)),
            pl.BlockSpec((1, A_QD), lambda i: (0, 0)),
        ],
        out_specs=[
            pl.BlockSpec((tm // LANES, A_HEADS, 2 * LANES, A_VD), lambda i: (i, 0, 0, 0)),
            pl.BlockSpec((tm, A_QD), lambda i: (i, 0)),
            pl.BlockSpec((tm // LANES, A_HEADS * A_VR, LANES), lambda i: (i, 0, 0)),
        ],
        out_shape=[
            jax.ShapeDtypeStruct((n // LANES, A_HEADS, 2 * LANES, A_VD), bf16),
            jax.ShapeDtypeStruct((n, A_QD), bf16),
            jax.ShapeDtypeStruct((n // LANES, A_HEADS * A_VR, LANES), bf16),
        ],
        compiler_params=_cparams(("arbitrary",)),
        name="proj_a",
    )(h, g.reshape(1, d), w, gq, gk)


def _attn_a_kernel(qs_ref, k_ref, vt_ref, km_ref, vtm_ref, kmf_ref, vtmf_ref, bias_ref, lam_ref, sub_ref,
                   o_ref, m_ref, acc_ref, gap_ref, *, nblk, lambda_init):
    s_id = pl.program_id(0)
    nb_real = pl.num_programs(0) - 1
    hw = 2 * A_HD

    def init():
        m_ref[...] = jnp.full(m_ref.shape, NEG_INF, f32)
        acc_ref[...] = jnp.zeros(acc_ref.shape, f32)

    def step_fns(kt_fn, vtt_fns, bias_fn):
        def logits():
            sts = [_dot_nt(kt_fn(h), qs_ref[0, h]) for h in range(A_HEADS)]
            if bias_fn is not None:
                sts = [st + bias_fn(h) for h, st in enumerate(sts)]
            return jnp.concatenate(sts, axis=1)

        def pv(pb):
            outs = []
            for h in range(A_HEADS):
                acc = None
                for vtt_fn, r0 in vtt_fns:
                    vtt = vtt_fn(h)
                    part = _dot(vtt, pb[r0:r0 + vtt.shape[1], h * 2 * LANES:(h + 1) * 2 * LANES])
                    acc = part if acc is None else acc + part
                outs.append(acc)
            return jnp.concatenate(outs, axis=1)

        return logits, pv

    def bias_rows(kind, h, rows):
        return jnp.concatenate([bias_ref[kind, h, 0:rows, :], bias_ref[kind, A_HEADS + h, 0:rows, :]], axis=1)

    def finalize():
        lam = lam_ref[...]
        lam_full = (jnp.exp(jnp.sum(lam[0:1] * lam[1:2], axis=-1, keepdims=True))
                    - jnp.exp(jnp.sum(lam[2:3] * lam[3:4], axis=-1, keepdims=True)) + lambda_init)
        o = acc_ref[0:A_VD, :] / acc_ref[A_VD:A_VD + 1, :]
        for h in range(A_HEADS):
            d = o[:, 2 * h * LANES:(2 * h + 1) * LANES] - lam_full * o[:, (2 * h + 1) * LANES:(2 * h + 2) * LANES]
            d = d * lax.rsqrt(jnp.mean(d * d, axis=0, keepdims=True) + EPS) * sub_ref[...] * (1.0 - lambda_init)
            o_ref[:, h * hw:(h + 1) * hw] = d.T.astype(bf16)

    @pl.when(s_id < nb_real)
    def _():
        i = s_id % nblk
        kind_m = jnp.where(i == 0, KIND_META0, KIND_FAR)
        nfar, near0, nnear = _sweep_steps(i)

        def k_fn(b0, nsub):
            off = pl.multiple_of(b0 * LANES, NSUB * LANES)
            return lambda h: k_ref[pl.ds(off, nsub * LANES), h * hw:(h + 1) * hw]

        def vt_fn(b0, nsub):
            return lambda h: jnp.concatenate([vt_ref[b0 + t, h * A_VR:(h + 1) * A_VR, :] for t in range(nsub)], axis=1)

        def near_bias(b0):
            kinds = [_block_kind(b0 + t - i) for t in range(NSUB)]
            return [lambda h, kind=kind: bias_rows(kind, h, LANES) for kind in kinds]

        def first_fns():
            biases = [lambda h: bias_rows(kind_m, h, N_META)] + near_bias(0)
            return step_fns(
                lambda h: jnp.concatenate([km_ref[0, :, h * hw:(h + 1) * hw], k_fn(0, NSUB)(h)], axis=0),
                [(lambda h: vtm_ref[0, h * A_VR:(h + 1) * A_VR, :], 0), (vt_fn(0, NSUB), N_META)],
                lambda h: jnp.concatenate([b(h) for b in biases], axis=0))

        def far_step(w):
            b0 = NSUB + w * NSUB_FAR
            far_bias = jnp.concatenate([bias_rows(KIND_FAR, h, 1) for h in range(A_HEADS)], axis=1)
            return step_fns(k_fn(b0, NSUB_FAR), [(vt_fn(b0, NSUB_FAR), 0)], None) + (far_bias,)

        def near_step(u):
            b0 = near0 + u * NSUB
            biases = near_bias(b0)
            return step_fns(k_fn(b0, NSUB), [(vt_fn(b0, NSUB), 0)],
                            lambda h: jnp.concatenate([b(h) for b in biases], axis=0)) + (None,)

        def sweep(gap_ref):
            init()
            _softmax_step(*first_fns(), m_ref, acc_ref)
            _softmax_loop(0, nfar, far_step, m_ref, acc_ref, gap_ref)
            _softmax_loop(0, nnear, near_step, m_ref, acc_ref, gap_ref)

        gap_ref[...] = jnp.full(gap_ref.shape, NEG_INF, f32)
        sweep(gap_ref)
        pl.when(_overshot(gap_ref))(lambda: sweep(None))
        finalize()

    @pl.when(s_id == nb_real)
    def _():
        init()
        _softmax_step(*step_fns(lambda h: kmf_ref[:, h * hw:(h + 1) * hw],
                                [(lambda h: vtmf_ref[0, h * A_VR:(h + 1) * A_VR, :], 0)],
                                lambda h: bias_rows(KIND_METAMETA, h, LANES)), m_ref, acc_ref)
        finalize()


def _attn_a(qs, k, vt, bias, lam, subln, *, bsz, nblk, lambda_init):
    n = k.shape[0]
    nb_real = bsz * nblk
    seq = nblk * LANES
    d = A_QD
    vr = A_HEADS * A_VR
    assert nblk % NSUB == 0
    km, vtm = _meta_views(k[nb_real * LANES:], vt[nb_real], bsz)
    kern = functools.partial(_attn_a_kernel, nblk=nblk, lambda_init=lambda_init)
    bclamp = lambda s: jnp.minimum(s // nblk, bsz - 1)
    return pl.pallas_call(
        kern,
        grid=(nb_real + 1,),
        in_specs=[
            pl.BlockSpec((1,) + qs.shape[1:], lambda s: (s, 0, 0, 0)),
            pl.BlockSpec((seq, d), lambda s: (bclamp(s), 0)),
            pl.BlockSpec((nblk, vr, LANES), lambda s: (bclamp(s), 0, 0)),
            pl.BlockSpec((1, N_META, d), lambda s: (bclamp(s), 0, 0)),
            pl.BlockSpec((1, vr, N_META), lambda s: (bclamp(s), 0, 0)),
            pl.BlockSpec((LANES, d), lambda s: (nb_real, 0)),
            pl.BlockSpec((1, vr, LANES), lambda s: (nb_real, 0, 0)),
            _bias_spec(),
            pl.BlockSpec((4, A_HD), lambda s: (0, 0)),
            pl.BlockSpec((A_VD, LANES), lambda s: (0, 0)),
        ],
        out_specs=pl.BlockSpec((LANES, d), lambda s: (s, 0)),
        out_shape=jax.ShapeDtypeStruct((n, d), bf16),
        scratch_shapes=[
            pltpu.VMEM((1, A_HEADS * 2 * LANES), f32),
            pltpu.VMEM((A_VR, A_HEADS * 2 * LANES), f32),
            pltpu.VMEM((1, A_HEADS * 2 * LANES), f32),
        ],
        compiler_params=_cparams(("arbitrary",)),
        name="attn_a",
    )(qs, k, vt, km, vtm, k, vt, bias, lam, jnp.broadcast_to(subln[:, None], (A_VD, LANES)))


C_QD = C_Q_HEADS * C_HD
C_KD = 2 * C_KV_HEADS * C_HD
C_VR = 2 * C_HD + ONES_ROWS


def _proj_c_kernel(h_ref, g_ref, w_ref, gq_ref, gk_ref, o_ref, vt_ref):
    xn = (_rms_rows(h_ref[...]) * g_ref[...]).astype(bf16)
    y = _dot(xn, w_ref[...])
    q = _group_rms(y[:, :C_QD], C_HD) * gq_ref[...]
    lo = _lo_half_mask(q.shape, 2 * C_HD, C_HD)
    o_ref[:, :C_QD] = jnp.where(lo, q, 0.0).astype(bf16)
    o_ref[:, C_QD:2 * C_QD] = jnp.where(lo, 0.0, q).astype(bf16)
    o_ref[:, 2 * C_QD:] = (_group_rms(y[:, C_QD:C_QD + C_KD], C_HD) * gk_ref[...]).astype(bf16)
    ones = jnp.ones((ONES_ROWS, LANES), bf16)
    for t in range(vt_ref.shape[0]):
        vt = y[t * LANES:(t + 1) * LANES, C_QD + C_KD:].T.astype(bf16)
        for g in range(C_KV_HEADS):
            vt_ref[t, g * C_VR:g * C_VR + 2 * C_HD, :] = vt[g * 2 * C_HD:(g + 1) * 2 * C_HD]
            vt_ref[t, g * C_VR + 2 * C_HD:(g + 1) * C_VR, :] = ones


def _proj_c(h, g, w, qk_norm, *, tm):
    n, d = h.shape
    nw = w.shape[1]
    nout = 2 * C_QD + C_KD
    gq = jnp.tile(qk_norm[0] * (C_HD ** -0.5 * LOG2E), C_QD // C_HD).reshape(1, C_QD)
    gk = jnp.tile(qk_norm[1], C_KD // C_HD).reshape(1, C_KD)
    return pl.pallas_call(
        _proj_c_kernel,
        grid=(n // tm,),
        in_specs=[
            pl.BlockSpec((tm, d), lambda i: (i, 0)),
            pl.BlockSpec((1, d), lambda i: (0, 0)),
            pl.BlockSpec((d, nw), lambda i: (0, 0)),
            pl.BlockSpec((1, C_QD), lambda i: (0, 0)),
            pl.BlockSpec((1, C_KD), lambda i: (0, 0)),
        ],
        out_specs=[
            pl.BlockSpec((tm, nout), lambda i: (i, 0)),
            pl.BlockSpec((tm // LANES, C_KV_HEADS * C_VR, LANES), lambda i: (i, 0, 0)),
        ],
        out_shape=[
            jax.ShapeDtypeStruct((n, nout), bf16),
            jax.ShapeDtypeStruct((n // LANES, C_KV_HEADS * C_VR, LANES), bf16),
        ],
        compiler_params=_cparams(("arbitrary",)),
        name="proj_c",
    )(h, g.reshape(1, d), w, gq, gk)


def _attn_c_kernel(sink_ref, qlo_ref, qhi_ref, k_ref, vt_ref, km_ref, vtm_ref, kmf_ref, vtmf_ref, bias_ref,
                   o_ref, qs_ref, *, nblk):
    s_id = pl.program_id(0)
    nb_real = pl.num_programs(0) - 1
    r_io, _ = _tile_iotas()
    vd = 2 * C_HD

    def attend(tiles):
        top = r_io < C_HD
        for g in range(C_KV_HEADS):
            for hh in range(C_GROUP):
                cc = (g * C_GROUP + hh) // 2
                src = qlo_ref if hh % 2 == 0 else qhi_ref
                qs_ref[hh * LANES:(hh + 1) * LANES, :] = src[:, cc * LANES:(cc + 1) * LANES]
            sink = jnp.concatenate(
                [jnp.full((1, LANES), sink_ref[g * C_GROUP + hh] * LOG2E, f32) for hh in range(C_GROUP)], axis=1)
            sts = []
            m = sink
            for (k_fn, vt_fn, bias_fn) in tiles:
                st = _dot_nt(k_fn(g), qs_ref[...])
                st = st + jnp.concatenate([bias_fn(g * C_GROUP + hh) for hh in range(C_GROUP)], axis=1)
                m = jnp.maximum(m, jnp.max(st, axis=0, keepdims=True))
                sts.append(st)
            acc = None
            for st, (k_fn, vt_fn, bias_fn) in zip(sts, tiles):
                pv = _dot(vt_fn(g), jnp.exp2(st - m).astype(bf16))
                acc = pv if acc is None else acc + pv
            o = acc[0:vd, :] / (acc[vd:vd + 1, :] + jnp.exp2(sink - m))
            for cc in range(C_GROUP // 2):
                even = o[:, (2 * cc) * LANES:(2 * cc + 1) * LANES]
                odd = o[:, (2 * cc + 1) * LANES:(2 * cc + 2) * LANES]
                col = (g * (C_GROUP // 2) + cc) * LANES
                o_ref[:, col:col + LANES] = jnp.where(top, even, odd).T.astype(bf16)

    @pl.when(s_id < nb_real)
    def _():
        i = s_id % nblk
        prev = jnp.maximum(i - 1, 0)
        poff = pl.multiple_of(prev * LANES, LANES)
        coff = pl.multiple_of(i * LANES, LANES)
        kind_m = jnp.where(i == 0, KIND_META0, KIND_FAR)
        kind_p = jnp.where(i == 0, KIND_MASKED, KIND_PREVWIN)
        attend([
            (lambda g: km_ref[0, :, g * LANES:(g + 1) * LANES], lambda g: vtm_ref[0, g * C_VR:(g + 1) * C_VR, :],
             lambda h: bias_ref[kind_m, h, 0:N_META, :]),
            (lambda g: k_ref[pl.ds(poff, LANES), g * LANES:(g + 1) * LANES],
             lambda g: vt_ref[prev, g * C_VR:(g + 1) * C_VR, :], lambda h: bias_ref[kind_p, h]),
            (lambda g: k_ref[pl.ds(coff, LANES), g * LANES:(g + 1) * LANES],
             lambda g: vt_ref[i, g * C_VR:(g + 1) * C_VR, :], lambda h: bias_ref[KIND_DIAG, h]),
        ])

    @pl.when(s_id == nb_real)
    def _():
        attend([(lambda g: kmf_ref[:, g * LANES:(g + 1) * LANES], lambda g: vtmf_ref[0, g * C_VR:(g + 1) * C_VR, :],
                 lambda h: bias_ref[KIND_METAMETA, h])])


def _attn_c(qkv, vt, bias, sinks, *, bsz, nblk):
    n = qkv.shape[0]
    nb_real = bsz * nblk
    seq = nblk * LANES
    vr = C_KV_HEADS * C_VR
    kern = functools.partial(_attn_c_kernel, nblk=nblk)
    bclamp = lambda s: jnp.minimum(s // nblk, bsz - 1)
    kcol = 2 * C_QD // C_KD
    km, vtm = _meta_views(qkv[nb_real * LANES:, 2 * C_QD:], vt[nb_real], bsz)
    return pl.pallas_call(
        kern,
        grid=(nb_real + 1,),
        in_specs=[
            pl.BlockSpec(memory_space=pltpu.SMEM),
            pl.BlockSpec((LANES, C_QD), lambda s: (s, 0)),
            pl.BlockSpec((LANES, C_QD), lambda s: (s, 1)),
            pl.BlockSpec((seq, C_KD), lambda s: (bclamp(s), kcol)),
            pl.BlockSpec((nblk, vr, LANES), lambda s: (bclamp(s), 0, 0)),
            pl.BlockSpec((1, N_META, C_KD), lambda s: (bclamp(s), 0, 0)),
            pl.BlockSpec((1, vr, N_META), lambda s: (bclamp(s), 0, 0)),
            pl.BlockSpec((LANES, C_KD), lambda s: (nb_real, kcol)),
            pl.BlockSpec((1, vr, LANES), lambda s: (nb_real, 0, 0)),
            _bias_spec(),
        ],
        out_specs=pl.BlockSpec((LANES, C_QD), lambda s: (s, 0)),
        out_shape=jax.ShapeDtypeStruct((n, C_QD), bf16),
        scratch_shapes=[pltpu.VMEM((C_GROUP * LANES, LANES), bf16)],
        compiler_params=_cparams(("arbitrary",)),
        name="attn_c",
    )(sinks, qkv, qkv, qkv, vt, km, vtm, qkv, vt, bias)


B_QA = B_HEADS * B_KV_RANK
B_QI = IDX_HEADS * IDX_DIM
B_W1 = 2 * B_Q_RANK + 2 * LANES
B_TR = B_KV_RANK + ONES_ROWS


def _proj_b_kernel(h_ref, g_ref, w1_ref, ln_ref, wuq_ref, qn_ref,
                   qa_ref, qi_ref, ckv_ref, ckvt_ref, kk_ref, wit_ref):
    xn = (_rms_rows(h_ref[...]) * g_ref[...]).astype(bf16)
    y = _dot(xn, w1_ref[...])
    r = B_Q_RANK
    cq = (_rms_rows(y[:, :r]) * ln_ref[0:1, :]).astype(bf16)
    ckv = _rms_rows(y[:, r:2 * r]) * ln_ref[1:2, :]
    ckv_ref[...] = ckv.astype(bf16)
    kk_ref[...] = _rms_rows(y[:, 2 * r:2 * r + LANES]).astype(bf16)
    wi = y[:, 2 * r + LANES:] * (IDX_HEADS ** -0.5)
    ones = jnp.ones((ONES_ROWS, LANES), bf16)
    for t in range(ckvt_ref.shape[0]):
        ckvt_ref[t, 0:r, :] = ckv[t * LANES:(t + 1) * LANES, :].T.astype(bf16)
        ckvt_ref[t, r:, :] = ones
        wit_ref[t] = wi[t * LANES:(t + 1) * LANES, :].T[0:IDX_HEADS, :]
    z = _dot(cq, wuq_ref[...])
    qa = (_group_rms(z[:, :B_QA], B_KV_RANK) * qn_ref[...]).astype(bf16)
    qi = z[:, B_QA:] * (IDX_DIM ** -0.5)
    lo = _lo_half_mask(qi.shape, 2 * IDX_DIM, IDX_DIM)
    qi_lo = jnp.where(lo, qi, 0.0).astype(bf16)
    qi_hi = jnp.where(lo, 0.0, qi).astype(bf16)
    for t in range(qa_ref.shape[0]):
        rows = slice(t * LANES, (t + 1) * LANES)
        for hd in range(B_HEADS):
            qa_ref[t, hd * LANES:(hd + 1) * LANES, :] = qa[rows, hd * r:(hd + 1) * r]
        for hh in range(IDX_HEADS):
            src = qi_lo if hh % 2 == 0 else qi_hi
            qi_ref[t, hh * LANES:(hh + 1) * LANES, :] = src[rows, (hh // 2) * LANES:(hh // 2 + 1) * LANES]


def _proj_b(h, g, w1, latent_norm, wuq, q_norm, *, tm):
    n, d = h.shape
    qn = jnp.tile(q_norm * (B_KV_RANK ** -0.5 * LOG2E), B_HEADS).reshape(1, B_QA)
    row = lambda i: (i, 0)
    row3 = lambda i: (i, 0, 0)
    const = lambda i: (0, 0)
    nt = tm // LANES
    return pl.pallas_call(
        _proj_b_kernel,
        grid=(n // tm,),
        in_specs=[
            pl.BlockSpec((tm, d), row),
            pl.BlockSpec((1, d), const),
            pl.BlockSpec(w1.shape, const),
            pl.BlockSpec(latent_norm.shape, const),
            pl.BlockSpec(wuq.shape, const),
            pl.BlockSpec((1, B_QA), const),
        ],
        out_specs=[
            pl.BlockSpec((nt, B_HEADS * LANES, B_KV_RANK), row3),
            pl.BlockSpec((nt, IDX_HEADS * LANES, LANES), row3),
            pl.BlockSpec((tm, B_KV_RANK), row),
            pl.BlockSpec((nt, B_TR, LANES), row3),
            pl.BlockSpec((tm, LANES), row),
            pl.BlockSpec((nt, IDX_HEADS, LANES), row3),
        ],
        out_shape=[
            jax.ShapeDtypeStruct((n // LANES, B_HEADS * LANES, B_KV_RANK), bf16),
            jax.ShapeDtypeStruct((n // LANES, IDX_HEADS * LANES, LANES), bf16),
            jax.ShapeDtypeStruct((n, B_KV_RANK), bf16),
            jax.ShapeDtypeStruct((n // LANES, B_TR, LANES), bf16),
            jax.ShapeDtypeStruct((n, LANES), bf16),
            jax.ShapeDtypeStruct((n // LANES, IDX_HEADS, LANES), f32),
        ],
        compiler_params=_cparams(("arbitrary",)),
        name="proj_b",
    )(h, g.reshape(1, d), w1, latent_norm, wuq, qn)


def _attn_b_kernel(qs_ref, is_ref, wit_ref, ckv_ref, ckvt_ref, kk_ref, ckvm_ref, ckvtm_ref, kkm_ref,
                   ckvmf_ref, ckvtmf_ref, bias_ref, wuvt_ref,
                   o_ref, key_ref, pen_ref, m_ref, acc_ref, gap_ref, *, nblk, k_sel):
    s_id = pl.program_id(0)
    nb_real = pl.num_programs(0) - 1
    r_io, c_io = _tile_iotas()
    rk = B_KV_RANK

    def init():
        m_ref[...] = jnp.full(m_ref.shape, NEG_INF, f32)
        acc_ref[...] = jnp.zeros(acc_ref.shape, f32)

    def add_per_head(st, bias_fn, pen):
        cols = []
        for h in range(B_HEADS):
            add = pen if bias_fn is None else (bias_fn(h) if pen is None else bias_fn(h) + pen)
            cols.append(st[:, h * LANES:(h + 1) * LANES] + add)
        return jnp.concatenate(cols, axis=1)

    def step_fns(ckv_fn, ckvt_fns, bias_fn, pen_fn):
        def logits():
            st = _dot_nt(ckv_fn(), qs_ref[0])
            return add_per_head(st, bias_fn, None if pen_fn is None else pen_fn())

        def pv(pb):
            acc = None
            for ckvt_fn, r0 in ckvt_fns:
                ckvt = ckvt_fn()
                part = _dot(ckvt, pb[r0:r0 + ckvt.shape[1], :])
                acc = part if acc is None else acc + part
            return acc

        return logits, pv

    def finalize():
        olat = (acc_ref[0:rk, :] / acc_ref[rk:rk + 1, :]).astype(bf16)
        ot = jnp.concatenate([_dot(wuvt_ref[h], olat[:, h * LANES:(h + 1) * LANES]) for h in range(B_HEADS)], axis=0)
        o_ref[...] = ot.T.astype(bf16)

    def index_scores(kk):
        s = jnp.maximum(_dot_nt(kk, is_ref[0]), 0.0)
        wt = wit_ref[0]
        sc = jnp.zeros((kk.shape[0], LANES), f32)
        for hh in range(IDX_HEADS):
            sc = sc + wt[hh:hh + 1, :] * s[:, hh * LANES:(hh + 1) * LANES]
        return sc

    def sort_key(sc):
        bits = lax.bitcast_convert_type(sc + 0.0, jnp.int32)
        return jnp.where(bits < 0, bits ^ jnp.int32(0x7FFFFFFF), bits)

    @pl.when(s_id < nb_real)
    def _():
        i = s_id % nblk
        ntile = i + 2

        int_min_tile = jnp.full((LANES, LANES), INT_MIN, jnp.int32)
        key_ref[0] = int_min_tile
        key_ref[0, 0:N_META, :] = sort_key(index_scores(kkm_ref[0]))
        key_ref[i + 2] = int_min_tile

        def score_body(jp, carry):
            off = pl.multiple_of(jp * (2 * LANES), 2 * LANES)
            keys = sort_key(index_scores(kk_ref[pl.ds(off, 2 * LANES), :]))
            for t in range(2):
                j = _vec(2 * jp + t)
                vis = (j < i) | ((j == i) & ((r_io >> 6) <= (c_io >> 6)))
                key_ref[2 * jp + t + 1] = jnp.where(vis, keys[t * LANES:(t + 1) * LANES], jnp.int32(INT_MIN))
            return carry

        lax.fori_loop(0, i // 2 + 1, score_body, 0)

        def count(pred):
            def cbody(tp, accv):
                for t in (2 * tp, 2 * tp + 1):
                    accv = accv + jnp.where(pred(key_ref[t], t), 1.0, 0.0)
                return accv
            accv = lax.fori_loop(0, (ntile + 1) // 2, cbody, jnp.zeros((LANES, LANES), f32))
            return jnp.sum(accv, axis=0, keepdims=True)

        kf = float(k_sel)
        zero = jnp.zeros((1, LANES), jnp.int32)
        t0 = jnp.where(count(lambda k, t: k >= zero) >= kf, zero, jnp.int32(INT_MIN))

        def bit_body(it, tcur):
            cand = tcur | jnp.left_shift(jnp.int32(1), 30 - it)
            return jnp.where(count(lambda k, t: k >= cand) >= kf, cand, tcur)

        thr = lax.fori_loop(0, 31, bit_body, t0)

        need = kf - count(lambda k, t: k > thr)
        n_eq = count(lambda k, t: k == thr)
        has_thr = thr > jnp.int32(INT_MIN)
        tied = jnp.max(jnp.where(has_thr & (n_eq > need), 1.0, 0.0)) > 0.0

        def tie_search(_):
            def jbody(it, jcur):
                cand = jcur | jnp.left_shift(jnp.int32(1), 11 - it)
                cnt = count(lambda k, t: (k == thr) & ((t * LANES + r_io) < cand))
                return jnp.where(cnt < need, cand, jcur)
            return lax.fori_loop(0, 12, jbody, jnp.zeros((1, LANES), jnp.int32))

        j_last = lax.cond(tied, tie_search, lambda _: jnp.full((1, LANES), 4095, jnp.int32), 0)
        j_last = jnp.where(has_thr, j_last, -1)

        def pen_body(t, carry):
            k = key_ref[t]
            sel = (k > thr) | ((k == thr) & ((t * LANES + r_io) <= j_last))
            pen_ref[t] = jnp.where(sel, 0.0, NEG_INF)
            return carry

        lax.fori_loop(0, ntile, pen_body, 0)
        for t in range(1, NSUB):
            pen_ref[i + 1 + t] = jnp.full((LANES, LANES), NEG_INF, f32)

        kind_m = jnp.where(i == 0, KIND_META0, KIND_FAR)
        nfar, near0, nnear = _sweep_steps(i)

        def ckv_fn(b0, nsub):
            off = pl.multiple_of(b0 * LANES, NSUB * LANES)
            return lambda: ckv_ref[pl.ds(off, nsub * LANES), :]

        def ckvt_fn(b0, nsub):
            return lambda: jnp.concatenate([ckvt_ref[b0 + t] for t in range(nsub)], axis=1)

        def pen_fn(b0, nsub):
            return lambda: jnp.concatenate([pen_ref[b0 + t + 1] for t in range(nsub)], axis=0)

        def near_bias(b0):
            kinds = [_block_kind(b0 + t - i) for t in range(NSUB)]
            return lambda h: jnp.concatenate([bias_ref[kinds[t], h] for t in range(NSUB)], axis=0)

        def first_pen():
            return jnp.concatenate([pen_ref[0, 0:N_META, :], pen_fn(0, NSUB)()], axis=0)

        def first_fns(with_pen):
            bias01 = near_bias(0)
            return step_fns(lambda: jnp.concatenate([ckvm_ref[0], ckv_fn(0, NSUB)()], axis=0),
                            [(lambda: ckvtm_ref[0], 0), (ckvt_fn(0, NSUB), N_META)],
                            lambda h: jnp.concatenate([bias_ref[kind_m, h, 0:N_META, :], bias01(h)], axis=0),
                            first_pen if with_pen else None)

        def far_step(w):
            b0 = NSUB + w * NSUB_FAR
            far_bias = jnp.concatenate([bias_ref[KIND_FAR, h, 0:1, :] for h in range(B_HEADS)], axis=1)
            return step_fns(ckv_fn(b0, NSUB_FAR), [(ckvt_fn(b0, NSUB_FAR), 0)], None, pen_fn(b0, NSUB_FAR)) + (far_bias,)

        def near_step(u):
            b0 = near0 + u * NSUB
            return step_fns(ckv_fn(b0, NSUB), [(ckvt_fn(b0, NSUB), 0)], near_bias(b0), pen_fn(b0, NSUB)) + (None,)

        def sweep(gap_ref):
            init()
            if gap_ref is None:
                _softmax_step(*first_fns(True), m_ref, acc_ref)
            else:
                _softmax_step(*first_fns(False), m_ref, acc_ref,
                              post_fn=lambda st: add_per_head(st, None, first_pen()))
            _softmax_loop(0, nfar, far_step, m_ref, acc_ref, gap_ref)
            _softmax_loop(0, nnear, near_step, m_ref, acc_ref, gap_ref)

        gap_ref[...] = jnp.full(gap_ref.shape, NEG_INF, f32)
        sweep(gap_ref)
        faded = jnp.logical_not(jnp.min(acc_ref[rk:rk + 1, :]) >= LAZY_FLOOR)
        pl.when(_overshot(gap_ref) | faded)(lambda: sweep(None))
        finalize()

    @pl.when(s_id == nb_real)
    def _():
        init()
        _softmax_step(*step_fns(lambda: ckvmf_ref[...], [(lambda: ckvtmf_ref[0], 0)],
                                lambda h: bias_ref[KIND_METAMETA, h], None), m_ref, acc_ref)
        finalize()


def _attn_b(qa, qi, wit, ckv, ckvt, kk, bias, wuvt, *, bsz, nblk, k_sel):
    n = ckv.shape[0]
    nb_real = bsz * nblk
    seq = nblk * LANES
    assert k_sel >= N_META and (nblk + 1) * LANES <= 4096 and nblk % NSUB == 0
    ckvm, ckvtm = _meta_views(ckv[nb_real * LANES:], ckvt[nb_real], bsz)
    kkm = kk[nb_real * LANES:].reshape(bsz, N_META, LANES)
    kern = functools.partial(_attn_b_kernel, nblk=nblk, k_sel=k_sel)
    bidx = lambda s: jnp.minimum(s // nblk, bsz - 1)
    blk = lambda s: (s, 0)
    return pl.pallas_call(
        kern,
        grid=(nb_real + 1,),
        in_specs=[
            pl.BlockSpec((1,) + qa.shape[1:], lambda s: (s, 0, 0)),
            pl.BlockSpec((1,) + qi.shape[1:], lambda s: (s, 0, 0)),
            pl.BlockSpec((1, IDX_HEADS, LANES), lambda s: (s, 0, 0)),
            pl.BlockSpec((seq, B_KV_RANK), lambda s: (bidx(s), 0)),
            pl.BlockSpec((nblk, B_TR, LANES), lambda s: (bidx(s), 0, 0)),
            pl.BlockSpec((seq, LANES), lambda s: (bidx(s), 0)),
            pl.BlockSpec((1, N_META, B_KV_RANK), lambda s: (bidx(s), 0, 0)),
            pl.BlockSpec((1, B_TR, N_META), lambda s: (bidx(s), 0, 0)),
            pl.BlockSpec((1, N_META, LANES), lambda s: (bidx(s), 0, 0)),
            pl.BlockSpec((LANES, B_KV_RANK), lambda s: (nb_real, 0)),
            pl.BlockSpec((1, B_TR, LANES), lambda s: (nb_real, 0, 0)),
            _bias_spec(),
            pl.BlockSpec(wuvt.shape, lambda s: (0, 0, 0)),
        ],
        out_specs=pl.BlockSpec((LANES, B_HEADS * B_VD), blk),
        out_shape=jax.ShapeDtypeStruct((n, B_HEADS * B_VD), bf16),
        scratch_shapes=[
            pltpu.VMEM((nblk + 2, LANES, LANES), jnp.int32),
            pltpu.VMEM((nblk + NSUB, LANES, LANES), f32),
            pltpu.VMEM((1, B_HEADS * LANES), f32),
            pltpu.VMEM((B_TR, B_HEADS * LANES), f32),
            pltpu.VMEM((1, B_HEADS * LANES), f32),
        ],
        compiler_params=_cparams(("arbitrary",)),
        name="attn_b",
    )(qa, qi, wit, ckv, ckvt, kk, ckvm, ckvtm, kkm, ckv, ckvt, bias, wuvt)


def kernel(x, meta_tokens, rel_bias, ln_ffn1, ffn1_wi, ffn1_wo, ln_mix, w_out, ln_ffn2, ffn2_wi, ffn2_wo, a_w_in, a_qk_norm, a_lambda, a_subln, b_w_in, b_latent_norm, b_w_uq, b_q_norm, b_w_uv, c_w_in, c_qk_norm, c_sinks):
    bsz, seq, d = x.shape
    assert d == D_MODEL and seq % LANES == 0 and bsz * N_META == LANES
    nblk = seq // LANES
    n = bsz * seq + LANES
    k_sel = min(TOPK_MAX, seq // 4)
    tm_ffn = _row_tile(n, 1408)
    tm_last = _row_tile(bsz * seq, 1408)
    tm_proj = _row_tile(n, 384, LANES)
    fc = 256

    h = jnp.concatenate([x.reshape(bsz * seq, d),
                         jnp.broadcast_to(meta_tokens.astype(x.dtype), (bsz, N_META, d)).reshape(LANES, d)], axis=0)
    bias = _bias_tiles(rel_bias)

    for layer in range(DEPTH):
        h = _ffn(h, ln_ffn1[layer], ffn1_wi, ffn1_wo, layer, tm=tm_ffn, fc=fc)
        kind, j = layer % N_MIXERS, layer // N_MIXERS
        g = ln_mix[layer]
        if kind == 0:
            lambda_init = 0.8 - 0.6 * math.exp(-0.3 * layer)
            qs, k, vt = _proj_a(h, g, a_w_in[j].astype(bf16), a_qk_norm[j], tm=tm_proj)
            mix = _attn_a(qs, k, vt, bias, a_lambda[j], a_subln[j], bsz=bsz, nblk=nblk, lambda_init=lambda_init)
        elif kind == 1:
            w = b_w_in[j]
            r2 = B_Q_RANK + B_KV_RANK
            kcol = w[:, r2:r2 + IDX_DIM]
            w1 = jnp.concatenate([w[:, :r2], kcol, kcol, w[:, r2 + IDX_DIM:],
                                  jnp.zeros((d, LANES - IDX_HEADS), w.dtype)], axis=1).astype(bf16)
            assert w1.shape[1] == B_W1
            qa, qi, ckv, ckvt, kk, wit = _proj_b(h, g, w1, b_latent_norm[j], b_w_uq[j].astype(bf16), b_q_norm[j],
                                                 tm=tm_proj)
            wuvt = jnp.swapaxes(b_w_uv[j], 1, 2).astype(bf16)
            mix = _attn_b(qa, qi, wit, ckv, ckvt, kk, bias, wuvt, bsz=bsz, nblk=nblk, k_sel=k_sel)
        else:
            w = c_w_in[j]
            kcols = [w[:, C_QD + gi * C_HD:C_QD + (gi + 1) * C_HD] for gi in range(C_KV_HEADS)]
            voff = C_QD + C_KV_HEADS * C_HD
            vcols = [w[:, voff + gi * C_HD:voff + (gi + 1) * C_HD] for gi in range(C_KV_HEADS)]
            wc = jnp.concatenate([w[:, :C_QD]] + [kc for kc in kcols for _ in range(2)]
                                 + [vc for vc in vcols for _ in range(2)], axis=1).astype(bf16)
            qkv, vt = _proj_c(h, g, wc, c_qk_norm[j], tm=tm_proj)
            mix = _attn_c(qkv, vt, bias, c_sinks[j], bsz=bsz, nblk=nblk)
        last = layer == DEPTH - 1
        h = _ffn(h, ln_ffn2[layer], ffn2_wi, ffn2_wo, layer, tm=tm_last if last else tm_ffn, fc=fc,
                 mix=mix, wout=w_out[layer].astype(bf16), n_rows=bsz * seq if last else None)
    return h.reshape(bsz, seq, d)
```

```python
import functools
import math

import numpy as np
import jax
import jax.numpy as jnp
from jax import lax
from jax.experimental import pallas as pl
from jax.experimental.pallas import tpu as pltpu

D_MODEL = 1024
DEPTH = 4
CHUNK = 64
N_META = 16
N_MIXERS = 3
NEG_INF = -1e30
REL_BUCKETS = 32
REL_MAX_DIST = 128
REL_HEADS = 16
D_FF = 2816
A_HEADS = 8
A_HD = 64
A_VD = 2 * A_HD
B_HEADS = 16
B_Q_RANK = 256
B_KV_RANK = 256
B_VD = 64
IDX_HEADS = 8
IDX_DIM = 64
TOPK_MAX = 256
C_Q_HEADS = 16
C_KV_HEADS = 2
C_GROUP = C_Q_HEADS // C_KV_HEADS
C_HD = 64
EPS = 1e-6

LANES = 128
BF16_ROWS = 16
VMEM_LIMIT = 56 * 1024 * 1024
INT_MIN = -(2 ** 31)
NSUB = 2
NSUB_FAR = 4
LOG2E = math.log2(math.e)
LAZY_GAP = 57.0
LAZY_FLOOR = 2.0 ** -100
ONES_ROWS = BF16_ROWS

KIND_DIAG, KIND_PREV, KIND_FAR, KIND_META0, KIND_METAMETA, KIND_MASKED, KIND_PREVWIN = 0, 1, 2, 3, 4, 5, 6
N_KINDS = 7

f32 = jnp.float32
bf16 = jnp.bfloat16


def _cparams(sem):
    return pltpu.CompilerParams(dimension_semantics=sem, vmem_limit_bytes=VMEM_LIMIT)


def _row_tile(n, cap, mult=BF16_ROWS):
    best = None
    for t in range(mult, cap + 1, mult):
        if n % t == 0:
            best = t
    assert best is not None
    return best


def _dot(a, b):
    return jnp.dot(a, b, preferred_element_type=f32)


def _dot_nt(a, b):
    return lax.dot_general(a, b, (((1,), (1,)), ((), ())), preferred_element_type=f32)


def _rms_rows(x):
    return x * lax.rsqrt(jnp.mean(x * x, axis=-1, keepdims=True) + EPS)


def _lo_half_mask(shape, period, half):
    return (lax.broadcasted_iota(jnp.int32, shape, 1) & (period - 1)) < half


def _group_rms(x, group):
    r, c = x.shape
    outs = []
    if group == 64:
        lo = _lo_half_mask((r, LANES), LANES, 64)
        for ci in range(c // LANES):
            xc = x[:, ci * LANES:(ci + 1) * LANES]
            x2 = xc * xc
            s_lo = jnp.sum(jnp.where(lo, x2, 0.0), axis=-1, keepdims=True)
            s_hi = jnp.sum(jnp.where(lo, 0.0, x2), axis=-1, keepdims=True)
            inv = jnp.where(lo, lax.rsqrt(s_lo * (1.0 / 64) + EPS), lax.rsqrt(s_hi * (1.0 / 64) + EPS))
            outs.append(xc * inv)
    else:
        for gi in range(c // group):
            outs.append(_rms_rows(x[:, gi * group:(gi + 1) * group]))
    return outs[0] if len(outs) == 1 else jnp.concatenate(outs, axis=-1)


def _tile_iotas():
    r = lax.broadcasted_iota(jnp.int32, (LANES, LANES), 0)
    c = lax.broadcasted_iota(jnp.int32, (LANES, LANES), 1)
    return r, c


def _vec(s):
    return jnp.full((LANES, LANES), s, jnp.int32)


def _softmax_step(logits_fn, pv_fn, m_ref, acc_ref, gap_ref=None, offset=None, post_fn=None):
    st = logits_fn()
    if gap_ref is None:
        if offset is not None:
            st = st + offset
        m_old = m_ref[...]
        m_new = jnp.maximum(m_old, jnp.max(st, axis=0, keepdims=True))
        if post_fn is not None:
            st = post_fn(st)
        acc_ref[...] = jnp.exp2(m_old - m_new) * acc_ref[...] + pv_fn(jnp.exp2(st - m_new).astype(bf16))
        m_ref[...] = m_new
    else:
        shift = m_ref[...] if offset is None else m_ref[...] - offset
        gap_ref[...] = jnp.maximum(gap_ref[...], jnp.max(st, axis=0, keepdims=True) - shift)
        acc_ref[...] += pv_fn(jnp.exp2(st - shift).astype(bf16))


def _softmax_loop(lo, hi, step_fn, m_ref, acc_ref, gap_ref):
    def body(w, carry):
        logits_fn, pv_fn, offset = step_fn(w)
        _softmax_step(logits_fn, pv_fn, m_ref, acc_ref, gap_ref, offset)
        return carry

    lax.fori_loop(lo, hi, body, 0)


def _overshot(gap_ref):
    return jnp.logical_not(jnp.max(gap_ref[...]) <= LAZY_GAP)


def _block_kind(rel):
    return jnp.where(rel < -1, KIND_FAR,
                     jnp.where(rel == -1, KIND_PREV, jnp.where(rel == 0, KIND_DIAG, KIND_MASKED)))


def _sweep_steps(i):
    nfar = jnp.maximum(i - 1 - NSUB, 0) // NSUB_FAR
    near0 = NSUB + nfar * NSUB_FAR
    return nfar, near0, (i - near0 + NSUB) // NSUB


def _ffn_kernel(*refs, fuse_out, layer, fc, nj):
    if fuse_out:
        h_ref, mix_ref, wout_ref, g_ref, wi_hbm, wo_hbm, o_ref, xn_ref, wa_buf, wb_buf, wo_buf, sem = refs
    else:
        h_ref, g_ref, wi_hbm, wo_hbm, o_ref, xn_ref, wa_buf, wb_buf, wo_buf, sem = refs

    def chunk_copies(j, slot):
        lo = pl.multiple_of(j * fc, fc)
        hi = pl.multiple_of((nj + j) * fc, fc)
        return (pltpu.make_async_copy(wi_hbm.at[layer, :, pl.ds(lo, fc)], wa_buf.at[slot], sem.at[0, slot]),
                pltpu.make_async_copy(wi_hbm.at[layer, :, pl.ds(hi, fc)], wb_buf.at[slot], sem.at[1, slot]),
                pltpu.make_async_copy(wo_hbm.at[layer, pl.ds(lo, fc), :], wo_buf.at[slot], sem.at[2, slot]))

    i = pl.program_id(0)
    first = i * nj

    @pl.when(i == 0)
    def _():
        for c in chunk_copies(0, 0):
            c.start()

    r = h_ref[...]
    if fuse_out:
        r = r + _dot(mix_ref[...], wout_ref[...])
    o_ref[...] = r
    xn_ref[...] = (_rms_rows(r) * g_ref[...]).astype(bf16)

    def body(j, carry):
        slot = (first + j) & 1

        @pl.when((j + 1 < nj) | (i + 1 < pl.num_programs(0)))
        def _():
            for c in chunk_copies(jnp.where(j + 1 < nj, j + 1, 0), 1 - slot):
                c.start()

        for c in chunk_copies(j, slot):
            c.wait()
        xn = xn_ref[...]
        a = _dot(xn, wa_buf[slot].astype(bf16))
        b = _dot(xn, wb_buf[slot].astype(bf16))
        act = (a / (1.0 + jnp.exp(-a)) * b).astype(bf16)
        o_ref[...] += 0.5 * _dot(act, wo_buf[slot].astype(bf16))
        return carry

    lax.fori_loop(0, nj, body, 0)


def _ffn(h, g, wi, wo, layer, *, tm, fc, mix=None, wout=None, n_rows=None):
    d = h.shape[1]
    n = h.shape[0] if n_rows is None else n_rows
    assert n % tm == 0
    dff = wo.shape[1]
    nj = dff // fc
    fuse = mix is not None
    row = lambda i: (i, 0)
    in_specs = [pl.BlockSpec((tm, d), row)]
    args = [h]
    if fuse:
        in_specs += [pl.BlockSpec((tm, mix.shape[1]), row), pl.BlockSpec(wout.shape, lambda i: (0, 0))]
        args += [mix, wout]
    in_specs += [
        pl.BlockSpec((1, d), lambda i: (0, 0)),
        pl.BlockSpec(memory_space=pl.ANY),
        pl.BlockSpec(memory_space=pl.ANY),
    ]
    args += [g.reshape(1, d), wi, wo]
    return pl.pallas_call(
        functools.partial(_ffn_kernel, fuse_out=fuse, layer=layer, fc=fc, nj=nj),
        grid=(n // tm,),
        in_specs=in_specs,
        out_specs=pl.BlockSpec((tm, d), row),
        out_shape=jax.ShapeDtypeStruct((n, d), f32),
        scratch_shapes=[
            pltpu.VMEM((tm, d), bf16),
            pltpu.VMEM((2, d, fc), f32),
            pltpu.VMEM((2, d, fc), f32),
            pltpu.VMEM((2, fc, d), f32),
            pltpu.SemaphoreType.DMA((3, 2)),
        ],
        compiler_params=_cparams(("arbitrary",)),
        name="ffn_out" if fuse else "ffn",
    )(*args)


def _rel_bucket(rel):
    half = REL_BUCKETS // 2
    max_exact = half // 2
    n = jnp.abs(rel)
    large = max_exact + (jnp.log(jnp.maximum(n, 1).astype(jnp.float32) / max_exact)
                         / math.log(REL_MAX_DIST / max_exact) * (half - max_exact)).astype(jnp.int32)
    large = jnp.minimum(large, half - 1)
    return jnp.where(rel > 0, half, 0) + jnp.where(n < max_exact, n, large)


def _rel_tiles():
    k = np.arange(LANES)[:, None]
    q = np.arange(LANES)[None, :]
    far = np.full((LANES, LANES), -4 * LANES)
    ones = np.ones((LANES, LANES), bool)
    rels = [k - q, k - q - LANES, far, (k % N_META) - N_META - q, (k % N_META) - (q % N_META), far, k - q - LANES]
    vis = [(k // CHUNK) <= (q // CHUNK), ones, ones, ones, (k // N_META) == (q // N_META), ~ones,
           (q < CHUNK) | (k >= CHUNK)]
    return (np.stack([np.broadcast_to(a, (LANES, LANES)) for a in rels]).astype(np.int32),
            np.stack([np.broadcast_to(a, (LANES, LANES)) for a in vis]).astype(np.int32))


def _bias_kernel(rb_ref, bucket_ref, vis_ref, o_ref):
    h = pl.program_id(0)
    for kind in range(N_KINDS):
        bk = bucket_ref[kind]
        acc = jnp.zeros((LANES, LANES), f32)
        for b in range(REL_BUCKETS):
            acc = jnp.where(bk == b, rb_ref[b, h], acc)
        o_ref[kind, 0] = jnp.where(vis_ref[kind] != 0, acc * LOG2E, NEG_INF)


def _bias_tiles(rel_bias):
    rel, vis = _rel_tiles()
    bucket = _rel_bucket(jnp.asarray(rel))
    nk = N_KINDS
    return pl.pallas_call(
        _bias_kernel,
        grid=(REL_HEADS,),
        in_specs=[
            pl.BlockSpec(memory_space=pltpu.SMEM),
            pl.BlockSpec((nk, LANES, LANES), lambda h: (0, 0, 0)),
            pl.BlockSpec((nk, LANES, LANES), lambda h: (0, 0, 0)),
        ],
        out_specs=pl.BlockSpec((nk, 1, LANES, LANES), lambda h: (0, h, 0, 0)),
        out_shape=jax.ShapeDtypeStruct((nk, REL_HEADS, LANES, LANES), f32),
        compiler_params=_cparams(("arbitrary",)),
        name="bias_tiles",
    )(rel_bias, bucket, jnp.asarray(vis))


def _bias_spec():
    return pl.BlockSpec((N_KINDS, REL_HEADS, LANES, LANES), lambda s: (0, 0, 0, 0))


def _meta_views(rows, cols_t, bsz):
    f = rows.shape[1]
    return (rows.reshape(bsz, N_META, f),
            cols_t.reshape(cols_t.shape[0], bsz, N_META).transpose(1, 0, 2))


A_QD = A_HEADS * 2 * A_HD
A_VR = A_VD + ONES_ROWS


def _proj_a_kernel(h_ref, g_ref, w_ref, gq_ref, gk_ref, qs_ref, k_ref, vt_ref):
    xn = (_rms_rows(h_ref[...]) * g_ref[...]).astype(bf16)
    y = _dot(xn, w_ref[...])
    q = _group_rms(y[:, :A_QD], A_HD) * gq_ref[...]
    lo = _lo_half_mask(q.shape, 2 * A_HD, A_HD)
    q_lo = jnp.where(lo, q, 0.0).astype(bf16)
    q_hi = jnp.where(lo, 0.0, q).astype(bf16)
    k_ref[...] = (_group_rms(y[:, A_QD:2 * A_QD], A_HD) * gk_ref[...]).astype(bf16)
    ones = jnp.ones((ONES_ROWS, LANES), bf16)
    for t in range(vt_ref.shape[0]):
        rows = slice(t * LANES, (t + 1) * LANES)
        vt = y[rows, 2 * A_QD:].T.astype(bf16)
        for h in range(A_HEADS):
            qs_ref[t, h, :LANES, :] = q_lo[rows, h * A_VD:(h + 1) * A_VD]
            qs_ref[t, h, LANES:, :] = q_hi[rows, h * A_VD:(h + 1) * A_VD]
            vt_ref[t, h * A_VR:h * A_VR + A_VD, :] = vt[h * A_VD:(h + 1) * A_VD]
            vt_ref[t, h * A_VR + A_VD:(h + 1) * A_VR, :] = ones


def _proj_a(h, g, w, qk_norm, *, tm):
    n, d = h.shape
    nw = w.shape[1]
    gq = jnp.tile(qk_norm[0] * (A_HD ** -0.5 * LOG2E), A_QD // A_HD).reshape(1, A_QD)
    gk = jnp.tile(qk_norm[1], A_QD // A_HD).reshape(1, A_QD)
    return pl.pallas_call(
        _proj_a_kernel,
        grid=(n // tm,),
        in_specs=[
            pl.BlockSpec((tm, d), lambda i: (i, 0)),
            pl.BlockSpec((1, d), lambda i: (0, 0)),
            pl.BlockSpec((d, nw), lambda i: (0, 0)),
            pl.BlockSpec((1, A_QD), lambda i: (0, 0)),
            pl.BlockSpec((1, A_QD), lambda i: (0, 0)),
        ],
        out_specs=[
            pl.BlockSpec((tm // LANES, A_HEADS, 2 * LANES, A_VD), lambda i: (i, 0, 0, 0)),
            pl.BlockSpec((tm, A_QD), lambda i: (i, 0)),
            pl.BlockSpec((tm // LANES, A_HEADS * A_VR, LANES), lambda i: (i, 0, 0)),
        ],
        out_shape=[
            jax.ShapeDtypeStruct((n // LANES, A_HEADS, 2 * LANES, A_VD), bf16),
            jax.ShapeDtypeStruct((n, A_QD), bf16),
            jax.ShapeDtypeStruct((n // LANES, A_HEADS * A_VR, LANES), bf16),
        ],
        compiler_params=_cparams(("arbitrary",)),
        name="proj_a",
    )(h, g.reshape(1, d), w, gq, gk)


def _attn_a_kernel(qs_ref, k_ref, vt_ref, km_ref, vtm_ref, kmf_ref, vtmf_ref, bias_ref, lam_ref, sub_ref,
                   o_ref, m_ref, acc_ref, gap_ref, *, nblk, lambda_init):
    s_id = pl.program_id(0)
    nb_real = pl.num_programs(0) - 1
    hw = 2 * A_HD

    def init():
        m_ref[...] = jnp.full(m_ref.shape, NEG_INF, f32)
        acc_ref[...] = jnp.zeros(acc_ref.shape, f32)

    def step_fns(kt_fn, vtt_fns, bias_fn):
        def logits():
            sts = [_dot_nt(kt_fn(h), qs_ref[0, h]) for h in range(A_HEADS)]
            if bias_fn is not None:
                sts = [st + bias_fn(h) for h, st in enumerate(sts)]
            return jnp.concatenate(sts, axis=1)

        def pv(pb):
            outs = []
            for h in range(A_HEADS):
                acc = None
                for vtt_fn, r0 in vtt_fns:
                    vtt = vtt_fn(h)
                    part = _dot(vtt, pb[r0:r0 + vtt.shape[1], h * 2 * LANES:(h + 1) * 2 * LANES])
                    acc = part if acc is None else acc + part
                outs.append(acc)
            return jnp.concatenate(outs, axis=1)

        return logits, pv

    def bias_rows(kind, h, rows):
        return jnp.concatenate([bias_ref[kind, h, 0:rows, :], bias_ref[kind, A_HEADS + h, 0:rows, :]], axis=1)

    def finalize():
        lam = lam_ref[...]
        lam_full = (jnp.exp(jnp.sum(lam[0:1] * lam[1:2], axis=-1, keepdims=True))
                    - jnp.exp(jnp.sum(lam[2:3] * lam[3:4], axis=-1, keepdims=True)) + lambda_init)
        o = acc_ref[0:A_VD, :] * (1.0 / acc_ref[A_VD:A_VD + 1, :])
        for h in range(A_HEADS):
            d = o[:, 2 * h * LANES:(2 * h + 1) * LANES] - lam_full * o[:, (2 * h + 1) * LANES:(2 * h + 2) * LANES]
            d = d * lax.rsqrt(jnp.mean(d * d, axis=0, keepdims=True) + EPS) * sub_ref[...] * (1.0 - lambda_init)
            o_ref[:, h * hw:(h + 1) * hw] = d.T.astype(bf16)

    @pl.when(s_id < nb_real)
    def _():
        i = s_id % nblk
        kind_m = jnp.where(i == 0, KIND_META0, KIND_FAR)
        nfar, near0, nnear = _sweep_steps(i)

        def k_fn(b0, nsub):
            off = pl.multiple_of(b0 * LANES, NSUB * LANES)
            return lambda h: k_ref[pl.ds(off, nsub * LANES), h * hw:(h + 1) * hw]

        def vt_fn(b0, nsub):
            return lambda h: jnp.concatenate([vt_ref[b0 + t, h * A_VR:(h + 1) * A_VR, :] for t in range(nsub)], axis=1)

        def near_bias(b0):
            kinds = [_block_kind(b0 + t - i) for t in range(NSUB)]
            return [lambda h, kind=kind: bias_rows(kind, h, LANES) for kind in kinds]

        def first_fns():
            biases = [lambda h: bias_rows(kind_m, h, N_META)] + near_bias(0)
            return step_fns(
                lambda h: jnp.concatenate([km_ref[0, :, h * hw:(h + 1) * hw], k_fn(0, NSUB)(h)], axis=0),
                [(lambda h: vtm_ref[0, h * A_VR:(h + 1) * A_VR, :], 0), (vt_fn(0, NSUB), N_META)],
                lambda h: jnp.concatenate([b(h) for b in biases], axis=0))

        def far_step(w):
            b0 = NSUB + w * NSUB_FAR
            far_bias = jnp.concatenate([bias_rows(KIND_FAR, h, 1) for h in range(A_HEADS)], axis=1)
            return step_fns(k_fn(b0, NSUB_FAR), [(vt_fn(b0, NSUB_FAR), 0)], None) + (far_bias,)

        def near_step(u):
            b0 = near0 + u * NSUB
            biases = near_bias(b0)
            return step_fns(k_fn(b0, NSUB), [(vt_fn(b0, NSUB), 0)],
                            lambda h: jnp.concatenate([b(h) for b in biases], axis=0)) + (None,)

        def sweep(gap_ref):
            init()
            _softmax_step(*first_fns(), m_ref, acc_ref)
            _softmax_loop(0, nfar, far_step, m_ref, acc_ref, gap_ref)
            _softmax_loop(0, nnear, near_step, m_ref, acc_ref, gap_ref)

        gap_ref[...] = jnp.full(gap_ref.shape, NEG_INF, f32)
        sweep(gap_ref)
        pl.when(_overshot(gap_ref))(lambda: sweep(None))
        finalize()

    @pl.when(s_id == nb_real)
    def _():
        init()
        _softmax_step(*step_fns(lambda h: kmf_ref[:, h * hw:(h + 1) * hw],
                                [(lambda h: vtmf_ref[0, h * A_VR:(h + 1) * A_VR, :], 0)],
                                lambda h: bias_rows(KIND_METAMETA, h, LANES)), m_ref, acc_ref)
        finalize()


def _attn_a(qs, k, vt, bias, lam, subln, *, bsz, nblk, lambda_init):
    n = k.shape[0]
    nb_real = bsz * nblk
    seq = nblk * LANES
    d = A_QD
    vr = A_HEADS * A_VR
    assert nblk % NSUB == 0
    km, vtm = _meta_views(k[nb_real * LANES:], vt[nb_real], bsz)
    kern = functools.partial(_attn_a_kernel, nblk=nblk, lambda_init=lambda_init)
    bclamp = lambda s: jnp.minimum(s // nblk, bsz - 1)
    return pl.pallas_call(
        kern,
        grid=(nb_real + 1,),
        in_specs=[
            pl.BlockSpec((1,) + qs.shape[1:], lambda s: (s, 0, 0, 0)),
            pl.BlockSpec((seq, d), lambda s: (bclamp(s), 0)),
            pl.BlockSpec((nblk, vr, LANES), lambda s: (bclamp(s), 0, 0)),
            pl.BlockSpec((1, N_META, d), lambda s: (bclamp(s), 0, 0)),
            pl.BlockSpec((1, vr, N_META), lambda s: (bclamp(s), 0, 0)),
            pl.BlockSpec((LANES, d), lambda s: (nb_real, 0)),
            pl.BlockSpec((1, vr, LANES), lambda s: (nb_real, 0, 0)),
            _bias_spec(),
            pl.BlockSpec((4, A_HD), lambda s: (0, 0)),
            pl.BlockSpec((A_VD, LANES), lambda s: (0, 0)),
        ],
        out_specs=pl.BlockSpec((LANES, d), lambda s: (s, 0)),
        out_shape=jax.ShapeDtypeStruct((n, d), bf16),
        scratch_shapes=[
            pltpu.VMEM((1, A_HEADS * 2 * LANES), f32),
            pltpu.VMEM((A_VR, A_HEADS * 2 * LANES), f32),
            pltpu.VMEM((1, A_HEADS * 2 * LANES), f32),
        ],
        compiler_params=_cparams(("arbitrary",)),
        name="attn_a",
    )(qs, k, vt, km, vtm, k, vt, bias, lam, jnp.broadcast_to(subln[:, None], (A_VD, LANES)))


C_QD = C_Q_HEADS * C_HD
C_KD = 2 * C_KV_HEADS * C_HD
C_VR = 2 * C_HD + ONES_ROWS


def _proj_c_kernel(h_ref, g_ref, w_ref, gq_ref, gk_ref, o_ref, vt_ref):
    xn = (_rms_rows(h_ref[...]) * g_ref[...]).astype(bf16)
    y = _dot(xn, w_ref[...])
    q = _group_rms(y[:, :C_QD], C_HD) * gq_ref[...]
    lo = _lo_half_mask(q.shape, 2 * C_HD, C_HD)
    o_ref[:, :C_QD] = jnp.where(lo, q, 0.0).astype(bf16)
    o_ref[:, C_QD:2 * C_QD] = jnp.where(lo, 0.0, q).astype(bf16)
    o_ref[:, 2 * C_QD:] = (_group_rms(y[:, C_QD:C_QD + C_KD], C_HD) * gk_ref[...]).astype(bf16)
    ones = jnp.ones((ONES_ROWS, LANES), bf16)
    for t in range(vt_ref.shape[0]):
        vt = y[t * LANES:(t + 1) * LANES, C_QD + C_KD:].T.astype(bf16)
        for g in range(C_KV_HEADS):
            vt_ref[t, g * C_VR:g * C_VR + 2 * C_HD, :] = vt[g * 2 * C_HD:(g + 1) * 2 * C_HD]
            vt_ref[t, g * C_VR + 2 * C_HD:(g + 1) * C_VR, :] = ones


def _proj_c(h, g, w, qk_norm, *, tm):
    n, d = h.shape
    nw = w.shape[1]
    nout = 2 * C_QD + C_KD
    gq = jnp.tile(qk_norm[0] * (C_HD ** -0.5 * LOG2E), C_QD // C_HD).reshape(1, C_QD)
    gk = jnp.tile(qk_norm[1], C_KD // C_HD).reshape(1, C_KD)
    return pl.pallas_call(
        _proj_c_kernel,
        grid=(n // tm,),
        in_specs=[
            pl.BlockSpec((tm, d), lambda i: (i, 0)),
            pl.BlockSpec((1, d), lambda i: (0, 0)),
            pl.BlockSpec((d, nw), lambda i: (0, 0)),
            pl.BlockSpec((1, C_QD), lambda i: (0, 0)),
            pl.BlockSpec((1, C_KD), lambda i: (0, 0)),
        ],
        out_specs=[
            pl.BlockSpec((tm, nout), lambda i: (i, 0)),
            pl.BlockSpec((tm // LANES, C_KV_HEADS * C_VR, LANES), lambda i: (i, 0, 0)),
        ],
        out_shape=[
            jax.ShapeDtypeStruct((n, nout), bf16),
            jax.ShapeDtypeStruct((n // LANES, C_KV_HEADS * C_VR, LANES), bf16),
        ],
        compiler_params=_cparams(("arbitrary",)),
        name="proj_c",
    )(h, g.reshape(1, d), w, gq, gk)


def _attn_c_kernel(sink_ref, qlo_ref, qhi_ref, k_ref, vt_ref, km_ref, vtm_ref, kmf_ref, vtmf_ref, bias_ref,
                   o_ref, qs_ref, *, nblk):
    s_id = pl.program_id(0)
    nb_real = pl.num_programs(0) - 1
    r_io, _ = _tile_iotas()
    vd = 2 * C_HD

    def attend(tiles):
        top = r_io < C_HD
        for g in range(C_KV_HEADS):
            for hh in range(C_GROUP):
                cc = (g * C_GROUP + hh) // 2
                src = qlo_ref if hh % 2 == 0 else qhi_ref
                qs_ref[hh * LANES:(hh + 1) * LANES, :] = src[:, cc * LANES:(cc + 1) * LANES]
            sink = jnp.concatenate(
                [jnp.full((1, LANES), sink_ref[g * C_GROUP + hh] * LOG2E, f32) for hh in range(C_GROUP)], axis=1)
            sts = []
            m = sink
            for (k_fn, vt_fn, bias_fn) in tiles:
                st = _dot_nt(k_fn(g), qs_ref[...])
                st = st + jnp.concatenate([bias_fn(g * C_GROUP + hh) for hh in range(C_GROUP)], axis=1)
                m = jnp.maximum(m, jnp.max(st, axis=0, keepdims=True))
                sts.append(st)
            acc = None
            for st, (k_fn, vt_fn, bias_fn) in zip(sts, tiles):
                pv = _dot(vt_fn(g), jnp.exp2(st - m).astype(bf16))
                acc = pv if acc is None else acc + pv
            o = acc[0:vd, :] * (1.0 / (acc[vd:vd + 1, :] + jnp.exp2(sink - m)))
            for cc in range(C_GROUP // 2):
                even = o[:, (2 * cc) * LANES:(2 * cc + 1) * LANES]
                odd = o[:, (2 * cc + 1) * LANES:(2 * cc + 2) * LANES]
                col = (g * (C_GROUP // 2) + cc) * LANES
                o_ref[:, col:col + LANES] = jnp.where(top, even, odd).T.astype(bf16)

    @pl.when(s_id < nb_real)
    def _():
        i = s_id % nblk
        prev = jnp.maximum(i - 1, 0)
        poff = pl.multiple_of(prev * LANES, LANES)
        coff = pl.multiple_of(i * LANES, LANES)
        kind_m = jnp.where(i == 0, KIND_META0, KIND_FAR)
        kind_p = jnp.where(i == 0, KIND_MASKED, KIND_PREVWIN)
        attend([
            (lambda g: km_ref[0, :, g * LANES:(g + 1) * LANES], lambda g: vtm_ref[0, g * C_VR:(g + 1) * C_VR, :],
             lambda h: bias_ref[kind_m, h, 0:N_META, :]),
            (lambda g: k_ref[pl.ds(poff, LANES), g * LANES:(g + 1) * LANES],
             lambda g: vt_ref[prev, g * C_VR:(g + 1) * C_VR, :], lambda h: bias_ref[kind_p, h]),
            (lambda g: k_ref[pl.ds(coff, LANES), g * LANES:(g + 1) * LANES],
             lambda g: vt_ref[i, g * C_VR:(g + 1) * C_VR, :], lambda h: bias_ref[KIND_DIAG, h]),
        ])

    @pl.when(s_id == nb_real)
    def _():
        attend([(lambda g: kmf_ref[:, g * LANES:(g + 1) * LANES], lambda g: vtmf_ref[0, g * C_VR:(g + 1) * C_VR, :],
                 lambda h: bias_ref[KIND_METAMETA, h])])


def _attn_c(qkv, vt, bias, sinks, *, bsz, nblk):
    n = qkv.shape[0]
    nb_real = bsz * nblk
    seq = nblk * LANES
    vr = C_KV_HEADS * C_VR
    kern = functools.partial(_attn_c_kernel, nblk=nblk)
    bclamp = lambda s: jnp.minimum(s // nblk, bsz - 1)
    kcol = 2 * C_QD // C_KD
    km, vtm = _meta_views(qkv[nb_real * LANES:, 2 * C_QD:], vt[nb_real], bsz)
    return pl.pallas_call(
        kern,
        grid=(nb_real + 1,),
        in_specs=[
            pl.BlockSpec(memory_space=pltpu.SMEM),
            pl.BlockSpec((LANES, C_QD), lambda s: (s, 0)),
            pl.BlockSpec((LANES, C_QD), lambda s: (s, 1)),
            pl.BlockSpec((seq, C_KD), lambda s: (bclamp(s), kcol)),
            pl.BlockSpec((nblk, vr, LANES), lambda s: (bclamp(s), 0, 0)),
            pl.BlockSpec((1, N_META, C_KD), lambda s: (bclamp(s), 0, 0)),
            pl.BlockSpec((1, vr, N_META), lambda s: (bclamp(s), 0, 0)),
            pl.BlockSpec((LANES, C_KD), lambda s: (nb_real, kcol)),
            pl.BlockSpec((1, vr, LANES), lambda s: (nb_real, 0, 0)),
            _bias_spec(),
        ],
        out_specs=pl.BlockSpec((LANES, C_QD), lambda s: (s, 0)),
        out_shape=jax.ShapeDtypeStruct((n, C_QD), bf16),
        scratch_shapes=[pltpu.VMEM((C_GROUP * LANES, LANES), bf16)],
        compiler_params=_cparams(("arbitrary",)),
        name="attn_c",
    )(sinks, qkv, qkv, qkv, vt, km, vtm, qkv, vt, bias)


B_QA = B_HEADS * B_KV_RANK
B_QI = IDX_HEADS * IDX_DIM
B_W1 = 2 * B_Q_RANK + 2 * LANES
B_TR = B_KV_RANK + ONES_ROWS


def _proj_b_kernel(h_ref, g_ref, w1_ref, ln_ref, wuq_ref, qn_ref,
                   qa_ref, qi_ref, ckv_ref, ckvt_ref, kk_ref, wit_ref):
    xn = (_rms_rows(h_ref[...]) * g_ref[...]).astype(bf16)
    y = _dot(xn, w1_ref[...])
    r = B_Q_RANK
    cq = (_rms_rows(y[:, :r]) * ln_ref[0:1, :]).astype(bf16)
    ckv = _rms_rows(y[:, r:2 * r]) * ln_ref[1:2, :]
    ckv_ref[...] = ckv.astype(bf16)
    kk_ref[...] = _rms_rows(y[:, 2 * r:2 * r + LANES]).astype(bf16)
    wi = y[:, 2 * r + LANES:] * (IDX_HEADS ** -0.5)
    ones = jnp.ones((ONES_ROWS, LANES), bf16)
    for t in range(ckvt_ref.shape[0]):
        ckvt_ref[t, 0:r, :] = ckv[t * LANES:(t + 1) * LANES, :].T.astype(bf16)
        ckvt_ref[t, r:, :] = ones
        wit_ref[t] = wi[t * LANES:(t + 1) * LANES, :].T[0:IDX_HEADS, :]
    z = _dot(cq, wuq_ref[...])
    qa = (_group_rms(z[:, :B_QA], B_KV_RANK) * qn_ref[...]).astype(bf16)
    qi = z[:, B_QA:] * (IDX_DIM ** -0.5)
    lo = _lo_half_mask(qi.shape, 2 * IDX_DIM, IDX_DIM)
    qi_lo = jnp.where(lo, qi, 0.0).astype(bf16)
    qi_hi = jnp.where(lo, 0.0, qi).astype(bf16)
    for t in range(qa_ref.shape[0]):
        rows = slice(t * LANES, (t + 1) * LANES)
        for hd in range(B_HEADS):
            qa_ref[t, hd * LANES:(hd + 1) * LANES, :] = qa[rows, hd * r:(hd + 1) * r]
        for hh in range(IDX_HEADS):
            src = qi_lo if hh % 2 == 0 else qi_hi
            qi_ref[t, hh * LANES:(hh + 1) * LANES, :] = src[rows, (hh // 2) * LANES:(hh // 2 + 1) * LANES]


def _proj_b(h, g, w1, latent_norm, wuq, q_norm, *, tm):
    n, d = h.shape
    qn = jnp.tile(q_norm * (B_KV_RANK ** -0.5 * LOG2E), B_HEADS).reshape(1, B_QA)
    row = lambda i: (i, 0)
    row3 = lambda i: (i, 0, 0)
    const = lambda i: (0, 0)
    nt = tm // LANES
    return pl.pallas_call(
        _proj_b_kernel,
        grid=(n // tm,),
        in_specs=[
            pl.BlockSpec((tm, d), row),
            pl.BlockSpec((1, d), const),
            pl.BlockSpec(w1.shape, const),
            pl.BlockSpec(latent_norm.shape, const),
            pl.BlockSpec(wuq.shape, const),
            pl.BlockSpec((1, B_QA), const),
        ],
        out_specs=[
            pl.BlockSpec((nt, B_HEADS * LANES, B_KV_RANK), row3),
            pl.BlockSpec((nt, IDX_HEADS * LANES, LANES), row3),
            pl.BlockSpec((tm, B_KV_RANK), row),
            pl.BlockSpec((nt, B_TR, LANES), row3),
            pl.BlockSpec((tm, LANES), row),
            pl.BlockSpec((nt, IDX_HEADS, LANES), row3),
        ],
        out_shape=[
            jax.ShapeDtypeStruct((n // LANES, B_HEADS * LANES, B_KV_RANK), bf16),
            jax.ShapeDtypeStruct((n // LANES, IDX_HEADS * LANES, LANES), bf16),
            jax.ShapeDtypeStruct((n, B_KV_RANK), bf16),
            jax.ShapeDtypeStruct((n // LANES, B_TR, LANES), bf16),
            jax.ShapeDtypeStruct((n, LANES), bf16),
            jax.ShapeDtypeStruct((n // LANES, IDX_HEADS, LANES), f32),
        ],
        compiler_params=_cparams(("arbitrary",)),
        name="proj_b",
    )(h, g.reshape(1, d), w1, latent_norm, wuq, qn)


def _attn_b_kernel(qs_ref, is_ref, wit_ref, ckv_ref, ckvt_ref, kk_ref, ckvm_ref, ckvtm_ref, kkm_ref,
                   ckvmf_ref, ckvtmf_ref, bias_ref, wuvt_ref,
                   o_ref, key_ref, pen_ref, m_ref, acc_ref, gap_ref, *, nblk, k_sel):
    s_id = pl.program_id(0)
    nb_real = pl.num_programs(0) - 1
    r_io, c_io = _tile_iotas()
    rk = B_KV_RANK

    def init():
        m_ref[...] = jnp.full(m_ref.shape, NEG_INF, f32)
        acc_ref[...] = jnp.zeros(acc_ref.shape, f32)

    def add_per_head(st, bias_fn, pen):
        cols = []
        for h in range(B_HEADS):
            add = pen if bias_fn is None else (bias_fn(h) if pen is None else bias_fn(h) + pen)
            cols.append(st[:, h * LANES:(h + 1) * LANES] + add)
        return jnp.concatenate(cols, axis=1)

    def step_fns(ckv_fn, ckvt_fns, bias_fn, pen_fn):
        def logits():
            st = _dot_nt(ckv_fn(), qs_ref[0])
            return add_per_head(st, bias_fn, None if pen_fn is None else pen_fn())

        def pv(pb):
            acc = None
            for ckvt_fn, r0 in ckvt_fns:
                ckvt = ckvt_fn()
                part = _dot(ckvt, pb[r0:r0 + ckvt.shape[1], :])
                acc = part if acc is None else acc + part
            return acc

        return logits, pv

    def finalize():
        olat = (acc_ref[0:rk, :] * (1.0 / acc_ref[rk:rk + 1, :])).astype(bf16)
        ot = jnp.concatenate([_dot(wuvt_ref[h], olat[:, h * LANES:(h + 1) * LANES]) for h in range(B_HEADS)], axis=0)
        o_ref[...] = ot.T.astype(bf16)

    def index_scores(kk):
        s = jnp.maximum(_dot_nt(kk, is_ref[0]), 0.0)
        wt = wit_ref[0]
        sc = jnp.zeros((kk.shape[0], LANES), f32)
        for hh in range(IDX_HEADS):
            sc = sc + wt[hh:hh + 1, :] * s[:, hh * LANES:(hh + 1) * LANES]
        return sc

    def sort_key(sc):
        bits = lax.bitcast_convert_type(sc + 0.0, jnp.int32)
        return jnp.where(bits < 0, bits ^ jnp.int32(0x7FFFFFFF), bits)

    @pl.when(s_id < nb_real)
    def _():
        i = s_id % nblk
        ntile = i + 2

        int_min_tile = jnp.full((LANES, LANES), INT_MIN, jnp.int32)
        key_ref[0] = int_min_tile
        key_ref[0, 0:N_META, :] = sort_key(index_scores(kkm_ref[0]))
        key_ref[i + 2] = int_min_tile

        def score_body(jp, carry):
            off = pl.multiple_of(jp * (2 * LANES), 2 * LANES)
            keys = sort_key(index_scores(kk_ref[pl.ds(off, 2 * LANES), :]))
            for t in range(2):
                j = _vec(2 * jp + t)
                vis = (j < i) | ((j == i) & ((r_io >> 6) <= (c_io >> 6)))
                key_ref[2 * jp + t + 1] = jnp.where(vis, keys[t * LANES:(t + 1) * LANES], jnp.int32(INT_MIN))
            return carry

        lax.fori_loop(0, i // 2 + 1, score_body, 0)

        def count(pred):
            def cbody(tp, accv):
                for t in (2 * tp, 2 * tp + 1):
                    accv = accv + jnp.where(pred(key_ref[t], t), 1.0, 0.0)
                return accv
            accv = lax.fori_loop(0, (ntile + 1) // 2, cbody, jnp.zeros((LANES, LANES), f32))
            return jnp.sum(accv, axis=0, keepdims=True)

        kf = float(k_sel)
        zero = jnp.zeros((1, LANES), jnp.int32)
        t0 = jnp.where(count(lambda k, t: k >= zero) >= kf, zero, jnp.int32(INT_MIN))

        def bit_body(it, tcur):
            cand = tcur | jnp.left_shift(jnp.int32(1), 30 - it)
            return jnp.where(count(lambda k, t: k >= cand) >= kf, cand, tcur)

        thr = lax.fori_loop(0, 31, bit_body, t0)

        need = kf - count(lambda k, t: k > thr)
        n_eq = count(lambda k, t: k == thr)
        has_thr = thr > jnp.int32(INT_MIN)
        tied = jnp.max(jnp.where(has_thr & (n_eq > need), 1.0, 0.0)) > 0.0

        def tie_search(_):
            def jbody(it, jcur):
                cand = jcur | jnp.left_shift(jnp.int32(1), 11 - it)
                cnt = count(lambda k, t: (k == thr) & ((t * LANES + r_io) < cand))
                return jnp.where(cnt < need, cand, jcur)
            return lax.fori_loop(0, 12, jbody, jnp.zeros((1, LANES), jnp.int32))

        j_last = lax.cond(tied, tie_search, lambda _: jnp.full((1, LANES), 4095, jnp.int32), 0)
        j_last = jnp.where(has_thr, j_last, -1)

        def pen_body(t, carry):
            k = key_ref[t]
            sel = (k > thr) | ((k == thr) & ((t * LANES + r_io) <= j_last))
            pen_ref[t] = jnp.where(sel, 0.0, NEG_INF)
            return carry

        lax.fori_loop(0, ntile, pen_body, 0)
        for t in range(1, NSUB):
            pen_ref[i + 1 + t] = jnp.full((LANES, LANES), NEG_INF, f32)

        kind_m = jnp.where(i == 0, KIND_META0, KIND_FAR)
        nfar, near0, nnear = _sweep_steps(i)

        def ckv_fn(b0, nsub):
            off = pl.multiple_of(b0 * LANES, NSUB * LANES)
            return lambda: ckv_ref[pl.ds(off, nsub * LANES), :]

        def ckvt_fn(b0, nsub):
            return lambda: jnp.concatenate([ckvt_ref[b0 + t] for t in range(nsub)], axis=1)

        def pen_fn(b0, nsub):
            return lambda: jnp.concatenate([pen_ref[b0 + t + 1] for t in range(nsub)], axis=0)

        def near_bias(b0):
            kinds = [_block_kind(b0 + t - i) for t in range(NSUB)]
            return lambda h: jnp.concatenate([bias_ref[kinds[t], h] for t in range(NSUB)], axis=0)

        def first_pen():
            return jnp.concatenate([pen_ref[0, 0:N_META, :], pen_fn(0, NSUB)()], axis=0)

        def first_fns(with_pen):
            bias01 = near_bias(0)
            return step_fns(lambda: jnp.concatenate([ckvm_ref[0], ckv_fn(0, NSUB)()], axis=0),
                            [(lambda: ckvtm_ref[0], 0), (ckvt_fn(0, NSUB), N_META)],
                            lambda h: jnp.concatenate([bias_ref[kind_m, h, 0:N_META, :], bias01(h)], axis=0),
                            first_pen if with_pen else None)

        def far_step(w):
            b0 = NSUB + w * NSUB_FAR
            far_bias = jnp.concatenate([bias_ref[KIND_FAR, h, 0:1, :] for h in range(B_HEADS)], axis=1)
            return step_fns(ckv_fn(b0, NSUB_FAR), [(ckvt_fn(b0, NSUB_FAR), 0)], None, pen_fn(b0, NSUB_FAR)) + (far_bias,)

        def near_step(u):
            b0 = near0 + u * NSUB
            return step_fns(ckv_fn(b0, NSUB), [(ckvt_fn(b0, NSUB), 0)], near_bias(b0), pen_fn(b0, NSUB)) + (None,)

        def sweep(gap_ref):
            init()
            if gap_ref is None:
                _softmax_step(*first_fns(True), m_ref, acc_ref)
            else:
                _softmax_step(*first_fns(False), m_ref, acc_ref,
                              post_fn=lambda st: add_per_head(st, None, first_pen()))
            _softmax_loop(0, nfar, far_step, m_ref, acc_ref, gap_ref)
            _softmax_loop(0, nnear, near_step, m_ref, acc_ref, gap_ref)

        gap_ref[...] = jnp.full(gap_ref.shape, NEG_INF, f32)
        sweep(gap_ref)
        faded = jnp.logical_not(jnp.min(acc_ref[rk:rk + 1, :]) >= LAZY_FLOOR)
        pl.when(_overshot(gap_ref) | faded)(lambda: sweep(None))
        finalize()

    @pl.when(s_id == nb_real)
    def _():
        init()
        _softmax_step(*step_fns(lambda: ckvmf_ref[...], [(lambda: ckvtmf_ref[0], 0)],
                                lambda h: bias_ref[KIND_METAMETA, h], None), m_ref, acc_ref)
        finalize()


def _attn_b(qa, qi, wit, ckv, ckvt, kk, bias, wuvt, *, bsz, nblk, k_sel):
    n = ckv.shape[0]
    nb_real = bsz * nblk
    seq = nblk * LANES
    assert k_sel >= N_META and (nblk + 1) * LANES <= 4096 and nblk % NSUB == 0
    ckvm, ckvtm = _meta_views(ckv[nb_real * LANES:], ckvt[nb_real], bsz)
    kkm = kk[nb_real * LANES:].reshape(bsz, N_META, LANES)
    kern = functools.partial(_attn_b_kernel, nblk=nblk, k_sel=k_sel)
    bidx = lambda s: jnp.minimum(s // nblk, bsz - 1)
    blk = lambda s: (s, 0)
    return pl.pallas_call(
        kern,
        grid=(nb_real + 1,),
        in_specs=[
            pl.BlockSpec((1,) + qa.shape[1:], lambda s: (s, 0, 0)),
            pl.BlockSpec((1,) + qi.shape[1:], lambda s: (s, 0, 0)),
            pl.BlockSpec((1, IDX_HEADS, LANES), lambda s: (s, 0, 0)),
            pl.BlockSpec((seq, B_KV_RANK), lambda s: (bidx(s), 0)),
            pl.BlockSpec((nblk, B_TR, LANES), lambda s: (bidx(s), 0, 0)),
            pl.BlockSpec((seq, LANES), lambda s: (bidx(s), 0)),
            pl.BlockSpec((1, N_META, B_KV_RANK), lambda s: (bidx(s), 0, 0)),
            pl.BlockSpec((1, B_TR, N_META), lambda s: (bidx(s), 0, 0)),
            pl.BlockSpec((1, N_META, LANES), lambda s: (bidx(s), 0, 0)),
            pl.BlockSpec((LANES, B_KV_RANK), lambda s: (nb_real, 0)),
            pl.BlockSpec((1, B_TR, LANES), lambda s: (nb_real, 0, 0)),
            _bias_spec(),
            pl.BlockSpec(wuvt.shape, lambda s: (0, 0, 0)),
        ],
        out_specs=pl.BlockSpec((LANES, B_HEADS * B_VD), blk),
        out_shape=jax.ShapeDtypeStruct((n, B_HEADS * B_VD), bf16),
        scratch_shapes=[
            pltpu.VMEM((nblk + 2, LANES, LANES), jnp.int32),
            pltpu.VMEM((nblk + NSUB, LANES, LANES), f32),
            pltpu.VMEM((1, B_HEADS * LANES), f32),
            pltpu.VMEM((B_TR, B_HEADS * LANES), f32),
            pltpu.VMEM((1, B_HEADS * LANES), f32),
        ],
        compiler_params=_cparams(("arbitrary",)),
        name="attn_b",
    )(qa, qi, wit, ckv, ckvt, kk, ckvm, ckvtm, kkm, ckv, ckvt, bias, wuvt)


def kernel(x, meta_tokens, rel_bias, ln_ffn1, ffn1_wi, ffn1_wo, ln_mix, w_out, ln_ffn2, ffn2_wi, ffn2_wo, a_w_in, a_qk_norm, a_lambda, a_subln, b_w_in, b_latent_norm, b_w_uq, b_q_norm, b_w_uv, c_w_in, c_qk_norm, c_sinks):
    bsz, seq, d = x.shape
    assert d == D_MODEL and seq % LANES == 0 and bsz * N_META == LANES
    nblk = seq // LANES
    n = bsz * seq + LANES
    k_sel = min(TOPK_MAX, seq // 4)
    tm_ffn = _row_tile(n, 1408)
    tm_last = _row_tile(bsz * seq, 1408)
    tm_proj = _row_tile(n, 384, LANES)
    fc = 256

    h = jnp.concatenate([x.reshape(bsz * seq, d),
                         jnp.broadcast_to(meta_tokens.astype(x.dtype), (bsz, N_META, d)).reshape(LANES, d)], axis=0)
    bias = _bias_tiles(rel_bias)

    for layer in range(DEPTH):
        h = _ffn(h, ln_ffn1[layer], ffn1_wi, ffn1_wo, layer, tm=tm_ffn, fc=fc)
        kind, j = layer % N_MIXERS, layer // N_MIXERS
        g = ln_mix[layer]
        if kind == 0:
            lambda_init = 0.8 - 0.6 * math.exp(-0.3 * layer)
            qs, k, vt = _proj_a(h, g, a_w_in[j].astype(bf16), a_qk_norm[j], tm=tm_proj)
            mix = _attn_a(qs, k, vt, bias, a_lambda[j], a_subln[j], bsz=bsz, nblk=nblk, lambda_init=lambda_init)
        elif kind == 1:
            w = b_w_in[j]
            r2 = B_Q_RANK + B_KV_RANK
            kcol = w[:, r2:r2 + IDX_DIM]
            w1 = jnp.concatenate([w[:, :r2], kcol, kcol, w[:, r2 + IDX_DIM:],
                                  jnp.zeros((d, LANES - IDX_HEADS), w.dtype)], axis=1).astype(bf16)
            assert w1.shape[1] == B_W1
            qa, qi, ckv, ckvt, kk, wit = _proj_b(h, g, w1, b_latent_norm[j], b_w_uq[j].astype(bf16), b_q_norm[j],
                                                 tm=tm_proj)
            wuvt = jnp.swapaxes(b_w_uv[j], 1, 2).astype(bf16)
            mix = _attn_b(qa, qi, wit, ckv, ckvt, kk, bias, wuvt, bsz=bsz, nblk=nblk, k_sel=k_sel)
        else:
            w = c_w_in[j]
            kcols = [w[:, C_QD + gi * C_HD:C_QD + (gi + 1) * C_HD] for gi in range(C_KV_HEADS)]
            voff = C_QD + C_KV_HEADS * C_HD
            vcols = [w[:, voff + gi * C_HD:voff + (gi + 1) * C_HD] for gi in range(C_KV_HEADS)]
            wc = jnp.concatenate([w[:, :C_QD]] + [kc for kc in kcols for _ in range(2)]
                                 + [vc for vc in vcols for _ in range(2)], axis=1).astype(bf16)
            qkv, vt = _proj_c(h, g, wc, c_qk_norm[j], tm=tm_proj)
            mix = _attn_c(qkv, vt, bias, c_sinks[j], bsz=bsz, nblk=nblk)
        last = layer == DEPTH - 1
        h = _ffn(h, ln_ffn2[layer], ffn2_wi, ffn2_wo, layer, tm=tm_last if last else tm_ffn, fc=fc,
                 mix=mix, wout=w_out[layer].astype(bf16), n_rows=bsz * seq if last else None)
    return h.reshape(bsz, seq, d)
```

```python
import functools
import math

import numpy as np
import jax
import jax.numpy as jnp
from jax import lax
from jax.experimental import pallas as pl
from jax.experimental.pallas import tpu as pltpu

D_MODEL = 1024
DEPTH = 4
CHUNK = 64
N_META = 16
N_MIXERS = 3
NEG_INF = -1e30
REL_BUCKETS = 32
REL_MAX_DIST = 128
REL_HEADS = 16
D_FF = 2816
A_HEADS = 8
A_HD = 64
A_VD = 2 * A_HD
B_HEADS = 16
B_Q_RANK = 256
B_KV_RANK = 256
B_VD = 64
IDX_HEADS = 8
IDX_DIM = 64
TOPK_MAX = 256
C_Q_HEADS = 16
C_KV_HEADS = 2
C_GROUP = C_Q_HEADS // C_KV_HEADS
C_HD = 64
EPS = 1e-6

LANES = 128
BF16_ROWS = 16
VMEM_LIMIT = 56 * 1024 * 1024
INT_MIN = -(2 ** 31)
NSUB = 2
NSUB_FAR = 4
LOG2E = math.log2(math.e)
LAZY_GAP = 57.0
LAZY_FLOOR = 2.0 ** -100
ONES_ROWS = BF16_ROWS

KIND_DIAG, KIND_PREV, KIND_FAR, KIND_META0, KIND_METAMETA, KIND_MASKED, KIND_PREVWIN = 0, 1, 2, 3, 4, 5, 6
N_KINDS = 7

f32 = jnp.float32
bf16 = jnp.bfloat16


def _cparams(sem):
    return pltpu.CompilerParams(dimension_semantics=sem, vmem_limit_bytes=VMEM_LIMIT)


def _row_tile(n, cap, mult=BF16_ROWS):
    best = None
    for t in range(mult, cap + 1, mult):
        if n % t == 0:
            best = t
    assert best is not None
    return best


def _dot(a, b):
    return jnp.dot(a, b, preferred_element_type=f32)


def _dot_nt(a, b):
    return lax.dot_general(a, b, (((1,), (1,)), ((), ())), preferred_element_type=f32)


def _rms_rows(x):
    return x * lax.rsqrt(jnp.mean(x * x, axis=-1, keepdims=True) + EPS)


def _lo_half_mask(shape, period, half):
    return (lax.broadcasted_iota(jnp.int32, shape, 1) & (period - 1)) < half


def _group_rms(x, group):
    r, c = x.shape
    outs = []
    if group == 64:
        lo = _lo_half_mask((r, LANES), LANES, 64)
        for ci in range(c // LANES):
            xc = x[:, ci * LANES:(ci + 1) * LANES]
            x2 = xc * xc
            s_lo = jnp.sum(jnp.where(lo, x2, 0.0), axis=-1, keepdims=True)
            s_hi = jnp.sum(jnp.where(lo, 0.0, x2), axis=-1, keepdims=True)
            inv = jnp.where(lo, lax.rsqrt(s_lo * (1.0 / 64) + EPS), lax.rsqrt(s_hi * (1.0 / 64) + EPS))
            outs.append(xc * inv)
    else:
        for gi in range(c // group):
            outs.append(_rms_rows(x[:, gi * group:(gi + 1) * group]))
    return outs[0] if len(outs) == 1 else jnp.concatenate(outs, axis=-1)


def _tile_iotas():
    r = lax.broadcasted_iota(jnp.int32, (LANES, LANES), 0)
    c = lax.broadcasted_iota(jnp.int32, (LANES, LANES), 1)
    return r, c


def _vec(s):
    return jnp.full((LANES, LANES), s, jnp.int32)


def _softmax_step(logits_fn, pv_fn, m_ref, acc_ref, gap_ref=None, offset=None, post_fn=None):
    st = logits_fn()
    if gap_ref is None:
        if offset is not None:
            st = st + offset
        m_old = m_ref[...]
        m_new = jnp.maximum(m_old, jnp.max(st, axis=0, keepdims=True))
        if post_fn is not None:
            st = post_fn(st)
        acc_ref[...] = jnp.exp2(m_old - m_new) * acc_ref[...] + pv_fn(jnp.exp2(st - m_new).astype(bf16))
        m_ref[...] = m_new
    else:
        shift = m_ref[...] if offset is None else m_ref[...] - offset
        gap_ref[...] = jnp.maximum(gap_ref[...], jnp.max(st, axis=0, keepdims=True) - shift)
        acc_ref[...] += pv_fn(jnp.exp2(st - shift).astype(bf16))


def _softmax_loop(lo, hi, step_fn, m_ref, acc_ref, gap_ref):
    def body(w, carry):
        logits_fn, pv_fn, offset = step_fn(w)
        _softmax_step(logits_fn, pv_fn, m_ref, acc_ref, gap_ref, offset)
        return carry

    lax.fori_loop(lo, hi, body, 0)


def _overshot(gap_ref):
    return jnp.logical_not(jnp.max(gap_ref[...]) <= LAZY_GAP)


def _block_kind(rel):
    return jnp.where(rel < -1, KIND_FAR,
                     jnp.where(rel == -1, KIND_PREV, jnp.where(rel == 0, KIND_DIAG, KIND_MASKED)))


def _sweep_steps(i):
    nfar = jnp.maximum(i - 1 - NSUB, 0) // NSUB_FAR
    near0 = NSUB + nfar * NSUB_FAR
    return nfar, near0, (i - near0 + NSUB) // NSUB


def _ffn_kernel(*refs, fuse_out, layer, fc, nj):
    if fuse_out:
        h_ref, mix_ref, wout_ref, g_ref, wi_hbm, wo_hbm, o_ref, xn_ref, wa_buf, wb_buf, wo_buf, sem = refs
    else:
        h_ref, g_ref, wi_hbm, wo_hbm, o_ref, xn_ref, wa_buf, wb_buf, wo_buf, sem = refs

    def chunk_copies(j, slot):
        lo = pl.multiple_of(j * fc, fc)
        hi = pl.multiple_of((nj + j) * fc, fc)
        return (pltpu.make_async_copy(wi_hbm.at[layer, :, pl.ds(lo, fc)], wa_buf.at[slot], sem.at[0, slot]),
                pltpu.make_async_copy(wi_hbm.at[layer, :, pl.ds(hi, fc)], wb_buf.at[slot], sem.at[1, slot]),
                pltpu.make_async_copy(wo_hbm.at[layer, pl.ds(lo, fc), :], wo_buf.at[slot], sem.at[2, slot]))

    i = pl.program_id(0)
    first = i * nj

    @pl.when(i == 0)
    def _():
        for c in chunk_copies(0, 0):
            c.start()

    r = h_ref[...]
    if fuse_out:
        r = r + _dot(mix_ref[...], wout_ref[...])
    o_ref[...] = r
    xn_ref[...] = (_rms_rows(r) * g_ref[...]).astype(bf16)

    def body(j, carry):
        slot = (first + j) & 1

        @pl.when((j + 1 < nj) | (i + 1 < pl.num_programs(0)))
        def _():
            for c in chunk_copies(jnp.where(j + 1 < nj, j + 1, 0), 1 - slot):
                c.start()

        for c in chunk_copies(j, slot):
            c.wait()
        xn = xn_ref[...]
        a = _dot(xn, wa_buf[slot].astype(bf16))
        b = _dot(xn, wb_buf[slot].astype(bf16))
        act = (a / (1.0 + jnp.exp(-a)) * b).astype(bf16)
        o_ref[...] += 0.5 * _dot(act, wo_buf[slot].astype(bf16))
        return carry

    lax.fori_loop(0, nj, body, 0)


def _ffn(h, g, wi, wo, layer, *, tm, fc, mix=None, wout=None, n_rows=None):
    d = h.shape[1]
    n = h.shape[0] if n_rows is None else n_rows
    assert n % tm == 0
    dff = wo.shape[1]
    nj = dff // fc
    fuse = mix is not None
    row = lambda i: (i, 0)
    in_specs = [pl.BlockSpec((tm, d), row)]
    args = [h]
    if fuse:
        in_specs += [pl.BlockSpec((tm, mix.shape[1]), row), pl.BlockSpec(wout.shape, lambda i: (0, 0))]
        args += [mix, wout]
    in_specs += [
        pl.BlockSpec((1, d), lambda i: (0, 0)),
        pl.BlockSpec(memory_space=pl.ANY),
        pl.BlockSpec(memory_space=pl.ANY),
    ]
    args += [g.reshape(1, d), wi, wo]
    return pl.pallas_call(
        functools.partial(_ffn_kernel, fuse_out=fuse, layer=layer, fc=fc, nj=nj),
        grid=(n // tm,),
        in_specs=in_specs,
        out_specs=pl.BlockSpec((tm, d), row),
        out_shape=jax.ShapeDtypeStruct((n, d), f32),
        scratch_shapes=[
            pltpu.VMEM((tm, d), bf16),
            pltpu.VMEM((2, d, fc), f32),
            pltpu.VMEM((2, d, fc), f32),
            pltpu.VMEM((2, fc, d), f32),
            pltpu.SemaphoreType.DMA((3, 2)),
        ],
        compiler_params=_cparams(("arbitrary",)),
        name="ffn_out" if fuse else "ffn",
    )(*args)


def _rel_bucket(rel):
    half = REL_BUCKETS // 2
    max_exact = half // 2
    n = jnp.abs(rel)
    large = max_exact + (jnp.log(jnp.maximum(n, 1).astype(jnp.float32) / max_exact)
                         / math.log(REL_MAX_DIST / max_exact) * (half - max_exact)).astype(jnp.int32)
    large = jnp.minimum(large, half - 1)
    return jnp.where(rel > 0, half, 0) + jnp.where(n < max_exact, n, large)


def _rel_tiles():
    k = np.arange(LANES)[:, None]
    q = np.arange(LANES)[None, :]
    far = np.full((LANES, LANES), -4 * LANES)
    ones = np.ones((LANES, LANES), bool)
    rels = [k - q, k - q - LANES, far, (k % N_META) - N_META - q, (k % N_META) - (q % N_META), far, k - q - LANES]
    vis = [(k // CHUNK) <= (q // CHUNK), ones, ones, ones, (k // N_META) == (q // N_META), ~ones,
           (q < CHUNK) | (k >= CHUNK)]
    return (np.stack([np.broadcast_to(a, (LANES, LANES)) for a in rels]).astype(np.int32),
            np.stack([np.broadcast_to(a, (LANES, LANES)) for a in vis]).astype(np.int32))


def _bias_kernel(rb_ref, bucket_ref, vis_ref, o_ref):
    h = pl.program_id(0)
    for kind in range(N_KINDS):
        bk = bucket_ref[kind]
        acc = jnp.zeros((LANES, LANES), f32)
        for b in range(REL_BUCKETS):
            acc = jnp.where(bk == b, rb_ref[b, h], acc)
        o_ref[kind, 0] = jnp.where(vis_ref[kind] != 0, acc * LOG2E, NEG_INF)


def _bias_tiles(rel_bias):
    rel, vis = _rel_tiles()
    bucket = _rel_bucket(jnp.asarray(rel))
    nk = N_KINDS
    return pl.pallas_call(
        _bias_kernel,
        grid=(REL_HEADS,),
        in_specs=[
            pl.BlockSpec(memory_space=pltpu.SMEM),
            pl.BlockSpec((nk, LANES, LANES), lambda h: (0, 0, 0)),
            pl.BlockSpec((nk, LANES, LANES), lambda h: (0, 0, 0)),
        ],
        out_specs=pl.BlockSpec((nk, 1, LANES, LANES), lambda h: (0, h, 0, 0)),
        out_shape=jax.ShapeDtypeStruct((nk, REL_HEADS, LANES, LANES), f32),
        compiler_params=_cparams(("arbitrary",)),
        name="bias_tiles",
    )(rel_bias, bucket, jnp.asarray(vis))


def _bias_spec():
    return pl.BlockSpec((N_KINDS, REL_HEADS, LANES, LANES), lambda s: (0, 0, 0, 0))


def _meta_views(rows, cols_t, bsz):
    f = rows.shape[1]
    return (rows.reshape(bsz, N_META, f),
            cols_t.reshape(cols_t.shape[0], bsz, N_META).transpose(1, 0, 2))


A_QD = A_HEADS * 2 * A_HD
A_VR = A_VD + ONES_ROWS


def _proj_a_kernel(h_ref, g_ref, w_ref, gq_ref, gk_ref, qs_ref, k_ref, vt_ref):
    xn = (_rms_rows(h_ref[...]) * g_ref[...]).astype(bf16)
    y = _dot(xn, w_ref[...])
    q = _group_rms(y[:, :A_QD], A_HD) * gq_ref[...]
    lo = _lo_half_mask(q.shape, 2 * A_HD, A_HD)
    q_lo = jnp.where(lo, q, 0.0).astype(bf16)
    q_hi = jnp.where(lo, 0.0, q).astype(bf16)
    k_ref[...] = (_group_rms(y[:, A_QD:2 * A_QD], A_HD) * gk_ref[...]).astype(bf16)
    ones = jnp.ones((ONES_ROWS, LANES), bf16)
    for t in range(vt_ref.shape[0]):
        rows = slice(t * LANES, (t + 1) * LANES)
        vt = y[rows, 2 * A_QD:].T.astype(bf16)
        for h in range(A_HEADS):
            qs_ref[t, h, :LANES, :] = q_lo[rows, h * A_VD:(h + 1) * A_VD]
            qs_ref[t, h, LANES:, :] = q_hi[rows, h * A_VD:(h + 1) * A_VD]
            vt_ref[t, h * A_VR:h * A_VR + A_VD, :] = vt[h * A_VD:(h + 1) * A_VD]
            vt_ref[t, h * A_VR + A_VD:(h + 1) * A_VR, :] = ones


def _proj_a(h, g, w, qk_norm, *, tm):
    n, d = h.shape
    nw = w.shape[1]
    gq = jnp.tile(qk_norm[0] * (A_HD ** -0.5 * LOG2E), A_QD // A_HD).reshape(1, A_QD)
    gk = jnp.tile(qk_norm[1], A_QD // A_HD).reshape(1, A_QD)
    return pl.pallas_call(
        _proj_a_kernel,
        grid=(n // tm,),
        in_specs=[
            pl.BlockSpec((tm, d), lambda i: (i, 0)),
            pl.BlockSpec((1, d), lambda i: (0, 0)),
            pl.BlockSpec((d, nw), lambda i: (0, 0)),
            pl.BlockSpec((1, A_QD), lambda i: (0, 0)),
            pl.BlockSpec((1, A_QD), lambda i: (0, 0)),
        ],
        out_specs=[
            pl.BlockSpec((tm // LANES, A_HEADS, 2 * LANES, A_VD), lambda i: (i, 0, 0, 0)),
            pl.BlockSpec((tm, A_QD), lambda i: (i, 0)),
            pl.BlockSpec((tm // LANES, A_HEADS * A_VR, LANES), lambda i: (i, 0, 0)),
        ],
        out_shape=[
            jax.ShapeDtypeStruct((n // LANES, A_HEADS, 2 * LANES, A_VD), bf16),
            jax.ShapeDtypeStruct((n, A_QD), bf16),
            jax.ShapeDtypeStruct((n // LANES, A_HEADS * A_VR, LANES), bf16),
        ],
        compiler_params=_cparams(("arbitrary",)),
        name="proj_a",
    )(h, g.reshape(1, d), w, gq, gk)


def _attn_a_kernel(qs_ref, k_ref, vt_ref, km_ref, vtm_ref, kmf_ref, vtmf_ref, bias_ref, lam_ref, sub_ref,
                   o_ref, m_ref, acc_ref, gap_ref, *, nblk, lambda_init):
    s_id = pl.program_id(0)
    nb_real = pl.num_programs(0) - 1
    hw = 2 * A_HD

    def init():
        m_ref[...] = jnp.full(m_ref.shape, NEG_INF, f32)
        acc_ref[...] = jnp.zeros(acc_ref.shape, f32)

    def step_fns(kt_fn, vtt_fns, bias_fn):
        def logits():
            sts = [_dot_nt(kt_fn(h), qs_ref[0, h]) for h in range(A_HEADS)]
            if bias_fn is not None:
                sts = [st + bias_fn(h) for h, st in enumerate(sts)]
            return jnp.concatenate(sts, axis=1)

        def pv(pb):
            outs = []
            for h in range(A_HEADS):
                acc = None
                for vtt_fn, r0 in vtt_fns:
                    vtt = vtt_fn(h)
                    part = _dot(vtt, pb[r0:r0 + vtt.shape[1], h * 2 * LANES:(h + 1) * 2 * LANES])
                    acc = part if acc is None else acc + part
                outs.append(acc)
            return jnp.concatenate(outs, axis=1)

        return logits, pv

    def bias_rows(kind, h, rows):
        return jnp.concatenate([bias_ref[kind, h, 0:rows, :], bias_ref[kind, A_HEADS + h, 0:rows, :]], axis=1)

    def finalize():
        lam = lam_ref[...]
        lam_full = (jnp.exp(jnp.sum(lam[0:1] * lam[1:2], axis=-1, keepdims=True))
                    - jnp.exp(jnp.sum(lam[2:3] * lam[3:4], axis=-1, keepdims=True)) + lambda_init)
        o = acc_ref[0:A_VD, :] * (1.0 / acc_ref[A_VD:A_VD + 1, :])
        for h in range(A_HEADS):
            d = o[:, 2 * h * LANES:(2 * h + 1) * LANES] - lam_full * o[:, (2 * h + 1) * LANES:(2 * h + 2) * LANES]
            d = d * lax.rsqrt(jnp.mean(d * d, axis=0, keepdims=True) + EPS) * sub_ref[...] * (1.0 - lambda_init)
            o_ref[:, h * hw:(h + 1) * hw] = d.T.astype(bf16)

    @pl.when(s_id < nb_real)
    def _():
        i = s_id % nblk
        kind_m = jnp.where(i == 0, KIND_META0, KIND_FAR)
        nfar, near0, nnear = _sweep_steps(i)

        def k_fn(b0, nsub):
            off = pl.multiple_of(b0 * LANES, NSUB * LANES)
            return lambda h: k_ref[pl.ds(off, nsub * LANES), h * hw:(h + 1) * hw]

        def vt_fn(b0, nsub):
            return lambda h: jnp.concatenate([vt_ref[b0 + t, h * A_VR:(h + 1) * A_VR, :] for t in range(nsub)], axis=1)

        def near_bias(b0):
            kinds = [_block_kind(b0 + t - i) for t in range(NSUB)]
            return [lambda h, kind=kind: bias_rows(kind, h, LANES) for kind in kinds]

        def first_fns():
            biases = [lambda h: bias_rows(kind_m, h, N_META)] + near_bias(0)
            return step_fns(
                lambda h: jnp.concatenate([km_ref[0, :, h * hw:(h + 1) * hw], k_fn(0, NSUB)(h)], axis=0),
                [(lambda h: vtm_ref[0, h * A_VR:(h + 1) * A_VR, :], 0), (vt_fn(0, NSUB), N_META)],
                lambda h: jnp.concatenate([b(h) for b in biases], axis=0))

        def far_step(w):
            b0 = NSUB + w * NSUB_FAR
            far_bias = jnp.concatenate([bias_rows(KIND_FAR, h, 1) for h in range(A_HEADS)], axis=1)
            return step_fns(k_fn(b0, NSUB_FAR), [(vt_fn(b0, NSUB_FAR), 0)], None) + (far_bias,)

        def near_step(u):
            b0 = near0 + u * NSUB
            biases = near_bias(b0)
            return step_fns(k_fn(b0, NSUB), [(vt_fn(b0, NSUB), 0)],
                            lambda h: jnp.concatenate([b(h) for b in biases], axis=0)) + (None,)

        def sweep(gap_ref):
            init()
            _softmax_step(*first_fns(), m_ref, acc_ref)
            _softmax_loop(0, nfar, far_step, m_ref, acc_ref, gap_ref)
            _softmax_loop(0, nnear, near_step, m_ref, acc_ref, gap_ref)

        gap_ref[...] = jnp.full(gap_ref.shape, NEG_INF, f32)
        sweep(gap_ref)
        pl.when(_overshot(gap_ref))(lambda: sweep(None))
        finalize()

    @pl.when(s_id == nb_real)
    def _():
        init()
        _softmax_step(*step_fns(lambda h: kmf_ref[:, h * hw:(h + 1) * hw],
                                [(lambda h: vtmf_ref[0, h * A_VR:(h + 1) * A_VR, :], 0)],
                                lambda h: bias_rows(KIND_METAMETA, h, LANES)), m_ref, acc_ref)
        finalize()


def _attn_a(qs, k, vt, bias, lam, subln, *, bsz, nblk, lambda_init):
    n = k.shape[0]
    nb_real = bsz * nblk
    seq = nblk * LANES
    d = A_QD
    vr = A_HEADS * A_VR
    assert nblk % NSUB == 0
    km, vtm = _meta_views(k[nb_real * LANES:], vt[nb_real], bsz)
    kern = functools.partial(_attn_a_kernel, nblk=nblk, lambda_init=lambda_init)
    bclamp = lambda s: jnp.minimum(s // nblk, bsz - 1)
    return pl.pallas_call(
        kern,
        grid=(nb_real + 1,),
        in_specs=[
            pl.BlockSpec((1,) + qs.shape[1:], lambda s: (s, 0, 0, 0)),
            pl.BlockSpec((seq, d), lambda s: (bclamp(s), 0)),
            pl.BlockSpec((nblk, vr, LANES), lambda s: (bclamp(s), 0, 0)),
            pl.BlockSpec((1, N_META, d), lambda s: (bclamp(s), 0, 0)),
            pl.BlockSpec((1, vr, N_META), lambda s: (bclamp(s), 0, 0)),
            pl.BlockSpec((LANES, d), lambda s: (nb_real, 0)),
            pl.BlockSpec((1, vr, LANES), lambda s: (nb_real, 0, 0)),
            _bias_spec(),
            pl.BlockSpec((4, A_HD), lambda s: (0, 0)),
            pl.BlockSpec((A_VD, LANES), lambda s: (0, 0)),
        ],
        out_specs=pl.BlockSpec((LANES, d), lambda s: (s, 0)),
        out_shape=jax.ShapeDtypeStruct((n, d), bf16),
        scratch_shapes=[
            pltpu.VMEM((1, A_HEADS * 2 * LANES), f32),
            pltpu.VMEM((A_VR, A_HEADS * 2 * LANES), f32),
            pltpu.VMEM((1, A_HEADS * 2 * LANES), f32),
        ],
        compiler_params=_cparams(("arbitrary",)),
        name="attn_a",
    )(qs, k, vt, km, vtm, k, vt, bias, lam, jnp.broadcast_to(subln[:, None], (A_VD, LANES)))


C_QD = C_Q_HEADS * C_HD
C_KD = 2 * C_KV_HEADS * C_HD
C_VR = 2 * C_HD + ONES_ROWS


def _proj_c_kernel(h_ref, g_ref, w_ref, gq_ref, gk_ref, qs_ref, k_ref, vt_ref):
    xn = (_rms_rows(h_ref[...]) * g_ref[...]).astype(bf16)
    y = _dot(xn, w_ref[...])
    q = _group_rms(y[:, :C_QD], C_HD) * gq_ref[...]
    lo = _lo_half_mask(q.shape, 2 * C_HD, C_HD)
    q_even = jnp.where(lo, q, 0.0).astype(bf16)
    q_odd = jnp.where(lo, 0.0, q).astype(bf16)
    k_ref[...] = (_group_rms(y[:, C_QD:C_QD + C_KD], C_HD) * gk_ref[...]).astype(bf16)
    ones = jnp.ones((ONES_ROWS, LANES), bf16)
    for t in range(vt_ref.shape[0]):
        rows = slice(t * LANES, (t + 1) * LANES)
        vt = y[rows, C_QD + C_KD:].T.astype(bf16)
        for g in range(C_KV_HEADS):
            for hh in range(C_GROUP):
                pair = (g * C_GROUP + hh) // 2
                src = q_even if hh % 2 == 0 else q_odd
                qs_ref[t, g, hh * LANES:(hh + 1) * LANES, :] = src[rows, pair * LANES:(pair + 1) * LANES]
            vt_ref[t, g * C_VR:g * C_VR + 2 * C_HD, :] = vt[g * 2 * C_HD:(g + 1) * 2 * C_HD]
            vt_ref[t, g * C_VR + 2 * C_HD:(g + 1) * C_VR, :] = ones


def _proj_c(h, g, w, qk_norm, *, tm):
    n, d = h.shape
    nw = w.shape[1]
    nt = tm // LANES
    gq = jnp.tile(qk_norm[0] * (C_HD ** -0.5 * LOG2E), C_QD // C_HD).reshape(1, C_QD)
    gk = jnp.tile(qk_norm[1], C_KD // C_HD).reshape(1, C_KD)
    return pl.pallas_call(
        _proj_c_kernel,
        grid=(n // tm,),
        in_specs=[
            pl.BlockSpec((tm, d), lambda i: (i, 0)),
            pl.BlockSpec((1, d), lambda i: (0, 0)),
            pl.BlockSpec((d, nw), lambda i: (0, 0)),
            pl.BlockSpec((1, C_QD), lambda i: (0, 0)),
            pl.BlockSpec((1, C_KD), lambda i: (0, 0)),
        ],
        out_specs=[
            pl.BlockSpec((nt, C_KV_HEADS, C_GROUP * LANES, LANES), lambda i: (i, 0, 0, 0)),
            pl.BlockSpec((tm, C_KD), lambda i: (i, 0)),
            pl.BlockSpec((nt, C_KV_HEADS * C_VR, LANES), lambda i: (i, 0, 0)),
        ],
        out_shape=[
            jax.ShapeDtypeStruct((n // LANES, C_KV_HEADS, C_GROUP * LANES, LANES), bf16),
            jax.ShapeDtypeStruct((n, C_KD), bf16),
            jax.ShapeDtypeStruct((n // LANES, C_KV_HEADS * C_VR, LANES), bf16),
        ],
        compiler_params=_cparams(("arbitrary",)),
        name="proj_c",
    )(h, g.reshape(1, d), w, gq, gk)


def _attn_c_kernel(sink_ref, qs_ref, k_ref, vt_ref, km_ref, vtm_ref, kmf_ref, vtmf_ref, bias_ref, o_ref, *, nblk):
    s_id = pl.program_id(0)
    nb_real = pl.num_programs(0) - 1
    r_io, _ = _tile_iotas()
    vd = 2 * C_HD

    def attend(tiles):
        top = r_io < C_HD
        for g in range(C_KV_HEADS):
            sink = jnp.concatenate(
                [jnp.full((1, LANES), sink_ref[g * C_GROUP + hh] * LOG2E, f32) for hh in range(C_GROUP)], axis=1)
            sts = []
            m = sink
            for (k_fn, vt_fn, bias_fn) in tiles:
                st = _dot_nt(k_fn(g), qs_ref[0, g])
                st = st + jnp.concatenate([bias_fn(g * C_GROUP + hh) for hh in range(C_GROUP)], axis=1)
                m = jnp.maximum(m, jnp.max(st, axis=0, keepdims=True))
                sts.append(st)
            acc = None
            for st, (k_fn, vt_fn, bias_fn) in zip(sts, tiles):
                pv = _dot(vt_fn(g), jnp.exp2(st - m).astype(bf16))
                acc = pv if acc is None else acc + pv
            o = acc[0:vd, :] * (1.0 / (acc[vd:vd + 1, :] + jnp.exp2(sink - m)))
            for cc in range(C_GROUP // 2):
                even = o[:, (2 * cc) * LANES:(2 * cc + 1) * LANES]
                odd = o[:, (2 * cc + 1) * LANES:(2 * cc + 2) * LANES]
                col = (g * (C_GROUP // 2) + cc) * LANES
                o_ref[:, col:col + LANES] = jnp.where(top, even, odd).T.astype(bf16)

    @pl.when(s_id < nb_real)
    def _():
        i = s_id % nblk
        prev = jnp.maximum(i - 1, 0)
        poff = pl.multiple_of(prev * LANES, LANES)
        coff = pl.multiple_of(i * LANES, LANES)
        kind_m = jnp.where(i == 0, KIND_META0, KIND_FAR)
        kind_p = jnp.where(i == 0, KIND_MASKED, KIND_PREVWIN)
        attend([
            (lambda g: km_ref[0, :, g * LANES:(g + 1) * LANES], lambda g: vtm_ref[0, g * C_VR:(g + 1) * C_VR, :],
             lambda h: bias_ref[kind_m, h, 0:N_META, :]),
            (lambda g: k_ref[pl.ds(poff, LANES), g * LANES:(g + 1) * LANES],
             lambda g: vt_ref[prev, g * C_VR:(g + 1) * C_VR, :], lambda h: bias_ref[kind_p, h]),
            (lambda g: k_ref[pl.ds(coff, LANES), g * LANES:(g + 1) * LANES],
             lambda g: vt_ref[i, g * C_VR:(g + 1) * C_VR, :], lambda h: bias_ref[KIND_DIAG, h]),
        ])

    @pl.when(s_id == nb_real)
    def _():
        attend([(lambda g: kmf_ref[:, g * LANES:(g + 1) * LANES], lambda g: vtmf_ref[0, g * C_VR:(g + 1) * C_VR, :],
                 lambda h: bias_ref[KIND_METAMETA, h])])


def _attn_c(qs, k, vt, bias, sinks, *, bsz, nblk):
    n = k.shape[0]
    nb_real = bsz * nblk
    seq = nblk * LANES
    vr = C_KV_HEADS * C_VR
    kern = functools.partial(_attn_c_kernel, nblk=nblk)
    bclamp = lambda s: jnp.minimum(s // nblk, bsz - 1)
    km, vtm = _meta_views(k[nb_real * LANES:], vt[nb_real], bsz)
    return pl.pallas_call(
        kern,
        grid=(nb_real + 1,),
        in_specs=[
            pl.BlockSpec(memory_space=pltpu.SMEM),
            pl.BlockSpec((1,) + qs.shape[1:], lambda s: (s, 0, 0, 0)),
            pl.BlockSpec((seq, C_KD), lambda s: (bclamp(s), 0)),
            pl.BlockSpec((nblk, vr, LANES), lambda s: (bclamp(s), 0, 0)),
            pl.BlockSpec((1, N_META, C_KD), lambda s: (bclamp(s), 0, 0)),
            pl.BlockSpec((1, vr, N_META), lambda s: (bclamp(s), 0, 0)),
            pl.BlockSpec((LANES, C_KD), lambda s: (nb_real, 0)),
            pl.BlockSpec((1, vr, LANES), lambda s: (nb_real, 0, 0)),
            _bias_spec(),
        ],
        out_specs=pl.BlockSpec((LANES, C_QD), lambda s: (s, 0)),
        out_shape=jax.ShapeDtypeStruct((n, C_QD), bf16),
        compiler_params=_cparams(("arbitrary",)),
        name="attn_c",
    )(sinks, qs, k, vt, km, vtm, k, vt, bias)


B_QA = B_HEADS * B_KV_RANK
B_QI = IDX_HEADS * IDX_DIM
B_W1 = 2 * B_Q_RANK + 2 * LANES
B_TR = B_KV_RANK + ONES_ROWS


def _proj_b_kernel(h_ref, g_ref, w1_ref, ln_ref, wuq_ref, qn_ref,
                   qa_ref, qi_ref, ckv_ref, ckvt_ref, kk_ref, wit_ref):
    xn = (_rms_rows(h_ref[...]) * g_ref[...]).astype(bf16)
    y = _dot(xn, w1_ref[...])
    r = B_Q_RANK
    cq = (_rms_rows(y[:, :r]) * ln_ref[0:1, :]).astype(bf16)
    ckv = _rms_rows(y[:, r:2 * r]) * ln_ref[1:2, :]
    ckv_ref[...] = ckv.astype(bf16)
    kk_ref[...] = _rms_rows(y[:, 2 * r:2 * r + LANES]).astype(bf16)
    wi = y[:, 2 * r + LANES:] * (IDX_HEADS ** -0.5)
    ones = jnp.ones((ONES_ROWS, LANES), bf16)
    for t in range(ckvt_ref.shape[0]):
        ckvt_ref[t, 0:r, :] = ckv[t * LANES:(t + 1) * LANES, :].T.astype(bf16)
        ckvt_ref[t, r:, :] = ones
        wit_ref[t] = wi[t * LANES:(t + 1) * LANES, :].T[0:IDX_HEADS, :]
    z = _dot(cq, wuq_ref[...])
    qa = (_group_rms(z[:, :B_QA], B_KV_RANK) * qn_ref[...]).astype(bf16)
    qi = z[:, B_QA:] * (IDX_DIM ** -0.5)
    lo = _lo_half_mask(qi.shape, 2 * IDX_DIM, IDX_DIM)
    qi_lo = jnp.where(lo, qi, 0.0).astype(bf16)
    qi_hi = jnp.where(lo, 0.0, qi).astype(bf16)
    for t in range(qa_ref.shape[0]):
        rows = slice(t * LANES, (t + 1) * LANES)
        for hd in range(B_HEADS):
            qa_ref[t, hd * LANES:(hd + 1) * LANES, :] = qa[rows, hd * r:(hd + 1) * r]
        for hh in range(IDX_HEADS):
            src = qi_lo if hh % 2 == 0 else qi_hi
            qi_ref[t, hh * LANES:(hh + 1) * LANES, :] = src[rows, (hh // 2) * LANES:(hh // 2 + 1) * LANES]


def _proj_b(h, g, w1, latent_norm, wuq, q_norm, *, tm):
    n, d = h.shape
    qn = jnp.tile(q_norm * (B_KV_RANK ** -0.5 * LOG2E), B_HEADS).reshape(1, B_QA)
    row = lambda i: (i, 0)
    row3 = lambda i: (i, 0, 0)
    const = lambda i: (0, 0)
    nt = tm // LANES
    return pl.pallas_call(
        _proj_b_kernel,
        grid=(n // tm,),
        in_specs=[
            pl.BlockSpec((tm, d), row),
            pl.BlockSpec((1, d), const),
            pl.BlockSpec(w1.shape, const),
            pl.BlockSpec(latent_norm.shape, const),
            pl.BlockSpec(wuq.shape, const),
            pl.BlockSpec((1, B_QA), const),
        ],
        out_specs=[
            pl.BlockSpec((nt, B_HEADS * LANES, B_KV_RANK), row3),
            pl.BlockSpec((nt, IDX_HEADS * LANES, LANES), row3),
            pl.BlockSpec((tm, B_KV_RANK), row),
            pl.BlockSpec((nt, B_TR, LANES), row3),
            pl.BlockSpec((tm, LANES), row),
            pl.BlockSpec((nt, IDX_HEADS, LANES), row3),
        ],
        out_shape=[
            jax.ShapeDtypeStruct((n // LANES, B_HEADS * LANES, B_KV_RANK), bf16),
            jax.ShapeDtypeStruct((n // LANES, IDX_HEADS * LANES, LANES), bf16),
            jax.ShapeDtypeStruct((n, B_KV_RANK), bf16),
            jax.ShapeDtypeStruct((n // LANES, B_TR, LANES), bf16),
            jax.ShapeDtypeStruct((n, LANES), bf16),
            jax.ShapeDtypeStruct((n // LANES, IDX_HEADS, LANES), f32),
        ],
        compiler_params=_cparams(("arbitrary",)),
        name="proj_b",
    )(h, g.reshape(1, d), w1, latent_norm, wuq, qn)


def _attn_b_kernel(qs_ref, is_ref, wit_ref, ckv_ref, ckvt_ref, kk_ref, ckvm_ref, ckvtm_ref, kkm_ref,
                   ckvmf_ref, ckvtmf_ref, bias_ref, wuvt_ref,
                   o_ref, key_ref, pen_ref, m_ref, acc_ref, gap_ref, *, nblk, k_sel):
    s_id = pl.program_id(0)
    nb_real = pl.num_programs(0) - 1
    r_io, c_io = _tile_iotas()
    rk = B_KV_RANK

    def init():
        m_ref[...] = jnp.full(m_ref.shape, NEG_INF, f32)
        acc_ref[...] = jnp.zeros(acc_ref.shape, f32)

    def add_per_head(st, bias_fn, pen):
        cols = []
        for h in range(B_HEADS):
            add = pen if bias_fn is None else (bias_fn(h) if pen is None else bias_fn(h) + pen)
            cols.append(st[:, h * LANES:(h + 1) * LANES] + add)
        return jnp.concatenate(cols, axis=1)

    def step_fns(ckv_fn, ckvt_fns, bias_fn, pen_fn):
        def logits():
            st = _dot_nt(ckv_fn(), qs_ref[0])
            return add_per_head(st, bias_fn, None if pen_fn is None else pen_fn())

        def pv(pb):
            acc = None
            for ckvt_fn, r0 in ckvt_fns:
                ckvt = ckvt_fn()
                part = _dot(ckvt, pb[r0:r0 + ckvt.shape[1], :])
                acc = part if acc is None else acc + part
            return acc

        return logits, pv

    def finalize():
        olat = (acc_ref[0:rk, :] * (1.0 / acc_ref[rk:rk + 1, :])).astype(bf16)
        ot = jnp.concatenate([_dot(wuvt_ref[h], olat[:, h * LANES:(h + 1) * LANES]) for h in range(B_HEADS)], axis=0)
        o_ref[...] = ot.T.astype(bf16)

    def index_scores(kk):
        s = jnp.maximum(_dot_nt(kk, is_ref[0]), 0.0)
        wt = wit_ref[0]
        sc = jnp.zeros((kk.shape[0], LANES), f32)
        for hh in range(IDX_HEADS):
            sc = sc + wt[hh:hh + 1, :] * s[:, hh * LANES:(hh + 1) * LANES]
        return sc

    def sort_key(sc):
        bits = lax.bitcast_convert_type(sc + 0.0, jnp.int32)
        return jnp.where(bits < 0, bits ^ jnp.int32(0x7FFFFFFF), bits)

    @pl.when(s_id < nb_real)
    def _():
        i = s_id % nblk
        ntile = i + 2

        int_min_tile = jnp.full((LANES, LANES), INT_MIN, jnp.int32)
        key_ref[0] = int_min_tile
        key_ref[0, 0:N_META, :] = sort_key(index_scores(kkm_ref[0]))
        key_ref[i + 2] = int_min_tile

        def score_body(jp, carry):
            off = pl.multiple_of(jp * (2 * LANES), 2 * LANES)
            keys = sort_key(index_scores(kk_ref[pl.ds(off, 2 * LANES), :]))
            for t in range(2):
                j = _vec(2 * jp + t)
                vis = (j < i) | ((j == i) & ((r_io >> 6) <= (c_io >> 6)))
                key_ref[2 * jp + t + 1] = jnp.where(vis, keys[t * LANES:(t + 1) * LANES], jnp.int32(INT_MIN))
            return carry

        lax.fori_loop(0, i // 2 + 1, score_body, 0)

        def count(pred):
            def cbody(tp, accv):
                for t in (2 * tp, 2 * tp + 1):
                    accv = accv + jnp.where(pred(key_ref[t], t), 1.0, 0.0)
                return accv
            accv = lax.fori_loop(0, (ntile + 1) // 2, cbody, jnp.zeros((LANES, LANES), f32))
            return jnp.sum(accv, axis=0, keepdims=True)

        kf = float(k_sel)
        zero = jnp.zeros((1, LANES), jnp.int32)
        t0 = jnp.where(count(lambda k, t: k >= zero) >= kf, zero, jnp.int32(INT_MIN))

        def bit_body(it, tcur):
            cand = tcur | jnp.left_shift(jnp.int32(1), 30 - it)
            return jnp.where(count(lambda k, t: k >= cand) >= kf, cand, tcur)

        thr = lax.fori_loop(0, 31, bit_body, t0)

        need = kf - count(lambda k, t: k > thr)
        n_eq = count(lambda k, t: k == thr)
        has_thr = thr > jnp.int32(INT_MIN)
        tied = jnp.max(jnp.where(has_thr & (n_eq > need), 1.0, 0.0)) > 0.0

        def tie_search(_):
            def jbody(it, jcur):
                cand = jcur | jnp.left_shift(jnp.int32(1), 11 - it)
                cnt = count(lambda k, t: (k == thr) & ((t * LANES + r_io) < cand))
                return jnp.where(cnt < need, cand, jcur)
            return lax.fori_loop(0, 12, jbody, jnp.zeros((1, LANES), jnp.int32))

        j_last = lax.cond(tied, tie_search, lambda _: jnp.full((1, LANES), 4095, jnp.int32), 0)
        j_last = jnp.where(has_thr, j_last, -1)

        def pen_body(t, carry):
            k = key_ref[t]
            sel = (k > thr) | ((k == thr) & ((t * LANES + r_io) <= j_last))
            pen_ref[t] = jnp.where(sel, 0.0, NEG_INF)
            return carry

        lax.fori_loop(0, ntile, pen_body, 0)
        for t in range(1, NSUB):
            pen_ref[i + 1 + t] = jnp.full((LANES, LANES), NEG_INF, f32)

        kind_m = jnp.where(i == 0, KIND_META0, KIND_FAR)
        nfar, near0, nnear = _sweep_steps(i)

        def ckv_fn(b0, nsub):
            off = pl.multiple_of(b0 * LANES, NSUB * LANES)
            return lambda: ckv_ref[pl.ds(off, nsub * LANES), :]

        def ckvt_fn(b0, nsub):
            return lambda: jnp.concatenate([ckvt_ref[b0 + t] for t in range(nsub)], axis=1)

        def pen_fn(b0, nsub):
            return lambda: jnp.concatenate([pen_ref[b0 + t + 1] for t in range(nsub)], axis=0)

        def near_bias(b0):
            kinds = [_block_kind(b0 + t - i) for t in range(NSUB)]
            return lambda h: jnp.concatenate([bias_ref[kinds[t], h] for t in range(NSUB)], axis=0)

        def first_pen():
            return jnp.concatenate([pen_ref[0, 0:N_META, :], pen_fn(0, NSUB)()], axis=0)

        def first_fns(with_pen):
            bias01 = near_bias(0)
            return step_fns(lambda: jnp.concatenate([ckvm_ref[0], ckv_fn(0, NSUB)()], axis=0),
                            [(lambda: ckvtm_ref[0], 0), (ckvt_fn(0, NSUB), N_META)],
                            lambda h: jnp.concatenate([bias_ref[kind_m, h, 0:N_META, :], bias01(h)], axis=0),
                            first_pen if with_pen else None)

        def far_step(w):
            b0 = NSUB + w * NSUB_FAR
            far_bias = jnp.concatenate([bias_ref[KIND_FAR, h, 0:1, :] for h in range(B_HEADS)], axis=1)
            return step_fns(ckv_fn(b0, NSUB_FAR), [(ckvt_fn(b0, NSUB_FAR), 0)], None, pen_fn(b0, NSUB_FAR)) + (far_bias,)

        def near_step(u):
            b0 = near0 + u * NSUB
            return step_fns(ckv_fn(b0, NSUB), [(ckvt_fn(b0, NSUB), 0)], near_bias(b0), pen_fn(b0, NSUB)) + (None,)

        def sweep(gap_ref):
            init()
            if gap_ref is None:
                _softmax_step(*first_fns(True), m_ref, acc_ref)
            else:
                _softmax_step(*first_fns(False), m_ref, acc_ref,
                              post_fn=lambda st: add_per_head(st, None, first_pen()))
            _softmax_loop(0, nfar, far_step, m_ref, acc_ref, gap_ref)
            _softmax_loop(0, nnear, near_step, m_ref, acc_ref, gap_ref)

        gap_ref[...] = jnp.full(gap_ref.shape, NEG_INF, f32)
        sweep(gap_ref)
        faded = jnp.logical_not(jnp.min(acc_ref[rk:rk + 1, :]) >= LAZY_FLOOR)
        pl.when(_overshot(gap_ref) | faded)(lambda: sweep(None))
        finalize()

    @pl.when(s_id == nb_real)
    def _():
        init()
        _softmax_step(*step_fns(lambda: ckvmf_ref[...], [(lambda: ckvtmf_ref[0], 0)],
                                lambda h: bias_ref[KIND_METAMETA, h], None), m_ref, acc_ref)
        finalize()


def _attn_b(qa, qi, wit, ckv, ckvt, kk, bias, wuvt, *, bsz, nblk, k_sel):
    n = ckv.shape[0]
    nb_real = bsz * nblk
    seq = nblk * LANES
    assert k_sel >= N_META and (nblk + 1) * LANES <= 4096 and nblk % NSUB == 0
    ckvm, ckvtm = _meta_views(ckv[nb_real * LANES:], ckvt[nb_real], bsz)
    kkm = kk[nb_real * LANES:].reshape(bsz, N_META, LANES)
    kern = functools.partial(_attn_b_kernel, nblk=nblk, k_sel=k_sel)
    bidx = lambda s: jnp.minimum(s // nblk, bsz - 1)
    blk = lambda s: (s, 0)
    return pl.pallas_call(
        kern,
        grid=(nb_real + 1,),
        in_specs=[
            pl.BlockSpec((1,) + qa.shape[1:], lambda s: (s, 0, 0)),
            pl.BlockSpec((1,) + qi.shape[1:], lambda s: (s, 0, 0)),
            pl.BlockSpec((1, IDX_HEADS, LANES), lambda s: (s, 0, 0)),
            pl.BlockSpec((seq, B_KV_RANK), lambda s: (bidx(s), 0)),
            pl.BlockSpec((nblk, B_TR, LANES), lambda s: (bidx(s), 0, 0)),
            pl.BlockSpec((seq, LANES), lambda s: (bidx(s), 0)),
            pl.BlockSpec((1, N_META, B_KV_RANK), lambda s: (bidx(s), 0, 0)),
            pl.BlockSpec((1, B_TR, N_META), lambda s: (bidx(s), 0, 0)),
            pl.BlockSpec((1, N_META, LANES), lambda s: (bidx(s), 0, 0)),
            pl.BlockSpec((LANES, B_KV_RANK), lambda s: (nb_real, 0)),
            pl.BlockSpec((1, B_TR, LANES), lambda s: (nb_real, 0, 0)),
            _bias_spec(),
            pl.BlockSpec(wuvt.shape, lambda s: (0, 0, 0)),
        ],
        out_specs=pl.BlockSpec((LANES, B_HEADS * B_VD), blk),
        out_shape=jax.ShapeDtypeStruct((n, B_HEADS * B_VD), bf16),
        scratch_shapes=[
            pltpu.VMEM((nblk + 2, LANES, LANES), jnp.int32),
            pltpu.VMEM((nblk + NSUB, LANES, LANES), f32),
            pltpu.VMEM((1, B_HEADS * LANES), f32),
            pltpu.VMEM((B_TR, B_HEADS * LANES), f32),
            pltpu.VMEM((1, B_HEADS * LANES), f32),
        ],
        compiler_params=_cparams(("arbitrary",)),
        name="attn_b",
    )(qa, qi, wit, ckv, ckvt, kk, ckvm, ckvtm, kkm, ckv, ckvt, bias, wuvt)


def kernel(x, meta_tokens, rel_bias, ln_ffn1, ffn1_wi, ffn1_wo, ln_mix, w_out, ln_ffn2, ffn2_wi, ffn2_wo, a_w_in, a_qk_norm, a_lambda, a_subln, b_w_in, b_latent_norm, b_w_uq, b_q_norm, b_w_uv, c_w_in, c_qk_norm, c_sinks):
    bsz, seq, d = x.shape
    assert d == D_MODEL and seq % LANES == 0 and bsz * N_META == LANES
    nblk = seq // LANES
    n = bsz * seq + LANES
    k_sel = min(TOPK_MAX, seq // 4)
    tm_ffn = _row_tile(n, 1408)
    tm_last = _row_tile(bsz * seq, 1408)
    tm_proj = _row_tile(n, 384, LANES)
    fc = 256

    h = jnp.concatenate([x.reshape(bsz * seq, d),
                         jnp.broadcast_to(meta_tokens.astype(x.dtype), (bsz, N_META, d)).reshape(LANES, d)], axis=0)
    bias = _bias_tiles(rel_bias)

    for layer in range(DEPTH):
        h = _ffn(h, ln_ffn1[layer], ffn1_wi, ffn1_wo, layer, tm=tm_ffn, fc=fc)
        kind, j = layer % N_MIXERS, layer // N_MIXERS
        g = ln_mix[layer]
        if kind == 0:
            lambda_init = 0.8 - 0.6 * math.exp(-0.3 * layer)
            qs, k, vt = _proj_a(h, g, a_w_in[j].astype(bf16), a_qk_norm[j], tm=tm_proj)
            mix = _attn_a(qs, k, vt, bias, a_lambda[j], a_subln[j], bsz=bsz, nblk=nblk, lambda_init=lambda_init)
        elif kind == 1:
            w = b_w_in[j]
            r2 = B_Q_RANK + B_KV_RANK
            kcol = w[:, r2:r2 + IDX_DIM]
            w1 = jnp.concatenate([w[:, :r2], kcol, kcol, w[:, r2 + IDX_DIM:],
                                  jnp.zeros((d, LANES - IDX_HEADS), w.dtype)], axis=1).astype(bf16)
            assert w1.shape[1] == B_W1
            qa, qi, ckv, ckvt, kk, wit = _proj_b(h, g, w1, b_latent_norm[j], b_w_uq[j].astype(bf16), b_q_norm[j],
                                                 tm=tm_proj)
            wuvt = jnp.swapaxes(b_w_uv[j], 1, 2).astype(bf16)
            mix = _attn_b(qa, qi, wit, ckv, ckvt, kk, bias, wuvt, bsz=bsz, nblk=nblk, k_sel=k_sel)
        else:
            w = c_w_in[j]
            kcols = [w[:, C_QD + gi * C_HD:C_QD + (gi + 1) * C_HD] for gi in range(C_KV_HEADS)]
            voff = C_QD + C_KV_HEADS * C_HD
            vcols = [w[:, voff + gi * C_HD:voff + (gi + 1) * C_HD] for gi in range(C_KV_HEADS)]
            wc = jnp.concatenate([w[:, :C_QD]] + [kc for kc in kcols for _ in range(2)]
                                 + [vc for vc in vcols for _ in range(2)], axis=1).astype(bf16)
            qs, k, vt = _proj_c(h, g, wc, c_qk_norm[j], tm=tm_proj)
            mix = _attn_c(qs, k, vt, bias, c_sinks[j], bsz=bsz, nblk=nblk)
        last = layer == DEPTH - 1
        h = _ffn(h, ln_ffn2[layer], ffn2_wi, ffn2_wo, layer, tm=tm_last if last else tm_ffn, fc=fc,
                 mix=mix, wout=w_out[layer].astype(bf16), n_rows=bsz * seq if last else None)
    return h.reshape(bsz, seq, d)
```

```python
import functools
import math

import numpy as np
import jax
import jax.numpy as jnp
from jax import lax
from jax.experimental import pallas as pl
from jax.experimental.pallas import tpu as pltpu

D_MODEL = 1024
DEPTH = 4
CHUNK = 64
N_META = 16
N_MIXERS = 3
NEG_INF = -1e30
REL_BUCKETS = 32
REL_MAX_DIST = 128
REL_HEADS = 16
D_FF = 2816
A_HEADS = 8
A_HD = 64
A_VD = 2 * A_HD
B_HEADS = 16
B_Q_RANK = 256
B_KV_RANK = 256
B_VD = 64
IDX_HEADS = 8
IDX_DIM = 64
TOPK_MAX = 256
C_Q_HEADS = 16
C_KV_HEADS = 2
C_GROUP = C_Q_HEADS // C_KV_HEADS
C_HD = 64
EPS = 1e-6

LANES = 128
BF16_ROWS = 16
VMEM_LIMIT = 56 * 1024 * 1024
INT_MIN = -(2 ** 31)
NSUB = 2
NSUB_FAR = 4
LOG2E = math.log2(math.e)
LAZY_GAP = 57.0
LAZY_FLOOR = 2.0 ** -100
ONES_ROWS = BF16_ROWS

KIND_DIAG, KIND_PREV, KIND_FAR, KIND_META0, KIND_METAMETA, KIND_MASKED, KIND_PREVWIN = 0, 1, 2, 3, 4, 5, 6
N_KINDS = 7

f32 = jnp.float32
bf16 = jnp.bfloat16


def _cparams(sem):
    return pltpu.CompilerParams(dimension_semantics=sem, vmem_limit_bytes=VMEM_LIMIT)


def _row_tile(n, cap, mult=BF16_ROWS):
    best = None
    for t in range(mult, cap + 1, mult):
        if n % t == 0:
            best = t
    assert best is not None
    return best


def _dot(a, b):
    return jnp.dot(a, b, preferred_element_type=f32)


def _dot_nt(a, b):
    return lax.dot_general(a, b, (((1,), (1,)), ((), ())), preferred_element_type=f32)


def _rms_rows(x):
    return x * lax.rsqrt(jnp.mean(x * x, axis=-1, keepdims=True) + EPS)


def _lo_half_mask(shape, period, half):
    return (lax.broadcasted_iota(jnp.int32, shape, 1) & (period - 1)) < half


def _group_rms(x, group):
    r, c = x.shape
    outs = []
    if group == 64:
        lo = _lo_half_mask((r, LANES), LANES, 64)
        for ci in range(c // LANES):
            xc = x[:, ci * LANES:(ci + 1) * LANES]
            x2 = xc * xc
            s_lo = jnp.sum(jnp.where(lo, x2, 0.0), axis=-1, keepdims=True)
            s_hi = jnp.sum(jnp.where(lo, 0.0, x2), axis=-1, keepdims=True)
            inv = jnp.where(lo, lax.rsqrt(s_lo * (1.0 / 64) + EPS), lax.rsqrt(s_hi * (1.0 / 64) + EPS))
            outs.append(xc * inv)
    else:
        for gi in range(c // group):
            outs.append(_rms_rows(x[:, gi * group:(gi + 1) * group]))
    return outs[0] if len(outs) == 1 else jnp.concatenate(outs, axis=-1)


def _tile_iotas():
    r = lax.broadcasted_iota(jnp.int32, (LANES, LANES), 0)
    c = lax.broadcasted_iota(jnp.int32, (LANES, LANES), 1)
    return r, c


def _vec(s):
    return jnp.full((LANES, LANES), s, jnp.int32)


def _softmax_step(logits_fn, pv_fn, m_ref, acc_ref, gap_ref=None, offset=None):
    st = logits_fn()
    if offset is not None:
        st = st + offset
    if gap_ref is None:
        m_old = m_ref[...]
        m_new = jnp.maximum(m_old, jnp.max(st, axis=0, keepdims=True))
        acc_ref[...] = jnp.exp2(m_old - m_new) * acc_ref[...] + pv_fn(jnp.exp2(st - m_new).astype(bf16))
        m_ref[...] = m_new
    else:
        gap_ref[...] = jnp.maximum(gap_ref[...], jnp.max(st, axis=0, keepdims=True))
        acc_ref[...] += pv_fn(jnp.exp2(st).astype(bf16))


def _softmax_loop(lo, hi, step_fn, m_ref, acc_ref, gap_ref):
    def body(w, carry):
        logits_fn, pv_fn, offset = step_fn(w)
        _softmax_step(logits_fn, pv_fn, m_ref, acc_ref, gap_ref, offset)
        return carry

    lax.fori_loop(lo, hi, body, 0)


def _lazy_failed(gap_ref, denominators):
    return jnp.logical_not((jnp.max(gap_ref[...]) <= LAZY_GAP) & (jnp.min(denominators) >= LAZY_FLOOR))


def _block_kind(rel):
    return jnp.where(rel < -1, KIND_FAR,
                     jnp.where(rel == -1, KIND_PREV, jnp.where(rel == 0, KIND_DIAG, KIND_MASKED)))


def _sweep_steps(i):
    nfar = jnp.maximum(i - 1 - NSUB, 0) // NSUB_FAR
    near0 = NSUB + nfar * NSUB_FAR
    return nfar, near0, (i - near0 + NSUB) // NSUB


def _ffn_kernel(*refs, fuse_out, layer, fc, nj):
    if fuse_out:
        h_ref, mix_ref, wout_ref, g_ref, wi_hbm, wo_hbm, o_ref, xn_ref, wa_buf, wb_buf, wo_buf, sem = refs
    else:
        h_ref, g_ref, wi_hbm, wo_hbm, o_ref, xn_ref, wa_buf, wb_buf, wo_buf, sem = refs

    def chunk_copies(j, slot):
        lo = pl.multiple_of(j * fc, fc)
        hi = pl.multiple_of((nj + j) * fc, fc)
        return (pltpu.make_async_copy(wi_hbm.at[layer, :, pl.ds(lo, fc)], wa_buf.at[slot], sem.at[0, slot]),
                pltpu.make_async_copy(wi_hbm.at[layer, :, pl.ds(hi, fc)], wb_buf.at[slot], sem.at[1, slot]),
                pltpu.make_async_copy(wo_hbm.at[layer, pl.ds(lo, fc), :], wo_buf.at[slot], sem.at[2, slot]))

    i = pl.program_id(0)
    first = i * nj

    @pl.when(i == 0)
    def _():
        for c in chunk_copies(0, 0):
            c.start()

    r = h_ref[...]
    if fuse_out:
        r = r + _dot(mix_ref[...], wout_ref[...])
    o_ref[...] = r
    xn_ref[...] = (_rms_rows(r) * g_ref[...]).astype(bf16)

    def body(j, carry):
        slot = (first + j) & 1

        @pl.when((j + 1 < nj) | (i + 1 < pl.num_programs(0)))
        def _():
            for c in chunk_copies(jnp.where(j + 1 < nj, j + 1, 0), 1 - slot):
                c.start()

        for c in chunk_copies(j, slot):
            c.wait()
        xn = xn_ref[...]
        a = _dot(xn, wa_buf[slot].astype(bf16))
        b = _dot(xn, wb_buf[slot].astype(bf16))
        act = (a / (1.0 + jnp.exp(-a)) * b).astype(bf16)
        o_ref[...] += 0.5 * _dot(act, wo_buf[slot].astype(bf16))
        return carry

    lax.fori_loop(0, nj, body, 0)


def _ffn(h, g, wi, wo, layer, *, tm, fc, mix=None, wout=None, n_rows=None):
    d = h.shape[1]
    n = h.shape[0] if n_rows is None else n_rows
    assert n % tm == 0
    dff = wo.shape[1]
    nj = dff // fc
    fuse = mix is not None
    row = lambda i: (i, 0)
    in_specs = [pl.BlockSpec((tm, d), row)]
    args = [h]
    if fuse:
        in_specs += [pl.BlockSpec((tm, mix.shape[1]), row), pl.BlockSpec(wout.shape, lambda i: (0, 0))]
        args += [mix, wout]
    in_specs += [
        pl.BlockSpec((1, d), lambda i: (0, 0)),
        pl.BlockSpec(memory_space=pl.ANY),
        pl.BlockSpec(memory_space=pl.ANY),
    ]
    args += [g.reshape(1, d), wi, wo]
    return pl.pallas_call(
        functools.partial(_ffn_kernel, fuse_out=fuse, layer=layer, fc=fc, nj=nj),
        grid=(n // tm,),
        in_specs=in_specs,
        out_specs=pl.BlockSpec((tm, d), row),
        out_shape=jax.ShapeDtypeStruct((n, d), f32),
        scratch_shapes=[
            pltpu.VMEM((tm, d), bf16),
            pltpu.VMEM((2, d, fc), f32),
            pltpu.VMEM((2, d, fc), f32),
            pltpu.VMEM((2, fc, d), f32),
            pltpu.SemaphoreType.DMA((3, 2)),
        ],
        compiler_params=_cparams(("arbitrary",)),
        name="ffn_out" if fuse else "ffn",
    )(*args)


def _rel_bucket(rel):
    half = REL_BUCKETS // 2
    max_exact = half // 2
    n = jnp.abs(rel)
    large = max_exact + (jnp.log(jnp.maximum(n, 1).astype(jnp.float32) / max_exact)
                         / math.log(REL_MAX_DIST / max_exact) * (half - max_exact)).astype(jnp.int32)
    large = jnp.minimum(large, half - 1)
    return jnp.where(rel > 0, half, 0) + jnp.where(n < max_exact, n, large)


def _rel_tiles():
    k = np.arange(LANES)[:, None]
    q = np.arange(LANES)[None, :]
    far = np.full((LANES, LANES), -4 * LANES)
    ones = np.ones((LANES, LANES), bool)
    rels = [k - q, k - q - LANES, far, (k % N_META) - N_META - q, (k % N_META) - (q % N_META), far, k - q - LANES]
    vis = [(k // CHUNK) <= (q // CHUNK), ones, ones, ones, (k // N_META) == (q // N_META), ~ones,
           (q < CHUNK) | (k >= CHUNK)]
    return (np.stack([np.broadcast_to(a, (LANES, LANES)) for a in rels]).astype(np.int32),
            np.stack([np.broadcast_to(a, (LANES, LANES)) for a in vis]).astype(np.int32))


def _bias_kernel(rb_ref, bucket_ref, vis_ref, o_ref):
    h = pl.program_id(0)
    for kind in range(N_KINDS):
        bk = bucket_ref[kind]
        acc = jnp.zeros((LANES, LANES), f32)
        for b in range(REL_BUCKETS):
            acc = jnp.where(bk == b, rb_ref[b, h], acc)
        o_ref[kind, 0] = jnp.where(vis_ref[kind] != 0, acc * LOG2E, NEG_INF)


def _bias_tiles(rel_bias):
    rel, vis = _rel_tiles()
    bucket = _rel_bucket(jnp.asarray(rel))
    nk = N_KINDS
    return pl.pallas_call(
        _bias_kernel,
        grid=(REL_HEADS,),
        in_specs=[
            pl.BlockSpec(memory_space=pltpu.SMEM),
            pl.BlockSpec((nk, LANES, LANES), lambda h: (0, 0, 0)),
            pl.BlockSpec((nk, LANES, LANES), lambda h: (0, 0, 0)),
        ],
        out_specs=pl.BlockSpec((nk, 1, LANES, LANES), lambda h: (0, h, 0, 0)),
        out_shape=jax.ShapeDtypeStruct((nk, REL_HEADS, LANES, LANES), f32),
        compiler_params=_cparams(("arbitrary",)),
        name="bias_tiles",
    )(rel_bias, bucket, jnp.asarray(vis))


def _bias_spec():
    return pl.BlockSpec((N_KINDS, REL_HEADS, LANES, LANES), lambda s: (0, 0, 0, 0))


def _meta_views(rows, cols_t, bsz):
    f = rows.shape[1]
    return (rows.reshape(bsz, N_META, f),
            cols_t.reshape(cols_t.shape[0], bsz, N_META).transpose(1, 0, 2))


A_QD = A_HEADS * 2 * A_HD
A_VR = A_VD + ONES_ROWS


def _proj_a_kernel(h_ref, g_ref, w_ref, gq_ref, gk_ref, qs_ref, k_ref, vt_ref):
    xn = (_rms_rows(h_ref[...]) * g_ref[...]).astype(bf16)
    y = _dot(xn, w_ref[...])
    q = _group_rms(y[:, :A_QD], A_HD) * gq_ref[...]
    lo = _lo_half_mask(q.shape, 2 * A_HD, A_HD)
    q_lo = jnp.where(lo, q, 0.0).astype(bf16)
    q_hi = jnp.where(lo, 0.0, q).astype(bf16)
    k_ref[...] = (_group_rms(y[:, A_QD:2 * A_QD], A_HD) * gk_ref[...]).astype(bf16)
    ones = jnp.ones((ONES_ROWS, LANES), bf16)
    for t in range(vt_ref.shape[0]):
        rows = slice(t * LANES, (t + 1) * LANES)
        vt = y[rows, 2 * A_QD:].T.astype(bf16)
        for h in range(A_HEADS):
            qs_ref[t, h, :LANES, :] = q_lo[rows, h * A_VD:(h + 1) * A_VD]
            qs_ref[t, h, LANES:, :] = q_hi[rows, h * A_VD:(h + 1) * A_VD]
            vt_ref[t, h * A_VR:h * A_VR + A_VD, :] = vt[h * A_VD:(h + 1) * A_VD]
            vt_ref[t, h * A_VR + A_VD:(h + 1) * A_VR, :] = ones


def _proj_a(h, g, w, qk_norm, *, tm):
    n, d = h.shape
    nw = w.shape[1]
    gq = jnp.tile(qk_norm[0] * (A_HD ** -0.5 * LOG2E), A_QD // A_HD).reshape(1, A_QD)
    gk = jnp.tile(qk_norm[1], A_QD // A_HD).reshape(1, A_QD)
    return pl.pallas_call(
        _proj_a_kernel,
        grid=(n // tm,),
        in_specs=[
            pl.BlockSpec((tm, d), lambda i: (i, 0)),
            pl.BlockSpec((1, d), lambda i: (0, 0)),
            pl.BlockSpec((d, nw), lambda i: (0, 0)),
            pl.BlockSpec((1, A_QD), lambda i: (0, 0)),
            pl.BlockSpec((1, A_QD), lambda i: (0, 0)),
        ],
        out_specs=[
            pl.BlockSpec((tm // LANES, A_HEADS, 2 * LANES, A_VD), lambda i: (i, 0, 0, 0)),
            pl.BlockSpec((tm, A_QD), lambda i: (i, 0)),
            pl.BlockSpec((tm // LANES, A_HEADS * A_VR, LANES), lambda i: (i, 0, 0)),
        ],
        out_shape=[
            jax.ShapeDtypeStruct((n // LANES, A_HEADS, 2 * LANES, A_VD), bf16),
            jax.ShapeDtypeStruct((n, A_QD), bf16),
            jax.ShapeDtypeStruct((n // LANES, A_HEADS * A_VR, LANES), bf16),
        ],
        compiler_params=_cparams(("arbitrary",)),
        name="proj_a",
    )(h, g.reshape(1, d), w, gq, gk)


def _attn_a_kernel(qs_ref, k_ref, vt_ref, km_ref, vtm_ref, kmf_ref, vtmf_ref, bias_ref, lam_ref, sub_ref,
                   o_ref, m_ref, acc_ref, gap_ref, *, nblk, lambda_init):
    s_id = pl.program_id(0)
    nb_real = pl.num_programs(0) - 1
    hw = 2 * A_HD

    def init():
        m_ref[...] = jnp.full(m_ref.shape, NEG_INF, f32)
        acc_ref[...] = jnp.zeros(acc_ref.shape, f32)

    def step_fns(kt_fn, vtt_fns, bias_fn):
        def logits():
            sts = [_dot_nt(kt_fn(h), qs_ref[0, h]) for h in range(A_HEADS)]
            if bias_fn is not None:
                sts = [st + bias_fn(h) for h, st in enumerate(sts)]
            return jnp.concatenate(sts, axis=1)

        def pv(pb):
            outs = []
            for h in range(A_HEADS):
                acc = None
                for vtt_fn, r0 in vtt_fns:
                    vtt = vtt_fn(h)
                    part = _dot(vtt, pb[r0:r0 + vtt.shape[1], h * 2 * LANES:(h + 1) * 2 * LANES])
                    acc = part if acc is None else acc + part
                outs.append(acc)
            return jnp.concatenate(outs, axis=1)

        return logits, pv

    def bias_rows(kind, h, rows):
        return jnp.concatenate([bias_ref[kind, h, 0:rows, :], bias_ref[kind, A_HEADS + h, 0:rows, :]], axis=1)

    def finalize():
        lam = lam_ref[...]
        lam_full = (jnp.exp(jnp.sum(lam[0:1] * lam[1:2], axis=-1, keepdims=True))
                    - jnp.exp(jnp.sum(lam[2:3] * lam[3:4], axis=-1, keepdims=True)) + lambda_init)
        o = acc_ref[0:A_VD, :] * (1.0 / acc_ref[A_VD:A_VD + 1, :])
        for h in range(A_HEADS):
            d = o[:, 2 * h * LANES:(2 * h + 1) * LANES] - lam_full * o[:, (2 * h + 1) * LANES:(2 * h + 2) * LANES]
            d = d * lax.rsqrt(jnp.mean(d * d, axis=0, keepdims=True) + EPS) * sub_ref[...] * (1.0 - lambda_init)
            o_ref[:, h * hw:(h + 1) * hw] = d.T.astype(bf16)

    @pl.when(s_id < nb_real)
    def _():
        i = s_id % nblk
        kind_m = jnp.where(i == 0, KIND_META0, KIND_FAR)
        nfar, near0, nnear = _sweep_steps(i)

        def k_fn(b0, nsub):
            off = pl.multiple_of(b0 * LANES, NSUB * LANES)
            return lambda h: k_ref[pl.ds(off, nsub * LANES), h * hw:(h + 1) * hw]

        def vt_fn(b0, nsub):
            return lambda h: jnp.concatenate([vt_ref[b0 + t, h * A_VR:(h + 1) * A_VR, :] for t in range(nsub)], axis=1)

        def near_bias(b0):
            kinds = [_block_kind(b0 + t - i) for t in range(NSUB)]
            return [lambda h, kind=kind: bias_rows(kind, h, LANES) for kind in kinds]

        def first_fns():
            biases = [lambda h: bias_rows(kind_m, h, N_META)] + near_bias(0)
            return step_fns(
                lambda h: jnp.concatenate([km_ref[0, :, h * hw:(h + 1) * hw], k_fn(0, NSUB)(h)], axis=0),
                [(lambda h: vtm_ref[0, h * A_VR:(h + 1) * A_VR, :], 0), (vt_fn(0, NSUB), N_META)],
                lambda h: jnp.concatenate([b(h) for b in biases], axis=0))

        def far_step(w):
            b0 = NSUB + w * NSUB_FAR
            far_bias = jnp.concatenate([bias_rows(KIND_FAR, h, 1) for h in range(A_HEADS)], axis=1)
            return step_fns(k_fn(b0, NSUB_FAR), [(vt_fn(b0, NSUB_FAR), 0)], None) + (far_bias,)

        def near_step(u):
            b0 = near0 + u * NSUB
            biases = near_bias(b0)
            return step_fns(k_fn(b0, NSUB), [(vt_fn(b0, NSUB), 0)],
                            lambda h: jnp.concatenate([b(h) for b in biases], axis=0)) + (None,)

        def sweep(gap_ref):
            init()
            _softmax_step(*first_fns(), m_ref, acc_ref, gap_ref)
            _softmax_loop(0, nfar, far_step, m_ref, acc_ref, gap_ref)
            _softmax_loop(0, nnear, near_step, m_ref, acc_ref, gap_ref)

        gap_ref[...] = jnp.full(gap_ref.shape, NEG_INF, f32)
        sweep(gap_ref)
        pl.when(_lazy_failed(gap_ref, acc_ref[A_VD:A_VD + 1, :]))(lambda: sweep(None))
        finalize()

    @pl.when(s_id == nb_real)
    def _():
        init()
        _softmax_step(*step_fns(lambda h: kmf_ref[:, h * hw:(h + 1) * hw],
                                [(lambda h: vtmf_ref[0, h * A_VR:(h + 1) * A_VR, :], 0)],
                                lambda h: bias_rows(KIND_METAMETA, h, LANES)), m_ref, acc_ref)
        finalize()


def _attn_a(qs, k, vt, bias, lam, subln, *, bsz, nblk, lambda_init):
    n = k.shape[0]
    nb_real = bsz * nblk
    seq = nblk * LANES
    d = A_QD
    vr = A_HEADS * A_VR
    assert nblk % NSUB == 0
    km, vtm = _meta_views(k[nb_real * LANES:], vt[nb_real], bsz)
    kern = functools.partial(_attn_a_kernel, nblk=nblk, lambda_init=lambda_init)
    bclamp = lambda s: jnp.minimum(s // nblk, bsz - 1)
    return pl.pallas_call(
        kern,
        grid=(nb_real + 1,),
        in_specs=[
            pl.BlockSpec((1,) + qs.shape[1:], lambda s: (s, 0, 0, 0)),
            pl.BlockSpec((seq, d), lambda s: (bclamp(s), 0)),
            pl.BlockSpec((nblk, vr, LANES), lambda s: (bclamp(s), 0, 0)),
            pl.BlockSpec((1, N_META, d), lambda s: (bclamp(s), 0, 0)),
            pl.BlockSpec((1, vr, N_META), lambda s: (bclamp(s), 0, 0)),
            pl.BlockSpec((LANES, d), lambda s: (nb_real, 0)),
            pl.BlockSpec((1, vr, LANES), lambda s: (nb_real, 0, 0)),
            _bias_spec(),
            pl.BlockSpec((4, A_HD), lambda s: (0, 0)),
            pl.BlockSpec((A_VD, LANES), lambda s: (0, 0)),
        ],
        out_specs=pl.BlockSpec((LANES, d), lambda s: (s, 0)),
        out_shape=jax.ShapeDtypeStruct((n, d), bf16),
        scratch_shapes=[
            pltpu.VMEM((1, A_HEADS * 2 * LANES), f32),
            pltpu.VMEM((A_VR, A_HEADS * 2 * LANES), f32),
            pltpu.VMEM((1, A_HEADS * 2 * LANES), f32),
        ],
        compiler_params=_cparams(("arbitrary",)),
        name="attn_a",
    )(qs, k, vt, km, vtm, k, vt, bias, lam, jnp.broadcast_to(subln[:, None], (A_VD, LANES)))


C_QD = C_Q_HEADS * C_HD
C_KD = 2 * C_KV_HEADS * C_HD
C_VR = 2 * C_HD + ONES_ROWS


def _proj_c_kernel(h_ref, g_ref, w_ref, gq_ref, gk_ref, qs_ref, k_ref, vt_ref):
    xn = (_rms_rows(h_ref[...]) * g_ref[...]).astype(bf16)
    y = _dot(xn, w_ref[...])
    q = _group_rms(y[:, :C_QD], C_HD) * gq_ref[...]
    lo = _lo_half_mask(q.shape, 2 * C_HD, C_HD)
    q_even = jnp.where(lo, q, 0.0).astype(bf16)
    q_odd = jnp.where(lo, 0.0, q).astype(bf16)
    k_ref[...] = (_group_rms(y[:, C_QD:C_QD + C_KD], C_HD) * gk_ref[...]).astype(bf16)
    ones = jnp.ones((ONES_ROWS, LANES), bf16)
    for t in range(vt_ref.shape[0]):
        rows = slice(t * LANES, (t + 1) * LANES)
        vt = y[rows, C_QD + C_KD:].T.astype(bf16)
        for g in range(C_KV_HEADS):
            for hh in range(C_GROUP):
                pair = (g * C_GROUP + hh) // 2
                src = q_even if hh % 2 == 0 else q_odd
                qs_ref[t, g, hh * LANES:(hh + 1) * LANES, :] = src[rows, pair * LANES:(pair + 1) * LANES]
            vt_ref[t, g * C_VR:g * C_VR + 2 * C_HD, :] = vt[g * 2 * C_HD:(g + 1) * 2 * C_HD]
            vt_ref[t, g * C_VR + 2 * C_HD:(g + 1) * C_VR, :] = ones


def _proj_c(h, g, w, qk_norm, *, tm):
    n, d = h.shape
    nw = w.shape[1]
    nt = tm // LANES
    gq = jnp.tile(qk_norm[0] * (C_HD ** -0.5 * LOG2E), C_QD // C_HD).reshape(1, C_QD)
    gk = jnp.tile(qk_norm[1], C_KD // C_HD).reshape(1, C_KD)
    return pl.pallas_call(
        _proj_c_kernel,
        grid=(n // tm,),
        in_specs=[
            pl.BlockSpec((tm, d), lambda i: (i, 0)),
            pl.BlockSpec((1, d), lambda i: (0, 0)),
            pl.BlockSpec((d, nw), lambda i: (0, 0)),
            pl.BlockSpec((1, C_QD), lambda i: (0, 0)),
            pl.BlockSpec((1, C_KD), lambda i: (0, 0)),
        ],
        out_specs=[
            pl.BlockSpec((nt, C_KV_HEADS, C_GROUP * LANES, LANES), lambda i: (i, 0, 0, 0)),
            pl.BlockSpec((tm, C_KD), lambda i: (i, 0)),
            pl.BlockSpec((nt, C_KV_HEADS * C_VR, LANES), lambda i: (i, 0, 0)),
        ],
        out_shape=[
            jax.ShapeDtypeStruct((n // LANES, C_KV_HEADS, C_GROUP * LANES, LANES), bf16),
            jax.ShapeDtypeStruct((n, C_KD), bf16),
            jax.ShapeDtypeStruct((n // LANES, C_KV_HEADS * C_VR, LANES), bf16),
        ],
        compiler_params=_cparams(("arbitrary",)),
        name="proj_c",
    )(h, g.reshape(1, d), w, gq, gk)


def _attn_c_kernel(sink_ref, qs_ref, k_ref, vt_ref, km_ref, vtm_ref, kmf_ref, vtmf_ref, bias_ref, o_ref, *, nblk):
    s_id = pl.program_id(0)
    nb_real = pl.num_programs(0) - 1
    r_io, _ = _tile_iotas()
    vd = 2 * C_HD

    def attend(tiles):
        top = r_io < C_HD
        for g in range(C_KV_HEADS):
            sink = jnp.concatenate(
                [jnp.full((1, LANES), sink_ref[g * C_GROUP + hh] * LOG2E, f32) for hh in range(C_GROUP)], axis=1)
            sts = []
            m = sink
            for (k_fn, vt_fn, bias_fn) in tiles:
                st = _dot_nt(k_fn(g), qs_ref[0, g])
                st = st + jnp.concatenate([bias_fn(g * C_GROUP + hh) for hh in range(C_GROUP)], axis=1)
                m = jnp.maximum(m, jnp.max(st, axis=0, keepdims=True))
                sts.append(st)
            acc = None
            for st, (k_fn, vt_fn, bias_fn) in zip(sts, tiles):
                pv = _dot(vt_fn(g), jnp.exp2(st - m).astype(bf16))
                acc = pv if acc is None else acc + pv
            o = acc[0:vd, :] * (1.0 / (acc[vd:vd + 1, :] + jnp.exp2(sink - m)))
            for cc in range(C_GROUP // 2):
                even = o[:, (2 * cc) * LANES:(2 * cc + 1) * LANES]
                odd = o[:, (2 * cc + 1) * LANES:(2 * cc + 2) * LANES]
                col = (g * (C_GROUP // 2) + cc) * LANES
                o_ref[:, col:col + LANES] = jnp.where(top, even, odd).T.astype(bf16)

    @pl.when(s_id < nb_real)
    def _():
        i = s_id % nblk
        prev = jnp.maximum(i - 1, 0)
        poff = pl.multiple_of(prev * LANES, LANES)
        coff = pl.multiple_of(i * LANES, LANES)
        kind_m = jnp.where(i == 0, KIND_META0, KIND_FAR)
        kind_p = jnp.where(i == 0, KIND_MASKED, KIND_PREVWIN)
        attend([
            (lambda g: km_ref[0, :, g * LANES:(g + 1) * LANES], lambda g: vtm_ref[0, g * C_VR:(g + 1) * C_VR, :],
             lambda h: bias_ref[kind_m, h, 0:N_META, :]),
            (lambda g: k_ref[pl.ds(poff, LANES), g * LANES:(g + 1) * LANES],
             lambda g: vt_ref[prev, g * C_VR:(g + 1) * C_VR, :], lambda h: bias_ref[kind_p, h]),
            (lambda g: k_ref[pl.ds(coff, LANES), g * LANES:(g + 1) * LANES],
             lambda g: vt_ref[i, g * C_VR:(g + 1) * C_VR, :], lambda h: bias_ref[KIND_DIAG, h]),
        ])

    @pl.when(s_id == nb_real)
    def _():
        attend([(lambda g: kmf_ref[:, g * LANES:(g + 1) * LANES], lambda g: vtmf_ref[0, g * C_VR:(g + 1) * C_VR, :],
                 lambda h: bias_ref[KIND_METAMETA, h])])


def _attn_c(qs, k, vt, bias, sinks, *, bsz, nblk):
    n = k.shape[0]
    nb_real = bsz * nblk
    seq = nblk * LANES
    vr = C_KV_HEADS * C_VR
    kern = functools.partial(_attn_c_kernel, nblk=nblk)
    bclamp = lambda s: jnp.minimum(s // nblk, bsz - 1)
    km, vtm = _meta_views(k[nb_real * LANES:], vt[nb_real], bsz)
    return pl.pallas_call(
        kern,
        grid=(nb_real + 1,),
        in_specs=[
            pl.BlockSpec(memory_space=pltpu.SMEM),
            pl.BlockSpec((1,) + qs.shape[1:], lambda s: (s, 0, 0, 0)),
            pl.BlockSpec((seq, C_KD), lambda s: (bclamp(s), 0)),
            pl.BlockSpec((nblk, vr, LANES), lambda s: (bclamp(s), 0, 0)),
            pl.BlockSpec((1, N_META, C_KD), lambda s: (bclamp(s), 0, 0)),
            pl.BlockSpec((1, vr, N_META), lambda s: (bclamp(s), 0, 0)),
            pl.BlockSpec((LANES, C_KD), lambda s: (nb_real, 0)),
            pl.BlockSpec((1, vr, LANES), lambda s: (nb_real, 0, 0)),
            _bias_spec(),
        ],
        out_specs=pl.BlockSpec((LANES, C_QD), lambda s: (s, 0)),
        out_shape=jax.ShapeDtypeStruct((n, C_QD), bf16),
        compiler_params=_cparams(("arbitrary",)),
        name="attn_c",
    )(sinks, qs, k, vt, km, vtm, k, vt, bias)


B_QA = B_HEADS * B_KV_RANK
B_QI = IDX_HEADS * IDX_DIM
B_W1 = 2 * B_Q_RANK + 2 * LANES
B_TR = B_KV_RANK + ONES_ROWS


def _proj_b_kernel(h_ref, g_ref, w1_ref, ln_ref, wuq_ref, qn_ref,
                   qa_ref, qi_ref, ckv_ref, ckvt_ref, kk_ref, wit_ref):
    xn = (_rms_rows(h_ref[...]) * g_ref[...]).astype(bf16)
    y = _dot(xn, w1_ref[...])
    r = B_Q_RANK
    cq = (_rms_rows(y[:, :r]) * ln_ref[0:1, :]).astype(bf16)
    ckv = _rms_rows(y[:, r:2 * r]) * ln_ref[1:2, :]
    ckv_ref[...] = ckv.astype(bf16)
    kk_ref[...] = _rms_rows(y[:, 2 * r:2 * r + LANES]).astype(bf16)
    wi = y[:, 2 * r + LANES:] * (IDX_HEADS ** -0.5)
    ones = jnp.ones((ONES_ROWS, LANES), bf16)
    for t in range(ckvt_ref.shape[0]):
        ckvt_ref[t, 0:r, :] = ckv[t * LANES:(t + 1) * LANES, :].T.astype(bf16)
        ckvt_ref[t, r:, :] = ones
        wit_ref[t] = wi[t * LANES:(t + 1) * LANES, :].T[0:IDX_HEADS, :]
    z = _dot(cq, wuq_ref[...])
    qa = (_group_rms(z[:, :B_QA], B_KV_RANK) * qn_ref[...]).astype(bf16)
    qi = z[:, B_QA:] * (IDX_DIM ** -0.5)
    lo = _lo_half_mask(qi.shape, 2 * IDX_DIM, IDX_DIM)
    qi_lo = jnp.where(lo, qi, 0.0).astype(bf16)
    qi_hi = jnp.where(lo, 0.0, qi).astype(bf16)
    for t in range(qa_ref.shape[0]):
        rows = slice(t * LANES, (t + 1) * LANES)
        for hd in range(B_HEADS):
            qa_ref[t, hd * LANES:(hd + 1) * LANES, :] = qa[rows, hd * r:(hd + 1) * r]
        for hh in range(IDX_HEADS):
            src = qi_lo if hh % 2 == 0 else qi_hi
            qi_ref[t, hh * LANES:(hh + 1) * LANES, :] = src[rows, (hh // 2) * LANES:(hh // 2 + 1) * LANES]


def _proj_b(h, g, w1, latent_norm, wuq, q_norm, *, tm):
    n, d = h.shape
    qn = jnp.tile(q_norm * (B_KV_RANK ** -0.5 * LOG2E), B_HEADS).reshape(1, B_QA)
    row = lambda i: (i, 0)
    row3 = lambda i: (i, 0, 0)
    const = lambda i: (0, 0)
    nt = tm // LANES
    return pl.pallas_call(
        _proj_b_kernel,
        grid=(n // tm,),
        in_specs=[
            pl.BlockSpec((tm, d), row),
            pl.BlockSpec((1, d), const),
            pl.BlockSpec(w1.shape, const),
            pl.BlockSpec(latent_norm.shape, const),
            pl.BlockSpec(wuq.shape, const),
            pl.BlockSpec((1, B_QA), const),
        ],
        out_specs=[
            pl.BlockSpec((nt, B_HEADS * LANES, B_KV_RANK), row3),
            pl.BlockSpec((nt, IDX_HEADS * LANES, LANES), row3),
            pl.BlockSpec((tm, B_KV_RANK), row),
            pl.BlockSpec((nt, B_TR, LANES), row3),
            pl.BlockSpec((tm, LANES), row),
            pl.BlockSpec((nt, IDX_HEADS, LANES), row3),
        ],
        out_shape=[
            jax.ShapeDtypeStruct((n // LANES, B_HEADS * LANES, B_KV_RANK), bf16),
            jax.ShapeDtypeStruct((n // LANES, IDX_HEADS * LANES, LANES), bf16),
            jax.ShapeDtypeStruct((n, B_KV_RANK), bf16),
            jax.ShapeDtypeStruct((n // LANES, B_TR, LANES), bf16),
            jax.ShapeDtypeStruct((n, LANES), bf16),
            jax.ShapeDtypeStruct((n // LANES, IDX_HEADS, LANES), f32),
        ],
        compiler_params=_cparams(("arbitrary",)),
        name="proj_b",
    )(h, g.reshape(1, d), w1, latent_norm, wuq, qn)


def _attn_b_kernel(qs_ref, is_ref, wit_ref, ckv_ref, ckvt_ref, kk_ref, ckvm_ref, ckvtm_ref, kkm_ref,
                   ckvmf_ref, ckvtmf_ref, bias_ref, wuvt_ref,
                   o_ref, key_ref, pen_ref, m_ref, acc_ref, gap_ref, *, nblk, k_sel):
    s_id = pl.program_id(0)
    nb_real = pl.num_programs(0) - 1
    r_io, c_io = _tile_iotas()
    rk = B_KV_RANK

    def init():
        m_ref[...] = jnp.full(m_ref.shape, NEG_INF, f32)
        acc_ref[...] = jnp.zeros(acc_ref.shape, f32)

    def add_per_head(st, bias_fn, pen):
        cols = []
        for h in range(B_HEADS):
            add = pen if bias_fn is None else (bias_fn(h) if pen is None else bias_fn(h) + pen)
            cols.append(st[:, h * LANES:(h + 1) * LANES] + add)
        return jnp.concatenate(cols, axis=1)

    def step_fns(ckv_fn, ckvt_fns, bias_fn, pen_fn):
        def logits():
            st = _dot_nt(ckv_fn(), qs_ref[0])
            return add_per_head(st, bias_fn, None if pen_fn is None else pen_fn())

        def pv(pb):
            acc = None
            for ckvt_fn, r0 in ckvt_fns:
                ckvt = ckvt_fn()
                part = _dot(ckvt, pb[r0:r0 + ckvt.shape[1], :])
                acc = part if acc is None else acc + part
            return acc

        return logits, pv

    def finalize():
        olat = (acc_ref[0:rk, :] * (1.0 / acc_ref[rk:rk + 1, :])).astype(bf16)
        ot = jnp.concatenate([_dot(wuvt_ref[h], olat[:, h * LANES:(h + 1) * LANES]) for h in range(B_HEADS)], axis=0)
        o_ref[...] = ot.T.astype(bf16)

    def index_scores(kk):
        s = jnp.maximum(_dot_nt(kk, is_ref[0]), 0.0)
        wt = wit_ref[0]
        sc = jnp.zeros((kk.shape[0], LANES), f32)
        for hh in range(IDX_HEADS):
            sc = sc + wt[hh:hh + 1, :] * s[:, hh * LANES:(hh + 1) * LANES]
        return sc

    def sort_key(sc):
        bits = lax.bitcast_convert_type(sc + 0.0, jnp.int32)
        return jnp.where(bits < 0, bits ^ jnp.int32(0x7FFFFFFF), bits)

    @pl.when(s_id < nb_real)
    def _():
        i = s_id % nblk
        ntile = i + 2

        int_min_tile = jnp.full((LANES, LANES), INT_MIN, jnp.int32)
        key_ref[0] = int_min_tile
        key_ref[0, 0:N_META, :] = sort_key(index_scores(kkm_ref[0]))
        key_ref[i + 2] = int_min_tile

        def score_body(jp, carry):
            off = pl.multiple_of(jp * (2 * LANES), 2 * LANES)
            keys = sort_key(index_scores(kk_ref[pl.ds(off, 2 * LANES), :]))
            for t in range(2):
                j = _vec(2 * jp + t)
                vis = (j < i) | ((j == i) & ((r_io >> 6) <= (c_io >> 6)))
                key_ref[2 * jp + t + 1] = jnp.where(vis, keys[t * LANES:(t + 1) * LANES], jnp.int32(INT_MIN))
            return carry

        lax.fori_loop(0, i // 2 + 1, score_body, 0)

        def count(pred):
            def cbody(tp, accv):
                for t in (2 * tp, 2 * tp + 1):
                    accv = accv + jnp.where(pred(key_ref[t], t), 1.0, 0.0)
                return accv
            accv = lax.fori_loop(0, (ntile + 1) // 2, cbody, jnp.zeros((LANES, LANES), f32))
            return jnp.sum(accv, axis=0, keepdims=True)

        kf = float(k_sel)
        zero = jnp.zeros((1, LANES), jnp.int32)
        t0 = jnp.where(count(lambda k, t: k >= zero) >= kf, zero, jnp.int32(INT_MIN))

        def bit_body(it, tcur):
            cand = tcur | jnp.left_shift(jnp.int32(1), 30 - it)
            return jnp.where(count(lambda k, t: k >= cand) >= kf, cand, tcur)

        thr = lax.fori_loop(0, 31, bit_body, t0)

        need = kf - count(lambda k, t: k > thr)
        n_eq = count(lambda k, t: k == thr)
        has_thr = thr > jnp.int32(INT_MIN)
        tied = jnp.max(jnp.where(has_thr & (n_eq > need), 1.0, 0.0)) > 0.0

        def tie_search(_):
            def jbody(it, jcur):
                cand = jcur | jnp.left_shift(jnp.int32(1), 11 - it)
                cnt = count(lambda k, t: (k == thr) & ((t * LANES + r_io) < cand))
                return jnp.where(cnt < need, cand, jcur)
            return lax.fori_loop(0, 12, jbody, jnp.zeros((1, LANES), jnp.int32))

        j_last = lax.cond(tied, tie_search, lambda _: jnp.full((1, LANES), 4095, jnp.int32), 0)
        j_last = jnp.where(has_thr, j_last, -1)

        def pen_body(t, carry):
            k = key_ref[t]
            sel = (k > thr) | ((k == thr) & ((t * LANES + r_io) <= j_last))
            pen_ref[t] = jnp.where(sel, 0.0, NEG_INF)
            return carry

        lax.fori_loop(0, ntile, pen_body, 0)
        for t in range(1, NSUB):
            pen_ref[i + 1 + t] = jnp.full((LANES, LANES), NEG_INF, f32)

        kind_m = jnp.where(i == 0, KIND_META0, KIND_FAR)
        nfar, near0, nnear = _sweep_steps(i)

        def ckv_fn(b0, nsub):
            off = pl.multiple_of(b0 * LANES, NSUB * LANES)
            return lambda: ckv_ref[pl.ds(off, nsub * LANES), :]

        def ckvt_fn(b0, nsub):
            return lambda: jnp.concatenate([ckvt_ref[b0 + t] for t in range(nsub)], axis=1)

        def pen_fn(b0, nsub):
            return lambda: jnp.concatenate([pen_ref[b0 + t + 1] for t in range(nsub)], axis=0)

        def near_bias(b0):
            kinds = [_block_kind(b0 + t - i) for t in range(NSUB)]
            return lambda h: jnp.concatenate([bias_ref[kinds[t], h] for t in range(NSUB)], axis=0)

        def first_pen():
            return jnp.concatenate([pen_ref[0, 0:N_META, :], pen_fn(0, NSUB)()], axis=0)

        def first_fns():
            bias01 = near_bias(0)
            return step_fns(lambda: jnp.concatenate([ckvm_ref[0], ckv_fn(0, NSUB)()], axis=0),
                            [(lambda: ckvtm_ref[0], 0), (ckvt_fn(0, NSUB), N_META)],
                            lambda h: jnp.concatenate([bias_ref[kind_m, h, 0:N_META, :], bias01(h)], axis=0),
                            first_pen)

        def far_step(w):
            b0 = NSUB + w * NSUB_FAR
            far_bias = jnp.concatenate([bias_ref[KIND_FAR, h, 0:1, :] for h in range(B_HEADS)], axis=1)
            return step_fns(ckv_fn(b0, NSUB_FAR), [(ckvt_fn(b0, NSUB_FAR), 0)], None, pen_fn(b0, NSUB_FAR)) + (far_bias,)

        def near_step(u):
            b0 = near0 + u * NSUB
            return step_fns(ckv_fn(b0, NSUB), [(ckvt_fn(b0, NSUB), 0)], near_bias(b0), pen_fn(b0, NSUB)) + (None,)

        def sweep(gap_ref):
            init()
            _softmax_step(*first_fns(), m_ref, acc_ref, gap_ref)
            _softmax_loop(0, nfar, far_step, m_ref, acc_ref, gap_ref)
            _softmax_loop(0, nnear, near_step, m_ref, acc_ref, gap_ref)

        gap_ref[...] = jnp.full(gap_ref.shape, NEG_INF, f32)
        sweep(gap_ref)
        pl.when(_lazy_failed(gap_ref, acc_ref[rk:rk + 1, :]))(lambda: sweep(None))
        finalize()

    @pl.when(s_id == nb_real)
    def _():
        init()
        _softmax_step(*step_fns(lambda: ckvmf_ref[...], [(lambda: ckvtmf_ref[0], 0)],
                                lambda h: bias_ref[KIND_METAMETA, h], None), m_ref, acc_ref)
        finalize()


def _attn_b(qa, qi, wit, ckv, ckvt, kk, bias, wuvt, *, bsz, nblk, k_sel):
    n = ckv.shape[0]
    nb_real = bsz * nblk
    seq = nblk * LANES
    assert k_sel >= N_META and (nblk + 1) * LANES <= 4096 and nblk % NSUB == 0
    ckvm, ckvtm = _meta_views(ckv[nb_real * LANES:], ckvt[nb_real], bsz)
    kkm = kk[nb_real * LANES:].reshape(bsz, N_META, LANES)
    kern = functools.partial(_attn_b_kernel, nblk=nblk, k_sel=k_sel)
    bidx = lambda s: jnp.minimum(s // nblk, bsz - 1)
    blk = lambda s: (s, 0)
    return pl.pallas_call(
        kern,
        grid=(nb_real + 1,),
        in_specs=[
            pl.BlockSpec((1,) + qa.shape[1:], lambda s: (s, 0, 0)),
            pl.BlockSpec((1,) + qi.shape[1:], lambda s: (s, 0, 0)),
            pl.BlockSpec((1, IDX_HEADS, LANES), lambda s: (s, 0, 0)),
            pl.BlockSpec((seq, B_KV_RANK), lambda s: (bidx(s), 0)),
            pl.BlockSpec((nblk, B_TR, LANES), lambda s: (bidx(s), 0, 0)),
            pl.BlockSpec((seq, LANES), lambda s: (bidx(s), 0)),
            pl.BlockSpec((1, N_META, B_KV_RANK), lambda s: (bidx(s), 0, 0)),
            pl.BlockSpec((1, B_TR, N_META), lambda s: (bidx(s), 0, 0)),
            pl.BlockSpec((1, N_META, LANES), lambda s: (bidx(s), 0, 0)),
            pl.BlockSpec((LANES, B_KV_RANK), lambda s: (nb_real, 0)),
            pl.BlockSpec((1, B_TR, LANES), lambda s: (nb_real, 0, 0)),
            _bias_spec(),
            pl.BlockSpec(wuvt.shape, lambda s: (0, 0, 0)),
        ],
        out_specs=pl.BlockSpec((LANES, B_HEADS * B_VD), blk),
        out_shape=jax.ShapeDtypeStruct((n, B_HEADS * B_VD), bf16),
        scratch_shapes=[
            pltpu.VMEM((nblk + 2, LANES, LANES), jnp.int32),
            pltpu.VMEM((nblk + NSUB, LANES, LANES), f32),
            pltpu.VMEM((1, B_HEADS * LANES), f32),
            pltpu.VMEM((B_TR, B_HEADS * LANES), f32),
            pltpu.VMEM((1, B_HEADS * LANES), f32),
        ],
        compiler_params=_cparams(("arbitrary",)),
        name="attn_b",
    )(qa, qi, wit, ckv, ckvt, kk, ckvm, ckvtm, kkm, ckv, ckvt, bias, wuvt)


def kernel(x, meta_tokens, rel_bias, ln_ffn1, ffn1_wi, ffn1_wo, ln_mix, w_out, ln_ffn2, ffn2_wi, ffn2_wo, a_w_in, a_qk_norm, a_lambda, a_subln, b_w_in, b_latent_norm, b_w_uq, b_q_norm, b_w_uv, c_w_in, c_qk_norm, c_sinks):
    bsz, seq, d = x.shape
    assert d == D_MODEL and seq % LANES == 0 and bsz * N_META == LANES
    nblk = seq // LANES
    n = bsz * seq + LANES
    k_sel = min(TOPK_MAX, seq // 4)
    tm_ffn = _row_tile(n, 1408)
    tm_last = _row_tile(bsz * seq, 1408)
    tm_proj = _row_tile(n, 384, LANES)
    fc = 256

    h = jnp.concatenate([x.reshape(bsz * seq, d),
                         jnp.broadcast_to(meta_tokens.astype(x.dtype), (bsz, N_META, d)).reshape(LANES, d)], axis=0)
    bias = _bias_tiles(rel_bias)

    for layer in range(DEPTH):
        h = _ffn(h, ln_ffn1[layer], ffn1_wi, ffn1_wo, layer, tm=tm_ffn, fc=fc)
        kind, j = layer % N_MIXERS, layer // N_MIXERS
        g = ln_mix[layer]
        if kind == 0:
            lambda_init = 0.8 - 0.6 * math.exp(-0.3 * layer)
            qs, k, vt = _proj_a(h, g, a_w_in[j].astype(bf16), a_qk_norm[j], tm=tm_proj)
            mix = _attn_a(qs, k, vt, bias, a_lambda[j], a_subln[j], bsz=bsz, nblk=nblk, lambda_init=lambda_init)
        elif kind == 1:
            w = b_w_in[j]
            r2 = B_Q_RANK + B_KV_RANK
            kcol = w[:, r2:r2 + IDX_DIM]
            w1 = jnp.concatenate([w[:, :r2], kcol, kcol, w[:, r2 + IDX_DIM:],
                                  jnp.zeros((d, LANES - IDX_HEADS), w.dtype)], axis=1).astype(bf16)
            assert w1.shape[1] == B_W1
            qa, qi, ckv, ckvt, kk, wit = _proj_b(h, g, w1, b_latent_norm[j], b_w_uq[j].astype(bf16), b_q_norm[j],
                                                 tm=tm_proj)
            wuvt = jnp.swapaxes(b_w_uv[j], 1, 2).astype(bf16)
            mix = _attn_b(qa, qi, wit, ckv, ckvt, kk, bias, wuvt, bsz=bsz, nblk=nblk, k_sel=k_sel)
        else:
            w = c_w_in[j]
            kcols = [w[:, C_QD + gi * C_HD:C_QD + (gi + 1) * C_HD] for gi in range(C_KV_HEADS)]
            voff = C_QD + C_KV_HEADS * C_HD
            vcols = [w[:, voff + gi * C_HD:voff + (gi + 1) * C_HD] for gi in range(C_KV_HEADS)]
            wc = jnp.concatenate([w[:, :C_QD]] + [kc for kc in kcols for _ in range(2)]
                                 + [vc for vc in vcols for _ in range(2)], axis=1).astype(bf16)
            qs, k, vt = _proj_c(h, g, wc, c_qk_norm[j], tm=tm_proj)
            mix = _attn_c(qs, k, vt, bias, c_sinks[j], bsz=bsz, nblk=nblk)
        last = layer == DEPTH - 1
        h = _ffn(h, ln_ffn2[layer], ffn2_wi, ffn2_wo, layer, tm=tm_last if last else tm_ffn, fc=fc,
                 mix=mix, wout=w_out[layer].astype(bf16), n_rows=bsz * seq if last else None)
    return h.reshape(bsz, seq, d)
```

```python
import functools
import math

import numpy as np
import jax
import jax.numpy as jnp
from jax import lax
from jax.experimental import pallas as pl
from jax.experimental.pallas import tpu as pltpu

D_MODEL = 1024
DEPTH = 4
CHUNK = 64
N_META = 16
N_MIXERS = 3
NEG_INF = -1e30
REL_BUCKETS = 32
REL_MAX_DIST = 128
REL_HEADS = 16
D_FF = 2816
A_HEADS = 8
A_HD = 64
A_VD = 2 * A_HD
B_HEADS = 16
B_Q_RANK = 256
B_KV_RANK = 256
B_VD = 64
IDX_HEADS = 8
IDX_DIM = 64
TOPK_MAX = 256
C_Q_HEADS = 16
C_KV_HEADS = 2
C_GROUP = C_Q_HEADS // C_KV_HEADS
C_HD = 64
EPS = 1e-6

LANES = 128
BF16_ROWS = 16
VMEM_LIMIT = 56 * 1024 * 1024
INT_MIN = -(2 ** 31)
NSUB = 2
NSUB_FAR = 4
LOG2E = math.log2(math.e)
LAZY_GAP = 57.0
LAZY_FLOOR = 2.0 ** -100
ONES_ROWS = BF16_ROWS

KIND_DIAG, KIND_PREV, KIND_FAR, KIND_META0, KIND_METAMETA, KIND_MASKED, KIND_PREVWIN = 0, 1, 2, 3, 4, 5, 6
N_KINDS = 7

f32 = jnp.float32
bf16 = jnp.bfloat16


def _cparams(sem):
    return pltpu.CompilerParams(dimension_semantics=sem, vmem_limit_bytes=VMEM_LIMIT)


def _row_tile(n, cap, mult=BF16_ROWS):
    best = None
    for t in range(mult, cap + 1, mult):
        if n % t == 0:
            best = t
    assert best is not None
    return best


def _dot(a, b):
    return jnp.dot(a, b, preferred_element_type=f32)


def _dot_nt(a, b):
    return lax.dot_general(a, b, (((1,), (1,)), ((), ())), preferred_element_type=f32)


def _rms_rows(x):
    return x * lax.rsqrt(jnp.mean(x * x, axis=-1, keepdims=True) + EPS)


def _lo_half_mask(shape, period, half):
    return (lax.broadcasted_iota(jnp.int32, shape, 1) & (period - 1)) < half


def _group_rms(x, group):
    r, c = x.shape
    outs = []
    if group == 64:
        lo = _lo_half_mask((r, LANES), LANES, 64)
        for ci in range(c // LANES):
            xc = x[:, ci * LANES:(ci + 1) * LANES]
            x2 = xc * xc
            s_lo = jnp.sum(jnp.where(lo, x2, 0.0), axis=-1, keepdims=True)
            s_hi = jnp.sum(jnp.where(lo, 0.0, x2), axis=-1, keepdims=True)
            inv = jnp.where(lo, lax.rsqrt(s_lo * (1.0 / 64) + EPS), lax.rsqrt(s_hi * (1.0 / 64) + EPS))
            outs.append(xc * inv)
    else:
        for gi in range(c // group):
            outs.append(_rms_rows(x[:, gi * group:(gi + 1) * group]))
    return outs[0] if len(outs) == 1 else jnp.concatenate(outs, axis=-1)


def _tile_iotas():
    r = lax.broadcasted_iota(jnp.int32, (LANES, LANES), 0)
    c = lax.broadcasted_iota(jnp.int32, (LANES, LANES), 1)
    return r, c


def _vec(s):
    return jnp.full((LANES, LANES), s, jnp.int32)


def _softmax_step(logits_fn, pv_fn, m_ref, acc_ref, gap_ref=None, offset=None):
    st = logits_fn()
    if offset is not None:
        st = st + offset
    if gap_ref is None:
        m_old = m_ref[...]
        m_new = jnp.maximum(m_old, jnp.max(st, axis=0, keepdims=True))
        acc_ref[...] = jnp.exp2(m_old - m_new) * acc_ref[...] + pv_fn(jnp.exp2(st - m_new).astype(bf16))
        m_ref[...] = m_new
    else:
        gap_ref[...] = jnp.maximum(gap_ref[...], jnp.max(st, axis=0, keepdims=True))
        acc_ref[...] += pv_fn(jnp.exp2(st).astype(bf16))


def _softmax_loop(lo, hi, step_fn, m_ref, acc_ref, gap_ref):
    def body(w, carry):
        logits_fn, pv_fn, offset = step_fn(w)
        _softmax_step(logits_fn, pv_fn, m_ref, acc_ref, gap_ref, offset)
        return carry

    lax.fori_loop(lo, hi, body, 0)


def _lazy_failed(gap_ref, denominators):
    return jnp.logical_not((jnp.max(gap_ref[...]) <= LAZY_GAP) & (jnp.min(denominators) >= LAZY_FLOOR))


def _block_kind(rel):
    return jnp.where(rel < -1, KIND_FAR,
                     jnp.where(rel == -1, KIND_PREV, jnp.where(rel == 0, KIND_DIAG, KIND_MASKED)))


def _sweep_steps(i):
    nfar = jnp.maximum(i - 1 - NSUB, 0) // NSUB_FAR
    near0 = NSUB + nfar * NSUB_FAR
    return nfar, near0, (i - near0 + NSUB) // NSUB


def _ffn_kernel(*refs, fuse_out, layer, fc, nj):
    if fuse_out:
        h_ref, mix_ref, wout_ref, g_ref, wi_hbm, wo_hbm, o_ref, xn_ref, wa_buf, wb_buf, wo_buf, sem = refs
    else:
        h_ref, g_ref, wi_hbm, wo_hbm, o_ref, xn_ref, wa_buf, wb_buf, wo_buf, sem = refs

    def chunk_copies(j, slot):
        lo = pl.multiple_of(j * fc, fc)
        hi = pl.multiple_of((nj + j) * fc, fc)
        return (pltpu.make_async_copy(wi_hbm.at[layer, :, pl.ds(lo, fc)], wa_buf.at[slot], sem.at[0, slot]),
                pltpu.make_async_copy(wi_hbm.at[layer, :, pl.ds(hi, fc)], wb_buf.at[slot], sem.at[1, slot]),
                pltpu.make_async_copy(wo_hbm.at[layer, pl.ds(lo, fc), :], wo_buf.at[slot], sem.at[2, slot]))

    i = pl.program_id(0)
    first = i * nj

    @pl.when(i == 0)
    def _():
        for c in chunk_copies(0, 0):
            c.start()

    r = h_ref[...]
    if fuse_out:
        r = r + _dot(mix_ref[...], wout_ref[...])
    o_ref[...] = r
    xn_ref[...] = (_rms_rows(r) * g_ref[...]).astype(bf16)

    def body(j, carry):
        slot = (first + j) & 1

        @pl.when((j + 1 < nj) | (i + 1 < pl.num_programs(0)))
        def _():
            for c in chunk_copies(jnp.where(j + 1 < nj, j + 1, 0), 1 - slot):
                c.start()

        for c in chunk_copies(j, slot):
            c.wait()
        xn = xn_ref[...]
        a = _dot(xn, wa_buf[slot].astype(bf16))
        b = _dot(xn, wb_buf[slot].astype(bf16))
        act = (a / (1.0 + jnp.exp(-a)) * b).astype(bf16)
        o_ref[...] += 0.5 * _dot(act, wo_buf[slot].astype(bf16))
        return carry

    lax.fori_loop(0, nj, body, 0)


def _ffn(h, g, wi, wo, layer, *, tm, fc, mix=None, wout=None, n_rows=None):
    d = h.shape[1]
    n = h.shape[0] if n_rows is None else n_rows
    assert n % tm == 0
    dff = wo.shape[1]
    nj = dff // fc
    fuse = mix is not None
    row = lambda i: (i, 0)
    in_specs = [pl.BlockSpec((tm, d), row)]
    args = [h]
    if fuse:
        in_specs += [pl.BlockSpec((tm, mix.shape[1]), row), pl.BlockSpec(wout.shape, lambda i: (0, 0))]
        args += [mix, wout]
    in_specs += [
        pl.BlockSpec((1, d), lambda i: (0, 0)),
        pl.BlockSpec(memory_space=pl.ANY),
        pl.BlockSpec(memory_space=pl.ANY),
    ]
    args += [g.reshape(1, d), wi, wo]
    return pl.pallas_call(
        functools.partial(_ffn_kernel, fuse_out=fuse, layer=layer, fc=fc, nj=nj),
        grid=(n // tm,),
        in_specs=in_specs,
        out_specs=pl.BlockSpec((tm, d), row),
        out_shape=jax.ShapeDtypeStruct((n, d), f32),
        scratch_shapes=[
            pltpu.VMEM((tm, d), bf16),
            pltpu.VMEM((2, d, fc), f32),
            pltpu.VMEM((2, d, fc), f32),
            pltpu.VMEM((2, fc, d), f32),
            pltpu.SemaphoreType.DMA((3, 2)),
        ],
        compiler_params=_cparams(("arbitrary",)),
        name="ffn_out" if fuse else "ffn",
    )(*args)


def _rel_bucket(rel):
    half = REL_BUCKETS // 2
    max_exact = half // 2
    n = jnp.abs(rel)
    large = max_exact + (jnp.log(jnp.maximum(n, 1).astype(jnp.float32) / max_exact)
                         / math.log(REL_MAX_DIST / max_exact) * (half - max_exact)).astype(jnp.int32)
    large = jnp.minimum(large, half - 1)
    return jnp.where(rel > 0, half, 0) + jnp.where(n < max_exact, n, large)


def _rel_tiles():
    k = np.arange(LANES)[:, None]
    q = np.arange(LANES)[None, :]
    far = np.full((LANES, LANES), -4 * LANES)
    ones = np.ones((LANES, LANES), bool)
    rels = [k - q, k - q - LANES, far, (k % N_META) - N_META - q, (k % N_META) - (q % N_META), far, k - q - LANES]
    vis = [(k // CHUNK) <= (q // CHUNK), ones, ones, ones, (k // N_META) == (q // N_META), ~ones,
           (q < CHUNK) | (k >= CHUNK)]
    return (np.stack([np.broadcast_to(a, (LANES, LANES)) for a in rels]).astype(np.int32),
            np.stack([np.broadcast_to(a, (LANES, LANES)) for a in vis]).astype(np.int32))


def _bias_kernel(rb_ref, bucket_ref, vis_ref, o_ref):
    h = pl.program_id(0)
    for kind in range(N_KINDS):
        bk = bucket_ref[kind]
        acc = jnp.zeros((LANES, LANES), f32)
        for b in range(REL_BUCKETS):
            acc = jnp.where(bk == b, rb_ref[b, h], acc)
        o_ref[kind, 0] = jnp.where(vis_ref[kind] != 0, acc * LOG2E, NEG_INF)


def _bias_tiles(rel_bias):
    rel, vis = _rel_tiles()
    bucket = _rel_bucket(jnp.asarray(rel))
    nk = N_KINDS
    return pl.pallas_call(
        _bias_kernel,
        grid=(REL_HEADS,),
        in_specs=[
            pl.BlockSpec(memory_space=pltpu.SMEM),
            pl.BlockSpec((nk, LANES, LANES), lambda h: (0, 0, 0)),
            pl.BlockSpec((nk, LANES, LANES), lambda h: (0, 0, 0)),
        ],
        out_specs=pl.BlockSpec((nk, 1, LANES, LANES), lambda h: (0, h, 0, 0)),
        out_shape=jax.ShapeDtypeStruct((nk, REL_HEADS, LANES, LANES), f32),
        compiler_params=_cparams(("arbitrary",)),
        name="bias_tiles",
    )(rel_bias, bucket, jnp.asarray(vis))


def _bias_spec():
    return pl.BlockSpec((N_KINDS, REL_HEADS, LANES, LANES), lambda s: (0, 0, 0, 0))


def _meta_views(rows, cols_t, bsz):
    f = rows.shape[1]
    return (rows.reshape(bsz, N_META, f),
            cols_t.reshape(cols_t.shape[0], bsz, N_META).transpose(1, 0, 2))


A_QD = A_HEADS * 2 * A_HD
A_VR = A_VD + ONES_ROWS


def _proj_a_kernel(h_ref, g_ref, w_ref, gq_ref, gk_ref, qs_ref, k_ref, vt_ref):
    xn = (_rms_rows(h_ref[...]) * g_ref[...]).astype(bf16)
    y = _dot(xn, w_ref[...])
    q = _group_rms(y[:, :A_QD], A_HD) * gq_ref[...]
    lo = _lo_half_mask(q.shape, 2 * A_HD, A_HD)
    q_lo = jnp.where(lo, q, 0.0).astype(bf16)
    q_hi = jnp.where(lo, 0.0, q).astype(bf16)
    k_ref[...] = (_group_rms(y[:, A_QD:2 * A_QD], A_HD) * gk_ref[...]).astype(bf16)
    ones = jnp.ones((ONES_ROWS, LANES), bf16)
    for t in range(vt_ref.shape[0]):
        rows = slice(t * LANES, (t + 1) * LANES)
        vt = y[rows, 2 * A_QD:].T.astype(bf16)
        for h in range(A_HEADS):
            qs_ref[t, h, :LANES, :] = q_lo[rows, h * A_VD:(h + 1) * A_VD]
            qs_ref[t, h, LANES:, :] = q_hi[rows, h * A_VD:(h + 1) * A_VD]
            vt_ref[t, h * A_VR:h * A_VR + A_VD, :] = vt[h * A_VD:(h + 1) * A_VD]
            vt_ref[t, h * A_VR + A_VD:(h + 1) * A_VR, :] = ones


def _proj_a(h, g, w, qk_norm, *, tm):
    n, d = h.shape
    nw = w.shape[1]
    gq = jnp.tile(qk_norm[0] * (A_HD ** -0.5 * LOG2E), A_QD // A_HD).reshape(1, A_QD)
    gk = jnp.tile(qk_norm[1], A_QD // A_HD).reshape(1, A_QD)
    return pl.pallas_call(
        _proj_a_kernel,
        grid=(n // tm,),
        in_specs=[
            pl.BlockSpec((tm, d), lambda i: (i, 0)),
            pl.BlockSpec((1, d), lambda i: (0, 0)),
            pl.BlockSpec((d, nw), lambda i: (0, 0)),
            pl.BlockSpec((1, A_QD), lambda i: (0, 0)),
            pl.BlockSpec((1, A_QD), lambda i: (0, 0)),
        ],
        out_specs=[
            pl.BlockSpec((tm // LANES, A_HEADS, 2 * LANES, A_VD), lambda i: (i, 0, 0, 0)),
            pl.BlockSpec((tm, A_QD), lambda i: (i, 0)),
            pl.BlockSpec((tm // LANES, A_HEADS * A_VR, LANES), lambda i: (i, 0, 0)),
        ],
        out_shape=[
            jax.ShapeDtypeStruct((n // LANES, A_HEADS, 2 * LANES, A_VD), bf16),
            jax.ShapeDtypeStruct((n, A_QD), bf16),
            jax.ShapeDtypeStruct((n // LANES, A_HEADS * A_VR, LANES), bf16),
        ],
        compiler_params=_cparams(("arbitrary",)),
        name="proj_a",
    )(h, g.reshape(1, d), w, gq, gk)


def _attn_a_kernel(qs_ref, k_ref, vt_ref, km_ref, vtm_ref, kmf_ref, vtmf_ref, bias_ref, lam_ref, sub_ref,
                   o_ref, m_ref, acc_ref, gap_ref, *, nblk, lambda_init):
    s_id = pl.program_id(0)
    nb_real = pl.num_programs(0) - 1
    hw = 2 * A_HD

    def init():
        m_ref[...] = jnp.full(m_ref.shape, NEG_INF, f32)
        acc_ref[...] = jnp.zeros(acc_ref.shape, f32)

    def step_fns(kt_fn, vtt_fns, bias_fn):
        def logits():
            sts = [_dot_nt(kt_fn(h), qs_ref[0, h]) for h in range(A_HEADS)]
            if bias_fn is not None:
                sts = [st + bias_fn(h) for h, st in enumerate(sts)]
            return jnp.concatenate(sts, axis=1)

        def pv(pb):
            outs = []
            for h in range(A_HEADS):
                acc = None
                for vtt_fn, r0 in vtt_fns:
                    vtt = vtt_fn(h)
                    part = _dot(vtt, pb[r0:r0 + vtt.shape[1], h * 2 * LANES:(h + 1) * 2 * LANES])
                    acc = part if acc is None else acc + part
                outs.append(acc)
            return jnp.concatenate(outs, axis=1)

        return logits, pv

    def bias_rows(kind, h, rows):
        return jnp.concatenate([bias_ref[kind, h, 0:rows, :], bias_ref[kind, A_HEADS + h, 0:rows, :]], axis=1)

    def finalize():
        lam = lam_ref[...]
        lam_full = (jnp.exp(jnp.sum(lam[0:1] * lam[1:2], axis=-1, keepdims=True))
                    - jnp.exp(jnp.sum(lam[2:3] * lam[3:4], axis=-1, keepdims=True)) + lambda_init)
        o = acc_ref[0:A_VD, :] * (1.0 / acc_ref[A_VD:A_VD + 1, :])
        for h in range(A_HEADS):
            d = o[:, 2 * h * LANES:(2 * h + 1) * LANES] - lam_full * o[:, (2 * h + 1) * LANES:(2 * h + 2) * LANES]
            d = d * lax.rsqrt(jnp.mean(d * d, axis=0, keepdims=True) + EPS) * sub_ref[...] * (1.0 - lambda_init)
            o_ref[:, h * hw:(h + 1) * hw] = d.T.astype(bf16)

    @pl.when(s_id < nb_real)
    def _():
        i = s_id % nblk
        kind_m = jnp.where(i == 0, KIND_META0, KIND_FAR)
        nfar, near0, nnear = _sweep_steps(i)

        def k_fn(b0, nsub):
            off = pl.multiple_of(b0 * LANES, NSUB * LANES)
            return lambda h: k_ref[pl.ds(off, nsub * LANES), h * hw:(h + 1) * hw]

        def vt_fn(b0, nsub):
            return lambda h: jnp.concatenate([vt_ref[b0 + t, h * A_VR:(h + 1) * A_VR, :] for t in range(nsub)], axis=1)

        def near_bias(b0):
            kinds = [_block_kind(b0 + t - i) for t in range(NSUB)]
            return [lambda h, kind=kind: bias_rows(kind, h, LANES) for kind in kinds]

        def first_fns():
            biases = [lambda h: bias_rows(kind_m, h, N_META)] + near_bias(0)
            return step_fns(
                lambda h: jnp.concatenate([km_ref[0, :, h * hw:(h + 1) * hw], k_fn(0, NSUB)(h)], axis=0),
                [(lambda h: vtm_ref[0, h * A_VR:(h + 1) * A_VR, :], 0), (vt_fn(0, NSUB), N_META)],
                lambda h: jnp.concatenate([b(h) for b in biases], axis=0))

        def far_step(w):
            b0 = NSUB + w * NSUB_FAR
            far_bias = jnp.concatenate([bias_rows(KIND_FAR, h, 1) for h in range(A_HEADS)], axis=1)
            return step_fns(k_fn(b0, NSUB_FAR), [(vt_fn(b0, NSUB_FAR), 0)], None) + (far_bias,)

        def near_step(u):
            b0 = near0 + u * NSUB
            biases = near_bias(b0)
            return step_fns(k_fn(b0, NSUB), [(vt_fn(b0, NSUB), 0)],
                            lambda h: jnp.concatenate([b(h) for b in biases], axis=0)) + (None,)

        def sweep(gap_ref):
            init()
            _softmax_step(*first_fns(), m_ref, acc_ref, gap_ref)
            _softmax_loop(0, nfar, far_step, m_ref, acc_ref, gap_ref)
            _softmax_loop(0, nnear, near_step, m_ref, acc_ref, gap_ref)

        gap_ref[...] = jnp.full(gap_ref.shape, NEG_INF, f32)
        sweep(gap_ref)
        pl.when(_lazy_failed(gap_ref, acc_ref[A_VD:A_VD + 1, :]))(lambda: sweep(None))
        finalize()

    @pl.when(s_id == nb_real)
    def _():
        init()
        _softmax_step(*step_fns(lambda h: kmf_ref[:, h * hw:(h + 1) * hw],
                                [(lambda h: vtmf_ref[0, h * A_VR:(h + 1) * A_VR, :], 0)],
                                lambda h: bias_rows(KIND_METAMETA, h, LANES)), m_ref, acc_ref)
        finalize()


def _attn_a(qs, k, vt, bias, lam, subln, *, bsz, nblk, lambda_init):
    n = k.shape[0]
    nb_real = bsz * nblk
    seq = nblk * LANES
    d = A_QD
    vr = A_HEADS * A_VR
    assert nblk % NSUB == 0
    km, vtm = _meta_views(k[nb_real * LANES:], vt[nb_real], bsz)
    kern = functools.partial(_attn_a_kernel, nblk=nblk, lambda_init=lambda_init)
    bclamp = lambda s: jnp.minimum(s // nblk, bsz - 1)
    return pl.pallas_call(
        kern,
        grid=(nb_real + 1,),
        in_specs=[
            pl.BlockSpec((1,) + qs.shape[1:], lambda s: (s, 0, 0, 0)),
            pl.BlockSpec((seq, d), lambda s: (bclamp(s), 0)),
            pl.BlockSpec((nblk, vr, LANES), lambda s: (bclamp(s), 0, 0)),
            pl.BlockSpec((1, N_META, d), lambda s: (bclamp(s), 0, 0)),
            pl.BlockSpec((1, vr, N_META), lambda s: (bclamp(s), 0, 0)),
            pl.BlockSpec((LANES, d), lambda s: (nb_real, 0)),
            pl.BlockSpec((1, vr, LANES), lambda s: (nb_real, 0, 0)),
            _bias_spec(),
            pl.BlockSpec((4, A_HD), lambda s: (0, 0)),
            pl.BlockSpec((A_VD, LANES), lambda s: (0, 0)),
        ],
        out_specs=pl.BlockSpec((LANES, d), lambda s: (s, 0)),
        out_shape=jax.ShapeDtypeStruct((n, d), bf16),
        scratch_shapes=[
            pltpu.VMEM((1, A_HEADS * 2 * LANES), f32),
            pltpu.VMEM((A_VR, A_HEADS * 2 * LANES), f32),
            pltpu.VMEM((1, A_HEADS * 2 * LANES), f32),
        ],
        compiler_params=_cparams(("arbitrary",)),
        name="attn_a",
    )(qs, k, vt, km, vtm, k, vt, bias, lam, jnp.broadcast_to(subln[:, None], (A_VD, LANES)))


C_QD = C_Q_HEADS * C_HD
C_KD = 2 * C_KV_HEADS * C_HD
C_VR = 2 * C_HD + ONES_ROWS


def _proj_c_kernel(h_ref, g_ref, w_ref, gq_ref, gk_ref, qs_ref, k_ref, vt_ref):
    xn = (_rms_rows(h_ref[...]) * g_ref[...]).astype(bf16)
    y = _dot(xn, w_ref[...])
    q = _group_rms(y[:, :C_QD], C_HD) * gq_ref[...]
    lo = _lo_half_mask(q.shape, 2 * C_HD, C_HD)
    q_even = jnp.where(lo, q, 0.0).astype(bf16)
    q_odd = jnp.where(lo, 0.0, q).astype(bf16)
    k_ref[...] = (_group_rms(y[:, C_QD:C_QD + C_KD], C_HD) * gk_ref[...]).astype(bf16)
    ones = jnp.ones((ONES_ROWS, LANES), bf16)
    for t in range(vt_ref.shape[0]):
        rows = slice(t * LANES, (t + 1) * LANES)
        vt = y[rows, C_QD + C_KD:].T.astype(bf16)
        for g in range(C_KV_HEADS):
            for hh in range(C_GROUP):
                pair = (g * C_GROUP + hh) // 2
                src = q_even if hh % 2 == 0 else q_odd
                qs_ref[t, g, hh * LANES:(hh + 1) * LANES, :] = src[rows, pair * LANES:(pair + 1) * LANES]
            vt_ref[t, g * C_VR:g * C_VR + 2 * C_HD, :] = vt[g * 2 * C_HD:(g + 1) * 2 * C_HD]
            vt_ref[t, g * C_VR + 2 * C_HD:(g + 1) * C_VR, :] = ones


def _proj_c(h, g, w, qk_norm, *, tm):
    n, d = h.shape
    nw = w.shape[1]
    nt = tm // LANES
    gq = jnp.tile(qk_norm[0] * (C_HD ** -0.5 * LOG2E), C_QD // C_HD).reshape(1, C_QD)
    gk = jnp.tile(qk_norm[1], C_KD // C_HD).reshape(1, C_KD)
    return pl.pallas_call(
        _proj_c_kernel,
        grid=(n // tm,),
        in_specs=[
            pl.BlockSpec((tm, d), lambda i: (i, 0)),
            pl.BlockSpec((1, d), lambda i: (0, 0)),
            pl.BlockSpec((d, nw), lambda i: (0, 0)),
            pl.BlockSpec((1, C_QD), lambda i: (0, 0)),
            pl.BlockSpec((1, C_KD), lambda i: (0, 0)),
        ],
        out_specs=[
            pl.BlockSpec((nt, C_KV_HEADS, C_GROUP * LANES, LANES), lambda i: (i, 0, 0, 0)),
            pl.BlockSpec((tm, C_KD), lambda i: (i, 0)),
            pl.BlockSpec((nt, C_KV_HEADS * C_VR, LANES), lambda i: (i, 0, 0)),
        ],
        out_shape=[
            jax.ShapeDtypeStruct((n // LANES, C_KV_HEADS, C_GROUP * LANES, LANES), bf16),
            jax.ShapeDtypeStruct((n, C_KD), bf16),
            jax.ShapeDtypeStruct((n // LANES, C_KV_HEADS * C_VR, LANES), bf16),
        ],
        compiler_params=_cparams(("arbitrary",)),
        name="proj_c",
    )(h, g.reshape(1, d), w, gq, gk)


def _attn_c_kernel(sink_ref, qs_ref, k_ref, vt_ref, km_ref, vtm_ref, kmf_ref, vtmf_ref, bias_ref, o_ref, *, nblk):
    s_id = pl.program_id(0)
    nb_real = pl.num_programs(0) - 1
    r_io, _ = _tile_iotas()
    vd = 2 * C_HD

    def attend_with(tiles, lazy):
        top = r_io < C_HD
        failed = None
        for g in range(C_KV_HEADS):
            sink = jnp.concatenate(
                [jnp.full((1, LANES), sink_ref[g * C_GROUP + hh] * LOG2E, f32) for hh in range(C_GROUP)], axis=1)
            sts = []
            m = sink
            for (k_fn, vt_fn, bias_fn) in tiles:
                st = _dot_nt(k_fn(g), qs_ref[0, g])
                st = st + jnp.concatenate([bias_fn(g * C_GROUP + hh) for hh in range(C_GROUP)], axis=1)
                m = jnp.maximum(m, jnp.max(st, axis=0, keepdims=True))
                sts.append(st)
            acc = None
            for st, (k_fn, vt_fn, bias_fn) in zip(sts, tiles):
                pv = _dot(vt_fn(g), jnp.exp2(st if lazy else st - m).astype(bf16))
                acc = pv if acc is None else acc + pv
            denom = acc[vd:vd + 1, :] + jnp.exp2(sink if lazy else sink - m)
            if lazy:
                bad = jnp.logical_not((jnp.max(m) <= LAZY_GAP) & (jnp.min(denom) >= LAZY_FLOOR))
                failed = bad if failed is None else failed | bad
            o = acc[0:vd, :] * (1.0 / denom)
            for cc in range(C_GROUP // 2):
                even = o[:, (2 * cc) * LANES:(2 * cc + 1) * LANES]
                odd = o[:, (2 * cc + 1) * LANES:(2 * cc + 2) * LANES]
                col = (g * (C_GROUP // 2) + cc) * LANES
                o_ref[:, col:col + LANES] = jnp.where(top, even, odd).T.astype(bf16)
        return failed

    def attend(tiles):
        failed = attend_with(tiles, True)

        @pl.when(failed)
        def _():
            attend_with(tiles, False)

    @pl.when(s_id < nb_real)
    def _():
        i = s_id % nblk
        prev = jnp.maximum(i - 1, 0)
        poff = pl.multiple_of(prev * LANES, LANES)
        coff = pl.multiple_of(i * LANES, LANES)
        kind_m = jnp.where(i == 0, KIND_META0, KIND_FAR)
        kind_p = jnp.where(i == 0, KIND_MASKED, KIND_PREVWIN)
        attend([
            (lambda g: km_ref[0, :, g * LANES:(g + 1) * LANES], lambda g: vtm_ref[0, g * C_VR:(g + 1) * C_VR, :],
             lambda h: bias_ref[kind_m, h, 0:N_META, :]),
            (lambda g: k_ref[pl.ds(poff, LANES), g * LANES:(g + 1) * LANES],
             lambda g: vt_ref[prev, g * C_VR:(g + 1) * C_VR, :], lambda h: bias_ref[kind_p, h]),
            (lambda g: k_ref[pl.ds(coff, LANES), g * LANES:(g + 1) * LANES],
             lambda g: vt_ref[i, g * C_VR:(g + 1) * C_VR, :], lambda h: bias_ref[KIND_DIAG, h]),
        ])

    @pl.when(s_id == nb_real)
    def _():
        attend([(lambda g: kmf_ref[:, g * LANES:(g + 1) * LANES], lambda g: vtmf_ref[0, g * C_VR:(g + 1) * C_VR, :],
                 lambda h: bias_ref[KIND_METAMETA, h])])


def _attn_c(qs, k, vt, bias, sinks, *, bsz, nblk):
    n = k.shape[0]
    nb_real = bsz * nblk
    seq = nblk * LANES
    vr = C_KV_HEADS * C_VR
    kern = functools.partial(_attn_c_kernel, nblk=nblk)
    bclamp = lambda s: jnp.minimum(s // nblk, bsz - 1)
    km, vtm = _meta_views(k[nb_real * LANES:], vt[nb_real], bsz)
    return pl.pallas_call(
        kern,
        grid=(nb_real + 1,),
        in_specs=[
            pl.BlockSpec(memory_space=pltpu.SMEM),
            pl.BlockSpec((1,) + qs.shape[1:], lambda s: (s, 0, 0, 0)),
            pl.BlockSpec((seq, C_KD), lambda s: (bclamp(s), 0)),
            pl.BlockSpec((nblk, vr, LANES), lambda s: (bclamp(s), 0, 0)),
            pl.BlockSpec((1, N_META, C_KD), lambda s: (bclamp(s), 0, 0)),
            pl.BlockSpec((1, vr, N_META), lambda s: (bclamp(s), 0, 0)),
            pl.BlockSpec((LANES, C_KD), lambda s: (nb_real, 0)),
            pl.BlockSpec((1, vr, LANES), lambda s: (nb_real, 0, 0)),
            _bias_spec(),
        ],
        out_specs=pl.BlockSpec((LANES, C_QD), lambda s: (s, 0)),
        out_shape=jax.ShapeDtypeStruct((n, C_QD), bf16),
        compiler_params=_cparams(("arbitrary",)),
        name="attn_c",
    )(sinks, qs, k, vt, km, vtm, k, vt, bias)


B_QA = B_HEADS * B_KV_RANK
B_QI = IDX_HEADS * IDX_DIM
B_W1 = 2 * B_Q_RANK + 2 * LANES
B_TR = B_KV_RANK + ONES_ROWS


def _proj_b_kernel(h_ref, g_ref, w1_ref, ln_ref, wuq_ref, qn_ref,
                   qa_ref, qi_ref, ckv_ref, ckvt_ref, kk_ref, wit_ref):
    xn = (_rms_rows(h_ref[...]) * g_ref[...]).astype(bf16)
    y = _dot(xn, w1_ref[...])
    r = B_Q_RANK
    cq = (_rms_rows(y[:, :r]) * ln_ref[0:1, :]).astype(bf16)
    ckv = _rms_rows(y[:, r:2 * r]) * ln_ref[1:2, :]
    ckv_ref[...] = ckv.astype(bf16)
    kk_ref[...] = _rms_rows(y[:, 2 * r:2 * r + LANES]).astype(bf16)
    wi = y[:, 2 * r + LANES:] * (IDX_HEADS ** -0.5)
    ones = jnp.ones((ONES_ROWS, LANES), bf16)
    for t in range(ckvt_ref.shape[0]):
        ckvt_ref[t, 0:r, :] = ckv[t * LANES:(t + 1) * LANES, :].T.astype(bf16)
        ckvt_ref[t, r:, :] = ones
        wit_ref[t] = wi[t * LANES:(t + 1) * LANES, :].T[0:IDX_HEADS, :]
    z = _dot(cq, wuq_ref[...])
    qa = (_group_rms(z[:, :B_QA], B_KV_RANK) * qn_ref[...]).astype(bf16)
    qi = z[:, B_QA:] * (IDX_DIM ** -0.5)
    lo = _lo_half_mask(qi.shape, 2 * IDX_DIM, IDX_DIM)
    qi_lo = jnp.where(lo, qi, 0.0).astype(bf16)
    qi_hi = jnp.where(lo, 0.0, qi).astype(bf16)
    for t in range(qa_ref.shape[0]):
        rows = slice(t * LANES, (t + 1) * LANES)
        for hd in range(B_HEADS):
            qa_ref[t, hd * LANES:(hd + 1) * LANES, :] = qa[rows, hd * r:(hd + 1) * r]
        for hh in range(IDX_HEADS):
            src = qi_lo if hh % 2 == 0 else qi_hi
            qi_ref[t, hh * LANES:(hh + 1) * LANES, :] = src[rows, (hh // 2) * LANES:(hh // 2 + 1) * LANES]


def _proj_b(h, g, w1, latent_norm, wuq, q_norm, *, tm):
    n, d = h.shape
    qn = jnp.tile(q_norm * (B_KV_RANK ** -0.5 * LOG2E), B_HEADS).reshape(1, B_QA)
    row = lambda i: (i, 0)
    row3 = lambda i: (i, 0, 0)
    const = lambda i: (0, 0)
    nt = tm // LANES
    return pl.pallas_call(
        _proj_b_kernel,
        grid=(n // tm,),
        in_specs=[
            pl.BlockSpec((tm, d), row),
            pl.BlockSpec((1, d), const),
            pl.BlockSpec(w1.shape, const),
            pl.BlockSpec(latent_norm.shape, const),
            pl.BlockSpec(wuq.shape, const),
            pl.BlockSpec((1, B_QA), const),
        ],
        out_specs=[
            pl.BlockSpec((nt, B_HEADS * LANES, B_KV_RANK), row3),
            pl.BlockSpec((nt, IDX_HEADS * LANES, LANES), row3),
            pl.BlockSpec((tm, B_KV_RANK), row),
            pl.BlockSpec((nt, B_TR, LANES), row3),
            pl.BlockSpec((tm, LANES), row),
            pl.BlockSpec((nt, IDX_HEADS, LANES), row3),
        ],
        out_shape=[
            jax.ShapeDtypeStruct((n // LANES, B_HEADS * LANES, B_KV_RANK), bf16),
            jax.ShapeDtypeStruct((n // LANES, IDX_HEADS * LANES, LANES), bf16),
            jax.ShapeDtypeStruct((n, B_KV_RANK), bf16),
            jax.ShapeDtypeStruct((n // LANES, B_TR, LANES), bf16),
            jax.ShapeDtypeStruct((n, LANES), bf16),
            jax.ShapeDtypeStruct((n // LANES, IDX_HEADS, LANES), f32),
        ],
        compiler_params=_cparams(("arbitrary",)),
        name="proj_b",
    )(h, g.reshape(1, d), w1, latent_norm, wuq, qn)


def _attn_b_kernel(qs_ref, is_ref, wit_ref, ckv_ref, ckvt_ref, kk_ref, ckvm_ref, ckvtm_ref, kkm_ref,
                   ckvmf_ref, ckvtmf_ref, bias_ref, wuvt_ref,
                   o_ref, key_ref, pen_ref, m_ref, acc_ref, gap_ref, *, nblk, k_sel):
    s_id = pl.program_id(0)
    nb_real = pl.num_programs(0) - 1
    r_io, c_io = _tile_iotas()
    rk = B_KV_RANK

    def init():
        m_ref[...] = jnp.full(m_ref.shape, NEG_INF, f32)
        acc_ref[...] = jnp.zeros(acc_ref.shape, f32)

    def add_per_head(st, bias_fn, pen):
        cols = []
        for h in range(B_HEADS):
            add = pen if bias_fn is None else (bias_fn(h) if pen is None else bias_fn(h) + pen)
            cols.append(st[:, h * LANES:(h + 1) * LANES] + add)
        return jnp.concatenate(cols, axis=1)

    def step_fns(ckv_fn, ckvt_fns, bias_fn, pen_fn):
        def logits():
            st = _dot_nt(ckv_fn(), qs_ref[0])
            return add_per_head(st, bias_fn, None if pen_fn is None else pen_fn())

        def pv(pb):
            acc = None
            for ckvt_fn, r0 in ckvt_fns:
                ckvt = ckvt_fn()
                part = _dot(ckvt, pb[r0:r0 + ckvt.shape[1], :])
                acc = part if acc is None else acc + part
            return acc

        return logits, pv

    def finalize():
        olat = (acc_ref[0:rk, :] * (1.0 / acc_ref[rk:rk + 1, :])).astype(bf16)
        ot = jnp.concatenate([_dot(wuvt_ref[h], olat[:, h * LANES:(h + 1) * LANES]) for h in range(B_HEADS)], axis=0)
        o_ref[...] = ot.T.astype(bf16)

    def index_scores(kk):
        s = jnp.maximum(_dot_nt(kk, is_ref[0]), 0.0)
        wt = wit_ref[0]
        sc = jnp.zeros((kk.shape[0], LANES), f32)
        for hh in range(IDX_HEADS):
            sc = sc + wt[hh:hh + 1, :] * s[:, hh * LANES:(hh + 1) * LANES]
        return sc

    def sort_key(sc):
        bits = lax.bitcast_convert_type(sc + 0.0, jnp.int32)
        return jnp.where(bits < 0, bits ^ jnp.int32(0x7FFFFFFF), bits)

    @pl.when(s_id < nb_real)
    def _():
        i = s_id % nblk
        ntile = i + 2

        int_min_tile = jnp.full((LANES, LANES), INT_MIN, jnp.int32)
        key_ref[0] = int_min_tile
        key_ref[0, 0:N_META, :] = sort_key(index_scores(kkm_ref[0]))
        key_ref[i + 2] = int_min_tile

        def score_body(jp, carry):
            off = pl.multiple_of(jp * (2 * LANES), 2 * LANES)
            keys = sort_key(index_scores(kk_ref[pl.ds(off, 2 * LANES), :]))
            for t in range(2):
                j = _vec(2 * jp + t)
                vis = (j < i) | ((j == i) & ((r_io >> 6) <= (c_io >> 6)))
                key_ref[2 * jp + t + 1] = jnp.where(vis, keys[t * LANES:(t + 1) * LANES], jnp.int32(INT_MIN))
            return carry

        lax.fori_loop(0, i // 2 + 1, score_body, 0)

        def count(pred):
            def cbody(tp, accv):
                for t in (2 * tp, 2 * tp + 1):
                    accv = accv + jnp.where(pred(key_ref[t], t), 1.0, 0.0)
                return accv
            accv = lax.fori_loop(0, (ntile + 1) // 2, cbody, jnp.zeros((LANES, LANES), f32))
            return jnp.sum(accv, axis=0, keepdims=True)

        kf = float(k_sel)
        zero = jnp.zeros((1, LANES), jnp.int32)
        t0 = jnp.where(count(lambda k, t: k >= zero) >= kf, zero, jnp.int32(INT_MIN))

        def bit_body(it, tcur):
            cand = tcur | jnp.left_shift(jnp.int32(1), 30 - it)
            return jnp.where(count(lambda k, t: k >= cand) >= kf, cand, tcur)

        thr = lax.fori_loop(0, 31, bit_body, t0)

        need = kf - count(lambda k, t: k > thr)
        n_eq = count(lambda k, t: k == thr)
        has_thr = thr > jnp.int32(INT_MIN)
        tied = jnp.max(jnp.where(has_thr & (n_eq > need), 1.0, 0.0)) > 0.0

        def tie_search(_):
            def jbody(it, jcur):
                cand = jcur | jnp.left_shift(jnp.int32(1), 11 - it)
                cnt = count(lambda k, t: (k == thr) & ((t * LANES + r_io) < cand))
                return jnp.where(cnt < need, cand, jcur)
            return lax.fori_loop(0, 12, jbody, jnp.zeros((1, LANES), jnp.int32))

        j_last = lax.cond(tied, tie_search, lambda _: jnp.full((1, LANES), 4095, jnp.int32), 0)
        j_last = jnp.where(has_thr, j_last, -1)

        def pen_body(t, carry):
            k = key_ref[t]
            sel = (k > thr) | ((k == thr) & ((t * LANES + r_io) <= j_last))
            pen_ref[t] = jnp.where(sel, 0.0, NEG_INF)
            return carry

        lax.fori_loop(0, ntile, pen_body, 0)
        for t in range(1, NSUB):
            pen_ref[i + 1 + t] = jnp.full((LANES, LANES), NEG_INF, f32)

        kind_m = jnp.where(i == 0, KIND_META0, KIND_FAR)
        nfar, near0, nnear = _sweep_steps(i)

        def ckv_fn(b0, nsub):
            off = pl.multiple_of(b0 * LANES, NSUB * LANES)
            return lambda: ckv_ref[pl.ds(off, nsub * LANES), :]

        def ckvt_fn(b0, nsub):
            return lambda: jnp.concatenate([ckvt_ref[b0 + t] for t in range(nsub)], axis=1)

        def pen_fn(b0, nsub):
            return lambda: jnp.concatenate([pen_ref[b0 + t + 1] for t in range(nsub)], axis=0)

        def near_bias(b0):
            kinds = [_block_kind(b0 + t - i) for t in range(NSUB)]
            return lambda h: jnp.concatenate([bias_ref[kinds[t], h] for t in range(NSUB)], axis=0)

        def first_pen():
            return jnp.concatenate([pen_ref[0, 0:N_META, :], pen_fn(0, NSUB)()], axis=0)

        def first_fns():
            bias01 = near_bias(0)
            return step_fns(lambda: jnp.concatenate([ckvm_ref[0], ckv_fn(0, NSUB)()], axis=0),
                            [(lambda: ckvtm_ref[0], 0), (ckvt_fn(0, NSUB), N_META)],
                            lambda h: jnp.concatenate([bias_ref[kind_m, h, 0:N_META, :], bias01(h)], axis=0),
                            first_pen)

        def far_step(w):
            b0 = NSUB + w * NSUB_FAR
            far_bias = jnp.concatenate([bias_ref[KIND_FAR, h, 0:1, :] for h in range(B_HEADS)], axis=1)
            return step_fns(ckv_fn(b0, NSUB_FAR), [(ckvt_fn(b0, NSUB_FAR), 0)], None, pen_fn(b0, NSUB_FAR)) + (far_bias,)

        def near_step(u):
            b0 = near0 + u * NSUB
            return step_fns(ckv_fn(b0, NSUB), [(ckvt_fn(b0, NSUB), 0)], near_bias(b0), pen_fn(b0, NSUB)) + (None,)

        def sweep(gap_ref):
            init()
            _softmax_step(*first_fns(), m_ref, acc_ref, gap_ref)
            _softmax_loop(0, nfar, far_step, m_ref, acc_ref, gap_ref)
            _softmax_loop(0, nnear, near_step, m_ref, acc_ref, gap_ref)

        gap_ref[...] = jnp.full(gap_ref.shape, NEG_INF, f32)
        sweep(gap_ref)
        pl.when(_lazy_failed(gap_ref, acc_ref[rk:rk + 1, :]))(lambda: sweep(None))
        finalize()

    @pl.when(s_id == nb_real)
    def _():
        init()
        _softmax_step(*step_fns(lambda: ckvmf_ref[...], [(lambda: ckvtmf_ref[0], 0)],
                                lambda h: bias_ref[KIND_METAMETA, h], None), m_ref, acc_ref)
        finalize()


def _attn_b(qa, qi, wit, ckv, ckvt, kk, bias, wuvt, *, bsz, nblk, k_sel):
    n = ckv.shape[0]
    nb_real = bsz * nblk
    seq = nblk * LANES
    assert k_sel >= N_META and (nblk + 1) * LANES <= 4096 and nblk % NSUB == 0
    ckvm, ckvtm = _meta_views(ckv[nb_real * LANES:], ckvt[nb_real], bsz)
    kkm = kk[nb_real * LANES:].reshape(bsz, N_META, LANES)
    kern = functools.partial(_attn_b_kernel, nblk=nblk, k_sel=k_sel)
    bidx = lambda s: jnp.minimum(s // nblk, bsz - 1)
    blk = lambda s: (s, 0)
    return pl.pallas_call(
        kern,
        grid=(nb_real + 1,),
        in_specs=[
            pl.BlockSpec((1,) + qa.shape[1:], lambda s: (s, 0, 0)),
            pl.BlockSpec((1,) + qi.shape[1:], lambda s: (s, 0, 0)),
            pl.BlockSpec((1, IDX_HEADS, LANES), lambda s: (s, 0, 0)),
            pl.BlockSpec((seq, B_KV_RANK), lambda s: (bidx(s), 0)),
            pl.BlockSpec((nblk, B_TR, LANES), lambda s: (bidx(s), 0, 0)),
            pl.BlockSpec((seq, LANES), lambda s: (bidx(s), 0)),
            pl.BlockSpec((1, N_META, B_KV_RANK), lambda s: (bidx(s), 0, 0)),
            pl.BlockSpec((1, B_TR, N_META), lambda s: (bidx(s), 0, 0)),
            pl.BlockSpec((1, N_META, LANES), lambda s: (bidx(s), 0, 0)),
            pl.BlockSpec((LANES, B_KV_RANK), lambda s: (nb_real, 0)),
            pl.BlockSpec((1, B_TR, LANES), lambda s: (nb_real, 0, 0)),
            _bias_spec(),
            pl.BlockSpec(wuvt.shape, lambda s: (0, 0, 0)),
        ],
        out_specs=pl.BlockSpec((LANES, B_HEADS * B_VD), blk),
        out_shape=jax.ShapeDtypeStruct((n, B_HEADS * B_VD), bf16),
        scratch_shapes=[
            pltpu.VMEM((nblk + 2, LANES, LANES), jnp.int32),
            pltpu.VMEM((nblk + NSUB, LANES, LANES), f32),
            pltpu.VMEM((1, B_HEADS * LANES), f32),
            pltpu.VMEM((B_TR, B_HEADS * LANES), f32),
            pltpu.VMEM((1, B_HEADS * LANES), f32),
        ],
        compiler_params=_cparams(("arbitrary",)),
        name="attn_b",
    )(qa, qi, wit, ckv, ckvt, kk, ckvm, ckvtm, kkm, ckv, ckvt, bias, wuvt)


def kernel(x, meta_tokens, rel_bias, ln_ffn1, ffn1_wi, ffn1_wo, ln_mix, w_out, ln_ffn2, ffn2_wi, ffn2_wo, a_w_in, a_qk_norm, a_lambda, a_subln, b_w_in, b_latent_norm, b_w_uq, b_q_norm, b_w_uv, c_w_in, c_qk_norm, c_sinks):
    bsz, seq, d = x.shape
    assert d == D_MODEL and seq % LANES == 0 and bsz * N_META == LANES
    nblk = seq // LANES
    n = bsz * seq + LANES
    k_sel = min(TOPK_MAX, seq // 4)
    tm_ffn = _row_tile(n, 1408)
    tm_last = _row_tile(bsz * seq, 1408)
    tm_proj = _row_tile(n, 384, LANES)
    fc = 256

    h = jnp.concatenate([x.reshape(bsz * seq, d),
                         jnp.broadcast_to(meta_tokens.astype(x.dtype), (bsz, N_META, d)).reshape(LANES, d)], axis=0)
    bias = _bias_tiles(rel_bias)

    for layer in range(DEPTH):
        h = _ffn(h, ln_ffn1[layer], ffn1_wi, ffn1_wo, layer, tm=tm_ffn, fc=fc)
        kind, j = layer % N_MIXERS, layer // N_MIXERS
        g = ln_mix[layer]
        if kind == 0:
            lambda_init = 0.8 - 0.6 * math.exp(-0.3 * layer)
            qs, k, vt = _proj_a(h, g, a_w_in[j].astype(bf16), a_qk_norm[j], tm=tm_proj)
            mix = _attn_a(qs, k, vt, bias, a_lambda[j], a_subln[j], bsz=bsz, nblk=nblk, lambda_init=lambda_init)
        elif kind == 1:
            w = b_w_in[j]
            r2 = B_Q_RANK + B_KV_RANK
            kcol = w[:, r2:r2 + IDX_DIM]
            w1 = jnp.concatenate([w[:, :r2], kcol, kcol, w[:, r2 + IDX_DIM:],
                                  jnp.zeros((d, LANES - IDX_HEADS), w.dtype)], axis=1).astype(bf16)
            assert w1.shape[1] == B_W1
            qa, qi, ckv, ckvt, kk, wit = _proj_b(h, g, w1, b_latent_norm[j], b_w_uq[j].astype(bf16), b_q_norm[j],
                                                 tm=tm_proj)
            wuvt = jnp.swapaxes(b_w_uv[j], 1, 2).astype(bf16)
            mix = _attn_b(qa, qi, wit, ckv, ckvt, kk, bias, wuvt, bsz=bsz, nblk=nblk, k_sel=k_sel)
        else:
            w = c_w_in[j]
            kcols = [w[:, C_QD + gi * C_HD:C_QD + (gi + 1) * C_HD] for gi in range(C_KV_HEADS)]
            voff = C_QD + C_KV_HEADS * C_HD
            vcols = [w[:, voff + gi * C_HD:voff + (gi + 1) * C_HD] for gi in range(C_KV_HEADS)]
            wc = jnp.concatenate([w[:, :C_QD]] + [kc for kc in kcols for _ in range(2)]
                                 + [vc for vc in vcols for _ in range(2)], axis=1).astype(bf16)
            qs, k, vt = _proj_c(h, g, wc, c_qk_norm[j], tm=tm_proj)
            mix = _attn_c(qs, k, vt, bias, c_sinks[j], bsz=bsz, nblk=nblk)
        last = layer == DEPTH - 1
        h = _ffn(h, ln_ffn2[layer], ffn2_wi, ffn2_wo, layer, tm=tm_last if last else tm_ffn, fc=fc,
                 mix=mix, wout=w_out[layer].astype(bf16), n_rows=bsz * seq if last else None)
    return h.reshape(bsz, seq, d)
```

```python
import functools
import math

import numpy as np
import jax
import jax.numpy as jnp
from jax import lax
from jax.experimental import pallas as pl
from jax.experimental.pallas import tpu as pltpu

D_MODEL = 1024
DEPTH = 4
CHUNK = 64
N_META = 16
N_MIXERS = 3
NEG_INF = -1e30
REL_BUCKETS = 32
REL_MAX_DIST = 128
REL_HEADS = 16
D_FF = 2816
A_HEADS = 8
A_HD = 64
A_VD = 2 * A_HD
B_HEADS = 16
B_Q_RANK = 256
B_KV_RANK = 256
B_VD = 64
IDX_HEADS = 8
IDX_DIM = 64
TOPK_MAX = 256
C_Q_HEADS = 16
C_KV_HEADS = 2
C_GROUP = C_Q_HEADS // C_KV_HEADS
C_HD = 64
EPS = 1e-6

LANES = 128
BF16_ROWS = 16
VMEM_LIMIT = 56 * 1024 * 1024
INT_MIN = -(2 ** 31)
NSUB = 2
NSUB_FAR = 4
LOG2E = math.log2(math.e)
LAZY_GAP = 57.0
LAZY_FLOOR = 2.0 ** -100
ONES_ROWS = BF16_ROWS

KIND_DIAG, KIND_PREV, KIND_FAR, KIND_META0, KIND_METAMETA, KIND_MASKED, KIND_PREVWIN = 0, 1, 2, 3, 4, 5, 6
N_KINDS = 7

f32 = jnp.float32
bf16 = jnp.bfloat16


def _cparams(sem):
    return pltpu.CompilerParams(dimension_semantics=sem, vmem_limit_bytes=VMEM_LIMIT)


def _row_tile(n, cap, mult=BF16_ROWS):
    best = None
    for t in range(mult, cap + 1, mult):
        if n % t == 0:
            best = t
    assert best is not None
    return best


def _dot(a, b):
    return jnp.dot(a, b, preferred_element_type=f32)


def _dot_nt(a, b):
    return lax.dot_general(a, b, (((1,), (1,)), ((), ())), preferred_element_type=f32)


def _rms_rows(x):
    return x * lax.rsqrt(jnp.mean(x * x, axis=-1, keepdims=True) + EPS)


def _lo_half_mask(shape, period, half):
    return (lax.broadcasted_iota(jnp.int32, shape, 1) & (period - 1)) < half


def _group_rms(x, group):
    r, c = x.shape
    outs = []
    if group == 64:
        lo = _lo_half_mask((r, LANES), LANES, 64)
        for ci in range(c // LANES):
            xc = x[:, ci * LANES:(ci + 1) * LANES]
            x2 = xc * xc
            s_lo = jnp.sum(jnp.where(lo, x2, 0.0), axis=-1, keepdims=True)
            s_hi = jnp.sum(jnp.where(lo, 0.0, x2), axis=-1, keepdims=True)
            inv = jnp.where(lo, lax.rsqrt(s_lo * (1.0 / 64) + EPS), lax.rsqrt(s_hi * (1.0 / 64) + EPS))
            outs.append(xc * inv)
    else:
        for gi in range(c // group):
            outs.append(_rms_rows(x[:, gi * group:(gi + 1) * group]))
    return outs[0] if len(outs) == 1 else jnp.concatenate(outs, axis=-1)


def _tile_iotas():
    r = lax.broadcasted_iota(jnp.int32, (LANES, LANES), 0)
    c = lax.broadcasted_iota(jnp.int32, (LANES, LANES), 1)
    return r, c


def _vec(s):
    return jnp.full((LANES, LANES), s, jnp.int32)


def _softmax_step(logits_fn, pv_fn, m_ref, acc_ref, gap_ref=None, offset=None):
    st = logits_fn()
    if offset is not None:
        st = st + offset
    if gap_ref is None:
        m_old = m_ref[...]
        m_new = jnp.maximum(m_old, jnp.max(st, axis=0, keepdims=True))
        acc_ref[...] = jnp.exp2(m_old - m_new) * acc_ref[...] + pv_fn(jnp.exp2(st - m_new).astype(bf16))
        m_ref[...] = m_new
    else:
        gap_ref[...] = jnp.maximum(gap_ref[...], jnp.max(st, axis=0, keepdims=True))
        acc_ref[...] += pv_fn(jnp.exp2(st).astype(bf16))


def _softmax_loop(lo, hi, step_fn, m_ref, acc_ref, gap_ref):
    def body(w, carry):
        logits_fn, pv_fn, offset = step_fn(w)
        _softmax_step(logits_fn, pv_fn, m_ref, acc_ref, gap_ref, offset)
        return carry

    lax.fori_loop(lo, hi, body, 0)


def _lazy_failed(gap_ref, denominators):
    return jnp.logical_not((jnp.max(gap_ref[...]) <= LAZY_GAP) & (jnp.min(denominators) >= LAZY_FLOOR))


def _block_kind(rel):
    return jnp.where(rel < -1, KIND_FAR,
                     jnp.where(rel == -1, KIND_PREV, jnp.where(rel == 0, KIND_DIAG, KIND_MASKED)))


def _sweep_steps(i):
    nfar = jnp.maximum(i - 1 - NSUB, 0) // NSUB_FAR
    near0 = NSUB + nfar * NSUB_FAR
    return nfar, near0, (i - near0 + NSUB) // NSUB


def _ffn_kernel(*refs, fuse_out, tail_rows, layer, fc, nj):
    if fuse_out:
        h_ref, mix_ref, wout_ref, g_ref, wi_hbm, wo_hbm, o_ref, xn_ref, wa_buf, wb_buf, wo_buf, sem = refs
    elif tail_rows:
        h_ref, tail_ref, g_ref, wi_hbm, wo_hbm, o_ref, xn_ref, wa_buf, wb_buf, wo_buf, sem = refs
    else:
        h_ref, g_ref, wi_hbm, wo_hbm, o_ref, xn_ref, wa_buf, wb_buf, wo_buf, sem = refs

    def chunk_copies(j, slot):
        lo = pl.multiple_of(j * fc, fc)
        hi = pl.multiple_of((nj + j) * fc, fc)
        return (pltpu.make_async_copy(wi_hbm.at[layer, :, pl.ds(lo, fc)], wa_buf.at[slot], sem.at[0, slot]),
                pltpu.make_async_copy(wi_hbm.at[layer, :, pl.ds(hi, fc)], wb_buf.at[slot], sem.at[1, slot]),
                pltpu.make_async_copy(wo_hbm.at[layer, pl.ds(lo, fc), :], wo_buf.at[slot], sem.at[2, slot]))

    i = pl.program_id(0)
    first = i * nj

    @pl.when(i == 0)
    def _():
        for c in chunk_copies(0, 0):
            c.start()

    if tail_rows:
        last = pl.num_programs(0) - 1
        nvalid = o_ref.shape[0] - tail_rows

        @pl.when(i < last)
        def _():
            o_ref[...] = h_ref[...]

        @pl.when(i == last)
        def _():
            o_ref[0:nvalid, :] = h_ref[0:nvalid, :]
            o_ref[nvalid:, :] = tail_ref[...]

        r = o_ref[...]
    else:
        r = h_ref[...]
        if fuse_out:
            r = r + _dot(mix_ref[...], wout_ref[...])
        o_ref[...] = r
    xn_ref[...] = (_rms_rows(r) * g_ref[...]).astype(bf16)

    def body(j, carry):
        slot = (first + j) & 1

        @pl.when((j + 1 < nj) | (i + 1 < pl.num_programs(0)))
        def _():
            for c in chunk_copies(jnp.where(j + 1 < nj, j + 1, 0), 1 - slot):
                c.start()

        for c in chunk_copies(j, slot):
            c.wait()
        xn = xn_ref[...]
        a = _dot(xn, wa_buf[slot].astype(bf16))
        b = _dot(xn, wb_buf[slot].astype(bf16))
        act = (a / (1.0 + jnp.exp(-a)) * b).astype(bf16)
        o_ref[...] += 0.5 * _dot(act, wo_buf[slot].astype(bf16))
        return carry

    lax.fori_loop(0, nj, body, 0)


def _ffn(h, g, wi, wo, layer, *, tm, fc, mix=None, wout=None, n_rows=None, tail=None):
    d = h.shape[1]
    tail_rows = 0 if tail is None else tail.shape[0]
    n = h.shape[0] + tail_rows if n_rows is None else n_rows
    assert n % tm == 0 and 0 <= tail_rows < tm and not (tail_rows and (mix is not None or n_rows is not None))
    dff = wo.shape[1]
    nj = dff // fc
    fuse = mix is not None
    row = lambda i: (i, 0)
    in_specs = [pl.BlockSpec((tm, d), row)]
    args = [h]
    if tail_rows:
        in_specs += [pl.BlockSpec(tail.shape, lambda i: (0, 0))]
        args += [tail]
    if fuse:
        in_specs += [pl.BlockSpec((tm, mix.shape[1]), row), pl.BlockSpec(wout.shape, lambda i: (0, 0))]
        args += [mix, wout]
    in_specs += [
        pl.BlockSpec((1, d), lambda i: (0, 0)),
        pl.BlockSpec(memory_space=pl.ANY),
        pl.BlockSpec(memory_space=pl.ANY),
    ]
    args += [g.reshape(1, d), wi, wo]
    return pl.pallas_call(
        functools.partial(_ffn_kernel, fuse_out=fuse, tail_rows=tail_rows, layer=layer, fc=fc, nj=nj),
        grid=(n // tm,),
        in_specs=in_specs,
        out_specs=pl.BlockSpec((tm, d), row),
        out_shape=jax.ShapeDtypeStruct((n, d), f32),
        scratch_shapes=[
            pltpu.VMEM((tm, d), bf16),
            pltpu.VMEM((2, d, fc), f32),
            pltpu.VMEM((2, d, fc), f32),
            pltpu.VMEM((2, fc, d), f32),
            pltpu.SemaphoreType.DMA((3, 2)),
        ],
        compiler_params=_cparams(("arbitrary",)),
        name="ffn_out" if fuse else "ffn",
    )(*args)


def _rel_bucket(rel):
    half = REL_BUCKETS // 2
    max_exact = half // 2
    n = jnp.abs(rel)
    large = max_exact + (jnp.log(jnp.maximum(n, 1).astype(jnp.float32) / max_exact)
                         / math.log(REL_MAX_DIST / max_exact) * (half - max_exact)).astype(jnp.int32)
    large = jnp.minimum(large, half - 1)
    return jnp.where(rel > 0, half, 0) + jnp.where(n < max_exact, n, large)


def _rel_tiles():
    k = np.arange(LANES)[:, None]
    q = np.arange(LANES)[None, :]
    far = np.full((LANES, LANES), -4 * LANES)
    ones = np.ones((LANES, LANES), bool)
    rels = [k - q, k - q - LANES, far, (k % N_META) - N_META - q, (k % N_META) - (q % N_META), far, k - q - LANES]
    vis = [(k // CHUNK) <= (q // CHUNK), ones, ones, ones, (k // N_META) == (q // N_META), ~ones,
           (q < CHUNK) | (k >= CHUNK)]
    return (np.stack([np.broadcast_to(a, (LANES, LANES)) for a in rels]).astype(np.int32),
            np.stack([np.broadcast_to(a, (LANES, LANES)) for a in vis]).astype(np.int32))


def _bias_kernel(rb_ref, bucket_ref, vis_ref, o_ref):
    h = pl.program_id(0)
    for kind in range(N_KINDS):
        bk = bucket_ref[kind]
        acc = jnp.zeros((LANES, LANES), f32)
        for b in range(REL_BUCKETS):
            acc = jnp.where(bk == b, rb_ref[b, h], acc)
        o_ref[kind, 0] = jnp.where(vis_ref[kind] != 0, acc * LOG2E, NEG_INF)


def _bias_tiles(rel_bias):
    rel, vis = _rel_tiles()
    bucket = _rel_bucket(jnp.asarray(rel))
    nk = N_KINDS
    return pl.pallas_call(
        _bias_kernel,
        grid=(REL_HEADS,),
        in_specs=[
            pl.BlockSpec(memory_space=pltpu.SMEM),
            pl.BlockSpec((nk, LANES, LANES), lambda h: (0, 0, 0)),
            pl.BlockSpec((nk, LANES, LANES), lambda h: (0, 0, 0)),
        ],
        out_specs=pl.BlockSpec((nk, 1, LANES, LANES), lambda h: (0, h, 0, 0)),
        out_shape=jax.ShapeDtypeStruct((nk, REL_HEADS, LANES, LANES), f32),
        compiler_params=_cparams(("arbitrary",)),
        name="bias_tiles",
    )(rel_bias, bucket, jnp.asarray(vis))


def _bias_spec():
    return pl.BlockSpec((N_KINDS, REL_HEADS, LANES, LANES), lambda s: (0, 0, 0, 0))


def _meta_views(rows, cols_t, bsz):
    f = rows.shape[1]
    return (rows.reshape(bsz, N_META, f),
            cols_t.reshape(cols_t.shape[0], bsz, N_META).transpose(1, 0, 2))


A_QD = A_HEADS * 2 * A_HD
A_VR = A_VD + ONES_ROWS


def _proj_a_kernel(h_ref, g_ref, w_ref, gq_ref, gk_ref, qs_ref, k_ref, vt_ref):
    xn = (_rms_rows(h_ref[...]) * g_ref[...]).astype(bf16)
    y = _dot(xn, w_ref[...])
    q = _group_rms(y[:, :A_QD], A_HD) * gq_ref[...]
    lo = _lo_half_mask(q.shape, 2 * A_HD, A_HD)
    q_lo = jnp.where(lo, q, 0.0).astype(bf16)
    q_hi = jnp.where(lo, 0.0, q).astype(bf16)
    k_ref[...] = (_group_rms(y[:, A_QD:2 * A_QD], A_HD) * gk_ref[...]).astype(bf16)
    ones = jnp.ones((ONES_ROWS, LANES), bf16)
    for t in range(vt_ref.shape[0]):
        rows = slice(t * LANES, (t + 1) * LANES)
        vt = y[rows, 2 * A_QD:].T.astype(bf16)
        for h in range(A_HEADS):
            qs_ref[t, h, :LANES, :] = q_lo[rows, h * A_VD:(h + 1) * A_VD]
            qs_ref[t, h, LANES:, :] = q_hi[rows, h * A_VD:(h + 1) * A_VD]
            vt_ref[t, h * A_VR:h * A_VR + A_VD, :] = vt[h * A_VD:(h + 1) * A_VD]
            vt_ref[t, h * A_VR + A_VD:(h + 1) * A_VR, :] = ones


def _proj_a(h, g, w, qk_norm, *, tm):
    n, d = h.shape
    nw = w.shape[1]
    gq = jnp.tile(qk_norm[0] * (A_HD ** -0.5 * LOG2E), A_QD // A_HD).reshape(1, A_QD)
    gk = jnp.tile(qk_norm[1], A_QD // A_HD).reshape(1, A_QD)
    return pl.pallas_call(
        _proj_a_kernel,
        grid=(n // tm,),
        in_specs=[
            pl.BlockSpec((tm, d), lambda i: (i, 0)),
            pl.BlockSpec((1, d), lambda i: (0, 0)),
            pl.BlockSpec((d, nw), lambda i: (0, 0)),
            pl.BlockSpec((1, A_QD), lambda i: (0, 0)),
            pl.BlockSpec((1, A_QD), lambda i: (0, 0)),
        ],
        out_specs=[
            pl.BlockSpec((tm // LANES, A_HEADS, 2 * LANES, A_VD), lambda i: (i, 0, 0, 0)),
            pl.BlockSpec((tm, A_QD), lambda i: (i, 0)),
            pl.BlockSpec((tm // LANES, A_HEADS * A_VR, LANES), lambda i: (i, 0, 0)),
        ],
        out_shape=[
            jax.ShapeDtypeStruct((n // LANES, A_HEADS, 2 * LANES, A_VD), bf16),
            jax.ShapeDtypeStruct((n, A_QD), bf16),
            jax.ShapeDtypeStruct((n // LANES, A_HEADS * A_VR, LANES), bf16),
        ],
        compiler_params=_cparams(("arbitrary",)),
        name="proj_a",
    )(h, g.reshape(1, d), w, gq, gk)


def _attn_a_kernel(qs_ref, k_ref, vt_ref, km_ref, vtm_ref, kmf_ref, vtmf_ref, bias_ref, lam_ref, sub_ref,
                   o_ref, m_ref, acc_ref, gap_ref, *, nblk, lambda_init):
    s_id = pl.program_id(0)
    nb_real = pl.num_programs(0) - 1
    hw = 2 * A_HD

    def init():
        m_ref[...] = jnp.full(m_ref.shape, NEG_INF, f32)
        acc_ref[...] = jnp.zeros(acc_ref.shape, f32)

    def step_fns(kt_fn, vtt_fns, bias_fn):
        def logits():
            sts = [_dot_nt(kt_fn(h), qs_ref[0, h]) for h in range(A_HEADS)]
            if bias_fn is not None:
                sts = [st + bias_fn(h) for h, st in enumerate(sts)]
            return jnp.concatenate(sts, axis=1)

        def pv(pb):
            outs = []
            for h in range(A_HEADS):
                acc = None
                for vtt_fn, r0 in vtt_fns:
                    vtt = vtt_fn(h)
                    part = _dot(vtt, pb[r0:r0 + vtt.shape[1], h * 2 * LANES:(h + 1) * 2 * LANES])
                    acc = part if acc is None else acc + part
                outs.append(acc)
            return jnp.concatenate(outs, axis=1)

        return logits, pv

    def bias_rows(kind, h, rows):
        return jnp.concatenate([bias_ref[kind, h, 0:rows, :], bias_ref[kind, A_HEADS + h, 0:rows, :]], axis=1)

    def finalize():
        lam = lam_ref[...]
        lam_full = (jnp.exp(jnp.sum(lam[0:1] * lam[1:2], axis=-1, keepdims=True))
                    - jnp.exp(jnp.sum(lam[2:3] * lam[3:4], axis=-1, keepdims=True)) + lambda_init)
        o = acc_ref[0:A_VD, :] * (1.0 / acc_ref[A_VD:A_VD + 1, :])
        for h in range(A_HEADS):
            d = o[:, 2 * h * LANES:(2 * h + 1) * LANES] - lam_full * o[:, (2 * h + 1) * LANES:(2 * h + 2) * LANES]
            d = d * lax.rsqrt(jnp.mean(d * d, axis=0, keepdims=True) + EPS) * sub_ref[...] * (1.0 - lambda_init)
            o_ref[:, h * hw:(h + 1) * hw] = d.T.astype(bf16)

    @pl.when(s_id < nb_real)
    def _():
        i = s_id % nblk
        kind_m = jnp.where(i == 0, KIND_META0, KIND_FAR)
        nfar, near0, nnear = _sweep_steps(i)

        def k_fn(b0, nsub):
            off = pl.multiple_of(b0 * LANES, NSUB * LANES)
            return lambda h: k_ref[pl.ds(off, nsub * LANES), h * hw:(h + 1) * hw]

        def vt_fn(b0, nsub):
            return lambda h: jnp.concatenate([vt_ref[b0 + t, h * A_VR:(h + 1) * A_VR, :] for t in range(nsub)], axis=1)

        def near_bias(b0):
            kinds = [_block_kind(b0 + t - i) for t in range(NSUB)]
            return [lambda h, kind=kind: bias_rows(kind, h, LANES) for kind in kinds]

        def first_fns():
            biases = [lambda h: bias_rows(kind_m, h, N_META)] + near_bias(0)
            return step_fns(
                lambda h: jnp.concatenate([km_ref[0, :, h * hw:(h + 1) * hw], k_fn(0, NSUB)(h)], axis=0),
                [(lambda h: vtm_ref[0, h * A_VR:(h + 1) * A_VR, :], 0), (vt_fn(0, NSUB), N_META)],
                lambda h: jnp.concatenate([b(h) for b in biases], axis=0))

        def far_step(w):
            b0 = NSUB + w * NSUB_FAR
            far_bias = jnp.concatenate([bias_rows(KIND_FAR, h, 1) for h in range(A_HEADS)], axis=1)
            return step_fns(k_fn(b0, NSUB_FAR), [(vt_fn(b0, NSUB_FAR), 0)], None) + (far_bias,)

        def near_step(u):
            b0 = near0 + u * NSUB
            biases = near_bias(b0)
            return step_fns(k_fn(b0, NSUB), [(vt_fn(b0, NSUB), 0)],
                            lambda h: jnp.concatenate([b(h) for b in biases], axis=0)) + (None,)

        def sweep(gap_ref):
            init()
            _softmax_step(*first_fns(), m_ref, acc_ref, gap_ref)
            _softmax_loop(0, nfar, far_step, m_ref, acc_ref, gap_ref)
            _softmax_loop(0, nnear, near_step, m_ref, acc_ref, gap_ref)

        gap_ref[...] = jnp.full(gap_ref.shape, NEG_INF, f32)
        sweep(gap_ref)
        pl.when(_lazy_failed(gap_ref, acc_ref[A_VD:A_VD + 1, :]))(lambda: sweep(None))
        finalize()

    @pl.when(s_id == nb_real)
    def _():
        init()
        _softmax_step(*step_fns(lambda h: kmf_ref[:, h * hw:(h + 1) * hw],
                                [(lambda h: vtmf_ref[0, h * A_VR:(h + 1) * A_VR, :], 0)],
                                lambda h: bias_rows(KIND_METAMETA, h, LANES)), m_ref, acc_ref)
        finalize()


def _attn_a(qs, k, vt, bias, lam, subln, *, bsz, nblk, lambda_init):
    n = k.shape[0]
    nb_real = bsz * nblk
    seq = nblk * LANES
    d = A_QD
    vr = A_HEADS * A_VR
    assert nblk % NSUB == 0
    km, vtm = _meta_views(k[nb_real * LANES:], vt[nb_real], bsz)
    kern = functools.partial(_attn_a_kernel, nblk=nblk, lambda_init=lambda_init)
    bclamp = lambda s: jnp.minimum(s // nblk, bsz - 1)
    return pl.pallas_call(
        kern,
        grid=(nb_real + 1,),
        in_specs=[
            pl.BlockSpec((1,) + qs.shape[1:], lambda s: (s, 0, 0, 0)),
            pl.BlockSpec((seq, d), lambda s: (bclamp(s), 0)),
            pl.BlockSpec((nblk, vr, LANES), lambda s: (bclamp(s), 0, 0)),
            pl.BlockSpec((1, N_META, d), lambda s: (bclamp(s), 0, 0)),
            pl.BlockSpec((1, vr, N_META), lambda s: (bclamp(s), 0, 0)),
            pl.BlockSpec((LANES, d), lambda s: (nb_real, 0)),
            pl.BlockSpec((1, vr, LANES), lambda s: (nb_real, 0, 0)),
            _bias_spec(),
            pl.BlockSpec((4, A_HD), lambda s: (0, 0)),
            pl.BlockSpec((A_VD, LANES), lambda s: (0, 0)),
        ],
        out_specs=pl.BlockSpec((LANES, d), lambda s: (s, 0)),
        out_shape=jax.ShapeDtypeStruct((n, d), bf16),
        scratch_shapes=[
            pltpu.VMEM((1, A_HEADS * 2 * LANES), f32),
            pltpu.VMEM((A_VR, A_HEADS * 2 * LANES), f32),
            pltpu.VMEM((1, A_HEADS * 2 * LANES), f32),
        ],
        compiler_params=_cparams(("arbitrary",)),
        name="attn_a",
    )(qs, k, vt, km, vtm, k, vt, bias, lam, jnp.broadcast_to(subln[:, None], (A_VD, LANES)))


C_QD = C_Q_HEADS * C_HD
C_KD = 2 * C_KV_HEADS * C_HD
C_VR = 2 * C_HD + ONES_ROWS


def _proj_c_kernel(h_ref, g_ref, w_ref, gq_ref, gk_ref, qs_ref, k_ref, vt_ref):
    xn = (_rms_rows(h_ref[...]) * g_ref[...]).astype(bf16)
    y = _dot(xn, w_ref[...])
    q = _group_rms(y[:, :C_QD], C_HD) * gq_ref[...]
    lo = _lo_half_mask(q.shape, 2 * C_HD, C_HD)
    q_even = jnp.where(lo, q, 0.0).astype(bf16)
    q_odd = jnp.where(lo, 0.0, q).astype(bf16)
    k_ref[...] = (_group_rms(y[:, C_QD:C_QD + C_KD], C_HD) * gk_ref[...]).astype(bf16)
    ones = jnp.ones((ONES_ROWS, LANES), bf16)
    for t in range(vt_ref.shape[0]):
        rows = slice(t * LANES, (t + 1) * LANES)
        vt = y[rows, C_QD + C_KD:].T.astype(bf16)
        for g in range(C_KV_HEADS):
            for hh in range(C_GROUP):
                pair = (g * C_GROUP + hh) // 2
                src = q_even if hh % 2 == 0 else q_odd
                qs_ref[t, g, hh * LANES:(hh + 1) * LANES, :] = src[rows, pair * LANES:(pair + 1) * LANES]
            vt_ref[t, g * C_VR:g * C_VR + 2 * C_HD, :] = vt[g * 2 * C_HD:(g + 1) * 2 * C_HD]
            vt_ref[t, g * C_VR + 2 * C_HD:(g + 1) * C_VR, :] = ones


def _proj_c(h, g, w, qk_norm, *, tm):
    n, d = h.shape
    nw = w.shape[1]
    nt = tm // LANES
    gq = jnp.tile(qk_norm[0] * (C_HD ** -0.5 * LOG2E), C_QD // C_HD).reshape(1, C_QD)
    gk = jnp.tile(qk_norm[1], C_KD // C_HD).reshape(1, C_KD)
    return pl.pallas_call(
        _proj_c_kernel,
        grid=(n // tm,),
        in_specs=[
            pl.BlockSpec((tm, d), lambda i: (i, 0)),
            pl.BlockSpec((1, d), lambda i: (0, 0)),
            pl.BlockSpec((d, nw), lambda i: (0, 0)),
            pl.BlockSpec((1, C_QD), lambda i: (0, 0)),
            pl.BlockSpec((1, C_KD), lambda i: (0, 0)),
        ],
        out_specs=[
            pl.BlockSpec((nt, C_KV_HEADS, C_GROUP * LANES, LANES), lambda i: (i, 0, 0, 0)),
            pl.BlockSpec((tm, C_KD), lambda i: (i, 0)),
            pl.BlockSpec((nt, C_KV_HEADS * C_VR, LANES), lambda i: (i, 0, 0)),
        ],
        out_shape=[
            jax.ShapeDtypeStruct((n // LANES, C_KV_HEADS, C_GROUP * LANES, LANES), bf16),
            jax.ShapeDtypeStruct((n, C_KD), bf16),
            jax.ShapeDtypeStruct((n // LANES, C_KV_HEADS * C_VR, LANES), bf16),
        ],
        compiler_params=_cparams(("arbitrary",)),
        name="proj_c",
    )(h, g.reshape(1, d), w, gq, gk)


def _attn_c_kernel(sink_ref, qs_ref, k_ref, vt_ref, km_ref, vtm_ref, kmf_ref, vtmf_ref, bias_ref, o_ref, *, nblk):
    s_id = pl.program_id(0)
    nb_real = pl.num_programs(0) - 1
    r_io, _ = _tile_iotas()
    vd = 2 * C_HD

    def attend_with(tiles, lazy):
        top = r_io < C_HD
        failed = None
        for g in range(C_KV_HEADS):
            sink = jnp.concatenate(
                [jnp.full((1, LANES), sink_ref[g * C_GROUP + hh] * LOG2E, f32) for hh in range(C_GROUP)], axis=1)
            sts = []
            m = sink
            for (k_fn, vt_fn, bias_fn) in tiles:
                st = _dot_nt(k_fn(g), qs_ref[0, g])
                st = st + jnp.concatenate([bias_fn(g * C_GROUP + hh) for hh in range(C_GROUP)], axis=1)
                m = jnp.maximum(m, jnp.max(st, axis=0, keepdims=True))
                sts.append(st)
            acc = None
            for st, (k_fn, vt_fn, bias_fn) in zip(sts, tiles):
                pv = _dot(vt_fn(g), jnp.exp2(st if lazy else st - m).astype(bf16))
                acc = pv if acc is None else acc + pv
            denom = acc[vd:vd + 1, :] + jnp.exp2(sink if lazy else sink - m)
            if lazy:
                bad = jnp.logical_not((jnp.max(m) <= LAZY_GAP) & (jnp.min(denom) >= LAZY_FLOOR))
                failed = bad if failed is None else failed | bad
            o = acc[0:vd, :] * (1.0 / denom)
            for cc in range(C_GROUP // 2):
                even = o[:, (2 * cc) * LANES:(2 * cc + 1) * LANES]
                odd = o[:, (2 * cc + 1) * LANES:(2 * cc + 2) * LANES]
                col = (g * (C_GROUP // 2) + cc) * LANES
                o_ref[:, col:col + LANES] = jnp.where(top, even, odd).T.astype(bf16)
        return failed

    def attend(tiles):
        failed = attend_with(tiles, True)

        @pl.when(failed)
        def _():
            attend_with(tiles, False)

    @pl.when(s_id < nb_real)
    def _():
        i = s_id % nblk
        prev = jnp.maximum(i - 1, 0)
        poff = pl.multiple_of(prev * LANES, LANES)
        coff = pl.multiple_of(i * LANES, LANES)
        kind_m = jnp.where(i == 0, KIND_META0, KIND_FAR)
        kind_p = jnp.where(i == 0, KIND_MASKED, KIND_PREVWIN)
        attend([
            (lambda g: km_ref[0, :, g * LANES:(g + 1) * LANES], lambda g: vtm_ref[0, g * C_VR:(g + 1) * C_VR, :],
             lambda h: bias_ref[kind_m, h, 0:N_META, :]),
            (lambda g: k_ref[pl.ds(poff, LANES), g * LANES:(g + 1) * LANES],
             lambda g: vt_ref[prev, g * C_VR:(g + 1) * C_VR, :], lambda h: bias_ref[kind_p, h]),
            (lambda g: k_ref[pl.ds(coff, LANES), g * LANES:(g + 1) * LANES],
             lambda g: vt_ref[i, g * C_VR:(g + 1) * C_VR, :], lambda h: bias_ref[KIND_DIAG, h]),
        ])

    @pl.when(s_id == nb_real)
    def _():
        attend([(lambda g: kmf_ref[:, g * LANES:(g + 1) * LANES], lambda g: vtmf_ref[0, g * C_VR:(g + 1) * C_VR, :],
                 lambda h: bias_ref[KIND_METAMETA, h])])


def _attn_c(qs, k, vt, bias, sinks, *, bsz, nblk):
    n = k.shape[0]
    nb_real = bsz * nblk
    seq = nblk * LANES
    vr = C_KV_HEADS * C_VR
    kern = functools.partial(_attn_c_kernel, nblk=nblk)
    bclamp = lambda s: jnp.minimum(s // nblk, bsz - 1)
    km, vtm = _meta_views(k[nb_real * LANES:], vt[nb_real], bsz)
    return pl.pallas_call(
        kern,
        grid=(nb_real + 1,),
        in_specs=[
            pl.BlockSpec(memory_space=pltpu.SMEM),
            pl.BlockSpec((1,) + qs.shape[1:], lambda s: (s, 0, 0, 0)),
            pl.BlockSpec((seq, C_KD), lambda s: (bclamp(s), 0)),
            pl.BlockSpec((nblk, vr, LANES), lambda s: (bclamp(s), 0, 0)),
            pl.BlockSpec((1, N_META, C_KD), lambda s: (bclamp(s), 0, 0)),
            pl.BlockSpec((1, vr, N_META), lambda s: (bclamp(s), 0, 0)),
            pl.BlockSpec((LANES, C_KD), lambda s: (nb_real, 0)),
            pl.BlockSpec((1, vr, LANES), lambda s: (nb_real, 0, 0)),
            _bias_spec(),
        ],
        out_specs=pl.BlockSpec((LANES, C_QD), lambda s: (s, 0)),
        out_shape=jax.ShapeDtypeStruct((n, C_QD), bf16),
        compiler_params=_cparams(("arbitrary",)),
        name="attn_c",
    )(sinks, qs, k, vt, km, vtm, k, vt, bias)


B_QA = B_HEADS * B_KV_RANK
B_QI = IDX_HEADS * IDX_DIM
B_W1 = 2 * B_Q_RANK + 2 * LANES
B_TR = B_KV_RANK + ONES_ROWS


def _proj_b_kernel(h_ref, g_ref, w1_ref, ln_ref, wuq_ref, qn_ref,
                   qa_ref, qi_ref, ckv_ref, ckvt_ref, kk_ref, wit_ref):
    xn = (_rms_rows(h_ref[...]) * g_ref[...]).astype(bf16)
    y = _dot(xn, w1_ref[...])
    r = B_Q_RANK
    cq = (_rms_rows(y[:, :r]) * ln_ref[0:1, :]).astype(bf16)
    ckv = _rms_rows(y[:, r:2 * r]) * ln_ref[1:2, :]
    ckv_ref[...] = ckv.astype(bf16)
    kk_ref[...] = _rms_rows(y[:, 2 * r:2 * r + LANES]).astype(bf16)
    wi = y[:, 2 * r + LANES:] * (IDX_HEADS ** -0.5)
    ones = jnp.ones((ONES_ROWS, LANES), bf16)
    for t in range(ckvt_ref.shape[0]):
        ckvt_ref[t, 0:r, :] = ckv[t * LANES:(t + 1) * LANES, :].T.astype(bf16)
        ckvt_ref[t, r:, :] = ones
        wit_ref[t] = wi[t * LANES:(t + 1) * LANES, :].T[0:IDX_HEADS, :]
    z = _dot(cq, wuq_ref[...])
    qa = (_group_rms(z[:, :B_QA], B_KV_RANK) * qn_ref[...]).astype(bf16)
    qi = z[:, B_QA:] * (IDX_DIM ** -0.5)
    lo = _lo_half_mask(qi.shape, 2 * IDX_DIM, IDX_DIM)
    qi_lo = jnp.where(lo, qi, 0.0).astype(bf16)
    qi_hi = jnp.where(lo, 0.0, qi).astype(bf16)
    for t in range(qa_ref.shape[0]):
        rows = slice(t * LANES, (t + 1) * LANES)
        for hd in range(B_HEADS):
            qa_ref[t, hd * LANES:(hd + 1) * LANES, :] = qa[rows, hd * r:(hd + 1) * r]
        for hh in range(IDX_HEADS):
            src = qi_lo if hh % 2 == 0 else qi_hi
            qi_ref[t, hh * LANES:(hh + 1) * LANES, :] = src[rows, (hh // 2) * LANES:(hh // 2 + 1) * LANES]


def _proj_b(h, g, w1, latent_norm, wuq, q_norm, *, tm):
    n, d = h.shape
    qn = jnp.tile(q_norm * (B_KV_RANK ** -0.5 * LOG2E), B_HEADS).reshape(1, B_QA)
    row = lambda i: (i, 0)
    row3 = lambda i: (i, 0, 0)
    const = lambda i: (0, 0)
    nt = tm // LANES
    return pl.pallas_call(
        _proj_b_kernel,
        grid=(n // tm,),
        in_specs=[
            pl.BlockSpec((tm, d), row),
            pl.BlockSpec((1, d), const),
            pl.BlockSpec(w1.shape, const),
            pl.BlockSpec(latent_norm.shape, const),
            pl.BlockSpec(wuq.shape, const),
            pl.BlockSpec((1, B_QA), const),
        ],
        out_specs=[
            pl.BlockSpec((nt, B_HEADS * LANES, B_KV_RANK), row3),
            pl.BlockSpec((nt, IDX_HEADS * LANES, LANES), row3),
            pl.BlockSpec((tm, B_KV_RANK), row),
            pl.BlockSpec((nt, B_TR, LANES), row3),
            pl.BlockSpec((tm, LANES), row),
            pl.BlockSpec((nt, IDX_HEADS, LANES), row3),
        ],
        out_shape=[
            jax.ShapeDtypeStruct((n // LANES, B_HEADS * LANES, B_KV_RANK), bf16),
            jax.ShapeDtypeStruct((n // LANES, IDX_HEADS * LANES, LANES), bf16),
            jax.ShapeDtypeStruct((n, B_KV_RANK), bf16),
            jax.ShapeDtypeStruct((n // LANES, B_TR, LANES), bf16),
            jax.ShapeDtypeStruct((n, LANES), bf16),
            jax.ShapeDtypeStruct((n // LANES, IDX_HEADS, LANES), f32),
        ],
        compiler_params=_cparams(("arbitrary",)),
        name="proj_b",
    )(h, g.reshape(1, d), w1, latent_norm, wuq, qn)


def _attn_b_kernel(qs_ref, is_ref, wit_ref, ckv_ref, ckvt_ref, kk_ref, ckvm_ref, ckvtm_ref, kkm_ref,
                   ckvmf_ref, ckvtmf_ref, bias_ref, wuvt_ref,
                   o_ref, key_ref, pen_ref, m_ref, acc_ref, gap_ref, *, nblk, k_sel):
    s_id = pl.program_id(0)
    nb_real = pl.num_programs(0) - 1
    r_io, c_io = _tile_iotas()
    rk = B_KV_RANK

    def init():
        m_ref[...] = jnp.full(m_ref.shape, NEG_INF, f32)
        acc_ref[...] = jnp.zeros(acc_ref.shape, f32)

    def add_per_head(st, bias_fn, pen):
        cols = []
        for h in range(B_HEADS):
            add = pen if bias_fn is None else (bias_fn(h) if pen is None else bias_fn(h) + pen)
            cols.append(st[:, h * LANES:(h + 1) * LANES] + add)
        return jnp.concatenate(cols, axis=1)

    def step_fns(ckv_fn, ckvt_fns, bias_fn, pen_fn):
        def logits():
            st = _dot_nt(ckv_fn(), qs_ref[0])
            return add_per_head(st, bias_fn, None if pen_fn is None else pen_fn())

        def pv(pb):
            acc = None
            for ckvt_fn, r0 in ckvt_fns:
                ckvt = ckvt_fn()
                part = _dot(ckvt, pb[r0:r0 + ckvt.shape[1], :])
                acc = part if acc is None else acc + part
            return acc

        return logits, pv

    def finalize():
        olat = (acc_ref[0:rk, :] * (1.0 / acc_ref[rk:rk + 1, :])).astype(bf16)
        ot = jnp.concatenate([_dot(wuvt_ref[h], olat[:, h * LANES:(h + 1) * LANES]) for h in range(B_HEADS)], axis=0)
        o_ref[...] = ot.T.astype(bf16)

    def index_scores(kk):
        s = jnp.maximum(_dot_nt(kk, is_ref[0]), 0.0)
        wt = wit_ref[0]
        sc = jnp.zeros((kk.shape[0], LANES), f32)
        for hh in range(IDX_HEADS):
            sc = sc + wt[hh:hh + 1, :] * s[:, hh * LANES:(hh + 1) * LANES]
        return sc

    def sort_key(sc):
        bits = lax.bitcast_convert_type(sc + 0.0, jnp.int32)
        return jnp.where(bits < 0, bits ^ jnp.int32(0x7FFFFFFF), bits)

    @pl.when(s_id < nb_real)
    def _():
        i = s_id % nblk
        ntile = i + 2

        int_min_tile = jnp.full((LANES, LANES), INT_MIN, jnp.int32)
        key_ref[0] = int_min_tile
        key_ref[0, 0:N_META, :] = sort_key(index_scores(kkm_ref[0]))
        key_ref[i + 2] = int_min_tile

        def score_body(jp, carry):
            off = pl.multiple_of(jp * (2 * LANES), 2 * LANES)
            keys = sort_key(index_scores(kk_ref[pl.ds(off, 2 * LANES), :]))
            for t in range(2):
                j = _vec(2 * jp + t)
                vis = (j < i) | ((j == i) & ((r_io >> 6) <= (c_io >> 6)))
                key_ref[2 * jp + t + 1] = jnp.where(vis, keys[t * LANES:(t + 1) * LANES], jnp.int32(INT_MIN))
            return carry

        lax.fori_loop(0, i // 2 + 1, score_body, 0)

        def count(pred):
            def cbody(tp, accv):
                for t in (2 * tp, 2 * tp + 1):
                    accv = accv + jnp.where(pred(key_ref[t], t), 1.0, 0.0)
                return accv
            accv = lax.fori_loop(0, (ntile + 1) // 2, cbody, jnp.zeros((LANES, LANES), f32))
            return jnp.sum(accv, axis=0, keepdims=True)

        kf = float(k_sel)
        zero = jnp.zeros((1, LANES), jnp.int32)
        t0 = jnp.where(count(lambda k, t: k >= zero) >= kf, zero, jnp.int32(INT_MIN))

        def bit_body(it, tcur):
            cand = tcur | jnp.left_shift(jnp.int32(1), 30 - it)
            return jnp.where(count(lambda k, t: k >= cand) >= kf, cand, tcur)

        thr = lax.fori_loop(0, 31, bit_body, t0)

        need = kf - count(lambda k, t: k > thr)
        n_eq = count(lambda k, t: k == thr)
        has_thr = thr > jnp.int32(INT_MIN)
        tied = jnp.max(jnp.where(has_thr & (n_eq > need), 1.0, 0.0)) > 0.0

        def tie_search(_):
            def jbody(it, jcur):
                cand = jcur | jnp.left_shift(jnp.int32(1), 11 - it)
                cnt = count(lambda k, t: (k == thr) & ((t * LANES + r_io) < cand))
                return jnp.where(cnt < need, cand, jcur)
            return lax.fori_loop(0, 12, jbody, jnp.zeros((1, LANES), jnp.int32))

        j_last = lax.cond(tied, tie_search, lambda _: jnp.full((1, LANES), 4095, jnp.int32), 0)
        j_last = jnp.where(has_thr, j_last, -1)

        def pen_body(t, carry):
            k = key_ref[t]
            sel = (k > thr) | ((k == thr) & ((t * LANES + r_io) <= j_last))
            pen_ref[t] = jnp.where(sel, 0.0, NEG_INF)
            return carry

        lax.fori_loop(0, ntile, pen_body, 0)
        for t in range(1, NSUB):
            pen_ref[i + 1 + t] = jnp.full((LANES, LANES), NEG_INF, f32)

        kind_m = jnp.where(i == 0, KIND_META0, KIND_FAR)
        nfar, near0, nnear = _sweep_steps(i)

        def ckv_fn(b0, nsub):
            off = pl.multiple_of(b0 * LANES, NSUB * LANES)
            return lambda: ckv_ref[pl.ds(off, nsub * LANES), :]

        def ckvt_fn(b0, nsub):
            return lambda: jnp.concatenate([ckvt_ref[b0 + t] for t in range(nsub)], axis=1)

        def pen_fn(b0, nsub):
            return lambda: jnp.concatenate([pen_ref[b0 + t + 1] for t in range(nsub)], axis=0)

        def near_bias(b0):
            kinds = [_block_kind(b0 + t - i) for t in range(NSUB)]
            return lambda h: jnp.concatenate([bias_ref[kinds[t], h] for t in range(NSUB)], axis=0)

        def first_pen():
            return jnp.concatenate([pen_ref[0, 0:N_META, :], pen_fn(0, NSUB)()], axis=0)

        def first_fns():
            bias01 = near_bias(0)
            return step_fns(lambda: jnp.concatenate([ckvm_ref[0], ckv_fn(0, NSUB)()], axis=0),
                            [(lambda: ckvtm_ref[0], 0), (ckvt_fn(0, NSUB), N_META)],
                            lambda h: jnp.concatenate([bias_ref[kind_m, h, 0:N_META, :], bias01(h)], axis=0),
                            first_pen)

        def far_step(w):
            b0 = NSUB + w * NSUB_FAR
            far_bias = jnp.concatenate([bias_ref[KIND_FAR, h, 0:1, :] for h in range(B_HEADS)], axis=1)
            return step_fns(ckv_fn(b0, NSUB_FAR), [(ckvt_fn(b0, NSUB_FAR), 0)], None, pen_fn(b0, NSUB_FAR)) + (far_bias,)

        def near_step(u):
            b0 = near0 + u * NSUB
            return step_fns(ckv_fn(b0, NSUB), [(ckvt_fn(b0, NSUB), 0)], near_bias(b0), pen_fn(b0, NSUB)) + (None,)

        def sweep(gap_ref):
            init()
            _softmax_step(*first_fns(), m_ref, acc_ref, gap_ref)
            _softmax_loop(0, nfar, far_step, m_ref, acc_ref, gap_ref)
            _softmax_loop(0, nnear, near_step, m_ref, acc_ref, gap_ref)

        gap_ref[...] = jnp.full(gap_ref.shape, NEG_INF, f32)
        sweep(gap_ref)
        pl.when(_lazy_failed(gap_ref, acc_ref[rk:rk + 1, :]))(lambda: sweep(None))
        finalize()

    @pl.when(s_id == nb_real)
    def _():
        init()
        _softmax_step(*step_fns(lambda: ckvmf_ref[...], [(lambda: ckvtmf_ref[0], 0)],
                                lambda h: bias_ref[KIND_METAMETA, h], None), m_ref, acc_ref)
        finalize()


def _attn_b(qa, qi, wit, ckv, ckvt, kk, bias, wuvt, *, bsz, nblk, k_sel):
    n = ckv.shape[0]
    nb_real = bsz * nblk
    seq = nblk * LANES
    assert k_sel >= N_META and (nblk + 1) * LANES <= 4096 and nblk % NSUB == 0
    ckvm, ckvtm = _meta_views(ckv[nb_real * LANES:], ckvt[nb_real], bsz)
    kkm = kk[nb_real * LANES:].reshape(bsz, N_META, LANES)
    kern = functools.partial(_attn_b_kernel, nblk=nblk, k_sel=k_sel)
    bidx = lambda s: jnp.minimum(s // nblk, bsz - 1)
    blk = lambda s: (s, 0)
    return pl.pallas_call(
        kern,
        grid=(nb_real + 1,),
        in_specs=[
            pl.BlockSpec((1,) + qa.shape[1:], lambda s: (s, 0, 0)),
            pl.BlockSpec((1,) + qi.shape[1:], lambda s: (s, 0, 0)),
            pl.BlockSpec((1, IDX_HEADS, LANES), lambda s: (s, 0, 0)),
            pl.BlockSpec((seq, B_KV_RANK), lambda s: (bidx(s), 0)),
            pl.BlockSpec((nblk, B_TR, LANES), lambda s: (bidx(s), 0, 0)),
            pl.BlockSpec((seq, LANES), lambda s: (bidx(s), 0)),
            pl.BlockSpec((1, N_META, B_KV_RANK), lambda s: (bidx(s), 0, 0)),
            pl.BlockSpec((1, B_TR, N_META), lambda s: (bidx(s), 0, 0)),
            pl.BlockSpec((1, N_META, LANES), lambda s: (bidx(s), 0, 0)),
            pl.BlockSpec((LANES, B_KV_RANK), lambda s: (nb_real, 0)),
            pl.BlockSpec((1, B_TR, LANES), lambda s: (nb_real, 0, 0)),
            _bias_spec(),
            pl.BlockSpec(wuvt.shape, lambda s: (0, 0, 0)),
        ],
        out_specs=pl.BlockSpec((LANES, B_HEADS * B_VD), blk),
        out_shape=jax.ShapeDtypeStruct((n, B_HEADS * B_VD), bf16),
        scratch_shapes=[
            pltpu.VMEM((nblk + 2, LANES, LANES), jnp.int32),
            pltpu.VMEM((nblk + NSUB, LANES, LANES), f32),
            pltpu.VMEM((1, B_HEADS * LANES), f32),
            pltpu.VMEM((B_TR, B_HEADS * LANES), f32),
            pltpu.VMEM((1, B_HEADS * LANES), f32),
        ],
        compiler_params=_cparams(("arbitrary",)),
        name="attn_b",
    )(qa, qi, wit, ckv, ckvt, kk, ckvm, ckvtm, kkm, ckv, ckvt, bias, wuvt)


def kernel(x, meta_tokens, rel_bias, ln_ffn1, ffn1_wi, ffn1_wo, ln_mix, w_out, ln_ffn2, ffn2_wi, ffn2_wo, a_w_in, a_qk_norm, a_lambda, a_subln, b_w_in, b_latent_norm, b_w_uq, b_q_norm, b_w_uv, c_w_in, c_qk_norm, c_sinks):
    bsz, seq, d = x.shape
    assert d == D_MODEL and seq % LANES == 0 and bsz * N_META == LANES
    nblk = seq // LANES
    n = bsz * seq + LANES
    k_sel = min(TOPK_MAX, seq // 4)
    tm_ffn = _row_tile(n, 1408)
    tm_last = _row_tile(bsz * seq, 1408)
    tm_proj = _row_tile(n, 384, LANES)
    fc = 256

    h = x.reshape(bsz * seq, d)
    meta_rows = jnp.broadcast_to(meta_tokens.astype(x.dtype), (bsz, N_META, d)).reshape(LANES, d)
    bias = _bias_tiles(rel_bias)

    for layer in range(DEPTH):
        h = _ffn(h, ln_ffn1[layer], ffn1_wi, ffn1_wo, layer, tm=tm_ffn, fc=fc,
                 tail=meta_rows if layer == 0 else None)
        kind, j = layer % N_MIXERS, layer // N_MIXERS
        g = ln_mix[layer]
        if kind == 0:
            lambda_init = 0.8 - 0.6 * math.exp(-0.3 * layer)
            qs, k, vt = _proj_a(h, g, a_w_in[j].astype(bf16), a_qk_norm[j], tm=tm_proj)
            mix = _attn_a(qs, k, vt, bias, a_lambda[j], a_subln[j], bsz=bsz, nblk=nblk, lambda_init=lambda_init)
        elif kind == 1:
            w = b_w_in[j]
            r2 = B_Q_RANK + B_KV_RANK
            kcol = w[:, r2:r2 + IDX_DIM]
            w1 = jnp.concatenate([w[:, :r2], kcol, kcol, w[:, r2 + IDX_DIM:],
                                  jnp.zeros((d, LANES - IDX_HEADS), w.dtype)], axis=1).astype(bf16)
            assert w1.shape[1] == B_W1
            qa, qi, ckv, ckvt, kk, wit = _proj_b(h, g, w1, b_latent_norm[j], b_w_uq[j].astype(bf16), b_q_norm[j],
                                                 tm=tm_proj)
            wuvt = jnp.swapaxes(b_w_uv[j], 1, 2).astype(bf16)
            mix = _attn_b(qa, qi, wit, ckv, ckvt, kk, bias, wuvt, bsz=bsz, nblk=nblk, k_sel=k_sel)
        else:
            w = c_w_in[j]
            kcols = [w[:, C_QD + gi * C_HD:C_QD + (gi + 1) * C_HD] for gi in range(C_KV_HEADS)]
            voff = C_QD + C_KV_HEADS * C_HD
            vcols = [w[:, voff + gi * C_HD:voff + (gi + 1) * C_HD] for gi in range(C_KV_HEADS)]
            wc = jnp.concatenate([w[:, :C_QD]] + [kc for kc in kcols for _ in range(2)]
                                 + [vc for vc in vcols for _ in range(2)], axis=1).astype(bf16)
            qs, k, vt = _proj_c(h, g, wc, c_qk_norm[j], tm=tm_proj)
            mix = _attn_c(qs, k, vt, bias, c_sinks[j], bsz=bsz, nblk=nblk)
        last = layer == DEPTH - 1
        h = _ffn(h, ln_ffn2[layer], ffn2_wi, ffn2_wo, layer, tm=tm_last if last else tm_ffn, fc=fc,
                 mix=mix, wout=w_out[layer].astype(bf16), n_rows=bsz * seq if last else None)
    return h.reshape(bsz, seq, d)
```

```python
import functools
import math

import numpy as np
import jax
import jax.numpy as jnp
from jax import lax
from jax.experimental import pallas as pl
from jax.experimental.pallas import tpu as pltpu

D_MODEL = 1024
DEPTH = 4
CHUNK = 64
N_META = 16
N_MIXERS = 3
NEG_INF = -1e30
REL_BUCKETS = 32
REL_MAX_DIST = 128
REL_HEADS = 16
D_FF = 2816
A_HEADS = 8
A_HD = 64
A_VD = 2 * A_HD
B_HEADS = 16
B_Q_RANK = 256
B_KV_RANK = 256
B_VD = 64
IDX_HEADS = 8
IDX_DIM = 64
TOPK_MAX = 256
C_Q_HEADS = 16
C_KV_HEADS = 2
C_GROUP = C_Q_HEADS // C_KV_HEADS
C_HD = 64
EPS = 1e-6

LANES = 128
BF16_ROWS = 16
VMEM_LIMIT = 56 * 1024 * 1024
INT_MIN = -(2 ** 31)
NSUB = 2
NSUB_FAR = 4
LOG2E = math.log2(math.e)
LAZY_CEIL = 2.0 ** 70
LAZY_FLOOR = 2.0 ** -100
ONES_ROWS = BF16_ROWS

KIND_DIAG, KIND_PREV, KIND_FAR, KIND_META0, KIND_METAMETA, KIND_MASKED, KIND_PREVWIN = 0, 1, 2, 3, 4, 5, 6
N_KINDS = 7

f32 = jnp.float32
bf16 = jnp.bfloat16


def _cparams(sem):
    return pltpu.CompilerParams(dimension_semantics=sem, vmem_limit_bytes=VMEM_LIMIT)


def _row_tile(n, cap, mult=BF16_ROWS):
    best = None
    for t in range(mult, cap + 1, mult):
        if n % t == 0:
            best = t
    assert best is not None
    return best


def _dot(a, b):
    return jnp.dot(a, b, preferred_element_type=f32)


def _dot_nt(a, b):
    return lax.dot_general(a, b, (((1,), (1,)), ((), ())), preferred_element_type=f32)


def _rms_rows(x):
    return x * lax.rsqrt(jnp.mean(x * x, axis=-1, keepdims=True) + EPS)


def _lo_half_mask(shape, period, half):
    return (lax.broadcasted_iota(jnp.int32, shape, 1) & (period - 1)) < half


def _group_rms(x, group):
    r, c = x.shape
    outs = []
    if group == 64:
        lo = _lo_half_mask((r, LANES), LANES, 64)
        for ci in range(c // LANES):
            xc = x[:, ci * LANES:(ci + 1) * LANES]
            x2 = xc * xc
            s_lo = jnp.sum(jnp.where(lo, x2, 0.0), axis=-1, keepdims=True)
            s_hi = jnp.sum(jnp.where(lo, 0.0, x2), axis=-1, keepdims=True)
            inv = jnp.where(lo, lax.rsqrt(s_lo * (1.0 / 64) + EPS), lax.rsqrt(s_hi * (1.0 / 64) + EPS))
            outs.append(xc * inv)
    else:
        for gi in range(c // group):
            outs.append(_rms_rows(x[:, gi * group:(gi + 1) * group]))
    return outs[0] if len(outs) == 1 else jnp.concatenate(outs, axis=-1)


def _tile_iotas():
    r = lax.broadcasted_iota(jnp.int32, (LANES, LANES), 0)
    c = lax.broadcasted_iota(jnp.int32, (LANES, LANES), 1)
    return r, c


def _vec(s):
    return jnp.full((LANES, LANES), s, jnp.int32)


def _softmax_step(logits_fn, pv_fn, m_ref, acc_ref, lazy=False, offset=None):
    st = logits_fn()
    if offset is not None:
        st = st + offset
    if lazy:
        acc_ref[...] += pv_fn(jnp.exp2(st).astype(bf16))
    else:
        m_old = m_ref[...]
        m_new = jnp.maximum(m_old, jnp.max(st, axis=0, keepdims=True))
        acc_ref[...] = jnp.exp2(m_old - m_new) * acc_ref[...] + pv_fn(jnp.exp2(st - m_new).astype(bf16))
        m_ref[...] = m_new


def _softmax_loop(lo, hi, step_fn, m_ref, acc_ref, lazy):
    def body(w, carry):
        logits_fn, pv_fn, offset = step_fn(w)
        _softmax_step(logits_fn, pv_fn, m_ref, acc_ref, lazy, offset)
        return carry

    lax.fori_loop(lo, hi, body, 0)


def _lazy_failed(denominators):
    return jnp.logical_not((jnp.max(denominators) <= LAZY_CEIL) & (jnp.min(denominators) >= LAZY_FLOOR))


def _block_kind(rel):
    return jnp.where(rel < -1, KIND_FAR,
                     jnp.where(rel == -1, KIND_PREV, jnp.where(rel == 0, KIND_DIAG, KIND_MASKED)))


def _sweep_steps(i):
    nfar = jnp.maximum(i - 1 - NSUB, 0) // NSUB_FAR
    near0 = NSUB + nfar * NSUB_FAR
    return nfar, near0, (i - near0 + NSUB) // NSUB


def _ffn_kernel(*refs, fuse_out, tail_rows, layer, fc, nj):
    if fuse_out:
        h_ref, mix_ref, wout_ref, g_ref, wi_hbm, wo_hbm, o_ref, xn_ref, wa_buf, wb_buf, wo_buf, sem = refs
    elif tail_rows:
        h_ref, tail_ref, g_ref, wi_hbm, wo_hbm, o_ref, xn_ref, wa_buf, wb_buf, wo_buf, sem = refs
    else:
        h_ref, g_ref, wi_hbm, wo_hbm, o_ref, xn_ref, wa_buf, wb_buf, wo_buf, sem = refs

    def chunk_copies(j, slot):
        lo = pl.multiple_of(j * fc, fc)
        hi = pl.multiple_of((nj + j) * fc, fc)
        return (pltpu.make_async_copy(wi_hbm.at[layer, :, pl.ds(lo, fc)], wa_buf.at[slot], sem.at[0, slot]),
                pltpu.make_async_copy(wi_hbm.at[layer, :, pl.ds(hi, fc)], wb_buf.at[slot], sem.at[1, slot]),
                pltpu.make_async_copy(wo_hbm.at[layer, pl.ds(lo, fc), :], wo_buf.at[slot], sem.at[2, slot]))

    i = pl.program_id(0)
    first = i * nj

    @pl.when(i == 0)
    def _():
        for c in chunk_copies(0, 0):
            c.start()

    if tail_rows:
        last = pl.num_programs(0) - 1
        nvalid = o_ref.shape[0] - tail_rows

        @pl.when(i < last)
        def _():
            o_ref[...] = h_ref[...]

        @pl.when(i == last)
        def _():
            o_ref[0:nvalid, :] = h_ref[0:nvalid, :]
            o_ref[nvalid:, :] = tail_ref[...]

        r = o_ref[...]
    else:
        r = h_ref[...]
        if fuse_out:
            r = r + _dot(mix_ref[...], wout_ref[...])
        o_ref[...] = r
    xn_ref[...] = (_rms_rows(r) * g_ref[...]).astype(bf16)

    def body(j, carry):
        slot = (first + j) & 1

        @pl.when((j + 1 < nj) | (i + 1 < pl.num_programs(0)))
        def _():
            for c in chunk_copies(jnp.where(j + 1 < nj, j + 1, 0), 1 - slot):
                c.start()

        for c in chunk_copies(j, slot):
            c.wait()
        xn = xn_ref[...]
        a = _dot(xn, wa_buf[slot].astype(bf16))
        b = _dot(xn, wb_buf[slot].astype(bf16))
        act = (a / (1.0 + jnp.exp(-a)) * b).astype(bf16)
        o_ref[...] += 0.5 * _dot(act, wo_buf[slot].astype(bf16))
        return carry

    lax.fori_loop(0, nj, body, 0)


def _ffn(h, g, wi, wo, layer, *, tm, fc, mix=None, wout=None, n_rows=None, tail=None):
    d = h.shape[1]
    tail_rows = 0 if tail is None else tail.shape[0]
    n = h.shape[0] + tail_rows if n_rows is None else n_rows
    assert n % tm == 0 and 0 <= tail_rows < tm and not (tail_rows and (mix is not None or n_rows is not None))
    dff = wo.shape[1]
    nj = dff // fc
    fuse = mix is not None
    row = lambda i: (i, 0)
    in_specs = [pl.BlockSpec((tm, d), row)]
    args = [h]
    if tail_rows:
        in_specs += [pl.BlockSpec(tail.shape, lambda i: (0, 0))]
        args += [tail]
    if fuse:
        in_specs += [pl.BlockSpec((tm, mix.shape[1]), row), pl.BlockSpec(wout.shape, lambda i: (0, 0))]
        args += [mix, wout]
    in_specs += [
        pl.BlockSpec((1, d), lambda i: (0, 0)),
        pl.BlockSpec(memory_space=pl.ANY),
        pl.BlockSpec(memory_space=pl.ANY),
    ]
    args += [g.reshape(1, d), wi, wo]
    return pl.pallas_call(
        functools.partial(_ffn_kernel, fuse_out=fuse, tail_rows=tail_rows, layer=layer, fc=fc, nj=nj),
        grid=(n // tm,),
        in_specs=in_specs,
        out_specs=pl.BlockSpec((tm, d), row),
        out_shape=jax.ShapeDtypeStruct((n, d), f32),
        scratch_shapes=[
            pltpu.VMEM((tm, d), bf16),
            pltpu.VMEM((2, d, fc), f32),
            pltpu.VMEM((2, d, fc), f32),
            pltpu.VMEM((2, fc, d), f32),
            pltpu.SemaphoreType.DMA((3, 2)),
        ],
        compiler_params=_cparams(("arbitrary",)),
        name="ffn_out" if fuse else "ffn",
    )(*args)


def _rel_bucket(rel):
    half = REL_BUCKETS // 2
    max_exact = half // 2
    n = jnp.abs(rel)
    large = max_exact + (jnp.log(jnp.maximum(n, 1).astype(jnp.float32) / max_exact)
                         / math.log(REL_MAX_DIST / max_exact) * (half - max_exact)).astype(jnp.int32)
    large = jnp.minimum(large, half - 1)
    return jnp.where(rel > 0, half, 0) + jnp.where(n < max_exact, n, large)


def _rel_tiles():
    k = np.arange(LANES)[:, None]
    q = np.arange(LANES)[None, :]
    far = np.full((LANES, LANES), -4 * LANES)
    ones = np.ones((LANES, LANES), bool)
    rels = [k - q, k - q - LANES, far, (k % N_META) - N_META - q, (k % N_META) - (q % N_META), far, k - q - LANES]
    vis = [(k // CHUNK) <= (q // CHUNK), ones, ones, ones, (k // N_META) == (q // N_META), ~ones,
           (q < CHUNK) | (k >= CHUNK)]
    return (np.stack([np.broadcast_to(a, (LANES, LANES)) for a in rels]).astype(np.int32),
            np.stack([np.broadcast_to(a, (LANES, LANES)) for a in vis]).astype(np.int32))


def _bias_kernel(rb_ref, bucket_ref, vis_ref, o_ref):
    h = pl.program_id(0)
    for kind in range(N_KINDS):
        bk = bucket_ref[kind]
        acc = jnp.zeros((LANES, LANES), f32)
        for b in range(REL_BUCKETS):
            acc = jnp.where(bk == b, rb_ref[b, h], acc)
        o_ref[kind, 0] = jnp.where(vis_ref[kind] != 0, acc * LOG2E, NEG_INF)


def _bias_tiles(rel_bias):
    rel, vis = _rel_tiles()
    bucket = _rel_bucket(jnp.asarray(rel))
    nk = N_KINDS
    return pl.pallas_call(
        _bias_kernel,
        grid=(REL_HEADS,),
        in_specs=[
            pl.BlockSpec(memory_space=pltpu.SMEM),
            pl.BlockSpec((nk, LANES, LANES), lambda h: (0, 0, 0)),
            pl.BlockSpec((nk, LANES, LANES), lambda h: (0, 0, 0)),
        ],
        out_specs=pl.BlockSpec((nk, 1, LANES, LANES), lambda h: (0, h, 0, 0)),
        out_shape=jax.ShapeDtypeStruct((nk, REL_HEADS, LANES, LANES), f32),
        compiler_params=_cparams(("arbitrary",)),
        name="bias_tiles",
    )(rel_bias, bucket, jnp.asarray(vis))


def _bias_spec():
    return pl.BlockSpec((N_KINDS, REL_HEADS, LANES, LANES), lambda s: (0, 0, 0, 0))


def _meta_views(rows, cols_t, bsz):
    f = rows.shape[1]
    return (rows.reshape(bsz, N_META, f),
            cols_t.reshape(cols_t.shape[0], bsz, N_META).transpose(1, 0, 2))


A_QD = A_HEADS * 2 * A_HD
A_VR = A_VD + ONES_ROWS


def _proj_a_kernel(h_ref, g_ref, w_ref, gq_ref, gk_ref, qs_ref, k_ref, vt_ref):
    xn = (_rms_rows(h_ref[...]) * g_ref[...]).astype(bf16)
    y = _dot(xn, w_ref[...])
    q = _group_rms(y[:, :A_QD], A_HD) * gq_ref[...]
    lo = _lo_half_mask(q.shape, 2 * A_HD, A_HD)
    q_lo = jnp.where(lo, q, 0.0).astype(bf16)
    q_hi = jnp.where(lo, 0.0, q).astype(bf16)
    k_ref[...] = (_group_rms(y[:, A_QD:2 * A_QD], A_HD) * gk_ref[...]).astype(bf16)
    ones = jnp.ones((ONES_ROWS, LANES), bf16)
    for t in range(vt_ref.shape[0]):
        rows = slice(t * LANES, (t + 1) * LANES)
        vt = y[rows, 2 * A_QD:].T.astype(bf16)
        for h in range(A_HEADS):
            qs_ref[t, h, :LANES, :] = q_lo[rows, h * A_VD:(h + 1) * A_VD]
            qs_ref[t, h, LANES:, :] = q_hi[rows, h * A_VD:(h + 1) * A_VD]
            vt_ref[t, h * A_VR:h * A_VR + A_VD, :] = vt[h * A_VD:(h + 1) * A_VD]
            vt_ref[t, h * A_VR + A_VD:(h + 1) * A_VR, :] = ones


def _proj_a(h, g, w, qk_norm, *, tm):
    n, d = h.shape
    nw = w.shape[1]
    gq = jnp.tile(qk_norm[0] * (A_HD ** -0.5 * LOG2E), A_QD // A_HD).reshape(1, A_QD)
    gk = jnp.tile(qk_norm[1], A_QD // A_HD).reshape(1, A_QD)
    return pl.pallas_call(
        _proj_a_kernel,
        grid=(n // tm,),
        in_specs=[
            pl.BlockSpec((tm, d), lambda i: (i, 0)),
            pl.BlockSpec((1, d), lambda i: (0, 0)),
            pl.BlockSpec((d, nw), lambda i: (0, 0)),
            pl.BlockSpec((1, A_QD), lambda i: (0, 0)),
            pl.BlockSpec((1, A_QD), lambda i: (0, 0)),
        ],
        out_specs=[
            pl.BlockSpec((tm // LANES, A_HEADS, 2 * LANES, A_VD), lambda i: (i, 0, 0, 0)),
            pl.BlockSpec((tm, A_QD), lambda i: (i, 0)),
            pl.BlockSpec((tm // LANES, A_HEADS * A_VR, LANES), lambda i: (i, 0, 0)),
        ],
        out_shape=[
            jax.ShapeDtypeStruct((n // LANES, A_HEADS, 2 * LANES, A_VD), bf16),
            jax.ShapeDtypeStruct((n, A_QD), bf16),
            jax.ShapeDtypeStruct((n // LANES, A_HEADS * A_VR, LANES), bf16),
        ],
        compiler_params=_cparams(("arbitrary",)),
        name="proj_a",
    )(h, g.reshape(1, d), w, gq, gk)


def _attn_a_kernel(qs_ref, k_ref, vt_ref, km_ref, vtm_ref, kmf_ref, vtmf_ref, bias_ref, lam_ref, sub_ref,
                   o_ref, m_ref, acc_ref, *, nblk, lambda_init):
    s_id = pl.program_id(0)
    nb_real = pl.num_programs(0) - 1
    hw = 2 * A_HD

    def init():
        m_ref[...] = jnp.full(m_ref.shape, NEG_INF, f32)
        acc_ref[...] = jnp.zeros(acc_ref.shape, f32)

    def step_fns(kt_fn, vtt_fns, bias_fn):
        def logits():
            sts = [_dot_nt(kt_fn(h), qs_ref[0, h]) for h in range(A_HEADS)]
            if bias_fn is not None:
                sts = [st + bias_fn(h) for h, st in enumerate(sts)]
            return jnp.concatenate(sts, axis=1)

        def pv(pb):
            outs = []
            for h in range(A_HEADS):
                acc = None
                for vtt_fn, r0 in vtt_fns:
                    vtt = vtt_fn(h)
                    part = _dot(vtt, pb[r0:r0 + vtt.shape[1], h * 2 * LANES:(h + 1) * 2 * LANES])
                    acc = part if acc is None else acc + part
                outs.append(acc)
            return jnp.concatenate(outs, axis=1)

        return logits, pv

    def bias_rows(kind, h, rows):
        return jnp.concatenate([bias_ref[kind, h, 0:rows, :], bias_ref[kind, A_HEADS + h, 0:rows, :]], axis=1)

    def finalize():
        lam = lam_ref[...]
        lam_full = (jnp.exp(jnp.sum(lam[0:1] * lam[1:2], axis=-1, keepdims=True))
                    - jnp.exp(jnp.sum(lam[2:3] * lam[3:4], axis=-1, keepdims=True)) + lambda_init)
        o = acc_ref[0:A_VD, :] * (1.0 / acc_ref[A_VD:A_VD + 1, :])
        for h in range(A_HEADS):
            d = o[:, 2 * h * LANES:(2 * h + 1) * LANES] - lam_full * o[:, (2 * h + 1) * LANES:(2 * h + 2) * LANES]
            d = d * lax.rsqrt(jnp.mean(d * d, axis=0, keepdims=True) + EPS) * sub_ref[...] * (1.0 - lambda_init)
            o_ref[:, h * hw:(h + 1) * hw] = d.T.astype(bf16)

    @pl.when(s_id < nb_real)
    def _():
        i = s_id % nblk
        kind_m = jnp.where(i == 0, KIND_META0, KIND_FAR)
        nfar, near0, nnear = _sweep_steps(i)

        def k_fn(b0, nsub):
            off = pl.multiple_of(b0 * LANES, NSUB * LANES)
            return lambda h: k_ref[pl.ds(off, nsub * LANES), h * hw:(h + 1) * hw]

        def vt_fn(b0, nsub):
            return lambda h: jnp.concatenate([vt_ref[b0 + t, h * A_VR:(h + 1) * A_VR, :] for t in range(nsub)], axis=1)

        def near_bias(b0):
            kinds = [_block_kind(b0 + t - i) for t in range(NSUB)]
            return [lambda h, kind=kind: bias_rows(kind, h, LANES) for kind in kinds]

        def first_fns():
            biases = [lambda h: bias_rows(kind_m, h, N_META)] + near_bias(0)
            return step_fns(
                lambda h: jnp.concatenate([km_ref[0, :, h * hw:(h + 1) * hw], k_fn(0, NSUB)(h)], axis=0),
                [(lambda h: vtm_ref[0, h * A_VR:(h + 1) * A_VR, :], 0), (vt_fn(0, NSUB), N_META)],
                lambda h: jnp.concatenate([b(h) for b in biases], axis=0))

        def far_step(w):
            b0 = NSUB + w * NSUB_FAR
            far_bias = jnp.concatenate([bias_rows(KIND_FAR, h, 1) for h in range(A_HEADS)], axis=1)
            return step_fns(k_fn(b0, NSUB_FAR), [(vt_fn(b0, NSUB_FAR), 0)], None) + (far_bias,)

        def near_step(u):
            b0 = near0 + u * NSUB
            biases = near_bias(b0)
            return step_fns(k_fn(b0, NSUB), [(vt_fn(b0, NSUB), 0)],
                            lambda h: jnp.concatenate([b(h) for b in biases], axis=0)) + (None,)

        def sweep(lazy):
            init()
            _softmax_step(*first_fns(), m_ref, acc_ref, lazy)
            _softmax_loop(0, nfar, far_step, m_ref, acc_ref, lazy)
            _softmax_loop(0, nnear, near_step, m_ref, acc_ref, lazy)

        sweep(True)
        pl.when(_lazy_failed(acc_ref[A_VD:A_VD + 1, :]))(lambda: sweep(False))
        finalize()

    @pl.when(s_id == nb_real)
    def _():
        init()
        _softmax_step(*step_fns(lambda h: kmf_ref[:, h * hw:(h + 1) * hw],
                                [(lambda h: vtmf_ref[0, h * A_VR:(h + 1) * A_VR, :], 0)],
                                lambda h: bias_rows(KIND_METAMETA, h, LANES)), m_ref, acc_ref)
        finalize()


def _attn_a(qs, k, vt, bias, lam, subln, *, bsz, nblk, lambda_init):
    n = k.shape[0]
    nb_real = bsz * nblk
    seq = nblk * LANES
    d = A_QD
    vr = A_HEADS * A_VR
    assert nblk % NSUB == 0
    km, vtm = _meta_views(k[nb_real * LANES:], vt[nb_real], bsz)
    kern = functools.partial(_attn_a_kernel, nblk=nblk, lambda_init=lambda_init)
    bclamp = lambda s: jnp.minimum(s // nblk, bsz - 1)
    return pl.pallas_call(
        kern,
        grid=(nb_real + 1,),
        in_specs=[
            pl.BlockSpec((1,) + qs.shape[1:], lambda s: (s, 0, 0, 0)),
            pl.BlockSpec((seq, d), lambda s: (bclamp(s), 0)),
            pl.BlockSpec((nblk, vr, LANES), lambda s: (bclamp(s), 0, 0)),
            pl.BlockSpec((1, N_META, d), lambda s: (bclamp(s), 0, 0)),
            pl.BlockSpec((1, vr, N_META), lambda s: (bclamp(s), 0, 0)),
            pl.BlockSpec((LANES, d), lambda s: (nb_real, 0)),
            pl.BlockSpec((1, vr, LANES), lambda s: (nb_real, 0, 0)),
            _bias_spec(),
            pl.BlockSpec((4, A_HD), lambda s: (0, 0)),
            pl.BlockSpec((A_VD, LANES), lambda s: (0, 0)),
        ],
        out_specs=pl.BlockSpec((LANES, d), lambda s: (s, 0)),
        out_shape=jax.ShapeDtypeStruct((n, d), bf16),
        scratch_shapes=[
            pltpu.VMEM((1, A_HEADS * 2 * LANES), f32),
            pltpu.VMEM((A_VR, A_HEADS * 2 * LANES), f32),
        ],
        compiler_params=_cparams(("arbitrary",)),
        name="attn_a",
    )(qs, k, vt, km, vtm, k, vt, bias, lam, jnp.broadcast_to(subln[:, None], (A_VD, LANES)))


C_QD = C_Q_HEADS * C_HD
C_KD = 2 * C_KV_HEADS * C_HD
C_VR = 2 * C_HD + ONES_ROWS


def _proj_c_kernel(h_ref, g_ref, w_ref, gq_ref, gk_ref, qs_ref, k_ref, vt_ref):
    xn = (_rms_rows(h_ref[...]) * g_ref[...]).astype(bf16)
    y = _dot(xn, w_ref[...])
    q = _group_rms(y[:, :C_QD], C_HD) * gq_ref[...]
    lo = _lo_half_mask(q.shape, 2 * C_HD, C_HD)
    q_even = jnp.where(lo, q, 0.0).astype(bf16)
    q_odd = jnp.where(lo, 0.0, q).astype(bf16)
    k_ref[...] = (_group_rms(y[:, C_QD:C_QD + C_KD], C_HD) * gk_ref[...]).astype(bf16)
    ones = jnp.ones((ONES_ROWS, LANES), bf16)
    for t in range(vt_ref.shape[0]):
        rows = slice(t * LANES, (t + 1) * LANES)
        vt = y[rows, C_QD + C_KD:].T.astype(bf16)
        for g in range(C_KV_HEADS):
            for hh in range(C_GROUP):
                pair = (g * C_GROUP + hh) // 2
                src = q_even if hh % 2 == 0 else q_odd
                qs_ref[t, g, hh * LANES:(hh + 1) * LANES, :] = src[rows, pair * LANES:(pair + 1) * LANES]
            vt_ref[t, g * C_VR:g * C_VR + 2 * C_HD, :] = vt[g * 2 * C_HD:(g + 1) * 2 * C_HD]
            vt_ref[t, g * C_VR + 2 * C_HD:(g + 1) * C_VR, :] = ones


def _proj_c(h, g, w, qk_norm, *, tm):
    n, d = h.shape
    nw = w.shape[1]
    nt = tm // LANES
    gq = jnp.tile(qk_norm[0] * (C_HD ** -0.5 * LOG2E), C_QD // C_HD).reshape(1, C_QD)
    gk = jnp.tile(qk_norm[1], C_KD // C_HD).reshape(1, C_KD)
    return pl.pallas_call(
        _proj_c_kernel,
        grid=(n // tm,),
        in_specs=[
            pl.BlockSpec((tm, d), lambda i: (i, 0)),
            pl.BlockSpec((1, d), lambda i: (0, 0)),
            pl.BlockSpec((d, nw), lambda i: (0, 0)),
            pl.BlockSpec((1, C_QD), lambda i: (0, 0)),
            pl.BlockSpec((1, C_KD), lambda i: (0, 0)),
        ],
        out_specs=[
            pl.BlockSpec((nt, C_KV_HEADS, C_GROUP * LANES, LANES), lambda i: (i, 0, 0, 0)),
            pl.BlockSpec((tm, C_KD), lambda i: (i, 0)),
            pl.BlockSpec((nt, C_KV_HEADS * C_VR, LANES), lambda i: (i, 0, 0)),
        ],
        out_shape=[
            jax.ShapeDtypeStruct((n // LANES, C_KV_HEADS, C_GROUP * LANES, LANES), bf16),
            jax.ShapeDtypeStruct((n, C_KD), bf16),
            jax.ShapeDtypeStruct((n // LANES, C_KV_HEADS * C_VR, LANES), bf16),
        ],
        compiler_params=_cparams(("arbitrary",)),
        name="proj_c",
    )(h, g.reshape(1, d), w, gq, gk)


def _attn_c_kernel(sink_ref, qs_ref, k_ref, vt_ref, km_ref, vtm_ref, kmf_ref, vtmf_ref, bias_ref, o_ref, *, nblk):
    s_id = pl.program_id(0)
    nb_real = pl.num_programs(0) - 1
    r_io, _ = _tile_iotas()
    vd = 2 * C_HD

    def attend_with(tiles, lazy):
        top = r_io < C_HD
        failed = None
        for g in range(C_KV_HEADS):
            sink = jnp.concatenate(
                [jnp.full((1, LANES), sink_ref[g * C_GROUP + hh] * LOG2E, f32) for hh in range(C_GROUP)], axis=1)
            sts = []
            m = sink
            for (k_fn, vt_fn, bias_fn) in tiles:
                st = _dot_nt(k_fn(g), qs_ref[0, g])
                st = st + jnp.concatenate([bias_fn(g * C_GROUP + hh) for hh in range(C_GROUP)], axis=1)
                if not lazy:
                    m = jnp.maximum(m, jnp.max(st, axis=0, keepdims=True))
                sts.append(st)
            acc = None
            for st, (k_fn, vt_fn, bias_fn) in zip(sts, tiles):
                pv = _dot(vt_fn(g), jnp.exp2(st if lazy else st - m).astype(bf16))
                acc = pv if acc is None else acc + pv
            denom = acc[vd:vd + 1, :] + jnp.exp2(sink if lazy else sink - m)
            if lazy:
                failed = _lazy_failed(denom) if failed is None else failed | _lazy_failed(denom)
            o = acc[0:vd, :] * (1.0 / denom)
            for cc in range(C_GROUP // 2):
                even = o[:, (2 * cc) * LANES:(2 * cc + 1) * LANES]
                odd = o[:, (2 * cc + 1) * LANES:(2 * cc + 2) * LANES]
                col = (g * (C_GROUP // 2) + cc) * LANES
                o_ref[:, col:col + LANES] = jnp.where(top, even, odd).T.astype(bf16)
        return failed

    def attend(tiles):
        failed = attend_with(tiles, True)

        @pl.when(failed)
        def _():
            attend_with(tiles, False)

    @pl.when(s_id < nb_real)
    def _():
        i = s_id % nblk
        prev = jnp.maximum(i - 1, 0)
        poff = pl.multiple_of(prev * LANES, LANES)
        coff = pl.multiple_of(i * LANES, LANES)
        kind_m = jnp.where(i == 0, KIND_META0, KIND_FAR)
        kind_p = jnp.where(i == 0, KIND_MASKED, KIND_PREVWIN)
        attend([
            (lambda g: km_ref[0, :, g * LANES:(g + 1) * LANES], lambda g: vtm_ref[0, g * C_VR:(g + 1) * C_VR, :],
             lambda h: bias_ref[kind_m, h, 0:N_META, :]),
            (lambda g: k_ref[pl.ds(poff, LANES), g * LANES:(g + 1) * LANES],
             lambda g: vt_ref[prev, g * C_VR:(g + 1) * C_VR, :], lambda h: bias_ref[kind_p, h]),
            (lambda g: k_ref[pl.ds(coff, LANES), g * LANES:(g + 1) * LANES],
             lambda g: vt_ref[i, g * C_VR:(g + 1) * C_VR, :], lambda h: bias_ref[KIND_DIAG, h]),
        ])

    @pl.when(s_id == nb_real)
    def _():
        attend([(lambda g: kmf_ref[:, g * LANES:(g + 1) * LANES], lambda g: vtmf_ref[0, g * C_VR:(g + 1) * C_VR, :],
                 lambda h: bias_ref[KIND_METAMETA, h])])


def _attn_c(qs, k, vt, bias, sinks, *, bsz, nblk):
    n = k.shape[0]
    nb_real = bsz * nblk
    seq = nblk * LANES
    vr = C_KV_HEADS * C_VR
    kern = functools.partial(_attn_c_kernel, nblk=nblk)
    bclamp = lambda s: jnp.minimum(s // nblk, bsz - 1)
    km, vtm = _meta_views(k[nb_real * LANES:], vt[nb_real], bsz)
    return pl.pallas_call(
        kern,
        grid=(nb_real + 1,),
        in_specs=[
            pl.BlockSpec(memory_space=pltpu.SMEM),
            pl.BlockSpec((1,) + qs.shape[1:], lambda s: (s, 0, 0, 0)),
            pl.BlockSpec((seq, C_KD), lambda s: (bclamp(s), 0)),
            pl.BlockSpec((nblk, vr, LANES), lambda s: (bclamp(s), 0, 0)),
            pl.BlockSpec((1, N_META, C_KD), lambda s: (bclamp(s), 0, 0)),
            pl.BlockSpec((1, vr, N_META), lambda s: (bclamp(s), 0, 0)),
            pl.BlockSpec((LANES, C_KD), lambda s: (nb_real, 0)),
            pl.BlockSpec((1, vr, LANES), lambda s: (nb_real, 0, 0)),
            _bias_spec(),
        ],
        out_specs=pl.BlockSpec((LANES, C_QD), lambda s: (s, 0)),
        out_shape=jax.ShapeDtypeStruct((n, C_QD), bf16),
        compiler_params=_cparams(("arbitrary",)),
        name="attn_c",
    )(sinks, qs, k, vt, km, vtm, k, vt, bias)


B_QA = B_HEADS * B_KV_RANK
B_QI = IDX_HEADS * IDX_DIM
B_W1 = 2 * B_Q_RANK + 2 * LANES
B_TR = B_KV_RANK + ONES_ROWS


def _proj_b_kernel(h_ref, g_ref, w1_ref, ln_ref, wuq_ref, qn_ref,
                   qa_ref, qi_ref, ckv_ref, ckvt_ref, kk_ref, wit_ref):
    xn = (_rms_rows(h_ref[...]) * g_ref[...]).astype(bf16)
    y = _dot(xn, w1_ref[...])
    r = B_Q_RANK
    cq = (_rms_rows(y[:, :r]) * ln_ref[0:1, :]).astype(bf16)
    ckv = _rms_rows(y[:, r:2 * r]) * ln_ref[1:2, :]
    ckv_ref[...] = ckv.astype(bf16)
    kk_ref[...] = _rms_rows(y[:, 2 * r:2 * r + LANES]).astype(bf16)
    wi = y[:, 2 * r + LANES:] * (IDX_HEADS ** -0.5)
    ones = jnp.ones((ONES_ROWS, LANES), bf16)
    for t in range(ckvt_ref.shape[0]):
        ckvt_ref[t, 0:r, :] = ckv[t * LANES:(t + 1) * LANES, :].T.astype(bf16)
        ckvt_ref[t, r:, :] = ones
        wit_ref[t] = wi[t * LANES:(t + 1) * LANES, :].T[0:IDX_HEADS, :]
    z = _dot(cq, wuq_ref[...])
    qa = (_group_rms(z[:, :B_QA], B_KV_RANK) * qn_ref[...]).astype(bf16)
    qi = z[:, B_QA:] * (IDX_DIM ** -0.5)
    lo = _lo_half_mask(qi.shape, 2 * IDX_DIM, IDX_DIM)
    qi_lo = jnp.where(lo, qi, 0.0).astype(bf16)
    qi_hi = jnp.where(lo, 0.0, qi).astype(bf16)
    for t in range(qa_ref.shape[0]):
        rows = slice(t * LANES, (t + 1) * LANES)
        for hd in range(B_HEADS):
            qa_ref[t, hd * LANES:(hd + 1) * LANES, :] = qa[rows, hd * r:(hd + 1) * r]
        for hh in range(IDX_HEADS):
            src = qi_lo if hh % 2 == 0 else qi_hi
            qi_ref[t, hh * LANES:(hh + 1) * LANES, :] = src[rows, (hh // 2) * LANES:(hh // 2 + 1) * LANES]


def _proj_b(h, g, w1, latent_norm, wuq, q_norm, *, tm):
    n, d = h.shape
    qn = jnp.tile(q_norm * (B_KV_RANK ** -0.5 * LOG2E), B_HEADS).reshape(1, B_QA)
    row = lambda i: (i, 0)
    row3 = lambda i: (i, 0, 0)
    const = lambda i: (0, 0)
    nt = tm // LANES
    return pl.pallas_call(
        _proj_b_kernel,
        grid=(n // tm,),
        in_specs=[
            pl.BlockSpec((tm, d), row),
            pl.BlockSpec((1, d), const),
            pl.BlockSpec(w1.shape, const),
            pl.BlockSpec(latent_norm.shape, const),
            pl.BlockSpec(wuq.shape, const),
            pl.BlockSpec((1, B_QA), const),
        ],
        out_specs=[
            pl.BlockSpec((nt, B_HEADS * LANES, B_KV_RANK), row3),
            pl.BlockSpec((nt, IDX_HEADS * LANES, LANES), row3),
            pl.BlockSpec((tm, B_KV_RANK), row),
            pl.BlockSpec((nt, B_TR, LANES), row3),
            pl.BlockSpec((tm, LANES), row),
            pl.BlockSpec((nt, IDX_HEADS, LANES), row3),
        ],
        out_shape=[
            jax.ShapeDtypeStruct((n // LANES, B_HEADS * LANES, B_KV_RANK), bf16),
            jax.ShapeDtypeStruct((n // LANES, IDX_HEADS * LANES, LANES), bf16),
            jax.ShapeDtypeStruct((n, B_KV_RANK), bf16),
            jax.ShapeDtypeStruct((n // LANES, B_TR, LANES), bf16),
            jax.ShapeDtypeStruct((n, LANES), bf16),
            jax.ShapeDtypeStruct((n // LANES, IDX_HEADS, LANES), f32),
        ],
        compiler_params=_cparams(("arbitrary",)),
        name="proj_b",
    )(h, g.reshape(1, d), w1, latent_norm, wuq, qn)


def _attn_b_kernel(qs_ref, is_ref, wit_ref, ckv_ref, ckvt_ref, kk_ref, ckvm_ref, ckvtm_ref, kkm_ref,
                   ckvmf_ref, ckvtmf_ref, bias_ref, wuvt_ref,
                   o_ref, key_ref, pen_ref, m_ref, acc_ref, *, nblk, k_sel):
    s_id = pl.program_id(0)
    nb_real = pl.num_programs(0) - 1
    r_io, c_io = _tile_iotas()
    rk = B_KV_RANK

    def init():
        m_ref[...] = jnp.full(m_ref.shape, NEG_INF, f32)
        acc_ref[...] = jnp.zeros(acc_ref.shape, f32)

    def add_per_head(st, bias_fn, pen):
        cols = []
        for h in range(B_HEADS):
            add = pen if bias_fn is None else (bias_fn(h) if pen is None else bias_fn(h) + pen)
            cols.append(st[:, h * LANES:(h + 1) * LANES] + add)
        return jnp.concatenate(cols, axis=1)

    def step_fns(ckv_fn, ckvt_fns, bias_fn, pen_fn):
        def logits():
            st = _dot_nt(ckv_fn(), qs_ref[0])
            return add_per_head(st, bias_fn, None if pen_fn is None else pen_fn())

        def pv(pb):
            acc = None
            for ckvt_fn, r0 in ckvt_fns:
                ckvt = ckvt_fn()
                part = _dot(ckvt, pb[r0:r0 + ckvt.shape[1], :])
                acc = part if acc is None else acc + part
            return acc

        return logits, pv

    def finalize():
        olat = (acc_ref[0:rk, :] * (1.0 / acc_ref[rk:rk + 1, :])).astype(bf16)
        ot = jnp.concatenate([_dot(wuvt_ref[h], olat[:, h * LANES:(h + 1) * LANES]) for h in range(B_HEADS)], axis=0)
        o_ref[...] = ot.T.astype(bf16)

    def index_scores(kk):
        s = jnp.maximum(_dot_nt(kk, is_ref[0]), 0.0)
        wt = wit_ref[0]
        sc = jnp.zeros((kk.shape[0], LANES), f32)
        for hh in range(IDX_HEADS):
            sc = sc + wt[hh:hh + 1, :] * s[:, hh * LANES:(hh + 1) * LANES]
        return sc

    def sort_key(sc):
        bits = lax.bitcast_convert_type(sc + 0.0, jnp.int32)
        return jnp.where(bits < 0, bits ^ jnp.int32(0x7FFFFFFF), bits)

    @pl.when(s_id < nb_real)
    def _():
        i = s_id % nblk
        ntile = i + 2

        int_min_tile = jnp.full((LANES, LANES), INT_MIN, jnp.int32)
        key_ref[0] = int_min_tile
        key_ref[0, 0:N_META, :] = sort_key(index_scores(kkm_ref[0]))
        key_ref[i + 2] = int_min_tile

        def score_body(jp, carry):
            off = pl.multiple_of(jp * (2 * LANES), 2 * LANES)
            keys = sort_key(index_scores(kk_ref[pl.ds(off, 2 * LANES), :]))
            for t in range(2):
                j = _vec(2 * jp + t)
                vis = (j < i) | ((j == i) & ((r_io >> 6) <= (c_io >> 6)))
                key_ref[2 * jp + t + 1] = jnp.where(vis, keys[t * LANES:(t + 1) * LANES], jnp.int32(INT_MIN))
            return carry

        lax.fori_loop(0, i // 2 + 1, score_body, 0)

        def count(pred):
            def cbody(tp, accv):
                for t in (2 * tp, 2 * tp + 1):
                    accv = accv + jnp.where(pred(key_ref[t], t), 1.0, 0.0)
                return accv
            accv = lax.fori_loop(0, (ntile + 1) // 2, cbody, jnp.zeros((LANES, LANES), f32))
            return jnp.sum(accv, axis=0, keepdims=True)

        kf = float(k_sel)
        zero = jnp.zeros((1, LANES), jnp.int32)
        t0 = jnp.where(count(lambda k, t: k >= zero) >= kf, zero, jnp.int32(INT_MIN))

        def bit_body(it, tcur):
            cand = tcur | jnp.left_shift(jnp.int32(1), 30 - it)
            return jnp.where(count(lambda k, t: k >= cand) >= kf, cand, tcur)

        thr = lax.fori_loop(0, 31, bit_body, t0)

        need = kf - count(lambda k, t: k > thr)
        n_eq = count(lambda k, t: k == thr)
        has_thr = thr > jnp.int32(INT_MIN)
        tied = jnp.max(jnp.where(has_thr & (n_eq > need), 1.0, 0.0)) > 0.0

        def tie_search(_):
            def jbody(it, jcur):
                cand = jcur | jnp.left_shift(jnp.int32(1), 11 - it)
                cnt = count(lambda k, t: (k == thr) & ((t * LANES + r_io) < cand))
                return jnp.where(cnt < need, cand, jcur)
            return lax.fori_loop(0, 12, jbody, jnp.zeros((1, LANES), jnp.int32))

        j_last = lax.cond(tied, tie_search, lambda _: jnp.full((1, LANES), 4095, jnp.int32), 0)
        j_last = jnp.where(has_thr, j_last, -1)

        def pen_body(t, carry):
            k = key_ref[t]
            sel = (k > thr) | ((k == thr) & ((t * LANES + r_io) <= j_last))
            pen_ref[t] = jnp.where(sel, 0.0, NEG_INF)
            return carry

        lax.fori_loop(0, ntile, pen_body, 0)
        for t in range(1, NSUB):
            pen_ref[i + 1 + t] = jnp.full((LANES, LANES), NEG_INF, f32)

        kind_m = jnp.where(i == 0, KIND_META0, KIND_FAR)
        nfar, near0, nnear = _sweep_steps(i)

        def ckv_fn(b0, nsub):
            off = pl.multiple_of(b0 * LANES, NSUB * LANES)
            return lambda: ckv_ref[pl.ds(off, nsub * LANES), :]

        def ckvt_fn(b0, nsub):
            return lambda: jnp.concatenate([ckvt_ref[b0 + t] for t in range(nsub)], axis=1)

        def pen_fn(b0, nsub):
            return lambda: jnp.concatenate([pen_ref[b0 + t + 1] for t in range(nsub)], axis=0)

        def near_bias(b0):
            kinds = [_block_kind(b0 + t - i) for t in range(NSUB)]
            return lambda h: jnp.concatenate([bias_ref[kinds[t], h] for t in range(NSUB)], axis=0)

        def first_pen():
            return jnp.concatenate([pen_ref[0, 0:N_META, :], pen_fn(0, NSUB)()], axis=0)

        def first_fns():
            bias01 = near_bias(0)
            return step_fns(lambda: jnp.concatenate([ckvm_ref[0], ckv_fn(0, NSUB)()], axis=0),
                            [(lambda: ckvtm_ref[0], 0), (ckvt_fn(0, NSUB), N_META)],
                            lambda h: jnp.concatenate([bias_ref[kind_m, h, 0:N_META, :], bias01(h)], axis=0),
                            first_pen)

        def far_step(w):
            b0 = NSUB + w * NSUB_FAR
            far_bias = jnp.concatenate([bias_ref[KIND_FAR, h, 0:1, :] for h in range(B_HEADS)], axis=1)
            return step_fns(ckv_fn(b0, NSUB_FAR), [(ckvt_fn(b0, NSUB_FAR), 0)], None, pen_fn(b0, NSUB_FAR)) + (far_bias,)

        def near_step(u):
            b0 = near0 + u * NSUB
            return step_fns(ckv_fn(b0, NSUB), [(ckvt_fn(b0, NSUB), 0)], near_bias(b0), pen_fn(b0, NSUB)) + (None,)

        def sweep(lazy):
            init()
            _softmax_step(*first_fns(), m_ref, acc_ref, lazy)
            _softmax_loop(0, nfar, far_step, m_ref, acc_ref, lazy)
            _softmax_loop(0, nnear, near_step, m_ref, acc_ref, lazy)

        sweep(True)
        pl.when(_lazy_failed(acc_ref[rk:rk + 1, :]))(lambda: sweep(False))
        finalize()

    @pl.when(s_id == nb_real)
    def _():
        init()
        _softmax_step(*step_fns(lambda: ckvmf_ref[...], [(lambda: ckvtmf_ref[0], 0)],
                                lambda h: bias_ref[KIND_METAMETA, h], None), m_ref, acc_ref)
        finalize()


def _attn_b(qa, qi, wit, ckv, ckvt, kk, bias, wuvt, *, bsz, nblk, k_sel):
    n = ckv.shape[0]
    nb_real = bsz * nblk
    seq = nblk * LANES
    assert k_sel >= N_META and (nblk + 1) * LANES <= 4096 and nblk % NSUB == 0
    ckvm, ckvtm = _meta_views(ckv[nb_real * LANES:], ckvt[nb_real], bsz)
    kkm = kk[nb_real * LANES:].reshape(bsz, N_META, LANES)
    kern = functools.partial(_attn_b_kernel, nblk=nblk, k_sel=k_sel)
    bidx = lambda s: jnp.minimum(s // nblk, bsz - 1)
    blk = lambda s: (s, 0)
    return pl.pallas_call(
        kern,
        grid=(nb_real + 1,),
        in_specs=[
            pl.BlockSpec((1,) + qa.shape[1:], lambda s: (s, 0, 0)),
            pl.BlockSpec((1,) + qi.shape[1:], lambda s: (s, 0, 0)),
            pl.BlockSpec((1, IDX_HEADS, LANES), lambda s: (s, 0, 0)),
            pl.BlockSpec((seq, B_KV_RANK), lambda s: (bidx(s), 0)),
            pl.BlockSpec((nblk, B_TR, LANES), lambda s: (bidx(s), 0, 0)),
            pl.BlockSpec((seq, LANES), lambda s: (bidx(s), 0)),
            pl.BlockSpec((1, N_META, B_KV_RANK), lambda s: (bidx(s), 0, 0)),
            pl.BlockSpec((1, B_TR, N_META), lambda s: (bidx(s), 0, 0)),
            pl.BlockSpec((1, N_META, LANES), lambda s: (bidx(s), 0, 0)),
            pl.BlockSpec((LANES, B_KV_RANK), lambda s: (nb_real, 0)),
            pl.BlockSpec((1, B_TR, LANES), lambda s: (nb_real, 0, 0)),
            _bias_spec(),
            pl.BlockSpec(wuvt.shape, lambda s: (0, 0, 0)),
        ],
        out_specs=pl.BlockSpec((LANES, B_HEADS * B_VD), blk),
        out_shape=jax.ShapeDtypeStruct((n, B_HEADS * B_VD), bf16),
        scratch_shapes=[
            pltpu.VMEM((nblk + 2, LANES, LANES), jnp.int32),
            pltpu.VMEM((nblk + NSUB, LANES, LANES), f32),
            pltpu.VMEM((1, B_HEADS * LANES), f32),
            pltpu.VMEM((B_TR, B_HEADS * LANES), f32),
        ],
        compiler_params=_cparams(("arbitrary",)),
        name="attn_b",
    )(qa, qi, wit, ckv, ckvt, kk, ckvm, ckvtm, kkm, ckv, ckvt, bias, wuvt)


def kernel(x, meta_tokens, rel_bias, ln_ffn1, ffn1_wi, ffn1_wo, ln_mix, w_out, ln_ffn2, ffn2_wi, ffn2_wo, a_w_in, a_qk_norm, a_lambda, a_subln, b_w_in, b_latent_norm, b_w_uq, b_q_norm, b_w_uv, c_w_in, c_qk_norm, c_sinks):
    bsz, seq, d = x.shape
    assert d == D_MODEL and seq % LANES == 0 and bsz * N_META == LANES
    nblk = seq // LANES
    n = bsz * seq + LANES
    k_sel = min(TOPK_MAX, seq // 4)
    tm_ffn = _row_tile(n, 1408)
    tm_last = _row_tile(bsz * seq, 1408)
    tm_proj = _row_tile(n, 384, LANES)
    fc = 256

    h = x.reshape(bsz * seq, d)
    meta_rows = jnp.broadcast_to(meta_tokens.astype(x.dtype), (bsz, N_META, d)).reshape(LANES, d)
    bias = _bias_tiles(rel_bias)

    for layer in range(DEPTH):
        h = _ffn(h, ln_ffn1[layer], ffn1_wi, ffn1_wo, layer, tm=tm_ffn, fc=fc,
                 tail=meta_rows if layer == 0 else None)
        kind, j = layer % N_MIXERS, layer // N_MIXERS
        g = ln_mix[layer]
        if kind == 0:
            lambda_init = 0.8 - 0.6 * math.exp(-0.3 * layer)
            qs, k, vt = _proj_a(h, g, a_w_in[j].astype(bf16), a_qk_norm[j], tm=tm_proj)
            mix = _attn_a(qs, k, vt, bias, a_lambda[j], a_subln[j], bsz=bsz, nblk=nblk, lambda_init=lambda_init)
        elif kind == 1:
            w = b_w_in[j]
            r2 = B_Q_RANK + B_KV_RANK
            kcol = w[:, r2:r2 + IDX_DIM]
            w1 = jnp.concatenate([w[:, :r2], kcol, kcol, w[:, r2 + IDX_DIM:],
                                  jnp.zeros((d, LANES - IDX_HEADS), w.dtype)], axis=1).astype(bf16)
            assert w1.shape[1] == B_W1
            qa, qi, ckv, ckvt, kk, wit = _proj_b(h, g, w1, b_latent_norm[j], b_w_uq[j].astype(bf16), b_q_norm[j],
                                                 tm=tm_proj)
            wuvt = jnp.swapaxes(b_w_uv[j], 1, 2).astype(bf16)
            mix = _attn_b(qa, qi, wit, ckv, ckvt, kk, bias, wuvt, bsz=bsz, nblk=nblk, k_sel=k_sel)
        else:
            w = c_w_in[j]
            kcols = [w[:, C_QD + gi * C_HD:C_QD + (gi + 1) * C_HD] for gi in range(C_KV_HEADS)]
            voff = C_QD + C_KV_HEADS * C_HD
            vcols = [w[:, voff + gi * C_HD:voff + (gi + 1) * C_HD] for gi in range(C_KV_HEADS)]
            wc = jnp.concatenate([w[:, :C_QD]] + [kc for kc in kcols for _ in range(2)]
                                 + [vc for vc in vcols for _ in range(2)], axis=1).astype(bf16)
            qs, k, vt = _proj_c(h, g, wc, c_qk_norm[j], tm=tm_proj)
            mix = _attn_c(qs, k, vt, bias, c_sinks[j], bsz=bsz, nblk=nblk)
        last = layer == DEPTH - 1
        h = _ffn(h, ln_ffn2[layer], ffn2_wi, ffn2_wo, layer, tm=tm_last if last else tm_ffn, fc=fc,
                 mix=mix, wout=w_out[layer].astype(bf16), n_rows=bsz * seq if last else None)
    return h.reshape(bsz, seq, d)
```

```python
import functools
import math

import numpy as np
import jax
import jax.numpy as jnp
from jax import lax
from jax.experimental import pallas as pl
from jax.experimental.pallas import tpu as pltpu

D_MODEL = 1024
DEPTH = 4
CHUNK = 64
N_META = 16
N_MIXERS = 3
NEG_INF = -1e30
REL_BUCKETS = 32
REL_MAX_DIST = 128
REL_HEADS = 16
D_FF = 2816
A_HEADS = 8
A_HD = 64
A_VD = 2 * A_HD
B_HEADS = 16
B_Q_RANK = 256
B_KV_RANK = 256
B_VD = 64
IDX_HEADS = 8
IDX_DIM = 64
TOPK_MAX = 256
C_Q_HEADS = 16
C_KV_HEADS = 2
C_GROUP = C_Q_HEADS // C_KV_HEADS
C_HD = 64
EPS = 1e-6

LANES = 128
BF16_ROWS = 16
VMEM_LIMIT = 56 * 1024 * 1024
INT_MIN = -(2 ** 31)
NSUB = 2
NSUB_FAR = 4
COUNT_TILES = 4
LOG2E = math.log2(math.e)
LAZY_CEIL = 2.0 ** 70
LAZY_FLOOR = 2.0 ** -100
ONES_ROWS = BF16_ROWS

KIND_DIAG, KIND_PREV, KIND_FAR, KIND_META0, KIND_METAMETA, KIND_MASKED, KIND_PREVWIN = 0, 1, 2, 3, 4, 5, 6
N_KINDS = 7

f32 = jnp.float32
bf16 = jnp.bfloat16


def _cparams(sem):
    return pltpu.CompilerParams(dimension_semantics=sem, vmem_limit_bytes=VMEM_LIMIT)


def _row_tile(n, cap, mult=BF16_ROWS):
    best = None
    for t in range(mult, cap + 1, mult):
        if n % t == 0:
            best = t
    assert best is not None
    return best


def _dot(a, b):
    return jnp.dot(a, b, preferred_element_type=f32)


def _dot_nt(a, b):
    return lax.dot_general(a, b, (((1,), (1,)), ((), ())), preferred_element_type=f32)


def _rms_rows(x):
    return x * lax.rsqrt(jnp.mean(x * x, axis=-1, keepdims=True) + EPS)


def _lo_half_mask(shape, period, half):
    return (lax.broadcasted_iota(jnp.int32, shape, 1) & (period - 1)) < half


def _group_rms(x, group):
    r, c = x.shape
    outs = []
    if group == 64:
        lo = _lo_half_mask((r, LANES), LANES, 64)
        for ci in range(c // LANES):
            xc = x[:, ci * LANES:(ci + 1) * LANES]
            x2 = xc * xc
            s_lo = jnp.sum(jnp.where(lo, x2, 0.0), axis=-1, keepdims=True)
            s_hi = jnp.sum(jnp.where(lo, 0.0, x2), axis=-1, keepdims=True)
            inv = jnp.where(lo, lax.rsqrt(s_lo * (1.0 / 64) + EPS), lax.rsqrt(s_hi * (1.0 / 64) + EPS))
            outs.append(xc * inv)
    else:
        for gi in range(c // group):
            outs.append(_rms_rows(x[:, gi * group:(gi + 1) * group]))
    return outs[0] if len(outs) == 1 else jnp.concatenate(outs, axis=-1)


def _tile_iotas():
    r = lax.broadcasted_iota(jnp.int32, (LANES, LANES), 0)
    c = lax.broadcasted_iota(jnp.int32, (LANES, LANES), 1)
    return r, c


def _vec(s):
    return jnp.full((LANES, LANES), s, jnp.int32)


def _softmax_step(logits_fn, pv_fn, m_ref, acc_ref, lazy=False, offset=None):
    st = logits_fn()
    if offset is not None:
        st = st + offset
    if lazy:
        acc_ref[...] += pv_fn(jnp.exp2(st).astype(bf16))
    else:
        m_old = m_ref[...]
        m_new = jnp.maximum(m_old, jnp.max(st, axis=0, keepdims=True))
        acc_ref[...] = jnp.exp2(m_old - m_new) * acc_ref[...] + pv_fn(jnp.exp2(st - m_new).astype(bf16))
        m_ref[...] = m_new


def _softmax_loop(lo, hi, step_fn, m_ref, acc_ref, lazy):
    def body(w, carry):
        logits_fn, pv_fn, offset = step_fn(w)
        _softmax_step(logits_fn, pv_fn, m_ref, acc_ref, lazy, offset)
        return carry

    lax.fori_loop(lo, hi, body, 0)


def _lazy_failed(denominators):
    return jnp.logical_not((jnp.max(denominators) <= LAZY_CEIL) & (jnp.min(denominators) >= LAZY_FLOOR))


def _block_kind(rel):
    return jnp.where(rel < -1, KIND_FAR,
                     jnp.where(rel == -1, KIND_PREV, jnp.where(rel == 0, KIND_DIAG, KIND_MASKED)))


def _sweep_steps(i):
    nfar = jnp.maximum(i - 1 - NSUB, 0) // NSUB_FAR
    near0 = NSUB + nfar * NSUB_FAR
    return nfar, near0, (i - near0 + NSUB) // NSUB


def _ffn_kernel(*refs, fuse_out, tail_rows, layer, fc, nj):
    if fuse_out:
        h_ref, mix_ref, wout_ref, g_ref, wi_hbm, wo_hbm, o_ref, xn_ref, wa_buf, wb_buf, wo_buf, sem = refs
    elif tail_rows:
        h_ref, tail_ref, g_ref, wi_hbm, wo_hbm, o_ref, xn_ref, wa_buf, wb_buf, wo_buf, sem = refs
    else:
        h_ref, g_ref, wi_hbm, wo_hbm, o_ref, xn_ref, wa_buf, wb_buf, wo_buf, sem = refs

    def chunk_copies(j, slot):
        lo = pl.multiple_of(j * fc, fc)
        hi = pl.multiple_of((nj + j) * fc, fc)
        return (pltpu.make_async_copy(wi_hbm.at[layer, :, pl.ds(lo, fc)], wa_buf.at[slot], sem.at[0, slot]),
                pltpu.make_async_copy(wi_hbm.at[layer, :, pl.ds(hi, fc)], wb_buf.at[slot], sem.at[1, slot]),
                pltpu.make_async_copy(wo_hbm.at[layer, pl.ds(lo, fc), :], wo_buf.at[slot], sem.at[2, slot]))

    i = pl.program_id(0)
    first = i * nj

    @pl.when(i == 0)
    def _():
        for c in chunk_copies(0, 0):
            c.start()

    if tail_rows:
        last = pl.num_programs(0) - 1
        nvalid = o_ref.shape[0] - tail_rows

        @pl.when(i < last)
        def _():
            o_ref[...] = h_ref[...]

        @pl.when(i == last)
        def _():
            o_ref[0:nvalid, :] = h_ref[0:nvalid, :]
            o_ref[nvalid:, :] = tail_ref[...]

        r = o_ref[...]
    else:
        r = h_ref[...]
        if fuse_out:
            r = r + _dot(mix_ref[...], wout_ref[...])
        o_ref[...] = r
    xn_ref[...] = (_rms_rows(r) * g_ref[...]).astype(bf16)

    def body(j, carry):
        slot = (first + j) & 1

        @pl.when((j + 1 < nj) | (i + 1 < pl.num_programs(0)))
        def _():
            for c in chunk_copies(jnp.where(j + 1 < nj, j + 1, 0), 1 - slot):
                c.start()

        for c in chunk_copies(j, slot):
            c.wait()
        xn = xn_ref[...]
        a = _dot(xn, wa_buf[slot].astype(bf16))
        b = _dot(xn, wb_buf[slot].astype(bf16))
        act = (a / (1.0 + jnp.exp(-a)) * b).astype(bf16)
        o_ref[...] += 0.5 * _dot(act, wo_buf[slot].astype(bf16))
        return carry

    lax.fori_loop(0, nj, body, 0)


def _ffn(h, g, wi, wo, layer, *, tm, fc, mix=None, wout=None, n_rows=None, tail=None):
    d = h.shape[1]
    tail_rows = 0 if tail is None else tail.shape[0]
    n = h.shape[0] + tail_rows if n_rows is None else n_rows
    assert n % tm == 0 and 0 <= tail_rows < tm and not (tail_rows and (mix is not None or n_rows is not None))
    dff = wo.shape[1]
    nj = dff // fc
    fuse = mix is not None
    row = lambda i: (i, 0)
    in_specs = [pl.BlockSpec((tm, d), row)]
    args = [h]
    if tail_rows:
        in_specs += [pl.BlockSpec(tail.shape, lambda i: (0, 0))]
        args += [tail]
    if fuse:
        in_specs += [pl.BlockSpec((tm, mix.shape[1]), row), pl.BlockSpec(wout.shape, lambda i: (0, 0))]
        args += [mix, wout]
    in_specs += [
        pl.BlockSpec((1, d), lambda i: (0, 0)),
        pl.BlockSpec(memory_space=pl.ANY),
        pl.BlockSpec(memory_space=pl.ANY),
    ]
    args += [g.reshape(1, d), wi, wo]
    return pl.pallas_call(
        functools.partial(_ffn_kernel, fuse_out=fuse, tail_rows=tail_rows, layer=layer, fc=fc, nj=nj),
        grid=(n // tm,),
        in_specs=in_specs,
        out_specs=pl.BlockSpec((tm, d), row),
        out_shape=jax.ShapeDtypeStruct((n, d), f32),
        scratch_shapes=[
            pltpu.VMEM((tm, d), bf16),
            pltpu.VMEM((2, d, fc), f32),
            pltpu.VMEM((2, d, fc), f32),
            pltpu.VMEM((2, fc, d), f32),
            pltpu.SemaphoreType.DMA((3, 2)),
        ],
        compiler_params=_cparams(("arbitrary",)),
        name="ffn_out" if fuse else "ffn",
    )(*args)


def _rel_bucket(rel):
    half = REL_BUCKETS // 2
    max_exact = half // 2
    n = jnp.abs(rel)
    large = max_exact + (jnp.log(jnp.maximum(n, 1).astype(jnp.float32) / max_exact)
                         / math.log(REL_MAX_DIST / max_exact) * (half - max_exact)).astype(jnp.int32)
    large = jnp.minimum(large, half - 1)
    return jnp.where(rel > 0, half, 0) + jnp.where(n < max_exact, n, large)


def _rel_tiles():
    k = np.arange(LANES)[:, None]
    q = np.arange(LANES)[None, :]
    far = np.full((LANES, LANES), -4 * LANES)
    ones = np.ones((LANES, LANES), bool)
    rels = [k - q, k - q - LANES, far, (k % N_META) - N_META - q, (k % N_META) - (q % N_META), far, k - q - LANES]
    vis = [(k // CHUNK) <= (q // CHUNK), ones, ones, ones, (k // N_META) == (q // N_META), ~ones,
           (q < CHUNK) | (k >= CHUNK)]
    return (np.stack([np.broadcast_to(a, (LANES, LANES)) for a in rels]).astype(np.int32),
            np.stack([np.broadcast_to(a, (LANES, LANES)) for a in vis]).astype(np.int32))


def _bias_kernel(rb_ref, bucket_ref, vis_ref, o_ref):
    h = pl.program_id(0)
    for kind in range(N_KINDS):
        bk = bucket_ref[kind]
        acc = jnp.zeros((LANES, LANES), f32)
        for b in range(REL_BUCKETS):
            acc = jnp.where(bk == b, rb_ref[b, h], acc)
        o_ref[kind, 0] = jnp.where(vis_ref[kind] != 0, acc * LOG2E, NEG_INF)


def _bias_tiles(rel_bias):
    rel, vis = _rel_tiles()
    bucket = _rel_bucket(jnp.asarray(rel))
    nk = N_KINDS
    return pl.pallas_call(
        _bias_kernel,
        grid=(REL_HEADS,),
        in_specs=[
            pl.BlockSpec(memory_space=pltpu.SMEM),
            pl.BlockSpec((nk, LANES, LANES), lambda h: (0, 0, 0)),
            pl.BlockSpec((nk, LANES, LANES), lambda h: (0, 0, 0)),
        ],
        out_specs=pl.BlockSpec((nk, 1, LANES, LANES), lambda h: (0, h, 0, 0)),
        out_shape=jax.ShapeDtypeStruct((nk, REL_HEADS, LANES, LANES), f32),
        compiler_params=_cparams(("arbitrary",)),
        name="bias_tiles",
    )(rel_bias, bucket, jnp.asarray(vis))


def _bias_spec():
    return pl.BlockSpec((N_KINDS, REL_HEADS, LANES, LANES), lambda s: (0, 0, 0, 0))


def _meta_views(rows, cols_t, bsz):
    f = rows.shape[1]
    return (rows.reshape(bsz, N_META, f),
            cols_t.reshape(cols_t.shape[0], bsz, N_META).transpose(1, 0, 2))


A_QD = A_HEADS * 2 * A_HD
A_VR = A_VD + ONES_ROWS


def _proj_a_kernel(h_ref, g_ref, w_ref, gq_ref, gk_ref, qs_ref, k_ref, vt_ref):
    xn = (_rms_rows(h_ref[...]) * g_ref[...]).astype(bf16)
    y = _dot(xn, w_ref[...])
    q = _group_rms(y[:, :A_QD], A_HD) * gq_ref[...]
    lo = _lo_half_mask(q.shape, 2 * A_HD, A_HD)
    q_lo = jnp.where(lo, q, 0.0).astype(bf16)
    q_hi = jnp.where(lo, 0.0, q).astype(bf16)
    k_ref[...] = (_group_rms(y[:, A_QD:2 * A_QD], A_HD) * gk_ref[...]).astype(bf16)
    ones = jnp.ones((ONES_ROWS, LANES), bf16)
    for t in range(vt_ref.shape[0]):
        rows = slice(t * LANES, (t + 1) * LANES)
        vt = y[rows, 2 * A_QD:].T.astype(bf16)
        for h in range(A_HEADS):
            qs_ref[t, h, :LANES, :] = q_lo[rows, h * A_VD:(h + 1) * A_VD]
            qs_ref[t, h, LANES:, :] = q_hi[rows, h * A_VD:(h + 1) * A_VD]
            vt_ref[t, h * A_VR:h * A_VR + A_VD, :] = vt[h * A_VD:(h + 1) * A_VD]
            vt_ref[t, h * A_VR + A_VD:(h + 1) * A_VR, :] = ones


def _proj_a(h, g, w, qk_norm, *, tm):
    n, d = h.shape
    nw = w.shape[1]
    gq = jnp.tile(qk_norm[0] * (A_HD ** -0.5 * LOG2E), A_QD // A_HD).reshape(1, A_QD)
    gk = jnp.tile(qk_norm[1], A_QD // A_HD).reshape(1, A_QD)
    return pl.pallas_call(
        _proj_a_kernel,
        grid=(n // tm,),
        in_specs=[
            pl.BlockSpec((tm, d), lambda i: (i, 0)),
            pl.BlockSpec((1, d), lambda i: (0, 0)),
            pl.BlockSpec((d, nw), lambda i: (0, 0)),
            pl.BlockSpec((1, A_QD), lambda i: (0, 0)),
            pl.BlockSpec((1, A_QD), lambda i: (0, 0)),
        ],
        out_specs=[
            pl.BlockSpec((tm // LANES, A_HEADS, 2 * LANES, A_VD), lambda i: (i, 0, 0, 0)),
            pl.BlockSpec((tm, A_QD), lambda i: (i, 0)),
            pl.BlockSpec((tm // LANES, A_HEADS * A_VR, LANES), lambda i: (i, 0, 0)),
        ],
        out_shape=[
            jax.ShapeDtypeStruct((n // LANES, A_HEADS, 2 * LANES, A_VD), bf16),
            jax.ShapeDtypeStruct((n, A_QD), bf16),
            jax.ShapeDtypeStruct((n // LANES, A_HEADS * A_VR, LANES), bf16),
        ],
        compiler_params=_cparams(("arbitrary",)),
        name="proj_a",
    )(h, g.reshape(1, d), w, gq, gk)


def _attn_a_kernel(qs_ref, k_ref, vt_ref, km_ref, vtm_ref, kmf_ref, vtmf_ref, bias_ref, lam_ref, sub_ref,
                   o_ref, m_ref, acc_ref, *, nblk, lambda_init):
    s_id = pl.program_id(0)
    nb_real = pl.num_programs(0) - 1
    hw = 2 * A_HD

    def init():
        m_ref[...] = jnp.full(m_ref.shape, NEG_INF, f32)
        acc_ref[...] = jnp.zeros(acc_ref.shape, f32)

    def step_fns(kt_fn, vtt_fns, bias_fn):
        def logits():
            sts = [_dot_nt(kt_fn(h), qs_ref[0, h]) for h in range(A_HEADS)]
            if bias_fn is not None:
                sts = [st + bias_fn(h) for h, st in enumerate(sts)]
            return jnp.concatenate(sts, axis=1)

        def pv(pb):
            outs = []
            for h in range(A_HEADS):
                acc = None
                for vtt_fn, r0 in vtt_fns:
                    vtt = vtt_fn(h)
                    part = _dot(vtt, pb[r0:r0 + vtt.shape[1], h * 2 * LANES:(h + 1) * 2 * LANES])
                    acc = part if acc is None else acc + part
                outs.append(acc)
            return jnp.concatenate(outs, axis=1)

        return logits, pv

    def bias_rows(kind, h, rows):
        return jnp.concatenate([bias_ref[kind, h, 0:rows, :], bias_ref[kind, A_HEADS + h, 0:rows, :]], axis=1)

    def finalize():
        lam = lam_ref[...]
        lam_full = (jnp.exp(jnp.sum(lam[0:1] * lam[1:2], axis=-1, keepdims=True))
                    - jnp.exp(jnp.sum(lam[2:3] * lam[3:4], axis=-1, keepdims=True)) + lambda_init)
        o = acc_ref[0:A_VD, :] * (1.0 / acc_ref[A_VD:A_VD + 1, :])
        for h in range(A_HEADS):
            d = o[:, 2 * h * LANES:(2 * h + 1) * LANES] - lam_full * o[:, (2 * h + 1) * LANES:(2 * h + 2) * LANES]
            d = d * lax.rsqrt(jnp.mean(d * d, axis=0, keepdims=True) + EPS) * sub_ref[...] * (1.0 - lambda_init)
            o_ref[:, h * hw:(h + 1) * hw] = d.T.astype(bf16)

    @pl.when(s_id < nb_real)
    def _():
        i = s_id % nblk
        kind_m = jnp.where(i == 0, KIND_META0, KIND_FAR)
        nfar, near0, nnear = _sweep_steps(i)

        def k_fn(b0, nsub):
            off = pl.multiple_of(b0 * LANES, NSUB * LANES)
            return lambda h: k_ref[pl.ds(off, nsub * LANES), h * hw:(h + 1) * hw]

        def vt_fn(b0, nsub):
            return lambda h: jnp.concatenate([vt_ref[b0 + t, h * A_VR:(h + 1) * A_VR, :] for t in range(nsub)], axis=1)

        def near_bias(b0):
            kinds = [_block_kind(b0 + t - i) for t in range(NSUB)]
            return [lambda h, kind=kind: bias_rows(kind, h, LANES) for kind in kinds]

        def first_fns():
            biases = [lambda h: bias_rows(kind_m, h, N_META)] + near_bias(0)
            return step_fns(
                lambda h: jnp.concatenate([km_ref[0, :, h * hw:(h + 1) * hw], k_fn(0, NSUB)(h)], axis=0),
                [(lambda h: vtm_ref[0, h * A_VR:(h + 1) * A_VR, :], 0), (vt_fn(0, NSUB), N_META)],
                lambda h: jnp.concatenate([b(h) for b in biases], axis=0))

        def far_step(w):
            b0 = NSUB + w * NSUB_FAR
            far_bias = jnp.concatenate([bias_rows(KIND_FAR, h, 1) for h in range(A_HEADS)], axis=1)
            return step_fns(k_fn(b0, NSUB_FAR), [(vt_fn(b0, NSUB_FAR), 0)], None) + (far_bias,)

        def near_step(u):
            b0 = near0 + u * NSUB
            biases = near_bias(b0)
            return step_fns(k_fn(b0, NSUB), [(vt_fn(b0, NSUB), 0)],
                            lambda h: jnp.concatenate([b(h) for b in biases], axis=0)) + (None,)

        def sweep(lazy):
            init()
            _softmax_step(*first_fns(), m_ref, acc_ref, lazy)
            _softmax_loop(0, nfar, far_step, m_ref, acc_ref, lazy)
            _softmax_loop(0, nnear, near_step, m_ref, acc_ref, lazy)

        sweep(True)
        pl.when(_lazy_failed(acc_ref[A_VD:A_VD + 1, :]))(lambda: sweep(False))
        finalize()

    @pl.when(s_id == nb_real)
    def _():
        init()
        _softmax_step(*step_fns(lambda h: kmf_ref[:, h * hw:(h + 1) * hw],
                                [(lambda h: vtmf_ref[0, h * A_VR:(h + 1) * A_VR, :], 0)],
                                lambda h: bias_rows(KIND_METAMETA, h, LANES)), m_ref, acc_ref)
        finalize()


def _attn_a(qs, k, vt, bias, lam, subln, *, bsz, nblk, lambda_init):
    n = k.shape[0]
    nb_real = bsz * nblk
    seq = nblk * LANES
    d = A_QD
    vr = A_HEADS * A_VR
    assert nblk % NSUB == 0
    km, vtm = _meta_views(k[nb_real * LANES:], vt[nb_real], bsz)
    kern = functools.partial(_attn_a_kernel, nblk=nblk, lambda_init=lambda_init)
    bclamp = lambda s: jnp.minimum(s // nblk, bsz - 1)
    return pl.pallas_call(
        kern,
        grid=(nb_real + 1,),
        in_specs=[
            pl.BlockSpec((1,) + qs.shape[1:], lambda s: (s, 0, 0, 0)),
            pl.BlockSpec((seq, d), lambda s: (bclamp(s), 0)),
            pl.BlockSpec((nblk, vr, LANES), lambda s: (bclamp(s), 0, 0)),
            pl.BlockSpec((1, N_META, d), lambda s: (bclamp(s), 0, 0)),
            pl.BlockSpec((1, vr, N_META), lambda s: (bclamp(s), 0, 0)),
            pl.BlockSpec((LANES, d), lambda s: (nb_real, 0)),
            pl.BlockSpec((1, vr, LANES), lambda s: (nb_real, 0, 0)),
            _bias_spec(),
            pl.BlockSpec((4, A_HD), lambda s: (0, 0)),
            pl.BlockSpec((A_VD, LANES), lambda s: (0, 0)),
        ],
        out_specs=pl.BlockSpec((LANES, d), lambda s: (s, 0)),
        out_shape=jax.ShapeDtypeStruct((n, d), bf16),
        scratch_shapes=[
            pltpu.VMEM((1, A_HEADS * 2 * LANES), f32),
            pltpu.VMEM((A_VR, A_HEADS * 2 * LANES), f32),
        ],
        compiler_params=_cparams(("arbitrary",)),
        name="attn_a",
    )(qs, k, vt, km, vtm, k, vt, bias, lam, jnp.broadcast_to(subln[:, None], (A_VD, LANES)))


C_QD = C_Q_HEADS * C_HD
C_KD = 2 * C_KV_HEADS * C_HD
C_VR = 2 * C_HD + ONES_ROWS


def _proj_c_kernel(h_ref, g_ref, w_ref, gq_ref, gk_ref, qs_ref, k_ref, vt_ref):
    xn = (_rms_rows(h_ref[...]) * g_ref[...]).astype(bf16)
    y = _dot(xn, w_ref[...])
    q = _group_rms(y[:, :C_QD], C_HD) * gq_ref[...]
    lo = _lo_half_mask(q.shape, 2 * C_HD, C_HD)
    q_even = jnp.where(lo, q, 0.0).astype(bf16)
    q_odd = jnp.where(lo, 0.0, q).astype(bf16)
    k_ref[...] = (_group_rms(y[:, C_QD:C_QD + C_KD], C_HD) * gk_ref[...]).astype(bf16)
    ones = jnp.ones((ONES_ROWS, LANES), bf16)
    for t in range(vt_ref.shape[0]):
        rows = slice(t * LANES, (t + 1) * LANES)
        vt = y[rows, C_QD + C_KD:].T.astype(bf16)
        for g in range(C_KV_HEADS):
            for hh in range(C_GROUP):
                pair = (g * C_GROUP + hh) // 2
                src = q_even if hh % 2 == 0 else q_odd
                qs_ref[t, g, hh * LANES:(hh + 1) * LANES, :] = src[rows, pair * LANES:(pair + 1) * LANES]
            vt_ref[t, g * C_VR:g * C_VR + 2 * C_HD, :] = vt[g * 2 * C_HD:(g + 1) * 2 * C_HD]
            vt_ref[t, g * C_VR + 2 * C_HD:(g + 1) * C_VR, :] = ones


def _proj_c(h, g, w, qk_norm, *, tm):
    n, d = h.shape
    nw = w.shape[1]
    nt = tm // LANES
    gq = jnp.tile(qk_norm[0] * (C_HD ** -0.5 * LOG2E), C_QD // C_HD).reshape(1, C_QD)
    gk = jnp.tile(qk_norm[1], C_KD // C_HD).reshape(1, C_KD)
    return pl.pallas_call(
        _proj_c_kernel,
        grid=(n // tm,),
        in_specs=[
            pl.BlockSpec((tm, d), lambda i: (i, 0)),
            pl.BlockSpec((1, d), lambda i: (0, 0)),
            pl.BlockSpec((d, nw), lambda i: (0, 0)),
            pl.BlockSpec((1, C_QD), lambda i: (0, 0)),
            pl.BlockSpec((1, C_KD), lambda i: (0, 0)),
        ],
        out_specs=[
            pl.BlockSpec((nt, C_KV_HEADS, C_GROUP * LANES, LANES), lambda i: (i, 0, 0, 0)),
            pl.BlockSpec((tm, C_KD), lambda i: (i, 0)),
            pl.BlockSpec((nt, C_KV_HEADS * C_VR, LANES), lambda i: (i, 0, 0)),
        ],
        out_shape=[
            jax.ShapeDtypeStruct((n // LANES, C_KV_HEADS, C_GROUP * LANES, LANES), bf16),
            jax.ShapeDtypeStruct((n, C_KD), bf16),
            jax.ShapeDtypeStruct((n // LANES, C_KV_HEADS * C_VR, LANES), bf16),
        ],
        compiler_params=_cparams(("arbitrary",)),
        name="proj_c",
    )(h, g.reshape(1, d), w, gq, gk)


def _attn_c_kernel(sink_ref, qs_ref, k_ref, vt_ref, km_ref, vtm_ref, kmf_ref, vtmf_ref, bias_ref, o_ref, *, nblk):
    s_id = pl.program_id(0)
    nb_real = pl.num_programs(0) - 1
    r_io, _ = _tile_iotas()
    vd = 2 * C_HD

    def attend_with(tiles, lazy):
        top = r_io < C_HD
        failed = None
        for g in range(C_KV_HEADS):
            sink = jnp.concatenate(
                [jnp.full((1, LANES), sink_ref[g * C_GROUP + hh] * LOG2E, f32) for hh in range(C_GROUP)], axis=1)
            sts = []
            m = sink
            for (k_fn, vt_fn, bias_fn) in tiles:
                st = _dot_nt(k_fn(g), qs_ref[0, g])
                st = st + jnp.concatenate([bias_fn(g * C_GROUP + hh) for hh in range(C_GROUP)], axis=1)
                if not lazy:
                    m = jnp.maximum(m, jnp.max(st, axis=0, keepdims=True))
                sts.append(st)
            acc = None
            for st, (k_fn, vt_fn, bias_fn) in zip(sts, tiles):
                pv = _dot(vt_fn(g), jnp.exp2(st if lazy else st - m).astype(bf16))
                acc = pv if acc is None else acc + pv
            denom = acc[vd:vd + 1, :] + jnp.exp2(sink if lazy else sink - m)
            if lazy:
                failed = _lazy_failed(denom) if failed is None else failed | _lazy_failed(denom)
            o = acc[0:vd, :] * (1.0 / denom)
            for cc in range(C_GROUP // 2):
                even = o[:, (2 * cc) * LANES:(2 * cc + 1) * LANES]
                odd = o[:, (2 * cc + 1) * LANES:(2 * cc + 2) * LANES]
                col = (g * (C_GROUP // 2) + cc) * LANES
                o_ref[:, col:col + LANES] = jnp.where(top, even, odd).T.astype(bf16)
        return failed

    def attend(tiles):
        failed = attend_with(tiles, True)

        @pl.when(failed)
        def _():
            attend_with(tiles, False)

    @pl.when(s_id < nb_real)
    def _():
        i = s_id % nblk
        prev = jnp.maximum(i - 1, 0)
        poff = pl.multiple_of(prev * LANES, LANES)
        coff = pl.multiple_of(i * LANES, LANES)
        kind_m = jnp.where(i == 0, KIND_META0, KIND_FAR)
        kind_p = jnp.where(i == 0, KIND_MASKED, KIND_PREVWIN)
        attend([
            (lambda g: km_ref[0, :, g * LANES:(g + 1) * LANES], lambda g: vtm_ref[0, g * C_VR:(g + 1) * C_VR, :],
             lambda h: bias_ref[kind_m, h, 0:N_META, :]),
            (lambda g: k_ref[pl.ds(poff, LANES), g * LANES:(g + 1) * LANES],
             lambda g: vt_ref[prev, g * C_VR:(g + 1) * C_VR, :], lambda h: bias_ref[kind_p, h]),
            (lambda g: k_ref[pl.ds(coff, LANES), g * LANES:(g + 1) * LANES],
             lambda g: vt_ref[i, g * C_VR:(g + 1) * C_VR, :], lambda h: bias_ref[KIND_DIAG, h]),
        ])

    @pl.when(s_id == nb_real)
    def _():
        attend([(lambda g: kmf_ref[:, g * LANES:(g + 1) * LANES], lambda g: vtmf_ref[0, g * C_VR:(g + 1) * C_VR, :],
                 lambda h: bias_ref[KIND_METAMETA, h])])


def _attn_c(qs, k, vt, bias, sinks, *, bsz, nblk):
    n = k.shape[0]
    nb_real = bsz * nblk
    seq = nblk * LANES
    vr = C_KV_HEADS * C_VR
    kern = functools.partial(_attn_c_kernel, nblk=nblk)
    bclamp = lambda s: jnp.minimum(s // nblk, bsz - 1)
    km, vtm = _meta_views(k[nb_real * LANES:], vt[nb_real], bsz)
    return pl.pallas_call(
        kern,
        grid=(nb_real + 1,),
        in_specs=[
            pl.BlockSpec(memory_space=pltpu.SMEM),
            pl.BlockSpec((1,) + qs.shape[1:], lambda s: (s, 0, 0, 0)),
            pl.BlockSpec((seq, C_KD), lambda s: (bclamp(s), 0)),
            pl.BlockSpec((nblk, vr, LANES), lambda s: (bclamp(s), 0, 0)),
            pl.BlockSpec((1, N_META, C_KD), lambda s: (bclamp(s), 0, 0)),
            pl.BlockSpec((1, vr, N_META), lambda s: (bclamp(s), 0, 0)),
            pl.BlockSpec((LANES, C_KD), lambda s: (nb_real, 0)),
            pl.BlockSpec((1, vr, LANES), lambda s: (nb_real, 0, 0)),
            _bias_spec(),
        ],
        out_specs=pl.BlockSpec((LANES, C_QD), lambda s: (s, 0)),
        out_shape=jax.ShapeDtypeStruct((n, C_QD), bf16),
        compiler_params=_cparams(("arbitrary",)),
        name="attn_c",
    )(sinks, qs, k, vt, km, vtm, k, vt, bias)


B_QA = B_HEADS * B_KV_RANK
B_QI = IDX_HEADS * IDX_DIM
B_W1 = 2 * B_Q_RANK + 2 * LANES
B_TR = B_KV_RANK + ONES_ROWS


def _proj_b_kernel(h_ref, g_ref, w1_ref, ln_ref, wuq_ref, qn_ref,
                   qa_ref, qi_ref, ckv_ref, ckvt_ref, kk_ref, wit_ref):
    xn = (_rms_rows(h_ref[...]) * g_ref[...]).astype(bf16)
    y = _dot(xn, w1_ref[...])
    r = B_Q_RANK
    cq = (_rms_rows(y[:, :r]) * ln_ref[0:1, :]).astype(bf16)
    ckv = _rms_rows(y[:, r:2 * r]) * ln_ref[1:2, :]
    ckv_ref[...] = ckv.astype(bf16)
    kk_ref[...] = _rms_rows(y[:, 2 * r:2 * r + LANES]).astype(bf16)
    wi = y[:, 2 * r + LANES:] * (IDX_HEADS ** -0.5)
    ones = jnp.ones((ONES_ROWS, LANES), bf16)
    for t in range(ckvt_ref.shape[0]):
        ckvt_ref[t, 0:r, :] = ckv[t * LANES:(t + 1) * LANES, :].T.astype(bf16)
        ckvt_ref[t, r:, :] = ones
        wit_ref[t] = wi[t * LANES:(t + 1) * LANES, :].T[0:IDX_HEADS, :]
    z = _dot(cq, wuq_ref[...])
    qa = (_group_rms(z[:, :B_QA], B_KV_RANK) * qn_ref[...]).astype(bf16)
    qi = z[:, B_QA:] * (IDX_DIM ** -0.5)
    lo = _lo_half_mask(qi.shape, 2 * IDX_DIM, IDX_DIM)
    qi_lo = jnp.where(lo, qi, 0.0).astype(bf16)
    qi_hi = jnp.where(lo, 0.0, qi).astype(bf16)
    for t in range(qa_ref.shape[0]):
        rows = slice(t * LANES, (t + 1) * LANES)
        for hd in range(B_HEADS):
            qa_ref[t, hd * LANES:(hd + 1) * LANES, :] = qa[rows, hd * r:(hd + 1) * r]
        for hh in range(IDX_HEADS):
            src = qi_lo if hh % 2 == 0 else qi_hi
            qi_ref[t, hh * LANES:(hh + 1) * LANES, :] = src[rows, (hh // 2) * LANES:(hh // 2 + 1) * LANES]


def _proj_b(h, g, w1, latent_norm, wuq, q_norm, *, tm):
    n, d = h.shape
    qn = jnp.tile(q_norm * (B_KV_RANK ** -0.5 * LOG2E), B_HEADS).reshape(1, B_QA)
    row = lambda i: (i, 0)
    row3 = lambda i: (i, 0, 0)
    const = lambda i: (0, 0)
    nt = tm // LANES
    return pl.pallas_call(
        _proj_b_kernel,
        grid=(n // tm,),
        in_specs=[
            pl.BlockSpec((tm, d), row),
            pl.BlockSpec((1, d), const),
            pl.BlockSpec(w1.shape, const),
            pl.BlockSpec(latent_norm.shape, const),
            pl.BlockSpec(wuq.shape, const),
            pl.BlockSpec((1, B_QA), const),
        ],
        out_specs=[
            pl.BlockSpec((nt, B_HEADS * LANES, B_KV_RANK), row3),
            pl.BlockSpec((nt, IDX_HEADS * LANES, LANES), row3),
            pl.BlockSpec((tm, B_KV_RANK), row),
            pl.BlockSpec((nt, B_TR, LANES), row3),
            pl.BlockSpec((tm, LANES), row),
            pl.BlockSpec((nt, IDX_HEADS, LANES), row3),
        ],
        out_shape=[
            jax.ShapeDtypeStruct((n // LANES, B_HEADS * LANES, B_KV_RANK), bf16),
            jax.ShapeDtypeStruct((n // LANES, IDX_HEADS * LANES, LANES), bf16),
            jax.ShapeDtypeStruct((n, B_KV_RANK), bf16),
            jax.ShapeDtypeStruct((n // LANES, B_TR, LANES), bf16),
            jax.ShapeDtypeStruct((n, LANES), bf16),
            jax.ShapeDtypeStruct((n // LANES, IDX_HEADS, LANES), f32),
        ],
        compiler_params=_cparams(("arbitrary",)),
        name="proj_b",
    )(h, g.reshape(1, d), w1, latent_norm, wuq, qn)


def _attn_b_kernel(qs_ref, is_ref, wit_ref, ckv_ref, ckvt_ref, kk_ref, ckvm_ref, ckvtm_ref, kkm_ref,
                   ckvmf_ref, ckvtmf_ref, bias_ref, wuvt_ref,
                   o_ref, key_ref, pen_ref, m_ref, acc_ref, *, nblk, k_sel):
    s_id = pl.program_id(0)
    nb_real = pl.num_programs(0) - 1
    r_io, c_io = _tile_iotas()
    rk = B_KV_RANK

    def init():
        m_ref[...] = jnp.full(m_ref.shape, NEG_INF, f32)
        acc_ref[...] = jnp.zeros(acc_ref.shape, f32)

    def add_per_head(st, bias_fn, pen):
        cols = []
        for h in range(B_HEADS):
            add = pen if bias_fn is None else (bias_fn(h) if pen is None else bias_fn(h) + pen)
            cols.append(st[:, h * LANES:(h + 1) * LANES] + add)
        return jnp.concatenate(cols, axis=1)

    def step_fns(ckv_fn, ckvt_fns, bias_fn, pen_fn):
        def logits():
            st = _dot_nt(ckv_fn(), qs_ref[0])
            return add_per_head(st, bias_fn, None if pen_fn is None else pen_fn())

        def pv(pb):
            acc = None
            for ckvt_fn, r0 in ckvt_fns:
                ckvt = ckvt_fn()
                part = _dot(ckvt, pb[r0:r0 + ckvt.shape[1], :])
                acc = part if acc is None else acc + part
            return acc

        return logits, pv

    def finalize():
        olat = (acc_ref[0:rk, :] * (1.0 / acc_ref[rk:rk + 1, :])).astype(bf16)
        ot = jnp.concatenate([_dot(wuvt_ref[h], olat[:, h * LANES:(h + 1) * LANES]) for h in range(B_HEADS)], axis=0)
        o_ref[...] = ot.T.astype(bf16)

    def index_scores(kk):
        s = jnp.maximum(_dot_nt(kk, is_ref[0]), 0.0)
        wt = wit_ref[0]
        sc = jnp.zeros((kk.shape[0], LANES), f32)
        for hh in range(IDX_HEADS):
            sc = sc + wt[hh:hh + 1, :] * s[:, hh * LANES:(hh + 1) * LANES]
        return sc

    def sort_key(sc):
        bits = lax.bitcast_convert_type(sc + 0.0, jnp.int32)
        return jnp.where(bits < 0, bits ^ jnp.int32(0x7FFFFFFF), bits)

    @pl.when(s_id < nb_real)
    def _():
        i = s_id % nblk
        ntile = i + 2

        int_min_tile = jnp.full((LANES, LANES), INT_MIN, jnp.int32)
        key_ref[0] = int_min_tile
        key_ref[0, 0:N_META, :] = sort_key(index_scores(kkm_ref[0]))
        for t in range(COUNT_TILES - 1):
            key_ref[i + 2 + t] = int_min_tile

        def score_body(jp, carry):
            off = pl.multiple_of(jp * (2 * LANES), 2 * LANES)
            keys = sort_key(index_scores(kk_ref[pl.ds(off, 2 * LANES), :]))
            for t in range(2):
                j = _vec(2 * jp + t)
                vis = (j < i) | ((j == i) & ((r_io >> 6) <= (c_io >> 6)))
                key_ref[2 * jp + t + 1] = jnp.where(vis, keys[t * LANES:(t + 1) * LANES], jnp.int32(INT_MIN))
            return carry

        lax.fori_loop(0, i // 2 + 1, score_body, 0)

        def count(pred):
            def cbody(tq, accv):
                for t in range(COUNT_TILES):
                    accv = accv + jnp.where(pred(key_ref[COUNT_TILES * tq + t], COUNT_TILES * tq + t), 1.0, 0.0)
                return accv
            accv = lax.fori_loop(0, (ntile + COUNT_TILES - 1) // COUNT_TILES, cbody,
                                 jnp.zeros((LANES, LANES), f32))
            return jnp.sum(accv, axis=0, keepdims=True)

        kf = float(k_sel)
        zero = jnp.zeros((1, LANES), jnp.int32)
        t0 = jnp.where(count(lambda k, t: k >= zero) >= kf, zero, jnp.int32(INT_MIN))

        def bit_body(it, tcur):
            cand = tcur | jnp.left_shift(jnp.int32(1), 30 - it)
            return jnp.where(count(lambda k, t: k >= cand) >= kf, cand, tcur)

        thr = lax.fori_loop(0, 31, bit_body, t0)

        need = kf - count(lambda k, t: k > thr)
        n_eq = count(lambda k, t: k == thr)
        has_thr = thr > jnp.int32(INT_MIN)
        tied = jnp.max(jnp.where(has_thr & (n_eq > need), 1.0, 0.0)) > 0.0

        def tie_search(_):
            def jbody(it, jcur):
                cand = jcur | jnp.left_shift(jnp.int32(1), 11 - it)
                cnt = count(lambda k, t: (k == thr) & ((t * LANES + r_io) < cand))
                return jnp.where(cnt < need, cand, jcur)
            return lax.fori_loop(0, 12, jbody, jnp.zeros((1, LANES), jnp.int32))

        j_last = lax.cond(tied, tie_search, lambda _: jnp.full((1, LANES), 4095, jnp.int32), 0)
        j_last = jnp.where(has_thr, j_last, -1)

        def pen_body(t, carry):
            k = key_ref[t]
            sel = (k > thr) | ((k == thr) & ((t * LANES + r_io) <= j_last))
            pen_ref[t] = jnp.where(sel, 0.0, NEG_INF)
            return carry

        lax.fori_loop(0, ntile, pen_body, 0)
        for t in range(1, NSUB):
            pen_ref[i + 1 + t] = jnp.full((LANES, LANES), NEG_INF, f32)

        kind_m = jnp.where(i == 0, KIND_META0, KIND_FAR)
        nfar, near0, nnear = _sweep_steps(i)

        def ckv_fn(b0, nsub):
            off = pl.multiple_of(b0 * LANES, NSUB * LANES)
            return lambda: ckv_ref[pl.ds(off, nsub * LANES), :]

        def ckvt_fn(b0, nsub):
            return lambda: jnp.concatenate([ckvt_ref[b0 + t] for t in range(nsub)], axis=1)

        def pen_fn(b0, nsub):
            return lambda: jnp.concatenate([pen_ref[b0 + t + 1] for t in range(nsub)], axis=0)

        def near_bias(b0):
            kinds = [_block_kind(b0 + t - i) for t in range(NSUB)]
            return lambda h: jnp.concatenate([bias_ref[kinds[t], h] for t in range(NSUB)], axis=0)

        def first_pen():
            return jnp.concatenate([pen_ref[0, 0:N_META, :], pen_fn(0, NSUB)()], axis=0)

        def first_fns():
            bias01 = near_bias(0)
            return step_fns(lambda: jnp.concatenate([ckvm_ref[0], ckv_fn(0, NSUB)()], axis=0),
                            [(lambda: ckvtm_ref[0], 0), (ckvt_fn(0, NSUB), N_META)],
                            lambda h: jnp.concatenate([bias_ref[kind_m, h, 0:N_META, :], bias01(h)], axis=0),
                            first_pen)

        def far_step(w):
            b0 = NSUB + w * NSUB_FAR
            far_bias = jnp.concatenate([bias_ref[KIND_FAR, h, 0:1, :] for h in range(B_HEADS)], axis=1)
            return step_fns(ckv_fn(b0, NSUB_FAR), [(ckvt_fn(b0, NSUB_FAR), 0)], None, pen_fn(b0, NSUB_FAR)) + (far_bias,)

        def near_step(u):
            b0 = near0 + u * NSUB
            return step_fns(ckv_fn(b0, NSUB), [(ckvt_fn(b0, NSUB), 0)], near_bias(b0), pen_fn(b0, NSUB)) + (None,)

        def sweep(lazy):
            init()
            _softmax_step(*first_fns(), m_ref, acc_ref, lazy)
            _softmax_loop(0, nfar, far_step, m_ref, acc_ref, lazy)
            _softmax_loop(0, nnear, near_step, m_ref, acc_ref, lazy)

        sweep(True)
        pl.when(_lazy_failed(acc_ref[rk:rk + 1, :]))(lambda: sweep(False))
        finalize()

    @pl.when(s_id == nb_real)
    def _():
        init()
        _softmax_step(*step_fns(lambda: ckvmf_ref[...], [(lambda: ckvtmf_ref[0], 0)],
                                lambda h: bias_ref[KIND_METAMETA, h], None), m_ref, acc_ref)
        finalize()


def _attn_b(qa, qi, wit, ckv, ckvt, kk, bias, wuvt, *, bsz, nblk, k_sel):
    n = ckv.shape[0]
    nb_real = bsz * nblk
    seq = nblk * LANES
    assert k_sel >= N_META and (nblk + 1) * LANES <= 4096 and nblk % NSUB == 0
    ckvm, ckvtm = _meta_views(ckv[nb_real * LANES:], ckvt[nb_real], bsz)
    kkm = kk[nb_real * LANES:].reshape(bsz, N_META, LANES)
    kern = functools.partial(_attn_b_kernel, nblk=nblk, k_sel=k_sel)
    bidx = lambda s: jnp.minimum(s // nblk, bsz - 1)
    blk = lambda s: (s, 0)
    return pl.pallas_call(
        kern,
        grid=(nb_real + 1,),
        in_specs=[
            pl.BlockSpec((1,) + qa.shape[1:], lambda s: (s, 0, 0)),
            pl.BlockSpec((1,) + qi.shape[1:], lambda s: (s, 0, 0)),
            pl.BlockSpec((1, IDX_HEADS, LANES), lambda s: (s, 0, 0)),
            pl.BlockSpec((seq, B_KV_RANK), lambda s: (bidx(s), 0)),
            pl.BlockSpec((nblk, B_TR, LANES), lambda s: (bidx(s), 0, 0)),
            pl.BlockSpec((seq, LANES), lambda s: (bidx(s), 0)),
            pl.BlockSpec((1, N_META, B_KV_RANK), lambda s: (bidx(s), 0, 0)),
            pl.BlockSpec((1, B_TR, N_META), lambda s: (bidx(s), 0, 0)),
            pl.BlockSpec((1, N_META, LANES), lambda s: (bidx(s), 0, 0)),
            pl.BlockSpec((LANES, B_KV_RANK), lambda s: (nb_real, 0)),
            pl.BlockSpec((1, B_TR, LANES), lambda s: (nb_real, 0, 0)),
            _bias_spec(),
            pl.BlockSpec(wuvt.shape, lambda s: (0, 0, 0)),
        ],
        out_specs=pl.BlockSpec((LANES, B_HEADS * B_VD), blk),
        out_shape=jax.ShapeDtypeStruct((n, B_HEADS * B_VD), bf16),
        scratch_shapes=[
            pltpu.VMEM((nblk + COUNT_TILES, LANES, LANES), jnp.int32),
            pltpu.VMEM((nblk + NSUB, LANES, LANES), f32),
            pltpu.VMEM((1, B_HEADS * LANES), f32),
            pltpu.VMEM((B_TR, B_HEADS * LANES), f32),
        ],
        compiler_params=_cparams(("arbitrary",)),
        name="attn_b",
    )(qa, qi, wit, ckv, ckvt, kk, ckvm, ckvtm, kkm, ckv, ckvt, bias, wuvt)


def kernel(x, meta_tokens, rel_bias, ln_ffn1, ffn1_wi, ffn1_wo, ln_mix, w_out, ln_ffn2, ffn2_wi, ffn2_wo, a_w_in, a_qk_norm, a_lambda, a_subln, b_w_in, b_latent_norm, b_w_uq, b_q_norm, b_w_uv, c_w_in, c_qk_norm, c_sinks):
    bsz, seq, d = x.shape
    assert d == D_MODEL and seq % LANES == 0 and bsz * N_META == LANES
    nblk = seq // LANES
    n = bsz * seq + LANES
    k_sel = min(TOPK_MAX, seq // 4)
    tm_ffn = _row_tile(n, 1408)
    tm_last = _row_tile(bsz * seq, 1408)
    tm_proj = _row_tile(n, 384, LANES)
    fc = 256

    h = x.reshape(bsz * seq, d)
    meta_rows = jnp.broadcast_to(meta_tokens.astype(x.dtype), (bsz, N_META, d)).reshape(LANES, d)
    bias = _bias_tiles(rel_bias)

    for layer in range(DEPTH):
        h = _ffn(h, ln_ffn1[layer], ffn1_wi, ffn1_wo, layer, tm=tm_ffn, fc=fc,
                 tail=meta_rows if layer == 0 else None)
        kind, j = layer % N_MIXERS, layer // N_MIXERS
        g = ln_mix[layer]
        if kind == 0:
            lambda_init = 0.8 - 0.6 * math.exp(-0.3 * layer)
            qs, k, vt = _proj_a(h, g, a_w_in[j].astype(bf16), a_qk_norm[j], tm=tm_proj)
            mix = _attn_a(qs, k, vt, bias, a_lambda[j], a_subln[j], bsz=bsz, nblk=nblk, lambda_init=lambda_init)
        elif kind == 1:
            w = b_w_in[j]
            r2 = B_Q_RANK + B_KV_RANK
            kcol = w[:, r2:r2 + IDX_DIM]
            w1 = jnp.concatenate([w[:, :r2], kcol, kcol, w[:, r2 + IDX_DIM:],
                                  jnp.zeros((d, LANES - IDX_HEADS), w.dtype)], axis=1).astype(bf16)
            assert w1.shape[1] == B_W1
            qa, qi, ckv, ckvt, kk, wit = _proj_b(h, g, w1, b_latent_norm[j], b_w_uq[j].astype(bf16), b_q_norm[j],
                                                 tm=tm_proj)
            wuvt = jnp.swapaxes(b_w_uv[j], 1, 2).astype(bf16)
            mix = _attn_b(qa, qi, wit, ckv, ckvt, kk, bias, wuvt, bsz=bsz, nblk=nblk, k_sel=k_sel)
        else:
            w = c_w_in[j]
            kcols = [w[:, C_QD + gi * C_HD:C_QD + (gi + 1) * C_HD] for gi in range(C_KV_HEADS)]
            voff = C_QD + C_KV_HEADS * C_HD
            vcols = [w[:, voff + gi * C_HD:voff + (gi + 1) * C_HD] for gi in range(C_KV_HEADS)]
            wc = jnp.concatenate([w[:, :C_QD]] + [kc for kc in kcols for _ in range(2)]
                                 + [vc for vc in vcols for _ in range(2)], axis=1).astype(bf16)
            qs, k, vt = _proj_c(h, g, wc, c_qk_norm[j], tm=tm_proj)
            mix = _attn_c(qs, k, vt, bias, c_sinks[j], bsz=bsz, nblk=nblk)
        last = layer == DEPTH - 1
        h = _ffn(h, ln_ffn2[layer], ffn2_wi, ffn2_wo, layer, tm=tm_last if last else tm_ffn, fc=fc,
                 mix=mix, wout=w_out[layer].astype(bf16), n_rows=bsz * seq if last else None)
    return h.reshape(bsz, seq, d)
```

```python
import functools
import math

import numpy as np
import jax
import jax.numpy as jnp
from jax import lax
from jax.experimental import pallas as pl
from jax.experimental.pallas import tpu as pltpu

D_MODEL = 1024
DEPTH = 4
CHUNK = 64
N_META = 16
N_MIXERS = 3
NEG_INF = -1e30
REL_BUCKETS = 32
REL_MAX_DIST = 128
REL_HEADS = 16
D_FF = 2816
A_HEADS = 8
A_HD = 64
A_VD = 2 * A_HD
B_HEADS = 16
B_Q_RANK = 256
B_KV_RANK = 256
B_VD = 64
IDX_HEADS = 8
IDX_DIM = 64
TOPK_MAX = 256
C_Q_HEADS = 16
C_KV_HEADS = 2
C_GROUP = C_Q_HEADS // C_KV_HEADS
C_HD = 64
EPS = 1e-6

LANES = 128
BF16_ROWS = 16
VMEM_LIMIT = 56 * 1024 * 1024
INT_MIN = -(2 ** 31)
NSUB = 2
NSUB_FAR = 4
LOG2E = math.log2(math.e)
LAZY_CEIL = 2.0 ** 70
LAZY_FLOOR = 2.0 ** -100
ONES_ROWS = BF16_ROWS

KIND_DIAG, KIND_PREV, KIND_FAR, KIND_META0, KIND_METAMETA, KIND_MASKED, KIND_PREVWIN = 0, 1, 2, 3, 4, 5, 6
N_KINDS = 7

f32 = jnp.float32
bf16 = jnp.bfloat16


def _cparams(sem, vmem_mib=None):
    limit = VMEM_LIMIT if vmem_mib is None else vmem_mib * 1024 * 1024
    return pltpu.CompilerParams(dimension_semantics=sem, vmem_limit_bytes=limit)


def _row_tile(n, cap, mult=BF16_ROWS):
    best = None
    for t in range(mult, cap + 1, mult):
        if n % t == 0:
            best = t
    assert best is not None
    return best


def _dot(a, b):
    return jnp.dot(a, b, preferred_element_type=f32)


def _dot_nt(a, b):
    return lax.dot_general(a, b, (((1,), (1,)), ((), ())), preferred_element_type=f32)


def _rms_rows(x):
    return x * lax.rsqrt(jnp.mean(x * x, axis=-1, keepdims=True) + EPS)


def _lo_half_mask(shape, period, half):
    return (lax.broadcasted_iota(jnp.int32, shape, 1) & (period - 1)) < half


def _group_rms(x, group):
    r, c = x.shape
    outs = []
    if group == 64:
        lo = _lo_half_mask((r, LANES), LANES, 64)
        for ci in range(c // LANES):
            xc = x[:, ci * LANES:(ci + 1) * LANES]
            x2 = xc * xc
            s_lo = jnp.sum(jnp.where(lo, x2, 0.0), axis=-1, keepdims=True)
            s_hi = jnp.sum(jnp.where(lo, 0.0, x2), axis=-1, keepdims=True)
            inv = jnp.where(lo, lax.rsqrt(s_lo * (1.0 / 64) + EPS), lax.rsqrt(s_hi * (1.0 / 64) + EPS))
            outs.append(xc * inv)
    else:
        for gi in range(c // group):
            outs.append(_rms_rows(x[:, gi * group:(gi + 1) * group]))
    return outs[0] if len(outs) == 1 else jnp.concatenate(outs, axis=-1)


def _tile_iotas():
    r = lax.broadcasted_iota(jnp.int32, (LANES, LANES), 0)
    c = lax.broadcasted_iota(jnp.int32, (LANES, LANES), 1)
    return r, c


def _vec(s):
    return jnp.full((LANES, LANES), s, jnp.int32)


def _softmax_step(logits_fn, pv_fn, m_ref, acc_ref, lazy=False, offset=None):
    st = logits_fn()
    if offset is not None:
        st = st + offset
    if lazy:
        acc_ref[...] += pv_fn(jnp.exp2(st).astype(bf16))
    else:
        m_old = m_ref[...]
        m_new = jnp.maximum(m_old, jnp.max(st, axis=0, keepdims=True))
        acc_ref[...] = jnp.exp2(m_old - m_new) * acc_ref[...] + pv_fn(jnp.exp2(st - m_new).astype(bf16))
        m_ref[...] = m_new


def _softmax_loop(lo, hi, step_fn, m_ref, acc_ref, lazy):
    def body(w, carry):
        logits_fn, pv_fn, offset = step_fn(w)
        _softmax_step(logits_fn, pv_fn, m_ref, acc_ref, lazy, offset)
        return carry

    lax.fori_loop(lo, hi, body, 0)


def _lazy_failed(denominators):
    return jnp.logical_not((jnp.max(denominators) <= LAZY_CEIL) & (jnp.min(denominators) >= LAZY_FLOOR))


def _block_kind(rel):
    return jnp.where(rel < -1, KIND_FAR,
                     jnp.where(rel == -1, KIND_PREV, jnp.where(rel == 0, KIND_DIAG, KIND_MASKED)))


def _sweep_steps(i):
    nfar = jnp.maximum(i - 1 - NSUB, 0) // NSUB_FAR
    near0 = NSUB + nfar * NSUB_FAR
    return nfar, near0, (i - near0 + NSUB) // NSUB


def _ffn_kernel(*refs, fuse_out, tail_rows, layer, fc, nj):
    if fuse_out:
        h_ref, mix_ref, wout_ref, g_ref, wi_hbm, wo_hbm, o_ref, xn_ref, wa_buf, wb_buf, wo_buf, sem = refs
    elif tail_rows:
        h_ref, tail_ref, g_ref, wi_hbm, wo_hbm, o_ref, xn_ref, wa_buf, wb_buf, wo_buf, sem = refs
    else:
        h_ref, g_ref, wi_hbm, wo_hbm, o_ref, xn_ref, wa_buf, wb_buf, wo_buf, sem = refs

    def chunk_copies(j, slot):
        lo = pl.multiple_of(j * fc, fc)
        hi = pl.multiple_of((nj + j) * fc, fc)
        return (pltpu.make_async_copy(wi_hbm.at[layer, :, pl.ds(lo, fc)], wa_buf.at[slot], sem.at[0, slot]),
                pltpu.make_async_copy(wi_hbm.at[layer, :, pl.ds(hi, fc)], wb_buf.at[slot], sem.at[1, slot]),
                pltpu.make_async_copy(wo_hbm.at[layer, pl.ds(lo, fc), :], wo_buf.at[slot], sem.at[2, slot]))

    i = pl.program_id(0)
    first = i * nj

    @pl.when(i == 0)
    def _():
        for c in chunk_copies(0, 0):
            c.start()

    if tail_rows:
        last = pl.num_programs(0) - 1
        nvalid = o_ref.shape[0] - tail_rows

        @pl.when(i < last)
        def _():
            o_ref[...] = h_ref[...]

        @pl.when(i == last)
        def _():
            o_ref[0:nvalid, :] = h_ref[0:nvalid, :]
            o_ref[nvalid:, :] = tail_ref[...]

        r = o_ref[...]
    else:
        r = h_ref[...]
        if fuse_out:
            r = r + _dot(mix_ref[...], wout_ref[...])
        o_ref[...] = r
    xn_ref[...] = (_rms_rows(r) * g_ref[...]).astype(bf16)

    def body(j, carry):
        slot = (first + j) & 1

        @pl.when((j + 1 < nj) | (i + 1 < pl.num_programs(0)))
        def _():
            for c in chunk_copies(jnp.where(j + 1 < nj, j + 1, 0), 1 - slot):
                c.start()

        for c in chunk_copies(j, slot):
            c.wait()
        xn = xn_ref[...]
        a = _dot(xn, wa_buf[slot].astype(bf16))
        b = _dot(xn, wb_buf[slot].astype(bf16))
        act = (a / (1.0 + jnp.exp(-a)) * b).astype(bf16)
        o_ref[...] += 0.5 * _dot(act, wo_buf[slot].astype(bf16))
        return carry

    lax.fori_loop(0, nj, body, 0)


def _ffn(h, g, wi, wo, layer, *, tm, fc, mix=None, wout=None, n_rows=None, tail=None):
    d = h.shape[1]
    tail_rows = 0 if tail is None else tail.shape[0]
    n = h.shape[0] + tail_rows if n_rows is None else n_rows
    assert n % tm == 0 and 0 <= tail_rows < tm and not (tail_rows and (mix is not None or n_rows is not None))
    dff = wo.shape[1]
    nj = dff // fc
    fuse = mix is not None
    row = lambda i: (i, 0)
    in_specs = [pl.BlockSpec((tm, d), row)]
    args = [h]
    if tail_rows:
        in_specs += [pl.BlockSpec(tail.shape, lambda i: (0, 0))]
        args += [tail]
    if fuse:
        in_specs += [pl.BlockSpec((tm, mix.shape[1]), row), pl.BlockSpec(wout.shape, lambda i: (0, 0))]
        args += [mix, wout]
    in_specs += [
        pl.BlockSpec((1, d), lambda i: (0, 0)),
        pl.BlockSpec(memory_space=pl.ANY),
        pl.BlockSpec(memory_space=pl.ANY),
    ]
    args += [g.reshape(1, d), wi, wo]
    return pl.pallas_call(
        functools.partial(_ffn_kernel, fuse_out=fuse, tail_rows=tail_rows, layer=layer, fc=fc, nj=nj),
        grid=(n // tm,),
        in_specs=in_specs,
        out_specs=pl.BlockSpec((tm, d), row),
        out_shape=jax.ShapeDtypeStruct((n, d), f32),
        scratch_shapes=[
            pltpu.VMEM((tm, d), bf16),
            pltpu.VMEM((2, d, fc), f32),
            pltpu.VMEM((2, d, fc), f32),
            pltpu.VMEM((2, fc, d), f32),
            pltpu.SemaphoreType.DMA((3, 2)),
        ],
        compiler_params=_cparams(("arbitrary",)),
        name="ffn_out" if fuse else "ffn",
    )(*args)


def _rel_bucket(rel):
    half = REL_BUCKETS // 2
    max_exact = half // 2
    n = jnp.abs(rel)
    large = max_exact + (jnp.log(jnp.maximum(n, 1).astype(jnp.float32) / max_exact)
                         / math.log(REL_MAX_DIST / max_exact) * (half - max_exact)).astype(jnp.int32)
    large = jnp.minimum(large, half - 1)
    return jnp.where(rel > 0, half, 0) + jnp.where(n < max_exact, n, large)


def _rel_tiles():
    k = np.arange(LANES)[:, None]
    q = np.arange(LANES)[None, :]
    far = np.full((LANES, LANES), -4 * LANES)
    ones = np.ones((LANES, LANES), bool)
    rels = [k - q, k - q - LANES, far, (k % N_META) - N_META - q, (k % N_META) - (q % N_META), far, k - q - LANES]
    vis = [(k // CHUNK) <= (q // CHUNK), ones, ones, ones, (k // N_META) == (q // N_META), ~ones,
           (q < CHUNK) | (k >= CHUNK)]
    return (np.stack([np.broadcast_to(a, (LANES, LANES)) for a in rels]).astype(np.int32),
            np.stack([np.broadcast_to(a, (LANES, LANES)) for a in vis]).astype(np.int32))


def _bias_kernel(rb_ref, bucket_ref, vis_ref, o_ref):
    h = pl.program_id(0)
    for kind in range(N_KINDS):
        bk = bucket_ref[kind]
        acc = jnp.zeros((LANES, LANES), f32)
        for b in range(REL_BUCKETS):
            acc = jnp.where(bk == b, rb_ref[b, h], acc)
        o_ref[kind, 0] = jnp.where(vis_ref[kind] != 0, acc * LOG2E, NEG_INF)


def _bias_tiles(rel_bias):
    rel, vis = _rel_tiles()
    bucket = _rel_bucket(jnp.asarray(rel))
    nk = N_KINDS
    return pl.pallas_call(
        _bias_kernel,
        grid=(REL_HEADS,),
        in_specs=[
            pl.BlockSpec(memory_space=pltpu.SMEM),
            pl.BlockSpec((nk, LANES, LANES), lambda h: (0, 0, 0)),
            pl.BlockSpec((nk, LANES, LANES), lambda h: (0, 0, 0)),
        ],
        out_specs=pl.BlockSpec((nk, 1, LANES, LANES), lambda h: (0, h, 0, 0)),
        out_shape=jax.ShapeDtypeStruct((nk, REL_HEADS, LANES, LANES), f32),
        compiler_params=_cparams(("arbitrary",), vmem_mib=16),
        name="bias_tiles",
    )(rel_bias, bucket, jnp.asarray(vis))


def _bias_spec():
    return pl.BlockSpec((N_KINDS, REL_HEADS, LANES, LANES), lambda s: (0, 0, 0, 0))


def _meta_views(rows, cols_t, bsz):
    f = rows.shape[1]
    return (rows.reshape(bsz, N_META, f),
            cols_t.reshape(cols_t.shape[0], bsz, N_META).transpose(1, 0, 2))


A_QD = A_HEADS * 2 * A_HD
A_VR = A_VD + ONES_ROWS


def _proj_a_kernel(h_ref, g_ref, w_ref, gq_ref, gk_ref, qs_ref, k_ref, vt_ref):
    xn = (_rms_rows(h_ref[...]) * g_ref[...]).astype(bf16)
    y = _dot(xn, w_ref[...])
    q = _group_rms(y[:, :A_QD], A_HD) * gq_ref[...]
    lo = _lo_half_mask(q.shape, 2 * A_HD, A_HD)
    q_lo = jnp.where(lo, q, 0.0).astype(bf16)
    q_hi = jnp.where(lo, 0.0, q).astype(bf16)
    k_ref[...] = (_group_rms(y[:, A_QD:2 * A_QD], A_HD) * gk_ref[...]).astype(bf16)
    ones = jnp.ones((ONES_ROWS, LANES), bf16)
    for t in range(vt_ref.shape[0]):
        rows = slice(t * LANES, (t + 1) * LANES)
        vt = y[rows, 2 * A_QD:].T.astype(bf16)
        for h in range(A_HEADS):
            qs_ref[t, h, :LANES, :] = q_lo[rows, h * A_VD:(h + 1) * A_VD]
            qs_ref[t, h, LANES:, :] = q_hi[rows, h * A_VD:(h + 1) * A_VD]
            vt_ref[t, h * A_VR:h * A_VR + A_VD, :] = vt[h * A_VD:(h + 1) * A_VD]
            vt_ref[t, h * A_VR + A_VD:(h + 1) * A_VR, :] = ones


def _proj_a(h, g, w, qk_norm, *, tm):
    n, d = h.shape
    nw = w.shape[1]
    gq = jnp.tile(qk_norm[0] * (A_HD ** -0.5 * LOG2E), A_QD // A_HD).reshape(1, A_QD)
    gk = jnp.tile(qk_norm[1], A_QD // A_HD).reshape(1, A_QD)
    return pl.pallas_call(
        _proj_a_kernel,
        grid=(n // tm,),
        in_specs=[
            pl.BlockSpec((tm, d), lambda i: (i, 0)),
            pl.BlockSpec((1, d), lambda i: (0, 0)),
            pl.BlockSpec((d, nw), lambda i: (0, 0)),
            pl.BlockSpec((1, A_QD), lambda i: (0, 0)),
            pl.BlockSpec((1, A_QD), lambda i: (0, 0)),
        ],
        out_specs=[
            pl.BlockSpec((tm // LANES, A_HEADS, 2 * LANES, A_VD), lambda i: (i, 0, 0, 0)),
            pl.BlockSpec((tm, A_QD), lambda i: (i, 0)),
            pl.BlockSpec((tm // LANES, A_HEADS * A_VR, LANES), lambda i: (i, 0, 0)),
        ],
        out_shape=[
            jax.ShapeDtypeStruct((n // LANES, A_HEADS, 2 * LANES, A_VD), bf16),
            jax.ShapeDtypeStruct((n, A_QD), bf16),
            jax.ShapeDtypeStruct((n // LANES, A_HEADS * A_VR, LANES), bf16),
        ],
        compiler_params=_cparams(("arbitrary",), vmem_mib=40),
        name="proj_a",
    )(h, g.reshape(1, d), w, gq, gk)


def _attn_a_kernel(qs_ref, k_ref, vt_ref, km_ref, vtm_ref, kmf_ref, vtmf_ref, bias_ref, lam_ref, sub_ref,
                   o_ref, m_ref, acc_ref, *, nblk, lambda_init):
    s_id = pl.program_id(0)
    nb_real = pl.num_programs(0) - 1
    hw = 2 * A_HD

    def init():
        m_ref[...] = jnp.full(m_ref.shape, NEG_INF, f32)
        acc_ref[...] = jnp.zeros(acc_ref.shape, f32)

    def step_fns(kt_fn, vtt_fns, bias_fn):
        def logits():
            sts = [_dot_nt(kt_fn(h), qs_ref[0, h]) for h in range(A_HEADS)]
            if bias_fn is not None:
                sts = [st + bias_fn(h) for h, st in enumerate(sts)]
            return jnp.concatenate(sts, axis=1)

        def pv(pb):
            outs = []
            for h in range(A_HEADS):
                acc = None
                for vtt_fn, r0 in vtt_fns:
                    vtt = vtt_fn(h)
                    part = _dot(vtt, pb[r0:r0 + vtt.shape[1], h * 2 * LANES:(h + 1) * 2 * LANES])
                    acc = part if acc is None else acc + part
                outs.append(acc)
            return jnp.concatenate(outs, axis=1)

        return logits, pv

    def bias_rows(kind, h, rows):
        return jnp.concatenate([bias_ref[kind, h, 0:rows, :], bias_ref[kind, A_HEADS + h, 0:rows, :]], axis=1)

    def finalize():
        lam = lam_ref[...]
        lam_full = (jnp.exp(jnp.sum(lam[0:1] * lam[1:2], axis=-1, keepdims=True))
                    - jnp.exp(jnp.sum(lam[2:3] * lam[3:4], axis=-1, keepdims=True)) + lambda_init)
        o = acc_ref[0:A_VD, :] * (1.0 / acc_ref[A_VD:A_VD + 1, :])
        for h in range(A_HEADS):
            d = o[:, 2 * h * LANES:(2 * h + 1) * LANES] - lam_full * o[:, (2 * h + 1) * LANES:(2 * h + 2) * LANES]
            d = d * lax.rsqrt(jnp.mean(d * d, axis=0, keepdims=True) + EPS) * sub_ref[...] * (1.0 - lambda_init)
            o_ref[:, h * hw:(h + 1) * hw] = d.T.astype(bf16)

    @pl.when(s_id < nb_real)
    def _():
        i = s_id % nblk
        kind_m = jnp.where(i == 0, KIND_META0, KIND_FAR)
        nfar, near0, nnear = _sweep_steps(i)

        def k_fn(b0, nsub):
            off = pl.multiple_of(b0 * LANES, NSUB * LANES)
            return lambda h: k_ref[pl.ds(off, nsub * LANES), h * hw:(h + 1) * hw]

        def vt_fn(b0, nsub):
            return lambda h: jnp.concatenate([vt_ref[b0 + t, h * A_VR:(h + 1) * A_VR, :] for t in range(nsub)], axis=1)

        def near_bias(b0):
            kinds = [_block_kind(b0 + t - i) for t in range(NSUB)]
            return [lambda h, kind=kind: bias_rows(kind, h, LANES) for kind in kinds]

        def first_fns():
            biases = [lambda h: bias_rows(kind_m, h, N_META)] + near_bias(0)
            return step_fns(
                lambda h: jnp.concatenate([km_ref[0, :, h * hw:(h + 1) * hw], k_fn(0, NSUB)(h)], axis=0),
                [(lambda h: vtm_ref[0, h * A_VR:(h + 1) * A_VR, :], 0), (vt_fn(0, NSUB), N_META)],
                lambda h: jnp.concatenate([b(h) for b in biases], axis=0))

        def far_step(w):
            b0 = NSUB + w * NSUB_FAR
            far_bias = jnp.concatenate([bias_rows(KIND_FAR, h, 1) for h in range(A_HEADS)], axis=1)
            return step_fns(k_fn(b0, NSUB_FAR), [(vt_fn(b0, NSUB_FAR), 0)], None) + (far_bias,)

        def near_step(u):
            b0 = near0 + u * NSUB
            biases = near_bias(b0)
            return step_fns(k_fn(b0, NSUB), [(vt_fn(b0, NSUB), 0)],
                            lambda h: jnp.concatenate([b(h) for b in biases], axis=0)) + (None,)

        def sweep(lazy):
            init()
            _softmax_step(*first_fns(), m_ref, acc_ref, lazy)
            _softmax_loop(0, nfar, far_step, m_ref, acc_ref, lazy)
            _softmax_loop(0, nnear, near_step, m_ref, acc_ref, lazy)

        sweep(True)
        pl.when(_lazy_failed(acc_ref[A_VD:A_VD + 1, :]))(lambda: sweep(False))
        finalize()

    @pl.when(s_id == nb_real)
    def _():
        init()
        _softmax_step(*step_fns(lambda h: kmf_ref[:, h * hw:(h + 1) * hw],
                                [(lambda h: vtmf_ref[0, h * A_VR:(h + 1) * A_VR, :], 0)],
                                lambda h: bias_rows(KIND_METAMETA, h, LANES)), m_ref, acc_ref)
        finalize()


def _attn_a(qs, k, vt, bias, lam, subln, *, bsz, nblk, lambda_init):
    n = k.shape[0]
    nb_real = bsz * nblk
    seq = nblk * LANES
    d = A_QD
    vr = A_HEADS * A_VR
    assert nblk % NSUB == 0
    km, vtm = _meta_views(k[nb_real * LANES:], vt[nb_real], bsz)
    kern = functools.partial(_attn_a_kernel, nblk=nblk, lambda_init=lambda_init)
    bclamp = lambda s: jnp.minimum(s // nblk, bsz - 1)
    return pl.pallas_call(
        kern,
        grid=(nb_real + 1,),
        in_specs=[
            pl.BlockSpec((1,) + qs.shape[1:], lambda s: (s, 0, 0, 0)),
            pl.BlockSpec((seq, d), lambda s: (bclamp(s), 0)),
            pl.BlockSpec((nblk, vr, LANES), lambda s: (bclamp(s), 0, 0)),
            pl.BlockSpec((1, N_META, d), lambda s: (bclamp(s), 0, 0)),
            pl.BlockSpec((1, vr, N_META), lambda s: (bclamp(s), 0, 0)),
            pl.BlockSpec((LANES, d), lambda s: (nb_real, 0)),
            pl.BlockSpec((1, vr, LANES), lambda s: (nb_real, 0, 0)),
            _bias_spec(),
            pl.BlockSpec((4, A_HD), lambda s: (0, 0)),
            pl.BlockSpec((A_VD, LANES), lambda s: (0, 0)),
        ],
        out_specs=pl.BlockSpec((LANES, d), lambda s: (s, 0)),
        out_shape=jax.ShapeDtypeStruct((n, d), bf16),
        scratch_shapes=[
            pltpu.VMEM((1, A_HEADS * 2 * LANES), f32),
            pltpu.VMEM((A_VR, A_HEADS * 2 * LANES), f32),
        ],
        compiler_params=_cparams(("arbitrary",)),
        name="attn_a",
    )(qs, k, vt, km, vtm, k, vt, bias, lam, jnp.broadcast_to(subln[:, None], (A_VD, LANES)))


C_QD = C_Q_HEADS * C_HD
C_KD = 2 * C_KV_HEADS * C_HD
C_VR = 2 * C_HD + ONES_ROWS


def _proj_c_kernel(h_ref, g_ref, w_ref, gq_ref, gk_ref, qs_ref, k_ref, vt_ref):
    xn = (_rms_rows(h_ref[...]) * g_ref[...]).astype(bf16)
    y = _dot(xn, w_ref[...])
    q = _group_rms(y[:, :C_QD], C_HD) * gq_ref[...]
    lo = _lo_half_mask(q.shape, 2 * C_HD, C_HD)
    q_even = jnp.where(lo, q, 0.0).astype(bf16)
    q_odd = jnp.where(lo, 0.0, q).astype(bf16)
    k_ref[...] = (_group_rms(y[:, C_QD:C_QD + C_KD], C_HD) * gk_ref[...]).astype(bf16)
    ones = jnp.ones((ONES_ROWS, LANES), bf16)
    for t in range(vt_ref.shape[0]):
        rows = slice(t * LANES, (t + 1) * LANES)
        vt = y[rows, C_QD + C_KD:].T.astype(bf16)
        for g in range(C_KV_HEADS):
            for hh in range(C_GROUP):
                pair = (g * C_GROUP + hh) // 2
                src = q_even if hh % 2 == 0 else q_odd
                qs_ref[t, g, hh * LANES:(hh + 1) * LANES, :] = src[rows, pair * LANES:(pair + 1) * LANES]
            vt_ref[t, g * C_VR:g * C_VR + 2 * C_HD, :] = vt[g * 2 * C_HD:(g + 1) * 2 * C_HD]
            vt_ref[t, g * C_VR + 2 * C_HD:(g + 1) * C_VR, :] = ones


def _proj_c(h, g, w, qk_norm, *, tm):
    n, d = h.shape
    nw = w.shape[1]
    nt = tm // LANES
    gq = jnp.tile(qk_norm[0] * (C_HD ** -0.5 * LOG2E), C_QD // C_HD).reshape(1, C_QD)
    gk = jnp.tile(qk_norm[1], C_KD // C_HD).reshape(1, C_KD)
    return pl.pallas_call(
        _proj_c_kernel,
        grid=(n // tm,),
        in_specs=[
            pl.BlockSpec((tm, d), lambda i: (i, 0)),
            pl.BlockSpec((1, d), lambda i: (0, 0)),
            pl.BlockSpec((d, nw), lambda i: (0, 0)),
            pl.BlockSpec((1, C_QD), lambda i: (0, 0)),
            pl.BlockSpec((1, C_KD), lambda i: (0, 0)),
        ],
        out_specs=[
            pl.BlockSpec((nt, C_KV_HEADS, C_GROUP * LANES, LANES), lambda i: (i, 0, 0, 0)),
            pl.BlockSpec((tm, C_KD), lambda i: (i, 0)),
            pl.BlockSpec((nt, C_KV_HEADS * C_VR, LANES), lambda i: (i, 0, 0)),
        ],
        out_shape=[
            jax.ShapeDtypeStruct((n // LANES, C_KV_HEADS, C_GROUP * LANES, LANES), bf16),
            jax.ShapeDtypeStruct((n, C_KD), bf16),
            jax.ShapeDtypeStruct((n // LANES, C_KV_HEADS * C_VR, LANES), bf16),
        ],
        compiler_params=_cparams(("arbitrary",), vmem_mib=36),
        name="proj_c",
    )(h, g.reshape(1, d), w, gq, gk)


def _attn_c_kernel(sink_ref, qs_ref, k_ref, vt_ref, km_ref, vtm_ref, kmf_ref, vtmf_ref, bias_ref, o_ref, *, nblk):
    s_id = pl.program_id(0)
    nb_real = pl.num_programs(0) - 1
    r_io, _ = _tile_iotas()
    vd = 2 * C_HD

    def attend_with(tiles, lazy):
        top = r_io < C_HD
        failed = None
        for g in range(C_KV_HEADS):
            sink = jnp.concatenate(
                [jnp.full((1, LANES), sink_ref[g * C_GROUP + hh] * LOG2E, f32) for hh in range(C_GROUP)], axis=1)
            sts = []
            m = sink
            for (k_fn, vt_fn, bias_fn) in tiles:
                st = _dot_nt(k_fn(g), qs_ref[0, g])
                st = st + jnp.concatenate([bias_fn(g * C_GROUP + hh) for hh in range(C_GROUP)], axis=1)
                if not lazy:
                    m = jnp.maximum(m, jnp.max(st, axis=0, keepdims=True))
                sts.append(st)
            acc = None
            for st, (k_fn, vt_fn, bias_fn) in zip(sts, tiles):
                pv = _dot(vt_fn(g), jnp.exp2(st if lazy else st - m).astype(bf16))
                acc = pv if acc is None else acc + pv
            denom = acc[vd:vd + 1, :] + jnp.exp2(sink if lazy else sink - m)
            if lazy:
                failed = _lazy_failed(denom) if failed is None else failed | _lazy_failed(denom)
            o = acc[0:vd, :] * (1.0 / denom)
            for cc in range(C_GROUP // 2):
                even = o[:, (2 * cc) * LANES:(2 * cc + 1) * LANES]
                odd = o[:, (2 * cc + 1) * LANES:(2 * cc + 2) * LANES]
                col = (g * (C_GROUP // 2) + cc) * LANES
                o_ref[:, col:col + LANES] = jnp.where(top, even, odd).T.astype(bf16)
        return failed

    def attend(tiles):
        failed = attend_with(tiles, True)

        @pl.when(failed)
        def _():
            attend_with(tiles, False)

    @pl.when(s_id < nb_real)
    def _():
        i = s_id % nblk
        prev = jnp.maximum(i - 1, 0)
        poff = pl.multiple_of(prev * LANES, LANES)
        coff = pl.multiple_of(i * LANES, LANES)
        kind_m = jnp.where(i == 0, KIND_META0, KIND_FAR)
        kind_p = jnp.where(i == 0, KIND_MASKED, KIND_PREVWIN)
        attend([
            (lambda g: km_ref[0, :, g * LANES:(g + 1) * LANES], lambda g: vtm_ref[0, g * C_VR:(g + 1) * C_VR, :],
             lambda h: bias_ref[kind_m, h, 0:N_META, :]),
            (lambda g: k_ref[pl.ds(poff, LANES), g * LANES:(g + 1) * LANES],
             lambda g: vt_ref[prev, g * C_VR:(g + 1) * C_VR, :], lambda h: bias_ref[kind_p, h]),
            (lambda g: k_ref[pl.ds(coff, LANES), g * LANES:(g + 1) * LANES],
             lambda g: vt_ref[i, g * C_VR:(g + 1) * C_VR, :], lambda h: bias_ref[KIND_DIAG, h]),
        ])

    @pl.when(s_id == nb_real)
    def _():
        attend([(lambda g: kmf_ref[:, g * LANES:(g + 1) * LANES], lambda g: vtmf_ref[0, g * C_VR:(g + 1) * C_VR, :],
                 lambda h: bias_ref[KIND_METAMETA, h])])


def _attn_c(qs, k, vt, bias, sinks, *, bsz, nblk):
    n = k.shape[0]
    nb_real = bsz * nblk
    seq = nblk * LANES
    vr = C_KV_HEADS * C_VR
    kern = functools.partial(_attn_c_kernel, nblk=nblk)
    bclamp = lambda s: jnp.minimum(s // nblk, bsz - 1)
    km, vtm = _meta_views(k[nb_real * LANES:], vt[nb_real], bsz)
    return pl.pallas_call(
        kern,
        grid=(nb_real + 1,),
        in_specs=[
            pl.BlockSpec(memory_space=pltpu.SMEM),
            pl.BlockSpec((1,) + qs.shape[1:], lambda s: (s, 0, 0, 0)),
            pl.BlockSpec((seq, C_KD), lambda s: (bclamp(s), 0)),
            pl.BlockSpec((nblk, vr, LANES), lambda s: (bclamp(s), 0, 0)),
            pl.BlockSpec((1, N_META, C_KD), lambda s: (bclamp(s), 0, 0)),
            pl.BlockSpec((1, vr, N_META), lambda s: (bclamp(s), 0, 0)),
            pl.BlockSpec((LANES, C_KD), lambda s: (nb_real, 0)),
            pl.BlockSpec((1, vr, LANES), lambda s: (nb_real, 0, 0)),
            _bias_spec(),
        ],
        out_specs=pl.BlockSpec((LANES, C_QD), lambda s: (s, 0)),
        out_shape=jax.ShapeDtypeStruct((n, C_QD), bf16),
        compiler_params=_cparams(("arbitrary",), vmem_mib=36),
        name="attn_c",
    )(sinks, qs, k, vt, km, vtm, k, vt, bias)


B_QA = B_HEADS * B_KV_RANK
B_QI = IDX_HEADS * IDX_DIM
B_W1 = 2 * B_Q_RANK + 2 * LANES
B_TR = B_KV_RANK + ONES_ROWS


def _proj_b_kernel(h_ref, g_ref, w1_ref, ln_ref, wuq_ref, qn_ref,
                   qa_ref, qi_ref, ckv_ref, ckvt_ref, kk_ref, wit_ref):
    xn = (_rms_rows(h_ref[...]) * g_ref[...]).astype(bf16)
    y = _dot(xn, w1_ref[...])
    r = B_Q_RANK
    cq = (_rms_rows(y[:, :r]) * ln_ref[0:1, :]).astype(bf16)
    ckv = _rms_rows(y[:, r:2 * r]) * ln_ref[1:2, :]
    ckv_ref[...] = ckv.astype(bf16)
    kk_ref[...] = _rms_rows(y[:, 2 * r:2 * r + LANES]).astype(bf16)
    wi = y[:, 2 * r + LANES:] * (IDX_HEADS ** -0.5)
    ones = jnp.ones((ONES_ROWS, LANES), bf16)
    for t in range(ckvt_ref.shape[0]):
        ckvt_ref[t, 0:r, :] = ckv[t * LANES:(t + 1) * LANES, :].T.astype(bf16)
        ckvt_ref[t, r:, :] = ones
        wit_ref[t] = wi[t * LANES:(t + 1) * LANES, :].T[0:IDX_HEADS, :]
    z = _dot(cq, wuq_ref[...])
    qa = (_group_rms(z[:, :B_QA], B_KV_RANK) * qn_ref[...]).astype(bf16)
    qi = z[:, B_QA:] * (IDX_DIM ** -0.5)
    lo = _lo_half_mask(qi.shape, 2 * IDX_DIM, IDX_DIM)
    qi_lo = jnp.where(lo, qi, 0.0).astype(bf16)
    qi_hi = jnp.where(lo, 0.0, qi).astype(bf16)
    for t in range(qa_ref.shape[0]):
        rows = slice(t * LANES, (t + 1) * LANES)
        for hd in range(B_HEADS):
            qa_ref[t, hd * LANES:(hd + 1) * LANES, :] = qa[rows, hd * r:(hd + 1) * r]
        for hh in range(IDX_HEADS):
            src = qi_lo if hh % 2 == 0 else qi_hi
            qi_ref[t, hh * LANES:(hh + 1) * LANES, :] = src[rows, (hh // 2) * LANES:(hh // 2 + 1) * LANES]


def _proj_b(h, g, w1, latent_norm, wuq, q_norm, *, tm):
    n, d = h.shape
    qn = jnp.tile(q_norm * (B_KV_RANK ** -0.5 * LOG2E), B_HEADS).reshape(1, B_QA)
    row = lambda i: (i, 0)
    row3 = lambda i: (i, 0, 0)
    const = lambda i: (0, 0)
    nt = tm // LANES
    return pl.pallas_call(
        _proj_b_kernel,
        grid=(n // tm,),
        in_specs=[
            pl.BlockSpec((tm, d), row),
            pl.BlockSpec((1, d), const),
            pl.BlockSpec(w1.shape, const),
            pl.BlockSpec(latent_norm.shape, const),
            pl.BlockSpec(wuq.shape, const),
            pl.BlockSpec((1, B_QA), const),
        ],
        out_specs=[
            pl.BlockSpec((nt, B_HEADS * LANES, B_KV_RANK), row3),
            pl.BlockSpec((nt, IDX_HEADS * LANES, LANES), row3),
            pl.BlockSpec((tm, B_KV_RANK), row),
            pl.BlockSpec((nt, B_TR, LANES), row3),
            pl.BlockSpec((tm, LANES), row),
            pl.BlockSpec((nt, IDX_HEADS, LANES), row3),
        ],
        out_shape=[
            jax.ShapeDtypeStruct((n // LANES, B_HEADS * LANES, B_KV_RANK), bf16),
            jax.ShapeDtypeStruct((n // LANES, IDX_HEADS * LANES, LANES), bf16),
            jax.ShapeDtypeStruct((n, B_KV_RANK), bf16),
            jax.ShapeDtypeStruct((n // LANES, B_TR, LANES), bf16),
            jax.ShapeDtypeStruct((n, LANES), bf16),
            jax.ShapeDtypeStruct((n // LANES, IDX_HEADS, LANES), f32),
        ],
        compiler_params=_cparams(("arbitrary",), vmem_mib=48),
        name="proj_b",
    )(h, g.reshape(1, d), w1, latent_norm, wuq, qn)


def _attn_b_kernel(qs_ref, is_ref, wit_ref, ckv_ref, ckvt_ref, kk_ref, ckvm_ref, ckvtm_ref, kkm_ref,
                   ckvmf_ref, ckvtmf_ref, bias_ref, wuvt_ref,
                   o_ref, key_ref, pen_ref, m_ref, acc_ref, *, nblk, k_sel):
    s_id = pl.program_id(0)
    nb_real = pl.num_programs(0) - 1
    r_io, c_io = _tile_iotas()
    rk = B_KV_RANK

    def init():
        m_ref[...] = jnp.full(m_ref.shape, NEG_INF, f32)
        acc_ref[...] = jnp.zeros(acc_ref.shape, f32)

    def add_per_head(st, bias_fn, pen):
        cols = []
        for h in range(B_HEADS):
            add = pen if bias_fn is None else (bias_fn(h) if pen is None else bias_fn(h) + pen)
            cols.append(st[:, h * LANES:(h + 1) * LANES] + add)
        return jnp.concatenate(cols, axis=1)

    def step_fns(ckv_fn, ckvt_fns, bias_fn, pen_fn):
        def logits():
            st = _dot_nt(ckv_fn(), qs_ref[0])
            return add_per_head(st, bias_fn, None if pen_fn is None else pen_fn())

        def pv(pb):
            acc = None
            for ckvt_fn, r0 in ckvt_fns:
                ckvt = ckvt_fn()
                part = _dot(ckvt, pb[r0:r0 + ckvt.shape[1], :])
                acc = part if acc is None else acc + part
            return acc

        return logits, pv

    def finalize():
        olat = (acc_ref[0:rk, :] * (1.0 / acc_ref[rk:rk + 1, :])).astype(bf16)
        ot = jnp.concatenate([_dot(wuvt_ref[h], olat[:, h * LANES:(h + 1) * LANES]) for h in range(B_HEADS)], axis=0)
        o_ref[...] = ot.T.astype(bf16)

    def index_scores(kk):
        s = jnp.maximum(_dot_nt(kk, is_ref[0]), 0.0)
        wt = wit_ref[0]
        sc = jnp.zeros((kk.shape[0], LANES), f32)
        for hh in range(IDX_HEADS):
            sc = sc + wt[hh:hh + 1, :] * s[:, hh * LANES:(hh + 1) * LANES]
        return sc

    def sort_key(sc):
        bits = lax.bitcast_convert_type(sc + 0.0, jnp.int32)
        return jnp.where(bits < 0, bits ^ jnp.int32(0x7FFFFFFF), bits)

    @pl.when(s_id < nb_real)
    def _():
        i = s_id % nblk
        ntile = i + 2

        int_min_tile = jnp.full((LANES, LANES), INT_MIN, jnp.int32)
        key_ref[0] = int_min_tile
        key_ref[0, 0:N_META, :] = sort_key(index_scores(kkm_ref[0]))
        key_ref[i + 2] = int_min_tile

        def score_body(jp, carry):
            off = pl.multiple_of(jp * (2 * LANES), 2 * LANES)
            keys = sort_key(index_scores(kk_ref[pl.ds(off, 2 * LANES), :]))
            for t in range(2):
                j = _vec(2 * jp + t)
                vis = (j < i) | ((j == i) & ((r_io >> 6) <= (c_io >> 6)))
                key_ref[2 * jp + t + 1] = jnp.where(vis, keys[t * LANES:(t + 1) * LANES], jnp.int32(INT_MIN))
            return carry

        lax.fori_loop(0, i // 2 + 1, score_body, 0)

        def count(pred):
            def cbody(tp, accv):
                for t in (2 * tp, 2 * tp + 1):
                    accv = accv + jnp.where(pred(key_ref[t], t), 1.0, 0.0)
                return accv
            accv = lax.fori_loop(0, (ntile + 1) // 2, cbody, jnp.zeros((LANES, LANES), f32))
            return jnp.sum(accv, axis=0, keepdims=True)

        kf = float(k_sel)
        zero = jnp.zeros((1, LANES), jnp.int32)
        t0 = jnp.where(count(lambda k, t: k >= zero) >= kf, zero, jnp.int32(INT_MIN))

        def bit_body(it, tcur):
            cand = tcur | jnp.left_shift(jnp.int32(1), 30 - it)
            return jnp.where(count(lambda k, t: k >= cand) >= kf, cand, tcur)

        thr = lax.fori_loop(0, 31, bit_body, t0)

        need = kf - count(lambda k, t: k > thr)
        n_eq = count(lambda k, t: k == thr)
        has_thr = thr > jnp.int32(INT_MIN)
        tied = jnp.max(jnp.where(has_thr & (n_eq > need), 1.0, 0.0)) > 0.0

        def tie_search(_):
            def jbody(it, jcur):
                cand = jcur | jnp.left_shift(jnp.int32(1), 11 - it)
                cnt = count(lambda k, t: (k == thr) & ((t * LANES + r_io) < cand))
                return jnp.where(cnt < need, cand, jcur)
            return lax.fori_loop(0, 12, jbody, jnp.zeros((1, LANES), jnp.int32))

        j_last = lax.cond(tied, tie_search, lambda _: jnp.full((1, LANES), 4095, jnp.int32), 0)
        j_last = jnp.where(has_thr, j_last, -1)

        def pen_body(t, carry):
            k = key_ref[t]
            sel = (k > thr) | ((k == thr) & ((t * LANES + r_io) <= j_last))
            pen_ref[t] = jnp.where(sel, 0.0, NEG_INF)
            return carry

        lax.fori_loop(0, ntile, pen_body, 0)
        for t in range(1, NSUB):
            pen_ref[i + 1 + t] = jnp.full((LANES, LANES), NEG_INF, f32)

        kind_m = jnp.where(i == 0, KIND_META0, KIND_FAR)
        nfar, near0, nnear = _sweep_steps(i)

        def ckv_fn(b0, nsub):
            off = pl.multiple_of(b0 * LANES, NSUB * LANES)
            return lambda: ckv_ref[pl.ds(off, nsub * LANES), :]

        def ckvt_fn(b0, nsub):
            return lambda: jnp.concatenate([ckvt_ref[b0 + t] for t in range(nsub)], axis=1)

        def pen_fn(b0, nsub):
            return lambda: jnp.concatenate([pen_ref[b0 + t + 1] for t in range(nsub)], axis=0)

        def near_bias(b0):
            kinds = [_block_kind(b0 + t - i) for t in range(NSUB)]
            return lambda h: jnp.concatenate([bias_ref[kinds[t], h] for t in range(NSUB)], axis=0)

        def first_pen():
            return jnp.concatenate([pen_ref[0, 0:N_META, :], pen_fn(0, NSUB)()], axis=0)

        def first_fns():
            bias01 = near_bias(0)
            return step_fns(lambda: jnp.concatenate([ckvm_ref[0], ckv_fn(0, NSUB)()], axis=0),
                            [(lambda: ckvtm_ref[0], 0), (ckvt_fn(0, NSUB), N_META)],
                            lambda h: jnp.concatenate([bias_ref[kind_m, h, 0:N_META, :], bias01(h)], axis=0),
                            first_pen)

        def far_step(w):
            b0 = NSUB + w * NSUB_FAR
            far_bias = jnp.concatenate([bias_ref[KIND_FAR, h, 0:1, :] for h in range(B_HEADS)], axis=1)
            return step_fns(ckv_fn(b0, NSUB_FAR), [(ckvt_fn(b0, NSUB_FAR), 0)], None, pen_fn(b0, NSUB_FAR)) + (far_bias,)

        def near_step(u):
            b0 = near0 + u * NSUB
            return step_fns(ckv_fn(b0, NSUB), [(ckvt_fn(b0, NSUB), 0)], near_bias(b0), pen_fn(b0, NSUB)) + (None,)

        def sweep(lazy):
            init()
            _softmax_step(*first_fns(), m_ref, acc_ref, lazy)
            _softmax_loop(0, nfar, far_step, m_ref, acc_ref, lazy)
            _softmax_loop(0, nnear, near_step, m_ref, acc_ref, lazy)

        sweep(True)
        pl.when(_lazy_failed(acc_ref[rk:rk + 1, :]))(lambda: sweep(False))
        finalize()

    @pl.when(s_id == nb_real)
    def _():
        init()
        _softmax_step(*step_fns(lambda: ckvmf_ref[...], [(lambda: ckvtmf_ref[0], 0)],
                                lambda h: bias_ref[KIND_METAMETA, h], None), m_ref, acc_ref)
        finalize()


def _attn_b(qa, qi, wit, ckv, ckvt, kk, bias, wuvt, *, bsz, nblk, k_sel):
    n = ckv.shape[0]
    nb_real = bsz * nblk
    seq = nblk * LANES
    assert k_sel >= N_META and (nblk + 1) * LANES <= 4096 and nblk % NSUB == 0
    ckvm, ckvtm = _meta_views(ckv[nb_real * LANES:], ckvt[nb_real], bsz)
    kkm = kk[nb_real * LANES:].reshape(bsz, N_META, LANES)
    kern = functools.partial(_attn_b_kernel, nblk=nblk, k_sel=k_sel)
    bidx = lambda s: jnp.minimum(s // nblk, bsz - 1)
    blk = lambda s: (s, 0)
    return pl.pallas_call(
        kern,
        grid=(nb_real + 1,),
        in_specs=[
            pl.BlockSpec((1,) + qa.shape[1:], lambda s: (s, 0, 0)),
            pl.BlockSpec((1,) + qi.shape[1:], lambda s: (s, 0, 0)),
            pl.BlockSpec((1, IDX_HEADS, LANES), lambda s: (s, 0, 0)),
            pl.BlockSpec((seq, B_KV_RANK), lambda s: (bidx(s), 0)),
            pl.BlockSpec((nblk, B_TR, LANES), lambda s: (bidx(s), 0, 0)),
            pl.BlockSpec((seq, LANES), lambda s: (bidx(s), 0)),
            pl.BlockSpec((1, N_META, B_KV_RANK), lambda s: (bidx(s), 0, 0)),
            pl.BlockSpec((1, B_TR, N_META), lambda s: (bidx(s), 0, 0)),
            pl.BlockSpec((1, N_META, LANES), lambda s: (bidx(s), 0, 0)),
            pl.BlockSpec((LANES, B_KV_RANK), lambda s: (nb_real, 0)),
            pl.BlockSpec((1, B_TR, LANES), lambda s: (nb_real, 0, 0)),
            _bias_spec(),
            pl.BlockSpec(wuvt.shape, lambda s: (0, 0, 0)),
        ],
        out_specs=pl.BlockSpec((LANES, B_HEADS * B_VD), blk),
        out_shape=jax.ShapeDtypeStruct((n, B_HEADS * B_VD), bf16),
        scratch_shapes=[
            pltpu.VMEM((nblk + 2, LANES, LANES), jnp.int32),
            pltpu.VMEM((nblk + NSUB, LANES, LANES), f32),
            pltpu.VMEM((1, B_HEADS * LANES), f32),
            pltpu.VMEM((B_TR, B_HEADS * LANES), f32),
        ],
        compiler_params=_cparams(("arbitrary",)),
        name="attn_b",
    )(qa, qi, wit, ckv, ckvt, kk, ckvm, ckvtm, kkm, ckv, ckvt, bias, wuvt)


def kernel(x, meta_tokens, rel_bias, ln_ffn1, ffn1_wi, ffn1_wo, ln_mix, w_out, ln_ffn2, ffn2_wi, ffn2_wo, a_w_in, a_qk_norm, a_lambda, a_subln, b_w_in, b_latent_norm, b_w_uq, b_q_norm, b_w_uv, c_w_in, c_qk_norm, c_sinks):
    bsz, seq, d = x.shape
    assert d == D_MODEL and seq % LANES == 0 and bsz * N_META == LANES
    nblk = seq // LANES
    n = bsz * seq + LANES
    k_sel = min(TOPK_MAX, seq // 4)
    tm_ffn = _row_tile(n, 1408)
    tm_last = _row_tile(bsz * seq, 1408)
    tm_proj = _row_tile(n, 384, LANES)
    fc = 256

    h = x.reshape(bsz * seq, d)
    meta_rows = jnp.broadcast_to(meta_tokens.astype(x.dtype), (bsz, N_META, d)).reshape(LANES, d)
    bias = _bias_tiles(rel_bias)

    for layer in range(DEPTH):
        h = _ffn(h, ln_ffn1[layer], ffn1_wi, ffn1_wo, layer, tm=tm_ffn, fc=fc,
                 tail=meta_rows if layer == 0 else None)
        kind, j = layer % N_MIXERS, layer // N_MIXERS
        g = ln_mix[layer]
        if kind == 0:
            lambda_init = 0.8 - 0.6 * math.exp(-0.3 * layer)
            qs, k, vt = _proj_a(h, g, a_w_in[j].astype(bf16), a_qk_norm[j], tm=tm_proj)
            mix = _attn_a(qs, k, vt, bias, a_lambda[j], a_subln[j], bsz=bsz, nblk=nblk, lambda_init=lambda_init)
        elif kind == 1:
            w = b_w_in[j]
            r2 = B_Q_RANK + B_KV_RANK
            kcol = w[:, r2:r2 + IDX_DIM]
            w1 = jnp.concatenate([w[:, :r2], kcol, kcol, w[:, r2 + IDX_DIM:],
                                  jnp.zeros((d, LANES - IDX_HEADS), w.dtype)], axis=1).astype(bf16)
            assert w1.shape[1] == B_W1
            qa, qi, ckv, ckvt, kk, wit = _proj_b(h, g, w1, b_latent_norm[j], b_w_uq[j].astype(bf16), b_q_norm[j],
                                                 tm=tm_proj)
            wuvt = jnp.swapaxes(b_w_uv[j], 1, 2).astype(bf16)
            mix = _attn_b(qa, qi, wit, ckv, ckvt, kk, bias, wuvt, bsz=bsz, nblk=nblk, k_sel=k_sel)
        else:
            w = c_w_in[j]
            kcols = [w[:, C_QD + gi * C_HD:C_QD + (gi + 1) * C_HD] for gi in range(C_KV_HEADS)]
            voff = C_QD + C_KV_HEADS * C_HD
            vcols = [w[:, voff + gi * C_HD:voff + (gi + 1) * C_HD] for gi in range(C_KV_HEADS)]
            wc = jnp.concatenate([w[:, :C_QD]] + [kc for kc in kcols for _ in range(2)]
                                 + [vc for vc in vcols for _ in range(2)], axis=1).astype(bf16)
            qs, k, vt = _proj_c(h, g, wc, c_qk_norm[j], tm=tm_proj)
            mix = _attn_c(qs, k, vt, bias, c_sinks[j], bsz=bsz, nblk=nblk)
        last = layer == DEPTH - 1
        h = _ffn(h, ln_ffn2[layer], ffn2_wi, ffn2_wo, layer, tm=tm_last if last else tm_ffn, fc=fc,
                 mix=mix, wout=w_out[layer].astype(bf16), n_rows=bsz * seq if last else None)
    return h.reshape(bsz, seq, d)
```

```python
import functools
import math

import numpy as np
import jax
import jax.numpy as jnp
from jax import lax
from jax.experimental import pallas as pl
from jax.experimental.pallas import tpu as pltpu

D_MODEL = 1024
DEPTH = 4
CHUNK = 64
N_META = 16
N_MIXERS = 3
NEG_INF = -1e30
REL_BUCKETS = 32
REL_MAX_DIST = 128
REL_HEADS = 16
D_FF = 2816
A_HEADS = 8
A_HD = 64
A_VD = 2 * A_HD
B_HEADS = 16
B_Q_RANK = 256
B_KV_RANK = 256
B_VD = 64
IDX_HEADS = 8
IDX_DIM = 64
TOPK_MAX = 256
C_Q_HEADS = 16
C_KV_HEADS = 2
C_GROUP = C_Q_HEADS // C_KV_HEADS
C_HD = 64
EPS = 1e-6

LANES = 128
BF16_ROWS = 16
VMEM_LIMIT = 56 * 1024 * 1024
INT_MIN = -(2 ** 31)
NSUB = 2
NSUB_FAR = 4
LOG2E = math.log2(math.e)
LAZY_CEIL = 2.0 ** 70
LAZY_FLOOR = 2.0 ** -100
ONES_ROWS = BF16_ROWS

KIND_DIAG, KIND_PREV, KIND_FAR, KIND_META0, KIND_METAMETA, KIND_MASKED, KIND_PREVWIN = 0, 1, 2, 3, 4, 5, 6
N_KINDS = 7

f32 = jnp.float32
bf16 = jnp.bfloat16


def _cparams(sem):
    return pltpu.CompilerParams(dimension_semantics=sem, vmem_limit_bytes=VMEM_LIMIT)


def _row_tile(n, cap, mult=BF16_ROWS):
    best = None
    for t in range(mult, cap + 1, mult):
        if n % t == 0:
            best = t
    assert best is not None
    return best


def _dot(a, b):
    return jnp.dot(a, b, preferred_element_type=f32)


def _dot_nt(a, b):
    return lax.dot_general(a, b, (((1,), (1,)), ((), ())), preferred_element_type=f32)


def _rms_rows(x):
    return x * lax.rsqrt(jnp.mean(x * x, axis=-1, keepdims=True) + EPS)


def _lo_half_mask(shape, period, half):
    return (lax.broadcasted_iota(jnp.int32, shape, 1) & (period - 1)) < half


def _group_rms(x, group):
    r, c = x.shape
    outs = []
    if group == 64:
        lo = _lo_half_mask((r, LANES), LANES, 64)
        for ci in range(c // LANES):
            xc = x[:, ci * LANES:(ci + 1) * LANES]
            x2 = xc * xc
            s_lo = jnp.sum(jnp.where(lo, x2, 0.0), axis=-1, keepdims=True)
            s_hi = jnp.sum(jnp.where(lo, 0.0, x2), axis=-1, keepdims=True)
            inv = jnp.where(lo, lax.rsqrt(s_lo * (1.0 / 64) + EPS), lax.rsqrt(s_hi * (1.0 / 64) + EPS))
            outs.append(xc * inv)
    else:
        for gi in range(c // group):
            outs.append(_rms_rows(x[:, gi * group:(gi + 1) * group]))
    return outs[0] if len(outs) == 1 else jnp.concatenate(outs, axis=-1)


def _tile_iotas():
    r = lax.broadcasted_iota(jnp.int32, (LANES, LANES), 0)
    c = lax.broadcasted_iota(jnp.int32, (LANES, LANES), 1)
    return r, c


def _vec(s):
    return jnp.full((LANES, LANES), s, jnp.int32)


def _softmax_step(logits_fn, pv_fn, m_ref, acc_ref, lazy=False, offset=None):
    st = logits_fn()
    if offset is not None:
        st = st + offset
    if lazy:
        acc_ref[...] += pv_fn(jnp.exp2(st).astype(bf16))
    else:
        m_old = m_ref[...]
        m_new = jnp.maximum(m_old, jnp.max(st, axis=0, keepdims=True))
        acc_ref[...] = jnp.exp2(m_old - m_new) * acc_ref[...] + pv_fn(jnp.exp2(st - m_new).astype(bf16))
        m_ref[...] = m_new


def _softmax_loop(lo, hi, step_fn, m_ref, acc_ref, lazy):
    def body(w, carry):
        logits_fn, pv_fn, offset = step_fn(w)
        _softmax_step(logits_fn, pv_fn, m_ref, acc_ref, lazy, offset)
        return carry

    lax.fori_loop(lo, hi, body, 0)


def _lazy_failed(denominators):
    return jnp.logical_not((jnp.max(denominators) <= LAZY_CEIL) & (jnp.min(denominators) >= LAZY_FLOOR))


def _block_kind(rel):
    return jnp.where(rel < -1, KIND_FAR,
                     jnp.where(rel == -1, KIND_PREV, jnp.where(rel == 0, KIND_DIAG, KIND_MASKED)))


def _sweep_steps(i):
    nfar = jnp.maximum(i - 1 - NSUB, 0) // NSUB_FAR
    near0 = NSUB + nfar * NSUB_FAR
    return nfar, near0, (i - near0 + NSUB) // NSUB


def _ffn_kernel(*refs, fuse_out, tail_rows, layer, fc, nj):
    if fuse_out:
        h_ref, mix_ref, wout_ref, g_ref, wi_hbm, wo_hbm, o_ref, xn_ref, wa_buf, wb_buf, wo_buf, sem = refs
    elif tail_rows:
        h_ref, tail_ref, g_ref, wi_hbm, wo_hbm, o_ref, xn_ref, wa_buf, wb_buf, wo_buf, sem = refs
    else:
        h_ref, g_ref, wi_hbm, wo_hbm, o_ref, xn_ref, wa_buf, wb_buf, wo_buf, sem = refs

    def chunk_copies(j, slot):
        lo = pl.multiple_of(j * fc, fc)
        hi = pl.multiple_of((nj + j) * fc, fc)
        return (pltpu.make_async_copy(wi_hbm.at[layer, :, pl.ds(lo, fc)], wa_buf.at[slot], sem.at[0, slot]),
                pltpu.make_async_copy(wi_hbm.at[layer, :, pl.ds(hi, fc)], wb_buf.at[slot], sem.at[1, slot]),
                pltpu.make_async_copy(wo_hbm.at[layer, pl.ds(lo, fc), :], wo_buf.at[slot], sem.at[2, slot]))

    i = pl.program_id(0)
    first = i * nj

    @pl.when(i == 0)
    def _():
        for c in chunk_copies(0, 0):
            c.start()

    if tail_rows:
        last = pl.num_programs(0) - 1
        nvalid = o_ref.shape[0] - tail_rows

        @pl.when(i < last)
        def _():
            o_ref[...] = h_ref[...]

        @pl.when(i == last)
        def _():
            o_ref[0:nvalid, :] = h_ref[0:nvalid, :]
            o_ref[nvalid:, :] = tail_ref[...]

        r = o_ref[...]
    else:
        r = h_ref[...]
        if fuse_out:
            r = r + _dot(mix_ref[...], wout_ref[...])
        o_ref[...] = r
    xn_ref[...] = (_rms_rows(r) * g_ref[...]).astype(bf16)

    def body(j, carry):
        slot = (first + j) & 1

        @pl.when((j + 1 < nj) | (i + 1 < pl.num_programs(0)))
        def _():
            for c in chunk_copies(jnp.where(j + 1 < nj, j + 1, 0), 1 - slot):
                c.start()

        for c in chunk_copies(j, slot):
            c.wait()
        xn = xn_ref[...]
        a = _dot(xn, wa_buf[slot].astype(bf16))
        b = _dot(xn, wb_buf[slot].astype(bf16))
        act = (a / (1.0 + jnp.exp(-a)) * b).astype(bf16)
        o_ref[...] += 0.5 * _dot(act, wo_buf[slot].astype(bf16))
        return carry

    lax.fori_loop(0, nj, body, 0)


def _ffn(h, g, wi, wo, layer, *, tm, fc, mix=None, wout=None, n_rows=None, tail=None):
    d = h.shape[1]
    tail_rows = 0 if tail is None else tail.shape[0]
    n = h.shape[0] + tail_rows if n_rows is None else n_rows
    assert n % tm == 0 and 0 <= tail_rows < tm and not (tail_rows and (mix is not None or n_rows is not None))
    dff = wo.shape[1]
    nj = dff // fc
    fuse = mix is not None
    row = lambda i: (i, 0)
    in_specs = [pl.BlockSpec((tm, d), row)]
    args = [h]
    if tail_rows:
        in_specs += [pl.BlockSpec(tail.shape, lambda i: (0, 0))]
        args += [tail]
    if fuse:
        in_specs += [pl.BlockSpec((tm, mix.shape[1]), row), pl.BlockSpec(wout.shape, lambda i: (0, 0))]
        args += [mix, wout]
    in_specs += [
        pl.BlockSpec((1, d), lambda i: (0, 0)),
        pl.BlockSpec(memory_space=pl.ANY),
        pl.BlockSpec(memory_space=pl.ANY),
    ]
    args += [g.reshape(1, d), wi, wo]
    return pl.pallas_call(
        functools.partial(_ffn_kernel, fuse_out=fuse, tail_rows=tail_rows, layer=layer, fc=fc, nj=nj),
        grid=(n // tm,),
        in_specs=in_specs,
        out_specs=pl.BlockSpec((tm, d), row),
        out_shape=jax.ShapeDtypeStruct((n, d), f32),
        scratch_shapes=[
            pltpu.VMEM((tm, d), bf16),
            pltpu.VMEM((2, d, fc), f32),
            pltpu.VMEM((2, d, fc), f32),
            pltpu.VMEM((2, fc, d), f32),
            pltpu.SemaphoreType.DMA((3, 2)),
        ],
        compiler_params=_cparams(("arbitrary",)),
        name="ffn_out" if fuse else "ffn",
    )(*args)


def _rel_bucket(rel):
    half = REL_BUCKETS // 2
    max_exact = half // 2
    n = jnp.abs(rel)
    large = max_exact + (jnp.log(jnp.maximum(n, 1).astype(jnp.float32) / max_exact)
                         / math.log(REL_MAX_DIST / max_exact) * (half - max_exact)).astype(jnp.int32)
    large = jnp.minimum(large, half - 1)
    return jnp.where(rel > 0, half, 0) + jnp.where(n < max_exact, n, large)


def _rel_tiles():
    k = np.arange(LANES)[:, None]
    q = np.arange(LANES)[None, :]
    far = np.full((LANES, LANES), -4 * LANES)
    ones = np.ones((LANES, LANES), bool)
    rels = [k - q, k - q - LANES, far, (k % N_META) - N_META - q, (k % N_META) - (q % N_META), far, k - q - LANES]
    vis = [(k // CHUNK) <= (q // CHUNK), ones, ones, ones, (k // N_META) == (q // N_META), ~ones,
           (q < CHUNK) | (k >= CHUNK)]
    return (np.stack([np.broadcast_to(a, (LANES, LANES)) for a in rels]).astype(np.int32),
            np.stack([np.broadcast_to(a, (LANES, LANES)) for a in vis]).astype(np.int32))


def _bias_kernel(rb_ref, bucket_ref, vis_ref, o_ref):
    h = pl.program_id(0)
    for kind in range(N_KINDS):
        bk = bucket_ref[kind]
        acc = jnp.zeros((LANES, LANES), f32)
        for b in range(REL_BUCKETS):
            acc = jnp.where(bk == b, rb_ref[b, h], acc)
        o_ref[kind, 0] = jnp.where(vis_ref[kind] != 0, acc * LOG2E, NEG_INF)


def _bias_tiles(rel_bias):
    rel, vis = _rel_tiles()
    bucket = _rel_bucket(jnp.asarray(rel))
    nk = N_KINDS
    return pl.pallas_call(
        _bias_kernel,
        grid=(REL_HEADS,),
        in_specs=[
            pl.BlockSpec(memory_space=pltpu.SMEM),
            pl.BlockSpec((nk, LANES, LANES), lambda h: (0, 0, 0)),
            pl.BlockSpec((nk, LANES, LANES), lambda h: (0, 0, 0)),
        ],
        out_specs=pl.BlockSpec((nk, 1, LANES, LANES), lambda h: (0, h, 0, 0)),
        out_shape=jax.ShapeDtypeStruct((nk, REL_HEADS, LANES, LANES), f32),
        compiler_params=_cparams(("arbitrary",)),
        name="bias_tiles",
    )(rel_bias, bucket, jnp.asarray(vis))


def _bias_spec():
    return pl.BlockSpec((N_KINDS, REL_HEADS, LANES, LANES), lambda s: (0, 0, 0, 0), pipeline_mode=pl.Buffered(1))


def _meta_views(rows, cols_t, bsz):
    f = rows.shape[1]
    return (rows.reshape(bsz, N_META, f),
            cols_t.reshape(cols_t.shape[0], bsz, N_META).transpose(1, 0, 2))


A_QD = A_HEADS * 2 * A_HD
A_VR = A_VD + ONES_ROWS


def _proj_a_kernel(h_ref, g_ref, w_ref, gq_ref, gk_ref, qs_ref, k_ref, vt_ref):
    xn = (_rms_rows(h_ref[...]) * g_ref[...]).astype(bf16)
    y = _dot(xn, w_ref[...])
    q = _group_rms(y[:, :A_QD], A_HD) * gq_ref[...]
    lo = _lo_half_mask(q.shape, 2 * A_HD, A_HD)
    q_lo = jnp.where(lo, q, 0.0).astype(bf16)
    q_hi = jnp.where(lo, 0.0, q).astype(bf16)
    k_ref[...] = (_group_rms(y[:, A_QD:2 * A_QD], A_HD) * gk_ref[...]).astype(bf16)
    ones = jnp.ones((ONES_ROWS, LANES), bf16)
    for t in range(vt_ref.shape[0]):
        rows = slice(t * LANES, (t + 1) * LANES)
        vt = y[rows, 2 * A_QD:].T.astype(bf16)
        for h in range(A_HEADS):
            qs_ref[t, h, :LANES, :] = q_lo[rows, h * A_VD:(h + 1) * A_VD]
            qs_ref[t, h, LANES:, :] = q_hi[rows, h * A_VD:(h + 1) * A_VD]
            vt_ref[t, h * A_VR:h * A_VR + A_VD, :] = vt[h * A_VD:(h + 1) * A_VD]
            vt_ref[t, h * A_VR + A_VD:(h + 1) * A_VR, :] = ones


def _proj_a(h, g, w, qk_norm, *, tm):
    n, d = h.shape
    nw = w.shape[1]
    gq = jnp.tile(qk_norm[0] * (A_HD ** -0.5 * LOG2E), A_QD // A_HD).reshape(1, A_QD)
    gk = jnp.tile(qk_norm[1], A_QD // A_HD).reshape(1, A_QD)
    return pl.pallas_call(
        _proj_a_kernel,
        grid=(n // tm,),
        in_specs=[
            pl.BlockSpec((tm, d), lambda i: (i, 0)),
            pl.BlockSpec((1, d), lambda i: (0, 0)),
            pl.BlockSpec((d, nw), lambda i: (0, 0)),
            pl.BlockSpec((1, A_QD), lambda i: (0, 0)),
            pl.BlockSpec((1, A_QD), lambda i: (0, 0)),
        ],
        out_specs=[
            pl.BlockSpec((tm // LANES, A_HEADS, 2 * LANES, A_VD), lambda i: (i, 0, 0, 0)),
            pl.BlockSpec((tm, A_QD), lambda i: (i, 0)),
            pl.BlockSpec((tm // LANES, A_HEADS * A_VR, LANES), lambda i: (i, 0, 0)),
        ],
        out_shape=[
            jax.ShapeDtypeStruct((n // LANES, A_HEADS, 2 * LANES, A_VD), bf16),
            jax.ShapeDtypeStruct((n, A_QD), bf16),
            jax.ShapeDtypeStruct((n // LANES, A_HEADS * A_VR, LANES), bf16),
        ],
        compiler_params=_cparams(("arbitrary",)),
        name="proj_a",
    )(h, g.reshape(1, d), w, gq, gk)


def _attn_a_kernel(qs_ref, k_ref, vt_ref, km_ref, vtm_ref, kmf_ref, vtmf_ref, bias_ref, lam_ref, sub_ref,
                   o_ref, m_ref, acc_ref, *, nblk, lambda_init):
    s_id = pl.program_id(0)
    nb_real = pl.num_programs(0) - 1
    hw = 2 * A_HD

    def init():
        m_ref[...] = jnp.full(m_ref.shape, NEG_INF, f32)
        acc_ref[...] = jnp.zeros(acc_ref.shape, f32)

    def step_fns(kt_fn, vtt_fns, bias_fn):
        def logits():
            sts = [_dot_nt(kt_fn(h), qs_ref[0, h]) for h in range(A_HEADS)]
            if bias_fn is not None:
                sts = [st + bias_fn(h) for h, st in enumerate(sts)]
            return jnp.concatenate(sts, axis=1)

        def pv(pb):
            outs = []
            for h in range(A_HEADS):
                acc = None
                for vtt_fn, r0 in vtt_fns:
                    vtt = vtt_fn(h)
                    part = _dot(vtt, pb[r0:r0 + vtt.shape[1], h * 2 * LANES:(h + 1) * 2 * LANES])
                    acc = part if acc is None else acc + part
                outs.append(acc)
            return jnp.concatenate(outs, axis=1)

        return logits, pv

    def bias_rows(kind, h, rows):
        return jnp.concatenate([bias_ref[kind, h, 0:rows, :], bias_ref[kind, A_HEADS + h, 0:rows, :]], axis=1)

    def finalize():
        lam = lam_ref[...]
        lam_full = (jnp.exp(jnp.sum(lam[0:1] * lam[1:2], axis=-1, keepdims=True))
                    - jnp.exp(jnp.sum(lam[2:3] * lam[3:4], axis=-1, keepdims=True)) + lambda_init)
        o = acc_ref[0:A_VD, :] * (1.0 / acc_ref[A_VD:A_VD + 1, :])
        for h in range(A_HEADS):
            d = o[:, 2 * h * LANES:(2 * h + 1) * LANES] - lam_full * o[:, (2 * h + 1) * LANES:(2 * h + 2) * LANES]
            d = d * lax.rsqrt(jnp.mean(d * d, axis=0, keepdims=True) + EPS) * sub_ref[...] * (1.0 - lambda_init)
            o_ref[:, h * hw:(h + 1) * hw] = d.T.astype(bf16)

    @pl.when(s_id < nb_real)
    def _():
        i = s_id % nblk
        kind_m = jnp.where(i == 0, KIND_META0, KIND_FAR)
        nfar, near0, nnear = _sweep_steps(i)

        def k_fn(b0, nsub):
            off = pl.multiple_of(b0 * LANES, NSUB * LANES)
            return lambda h: k_ref[pl.ds(off, nsub * LANES), h * hw:(h + 1) * hw]

        def vt_fn(b0, nsub):
            return lambda h: jnp.concatenate([vt_ref[b0 + t, h * A_VR:(h + 1) * A_VR, :] for t in range(nsub)], axis=1)

        def near_bias(b0):
            kinds = [_block_kind(b0 + t - i) for t in range(NSUB)]
            return [lambda h, kind=kind: bias_rows(kind, h, LANES) for kind in kinds]

        def first_fns():
            biases = [lambda h: bias_rows(kind_m, h, N_META)] + near_bias(0)
            return step_fns(
                lambda h: jnp.concatenate([km_ref[0, :, h * hw:(h + 1) * hw], k_fn(0, NSUB)(h)], axis=0),
                [(lambda h: vtm_ref[0, h * A_VR:(h + 1) * A_VR, :], 0), (vt_fn(0, NSUB), N_META)],
                lambda h: jnp.concatenate([b(h) for b in biases], axis=0))

        def far_step(w):
            b0 = NSUB + w * NSUB_FAR
            far_bias = jnp.concatenate([bias_rows(KIND_FAR, h, 1) for h in range(A_HEADS)], axis=1)
            return step_fns(k_fn(b0, NSUB_FAR), [(vt_fn(b0, NSUB_FAR), 0)], None) + (far_bias,)

        def near_step(u):
            b0 = near0 + u * NSUB
            biases = near_bias(b0)
            return step_fns(k_fn(b0, NSUB), [(vt_fn(b0, NSUB), 0)],
                            lambda h: jnp.concatenate([b(h) for b in biases], axis=0)) + (None,)

        def sweep(lazy):
            init()
            _softmax_step(*first_fns(), m_ref, acc_ref, lazy)
            _softmax_loop(0, nfar, far_step, m_ref, acc_ref, lazy)
            _softmax_loop(0, nnear, near_step, m_ref, acc_ref, lazy)

        sweep(True)
        pl.when(_lazy_failed(acc_ref[A_VD:A_VD + 1, :]))(lambda: sweep(False))
        finalize()

    @pl.when(s_id == nb_real)
    def _():
        init()
        _softmax_step(*step_fns(lambda h: kmf_ref[:, h * hw:(h + 1) * hw],
                                [(lambda h: vtmf_ref[0, h * A_VR:(h + 1) * A_VR, :], 0)],
                                lambda h: bias_rows(KIND_METAMETA, h, LANES)), m_ref, acc_ref)
        finalize()


def _attn_a(qs, k, vt, bias, lam, subln, *, bsz, nblk, lambda_init):
    n = k.shape[0]
    nb_real = bsz * nblk
    seq = nblk * LANES
    d = A_QD
    vr = A_HEADS * A_VR
    assert nblk % NSUB == 0
    km, vtm = _meta_views(k[nb_real * LANES:], vt[nb_real], bsz)
    kern = functools.partial(_attn_a_kernel, nblk=nblk, lambda_init=lambda_init)
    bclamp = lambda s: jnp.minimum(s // nblk, bsz - 1)
    return pl.pallas_call(
        kern,
        grid=(nb_real + 1,),
        in_specs=[
            pl.BlockSpec((1,) + qs.shape[1:], lambda s: (s, 0, 0, 0)),
            pl.BlockSpec((seq, d), lambda s: (bclamp(s), 0)),
            pl.BlockSpec((nblk, vr, LANES), lambda s: (bclamp(s), 0, 0)),
            pl.BlockSpec((1, N_META, d), lambda s: (bclamp(s), 0, 0)),
            pl.BlockSpec((1, vr, N_META), lambda s: (bclamp(s), 0, 0)),
            pl.BlockSpec((LANES, d), lambda s: (nb_real, 0)),
            pl.BlockSpec((1, vr, LANES), lambda s: (nb_real, 0, 0)),
            _bias_spec(),
            pl.BlockSpec((4, A_HD), lambda s: (0, 0)),
            pl.BlockSpec((A_VD, LANES), lambda s: (0, 0)),
        ],
        out_specs=pl.BlockSpec((LANES, d), lambda s: (s, 0)),
        out_shape=jax.ShapeDtypeStruct((n, d), bf16),
        scratch_shapes=[
            pltpu.VMEM((1, A_HEADS * 2 * LANES), f32),
            pltpu.VMEM((A_VR, A_HEADS * 2 * LANES), f32),
        ],
        compiler_params=_cparams(("arbitrary",)),
        name="attn_a",
    )(qs, k, vt, km, vtm, k, vt, bias, lam, jnp.broadcast_to(subln[:, None], (A_VD, LANES)))


C_QD = C_Q_HEADS * C_HD
C_KD = 2 * C_KV_HEADS * C_HD
C_VR = 2 * C_HD + ONES_ROWS


def _proj_c_kernel(h_ref, g_ref, w_ref, gq_ref, gk_ref, qs_ref, k_ref, vt_ref):
    xn = (_rms_rows(h_ref[...]) * g_ref[...]).astype(bf16)
    y = _dot(xn, w_ref[...])
    q = _group_rms(y[:, :C_QD], C_HD) * gq_ref[...]
    lo = _lo_half_mask(q.shape, 2 * C_HD, C_HD)
    q_even = jnp.where(lo, q, 0.0).astype(bf16)
    q_odd = jnp.where(lo, 0.0, q).astype(bf16)
    k_ref[...] = (_group_rms(y[:, C_QD:C_QD + C_KD], C_HD) * gk_ref[...]).astype(bf16)
    ones = jnp.ones((ONES_ROWS, LANES), bf16)
    for t in range(vt_ref.shape[0]):
        rows = slice(t * LANES, (t + 1) * LANES)
        vt = y[rows, C_QD + C_KD:].T.astype(bf16)
        for g in range(C_KV_HEADS):
            for hh in range(C_GROUP):
                pair = (g * C_GROUP + hh) // 2
                src = q_even if hh % 2 == 0 else q_odd
                qs_ref[t, g, hh * LANES:(hh + 1) * LANES, :] = src[rows, pair * LANES:(pair + 1) * LANES]
            vt_ref[t, g * C_VR:g * C_VR + 2 * C_HD, :] = vt[g * 2 * C_HD:(g + 1) * 2 * C_HD]
            vt_ref[t, g * C_VR + 2 * C_HD:(g + 1) * C_VR, :] = ones


def _proj_c(h, g, w, qk_norm, *, tm):
    n, d = h.shape
    nw = w.shape[1]
    nt = tm // LANES
    gq = jnp.tile(qk_norm[0] * (C_HD ** -0.5 * LOG2E), C_QD // C_HD).reshape(1, C_QD)
    gk = jnp.tile(qk_norm[1], C_KD // C_HD).reshape(1, C_KD)
    return pl.pallas_call(
        _proj_c_kernel,
        grid=(n // tm,),
        in_specs=[
            pl.BlockSpec((tm, d), lambda i: (i, 0)),
            pl.BlockSpec((1, d), lambda i: (0, 0)),
            pl.BlockSpec((d, nw), lambda i: (0, 0)),
            pl.BlockSpec((1, C_QD), lambda i: (0, 0)),
            pl.BlockSpec((1, C_KD), lambda i: (0, 0)),
        ],
        out_specs=[
            pl.BlockSpec((nt, C_KV_HEADS, C_GROUP * LANES, LANES), lambda i: (i, 0, 0, 0)),
            pl.BlockSpec((tm, C_KD), lambda i: (i, 0)),
            pl.BlockSpec((nt, C_KV_HEADS * C_VR, LANES), lambda i: (i, 0, 0)),
        ],
        out_shape=[
            jax.ShapeDtypeStruct((n // LANES, C_KV_HEADS, C_GROUP * LANES, LANES), bf16),
            jax.ShapeDtypeStruct((n, C_KD), bf16),
            jax.ShapeDtypeStruct((n // LANES, C_KV_HEADS * C_VR, LANES), bf16),
        ],
        compiler_params=_cparams(("arbitrary",)),
        name="proj_c",
    )(h, g.reshape(1, d), w, gq, gk)


def _attn_c_kernel(sink_ref, qs_ref, k_ref, vt_ref, km_ref, vtm_ref, kmf_ref, vtmf_ref, bias_ref, o_ref, *, nblk):
    s_id = pl.program_id(0)
    nb_real = pl.num_programs(0) - 1
    r_io, _ = _tile_iotas()
    vd = 2 * C_HD

    def attend_with(tiles, lazy):
        top = r_io < C_HD
        failed = None
        for g in range(C_KV_HEADS):
            sink = jnp.concatenate(
                [jnp.full((1, LANES), sink_ref[g * C_GROUP + hh] * LOG2E, f32) for hh in range(C_GROUP)], axis=1)
            sts = []
            m = sink
            for (k_fn, vt_fn, bias_fn) in tiles:
                st = _dot_nt(k_fn(g), qs_ref[0, g])
                st = st + jnp.concatenate([bias_fn(g * C_GROUP + hh) for hh in range(C_GROUP)], axis=1)
                if not lazy:
                    m = jnp.maximum(m, jnp.max(st, axis=0, keepdims=True))
                sts.append(st)
            acc = None
            for st, (k_fn, vt_fn, bias_fn) in zip(sts, tiles):
                pv = _dot(vt_fn(g), jnp.exp2(st if lazy else st - m).astype(bf16))
                acc = pv if acc is None else acc + pv
            denom = acc[vd:vd + 1, :] + jnp.exp2(sink if lazy else sink - m)
            if lazy:
                failed = _lazy_failed(denom) if failed is None else failed | _lazy_failed(denom)
            o = acc[0:vd, :] * (1.0 / denom)
            for cc in range(C_GROUP // 2):
                even = o[:, (2 * cc) * LANES:(2 * cc + 1) * LANES]
                odd = o[:, (2 * cc + 1) * LANES:(2 * cc + 2) * LANES]
                col = (g * (C_GROUP // 2) + cc) * LANES
                o_ref[:, col:col + LANES] = jnp.where(top, even, odd).T.astype(bf16)
        return failed

    def attend(tiles):
        failed = attend_with(tiles, True)

        @pl.when(failed)
        def _():
            attend_with(tiles, False)

    @pl.when(s_id < nb_real)
    def _():
        i = s_id % nblk
        prev = jnp.maximum(i - 1, 0)
        poff = pl.multiple_of(prev * LANES, LANES)
        coff = pl.multiple_of(i * LANES, LANES)
        kind_m = jnp.where(i == 0, KIND_META0, KIND_FAR)
        kind_p = jnp.where(i == 0, KIND_MASKED, KIND_PREVWIN)
        attend([
            (lambda g: km_ref[0, :, g * LANES:(g + 1) * LANES], lambda g: vtm_ref[0, g * C_VR:(g + 1) * C_VR, :],
             lambda h: bias_ref[kind_m, h, 0:N_META, :]),
            (lambda g: k_ref[pl.ds(poff, LANES), g * LANES:(g + 1) * LANES],
             lambda g: vt_ref[prev, g * C_VR:(g + 1) * C_VR, :], lambda h: bias_ref[kind_p, h]),
            (lambda g: k_ref[pl.ds(coff, LANES), g * LANES:(g + 1) * LANES],
             lambda g: vt_ref[i, g * C_VR:(g + 1) * C_VR, :], lambda h: bias_ref[KIND_DIAG, h]),
        ])

    @pl.when(s_id == nb_real)
    def _():
        attend([(lambda g: kmf_ref[:, g * LANES:(g + 1) * LANES], lambda g: vtmf_ref[0, g * C_VR:(g + 1) * C_VR, :],
                 lambda h: bias_ref[KIND_METAMETA, h])])


def _attn_c(qs, k, vt, bias, sinks, *, bsz, nblk):
    n = k.shape[0]
    nb_real = bsz * nblk
    seq = nblk * LANES
    vr = C_KV_HEADS * C_VR
    kern = functools.partial(_attn_c_kernel, nblk=nblk)
    bclamp = lambda s: jnp.minimum(s // nblk, bsz - 1)
    km, vtm = _meta_views(k[nb_real * LANES:], vt[nb_real], bsz)
    return pl.pallas_call(
        kern,
        grid=(nb_real + 1,),
        in_specs=[
            pl.BlockSpec(memory_space=pltpu.SMEM),
            pl.BlockSpec((1,) + qs.shape[1:], lambda s: (s, 0, 0, 0)),
            pl.BlockSpec((seq, C_KD), lambda s: (bclamp(s), 0)),
            pl.BlockSpec((nblk, vr, LANES), lambda s: (bclamp(s), 0, 0)),
            pl.BlockSpec((1, N_META, C_KD), lambda s: (bclamp(s), 0, 0)),
            pl.BlockSpec((1, vr, N_META), lambda s: (bclamp(s), 0, 0)),
            pl.BlockSpec((LANES, C_KD), lambda s: (nb_real, 0)),
            pl.BlockSpec((1, vr, LANES), lambda s: (nb_real, 0, 0)),
            _bias_spec(),
        ],
        out_specs=pl.BlockSpec((LANES, C_QD), lambda s: (s, 0)),
        out_shape=jax.ShapeDtypeStruct((n, C_QD), bf16),
        compiler_params=_cparams(("arbitrary",)),
        name="attn_c",
    )(sinks, qs, k, vt, km, vtm, k, vt, bias)


B_QA = B_HEADS * B_KV_RANK
B_QI = IDX_HEADS * IDX_DIM
B_W1 = 2 * B_Q_RANK + 2 * LANES
B_TR = B_KV_RANK + ONES_ROWS


def _proj_b_kernel(h_ref, g_ref, w1_ref, ln_ref, wuq_ref, qn_ref,
                   qa_ref, qi_ref, ckv_ref, ckvt_ref, kk_ref, wit_ref):
    xn = (_rms_rows(h_ref[...]) * g_ref[...]).astype(bf16)
    y = _dot(xn, w1_ref[...])
    r = B_Q_RANK
    cq = (_rms_rows(y[:, :r]) * ln_ref[0:1, :]).astype(bf16)
    ckv = _rms_rows(y[:, r:2 * r]) * ln_ref[1:2, :]
    ckv_ref[...] = ckv.astype(bf16)
    kk_ref[...] = _rms_rows(y[:, 2 * r:2 * r + LANES]).astype(bf16)
    wi = y[:, 2 * r + LANES:] * (IDX_HEADS ** -0.5)
    ones = jnp.ones((ONES_ROWS, LANES), bf16)
    for t in range(ckvt_ref.shape[0]):
        ckvt_ref[t, 0:r, :] = ckv[t * LANES:(t + 1) * LANES, :].T.astype(bf16)
        ckvt_ref[t, r:, :] = ones
        wit_ref[t] = wi[t * LANES:(t + 1) * LANES, :].T[0:IDX_HEADS, :]
    z = _dot(cq, wuq_ref[...])
    qa = (_group_rms(z[:, :B_QA], B_KV_RANK) * qn_ref[...]).astype(bf16)
    qi = z[:, B_QA:] * (IDX_DIM ** -0.5)
    lo = _lo_half_mask(qi.shape, 2 * IDX_DIM, IDX_DIM)
    qi_lo = jnp.where(lo, qi, 0.0).astype(bf16)
    qi_hi = jnp.where(lo, 0.0, qi).astype(bf16)
    for t in range(qa_ref.shape[0]):
        rows = slice(t * LANES, (t + 1) * LANES)
        for hd in range(B_HEADS):
            qa_ref[t, hd * LANES:(hd + 1) * LANES, :] = qa[rows, hd * r:(hd + 1) * r]
        for hh in range(IDX_HEADS):
            src = qi_lo if hh % 2 == 0 else qi_hi
            qi_ref[t, hh * LANES:(hh + 1) * LANES, :] = src[rows, (hh // 2) * LANES:(hh // 2 + 1) * LANES]


def _proj_b(h, g, w1, latent_norm, wuq, q_norm, *, tm):
    n, d = h.shape
    qn = jnp.tile(q_norm * (B_KV_RANK ** -0.5 * LOG2E), B_HEADS).reshape(1, B_QA)
    row = lambda i: (i, 0)
    row3 = lambda i: (i, 0, 0)
    const = lambda i: (0, 0)
    nt = tm // LANES
    return pl.pallas_call(
        _proj_b_kernel,
        grid=(n // tm,),
        in_specs=[
            pl.BlockSpec((tm, d), row),
            pl.BlockSpec((1, d), const),
            pl.BlockSpec(w1.shape, const),
            pl.BlockSpec(latent_norm.shape, const),
            pl.BlockSpec(wuq.shape, const),
            pl.BlockSpec((1, B_QA), const),
        ],
        out_specs=[
            pl.BlockSpec((nt, B_HEADS * LANES, B_KV_RANK), row3),
            pl.BlockSpec((nt, IDX_HEADS * LANES, LANES), row3),
            pl.BlockSpec((tm, B_KV_RANK), row),
            pl.BlockSpec((nt, B_TR, LANES), row3),
            pl.BlockSpec((tm, LANES), row),
            pl.BlockSpec((nt, IDX_HEADS, LANES), row3),
        ],
        out_shape=[
            jax.ShapeDtypeStruct((n // LANES, B_HEADS * LANES, B_KV_RANK), bf16),
            jax.ShapeDtypeStruct((n // LANES, IDX_HEADS * LANES, LANES), bf16),
            jax.ShapeDtypeStruct((n, B_KV_RANK), bf16),
            jax.ShapeDtypeStruct((n // LANES, B_TR, LANES), bf16),
            jax.ShapeDtypeStruct((n, LANES), bf16),
            jax.ShapeDtypeStruct((n // LANES, IDX_HEADS, LANES), f32),
        ],
        compiler_params=_cparams(("arbitrary",)),
        name="proj_b",
    )(h, g.reshape(1, d), w1, latent_norm, wuq, qn)


def _attn_b_kernel(qs_ref, is_ref, wit_ref, ckv_ref, ckvt_ref, kk_ref, ckvm_ref, ckvtm_ref, kkm_ref,
                   ckvmf_ref, ckvtmf_ref, bias_ref, wuvt_ref,
                   o_ref, key_ref, pen_ref, m_ref, acc_ref, *, nblk, k_sel):
    s_id = pl.program_id(0)
    nb_real = pl.num_programs(0) - 1
    r_io, c_io = _tile_iotas()
    rk = B_KV_RANK

    def init():
        m_ref[...] = jnp.full(m_ref.shape, NEG_INF, f32)
        acc_ref[...] = jnp.zeros(acc_ref.shape, f32)

    def add_per_head(st, bias_fn, pen):
        cols = []
        for h in range(B_HEADS):
            add = pen if bias_fn is None else (bias_fn(h) if pen is None else bias_fn(h) + pen)
            cols.append(st[:, h * LANES:(h + 1) * LANES] + add)
        return jnp.concatenate(cols, axis=1)

    def step_fns(ckv_fn, ckvt_fns, bias_fn, pen_fn):
        def logits():
            st = _dot_nt(ckv_fn(), qs_ref[0])
            return add_per_head(st, bias_fn, None if pen_fn is None else pen_fn())

        def pv(pb):
            acc = None
            for ckvt_fn, r0 in ckvt_fns:
                ckvt = ckvt_fn()
                part = _dot(ckvt, pb[r0:r0 + ckvt.shape[1], :])
                acc = part if acc is None else acc + part
            return acc

        return logits, pv

    def finalize():
        olat = (acc_ref[0:rk, :] * (1.0 / acc_ref[rk:rk + 1, :])).astype(bf16)
        ot = jnp.concatenate([_dot(wuvt_ref[h], olat[:, h * LANES:(h + 1) * LANES]) for h in range(B_HEADS)], axis=0)
        o_ref[...] = ot.T.astype(bf16)

    def index_scores(kk):
        s = jnp.maximum(_dot_nt(kk, is_ref[0]), 0.0)
        wt = wit_ref[0]
        sc = jnp.zeros((kk.shape[0], LANES), f32)
        for hh in range(IDX_HEADS):
            sc = sc + wt[hh:hh + 1, :] * s[:, hh * LANES:(hh + 1) * LANES]
        return sc

    def sort_key(sc):
        bits = lax.bitcast_convert_type(sc + 0.0, jnp.int32)
        return jnp.where(bits < 0, bits ^ jnp.int32(0x7FFFFFFF), bits)

    @pl.when(s_id < nb_real)
    def _():
        i = s_id % nblk
        ntile = i + 2

        int_min_tile = jnp.full((LANES, LANES), INT_MIN, jnp.int32)
        key_ref[0] = int_min_tile
        key_ref[0, 0:N_META, :] = sort_key(index_scores(kkm_ref[0]))
        key_ref[i + 2] = int_min_tile

        def score_body(jp, carry):
            off = pl.multiple_of(jp * (2 * LANES), 2 * LANES)
            keys = sort_key(index_scores(kk_ref[pl.ds(off, 2 * LANES), :]))
            for t in range(2):
                j = _vec(2 * jp + t)
                vis = (j < i) | ((j == i) & ((r_io >> 6) <= (c_io >> 6)))
                key_ref[2 * jp + t + 1] = jnp.where(vis, keys[t * LANES:(t + 1) * LANES], jnp.int32(INT_MIN))
            return carry

        lax.fori_loop(0, i // 2 + 1, score_body, 0)

        def count(pred):
            def cbody(tp, accv):
                for t in (2 * tp, 2 * tp + 1):
                    accv = accv + jnp.where(pred(key_ref[t], t), 1.0, 0.0)
                return accv
            accv = lax.fori_loop(0, (ntile + 1) // 2, cbody, jnp.zeros((LANES, LANES), f32))
            return jnp.sum(accv, axis=0, keepdims=True)

        kf = float(k_sel)
        zero = jnp.zeros((1, LANES), jnp.int32)
        t0 = jnp.where(count(lambda k, t: k >= zero) >= kf, zero, jnp.int32(INT_MIN))

        def bit_body(it, tcur):
            cand = tcur | jnp.left_shift(jnp.int32(1), 30 - it)
            return jnp.where(count(lambda k, t: k >= cand) >= kf, cand, tcur)

        thr = lax.fori_loop(0, 31, bit_body, t0)

        need = kf - count(lambda k, t: k > thr)
        n_eq = count(lambda k, t: k == thr)
        has_thr = thr > jnp.int32(INT_MIN)
        tied = jnp.max(jnp.where(has_thr & (n_eq > need), 1.0, 0.0)) > 0.0

        def tie_search(_):
            def jbody(it, jcur):
                cand = jcur | jnp.left_shift(jnp.int32(1), 11 - it)
                cnt = count(lambda k, t: (k == thr) & ((t * LANES + r_io) < cand))
                return jnp.where(cnt < need, cand, jcur)
            return lax.fori_loop(0, 12, jbody, jnp.zeros((1, LANES), jnp.int32))

        j_last = lax.cond(tied, tie_search, lambda _: jnp.full((1, LANES), 4095, jnp.int32), 0)
        j_last = jnp.where(has_thr, j_last, -1)

        def pen_body(t, carry):
            k = key_ref[t]
            sel = (k > thr) | ((k == thr) & ((t * LANES + r_io) <= j_last))
            pen_ref[t] = jnp.where(sel, 0.0, NEG_INF)
            return carry

        lax.fori_loop(0, ntile, pen_body, 0)
        for t in range(1, NSUB):
            pen_ref[i + 1 + t] = jnp.full((LANES, LANES), NEG_INF, f32)

        kind_m = jnp.where(i == 0, KIND_META0, KIND_FAR)
        nfar, near0, nnear = _sweep_steps(i)

        def ckv_fn(b0, nsub):
            off = pl.multiple_of(b0 * LANES, NSUB * LANES)
            return lambda: ckv_ref[pl.ds(off, nsub * LANES), :]

        def ckvt_fn(b0, nsub):
            return lambda: jnp.concatenate([ckvt_ref[b0 + t] for t in range(nsub)], axis=1)

        def pen_fn(b0, nsub):
            return lambda: jnp.concatenate([pen_ref[b0 + t + 1] for t in range(nsub)], axis=0)

        def near_bias(b0):
            kinds = [_block_kind(b0 + t - i) for t in range(NSUB)]
            return lambda h: jnp.concatenate([bias_ref[kinds[t], h] for t in range(NSUB)], axis=0)

        def first_pen():
            return jnp.concatenate([pen_ref[0, 0:N_META, :], pen_fn(0, NSUB)()], axis=0)

        def first_fns():
            bias01 = near_bias(0)
            return step_fns(lambda: jnp.concatenate([ckvm_ref[0], ckv_fn(0, NSUB)()], axis=0),
                            [(lambda: ckvtm_ref[0], 0), (ckvt_fn(0, NSUB), N_META)],
                            lambda h: jnp.concatenate([bias_ref[kind_m, h, 0:N_META, :], bias01(h)], axis=0),
                            first_pen)

        def far_step(w):
            b0 = NSUB + w * NSUB_FAR
            far_bias = jnp.concatenate([bias_ref[KIND_FAR, h, 0:1, :] for h in range(B_HEADS)], axis=1)
            return step_fns(ckv_fn(b0, NSUB_FAR), [(ckvt_fn(b0, NSUB_FAR), 0)], None, pen_fn(b0, NSUB_FAR)) + (far_bias,)

        def near_step(u):
            b0 = near0 + u * NSUB
            return step_fns(ckv_fn(b0, NSUB), [(ckvt_fn(b0, NSUB), 0)], near_bias(b0), pen_fn(b0, NSUB)) + (None,)

        def sweep(lazy):
            init()
            _softmax_step(*first_fns(), m_ref, acc_ref, lazy)
            _softmax_loop(0, nfar, far_step, m_ref, acc_ref, lazy)
            _softmax_loop(0, nnear, near_step, m_ref, acc_ref, lazy)

        sweep(True)
        pl.when(_lazy_failed(acc_ref[rk:rk + 1, :]))(lambda: sweep(False))
        finalize()

    @pl.when(s_id == nb_real)
    def _():
        init()
        _softmax_step(*step_fns(lambda: ckvmf_ref[...], [(lambda: ckvtmf_ref[0], 0)],
                                lambda h: bias_ref[KIND_METAMETA, h], None), m_ref, acc_ref)
        finalize()


def _attn_b(qa, qi, wit, ckv, ckvt, kk, bias, wuvt, *, bsz, nblk, k_sel):
    n = ckv.shape[0]
    nb_real = bsz * nblk
    seq = nblk * LANES
    assert k_sel >= N_META and (nblk + 1) * LANES <= 4096 and nblk % NSUB == 0
    ckvm, ckvtm = _meta_views(ckv[nb_real * LANES:], ckvt[nb_real], bsz)
    kkm = kk[nb_real * LANES:].reshape(bsz, N_META, LANES)
    kern = functools.partial(_attn_b_kernel, nblk=nblk, k_sel=k_sel)
    bidx = lambda s: jnp.minimum(s // nblk, bsz - 1)
    blk = lambda s: (s, 0)
    return pl.pallas_call(
        kern,
        grid=(nb_real + 1,),
        in_specs=[
            pl.BlockSpec((1,) + qa.shape[1:], lambda s: (s, 0, 0)),
            pl.BlockSpec((1,) + qi.shape[1:], lambda s: (s, 0, 0)),
            pl.BlockSpec((1, IDX_HEADS, LANES), lambda s: (s, 0, 0)),
            pl.BlockSpec((seq, B_KV_RANK), lambda s: (bidx(s), 0)),
            pl.BlockSpec((nblk, B_TR, LANES), lambda s: (bidx(s), 0, 0)),
            pl.BlockSpec((seq, LANES), lambda s: (bidx(s), 0)),
            pl.BlockSpec((1, N_META, B_KV_RANK), lambda s: (bidx(s), 0, 0)),
            pl.BlockSpec((1, B_TR, N_META), lambda s: (bidx(s), 0, 0)),
            pl.BlockSpec((1, N_META, LANES), lambda s: (bidx(s), 0, 0)),
            pl.BlockSpec((LANES, B_KV_RANK), lambda s: (nb_real, 0)),
            pl.BlockSpec((1, B_TR, LANES), lambda s: (nb_real, 0, 0)),
            _bias_spec(),
            pl.BlockSpec(wuvt.shape, lambda s: (0, 0, 0)),
        ],
        out_specs=pl.BlockSpec((LANES, B_HEADS * B_VD), blk),
        out_shape=jax.ShapeDtypeStruct((n, B_HEADS * B_VD), bf16),
        scratch_shapes=[
            pltpu.VMEM((nblk + 2, LANES, LANES), jnp.int32),
            pltpu.VMEM((nblk + NSUB, LANES, LANES), f32),
            pltpu.VMEM((1, B_HEADS * LANES), f32),
            pltpu.VMEM((B_TR, B_HEADS * LANES), f32),
        ],
        compiler_params=_cparams(("arbitrary",)),
        name="attn_b",
    )(qa, qi, wit, ckv, ckvt, kk, ckvm, ckvtm, kkm, ckv, ckvt, bias, wuvt)


def kernel(x, meta_tokens, rel_bias, ln_ffn1, ffn1_wi, ffn1_wo, ln_mix, w_out, ln_ffn2, ffn2_wi, ffn2_wo, a_w_in, a_qk_norm, a_lambda, a_subln, b_w_in, b_latent_norm, b_w_uq, b_q_norm, b_w_uv, c_w_in, c_qk_norm, c_sinks):
    bsz, seq, d = x.shape
    assert d == D_MODEL and seq % LANES == 0 and bsz * N_META == LANES
    nblk = seq // LANES
    n = bsz * seq + LANES
    k_sel = min(TOPK_MAX, seq // 4)
    tm_ffn = _row_tile(n, 1408)
    tm_last = _row_tile(bsz * seq, 1408)
    tm_proj = _row_tile(n, 384, LANES)
    fc = 256

    h = x.reshape(bsz * seq, d)
    meta_rows = jnp.broadcast_to(meta_tokens.astype(x.dtype), (bsz, N_META, d)).reshape(LANES, d)
    bias = _bias_tiles(rel_bias)

    for layer in range(DEPTH):
        h = _ffn(h, ln_ffn1[layer], ffn1_wi, ffn1_wo, layer, tm=tm_ffn, fc=fc,
                 tail=meta_rows if layer == 0 else None)
        kind, j = layer % N_MIXERS, layer // N_MIXERS
        g = ln_mix[layer]
        if kind == 0:
            lambda_init = 0.8 - 0.6 * math.exp(-0.3 * layer)
            qs, k, vt = _proj_a(h, g, a_w_in[j].astype(bf16), a_qk_norm[j], tm=tm_proj)
            mix = _attn_a(qs, k, vt, bias, a_lambda[j], a_subln[j], bsz=bsz, nblk=nblk, lambda_init=lambda_init)
        elif kind == 1:
            w = b_w_in[j]
            r2 = B_Q_RANK + B_KV_RANK
            kcol = w[:, r2:r2 + IDX_DIM]
            w1 = jnp.concatenate([w[:, :r2], kcol, kcol, w[:, r2 + IDX_DIM:],
                                  jnp.zeros((d, LANES - IDX_HEADS), w.dtype)], axis=1).astype(bf16)
            assert w1.shape[1] == B_W1
            qa, qi, ckv, ckvt, kk, wit = _proj_b(h, g, w1, b_latent_norm[j], b_w_uq[j].astype(bf16), b_q_norm[j],
                                                 tm=tm_proj)
            wuvt = jnp.swapaxes(b_w_uv[j], 1, 2).astype(bf16)
            mix = _attn_b(qa, qi, wit, ckv, ckvt, kk, bias, wuvt, bsz=bsz, nblk=nblk, k_sel=k_sel)
        else:
            w = c_w_in[j]
            kcols = [w[:, C_QD + gi * C_HD:C_QD + (gi + 1) * C_HD] for gi in range(C_KV_HEADS)]
            voff = C_QD + C_KV_HEADS * C_HD
            vcols = [w[:, voff + gi * C_HD:voff + (gi + 1) * C_HD] for gi in range(C_KV_HEADS)]
            wc = jnp.concatenate([w[:, :C_QD]] + [kc for kc in kcols for _ in range(2)]
                                 + [vc for vc in vcols for _ in range(2)], axis=1).astype(bf16)
            qs, k, vt = _proj_c(h, g, wc, c_qk_norm[j], tm=tm_proj)
            mix = _attn_c(qs, k, vt, bias, c_sinks[j], bsz=bsz, nblk=nblk)
        last = layer == DEPTH - 1
        h = _ffn(h, ln_ffn2[layer], ffn2_wi, ffn2_wo, layer, tm=tm_last if last else tm_ffn, fc=fc,
                 mix=mix, wout=w_out[layer].astype(bf16), n_rows=bsz * seq if last else None)
    return h.reshape(bsz, seq, d)
```
